```python
import jax, jax.numpy as jnp
from jax import lax
import numpy as np


D_MODEL = 2048
BATCH = 16
SEQ = 2048
DEPTH = 2

GRID_W = 64
GROUP_W = D_MODEL // 4
D_MIX = 4 * GROUP_W
CONV_K = 31
NA_HEADS = 8
NA_HEAD_DIM = GROUP_W // NA_HEADS
NA_ROWS = 8
NA_COLS = 16
GLA_HEADS = 4
GLA_DK = GROUP_W // 2 // GLA_HEADS
GLA_DV = GROUP_W // GLA_HEADS
GLA_RANK = 16
GLA_TAU = 16.0
GLA_CHUNK = 64
POOL_WINDOWS = (2, 4, 8, 16)
POOL_GROUPS = 4
POOL_CG = GROUP_W // POOL_GROUPS
EPS = 1e-6

IN_WIDTHS = (
    GROUP_W, GROUP_W, GROUP_W,
    GROUP_W, GROUP_W, GROUP_W, GROUP_W,
    GLA_HEADS * GLA_DK, GLA_HEADS * GLA_DK, GROUP_W, GROUP_W, GLA_RANK, GLA_RANK,
    GROUP_W, GROUP_W,
)
N_IN = sum(IN_WIDTHS)

kernel_name = 'hybrid_parallel_group_encoder_block'


def _rms_norm(x, g):
    xf = x.astype(jnp.float32)
    y = xf * lax.rsqrt(jnp.mean(xf * xf, axis=-1, keepdims=True) + EPS)
    return (y * g.astype(jnp.float32)).astype(x.dtype)


def _layer_norm(x, g, b):
    xf = x.astype(jnp.float32)
    mu = jnp.mean(xf, axis=-1, keepdims=True)
    var = jnp.mean(jnp.square(xf - mu), axis=-1, keepdims=True)
    y = (xf - mu) * lax.rsqrt(var + EPS)
    return (y * g.astype(jnp.float32) + b.astype(jnp.float32)).astype(x.dtype)


def _conformer_conv(u_val, u_glu, conv_w, conv_b, ln_g, ln_b):
    u = u_val * jax.nn.sigmoid(u_glu)
    y = lax.conv_general_dilated(
        u, conv_w[:, None, :].astype(u.dtype), window_strides=(1,),
        padding=[(CONV_K // 2, CONV_K // 2)],
        dimension_numbers=('NWC', 'WIO', 'NWC'),
        feature_group_count=u.shape[-1])
    y = y + conv_b.astype(y.dtype)
    return jax.nn.silu(_layer_norm(y, ln_g, ln_b))


def _neighbourhood_attention(q, k, v, rpb):
    B, S, H, Dh = q.shape
    rows = S // GRID_W
    kr = min(NA_ROWS, rows)
    kc = NA_COLS
    qg = jnp.moveaxis(q.reshape(B, rows, GRID_W, H, Dh), 1, 0)
    kg = k.reshape(B, rows, GRID_W, H, Dh)
    vg = v.reshape(B, rows, GRID_W, H, Dh)
    r_idx = jnp.arange(rows, dtype=jnp.int32)
    row_start = jnp.clip(r_idx - kr // 2, 0, rows - kr)
    c_idx = jnp.arange(GRID_W, dtype=jnp.int32)
    col_start = jnp.clip(c_idx - kc // 2, 0, GRID_W - kc)
    col_idx = col_start[:, None] + jnp.arange(kc, dtype=jnp.int32)[None, :]
    col_off = col_idx - c_idx[:, None] + (kc - 1)
    rpb_cols = rpb[:, :, col_off]
    scale = Dh ** -0.5

    def row_fn(inp):
        q_r, rs, r = inp
        k_rows = lax.dynamic_slice_in_dim(kg, rs, kr, axis=1)
        v_rows = lax.dynamic_slice_in_dim(vg, rs, kr, axis=1)
        k_win = k_rows[:, :, col_idx]
        v_win = v_rows[:, :, col_idx]
        row_off = rs + jnp.arange(kr, dtype=jnp.int32) - r + (NA_ROWS - 1)
        bias = jnp.transpose(rpb_cols[:, row_off], (0, 2, 1, 3))
        s = jnp.einsum('bqhd,brqchd->bhqrc', q_r, k_win).astype(jnp.float32) * scale
        s = s + bias.astype(jnp.float32)[None]
        p = jax.nn.softmax(s.reshape(B, H, GRID_W, kr * kc), axis=-1).reshape(s.shape)
        return jnp.einsum('bhqrc,brqchd->bqhd', p.astype(v_win.dtype), v_win)

    o = lax.map(row_fn, (qg, row_start, r_idx))
    return jnp.moveaxis(o, 0, 1).reshape(B, S, H, Dh)


def _gla_scan(q, k, v, g):
    B, S, H, dk = q.shape
    dv = v.shape[-1]
    C = GLA_CHUNK
    n = S // C

    def chunks(a):
        return jnp.moveaxis(a.reshape(B, n, C, H, a.shape[-1]), 1, 0)

    lower = jnp.tril(jnp.ones((C, C), jnp.float32))

    def step(state, inp):
        qc, kc, vc, gc = inp
        b = jnp.cumsum(gc, axis=1)
        b_last = b[:, -1]
        o_inter = jnp.einsum('bchk,bhkv->bchv', qc * jnp.exp(b), state)
        decay = jnp.exp(jnp.minimum(b[:, :, None] - b[:, None], 0.0)) * lower[None, :, :, None, None]
        attn = jnp.einsum('bihk,bjhk,bijhk->bhij', qc, kc, decay)
        o_intra = jnp.einsum('bhij,bjhv->bihv', attn, vc)
        state = state * jnp.exp(b_last)[..., None] + jnp.einsum(
            'bjhk,bjhv->bhkv', kc * jnp.exp(b_last[:, None] - b), vc)
        return state, o_inter + o_intra

    init = jnp.zeros((B, H, dk, dv), jnp.float32)
    _, o = lax.scan(step, init, (chunks(q), chunks(k), chunks(v), chunks(g)))
    return jnp.moveaxis(o, 0, 1).reshape(B, S, H, dv)


def _gla_branch(c_q, c_k, c_v, lr_f, lr_b, a2_f, ab_f, a2_b, ab_b, o_g):
    B, S, _ = c_q.shape
    f32 = jnp.float32
    q = c_q.astype(f32).reshape(B, S, GLA_HEADS, GLA_DK) * (GLA_DK ** -0.5)
    k = c_k.astype(f32).reshape(B, S, GLA_HEADS, GLA_DK)
    v = c_v.astype(f32).reshape(B, S, GLA_HEADS, GLA_DV)

    def log_decay(lr, a2, ab):
        z = jnp.einsum('bsr,rk->bsk', lr.astype(f32), a2.astype(f32)) + ab.astype(f32)
        return (jax.nn.log_sigmoid(z) / GLA_TAU).reshape(B, S, GLA_HEADS, GLA_DK)

    g_f = log_decay(lr_f, a2_f, ab_f)
    g_b = log_decay(lr_b, a2_b, ab_b)
    flip = lambda a: jnp.flip(a, axis=1)
    o = _gla_scan(q, k, v, g_f) + flip(_gla_scan(flip(q), flip(k), flip(v), flip(g_b)))
    o = o * lax.rsqrt(jnp.mean(o * o, axis=-1, keepdims=True) + EPS) * o_g.astype(f32)
    return o.reshape(B, S, GROUP_W)


def _multiscale_pool(u, w, scale):
    B, S, _ = u.shape
    f32 = jnp.float32
    uf = u.astype(f32).reshape(B, S, POOL_GROUPS, POOL_CG)
    cs = jnp.concatenate([jnp.zeros((B, 1, POOL_GROUPS, POOL_CG), f32), jnp.cumsum(uf, axis=1)], axis=1)
    win = jnp.array(POOL_WINDOWS, jnp.int32)[None, :]
    t = jnp.arange(S, dtype=jnp.int32)[:, None]
    lo = jnp.clip(t - win // 2, 0, S)
    hi = jnp.clip(t + win - win // 2, 0, S)
    g_idx = jnp.arange(POOL_GROUPS, dtype=jnp.int32)[None, :]
    mean = (cs[:, hi, g_idx] - cs[:, lo, g_idx]) / (hi - lo).astype(f32)[None, :, :, None]
    y = jnp.einsum('bsgc,gcd->bsgd', mean - uf, w.astype(f32))
    return y.reshape(B, S, GROUP_W) * scale.astype(f32)


def _hybrid_layer(x, norm_g, w_in, conv_w, conv_b, conv_ln_g, conv_ln_b, na_q_g, na_k_g, na_rpb,
                  gla_a2_f, gla_ab_f, gla_a2_b, gla_ab_b, gla_o_g, pool_w, pool_scale, w_out):
    B, S, _ = x.shape
    h = _rms_norm(x, norm_g)
    z = jnp.einsum('bsd,dn->bsn', h, w_in.astype(h.dtype))
    offsets = np.cumsum(IN_WIDTHS)[:-1].tolist()
    (a_val, a_glu, a_gate, b_q, b_k, b_v, b_gate,
     c_q, c_k, c_v, c_gate, c_lr_f, c_lr_b, d_val, d_gate) = jnp.split(z, offsets, axis=-1)
    dt = h.dtype
    y_a = _conformer_conv(a_val, a_glu, conv_w, conv_b, conv_ln_g, conv_ln_b) * jax.nn.silu(a_gate)
    q = _rms_norm(b_q.reshape(B, S, NA_HEADS, NA_HEAD_DIM), na_q_g)
    k = _rms_norm(b_k.reshape(B, S, NA_HEADS, NA_HEAD_DIM), na_k_g)
    v = b_v.reshape(B, S, NA_HEADS, NA_HEAD_DIM)
    y_b = _neighbourhood_attention(q, k, v, na_rpb).reshape(B, S, GROUP_W) * jax.nn.silu(b_gate)
    y_c = _gla_branch(c_q, c_k, c_v, c_lr_f, c_lr_b, gla_a2_f, gla_ab_f, gla_a2_b, gla_ab_b,
                      gla_o_g).astype(dt) * jax.nn.silu(c_gate)
    y_d = _multiscale_pool(d_val, pool_w, pool_scale).astype(dt) * jax.nn.silu(d_gate)
    y = jnp.concatenate([y_a.astype(dt), y_b.astype(dt), y_c.astype(dt), y_d.astype(dt)], axis=-1)
    return x + jnp.einsum('bsm,md->bsd', y, w_out.astype(dt)).astype(x.dtype)


def _fwd_setup_inputs(seed: int = 0) -> dict:
    key = jax.random.key(seed)
    ks = jax.random.split(key, 18)
    L = DEPTH
    n = lambda k, s: jax.random.normal(k, s, jnp.float32)
    return {
        'x': n(ks[0], (BATCH, SEQ, D_MODEL)),
        'norm_g': 1.0 + 0.02 * n(ks[1], (L, D_MODEL)),
        'w_in': n(ks[2], (L, D_MODEL, N_IN)) * D_MODEL ** -0.5,
        'conv_w': n(ks[3], (L, CONV_K, GROUP_W)) * CONV_K ** -0.5,
        'conv_b': 0.02 * n(ks[4], (L, GROUP_W)),
        'conv_ln_g': 1.0 + 0.02 * n(ks[5], (L, GROUP_W)),
        'conv_ln_b': 0.02 * n(ks[6], (L, GROUP_W)),
        'na_q_g': 1.0 + 0.02 * n(ks[7], (L, NA_HEADS, NA_HEAD_DIM)),
        'na_k_g': 1.0 + 0.02 * n(ks[8], (L, NA_HEADS, NA_HEAD_DIM)),
        'na_rpb': 0.1 * n(ks[9], (L, NA_HEADS, 2 * NA_ROWS - 1, 2 * NA_COLS - 1)),
        'gla_a2_f': n(ks[10], (L, GLA_RANK, GLA_HEADS * GLA_DK)) * GLA_RANK ** -0.5,
        'gla_ab_f': 1.0 + 0.5 * n(ks[11], (L, GLA_HEADS * GLA_DK)),
        'gla_a2_b': n(ks[12], (L, GLA_RANK, GLA_HEADS * GLA_DK)) * GLA_RANK ** -0.5,
        'gla_ab_b': 1.0 + 0.5 * n(ks[13], (L, GLA_HEADS * GLA_DK)),
        'gla_o_g': 1.0 + 0.02 * n(ks[14], (L, GLA_HEADS, GLA_DV)),
        'pool_w': n(ks[15], (L, POOL_GROUPS, POOL_CG, POOL_CG)) * POOL_CG ** -0.5,
        'pool_scale': 1.0 + 0.02 * n(ks[16], (L, GROUP_W)),
        'w_out': n(ks[17], (L, D_MIX, D_MODEL)) * D_MIX ** -0.5,
    }


def _fwd_reference(x, norm_g, w_in, conv_w, conv_b, conv_ln_g, conv_ln_b, na_q_g, na_k_g, na_rpb,
              gla_a2_f, gla_ab_f, gla_a2_b, gla_ab_b, gla_o_g, pool_w, pool_scale, w_out):
    for l in range(DEPTH):
        x = _hybrid_layer(x, norm_g[l], w_in[l], conv_w[l], conv_b[l], conv_ln_g[l], conv_ln_b[l],
                          na_q_g[l], na_k_g[l], na_rpb[l], gla_a2_f[l], gla_ab_f[l], gla_a2_b[l],
                          gla_ab_b[l], gla_o_g[l], pool_w[l], pool_scale[l], w_out[l])
    return x


import jax as _jax
import jax.numpy as _jnp

TWIN_FORMAT = 'train_step'
FWD_PARAMS = ['x', 'norm_g', 'w_in', 'conv_w', 'conv_b', 'conv_ln_g', 'conv_ln_b', 'na_q_g', 'na_k_g', 'na_rpb', 'gla_a2_f', 'gla_ab_f', 'gla_a2_b', 'gla_ab_b', 'gla_o_g', 'pool_w', 'pool_scale', 'w_out']
TWIN_WEIGHTS = ['norm_g', 'w_in', 'conv_w', 'conv_b', 'conv_ln_g', 'conv_ln_b', 'na_q_g', 'na_k_g', 'na_rpb', 'gla_a2_f', 'gla_ab_f', 'gla_a2_b', 'gla_ab_b', 'gla_o_g', 'pool_w', 'pool_scale', 'w_out']
TWIN_DIFF_INPUT = 'x'
TWIN_INPUTS = ['x', 'norm_g', 'w_in', 'conv_w', 'conv_b', 'conv_ln_g', 'conv_ln_b', 'na_q_g', 'na_k_g', 'na_rpb', 'gla_a2_f', 'gla_ab_f', 'gla_a2_b', 'gla_ab_b', 'gla_o_g', 'pool_w', 'pool_scale', 'w_out', 'loss_target', 'm_norm_g', 'm_w_in', 'm_conv_w', 'm_conv_b', 'm_conv_ln_g', 'm_conv_ln_b', 'm_na_q_g', 'm_na_k_g', 'm_na_rpb', 'm_gla_a2_f', 'm_gla_ab_f', 'm_gla_a2_b', 'm_gla_ab_b', 'm_gla_o_g', 'm_pool_w', 'm_pool_scale', 'm_w_out', 'v_norm_g', 'v_w_in', 'v_conv_w', 'v_conv_b', 'v_conv_ln_g', 'v_conv_ln_b', 'v_na_q_g', 'v_na_k_g', 'v_na_rpb', 'v_gla_a2_f', 'v_gla_ab_f', 'v_gla_a2_b', 'v_gla_ab_b', 'v_gla_o_g', 'v_pool_w', 'v_pool_scale', 'v_w_out']
TWIN_OUTPUTS = ['loss', 'grad_x', 'grad_norm_g', 'grad_w_in', 'grad_conv_w', 'grad_conv_b', 'grad_conv_ln_g', 'grad_conv_ln_b', 'grad_na_q_g', 'grad_na_k_g', 'grad_na_rpb', 'grad_gla_a2_f', 'grad_gla_ab_f', 'grad_gla_a2_b', 'grad_gla_ab_b', 'grad_gla_o_g', 'grad_pool_w', 'grad_pool_scale', 'grad_w_out', 'delta_norm_g', 'delta_w_in', 'delta_conv_w', 'delta_conv_b', 'delta_conv_ln_g', 'delta_conv_ln_b', 'delta_na_q_g', 'delta_na_k_g', 'delta_na_rpb', 'delta_gla_a2_f', 'delta_gla_ab_f', 'delta_gla_a2_b', 'delta_gla_ab_b', 'delta_gla_o_g', 'delta_pool_w', 'delta_pool_scale', 'delta_w_out', 'new_m_norm_g', 'new_m_w_in', 'new_m_conv_w', 'new_m_conv_b', 'new_m_conv_ln_g', 'new_m_conv_ln_b', 'new_m_na_q_g', 'new_m_na_k_g', 'new_m_na_rpb', 'new_m_gla_a2_f', 'new_m_gla_ab_f', 'new_m_gla_a2_b', 'new_m_gla_ab_b', 'new_m_gla_o_g', 'new_m_pool_w', 'new_m_pool_scale', 'new_m_w_out', 'new_v_norm_g', 'new_v_w_in', 'new_v_conv_w', 'new_v_conv_b', 'new_v_conv_ln_g', 'new_v_conv_ln_b', 'new_v_na_q_g', 'new_v_na_k_g', 'new_v_na_rpb', 'new_v_gla_a2_f', 'new_v_gla_ab_f', 'new_v_gla_a2_b', 'new_v_gla_ab_b', 'new_v_gla_o_g', 'new_v_pool_w', 'new_v_pool_scale', 'new_v_w_out']
TWIN_LEAF_KINDS = {'loss': 'loss', 'grad_x': 'grad_x', 'grad_norm_g': 'grad_w', 'grad_w_in': 'grad_w', 'grad_conv_w': 'grad_w', 'grad_conv_b': 'grad_w', 'grad_conv_ln_g': 'grad_w', 'grad_conv_ln_b': 'grad_w', 'grad_na_q_g': 'grad_w', 'grad_na_k_g': 'grad_w', 'grad_na_rpb': 'grad_w', 'grad_gla_a2_f': 'grad_w', 'grad_gla_ab_f': 'grad_w', 'grad_gla_a2_b': 'grad_w', 'grad_gla_ab_b': 'grad_w', 'grad_gla_o_g': 'grad_w', 'grad_pool_w': 'grad_w', 'grad_pool_scale': 'grad_w', 'grad_w_out': 'grad_w', 'delta_norm_g': 'delta_w', 'delta_w_in': 'delta_w', 'delta_conv_w': 'delta_w', 'delta_conv_b': 'delta_w', 'delta_conv_ln_g': 'delta_w', 'delta_conv_ln_b': 'delta_w', 'delta_na_q_g': 'delta_w', 'delta_na_k_g': 'delta_w', 'delta_na_rpb': 'delta_w', 'delta_gla_a2_f': 'delta_w', 'delta_gla_ab_f': 'delta_w', 'delta_gla_a2_b': 'delta_w', 'delta_gla_ab_b': 'delta_w', 'delta_gla_o_g': 'delta_w', 'delta_pool_w': 'delta_w', 'delta_pool_scale': 'delta_w', 'delta_w_out': 'delta_w', 'new_m_norm_g': 'new_m', 'new_m_w_in': 'new_m', 'new_m_conv_w': 'new_m', 'new_m_conv_b': 'new_m', 'new_m_conv_ln_g': 'new_m', 'new_m_conv_ln_b': 'new_m', 'new_m_na_q_g': 'new_m', 'new_m_na_k_g': 'new_m', 'new_m_na_rpb': 'new_m', 'new_m_gla_a2_f': 'new_m', 'new_m_gla_ab_f': 'new_m', 'new_m_gla_a2_b': 'new_m', 'new_m_gla_ab_b': 'new_m', 'new_m_gla_o_g': 'new_m', 'new_m_pool_w': 'new_m', 'new_m_pool_scale': 'new_m', 'new_m_w_out': 'new_m', 'new_v_norm_g': 'new_v', 'new_v_w_in': 'new_v', 'new_v_conv_w': 'new_v', 'new_v_conv_b': 'new_v', 'new_v_conv_ln_g': 'new_v', 'new_v_conv_ln_b': 'new_v', 'new_v_na_q_g': 'new_v', 'new_v_na_k_g': 'new_v', 'new_v_na_rpb': 'new_v', 'new_v_gla_a2_f': 'new_v', 'new_v_gla_ab_f': 'new_v', 'new_v_gla_a2_b': 'new_v', 'new_v_gla_ab_b': 'new_v', 'new_v_gla_o_g': 'new_v', 'new_v_pool_w': 'new_v', 'new_v_pool_scale': 'new_v', 'new_v_w_out': 'new_v'}


def _forward(args):
    return _fwd_reference(*[args[k] for k in FWD_PARAMS])


def _output_shape():
    out = _jax.eval_shape(lambda: _forward(_fwd_setup_inputs(0)))
    return out.shape, out.dtype

N_MICROBATCH = 1
ADAM_LR = 0.001
ADAM_B1 = 0.9
ADAM_B2 = 0.999
ADAM_EPS = 1e-08
ADAM_WD = 0.01
ADAM_STEP = 10
PER_EXAMPLE_BATCH_AXIS = {'x': 0, 'loss_target': 0}
SHARED_INPUTS = []
_WEIGHT_DTYPES = {'norm_g': _jnp.float32, 'w_in': _jnp.float32, 'conv_w': _jnp.float32, 'conv_b': _jnp.float32, 'conv_ln_g': _jnp.float32, 'conv_ln_b': _jnp.float32, 'na_q_g': _jnp.float32, 'na_k_g': _jnp.float32, 'na_rpb': _jnp.float32, 'gla_a2_f': _jnp.float32, 'gla_ab_f': _jnp.float32, 'gla_a2_b': _jnp.float32, 'gla_ab_b': _jnp.float32, 'gla_o_g': _jnp.float32, 'pool_w': _jnp.float32, 'pool_scale': _jnp.float32, 'w_out': _jnp.float32}
MOMENT_SCALE = {'norm_g': 4.550002e+00, 'w_in': 1.035196e-01, 'conv_w': 8.387631e-02, 'conv_b': 5.485502e-01, 'conv_ln_g': 2.429229e+00, 'conv_ln_b': 1.541492e+00, 'na_q_g': 3.784207e-02, 'na_k_g': 3.775491e-02, 'na_rpb': 7.029083e-03, 'gla_a2_f': 1.505473e-02, 'gla_ab_f': 7.589801e-02, 'gla_a2_b': 1.597965e-02, 'gla_ab_b': 7.476141e-02, 'gla_o_g': 5.651294e+00, 'pool_w': 2.979089e-01, 'pool_scale': 4.446484e+00, 'w_out': 9.161910e-02}


def _to_microbatches(a, axis):
    t = _jnp.moveaxis(a, axis, 0)
    t = t.reshape((N_MICROBATCH, t.shape[0] // N_MICROBATCH) + t.shape[1:])
    return _jnp.moveaxis(t, 1, axis + 1)


def setup_inputs(seed: int = 0) -> dict:
    inp = _fwd_setup_inputs(seed)
    key = _jax.random.fold_in(_jax.random.key(seed), 7919)
    shape, _ = _output_shape()
    out = dict(inp)
    out["loss_target"] = _jax.random.normal(_jax.random.fold_in(key, 0), shape, _jnp.float32)
    for i, name in enumerate(TWIN_WEIGHTS):
        w = inp[name].astype(_jnp.float32)
        if MOMENT_SCALE is None:
            s = _jnp.sqrt(_jnp.mean(_jnp.square(w)) + 1e-30)
        else:
            s = MOMENT_SCALE[name]
        km, kv = _jax.random.split(_jax.random.fold_in(key, i + 1))
        out[name] = w
        out["m_" + name] = s * _jax.random.normal(km, w.shape, _jnp.float32)
        out["v_" + name] = (s * s) * _jax.random.uniform(kv, w.shape, _jnp.float32, 0.5, 1.5)
    if N_MICROBATCH > 1:
        for name, axis in PER_EXAMPLE_BATCH_AXIS.items():
            out[name] = _to_microbatches(out[name], axis)
    return {'x': out['x'], 'norm_g': out['norm_g'], 'w_in': out['w_in'], 'conv_w': out['conv_w'], 'conv_b': out['conv_b'], 'conv_ln_g': out['conv_ln_g'], 'conv_ln_b': out['conv_ln_b'], 'na_q_g': out['na_q_g'], 'na_k_g': out['na_k_g'], 'na_rpb': out['na_rpb'], 'gla_a2_f': out['gla_a2_f'], 'gla_ab_f': out['gla_ab_f'], 'gla_a2_b': out['gla_a2_b'], 'gla_ab_b': out['gla_ab_b'], 'gla_o_g': out['gla_o_g'], 'pool_w': out['pool_w'], 'pool_scale': out['pool_scale'], 'w_out': out['w_out'], 'loss_target': out['loss_target'], 'm_norm_g': out['m_norm_g'], 'm_w_in': out['m_w_in'], 'm_conv_w': out['m_conv_w'], 'm_conv_b': out['m_conv_b'], 'm_conv_ln_g': out['m_conv_ln_g'], 'm_conv_ln_b': out['m_conv_ln_b'], 'm_na_q_g': out['m_na_q_g'], 'm_na_k_g': out['m_na_k_g'], 'm_na_rpb': out['m_na_rpb'], 'm_gla_a2_f': out['m_gla_a2_f'], 'm_gla_ab_f': out['m_gla_ab_f'], 'm_gla_a2_b': out['m_gla_a2_b'], 'm_gla_ab_b': out['m_gla_ab_b'], 'm_gla_o_g': out['m_gla_o_g'], 'm_pool_w': out['m_pool_w'], 'm_pool_scale': out['m_pool_scale'], 'm_w_out': out['m_w_out'], 'v_norm_g': out['v_norm_g'], 'v_w_in': out['v_w_in'], 'v_conv_w': out['v_conv_w'], 'v_conv_b': out['v_conv_b'], 'v_conv_ln_g': out['v_conv_ln_g'], 'v_conv_ln_b': out['v_conv_ln_b'], 'v_na_q_g': out['v_na_q_g'], 'v_na_k_g': out['v_na_k_g'], 'v_na_rpb': out['v_na_rpb'], 'v_gla_a2_f': out['v_gla_a2_f'], 'v_gla_ab_f': out['v_gla_ab_f'], 'v_gla_a2_b': out['v_gla_a2_b'], 'v_gla_ab_b': out['v_gla_ab_b'], 'v_gla_o_g': out['v_gla_o_g'], 'v_pool_w': out['v_pool_w'], 'v_pool_scale': out['v_pool_scale'], 'v_w_out': out['v_w_out']}


def _loss(weights, diff, rest, loss_target):
    with _jax.named_scope("forward"):
        args = {**rest, TWIN_DIFF_INPUT: diff, **{k: w.astype(_WEIGHT_DTYPES[k]) for k, w in weights.items()}}
        y = _forward(args)
    with _jax.named_scope("loss_head"):
        err = _jnp.square(y.astype(_jnp.float32) - loss_target)
        return 0.5 * _jnp.sum(_jnp.mean(err, axis=-1)) if err.ndim else 0.5 * err


def _adamw(w, g, m, v):
    m = ADAM_B1 * m + (1.0 - ADAM_B1) * g
    v = ADAM_B2 * v + (1.0 - ADAM_B2) * _jnp.square(g)
    m_hat = m / (1.0 - ADAM_B1 ** ADAM_STEP)
    v_hat = v / (1.0 - ADAM_B2 ** ADAM_STEP)
    delta = -ADAM_LR * (m_hat / (_jnp.sqrt(v_hat) + ADAM_EPS) + ADAM_WD * w)
    return delta, m, v


def reference(x, norm_g, w_in, conv_w, conv_b, conv_ln_g, conv_ln_b, na_q_g, na_k_g, na_rpb, gla_a2_f, gla_ab_f, gla_a2_b, gla_ab_b, gla_o_g, pool_w, pool_scale, w_out, loss_target, m_norm_g, m_w_in, m_conv_w, m_conv_b, m_conv_ln_g, m_conv_ln_b, m_na_q_g, m_na_k_g, m_na_rpb, m_gla_a2_f, m_gla_ab_f, m_gla_a2_b, m_gla_ab_b, m_gla_o_g, m_pool_w, m_pool_scale, m_w_out, v_norm_g, v_w_in, v_conv_w, v_conv_b, v_conv_ln_g, v_conv_ln_b, v_na_q_g, v_na_k_g, v_na_rpb, v_gla_a2_f, v_gla_ab_f, v_gla_a2_b, v_gla_ab_b, v_gla_o_g, v_pool_w, v_pool_scale, v_w_out):
    given = dict(x=x, norm_g=norm_g, w_in=w_in, conv_w=conv_w, conv_b=conv_b, conv_ln_g=conv_ln_g, conv_ln_b=conv_ln_b, na_q_g=na_q_g, na_k_g=na_k_g, na_rpb=na_rpb, gla_a2_f=gla_a2_f, gla_ab_f=gla_ab_f, gla_a2_b=gla_a2_b, gla_ab_b=gla_ab_b, gla_o_g=gla_o_g, pool_w=pool_w, pool_scale=pool_scale, w_out=w_out, loss_target=loss_target, m_norm_g=m_norm_g, m_w_in=m_w_in, m_conv_w=m_conv_w, m_conv_b=m_conv_b, m_conv_ln_g=m_conv_ln_g, m_conv_ln_b=m_conv_ln_b, m_na_q_g=m_na_q_g, m_na_k_g=m_na_k_g, m_na_rpb=m_na_rpb, m_gla_a2_f=m_gla_a2_f, m_gla_ab_f=m_gla_ab_f, m_gla_a2_b=m_gla_a2_b, m_gla_ab_b=m_gla_ab_b, m_gla_o_g=m_gla_o_g, m_pool_w=m_pool_w, m_pool_scale=m_pool_scale, m_w_out=m_w_out, v_norm_g=v_norm_g, v_w_in=v_w_in, v_conv_w=v_conv_w, v_conv_b=v_conv_b, v_conv_ln_g=v_conv_ln_g, v_conv_ln_b=v_conv_ln_b, v_na_q_g=v_na_q_g, v_na_k_g=v_na_k_g, v_na_rpb=v_na_rpb, v_gla_a2_f=v_gla_a2_f, v_gla_ab_f=v_gla_ab_f, v_gla_a2_b=v_gla_a2_b, v_gla_ab_b=v_gla_ab_b, v_gla_o_g=v_gla_o_g, v_pool_w=v_pool_w, v_pool_scale=v_pool_scale, v_w_out=v_w_out)
    weights = {n: given[n] for n in TWIN_WEIGHTS}
    shared = {n: given[n] for n in SHARED_INPUTS}
    per_example = {n: given[n] for n in ['x']}
    grad_fn = _jax.value_and_grad(_loss, argnums=(0, 1))

    def one_microbatch(ex, loss_target):
        ex = dict(ex)
        diff = ex.pop(TWIN_DIFF_INPUT)
        return grad_fn(weights, diff, {**shared, **ex}, loss_target)

    if N_MICROBATCH == 1:
        loss, (grad_w, grad_x) = one_microbatch(per_example, given["loss_target"])
    else:
        def body(carry, xs):
            loss_sum, grad_sum = carry
            l_k, (gw_k, gx_k) = one_microbatch(xs[0], xs[1])
            with _jax.named_scope("update"):
                return (loss_sum + l_k, _jax.tree.map(_jnp.add, grad_sum, gw_k)), gx_k

        init = (_jnp.zeros((), _jnp.float32), _jax.tree.map(_jnp.zeros_like, weights))
        (loss, grad_w), grad_x = _jax.lax.scan(body, init, (per_example, given["loss_target"]))
    with _jax.named_scope("update"):
        delta_w, new_m, new_v = {}, {}, {}
        for n in TWIN_WEIGHTS:
            delta_w[n], new_m[n], new_v[n] = _adamw(weights[n], grad_w[n], given["m_" + n], given["v_" + n])
    return (loss, grad_x, *[grad_w[n] for n in TWIN_WEIGHTS], *[delta_w[n] for n in TWIN_WEIGHTS],
            *[new_m[n] for n in TWIN_WEIGHTS], *[new_v[n] for n in TWIN_WEIGHTS])
```

```python
import functools

import numpy as np
import jax
import jax.numpy as jnp
from jax import lax
from jax.experimental import pallas as pl
from jax.experimental.pallas import tpu as pltpu

F32 = jnp.float32
BF16 = jnp.bfloat16
HI = lax.Precision.HIGHEST
MESH = pl.DeviceIdType.MESH

EPS = 1e-6
D_MODEL = 2048
GROUP_W = 512
SEQ = 2048
DEPTH = 2
N_IN = 6176
GRID_W = 64
CONV_K = 31
NA_HEADS = 8
NA_DH = 64
NA_ROWS = 8
NA_COLS = 16
GLA_HEADS = 4
GLA_DK = 64
GLA_DV = 128
GLA_RANK = 16
GLA_TAU = 16.0
CHUNK = 64
POOL_WINDOWS = (2, 4, 8, 16)
ADAM_LR, ADAM_B1, ADAM_B2, ADAM_EPS, ADAM_WD, ADAM_STEP = 0.001, 0.9, 0.999, 1e-08, 0.01, 10

A_VAL, A_GLU, A_GATE = 0, 512, 1024
B_Q, B_K, B_V, B_GATE = 1536, 2048, 2560, 3072
C_Q, C_K, C_V, C_GATE = 3584, 3840, 4096, 4608
D_VAL, D_GATE = 5120, 5632
LR_OFF = 6144
NZ = 6400
NEG = -1e30
VMEM_LIMIT = 56 * 1024 * 1024


def _cp(sem=None):
    return pltpu.CompilerParams(dimension_semantics=sem, vmem_limit_bytes=VMEM_LIMIT)


def _sigmoid(x):
    return 1.0 / (1.0 + jnp.exp(-x))


def _silu(x):
    return x * _sigmoid(x)


def _dsilu(x):
    s = _sigmoid(x)
    return s * (1.0 + x * (1.0 - s))


def _matmul(a, b, *, dims, out_dtype, tm, tn, tk, name, res=None):
    if dims == "nn":
        (M, K), N = a.shape, b.shape[1]
    elif dims == "nt":
        (M, K), N = a.shape, b.shape[0]
    else:
        (K, M), N = a.shape, b.shape[1]
    tm, tn, tk = min(tm, M), min(tn, N), min(tk, K)
    nk = K // tk
    assert M % tm == 0 and N % tn == 0 and K % tk == 0, (M, N, K, tm, tn, tk)
    dn = {"nn": (((1,), (0,)), ((), ())), "nt": (((1,), (1,)), ((), ())), "tn": (((0,), (0,)), ((), ()))}[dims]
    if dims == "tn":
        a_spec = pl.BlockSpec((tk, tm), lambda i, j, k: (k, i))
    else:
        a_spec = pl.BlockSpec((tm, tk), lambda i, j, k: (i, k))
    if dims == "nt":
        b_spec = pl.BlockSpec((tn, tk), lambda i, j, k: (j, k))
    else:
        b_spec = pl.BlockSpec((tk, tn), lambda i, j, k: (k, j))
    o_spec = pl.BlockSpec((tm, tn), lambda i, j, k: (i, j))
    has_res = res is not None

    def body(*refs):
        if has_res:
            a_ref, b_ref, r_ref, o_ref, acc = refs
        else:
            a_ref, b_ref, o_ref, acc = refs
        k = pl.program_id(2)

        @pl.when(k == 0)
        def _():
            acc[...] = jnp.zeros_like(acc)

        acc[...] += lax.dot_general(a_ref[...], b_ref[...], dn, preferred_element_type=F32)

        @pl.when(k == nk - 1)
        def _():
            r = acc[...]
            if has_res:
                r = r + r_ref[...]
            o_ref[...] = r.astype(o_ref.dtype)

    in_specs = [a_spec, b_spec] + ([o_spec] if has_res else [])
    args = (a, b) + ((res,) if has_res else ())
    return pl.pallas_call(
        body, out_shape=jax.ShapeDtypeStruct((M, N), out_dtype), grid=(M // tm, N // tn, nk),
        in_specs=in_specs, out_specs=o_spec, scratch_shapes=[pltpu.VMEM((tm, tn), F32)],
        name=name, compiler_params=_cp(("parallel", "parallel", "arbitrary")))(*args)


def _rmsnorm_fwd(x, g, name):
    T, D = x.shape
    tm = 256

    def body(x_ref, g_ref, h_ref):
        xv = x_ref[...]
        r = lax.rsqrt(jnp.mean(xv * xv, axis=-1, keepdims=True) + EPS)
        h_ref[...] = (xv * r * g_ref[...]).astype(h_ref.dtype)

    return pl.pallas_call(
        body, out_shape=jax.ShapeDtypeStruct((T, D), BF16), grid=(T // tm,),
        in_specs=[pl.BlockSpec((tm, D), lambda i: (i, 0)), pl.BlockSpec((1, D), lambda i: (0, 0))],
        out_specs=pl.BlockSpec((tm, D), lambda i: (i, 0)), name=name, compiler_params=_cp(("parallel",)))(x, g)


def _rmsnorm_bwd(x, g, dh, dres, name):
    T, D = x.shape
    tm = 256

    def body(x_ref, g_ref, dh_ref, dres_ref, dx_ref, dxb_ref, dg_ref):
        xv = x_ref[...]
        r = lax.rsqrt(jnp.mean(xv * xv, axis=-1, keepdims=True) + EPS)
        xh = xv * r
        dh_v = dh_ref[...]
        dxh = dh_v * g_ref[...]
        dx = r * (dxh - xh * jnp.mean(dxh * xh, axis=-1, keepdims=True)) + dres_ref[...]
        dx_ref[...] = dx
        dxb_ref[...] = dx.astype(BF16)

        @pl.when(pl.program_id(0) == 0)
        def _():
            dg_ref[...] = jnp.zeros_like(dg_ref)

        dg_ref[...] += jnp.sum(dh_v * xh, axis=0, keepdims=True)

    row = pl.BlockSpec((tm, D), lambda i: (i, 0))
    vec = pl.BlockSpec((1, D), lambda i: (0, 0))
    return pl.pallas_call(
        body, out_shape=(jax.ShapeDtypeStruct((T, D), F32), jax.ShapeDtypeStruct((T, D), BF16), jax.ShapeDtypeStruct((1, D), F32)),
        grid=(T // tm,), in_specs=[row, vec, row, row], out_specs=(row, row, vec), name=name,
        compiler_params=_cp(("arbitrary",)))(x, g, dh, dres)


def _loss_head(y, target, name):
    T, D = y.shape
    tm = 256

    def body(y_ref, t_ref, d_ref, db_ref, l_ref):
        e = y_ref[...] - t_ref[...]
        d = e * (1.0 / D)
        d_ref[...] = d
        db_ref[...] = d.astype(BF16)

        @pl.when(pl.program_id(0) == 0)
        def _():
            l_ref[...] = jnp.zeros_like(l_ref)

        row = jnp.sum(e * e, axis=-1, keepdims=True) * (0.5 / D)
        l_ref[...] += jnp.sum(row, axis=0, keepdims=True)

    row = pl.BlockSpec((tm, D), lambda i: (i, 0))
    return pl.pallas_call(
        body, out_shape=(jax.ShapeDtypeStruct((T, D), F32), jax.ShapeDtypeStruct((T, D), BF16),
                         jax.ShapeDtypeStruct((1, 1), F32)), grid=(T // tm,),
        in_specs=[row, row], out_specs=(row, row, pl.BlockSpec((1, 1), lambda i: (0, 0))),
        name=name, compiler_params=_cp(("arbitrary",)))(y, target)


_GATE_COLS = (A_GATE // GROUP_W, B_GATE // GROUP_W, C_GATE // GROUP_W, D_GATE // GROUP_W)


def _gate_fwd(pres, z, name):
    T = z.shape[0]
    tm = 256

    def body(pa, pb, pc, pd, ga, gb, gc, gd, y_ref):
        for n, (p, g) in enumerate(((pa, ga), (pb, gb), (pc, gc), (pd, gd))):
            y_ref[:, n * GROUP_W:(n + 1) * GROUP_W] = (p[...] * _silu(g[...])).astype(BF16)

    pre_spec = pl.BlockSpec((tm, GROUP_W), lambda i: (i, 0))
    gate_specs = [pl.BlockSpec((tm, GROUP_W), functools.partial(lambda i, c: (i, c), c=c)) for c in _GATE_COLS]
    return pl.pallas_call(
        body, out_shape=jax.ShapeDtypeStruct((T, 4 * GROUP_W), BF16), grid=(T // tm,),
        in_specs=[pre_spec] * 4 + gate_specs, out_specs=pl.BlockSpec((tm, 4 * GROUP_W), lambda i: (i, 0)),
        name=name, compiler_params=_cp(("parallel",)))(*pres, z, z, z, z)


def _gate_bwd(dy, pres, z, name):
    T = z.shape[0]
    tm = 256

    def body(dy_ref, pa, pb, pc, pd, ga, gb, gc, gd, dpa, dpb, dpc, dpd, dga, dgb, dgc, dgd):
        for n, (p, g, dp, dg) in enumerate(((pa, ga, dpa, dga), (pb, gb, dpb, dgb), (pc, gc, dpc, dgc), (pd, gd, dpd, dgd))):
            d = dy_ref[:, n * GROUP_W:(n + 1) * GROUP_W]
            gv = g[...]
            dp[...] = d * _silu(gv)
            dg[...] = (d * p[...] * _dsilu(gv)).astype(BF16)

    pre_spec = pl.BlockSpec((tm, GROUP_W), lambda i: (i, 0))
    gate_specs = [pl.BlockSpec((tm, GROUP_W), functools.partial(lambda i, c: (i, c), c=c)) for c in _GATE_COLS]
    outs = tuple([jax.ShapeDtypeStruct((T, GROUP_W), F32)] * 4 + [jax.ShapeDtypeStruct((T, GROUP_W), BF16)] * 4)
    return pl.pallas_call(
        body, out_shape=outs, grid=(T // tm,),
        in_specs=[pl.BlockSpec((tm, 4 * GROUP_W), lambda i: (i, 0))] + [pre_spec] * 4 + gate_specs,
        out_specs=tuple([pre_spec] * 8), name=name, compiler_params=_cp(("parallel",)))(dy, *pres, z, z, z, z)


_PAD = 16
_RC = 256


def _conv_fwd(z, conv_w32, conv_b, S, name):
    T = z.shape[0]
    E = T // S
    LW = 128

    def body(val_ref, glu_ref, w_ref, b_ref, y_ref, upad):
        upad[0:_PAD, :] = jnp.zeros((_PAD, LW), F32)
        upad[_PAD + S:_PAD + S + _PAD, :] = jnp.zeros((_PAD, LW), F32)
        upad[_PAD:_PAD + S, :] = val_ref[...] * _sigmoid(glu_ref[...])
        for r in range(S // _RC):
            acc = jnp.broadcast_to(b_ref[...], (_RC, LW))
            for k in range(CONV_K):
                st = r * _RC + k + 1
                acc = acc + upad[st:st + _RC, :] * w_ref[k:k + 1, :]
            y_ref[r * _RC:(r + 1) * _RC, :] = acc

    return pl.pallas_call(
        body, out_shape=jax.ShapeDtypeStruct((T, GROUP_W), F32), grid=(E, GROUP_W // LW),
        in_specs=[pl.BlockSpec((S, LW), lambda e, j: (e, A_VAL // LW + j)),
                  pl.BlockSpec((S, LW), lambda e, j: (e, A_GLU // LW + j)),
                  pl.BlockSpec((32, LW), lambda e, j: (0, j)),
                  pl.BlockSpec((1, LW), lambda e, j: (0, j))],
        out_specs=pl.BlockSpec((S, LW), lambda e, j: (e, j)),
        scratch_shapes=[pltpu.VMEM((S + 2 * _PAD, LW), F32)],
        name=name, compiler_params=_cp(("parallel", "parallel")))(z, z, conv_w32, conv_b)


def _conv_bwd(z, conv_w32, dyc, S, name):
    T = z.shape[0]
    E = T // S
    LW = 128

    def body(val_ref, glu_ref, w_ref, dy_ref, dval_ref, dglu_ref, dw_ref, db_ref, upad, dpad):
        e = pl.program_id(1)
        zeros = jnp.zeros((_PAD, LW), F32)
        upad[0:_PAD, :] = zeros
        upad[_PAD + S:_PAD + S + _PAD, :] = zeros
        dpad[0:_PAD, :] = zeros
        dpad[_PAD + S:_PAD + S + _PAD, :] = zeros
        upad[_PAD:_PAD + S, :] = val_ref[...] * _sigmoid(glu_ref[...])
        dpad[_PAD:_PAD + S, :] = dy_ref[...]

        @pl.when(e == 0)
        def _():
            dw_ref[...] = jnp.zeros_like(dw_ref)
            db_ref[...] = jnp.zeros_like(db_ref)

        db_ref[...] += jnp.sum(dy_ref[...], axis=0, keepdims=True)
        for r in range(S // _RC):
            dyr = dy_ref[r * _RC:(r + 1) * _RC, :]
            du = jnp.zeros((_RC, LW), F32)
            for k in range(CONV_K):
                st = r * _RC + k + 1
                dw_ref[k:k + 1, :] += jnp.sum(dyr * upad[st:st + _RC, :], axis=0, keepdims=True)
                sd = r * _RC + (CONV_K - 1 - k) + 1
                du = du + dpad[sd:sd + _RC, :] * w_ref[k:k + 1, :]
            sl = slice(r * _RC, (r + 1) * _RC)
            val = val_ref[sl, :]
            sg = _sigmoid(glu_ref[sl, :])
            dval_ref[sl, :] = (du * sg).astype(BF16)
            dglu_ref[sl, :] = (du * val * sg * (1.0 - sg)).astype(BF16)

    blk = pl.BlockSpec((S, LW), lambda j, e: (e, j))
    return pl.pallas_call(
        body, out_shape=(jax.ShapeDtypeStruct((T, GROUP_W), BF16), jax.ShapeDtypeStruct((T, GROUP_W), BF16),
                         jax.ShapeDtypeStruct((32, GROUP_W), F32), jax.ShapeDtypeStruct((1, GROUP_W), F32)),
        grid=(GROUP_W // LW, E),
        in_specs=[pl.BlockSpec((S, LW), lambda j, e: (e, A_VAL // LW + j)),
                  pl.BlockSpec((S, LW), lambda j, e: (e, A_GLU // LW + j)),
                  pl.BlockSpec((32, LW), lambda j, e: (0, j)), blk],
        out_specs=(blk, blk, pl.BlockSpec((32, LW), lambda j, e: (0, j)), pl.BlockSpec((1, LW), lambda j, e: (0, j))),
        scratch_shapes=[pltpu.VMEM((S + 2 * _PAD, LW), F32), pltpu.VMEM((S + 2 * _PAD, LW), F32)],
        name=name, compiler_params=_cp(("parallel", "arbitrary")))(z, z, conv_w32, dyc)


def _ln_silu_fwd(yc, g, b, name):
    T, C = yc.shape
    tm = 256

    def body(y_ref, g_ref, b_ref, o_ref):
        y = y_ref[...]
        mu = jnp.mean(y, axis=-1, keepdims=True)
        yc_ = y - mu
        r = lax.rsqrt(jnp.mean(yc_ * yc_, axis=-1, keepdims=True) + EPS)
        o_ref[...] = _silu(yc_ * r * g_ref[...] + b_ref[...])

    row = pl.BlockSpec((tm, C), lambda i: (i, 0))
    vec = pl.BlockSpec((1, C), lambda i: (0, 0))
    return pl.pallas_call(body, out_shape=jax.ShapeDtypeStruct((T, C), F32), grid=(T // tm,),
                          in_specs=[row, vec, vec], out_specs=row, name=name, compiler_params=_cp(("parallel",)))(yc, g, b)


def _ln_silu_bwd(yc, g, b, dpre, name):
    T, C = yc.shape
    tm = 256

    def body(y_ref, g_ref, b_ref, dp_ref, dy_ref, dg_ref, db_ref):
        y = y_ref[...]
        mu = jnp.mean(y, axis=-1, keepdims=True)
        yc_ = y - mu
        r = lax.rsqrt(jnp.mean(yc_ * yc_, axis=-1, keepdims=True) + EPS)
        xh = yc_ * r
        gv = g_ref[...]
        dln = dp_ref[...] * _dsilu(xh * gv + b_ref[...])
        dxh = dln * gv
        dy_ref[...] = r * (dxh - jnp.mean(dxh, axis=-1, keepdims=True) - xh * jnp.mean(dxh * xh, axis=-1, keepdims=True))

        @pl.when(pl.program_id(0) == 0)
        def _():
            dg_ref[...] = jnp.zeros_like(dg_ref)
            db_ref[...] = jnp.zeros_like(db_ref)

        dg_ref[...] += jnp.sum(dln * xh, axis=0, keepdims=True)
        db_ref[...] += jnp.sum(dln, axis=0, keepdims=True)

    row = pl.BlockSpec((tm, C), lambda i: (i, 0))
    vec = pl.BlockSpec((1, C), lambda i: (0, 0))
    return pl.pallas_call(
        body, out_shape=(jax.ShapeDtypeStruct((T, C), F32), jax.ShapeDtypeStruct((1, C), F32), jax.ShapeDtypeStruct((1, C), F32)),
        grid=(T // tm,), in_specs=[row, vec, vec, row], out_specs=(row, vec, vec), name=name,
        compiler_params=_cp(("arbitrary",)))(yc, g, b, dpre)


def _pool_counts(S, w, rows0, n):
    t = (lax.broadcasted_iota(jnp.int32, (n, 1), 0) + rows0)
    lo = jnp.maximum(t - w // 2, 0)
    hi = jnp.minimum(t + w // 2, S)
    return (hi - lo).astype(F32)


def _pool_fwd(z, pool_w, pool_scale, S, name):
    T = z.shape[0]
    E = T // S
    CG = 128

    def body(u_ref, w_ref, s_ref, o_ref, upad, dif):
        zeros = jnp.zeros((_PAD, GROUP_W), F32)
        upad[0:_PAD, :] = zeros
        upad[_PAD + S:_PAD + S + _PAD, :] = zeros
        upad[_PAD:_PAD + S, :] = u_ref[...]
        for gi, w in enumerate(POOL_WINDOWS):
            ls = slice(gi * CG, (gi + 1) * CG)
            for r in range(S // _RC):
                acc = jnp.zeros((_RC, CG), F32)
                for j in range(-(w // 2), w // 2):
                    st = _PAD + r * _RC + j
                    acc = acc + upad[st:st + _RC, ls]
                cnt = _pool_counts(S, w, r * _RC, _RC)
                dif[r * _RC:(r + 1) * _RC, :] = (acc / cnt - u_ref[r * _RC:(r + 1) * _RC, ls]).astype(BF16)
            yp = jnp.dot(dif[...], w_ref[gi], preferred_element_type=F32)
            o_ref[:, ls] = yp * s_ref[:, ls]

    return pl.pallas_call(
        body, out_shape=jax.ShapeDtypeStruct((T, GROUP_W), F32), grid=(E,),
        in_specs=[pl.BlockSpec((S, GROUP_W), lambda e: (e, D_VAL // GROUP_W)),
                  pl.BlockSpec((4, CG, CG), lambda e: (0, 0, 0)),
                  pl.BlockSpec((1, GROUP_W), lambda e: (0, 0))],
        out_specs=pl.BlockSpec((S, GROUP_W), lambda e: (e, 0)),
        scratch_shapes=[pltpu.VMEM((S + 2 * _PAD, GROUP_W), F32), pltpu.VMEM((S, CG), BF16)],
        name=name, compiler_params=_cp(("parallel",)))(z, pool_w, pool_scale)


def _pool_bwd(z, pool_w, pool_scale, dpre, S, name):
    T = z.shape[0]
    E = T // S
    CG = 128

    def body(u_ref, w_ref, s_ref, dp_ref, du_ref, dw_ref, ds_ref, upad, dif, qpad):
        zeros = jnp.zeros((_PAD, GROUP_W), F32)
        upad[0:_PAD, :] = zeros
        upad[_PAD + S:_PAD + S + _PAD, :] = zeros
        upad[_PAD:_PAD + S, :] = u_ref[...]
        zc = jnp.zeros((_PAD, CG), F32)
        qpad[0:_PAD, :] = zc
        qpad[_PAD + S:_PAD + S + _PAD, :] = zc

        @pl.when(pl.program_id(0) == 0)
        def _():
            dw_ref[...] = jnp.zeros_like(dw_ref)
            ds_ref[...] = jnp.zeros_like(ds_ref)

        for gi, w in enumerate(POOL_WINDOWS):
            ls = slice(gi * CG, (gi + 1) * CG)
            for r in range(S // _RC):
                acc = jnp.zeros((_RC, CG), F32)
                for j in range(-(w // 2), w // 2):
                    st = _PAD + r * _RC + j
                    acc = acc + upad[st:st + _RC, ls]
                cnt = _pool_counts(S, w, r * _RC, _RC)
                dif[r * _RC:(r + 1) * _RC, :] = (acc / cnt - u_ref[r * _RC:(r + 1) * _RC, ls]).astype(BF16)
            dp = dp_ref[:, ls]
            yp = jnp.dot(dif[...], w_ref[gi], preferred_element_type=F32)
            ds_ref[:, ls] += jnp.sum(dp * yp, axis=0, keepdims=True)
            dys = (dp * s_ref[:, ls]).astype(BF16)
            dw_ref[gi] += lax.dot_general(dif[...], dys, (((0,), (0,)), ((), ())), preferred_element_type=F32)
            dm = lax.dot_general(dys, w_ref[gi], (((1,), (1,)), ((), ())), preferred_element_type=F32)
            for r in range(S // _RC):
                cnt = _pool_counts(S, w, r * _RC, _RC)
                qpad[_PAD + r * _RC:_PAD + (r + 1) * _RC, :] = dm[r * _RC:(r + 1) * _RC, :] / cnt
            for r in range(S // _RC):
                acc = -dm[r * _RC:(r + 1) * _RC, :]
                for j in range(-(w // 2) + 1, w // 2 + 1):
                    st = _PAD + r * _RC + j
                    acc = acc + qpad[st:st + _RC, :]
                du_ref[r * _RC:(r + 1) * _RC, ls] = acc.astype(BF16)

    return pl.pallas_call(
        body, out_shape=(jax.ShapeDtypeStruct((T, GROUP_W), BF16), jax.ShapeDtypeStruct((4, CG, CG), F32),
                         jax.ShapeDtypeStruct((1, GROUP_W), F32)), grid=(E,),
        in_specs=[pl.BlockSpec((S, GROUP_W), lambda e: (e, D_VAL // GROUP_W)),
                  pl.BlockSpec((4, CG, CG), lambda e: (0, 0, 0)),
                  pl.BlockSpec((1, GROUP_W), lambda e: (0, 0)),
                  pl.BlockSpec((S, GROUP_W), lambda e: (e, 0))],
        out_specs=(pl.BlockSpec((S, GROUP_W), lambda e: (e, 0)), pl.BlockSpec((4, CG, CG), lambda e: (0, 0, 0)),
                   pl.BlockSpec((1, GROUP_W), lambda e: (0, 0))),
        scratch_shapes=[pltpu.VMEM((S + 2 * _PAD, GROUP_W), F32), pltpu.VMEM((S, CG), BF16),
                        pltpu.VMEM((S + 2 * _PAD, CG), F32)],
        name=name, compiler_params=_cp(("arbitrary",)))(z, pool_w, pool_scale, dpre)


def _na_tables():
    d = np.arange(NA_ROWS)[:, None]
    kr = np.arange(NA_ROWS)[None, :]
    ro = kr - d + (NA_ROWS - 1)
    qc = np.arange(GRID_W)[:, None]
    kc = np.arange(GRID_W)[None, :]
    cs = np.clip(qc - NA_COLS // 2, 0, GRID_W - NA_COLS)
    valid = (kc >= cs) & (kc < cs + NA_COLS)
    co = np.clip(kc - qc + (NA_COLS - 1), 0, 2 * NA_COLS - 2)
    return ro, co, valid


def _na_bias(rpb):
    ro, co, valid = _na_tables()
    t = rpb[:, ro]
    t = t[..., co]
    t = jnp.transpose(t, (0, 1, 3, 2, 4))
    t = jnp.where(valid[None, None, :, None, :], t, NEG)
    return t.reshape(NA_HEADS, NA_ROWS, GRID_W, NA_ROWS * GRID_W)


def _seg_mean_matrix(width, seg):
    i = np.arange(width)
    return jnp.asarray((i[:, None] // seg == i[None, :] // seg).astype(np.float32) / seg)


def _na_fwd(z, qg, kg, bias, S, name):
    T = z.shape[0]
    E = T // S
    rows = S // GRID_W
    WIN = NA_ROWS * GRID_W
    seg = _seg_mean_matrix(128, NA_DH)

    def body(q_ref, k_ref, v_ref, qg_ref, kg_ref, bias_ref, seg_ref, o_ref, qs, ks, vs):
        for c in range(S // _RC):
            sl = slice(c * _RC, (c + 1) * _RC)
            q = q_ref[sl, :]
            k = k_ref[sl, :]
            qn = q * lax.rsqrt(jnp.dot(q * q, seg_ref[...], precision=HI, preferred_element_type=F32) + EPS) * qg_ref[...]
            kn = k * lax.rsqrt(jnp.dot(k * k, seg_ref[...], precision=HI, preferred_element_type=F32) + EPS) * kg_ref[...]
            v = v_ref[sl, :]
            for hh in range(2):
                ls = slice(hh * NA_DH, (hh + 1) * NA_DH)
                qs[hh, sl, :] = qn[:, ls].astype(BF16)
                ks[hh, sl, :] = kn[:, ls].astype(BF16)
                vs[hh, sl, :] = v[:, ls].astype(BF16)
        for hh in range(2):
            def row(r, carry, hh=hh):
                rs = jnp.clip(r - NA_ROWS // 2, 0, rows - NA_ROWS)
                q0 = pl.multiple_of(r * GRID_W, GRID_W)
                k0 = pl.multiple_of(rs * GRID_W, GRID_W)
                qr = qs[hh, pl.ds(q0, GRID_W), :]
                kw = ks[hh, pl.ds(k0, WIN), :]
                vw = vs[hh, pl.ds(k0, WIN), :]
                s = lax.dot_general(qr, kw, (((1,), (1,)), ((), ())), preferred_element_type=F32) * (NA_DH ** -0.5)
                s = s + bias_ref[hh, r - rs]
                p = jnp.exp(s - jnp.max(s, axis=-1, keepdims=True))
                l = jnp.sum(p, axis=-1, keepdims=True)
                o = jnp.dot(p.astype(BF16), vw, preferred_element_type=F32) / l
                o_ref[pl.ds(q0, GRID_W), hh * NA_DH:(hh + 1) * NA_DH] = o
                return carry
            lax.fori_loop(0, rows, row, 0)

    LW = 128
    return pl.pallas_call(
        body, out_shape=jax.ShapeDtypeStruct((T, GROUP_W), F32), grid=(E, GROUP_W // LW),
        in_specs=[pl.BlockSpec((S, LW), lambda e, j: (e, B_Q // LW + j)),
                  pl.BlockSpec((S, LW), lambda e, j: (e, B_K // LW + j)),
                  pl.BlockSpec((S, LW), lambda e, j: (e, B_V // LW + j)),
                  pl.BlockSpec((1, LW), lambda e, j: (0, j)),
                  pl.BlockSpec((1, LW), lambda e, j: (0, j)),
                  pl.BlockSpec((2, NA_ROWS, GRID_W, WIN), lambda e, j: (j, 0, 0, 0)),
                  pl.BlockSpec((LW, LW), lambda e, j: (0, 0))],
        out_specs=pl.BlockSpec((S, LW), lambda e, j: (e, j)),
        scratch_shapes=[pltpu.VMEM((2, S, NA_DH), BF16)] * 3,
        name=name, compiler_params=_cp(("parallel", "parallel")))(z, z, z, qg, kg, bias, seg)


def _na_bwd(z, qg, kg, bias, do, S, name):
    T = z.shape[0]
    E = T // S
    rows = S // GRID_W
    WIN = NA_ROWS * GRID_W
    seg = _seg_mean_matrix(128, NA_DH)
    SC = NA_DH ** -0.5

    def body(q_ref, k_ref, v_ref, qg_ref, kg_ref, bias_ref, seg_ref, do_ref,
             dq_ref, dk_ref, dv_ref, dbias_ref, dqg_ref, dkg_ref, qs, ks, vs, dos, dqn, dkn, dvs):
        e = pl.program_id(1)

        @pl.when(e == 0)
        def _():
            dbias_ref[...] = jnp.zeros_like(dbias_ref)
            dqg_ref[...] = jnp.zeros_like(dqg_ref)
            dkg_ref[...] = jnp.zeros_like(dkg_ref)

        for c in range(S // _RC):
            sl = slice(c * _RC, (c + 1) * _RC)
            q = q_ref[sl, :]
            k = k_ref[sl, :]
            qn = q * lax.rsqrt(jnp.dot(q * q, seg_ref[...], precision=HI, preferred_element_type=F32) + EPS) * qg_ref[...]
            kn = k * lax.rsqrt(jnp.dot(k * k, seg_ref[...], precision=HI, preferred_element_type=F32) + EPS) * kg_ref[...]
            v = v_ref[sl, :]
            dd = do_ref[sl, :]
            for hh in range(2):
                ls = slice(hh * NA_DH, (hh + 1) * NA_DH)
                qs[hh, sl, :] = qn[:, ls].astype(BF16)
                ks[hh, sl, :] = kn[:, ls].astype(BF16)
                vs[hh, sl, :] = v[:, ls].astype(BF16)
                dos[hh, sl, :] = dd[:, ls].astype(BF16)
            dkn[sl, :] = jnp.zeros((_RC, 128), F32)
            dvs[sl, :] = jnp.zeros((_RC, 128), F32)

        for hh in range(2):
            ls = slice(hh * NA_DH, (hh + 1) * NA_DH)

            def row(r, carry, hh=hh, ls=ls):
                rs = jnp.clip(r - NA_ROWS // 2, 0, rows - NA_ROWS)
                q0 = pl.multiple_of(r * GRID_W, GRID_W)
                k0 = pl.multiple_of(rs * GRID_W, GRID_W)
                qr = qs[hh, pl.ds(q0, GRID_W), :]
                dor = dos[hh, pl.ds(q0, GRID_W), :]
                kw = ks[hh, pl.ds(k0, WIN), :]
                vw = vs[hh, pl.ds(k0, WIN), :]
                s = lax.dot_general(qr, kw, (((1,), (1,)), ((), ())), preferred_element_type=F32) * SC
                s = s + bias_ref[hh, r - rs]
                p = jnp.exp(s - jnp.max(s, axis=-1, keepdims=True))
                p = p / jnp.sum(p, axis=-1, keepdims=True)
                dp = lax.dot_general(dor, vw, (((1,), (1,)), ((), ())), preferred_element_type=F32)
                ds = p * (dp - jnp.sum(p * dp, axis=-1, keepdims=True))
                dbias_ref[hh, r - rs] += ds
                dsb = ds.astype(BF16)
                dqn[pl.ds(q0, GRID_W), ls] = jnp.dot(dsb, kw, preferred_element_type=F32) * SC
                dkn[pl.ds(k0, WIN), ls] += lax.dot_general(dsb, qr, (((0,), (0,)), ((), ())), preferred_element_type=F32) * SC
                dvs[pl.ds(k0, WIN), ls] += lax.dot_general(p.astype(BF16), dor, (((0,), (0,)), ((), ())), preferred_element_type=F32)
                return carry
            lax.fori_loop(0, rows, row, 0)

        for c in range(S // _RC):
            sl = slice(c * _RC, (c + 1) * _RC)
            for x_ref, g_ref, dn, dx_ref, dg_ref in ((q_ref, qg_ref, dqn, dq_ref, dqg_ref), (k_ref, kg_ref, dkn, dk_ref, dkg_ref)):
                x = x_ref[sl, :]
                r_ = lax.rsqrt(jnp.dot(x * x, seg_ref[...], precision=HI, preferred_element_type=F32) + EPS)
                xh = x * r_
                d = dn[sl, :]
                dxh = d * g_ref[...]
                mean = jnp.dot(dxh * xh, seg_ref[...], precision=HI, preferred_element_type=F32)
                dx_ref[sl, :] = (r_ * (dxh - xh * mean)).astype(BF16)
                dg_ref[...] += jnp.sum(d * xh, axis=0, keepdims=True)
            dv_ref[sl, :] = dvs[sl, :].astype(BF16)

    LW = 128
    blk = pl.BlockSpec((S, LW), lambda j, e: (e, j))
    vec = pl.BlockSpec((1, LW), lambda j, e: (0, j))
    bsp = pl.BlockSpec((2, NA_ROWS, GRID_W, WIN), lambda j, e: (j, 0, 0, 0))
    return pl.pallas_call(
        body, out_shape=(jax.ShapeDtypeStruct((T, GROUP_W), BF16),) * 3 + (
            jax.ShapeDtypeStruct((NA_HEADS, NA_ROWS, GRID_W, WIN), F32),
            jax.ShapeDtypeStruct((1, GROUP_W), F32), jax.ShapeDtypeStruct((1, GROUP_W), F32)),
        grid=(GROUP_W // LW, E),
        in_specs=[pl.BlockSpec((S, LW), lambda j, e: (e, B_Q // LW + j)),
                  pl.BlockSpec((S, LW), lambda j, e: (e, B_K // LW + j)),
                  pl.BlockSpec((S, LW), lambda j, e: (e, B_V // LW + j)),
                  vec, vec, bsp, pl.BlockSpec((LW, LW), lambda j, e: (0, 0)), blk],
        out_specs=(blk, blk, blk, bsp, vec, vec),
        scratch_shapes=[pltpu.VMEM((2, S, NA_DH), BF16)] * 4 + [pltpu.VMEM((S, LW), F32)] * 3,
        name=name, compiler_params=_cp(("parallel", "arbitrary")))(z, z, z, qg, kg, bias, seg, do)


def _na_rpb_grad(dbias, name):
    ro, co, valid = _na_tables()
    H = NA_HEADS
    x = dbias.reshape(H, NA_ROWS, GRID_W, NA_ROWS, GRID_W)
    x = jnp.transpose(x, (0, 1, 3, 2, 4)).reshape(H, NA_ROWS * NA_ROWS, GRID_W * GRID_W)
    e_np = np.zeros((GRID_W, GRID_W, 128), np.float32)
    qi, ki = np.nonzero(valid)
    e_np[qi, ki, co[qi, ki]] = 1.0
    a_np = np.zeros((16, NA_ROWS * NA_ROWS), np.float32)
    a_np[ro.reshape(-1), np.arange(NA_ROWS * NA_ROWS)] = 1.0

    def body(x_ref, e_ref, a_ref, o_ref):
        y = jnp.dot(x_ref[0], e_ref[...], precision=HI, preferred_element_type=F32)
        o_ref[0] = jnp.dot(a_ref[...], y, precision=HI, preferred_element_type=F32)

    out = pl.pallas_call(
        body, out_shape=jax.ShapeDtypeStruct((H, 16, 128), F32), grid=(H,),
        in_specs=[pl.BlockSpec((1, 64, GRID_W * GRID_W), lambda h: (h, 0, 0)),
                  pl.BlockSpec((GRID_W * GRID_W, 128), lambda h: (0, 0)),
                  pl.BlockSpec((16, 64), lambda h: (0, 0))],
        out_specs=pl.BlockSpec((1, 16, 128), lambda h: (h, 0, 0)),
        name=name, compiler_params=_cp(("parallel",)))(x, jnp.asarray(e_np.reshape(GRID_W * GRID_W, 128)), jnp.asarray(a_np))
    return out[:, :2 * NA_ROWS - 1, :2 * NA_COLS - 1]


_HK = GLA_HEADS * GLA_DK
_HV = GLA_HEADS * GLA_DV


def _gla_consts(reverse):
    i = np.arange(CHUNK)
    tri = (i[:, None] <= i[None, :]) if reverse else (i[:, None] >= i[None, :])
    j = np.arange(_HK)
    oseg = (j[:, None] // GLA_DK == j[None, :] // GLA_DK)
    return (jnp.asarray(tri.astype(np.float32)), jnp.asarray(tri.T.astype(np.float32)), jnp.asarray(oseg.astype(np.float32)))


def _log_decay(lr, a2, ab):
    zg = jnp.dot(lr, a2, precision=HI, preferred_element_type=F32) + ab
    g = (jnp.minimum(zg, 0.0) - jnp.log(1.0 + jnp.exp(-jnp.abs(zg)))) * (1.0 / GLA_TAU)
    return zg, g


def _dotf(a, b, dn):
    return lax.dot_general(a, b, dn, precision=HI, preferred_element_type=F32)


_NN = (((1,), (0,)), ((), ()))
_NT = (((1,), (1,)), ((), ()))
_TN = (((0,), (0,)), ((), ()))


def _gla_fwd(z, a2, ab, S, reverse, name):
    T = z.shape[0]
    E = T // S
    n = S // CHUNK
    tri, _, oseg = _gla_consts(reverse)

    def body(q_ref, k_ref, v_ref, lr_ref, a2_ref, ab_ref, tri_ref, oseg_ref, o_ref, a_ref, st_ref, st, b_s, q_s, k_s):
        @pl.when(pl.program_id(1) == 0)
        def _():
            st[...] = jnp.zeros_like(st)

        q = q_ref[...] * (GLA_DK ** -0.5)
        k = k_ref[...]
        v = v_ref[...]
        _, g = _log_decay(lr_ref[...], a2_ref[...], ab_ref[...])
        b = _dotf(tri_ref[...], g, _NN)
        bl_row = jnp.sum(g, axis=0, keepdims=True)
        bl_col = _dotf(g, jnp.ones((CHUNK, GLA_DV), F32), _TN)
        b_s[...] = b
        q_s[...] = q
        k_s[...] = k
        lane = lax.broadcasted_iota(jnp.int32, (1, _HK), 1) % GLA_DK

        def col(j, a):
            bj = b_s[pl.ds(j, 1), :]
            kj = k_s[pl.ds(j, 1), :]
            t = q_s[...] * jnp.exp(jnp.minimum(b_s[...] - bj, 0.0)) * kj
            r = _dotf(t, oseg_ref[...], _NN)
            return jnp.where(lane == j, r, a)

        a = lax.fori_loop(0, CHUNK, col, jnp.zeros((CHUNK, _HK), F32))
        rowi = lax.broadcasted_iota(jnp.int32, (CHUNK, 1), 0)
        keep = (rowi <= lane) if reverse else (rowi >= lane)
        a = jnp.where(keep, a, 0.0)
        a_ref[...] = a
        st_ref[0] = st[...]
        qb = q * jnp.exp(b)
        kd = k * jnp.exp(bl_row - b)
        for h in range(GLA_HEADS):
            ks_ = slice(h * GLA_DK, (h + 1) * GLA_DK)
            vs_ = slice(h * GLA_DV, (h + 1) * GLA_DV)
            s_h = st[ks_, :]
            o_ref[:, vs_] = _dotf(qb[:, ks_], s_h, _NN) + _dotf(a[:, ks_], v[:, vs_], _NN)
            st[ks_, :] = s_h * jnp.exp(bl_col[ks_, :]) + _dotf(kd[:, ks_], v[:, vs_], _TN)

    def rowblk(e, c):
        return e * n + ((n - 1 - c) if reverse else c)

    return pl.pallas_call(
        body, out_shape=(jax.ShapeDtypeStruct((T, _HV), F32), jax.ShapeDtypeStruct((T, _HK), F32),
                         jax.ShapeDtypeStruct((T // CHUNK, _HK, GLA_DV), F32)),
        grid=(E, n),
        in_specs=[pl.BlockSpec((CHUNK, _HK), lambda e, c: (rowblk(e, c), C_Q // _HK)),
                  pl.BlockSpec((CHUNK, _HK), lambda e, c: (rowblk(e, c), C_K // _HK)),
                  pl.BlockSpec((CHUNK, _HV), lambda e, c: (rowblk(e, c), C_V // _HV)),
                  pl.BlockSpec((CHUNK, 128), lambda e, c: (rowblk(e, c), LR_OFF // 128)),
                  pl.BlockSpec((128, _HK), lambda e, c: (0, 0)),
                  pl.BlockSpec((1, _HK), lambda e, c: (0, 0)),
                  pl.BlockSpec((CHUNK, CHUNK), lambda e, c: (0, 0)),
                  pl.BlockSpec((_HK, _HK), lambda e, c: (0, 0))],
        out_specs=(pl.BlockSpec((CHUNK, _HV), lambda e, c: (rowblk(e, c), 0)),
                   pl.BlockSpec((CHUNK, _HK), lambda e, c: (rowblk(e, c), 0)),
                   pl.BlockSpec((1, _HK, GLA_DV), lambda e, c: (rowblk(e, c), 0, 0))),
        scratch_shapes=[pltpu.VMEM((_HK, GLA_DV), F32)] + [pltpu.VMEM((CHUNK, _HK), F32)] * 3,
        name=name, compiler_params=_cp(("parallel", "arbitrary")))(z, z, z, z, a2, ab, tri, oseg)


def _gla_bwd(z, a2, ab, att, states, do, prev, S, reverse, name):
    T = z.shape[0]
    E = T // S
    n = S // CHUNK
    tri, tri_t, oseg = _gla_consts(reverse)
    has_prev = prev is not None
    odt = BF16 if has_prev else F32

    def body(*refs):
        (q_ref, k_ref, v_ref, lr_ref, a2_ref, ab_ref, tri_ref, trit_ref, oseg_ref, att_ref, st_ref, do_ref) = refs[:12]
        refs = refs[12:]
        if has_prev:
            pq_ref, pk_ref, pv_ref, pl_ref = refs[:4]
            refs = refs[4:]
        (dq_ref, dk_ref, dv_ref, dlr_ref, da2_ref, dab_ref, dst, b_s, q_s, k_s, da_s, dqb_s, dkd_s, dk3_s, dbn_s, dsp_s) = refs
        first = (pl.program_id(0) == 0) & (pl.program_id(1) == 0)

        @pl.when(first)
        def _():
            da2_ref[...] = jnp.zeros_like(da2_ref)
            dab_ref[...] = jnp.zeros_like(dab_ref)

        @pl.when(pl.program_id(1) == 0)
        def _():
            dst[...] = jnp.zeros_like(dst)

        q = q_ref[...] * (GLA_DK ** -0.5)
        k = k_ref[...]
        v = v_ref[...]
        lr = lr_ref[...]
        zg, g = _log_decay(lr, a2_ref[...], ab_ref[...])
        b = _dotf(tri_ref[...], g, _NN)
        bl_row = jnp.sum(g, axis=0, keepdims=True)
        bl_col = _dotf(g, jnp.ones((CHUNK, GLA_DV), F32), _TN)
        eb = jnp.exp(b)
        ekd = jnp.exp(bl_row - b)
        qb = q * eb
        kd = k * ekd
        b_s[...] = b
        q_s[...] = q
        k_s[...] = k
        att = att_ref[...]
        s_all = st_ref[0]
        dsn = dst[...]
        e_col = jnp.exp(bl_col)
        do = do_ref[...]
        lane = lax.broadcasted_iota(jnp.int32, (1, _HK), 1) % GLA_DK
        rowi = lax.broadcasted_iota(jnp.int32, (CHUNK, 1), 0)
        keep = (rowi <= lane) if reverse else (rowi >= lane)
        for h in range(GLA_HEADS):
            ks_ = slice(h * GLA_DK, (h + 1) * GLA_DK)
            vs_ = slice(h * GLA_DV, (h + 1) * GLA_DV)
            do_h = do[:, vs_]
            s_h = s_all[ks_, :]
            dsn_h = dsn[ks_, :]
            dqb_s[:, ks_] = _dotf(do_h, s_h, _NT)
            dsp_s[ks_, :] = _dotf(qb[:, ks_], do_h, _TN) + dsn_h * e_col[ks_, :]
            da_s[:, ks_] = _dotf(do_h, v[:, vs_], _NT)
            dv_h = _dotf(att[:, ks_], do_h, _TN) + _dotf(kd[:, ks_], dsn_h, _NN)
            if has_prev:
                dv_h = dv_h + pv_ref[:, vs_]
            dv_ref[:, vs_] = dv_h.astype(odt)
            dkd_s[:, ks_] = _dotf(v[:, vs_], dsn_h, _NT)
        da_s[...] = jnp.where(keep, da_s[...], 0.0)
        dqb = dqb_s[...]
        dkd = dkd_s[...]
        x = dsn * s_all * e_col
        dbl_row = _dotf(jnp.ones((8, GLA_DV), F32), x, _NT)[0:1, :] + jnp.sum(dkd * kd, axis=0, keepdims=True)

        def col(j, carry):
            dq3, db3 = carry
            bj = b_s[pl.ds(j, 1), :]
            kj = k_s[pl.ds(j, 1), :]
            dcol = _dotf(jnp.where(lane == j, da_s[...], 0.0), oseg_ref[...], _NN)
            tm_ = dcol * jnp.exp(jnp.minimum(b_s[...] - bj, 0.0))
            dq3 = dq3 + tm_ * kj
            gq = tm_ * q_s[...]
            dk3_s[pl.ds(j, 1), :] = jnp.sum(gq, axis=0, keepdims=True)
            w = gq * kj
            dbn_s[pl.ds(j, 1), :] = jnp.sum(w, axis=0, keepdims=True)
            return dq3, db3 + w

        zero = jnp.zeros((CHUNK, _HK), F32)
        dq3, db3 = lax.fori_loop(0, CHUNK, col, (zero, zero))
        dq = (dqb * eb + dq3) * (GLA_DK ** -0.5)
        dk = dkd * ekd + dk3_s[...]
        db = dqb * qb - dkd * kd + db3 - dbn_s[...]
        dg = _dotf(trit_ref[...], db, _NN) + dbl_row
        sneg = 1.0 / (1.0 + jnp.exp(zg))
        dzg = dg * sneg * (1.0 / GLA_TAU)
        dlr = _dotf(dzg, a2_ref[...], _NT)
        da2_ref[...] += _dotf(lr, dzg, _TN)
        dab_ref[...] += jnp.sum(dzg, axis=0, keepdims=True)
        if has_prev:
            dq = dq + pq_ref[...]
            dk = dk + pk_ref[...]
            dlr = dlr + pl_ref[...]
        dq_ref[...] = dq.astype(odt)
        dk_ref[...] = dk.astype(odt)
        dlr_ref[...] = dlr.astype(odt)
        dst[...] = dsp_s[...]

    def rowblk(e, c):
        return e * n + (c if reverse else (n - 1 - c))

    hk = pl.BlockSpec((CHUNK, _HK), lambda e, c: (rowblk(e, c), 0))
    hv = pl.BlockSpec((CHUNK, _HV), lambda e, c: (rowblk(e, c), 0))
    l128 = pl.BlockSpec((CHUNK, 128), lambda e, c: (rowblk(e, c), 0))
    in_specs = [pl.BlockSpec((CHUNK, _HK), lambda e, c: (rowblk(e, c), C_Q // _HK)),
                pl.BlockSpec((CHUNK, _HK), lambda e, c: (rowblk(e, c), C_K // _HK)),
                pl.BlockSpec((CHUNK, _HV), lambda e, c: (rowblk(e, c), C_V // _HV)),
                pl.BlockSpec((CHUNK, 128), lambda e, c: (rowblk(e, c), LR_OFF // 128)),
                pl.BlockSpec((128, _HK), lambda e, c: (0, 0)),
                pl.BlockSpec((1, _HK), lambda e, c: (0, 0)),
                pl.BlockSpec((CHUNK, CHUNK), lambda e, c: (0, 0)),
                pl.BlockSpec((CHUNK, CHUNK), lambda e, c: (0, 0)),
                pl.BlockSpec((_HK, _HK), lambda e, c: (0, 0)),
                hk, pl.BlockSpec((1, _HK, GLA_DV), lambda e, c: (rowblk(e, c), 0, 0)), hv]
    args = [z, z, z, z, a2, ab, tri, tri_t, oseg, att, states, do]
    if has_prev:
        in_specs += [hk, hk, hv, l128]
        args += list(prev)
    return pl.pallas_call(
        body, out_shape=(jax.ShapeDtypeStruct((T, _HK), odt), jax.ShapeDtypeStruct((T, _HK), odt),
                         jax.ShapeDtypeStruct((T, _HV), odt), jax.ShapeDtypeStruct((T, 128), odt),
                         jax.ShapeDtypeStruct((128, _HK), F32), jax.ShapeDtypeStruct((1, _HK), F32)),
        grid=(E, n), in_specs=in_specs,
        out_specs=(hk, hk, hv, l128, pl.BlockSpec((128, _HK), lambda e, c: (0, 0)), pl.BlockSpec((1, _HK), lambda e, c: (0, 0))),
        scratch_shapes=[pltpu.VMEM((_HK, GLA_DV), F32)] + [pltpu.VMEM((CHUNK, _HK), F32)] * 8 + [pltpu.VMEM((_HK, GLA_DV), F32)],
        name=name, compiler_params=_cp(("arbitrary", "arbitrary")))(*args)


def _gla_norm_fwd(of, ob, og, name):
    T = of.shape[0]
    tm = 256

    def body(f_ref, b_ref, g_ref, o_ref):
        for h in range(GLA_HEADS):
            vs_ = slice(h * GLA_DV, (h + 1) * GLA_DV)
            o = f_ref[:, vs_] + b_ref[:, vs_]
            o_ref[:, vs_] = o * lax.rsqrt(jnp.mean(o * o, axis=-1, keepdims=True) + EPS) * g_ref[:, vs_]

    row = pl.BlockSpec((tm, _HV), lambda i: (i, 0))
    vec = pl.BlockSpec((1, _HV), lambda i: (0, 0))
    return pl.pallas_call(body, out_shape=jax.ShapeDtypeStruct((T, _HV), F32), grid=(T // tm,),
                          in_specs=[row, row, vec], out_specs=row, name=name, compiler_params=_cp(("parallel",)))(of, ob, og)


def _gla_norm_bwd(of, ob, og, dpre, name):
    T = of.shape[0]
    tm = 256

    def body(f_ref, b_ref, g_ref, dp_ref, do_ref, dg_ref):
        @pl.when(pl.program_id(0) == 0)
        def _():
            dg_ref[...] = jnp.zeros_like(dg_ref)

        for h in range(GLA_HEADS):
            vs_ = slice(h * GLA_DV, (h + 1) * GLA_DV)
            o = f_ref[:, vs_] + b_ref[:, vs_]
            r = lax.rsqrt(jnp.mean(o * o, axis=-1, keepdims=True) + EPS)
            xh = o * r
            dp = dp_ref[:, vs_]
            dxh = dp * g_ref[:, vs_]
            do_ref[:, vs_] = r * (dxh - xh * jnp.mean(dxh * xh, axis=-1, keepdims=True))
            dg_ref[:, vs_] += jnp.sum(dp * xh, axis=0, keepdims=True)

    row = pl.BlockSpec((tm, _HV), lambda i: (i, 0))
    vec = pl.BlockSpec((1, _HV), lambda i: (0, 0))
    return pl.pallas_call(
        body, out_shape=(jax.ShapeDtypeStruct((T, _HV), F32), jax.ShapeDtypeStruct((1, _HV), F32)), grid=(T // tm,),
        in_specs=[row, row, vec, row], out_specs=(row, vec), name=name, compiler_params=_cp(("arbitrary",)))(of, ob, og, dpre)


_ANY = pl.BlockSpec(memory_space=pl.ANY)


def _coords():
    return lax.axis_index("x"), lax.axis_index("y"), lax.axis_index("c")


def _gather_weights(win, wout, small, name):
    def body(win_ref, wout_ref, sm_ref, gwin_ref, gwout_ref, gsm_ref, send_sems, recv_sems, local_sems):
        x, y, c = _coords()
        me = 2 * x + y
        srcs = (win_ref, wout_ref, sm_ref)
        dsts = (gwin_ref, gwout_ref, gsm_ref)
        loc = [pltpu.make_async_copy(s, d.at[me], local_sems.at[i]) for i, (s, d) in enumerate(zip(srcs, dsts))]
        for cp in loc:
            cp.start()
        rem = []
        for j, (px, py) in enumerate(((1 - x, y), (x, 1 - y), (1 - x, 1 - y))):
            for i, (s, d) in enumerate(zip(srcs, dsts)):
                rem.append(pltpu.make_async_remote_copy(
                    src_ref=s, dst_ref=d.at[me], send_sem=send_sems.at[3 * j + i], recv_sem=recv_sems.at[3 * j + i],
                    device_id=(px, py, c), device_id_type=MESH))
        for cp in rem:
            cp.start()
        for cp in rem:
            cp.wait()
        for cp in loc:
            cp.wait()

    return pl.pallas_call(
        body, out_shape=tuple(jax.ShapeDtypeStruct((4,) + a.shape, a.dtype) for a in (win, wout, small)),
        in_specs=[_ANY] * 3, out_specs=(_ANY,) * 3,
        scratch_shapes=[pltpu.SemaphoreType.DMA((9,)), pltpu.SemaphoreType.DMA((9,)), pltpu.SemaphoreType.DMA((3,))],
        name=name)(win, wout, small)


def _grad_exchange(pwin, pwout, small, name):
    def body(pw_ref, po_ref, sm_ref, rw_ref, ro_ref, rs_ref, send_sems, recv_sems, local_sems):
        x, y, c = _coords()
        me = 2 * x + y
        dev = 4 * x + 2 * y + c
        loc = [pltpu.make_async_copy(pw_ref.at[me], rw_ref.at[me], local_sems.at[0]),
               pltpu.make_async_copy(po_ref.at[me], ro_ref.at[me], local_sems.at[1]),
               pltpu.make_async_copy(sm_ref, rs_ref.at[dev], local_sems.at[2])]
        for cp in loc:
            cp.start()
        rem = []
        k = 0
        for px, py in ((1 - x, y), (x, 1 - y), (1 - x, 1 - y)):
            tgt = 2 * px + py
            for s, d in ((pw_ref, rw_ref), (po_ref, ro_ref)):
                rem.append(pltpu.make_async_remote_copy(
                    src_ref=s.at[tgt], dst_ref=d.at[me], send_sem=send_sems.at[k], recv_sem=recv_sems.at[k],
                    device_id=(px, py, c), device_id_type=MESH))
                k += 1
        for px, py, pc in ((x, y, 1 - c), (1 - x, y, c), (1 - x, y, 1 - c), (x, 1 - y, c), (x, 1 - y, 1 - c),
                           (1 - x, 1 - y, c), (1 - x, 1 - y, 1 - c)):
            rem.append(pltpu.make_async_remote_copy(
                src_ref=sm_ref, dst_ref=rs_ref.at[dev], send_sem=send_sems.at[k], recv_sem=recv_sems.at[k],
                device_id=(px, py, pc), device_id_type=MESH))
            k += 1
        for cp in rem:
            cp.start()
        for cp in rem:
            cp.wait()
        for cp in loc:
            cp.wait()

    return pl.pallas_call(
        body, out_shape=(jax.ShapeDtypeStruct(pwin.shape, pwin.dtype), jax.ShapeDtypeStruct(pwout.shape, pwout.dtype),
                         jax.ShapeDtypeStruct((8,) + small.shape, small.dtype)),
        in_specs=[_ANY] * 3, out_specs=(_ANY,) * 3,
        scratch_shapes=[pltpu.SemaphoreType.DMA((13,)), pltpu.SemaphoreType.DMA((13,)), pltpu.SemaphoreType.DMA((3,))],
        name=name)(pwin, pwout, small)


def _sibling_swap(a, b, name):
    def body(a_ref, b_ref, oa_ref, ob_ref, send_sems, recv_sems):
        x, y, c = _coords()
        rem = [pltpu.make_async_remote_copy(src_ref=s, dst_ref=d, send_sem=send_sems.at[i], recv_sem=recv_sems.at[i],
                                            device_id=(x, y, 1 - c), device_id_type=MESH)
               for i, (s, d) in enumerate(((a_ref, oa_ref), (b_ref, ob_ref)))]
        for cp in rem:
            cp.start()
        for cp in rem:
            cp.wait()

    return pl.pallas_call(
        body, out_shape=(jax.ShapeDtypeStruct(a.shape, a.dtype), jax.ShapeDtypeStruct(b.shape, b.dtype)),
        in_specs=[_ANY] * 2, out_specs=(_ANY,) * 2,
        scratch_shapes=[pltpu.SemaphoreType.DMA((2,)), pltpu.SemaphoreType.DMA((2,))], name=name)(a, b)


def _sum_slots(r, name):
    n, R, C = r.shape
    tm = 256

    def body(r_ref, o_ref):
        acc = r_ref[0].astype(F32)
        for i in range(1, n):
            acc = acc + r_ref[i].astype(F32)
        o_ref[...] = acc

    return pl.pallas_call(body, out_shape=jax.ShapeDtypeStruct((R, C), F32), grid=(R // tm,),
                          in_specs=[pl.BlockSpec((n, tm, C), lambda i: (0, i, 0))], out_specs=pl.BlockSpec((tm, C), lambda i: (i, 0)),
                          name=name, compiler_params=_cp(("parallel",)))(r)


def _adamw_math(w, g, m, v):
    m = ADAM_B1 * m + (1.0 - ADAM_B1) * g
    v = ADAM_B2 * v + (1.0 - ADAM_B2) * (g * g)
    m_hat = m / (1.0 - ADAM_B1 ** ADAM_STEP)
    v_hat = v / (1.0 - ADAM_B2 ** ADAM_STEP)
    delta = -ADAM_LR * (m_hat / (jnp.sqrt(v_hat) + ADAM_EPS) + ADAM_WD * w)
    return delta, m, v


def _adamw(w, gs, m, v, tm, name):
    R, C = w.shape

    def body(*refs):
        w_ref = refs[0]
        g_refs = refs[1:1 + len(gs)]
        m_ref, v_ref, g_out, d_out, m_out, v_out = refs[1 + len(gs):]
        g = None
        for gr in g_refs:
            parts = [gr[i] for i in range(gr.shape[0])] if len(gr.shape) == 3 else [gr[...]]
            for p in parts:
                g = p if g is None else g + p
        d, mn, vn = _adamw_math(w_ref[...], g, m_ref[...], v_ref[...])
        g_out[...] = g
        d_out[...] = d
        m_out[...] = mn
        v_out[...] = vn

    blk = pl.BlockSpec((tm, C), lambda i: (i, 0))
    g_specs = [pl.BlockSpec((g.shape[0], tm, C), lambda i: (0, i, 0)) if g.ndim == 3 else blk for g in gs]
    return pl.pallas_call(
        body, out_shape=tuple(jax.ShapeDtypeStruct((R, C), F32) for _ in range(4)), grid=(R // tm,),
        in_specs=[blk] + g_specs + [blk, blk], out_specs=(blk,) * 4, name=name,
        compiler_params=_cp(("parallel",)))(w, *gs, m, v)


WEIGHTS = ("norm_g", "w_in", "conv_w", "conv_b", "conv_ln_g", "conv_ln_b", "na_q_g", "na_k_g", "na_rpb", "gla_a2_f",
           "gla_ab_f", "gla_a2_b", "gla_ab_b", "gla_o_g", "pool_w", "pool_scale", "w_out")
_REPL = ("norm_g", "conv_b", "conv_ln_g", "conv_ln_b", "na_q_g", "na_k_g", "na_rpb", "gla_ab_f", "gla_ab_b", "gla_o_g",
         "pool_w", "pool_scale")
_SHARD_SMALL = ("conv_w", "gla_a2_f", "gla_a2_b")
_PACK_ROWS = 8 * 128


def _pack(arrs):
    flat = jnp.concatenate([a.reshape(-1) for a in arrs])
    n = -(-flat.shape[0] // _PACK_ROWS) * _PACK_ROWS
    return jnp.pad(flat, (0, n - flat.shape[0])).reshape(-1, 128)


def _unpack(p, shapes):
    flat = p.reshape(-1)
    out, o = [], 0
    for s in shapes:
        n = int(np.prod(s))
        out.append(flat[o:o + n].reshape(s))
        o += n
    return out


def _to_layout(w):
    pad = jnp.zeros(w.shape[:-1] + (NZ - N_IN,), w.dtype)
    return jnp.concatenate([w[..., :5120], w[..., 5152:6176], w[..., 5120:5152], pad], axis=-1)


def _from_layout(w):
    return jnp.concatenate([w[..., :5120], w[..., LR_OFF:LR_OFF + 32], w[..., 5120:LR_OFF]], axis=-1)


def _layer_fwd(l, x, P, S):
    n = f"l{l}_"
    h = _rmsnorm_fwd(x, P["norm_g"], n + "rms_fwd")
    z = _matmul(h, P["w_in"], dims="nn", out_dtype=F32, tm=512, tn=1280, tk=D_MODEL, name=n + "mm_z")
    yc = _conv_fwd(z, P["conv_w32"], P["conv_b"], S, n + "conv_fwd")
    pre_a = _ln_silu_fwd(yc, P["conv_ln_g"], P["conv_ln_b"], n + "ln_fwd")
    pre_b = _na_fwd(z, P["na_q_g"], P["na_k_g"], P["na_bias"], S, n + "na_fwd")
    of, af, sf = _gla_fwd(z, P["a2_f"], P["gla_ab_f"], S, False, n + "gla_fwd_f")
    ob, ab, sb = _gla_fwd(z, P["a2_b"], P["gla_ab_b"], S, True, n + "gla_fwd_b")
    pre_c = _gla_norm_fwd(of, ob, P["gla_o_g"], n + "gla_norm_fwd")
    pre_d = _pool_fwd(z, P["pool_w_bf"], P["pool_scale"], S, n + "pool_fwd")
    pres = (pre_a, pre_b, pre_c, pre_d)
    y = _gate_fwd(pres, z, n + "gate_fwd")
    out = _matmul(y, P["w_out"], dims="nn", out_dtype=F32, tm=512, tn=1024, tk=D_MODEL, name=n + "mm_out", res=x)
    return out, dict(x=x, h=h, z=z, yc=yc, pres=pres, of=of, af=af, sf=sf, ob=ob, ab=ab, sb=sb, y=y)


def _layer_bwd(l, dout, dout_bf, sv, P, S):
    n = f"l{l}_"
    z = sv["z"]
    T = z.shape[0]
    dy = _matmul(dout_bf, P["w_out"], dims="nt", out_dtype=F32, tm=512, tn=1024, tk=D_MODEL, name=n + "mm_dy")
    d_w_out = _matmul(sv["y"], dout_bf, dims="tn", out_dtype=BF16, tm=512, tn=1024, tk=1024, name=n + "mm_dwout")
    dpa, dpb, dpc, dpd, dga, dgb, dgc, dgd = _gate_bwd(dy, sv["pres"], z, n + "gate_bwd")
    dyc, d_ln_g, d_ln_b = _ln_silu_bwd(sv["yc"], P["conv_ln_g"], P["conv_ln_b"], dpa, n + "ln_bwd")
    dval, dglu, d_cw, d_cb = _conv_bwd(z, P["conv_w32"], dyc, S, n + "conv_bwd")
    dq, dk, dv, dbias, d_qg, d_kg = _na_bwd(z, P["na_q_g"], P["na_k_g"], P["na_bias"], dpb, S, n + "na_bwd")
    d_rpb = _na_rpb_grad(dbias, n + "na_rpb")
    do, d_og = _gla_norm_bwd(sv["of"], sv["ob"], P["gla_o_g"], dpc, n + "gla_norm_bwd")
    part = _gla_bwd(z, P["a2_f"], P["gla_ab_f"], sv["af"], sv["sf"], do, None, S, False, n + "gla_bwd_f")
    d_a2f, d_abf = part[4], part[5]
    dcq, dck, dcv, dlr, d_a2b, d_abb = _gla_bwd(z, P["a2_b"], P["gla_ab_b"], sv["ab"], sv["sb"], do, part[:4], S, True,
                                                n + "gla_bwd_b")
    dd, d_pw, d_ps = _pool_bwd(z, P["pool_w_bf"], P["pool_scale"], dpd, S, n + "pool_bwd")
    dz = jnp.concatenate([dval, dglu, dga, dq, dk, dv, dgb, dcq, dck, dcv, dgc, dd, dgd, dlr,
                          jnp.zeros((T, NZ - LR_OFF - 128), BF16)], axis=1)
    dh = _matmul(dz, P["w_in"], dims="nt", out_dtype=F32, tm=512, tn=1024, tk=1280, name=n + "mm_dh")
    d_w_in = _matmul(sv["h"], dz, dims="tn", out_dtype=BF16, tm=512, tn=1280, tk=1024, name=n + "mm_dwin")
    dx, dx_bf, d_ng = _rmsnorm_bwd(sv["x"], P["norm_g"], dh, dout, n + "rms_bwd")
    grads = dict(norm_g=d_ng[0], w_in=d_w_in, conv_w=d_cw[:CONV_K], conv_b=d_cb[0], conv_ln_g=d_ln_g[0], conv_ln_b=d_ln_b[0],
                 na_q_g=d_qg.reshape(NA_HEADS, NA_DH), na_k_g=d_kg.reshape(NA_HEADS, NA_DH), na_rpb=d_rpb,
                 gla_a2_f=d_a2f[0:GLA_RANK], gla_ab_f=d_abf[0], gla_a2_b=d_a2b[GLA_RANK:2 * GLA_RANK], gla_ab_b=d_abb[0],
                 gla_o_g=d_og.reshape(GLA_HEADS, GLA_DV), pool_w=d_pw, pool_scale=d_ps[0], w_out=d_w_out)
    return dx, dx_bf, grads


def kernel(x, norm_g, w_in, conv_w, conv_b, conv_ln_g, conv_ln_b, na_q_g, na_k_g, na_rpb, gla_a2_f, gla_ab_f, gla_a2_b, gla_ab_b, gla_o_g, pool_w, pool_scale, w_out, loss_target, m_norm_g, m_w_in, m_conv_w, m_conv_b, m_conv_ln_g, m_conv_ln_b, m_na_q_g, m_na_k_g, m_na_rpb, m_gla_a2_f, m_gla_ab_f, m_gla_a2_b, m_gla_ab_b, m_gla_o_g, m_pool_w, m_pool_scale, m_w_out, v_norm_g, v_w_in, v_conv_w, v_conv_b, v_conv_ln_g, v_conv_ln_b, v_na_q_g, v_na_k_g, v_na_rpb, v_gla_a2_f, v_gla_ab_f, v_gla_a2_b, v_gla_ab_b, v_gla_o_g, v_pool_w, v_pool_scale, v_w_out):
    W = dict(norm_g=norm_g, w_in=w_in, conv_w=conv_w, conv_b=conv_b, conv_ln_g=conv_ln_g, conv_ln_b=conv_ln_b, na_q_g=na_q_g,
             na_k_g=na_k_g, na_rpb=na_rpb, gla_a2_f=gla_a2_f, gla_ab_f=gla_ab_f, gla_a2_b=gla_a2_b, gla_ab_b=gla_ab_b,
             gla_o_g=gla_o_g, pool_w=pool_w, pool_scale=pool_scale, w_out=w_out)
    M = dict(norm_g=m_norm_g, w_in=m_w_in, conv_w=m_conv_w, conv_b=m_conv_b, conv_ln_g=m_conv_ln_g, conv_ln_b=m_conv_ln_b,
             na_q_g=m_na_q_g, na_k_g=m_na_k_g, na_rpb=m_na_rpb, gla_a2_f=m_gla_a2_f, gla_ab_f=m_gla_ab_f, gla_a2_b=m_gla_a2_b,
             gla_ab_b=m_gla_ab_b, gla_o_g=m_gla_o_g, pool_w=m_pool_w, pool_scale=m_pool_scale, w_out=m_w_out)
    V = dict(norm_g=v_norm_g, w_in=v_w_in, conv_w=v_conv_w, conv_b=v_conv_b, conv_ln_g=v_conv_ln_g, conv_ln_b=v_conv_ln_b,
             na_q_g=v_na_q_g, na_k_g=v_na_k_g, na_rpb=v_na_rpb, gla_a2_f=v_gla_a2_f, gla_ab_f=v_gla_ab_f, gla_a2_b=v_gla_a2_b,
             gla_ab_b=v_gla_ab_b, gla_o_g=v_gla_o_g, pool_w=v_pool_w, pool_scale=v_pool_scale, w_out=v_w_out)
    E, S, D = x.shape
    T = E * S
    L = DEPTH
    xi, yi, _ = _coords()
    chip = 2 * xi + yi
    cw_sh, a2_sh = conv_w.shape[-1], gla_a2_f.shape[-1]

    small_sh = jnp.concatenate([
        jnp.pad(conv_w, ((0, 0), (0, 1), (0, 0))),
        jnp.pad(gla_a2_f, ((0, 0), (0, 0), (0, 128 - a2_sh))),
        jnp.pad(gla_a2_b, ((0, 0), (0, 0), (0, 128 - a2_sh)))], axis=1)
    g_win, g_wout, g_small = _gather_weights(w_in.astype(BF16), w_out.astype(BF16), small_sh, "gather_weights")
    w_in_full = _to_layout(jnp.transpose(g_win, (1, 2, 0, 3)).reshape(L, D, N_IN))
    w_out_full = jnp.transpose(g_wout, (1, 0, 2, 3)).reshape(L, D, D)
    conv_w_full = jnp.transpose(g_small[:, :, 0:32, :], (1, 2, 0, 3)).reshape(L, 32, 4 * cw_sh)
    a2f_full = jnp.transpose(g_small[:, :, 32:48, :a2_sh], (1, 2, 0, 3)).reshape(L, GLA_RANK, 4 * a2_sh)
    a2b_full = jnp.transpose(g_small[:, :, 48:64, :a2_sh], (1, 2, 0, 3)).reshape(L, GLA_RANK, 4 * a2_sh)

    params = []
    for l in range(L):
        params.append(dict(
            norm_g=norm_g[l][None], w_in=w_in_full[l], w_out=w_out_full[l], conv_w32=conv_w_full[l], conv_b=conv_b[l][None],
            conv_ln_g=conv_ln_g[l][None], conv_ln_b=conv_ln_b[l][None], na_q_g=na_q_g[l].reshape(1, GROUP_W),
            na_k_g=na_k_g[l].reshape(1, GROUP_W), na_bias=_na_bias(na_rpb[l]),
            a2_f=jnp.zeros((128, _HK), F32).at[0:GLA_RANK].set(a2f_full[l]),
            a2_b=jnp.zeros((128, _HK), F32).at[GLA_RANK:2 * GLA_RANK].set(a2b_full[l]),
            gla_ab_f=gla_ab_f[l][None], gla_ab_b=gla_ab_b[l][None], gla_o_g=gla_o_g[l].reshape(1, GROUP_W),
            pool_w_bf=pool_w[l].astype(BF16), pool_scale=pool_scale[l][None]))

    act = x.reshape(T, D)
    saved = []
    for l in range(L):
        act, sv = _layer_fwd(l, act, params[l], S)
        saved.append(sv)
    dact, dact_bf, loss_loc = _loss_head(act, loss_target.reshape(T, D), "loss_head")
    loss = lax.psum(loss_loc[0, 0], ("x", "y", "c"))
    grads = [None] * L
    for l in reversed(range(L)):
        dact, dact_bf, grads[l] = _layer_bwd(l, dact, dact_bf, saved[l], params[l], S)
    grad_x = dact.reshape(E, S, D)
    G = {k: jnp.stack([grads[l][k] for l in range(L)]) for k in WEIGHTS}

    p_win = jnp.transpose(_from_layout(G["w_in"]).reshape(L, D, 4, N_IN // 4), (2, 0, 1, 3))
    p_wout = jnp.transpose(G["w_out"].reshape(L, 4, D // 4, D), (1, 0, 2, 3))
    small_names = _REPL + _SHARD_SMALL
    small_g = _pack([G[k] for k in small_names])
    r_win, r_wout, r_small = _grad_exchange(p_win, p_wout, small_g, "grad_exchange")
    rows_in, cols_in = L * D, N_IN // 4
    rows_out, cols_out = L * (D // 4), D
    part_in = _sum_slots(r_win.reshape(4, rows_in, cols_in), "sum_w_in")
    part_out = _sum_slots(r_wout.reshape(4, rows_out, cols_out), "sum_w_out")
    sib_in, sib_out = _sibling_swap(part_in, part_out, "sibling_swap")

    res = {}
    res["w_in"] = [a.reshape(L, D, cols_in) for a in _adamw(
        w_in.reshape(rows_in, cols_in), (part_in, sib_in), m_w_in.reshape(rows_in, cols_in), v_w_in.reshape(rows_in, cols_in),
        256, "adamw_w_in")]
    res["w_out"] = [a.reshape(L, D // 4, D) for a in _adamw(
        w_out.reshape(rows_out, cols_out), (part_out, sib_out), m_w_out.reshape(rows_out, cols_out),
        v_w_out.reshape(rows_out, cols_out), 256, "adamw_w_out")]
    zeros_sh = [jnp.zeros(G[k].shape, F32) for k in _SHARD_SMALL]
    pk = lambda dct: _pack([dct[k] for k in _REPL] + zeros_sh)
    small_res = _adamw(pk(W), (r_small,), pk(M), pk(V), small_g.shape[0], "adamw_small")
    shapes = [G[k].shape for k in small_names]
    unp = [_unpack(a, shapes) for a in small_res]
    for i, k in enumerate(_REPL):
        res[k] = [u[i] for u in unp]
    g_sh = []
    for i, k in enumerate(_SHARD_SMALL):
        gfull = unp[0][len(_REPL) + i]
        wdt = W[k].shape[-1]
        g_sh.append(lax.dynamic_slice_in_dim(gfull, chip * wdt, wdt, axis=2))
    g_sh_p = _pack(g_sh)
    sh_res = _adamw(_pack([W[k] for k in _SHARD_SMALL]), (g_sh_p,), _pack([M[k] for k in _SHARD_SMALL]),
                    _pack([V[k] for k in _SHARD_SMALL]), g_sh_p.shape[0], "adamw_shard_small")
    shapes2 = [W[k].shape for k in _SHARD_SMALL]
    unp2 = [_unpack(a, shapes2) for a in sh_res]
    for i, k in enumerate(_SHARD_SMALL):
        res[k] = [u[i] for u in unp2]

    outs = [loss, grad_x]
    for j in range(4):
        outs += [res[k][j] for k in WEIGHTS]
    return tuple(outs)
```

```python
import functools

import numpy as np
import jax
import jax.numpy as jnp
from jax import lax
from jax.experimental import pallas as pl
from jax.experimental.pallas import tpu as pltpu

F32 = jnp.float32
BF16 = jnp.bfloat16
HI = lax.Precision.HIGHEST
MESH = pl.DeviceIdType.MESH

EPS = 1e-6
D_MODEL = 2048
GROUP_W = 512
SEQ = 2048
DEPTH = 2
N_IN = 6176
GRID_W = 64
CONV_K = 31
NA_HEADS = 8
NA_DH = 64
NA_ROWS = 8
NA_COLS = 16
GLA_HEADS = 4
GLA_DK = 64
GLA_DV = 128
GLA_RANK = 16
GLA_TAU = 16.0
CHUNK = 64
POOL_WINDOWS = (2, 4, 8, 16)
ADAM_LR, ADAM_B1, ADAM_B2, ADAM_EPS, ADAM_WD, ADAM_STEP = 0.001, 0.9, 0.999, 1e-08, 0.01, 10

A_VAL, A_GLU, A_GATE = 0, 512, 1024
B_Q, B_K, B_V, B_GATE = 1536, 2048, 2560, 3072
C_Q, C_K, C_V, C_GATE = 3584, 3840, 4096, 4608
D_VAL, D_GATE = 5120, 5632
LR_OFF = 6144
NZ = 6400
NEG = -1e30
VMEM_LIMIT = 56 * 1024 * 1024


def _cp(sem=None):
    return pltpu.CompilerParams(dimension_semantics=sem, vmem_limit_bytes=VMEM_LIMIT)


def _sigmoid(x):
    return 1.0 / (1.0 + jnp.exp(-x))


def _silu(x):
    return x * _sigmoid(x)


def _dsilu(x):
    s = _sigmoid(x)
    return s * (1.0 + x * (1.0 - s))


def _matmul(a, b, *, dims, out_dtype, tm, tn, tk, name, res=None):
    if dims == "nn":
        (M, K), N = a.shape, b.shape[1]
    elif dims == "nt":
        (M, K), N = a.shape, b.shape[0]
    else:
        (K, M), N = a.shape, b.shape[1]
    tm, tn, tk = min(tm, M), min(tn, N), min(tk, K)
    nk = K // tk
    assert M % tm == 0 and N % tn == 0 and K % tk == 0, (M, N, K, tm, tn, tk)
    dn = {"nn": (((1,), (0,)), ((), ())), "nt": (((1,), (1,)), ((), ())), "tn": (((0,), (0,)), ((), ()))}[dims]
    if dims == "tn":
        a_spec = pl.BlockSpec((tk, tm), lambda i, j, k: (k, i))
    else:
        a_spec = pl.BlockSpec((tm, tk), lambda i, j, k: (i, k))
    if dims == "nt":
        b_spec = pl.BlockSpec((tn, tk), lambda i, j, k: (j, k))
    else:
        b_spec = pl.BlockSpec((tk, tn), lambda i, j, k: (k, j))
    o_spec = pl.BlockSpec((tm, tn), lambda i, j, k: (i, j))
    has_res = res is not None

    def body(*refs):
        if has_res:
            a_ref, b_ref, r_ref, o_ref, acc = refs
        else:
            a_ref, b_ref, o_ref, acc = refs
        k = pl.program_id(2)

        @pl.when(k == 0)
        def _():
            acc[...] = jnp.zeros_like(acc)

        acc[...] += lax.dot_general(a_ref[...], b_ref[...], dn, preferred_element_type=F32)

        @pl.when(k == nk - 1)
        def _():
            r = acc[...]
            if has_res:
                r = r + r_ref[...]
            o_ref[...] = r.astype(o_ref.dtype)

    in_specs = [a_spec, b_spec] + ([o_spec] if has_res else [])
    args = (a, b) + ((res,) if has_res else ())
    return pl.pallas_call(
        body, out_shape=jax.ShapeDtypeStruct((M, N), out_dtype), grid=(M // tm, N // tn, nk),
        in_specs=in_specs, out_specs=o_spec, scratch_shapes=[pltpu.VMEM((tm, tn), F32)],
        name=name, compiler_params=_cp(("parallel", "parallel", "arbitrary")))(*args)


def _rmsnorm_fwd(x, g, name):
    T, D = x.shape
    tm = 256

    def body(x_ref, g_ref, h_ref):
        xv = x_ref[...]
        r = lax.rsqrt(jnp.mean(xv * xv, axis=-1, keepdims=True) + EPS)
        h_ref[...] = (xv * r * g_ref[...]).astype(h_ref.dtype)

    return pl.pallas_call(
        body, out_shape=jax.ShapeDtypeStruct((T, D), BF16), grid=(T // tm,),
        in_specs=[pl.BlockSpec((tm, D), lambda i: (i, 0)), pl.BlockSpec((1, D), lambda i: (0, 0))],
        out_specs=pl.BlockSpec((tm, D), lambda i: (i, 0)), name=name, compiler_params=_cp(("parallel",)))(x, g)


def _rmsnorm_bwd(x, g, dh, dres, name):
    T, D = x.shape
    tm = 256

    def body(x_ref, g_ref, dh_ref, dres_ref, dx_ref, dxb_ref, dg_ref):
        xv = x_ref[...]
        r = lax.rsqrt(jnp.mean(xv * xv, axis=-1, keepdims=True) + EPS)
        xh = xv * r
        dh_v = dh_ref[...]
        dxh = dh_v * g_ref[...]
        dx = r * (dxh - xh * jnp.mean(dxh * xh, axis=-1, keepdims=True)) + dres_ref[...]
        dx_ref[...] = dx
        dxb_ref[...] = dx.astype(BF16)

        @pl.when(pl.program_id(0) == 0)
        def _():
            dg_ref[...] = jnp.zeros_like(dg_ref)

        dg_ref[...] += jnp.sum(dh_v * xh, axis=0, keepdims=True)

    row = pl.BlockSpec((tm, D), lambda i: (i, 0))
    vec = pl.BlockSpec((1, D), lambda i: (0, 0))
    return pl.pallas_call(
        body, out_shape=(jax.ShapeDtypeStruct((T, D), F32), jax.ShapeDtypeStruct((T, D), BF16), jax.ShapeDtypeStruct((1, D), F32)),
        grid=(T // tm,), in_specs=[row, vec, row, row], out_specs=(row, row, vec), name=name,
        compiler_params=_cp(("arbitrary",)))(x, g, dh, dres)


def _loss_head(y, target, name):
    T, D = y.shape
    tm = 256

    def body(y_ref, t_ref, d_ref, db_ref, l_ref):
        e = y_ref[...] - t_ref[...]
        d = e * (1.0 / D)
        d_ref[...] = d
        db_ref[...] = d.astype(BF16)

        @pl.when(pl.program_id(0) == 0)
        def _():
            l_ref[...] = jnp.zeros_like(l_ref)

        row = jnp.sum(e * e, axis=-1, keepdims=True) * (0.5 / D)
        l_ref[...] += jnp.sum(row, axis=0, keepdims=True)

    row = pl.BlockSpec((tm, D), lambda i: (i, 0))
    return pl.pallas_call(
        body, out_shape=(jax.ShapeDtypeStruct((T, D), F32), jax.ShapeDtypeStruct((T, D), BF16),
                         jax.ShapeDtypeStruct((1, 1), F32)), grid=(T // tm,),
        in_specs=[row, row], out_specs=(row, row, pl.BlockSpec((1, 1), lambda i: (0, 0))),
        name=name, compiler_params=_cp(("arbitrary",)))(y, target)


_GATE_COLS = (A_GATE // GROUP_W, B_GATE // GROUP_W, C_GATE // GROUP_W, D_GATE // GROUP_W)


def _gate_fwd(pres, z, name):
    T = z.shape[0]
    tm = 256

    def body(pa, pb, pc, pd, ga, gb, gc, gd, y_ref):
        for n, (p, g) in enumerate(((pa, ga), (pb, gb), (pc, gc), (pd, gd))):
            y_ref[:, n * GROUP_W:(n + 1) * GROUP_W] = (p[...] * _silu(g[...])).astype(BF16)

    pre_spec = pl.BlockSpec((tm, GROUP_W), lambda i: (i, 0))
    gate_specs = [pl.BlockSpec((tm, GROUP_W), functools.partial(lambda i, c: (i, c), c=c)) for c in _GATE_COLS]
    return pl.pallas_call(
        body, out_shape=jax.ShapeDtypeStruct((T, 4 * GROUP_W), BF16), grid=(T // tm,),
        in_specs=[pre_spec] * 4 + gate_specs, out_specs=pl.BlockSpec((tm, 4 * GROUP_W), lambda i: (i, 0)),
        name=name, compiler_params=_cp(("parallel",)))(*pres, z, z, z, z)


def _gate_bwd(dy, pres, z, name):
    T = z.shape[0]
    tm = 256

    def body(dy_ref, pa, pb, pc, pd, ga, gb, gc, gd, dpa, dpb, dpc, dpd, dga, dgb, dgc, dgd):
        for n, (p, g, dp, dg) in enumerate(((pa, ga, dpa, dga), (pb, gb, dpb, dgb), (pc, gc, dpc, dgc), (pd, gd, dpd, dgd))):
            d = dy_ref[:, n * GROUP_W:(n + 1) * GROUP_W]
            gv = g[...]
            dp[...] = d * _silu(gv)
            dg[...] = (d * p[...] * _dsilu(gv)).astype(BF16)

    pre_spec = pl.BlockSpec((tm, GROUP_W), lambda i: (i, 0))
    gate_specs = [pl.BlockSpec((tm, GROUP_W), functools.partial(lambda i, c: (i, c), c=c)) for c in _GATE_COLS]
    outs = tuple([jax.ShapeDtypeStruct((T, GROUP_W), F32)] * 4 + [jax.ShapeDtypeStruct((T, GROUP_W), BF16)] * 4)
    return pl.pallas_call(
        body, out_shape=outs, grid=(T // tm,),
        in_specs=[pl.BlockSpec((tm, 4 * GROUP_W), lambda i: (i, 0))] + [pre_spec] * 4 + gate_specs,
        out_specs=tuple([pre_spec] * 8), name=name, compiler_params=_cp(("parallel",)))(dy, *pres, z, z, z, z)


_PAD = 16
_RC = 256


def _conv_fwd(z, conv_w32, conv_b, S, name):
    T = z.shape[0]
    E = T // S
    LW = 128

    def body(val_ref, glu_ref, w_ref, b_ref, y_ref, upad):
        upad[0:_PAD, :] = jnp.zeros((_PAD, LW), F32)
        upad[_PAD + S:_PAD + S + _PAD, :] = jnp.zeros((_PAD, LW), F32)
        upad[_PAD:_PAD + S, :] = val_ref[...] * _sigmoid(glu_ref[...])
        for r in range(S // _RC):
            acc = jnp.broadcast_to(b_ref[...], (_RC, LW))
            for k in range(CONV_K):
                st = r * _RC + k + 1
                acc = acc + upad[st:st + _RC, :] * w_ref[k:k + 1, :]
            y_ref[r * _RC:(r + 1) * _RC, :] = acc

    return pl.pallas_call(
        body, out_shape=jax.ShapeDtypeStruct((T, GROUP_W), F32), grid=(E, GROUP_W // LW),
        in_specs=[pl.BlockSpec((S, LW), lambda e, j: (e, A_VAL // LW + j)),
                  pl.BlockSpec((S, LW), lambda e, j: (e, A_GLU // LW + j)),
                  pl.BlockSpec((32, LW), lambda e, j: (0, j)),
                  pl.BlockSpec((1, LW), lambda e, j: (0, j))],
        out_specs=pl.BlockSpec((S, LW), lambda e, j: (e, j)),
        scratch_shapes=[pltpu.VMEM((S + 2 * _PAD, LW), F32)],
        name=name, compiler_params=_cp(("parallel", "parallel")))(z, z, conv_w32, conv_b)


def _conv_bwd(z, conv_w32, dyc, S, name):
    T = z.shape[0]
    E = T // S
    LW = 128

    def body(val_ref, glu_ref, w_ref, dy_ref, dval_ref, dglu_ref, dw_ref, db_ref, upad, dpad):
        e = pl.program_id(1)
        zeros = jnp.zeros((_PAD, LW), F32)
        upad[0:_PAD, :] = zeros
        upad[_PAD + S:_PAD + S + _PAD, :] = zeros
        dpad[0:_PAD, :] = zeros
        dpad[_PAD + S:_PAD + S + _PAD, :] = zeros
        upad[_PAD:_PAD + S, :] = val_ref[...] * _sigmoid(glu_ref[...])
        dpad[_PAD:_PAD + S, :] = dy_ref[...]

        @pl.when(e == 0)
        def _():
            dw_ref[...] = jnp.zeros_like(dw_ref)
            db_ref[...] = jnp.zeros_like(db_ref)

        db_ref[...] += jnp.sum(dy_ref[...], axis=0, keepdims=True)
        for r in range(S // _RC):
            dyr = dy_ref[r * _RC:(r + 1) * _RC, :]
            du = jnp.zeros((_RC, LW), F32)
            for k in range(CONV_K):
                st = r * _RC + k + 1
                dw_ref[k:k + 1, :] += jnp.sum(dyr * upad[st:st + _RC, :], axis=0, keepdims=True)
                sd = r * _RC + (CONV_K - 1 - k) + 1
                du = du + dpad[sd:sd + _RC, :] * w_ref[k:k + 1, :]
            sl = slice(r * _RC, (r + 1) * _RC)
            val = val_ref[sl, :]
            sg = _sigmoid(glu_ref[sl, :])
            dval_ref[sl, :] = (du * sg).astype(BF16)
            dglu_ref[sl, :] = (du * val * sg * (1.0 - sg)).astype(BF16)

    blk = pl.BlockSpec((S, LW), lambda j, e: (e, j))
    return pl.pallas_call(
        body, out_shape=(jax.ShapeDtypeStruct((T, GROUP_W), BF16), jax.ShapeDtypeStruct((T, GROUP_W), BF16),
                         jax.ShapeDtypeStruct((32, GROUP_W), F32), jax.ShapeDtypeStruct((1, GROUP_W), F32)),
        grid=(GROUP_W // LW, E),
        in_specs=[pl.BlockSpec((S, LW), lambda j, e: (e, A_VAL // LW + j)),
                  pl.BlockSpec((S, LW), lambda j, e: (e, A_GLU // LW + j)),
                  pl.BlockSpec((32, LW), lambda j, e: (0, j)), blk],
        out_specs=(blk, blk, pl.BlockSpec((32, LW), lambda j, e: (0, j)), pl.BlockSpec((1, LW), lambda j, e: (0, j))),
        scratch_shapes=[pltpu.VMEM((S + 2 * _PAD, LW), F32), pltpu.VMEM((S + 2 * _PAD, LW), F32)],
        name=name, compiler_params=_cp(("parallel", "arbitrary")))(z, z, conv_w32, dyc)


def _ln_silu_fwd(yc, g, b, name):
    T, C = yc.shape
    tm = 256

    def body(y_ref, g_ref, b_ref, o_ref):
        y = y_ref[...]
        mu = jnp.mean(y, axis=-1, keepdims=True)
        yc_ = y - mu
        r = lax.rsqrt(jnp.mean(yc_ * yc_, axis=-1, keepdims=True) + EPS)
        o_ref[...] = _silu(yc_ * r * g_ref[...] + b_ref[...])

    row = pl.BlockSpec((tm, C), lambda i: (i, 0))
    vec = pl.BlockSpec((1, C), lambda i: (0, 0))
    return pl.pallas_call(body, out_shape=jax.ShapeDtypeStruct((T, C), F32), grid=(T // tm,),
                          in_specs=[row, vec, vec], out_specs=row, name=name, compiler_params=_cp(("parallel",)))(yc, g, b)


def _ln_silu_bwd(yc, g, b, dpre, name):
    T, C = yc.shape
    tm = 256

    def body(y_ref, g_ref, b_ref, dp_ref, dy_ref, dg_ref, db_ref):
        y = y_ref[...]
        mu = jnp.mean(y, axis=-1, keepdims=True)
        yc_ = y - mu
        r = lax.rsqrt(jnp.mean(yc_ * yc_, axis=-1, keepdims=True) + EPS)
        xh = yc_ * r
        gv = g_ref[...]
        dln = dp_ref[...] * _dsilu(xh * gv + b_ref[...])
        dxh = dln * gv
        dy_ref[...] = r * (dxh - jnp.mean(dxh, axis=-1, keepdims=True) - xh * jnp.mean(dxh * xh, axis=-1, keepdims=True))

        @pl.when(pl.program_id(0) == 0)
        def _():
            dg_ref[...] = jnp.zeros_like(dg_ref)
            db_ref[...] = jnp.zeros_like(db_ref)

        dg_ref[...] += jnp.sum(dln * xh, axis=0, keepdims=True)
        db_ref[...] += jnp.sum(dln, axis=0, keepdims=True)

    row = pl.BlockSpec((tm, C), lambda i: (i, 0))
    vec = pl.BlockSpec((1, C), lambda i: (0, 0))
    return pl.pallas_call(
        body, out_shape=(jax.ShapeDtypeStruct((T, C), F32), jax.ShapeDtypeStruct((1, C), F32), jax.ShapeDtypeStruct((1, C), F32)),
        grid=(T // tm,), in_specs=[row, vec, vec, row], out_specs=(row, vec, vec), name=name,
        compiler_params=_cp(("arbitrary",)))(yc, g, b, dpre)


def _pool_counts(S, w, rows0, n):
    t = (lax.broadcasted_iota(jnp.int32, (n, 1), 0) + rows0)
    lo = jnp.maximum(t - w // 2, 0)
    hi = jnp.minimum(t + w // 2, S)
    return (hi - lo).astype(F32)


def _pool_fwd(z, pool_w, pool_scale, S, name):
    T = z.shape[0]
    E = T // S
    CG = 128

    def body(u_ref, w_ref, s_ref, o_ref, upad, dif):
        zeros = jnp.zeros((_PAD, GROUP_W), F32)
        upad[0:_PAD, :] = zeros
        upad[_PAD + S:_PAD + S + _PAD, :] = zeros
        upad[_PAD:_PAD + S, :] = u_ref[...]
        for gi, w in enumerate(POOL_WINDOWS):
            ls = slice(gi * CG, (gi + 1) * CG)
            for r in range(S // _RC):
                acc = jnp.zeros((_RC, CG), F32)
                for j in range(-(w // 2), w // 2):
                    st = _PAD + r * _RC + j
                    acc = acc + upad[st:st + _RC, ls]
                cnt = _pool_counts(S, w, r * _RC, _RC)
                dif[r * _RC:(r + 1) * _RC, :] = (acc / cnt - u_ref[r * _RC:(r + 1) * _RC, ls]).astype(BF16)
            yp = jnp.dot(dif[...], w_ref[gi], preferred_element_type=F32)
            o_ref[:, ls] = yp * s_ref[:, ls]

    return pl.pallas_call(
        body, out_shape=jax.ShapeDtypeStruct((T, GROUP_W), F32), grid=(E,),
        in_specs=[pl.BlockSpec((S, GROUP_W), lambda e: (e, D_VAL // GROUP_W)),
                  pl.BlockSpec((4, CG, CG), lambda e: (0, 0, 0)),
                  pl.BlockSpec((1, GROUP_W), lambda e: (0, 0))],
        out_specs=pl.BlockSpec((S, GROUP_W), lambda e: (e, 0)),
        scratch_shapes=[pltpu.VMEM((S + 2 * _PAD, GROUP_W), F32), pltpu.VMEM((S, CG), BF16)],
        name=name, compiler_params=_cp(("parallel",)))(z, pool_w, pool_scale)


def _pool_bwd(z, pool_w, pool_scale, dpre, S, name):
    T = z.shape[0]
    E = T // S
    CG = 128

    def body(u_ref, w_ref, s_ref, dp_ref, du_ref, dw_ref, ds_ref, upad, dif, qpad):
        zeros = jnp.zeros((_PAD, GROUP_W), F32)
        upad[0:_PAD, :] = zeros
        upad[_PAD + S:_PAD + S + _PAD, :] = zeros
        upad[_PAD:_PAD + S, :] = u_ref[...]
        zc = jnp.zeros((_PAD, CG), F32)
        qpad[0:_PAD, :] = zc
        qpad[_PAD + S:_PAD + S + _PAD, :] = zc

        @pl.when(pl.program_id(0) == 0)
        def _():
            dw_ref[...] = jnp.zeros_like(dw_ref)
            ds_ref[...] = jnp.zeros_like(ds_ref)

        for gi, w in enumerate(POOL_WINDOWS):
            ls = slice(gi * CG, (gi + 1) * CG)
            for r in range(S // _RC):
                acc = jnp.zeros((_RC, CG), F32)
                for j in range(-(w // 2), w // 2):
                    st = _PAD + r * _RC + j
                    acc = acc + upad[st:st + _RC, ls]
                cnt = _pool_counts(S, w, r * _RC, _RC)
                dif[r * _RC:(r + 1) * _RC, :] = (acc / cnt - u_ref[r * _RC:(r + 1) * _RC, ls]).astype(BF16)
            dp = dp_ref[:, ls]
            yp = jnp.dot(dif[...], w_ref[gi], preferred_element_type=F32)
            ds_ref[:, ls] += jnp.sum(dp * yp, axis=0, keepdims=True)
            dys = (dp * s_ref[:, ls]).astype(BF16)
            dw_ref[gi] += lax.dot_general(dif[...], dys, (((0,), (0,)), ((), ())), preferred_element_type=F32)
            dm = lax.dot_general(dys, w_ref[gi], (((1,), (1,)), ((), ())), preferred_element_type=F32)
            for r in range(S // _RC):
                cnt = _pool_counts(S, w, r * _RC, _RC)
                qpad[_PAD + r * _RC:_PAD + (r + 1) * _RC, :] = dm[r * _RC:(r + 1) * _RC, :] / cnt
            for r in range(S // _RC):
                acc = -dm[r * _RC:(r + 1) * _RC, :]
                for j in range(-(w // 2) + 1, w // 2 + 1):
                    st = _PAD + r * _RC + j
                    acc = acc + qpad[st:st + _RC, :]
                du_ref[r * _RC:(r + 1) * _RC, ls] = acc.astype(BF16)

    return pl.pallas_call(
        body, out_shape=(jax.ShapeDtypeStruct((T, GROUP_W), BF16), jax.ShapeDtypeStruct((4, CG, CG), F32),
                         jax.ShapeDtypeStruct((1, GROUP_W), F32)), grid=(E,),
        in_specs=[pl.BlockSpec((S, GROUP_W), lambda e: (e, D_VAL // GROUP_W)),
                  pl.BlockSpec((4, CG, CG), lambda e: (0, 0, 0)),
                  pl.BlockSpec((1, GROUP_W), lambda e: (0, 0)),
                  pl.BlockSpec((S, GROUP_W), lambda e: (e, 0))],
        out_specs=(pl.BlockSpec((S, GROUP_W), lambda e: (e, 0)), pl.BlockSpec((4, CG, CG), lambda e: (0, 0, 0)),
                   pl.BlockSpec((1, GROUP_W), lambda e: (0, 0))),
        scratch_shapes=[pltpu.VMEM((S + 2 * _PAD, GROUP_W), F32), pltpu.VMEM((S, CG), BF16),
                        pltpu.VMEM((S + 2 * _PAD, CG), F32)],
        name=name, compiler_params=_cp(("arbitrary",)))(z, pool_w, pool_scale, dpre)


def _na_tables():
    d = np.arange(NA_ROWS)[:, None]
    kr = np.arange(NA_ROWS)[None, :]
    ro = kr - d + (NA_ROWS - 1)
    qc = np.arange(GRID_W)[:, None]
    kc = np.arange(GRID_W)[None, :]
    cs = np.clip(qc - NA_COLS // 2, 0, GRID_W - NA_COLS)
    valid = (kc >= cs) & (kc < cs + NA_COLS)
    co = np.clip(kc - qc + (NA_COLS - 1), 0, 2 * NA_COLS - 2)
    return ro, co, valid


def _na_bias(rpb):
    ro, co, valid = _na_tables()
    t = rpb[:, ro]
    t = t[..., co]
    t = jnp.transpose(t, (0, 1, 3, 2, 4))
    t = jnp.where(valid[None, None, :, None, :], t, NEG)
    return t.reshape(NA_HEADS, NA_ROWS, GRID_W, NA_ROWS * GRID_W)


def _seg_mean_matrix(width, seg):
    i = np.arange(width)
    return jnp.asarray((i[:, None] // seg == i[None, :] // seg).astype(np.float32) / seg)


def _na_fwd(z, qg, kg, bias, S, name):
    T = z.shape[0]
    E = T // S
    rows = S // GRID_W
    WIN = NA_ROWS * GRID_W
    seg = _seg_mean_matrix(128, NA_DH)

    def body(q_ref, k_ref, v_ref, qg_ref, kg_ref, bias_ref, seg_ref, o_ref, qs, ks, vs):
        for c in range(S // _RC):
            sl = slice(c * _RC, (c + 1) * _RC)
            q = q_ref[sl, :]
            k = k_ref[sl, :]
            qn = q * lax.rsqrt(jnp.dot(q * q, seg_ref[...], precision=HI, preferred_element_type=F32) + EPS) * qg_ref[...]
            kn = k * lax.rsqrt(jnp.dot(k * k, seg_ref[...], precision=HI, preferred_element_type=F32) + EPS) * kg_ref[...]
            v = v_ref[sl, :]
            for hh in range(2):
                ls = slice(hh * NA_DH, (hh + 1) * NA_DH)
                qs[hh, sl, :] = qn[:, ls].astype(BF16)
                ks[hh, sl, :] = kn[:, ls].astype(BF16)
                vs[hh, sl, :] = v[:, ls].astype(BF16)
        def row(r, carry):
            rs = jnp.clip(r - NA_ROWS // 2, 0, rows - NA_ROWS)
            q0 = pl.multiple_of(r * GRID_W, GRID_W)
            k0 = pl.multiple_of(rs * GRID_W, GRID_W)
            outs = []
            for hh in range(2):
                qr = qs[hh, pl.ds(q0, GRID_W), :]
                kw = ks[hh, pl.ds(k0, WIN), :]
                vw = vs[hh, pl.ds(k0, WIN), :]
                s = lax.dot_general(qr, kw, (((1,), (1,)), ((), ())), preferred_element_type=F32) * (NA_DH ** -0.5)
                s = s + bias_ref[hh, r - rs]
                p = jnp.exp(s - jnp.max(s, axis=-1, keepdims=True))
                l = jnp.sum(p, axis=-1, keepdims=True)
                outs.append(jnp.dot(p.astype(BF16), vw, preferred_element_type=F32) / l)
            o_ref[pl.ds(q0, GRID_W), :] = jnp.concatenate(outs, axis=1)
            return carry
        lax.fori_loop(0, rows, row, 0)

    LW = 128
    return pl.pallas_call(
        body, out_shape=jax.ShapeDtypeStruct((T, GROUP_W), F32), grid=(E, GROUP_W // LW),
        in_specs=[pl.BlockSpec((S, LW), lambda e, j: (e, B_Q // LW + j)),
                  pl.BlockSpec((S, LW), lambda e, j: (e, B_K // LW + j)),
                  pl.BlockSpec((S, LW), lambda e, j: (e, B_V // LW + j)),
                  pl.BlockSpec((1, LW), lambda e, j: (0, j)),
                  pl.BlockSpec((1, LW), lambda e, j: (0, j)),
                  pl.BlockSpec((2, NA_ROWS, GRID_W, WIN), lambda e, j: (j, 0, 0, 0)),
                  pl.BlockSpec((LW, LW), lambda e, j: (0, 0))],
        out_specs=pl.BlockSpec((S, LW), lambda e, j: (e, j)),
        scratch_shapes=[pltpu.VMEM((2, S, NA_DH), BF16)] * 3,
        name=name, compiler_params=_cp(("parallel", "parallel")))(z, z, z, qg, kg, bias, seg)


def _na_bwd(z, qg, kg, bias, do, S, name):
    T = z.shape[0]
    E = T // S
    rows = S // GRID_W
    WIN = NA_ROWS * GRID_W
    seg = _seg_mean_matrix(128, NA_DH)
    SC = NA_DH ** -0.5

    def body(q_ref, k_ref, v_ref, qg_ref, kg_ref, bias_ref, seg_ref, do_ref,
             dq_ref, dk_ref, dv_ref, dbias_ref, dqg_ref, dkg_ref, qs, ks, vs, dos, dqn, dkn, dvs):
        e = pl.program_id(1)

        @pl.when(e == 0)
        def _():
            dbias_ref[...] = jnp.zeros_like(dbias_ref)
            dqg_ref[...] = jnp.zeros_like(dqg_ref)
            dkg_ref[...] = jnp.zeros_like(dkg_ref)

        for c in range(S // _RC):
            sl = slice(c * _RC, (c + 1) * _RC)
            q = q_ref[sl, :]
            k = k_ref[sl, :]
            qn = q * lax.rsqrt(jnp.dot(q * q, seg_ref[...], precision=HI, preferred_element_type=F32) + EPS) * qg_ref[...]
            kn = k * lax.rsqrt(jnp.dot(k * k, seg_ref[...], precision=HI, preferred_element_type=F32) + EPS) * kg_ref[...]
            v = v_ref[sl, :]
            dd = do_ref[sl, :]
            for hh in range(2):
                ls = slice(hh * NA_DH, (hh + 1) * NA_DH)
                qs[hh, sl, :] = qn[:, ls].astype(BF16)
                ks[hh, sl, :] = kn[:, ls].astype(BF16)
                vs[hh, sl, :] = v[:, ls].astype(BF16)
                dos[hh, sl, :] = dd[:, ls].astype(BF16)
            dkn[sl, :] = jnp.zeros((_RC, 128), F32)
            dvs[sl, :] = jnp.zeros((_RC, 128), F32)

        def row(r, carry):
            rs = jnp.clip(r - NA_ROWS // 2, 0, rows - NA_ROWS)
            q0 = pl.multiple_of(r * GRID_W, GRID_W)
            k0 = pl.multiple_of(rs * GRID_W, GRID_W)
            dqs, dks, dvv = [], [], []
            for hh in range(2):
                qr = qs[hh, pl.ds(q0, GRID_W), :]
                dor = dos[hh, pl.ds(q0, GRID_W), :]
                kw = ks[hh, pl.ds(k0, WIN), :]
                vw = vs[hh, pl.ds(k0, WIN), :]
                s = lax.dot_general(qr, kw, (((1,), (1,)), ((), ())), preferred_element_type=F32) * SC
                s = s + bias_ref[hh, r - rs]
                p = jnp.exp(s - jnp.max(s, axis=-1, keepdims=True))
                p = p / jnp.sum(p, axis=-1, keepdims=True)
                dp = lax.dot_general(dor, vw, (((1,), (1,)), ((), ())), preferred_element_type=F32)
                ds = p * (dp - jnp.sum(p * dp, axis=-1, keepdims=True))
                dbias_ref[hh, r - rs] += ds
                dsb = ds.astype(BF16)
                dqs.append(jnp.dot(dsb, kw, preferred_element_type=F32) * SC)
                dks.append(lax.dot_general(dsb, qr, (((0,), (0,)), ((), ())), preferred_element_type=F32) * SC)
                dvv.append(lax.dot_general(p.astype(BF16), dor, (((0,), (0,)), ((), ())), preferred_element_type=F32))
            dqn[pl.ds(q0, GRID_W), :] = jnp.concatenate(dqs, axis=1)
            dkn[pl.ds(k0, WIN), :] += jnp.concatenate(dks, axis=1)
            dvs[pl.ds(k0, WIN), :] += jnp.concatenate(dvv, axis=1)
            return carry
        lax.fori_loop(0, rows, row, 0)

        for c in range(S // _RC):
            sl = slice(c * _RC, (c + 1) * _RC)
            for x_ref, g_ref, dn, dx_ref, dg_ref in ((q_ref, qg_ref, dqn, dq_ref, dqg_ref), (k_ref, kg_ref, dkn, dk_ref, dkg_ref)):
                x = x_ref[sl, :]
                r_ = lax.rsqrt(jnp.dot(x * x, seg_ref[...], precision=HI, preferred_element_type=F32) + EPS)
                xh = x * r_
                d = dn[sl, :]
                dxh = d * g_ref[...]
                mean = jnp.dot(dxh * xh, seg_ref[...], precision=HI, preferred_element_type=F32)
                dx_ref[sl, :] = (r_ * (dxh - xh * mean)).astype(BF16)
                dg_ref[...] += jnp.sum(d * xh, axis=0, keepdims=True)
            dv_ref[sl, :] = dvs[sl, :].astype(BF16)

    LW = 128
    blk = pl.BlockSpec((S, LW), lambda j, e: (e, j))
    vec = pl.BlockSpec((1, LW), lambda j, e: (0, j))
    bsp = pl.BlockSpec((2, NA_ROWS, GRID_W, WIN), lambda j, e: (j, 0, 0, 0))
    return pl.pallas_call(
        body, out_shape=(jax.ShapeDtypeStruct((T, GROUP_W), BF16),) * 3 + (
            jax.ShapeDtypeStruct((NA_HEADS, NA_ROWS, GRID_W, WIN), F32),
            jax.ShapeDtypeStruct((1, GROUP_W), F32), jax.ShapeDtypeStruct((1, GROUP_W), F32)),
        grid=(GROUP_W // LW, E),
        in_specs=[pl.BlockSpec((S, LW), lambda j, e: (e, B_Q // LW + j)),
                  pl.BlockSpec((S, LW), lambda j, e: (e, B_K // LW + j)),
                  pl.BlockSpec((S, LW), lambda j, e: (e, B_V // LW + j)),
                  vec, vec, bsp, pl.BlockSpec((LW, LW), lambda j, e: (0, 0)), blk],
        out_specs=(blk, blk, blk, bsp, vec, vec),
        scratch_shapes=[pltpu.VMEM((2, S, NA_DH), BF16)] * 4 + [pltpu.VMEM((S, LW), F32)] * 3,
        name=name, compiler_params=_cp(("parallel", "arbitrary")))(z, z, z, qg, kg, bias, seg, do)


def _na_rpb_grad(dbias, name):
    ro, co, valid = _na_tables()
    H = NA_HEADS
    x = dbias.reshape(H, NA_ROWS, GRID_W, NA_ROWS, GRID_W)
    x = jnp.transpose(x, (0, 1, 3, 2, 4)).reshape(H, NA_ROWS * NA_ROWS, GRID_W * GRID_W)
    e_np = np.zeros((GRID_W, GRID_W, 128), np.float32)
    qi, ki = np.nonzero(valid)
    e_np[qi, ki, co[qi, ki]] = 1.0
    a_np = np.zeros((16, NA_ROWS * NA_ROWS), np.float32)
    a_np[ro.reshape(-1), np.arange(NA_ROWS * NA_ROWS)] = 1.0

    def body(x_ref, e_ref, a_ref, o_ref):
        y = jnp.dot(x_ref[0], e_ref[...], precision=HI, preferred_element_type=F32)
        o_ref[0] = jnp.dot(a_ref[...], y, precision=HI, preferred_element_type=F32)

    out = pl.pallas_call(
        body, out_shape=jax.ShapeDtypeStruct((H, 16, 128), F32), grid=(H,),
        in_specs=[pl.BlockSpec((1, 64, GRID_W * GRID_W), lambda h: (h, 0, 0)),
                  pl.BlockSpec((GRID_W * GRID_W, 128), lambda h: (0, 0)),
                  pl.BlockSpec((16, 64), lambda h: (0, 0))],
        out_specs=pl.BlockSpec((1, 16, 128), lambda h: (h, 0, 0)),
        name=name, compiler_params=_cp(("parallel",)))(x, jnp.asarray(e_np.reshape(GRID_W * GRID_W, 128)), jnp.asarray(a_np))
    return out[:, :2 * NA_ROWS - 1, :2 * NA_COLS - 1]


_HK = GLA_HEADS * GLA_DK
_HV = GLA_HEADS * GLA_DV


def _gla_consts(reverse):
    i = np.arange(CHUNK)
    tri = (i[:, None] <= i[None, :]) if reverse else (i[:, None] >= i[None, :])
    j = np.arange(_HK)
    oseg = (j[:, None] // GLA_DK == j[None, :] // GLA_DK)
    return (jnp.asarray(tri.astype(np.float32)), jnp.asarray(tri.T.astype(np.float32)), jnp.asarray(oseg.astype(np.float32), BF16))


def _log_decay(lr, a2, ab):
    zg = jnp.dot(lr, a2, precision=HI, preferred_element_type=F32) + ab
    g = (jnp.minimum(zg, 0.0) - jnp.log(1.0 + jnp.exp(-jnp.abs(zg)))) * (1.0 / GLA_TAU)
    return zg, g


def _dotf(a, b, dn):
    return lax.dot_general(a, b, dn, precision=HI, preferred_element_type=F32)


def _dotb(a, b, dn):
    return lax.dot_general(a.astype(BF16), b.astype(BF16), dn, preferred_element_type=F32)


_COLS = 4


_NN = (((1,), (0,)), ((), ()))
_NT = (((1,), (1,)), ((), ()))
_TN = (((0,), (0,)), ((), ()))


def _gla_fwd(z, a2, ab, S, reverse, name):
    T = z.shape[0]
    E = T // S
    n = S // CHUNK
    tri, _, oseg = _gla_consts(reverse)

    def body(q_ref, k_ref, v_ref, lr_ref, a2_ref, ab_ref, tri_ref, oseg_ref, o_ref, a_ref, st_ref, st, b_s, q_s, k_s):
        @pl.when(pl.program_id(1) == 0)
        def _():
            st[...] = jnp.zeros_like(st)

        q = q_ref[...] * (GLA_DK ** -0.5)
        k = k_ref[...]
        v = v_ref[...]
        _, g = _log_decay(lr_ref[...], a2_ref[...], ab_ref[...])
        b = _dotf(tri_ref[...], g, _NN)
        bl_row = jnp.sum(g, axis=0, keepdims=True)
        bl_col = _dotf(g, jnp.ones((CHUNK, GLA_DV), F32), _TN)
        b_s[...] = b
        q_s[...] = q
        k_s[...] = k
        lane = lax.broadcasted_iota(jnp.int32, (1, _HK), 1) % GLA_DK

        def cols(jj, a):
            ts = []
            for u in range(_COLS):
                j = jj * _COLS + u
                bj = b_s[pl.ds(j, 1), :]
                kj = k_s[pl.ds(j, 1), :]
                ts.append((q_s[...] * jnp.exp(jnp.minimum(b_s[...] - bj, 0.0)) * kj).astype(BF16))
            r = jnp.dot(jnp.concatenate(ts, axis=0), oseg_ref[...], preferred_element_type=F32)
            for u in range(_COLS):
                a = jnp.where(lane == jj * _COLS + u, r[u * CHUNK:(u + 1) * CHUNK, :], a)
            return a

        a = lax.fori_loop(0, CHUNK // _COLS, cols, jnp.zeros((CHUNK, _HK), F32))
        rowi = lax.broadcasted_iota(jnp.int32, (CHUNK, 1), 0)
        keep = (rowi <= lane) if reverse else (rowi >= lane)
        a = jnp.where(keep, a, 0.0)
        a_ref[...] = a
        st_ref[0] = st[...]
        qb = q * jnp.exp(b)
        kd = k * jnp.exp(bl_row - b)
        for h in range(GLA_HEADS):
            ks_ = slice(h * GLA_DK, (h + 1) * GLA_DK)
            vs_ = slice(h * GLA_DV, (h + 1) * GLA_DV)
            s_h = st[ks_, :]
            o_ref[:, vs_] = _dotb(qb[:, ks_], s_h, _NN) + _dotb(a[:, ks_], v[:, vs_], _NN)
            st[ks_, :] = s_h * jnp.exp(bl_col[ks_, :]) + _dotb(kd[:, ks_], v[:, vs_], _TN)

    def rowblk(e, c):
        return e * n + ((n - 1 - c) if reverse else c)

    return pl.pallas_call(
        body, out_shape=(jax.ShapeDtypeStruct((T, _HV), F32), jax.ShapeDtypeStruct((T, _HK), F32),
                         jax.ShapeDtypeStruct((T // CHUNK, _HK, GLA_DV), F32)),
        grid=(E, n),
        in_specs=[pl.BlockSpec((CHUNK, _HK), lambda e, c: (rowblk(e, c), C_Q // _HK)),
                  pl.BlockSpec((CHUNK, _HK), lambda e, c: (rowblk(e, c), C_K // _HK)),
                  pl.BlockSpec((CHUNK, _HV), lambda e, c: (rowblk(e, c), C_V // _HV)),
                  pl.BlockSpec((CHUNK, 128), lambda e, c: (rowblk(e, c), LR_OFF // 128)),
                  pl.BlockSpec((128, _HK), lambda e, c: (0, 0)),
                  pl.BlockSpec((1, _HK), lambda e, c: (0, 0)),
                  pl.BlockSpec((CHUNK, CHUNK), lambda e, c: (0, 0)),
                  pl.BlockSpec((_HK, _HK), lambda e, c: (0, 0))],
        out_specs=(pl.BlockSpec((CHUNK, _HV), lambda e, c: (rowblk(e, c), 0)),
                   pl.BlockSpec((CHUNK, _HK), lambda e, c: (rowblk(e, c), 0)),
                   pl.BlockSpec((1, _HK, GLA_DV), lambda e, c: (rowblk(e, c), 0, 0))),
        scratch_shapes=[pltpu.VMEM((_HK, GLA_DV), F32)] + [pltpu.VMEM((CHUNK, _HK), F32)] * 3,
        name=name, compiler_params=_cp(("parallel", "arbitrary")))(z, z, z, z, a2, ab, tri, oseg)


def _gla_bwd(z, a2, ab, att, states, do, prev, S, reverse, name):
    T = z.shape[0]
    E = T // S
    n = S // CHUNK
    tri, tri_t, oseg = _gla_consts(reverse)
    has_prev = prev is not None
    odt = BF16 if has_prev else F32

    def body(*refs):
        (q_ref, k_ref, v_ref, lr_ref, a2_ref, ab_ref, tri_ref, trit_ref, oseg_ref, att_ref, st_ref, do_ref) = refs[:12]
        refs = refs[12:]
        if has_prev:
            pq_ref, pk_ref, pv_ref, pl_ref = refs[:4]
            refs = refs[4:]
        (dq_ref, dk_ref, dv_ref, dlr_ref, da2_ref, dab_ref, dst, b_s, q_s, k_s, da_s, dqb_s, dkd_s, dk3_s, dbn_s, dsp_s) = refs
        first = (pl.program_id(0) == 0) & (pl.program_id(1) == 0)

        @pl.when(first)
        def _():
            da2_ref[...] = jnp.zeros_like(da2_ref)
            dab_ref[...] = jnp.zeros_like(dab_ref)

        @pl.when(pl.program_id(1) == 0)
        def _():
            dst[...] = jnp.zeros_like(dst)

        q = q_ref[...] * (GLA_DK ** -0.5)
        k = k_ref[...]
        v = v_ref[...]
        lr = lr_ref[...]
        zg, g = _log_decay(lr, a2_ref[...], ab_ref[...])
        b = _dotf(tri_ref[...], g, _NN)
        bl_row = jnp.sum(g, axis=0, keepdims=True)
        bl_col = _dotf(g, jnp.ones((CHUNK, GLA_DV), F32), _TN)
        eb = jnp.exp(b)
        ekd = jnp.exp(bl_row - b)
        qb = q * eb
        kd = k * ekd
        b_s[...] = b
        q_s[...] = q
        k_s[...] = k
        att = att_ref[...]
        s_all = st_ref[0]
        dsn = dst[...]
        e_col = jnp.exp(bl_col)
        do = do_ref[...]
        lane = lax.broadcasted_iota(jnp.int32, (1, _HK), 1) % GLA_DK
        rowi = lax.broadcasted_iota(jnp.int32, (CHUNK, 1), 0)
        keep = (rowi <= lane) if reverse else (rowi >= lane)
        for h in range(GLA_HEADS):
            ks_ = slice(h * GLA_DK, (h + 1) * GLA_DK)
            vs_ = slice(h * GLA_DV, (h + 1) * GLA_DV)
            do_h = do[:, vs_]
            s_h = s_all[ks_, :]
            dsn_h = dsn[ks_, :]
            dqb_s[:, ks_] = _dotb(do_h, s_h, _NT)
            dsp_s[ks_, :] = _dotb(qb[:, ks_], do_h, _TN) + dsn_h * e_col[ks_, :]
            da_s[:, ks_] = _dotb(do_h, v[:, vs_], _NT)
            dv_h = _dotb(att[:, ks_], do_h, _TN) + _dotb(kd[:, ks_], dsn_h, _NN)
            if has_prev:
                dv_h = dv_h + pv_ref[:, vs_]
            dv_ref[:, vs_] = dv_h.astype(odt)
            dkd_s[:, ks_] = _dotb(v[:, vs_], dsn_h, _NT)
        da_s[...] = jnp.where(keep, da_s[...], 0.0)
        dqb = dqb_s[...]
        dkd = dkd_s[...]
        x = dsn * s_all * e_col
        dbl_row = _dotf(jnp.ones((8, GLA_DV), F32), x, _NT)[0:1, :] + jnp.sum(dkd * kd, axis=0, keepdims=True)

        def cols(jj, carry):
            dq3, db3 = carry
            sel = [jnp.where(lane == jj * _COLS + u, da_s[...], 0.0).astype(BF16) for u in range(_COLS)]
            dcols = jnp.dot(jnp.concatenate(sel, axis=0), oseg_ref[...], preferred_element_type=F32)
            for u in range(_COLS):
                j = jj * _COLS + u
                bj = b_s[pl.ds(j, 1), :]
                kj = k_s[pl.ds(j, 1), :]
                tm_ = dcols[u * CHUNK:(u + 1) * CHUNK, :] * jnp.exp(jnp.minimum(b_s[...] - bj, 0.0))
                dq3 = dq3 + tm_ * kj
                gq = tm_ * q_s[...]
                dk3_s[pl.ds(j, 1), :] = jnp.sum(gq, axis=0, keepdims=True)
                w = gq * kj
                dbn_s[pl.ds(j, 1), :] = jnp.sum(w, axis=0, keepdims=True)
                db3 = db3 + w
            return dq3, db3

        zero = jnp.zeros((CHUNK, _HK), F32)
        dq3, db3 = lax.fori_loop(0, CHUNK // _COLS, cols, (zero, zero))
        dq = (dqb * eb + dq3) * (GLA_DK ** -0.5)
        dk = dkd * ekd + dk3_s[...]
        db = dqb * qb - dkd * kd + db3 - dbn_s[...]
        dg = _dotf(trit_ref[...], db, _NN) + dbl_row
        sneg = 1.0 / (1.0 + jnp.exp(zg))
        dzg = dg * sneg * (1.0 / GLA_TAU)
        dlr = _dotf(dzg, a2_ref[...], _NT)
        da2_ref[...] += _dotf(lr, dzg, _TN)
        dab_ref[...] += jnp.sum(dzg, axis=0, keepdims=True)
        if has_prev:
            dq = dq + pq_ref[...]
            dk = dk + pk_ref[...]
            dlr = dlr + pl_ref[...]
        dq_ref[...] = dq.astype(odt)
        dk_ref[...] = dk.astype(odt)
        dlr_ref[...] = dlr.astype(odt)
        dst[...] = dsp_s[...]

    def rowblk(e, c):
        return e * n + (c if reverse else (n - 1 - c))

    hk = pl.BlockSpec((CHUNK, _HK), lambda e, c: (rowblk(e, c), 0))
    hv = pl.BlockSpec((CHUNK, _HV), lambda e, c: (rowblk(e, c), 0))
    l128 = pl.BlockSpec((CHUNK, 128), lambda e, c: (rowblk(e, c), 0))
    in_specs = [pl.BlockSpec((CHUNK, _HK), lambda e, c: (rowblk(e, c), C_Q // _HK)),
                pl.BlockSpec((CHUNK, _HK), lambda e, c: (rowblk(e, c), C_K // _HK)),
                pl.BlockSpec((CHUNK, _HV), lambda e, c: (rowblk(e, c), C_V // _HV)),
                pl.BlockSpec((CHUNK, 128), lambda e, c: (rowblk(e, c), LR_OFF // 128)),
                pl.BlockSpec((128, _HK), lambda e, c: (0, 0)),
                pl.BlockSpec((1, _HK), lambda e, c: (0, 0)),
                pl.BlockSpec((CHUNK, CHUNK), lambda e, c: (0, 0)),
                pl.BlockSpec((CHUNK, CHUNK), lambda e, c: (0, 0)),
                pl.BlockSpec((_HK, _HK), lambda e, c: (0, 0)),
                hk, pl.BlockSpec((1, _HK, GLA_DV), lambda e, c: (rowblk(e, c), 0, 0)), hv]
    args = [z, z, z, z, a2, ab, tri, tri_t, oseg, att, states, do]
    if has_prev:
        in_specs += [hk, hk, hv, l128]
        args += list(prev)
    return pl.pallas_call(
        body, out_shape=(jax.ShapeDtypeStruct((T, _HK), odt), jax.ShapeDtypeStruct((T, _HK), odt),
                         jax.ShapeDtypeStruct((T, _HV), odt), jax.ShapeDtypeStruct((T, 128), odt),
                         jax.ShapeDtypeStruct((128, _HK), F32), jax.ShapeDtypeStruct((1, _HK), F32)),
        grid=(E, n), in_specs=in_specs,
        out_specs=(hk, hk, hv, l128, pl.BlockSpec((128, _HK), lambda e, c: (0, 0)), pl.BlockSpec((1, _HK), lambda e, c: (0, 0))),
        scratch_shapes=[pltpu.VMEM((_HK, GLA_DV), F32)] + [pltpu.VMEM((CHUNK, _HK), F32)] * 8 + [pltpu.VMEM((_HK, GLA_DV), F32)],
        name=name, compiler_params=_cp(("arbitrary", "arbitrary")))(*args)


def _gla_norm_fwd(of, ob, og, name):
    T = of.shape[0]
    tm = 256

    def body(f_ref, b_ref, g_ref, o_ref):
        for h in range(GLA_HEADS):
            vs_ = slice(h * GLA_DV, (h + 1) * GLA_DV)
            o = f_ref[:, vs_] + b_ref[:, vs_]
            o_ref[:, vs_] = o * lax.rsqrt(jnp.mean(o * o, axis=-1, keepdims=True) + EPS) * g_ref[:, vs_]

    row = pl.BlockSpec((tm, _HV), lambda i: (i, 0))
    vec = pl.BlockSpec((1, _HV), lambda i: (0, 0))
    return pl.pallas_call(body, out_shape=jax.ShapeDtypeStruct((T, _HV), F32), grid=(T // tm,),
                          in_specs=[row, row, vec], out_specs=row, name=name, compiler_params=_cp(("parallel",)))(of, ob, og)


def _gla_norm_bwd(of, ob, og, dpre, name):
    T = of.shape[0]
    tm = 256

    def body(f_ref, b_ref, g_ref, dp_ref, do_ref, dg_ref):
        @pl.when(pl.program_id(0) == 0)
        def _():
            dg_ref[...] = jnp.zeros_like(dg_ref)

        for h in range(GLA_HEADS):
            vs_ = slice(h * GLA_DV, (h + 1) * GLA_DV)
            o = f_ref[:, vs_] + b_ref[:, vs_]
            r = lax.rsqrt(jnp.mean(o * o, axis=-1, keepdims=True) + EPS)
            xh = o * r
            dp = dp_ref[:, vs_]
            dxh = dp * g_ref[:, vs_]
            do_ref[:, vs_] = r * (dxh - xh * jnp.mean(dxh * xh, axis=-1, keepdims=True))
            dg_ref[:, vs_] += jnp.sum(dp * xh, axis=0, keepdims=True)

    row = pl.BlockSpec((tm, _HV), lambda i: (i, 0))
    vec = pl.BlockSpec((1, _HV), lambda i: (0, 0))
    return pl.pallas_call(
        body, out_shape=(jax.ShapeDtypeStruct((T, _HV), F32), jax.ShapeDtypeStruct((1, _HV), F32)), grid=(T // tm,),
        in_specs=[row, row, vec, row], out_specs=(row, vec), name=name, compiler_params=_cp(("arbitrary",)))(of, ob, og, dpre)


_ANY = pl.BlockSpec(memory_space=pl.ANY)


def _coords():
    return lax.axis_index("x"), lax.axis_index("y"), lax.axis_index("c")


def _gather_weights(win, wout, small, name):
    def body(win_ref, wout_ref, sm_ref, gwin_ref, gwout_ref, gsm_ref, send_sems, recv_sems, local_sems):
        x, y, c = _coords()
        me = 2 * x + y
        srcs = (win_ref, wout_ref, sm_ref)
        dsts = (gwin_ref, gwout_ref, gsm_ref)
        loc = [pltpu.make_async_copy(s, d.at[me], local_sems.at[i]) for i, (s, d) in enumerate(zip(srcs, dsts))]
        for cp in loc:
            cp.start()
        rem = []
        for j, (px, py) in enumerate(((1 - x, y), (x, 1 - y), (1 - x, 1 - y))):
            for i, (s, d) in enumerate(zip(srcs, dsts)):
                rem.append(pltpu.make_async_remote_copy(
                    src_ref=s, dst_ref=d.at[me], send_sem=send_sems.at[3 * j + i], recv_sem=recv_sems.at[3 * j + i],
                    device_id=(px, py, c), device_id_type=MESH))
        for cp in rem:
            cp.start()
        for cp in rem:
            cp.wait()
        for cp in loc:
            cp.wait()

    return pl.pallas_call(
        body, out_shape=tuple(jax.ShapeDtypeStruct((4,) + a.shape, a.dtype) for a in (win, wout, small)),
        in_specs=[_ANY] * 3, out_specs=(_ANY,) * 3,
        scratch_shapes=[pltpu.SemaphoreType.DMA((9,)), pltpu.SemaphoreType.DMA((9,)), pltpu.SemaphoreType.DMA((3,))],
        name=name)(win, wout, small)


def _grad_exchange(pwin, pwout, small, name):
    def body(pw_ref, po_ref, sm_ref, rw_ref, ro_ref, rs_ref, send_sems, recv_sems, local_sems):
        x, y, c = _coords()
        me = 2 * x + y
        dev = 4 * x + 2 * y + c
        loc = [pltpu.make_async_copy(pw_ref.at[me], rw_ref.at[me], local_sems.at[0]),
               pltpu.make_async_copy(po_ref.at[me], ro_ref.at[me], local_sems.at[1]),
               pltpu.make_async_copy(sm_ref, rs_ref.at[dev], local_sems.at[2])]
        for cp in loc:
            cp.start()
        rem = []
        k = 0
        for px, py in ((1 - x, y), (x, 1 - y), (1 - x, 1 - y)):
            tgt = 2 * px + py
            for s, d in ((pw_ref, rw_ref), (po_ref, ro_ref)):
                rem.append(pltpu.make_async_remote_copy(
                    src_ref=s.at[tgt], dst_ref=d.at[me], send_sem=send_sems.at[k], recv_sem=recv_sems.at[k],
                    device_id=(px, py, c), device_id_type=MESH))
                k += 1
        for px, py, pc in ((x, y, 1 - c), (1 - x, y, c), (1 - x, y, 1 - c), (x, 1 - y, c), (x, 1 - y, 1 - c),
                           (1 - x, 1 - y, c), (1 - x, 1 - y, 1 - c)):
            rem.append(pltpu.make_async_remote_copy(
                src_ref=sm_ref, dst_ref=rs_ref.at[dev], send_sem=send_sems.at[k], recv_sem=recv_sems.at[k],
                device_id=(px, py, pc), device_id_type=MESH))
            k += 1
        for cp in rem:
            cp.start()
        for cp in rem:
            cp.wait()
        for cp in loc:
            cp.wait()

    return pl.pallas_call(
        body, out_shape=(jax.ShapeDtypeStruct(pwin.shape, pwin.dtype), jax.ShapeDtypeStruct(pwout.shape, pwout.dtype),
                         jax.ShapeDtypeStruct((8,) + small.shape, small.dtype)),
        in_specs=[_ANY] * 3, out_specs=(_ANY,) * 3,
        scratch_shapes=[pltpu.SemaphoreType.DMA((13,)), pltpu.SemaphoreType.DMA((13,)), pltpu.SemaphoreType.DMA((3,))],
        name=name)(pwin, pwout, small)


def _sibling_swap(a, b, name):
    def body(a_ref, b_ref, oa_ref, ob_ref, send_sems, recv_sems):
        x, y, c = _coords()
        rem = [pltpu.make_async_remote_copy(src_ref=s, dst_ref=d, send_sem=send_sems.at[i], recv_sem=recv_sems.at[i],
                                            device_id=(x, y, 1 - c), device_id_type=MESH)
               for i, (s, d) in enumerate(((a_ref, oa_ref), (b_ref, ob_ref)))]
        for cp in rem:
            cp.start()
        for cp in rem:
            cp.wait()

    return pl.pallas_call(
        body, out_shape=(jax.ShapeDtypeStruct(a.shape, a.dtype), jax.ShapeDtypeStruct(b.shape, b.dtype)),
        in_specs=[_ANY] * 2, out_specs=(_ANY,) * 2,
        scratch_shapes=[pltpu.SemaphoreType.DMA((2,)), pltpu.SemaphoreType.DMA((2,))], name=name)(a, b)


def _sum_slots(r, name):
    n, R, C = r.shape
    tm = 256

    def body(r_ref, o_ref):
        acc = r_ref[0].astype(F32)
        for i in range(1, n):
            acc = acc + r_ref[i].astype(F32)
        o_ref[...] = acc

    return pl.pallas_call(body, out_shape=jax.ShapeDtypeStruct((R, C), F32), grid=(R // tm,),
                          in_specs=[pl.BlockSpec((n, tm, C), lambda i: (0, i, 0))], out_specs=pl.BlockSpec((tm, C), lambda i: (i, 0)),
                          name=name, compiler_params=_cp(("parallel",)))(r)


def _adamw_math(w, g, m, v):
    m = ADAM_B1 * m + (1.0 - ADAM_B1) * g
    v = ADAM_B2 * v + (1.0 - ADAM_B2) * (g * g)
    m_hat = m / (1.0 - ADAM_B1 ** ADAM_STEP)
    v_hat = v / (1.0 - ADAM_B2 ** ADAM_STEP)
    delta = -ADAM_LR * (m_hat / (jnp.sqrt(v_hat) + ADAM_EPS) + ADAM_WD * w)
    return delta, m, v


def _adamw(w, gs, m, v, tm, name):
    R, C = w.shape

    def body(*refs):
        w_ref = refs[0]
        g_refs = refs[1:1 + len(gs)]
        m_ref, v_ref, g_out, d_out, m_out, v_out = refs[1 + len(gs):]
        g = None
        for gr in g_refs:
            parts = [gr[i] for i in range(gr.shape[0])] if len(gr.shape) == 3 else [gr[...]]
            for p in parts:
                g = p if g is None else g + p
        d, mn, vn = _adamw_math(w_ref[...], g, m_ref[...], v_ref[...])
        g_out[...] = g
        d_out[...] = d
        m_out[...] = mn
        v_out[...] = vn

    blk = pl.BlockSpec((tm, C), lambda i: (i, 0))
    g_specs = [pl.BlockSpec((g.shape[0], tm, C), lambda i: (0, i, 0)) if g.ndim == 3 else blk for g in gs]
    return pl.pallas_call(
        body, out_shape=tuple(jax.ShapeDtypeStruct((R, C), F32) for _ in range(4)), grid=(R // tm,),
        in_specs=[blk] + g_specs + [blk, blk], out_specs=(blk,) * 4, name=name,
        compiler_params=_cp(("parallel",)))(w, *gs, m, v)


WEIGHTS = ("norm_g", "w_in", "conv_w", "conv_b", "conv_ln_g", "conv_ln_b", "na_q_g", "na_k_g", "na_rpb", "gla_a2_f",
           "gla_ab_f", "gla_a2_b", "gla_ab_b", "gla_o_g", "pool_w", "pool_scale", "w_out")
_REPL = ("norm_g", "conv_b", "conv_ln_g", "conv_ln_b", "na_q_g", "na_k_g", "na_rpb", "gla_ab_f", "gla_ab_b", "gla_o_g",
         "pool_w", "pool_scale")
_SHARD_SMALL = ("conv_w", "gla_a2_f", "gla_a2_b")
_PACK_ROWS = 8 * 128


def _pack(arrs):
    flat = jnp.concatenate([a.reshape(-1) for a in arrs])
    n = -(-flat.shape[0] // _PACK_ROWS) * _PACK_ROWS
    return jnp.pad(flat, (0, n - flat.shape[0])).reshape(-1, 128)


def _unpack(p, shapes):
    flat = p.reshape(-1)
    out, o = [], 0
    for s in shapes:
        n = int(np.prod(s))
        out.append(flat[o:o + n].reshape(s))
        o += n
    return out


def _to_layout(w):
    pad = jnp.zeros(w.shape[:-1] + (NZ - N_IN,), w.dtype)
    return jnp.concatenate([w[..., :5120], w[..., 5152:6176], w[..., 5120:5152], pad], axis=-1)


def _from_layout(w):
    return jnp.concatenate([w[..., :5120], w[..., LR_OFF:LR_OFF + 32], w[..., 5120:LR_OFF]], axis=-1)


def _layer_fwd(l, x, P, S):
    n = f"l{l}_"
    h = _rmsnorm_fwd(x, P["norm_g"], n + "rms_fwd")
    z = _matmul(h, P["w_in"], dims="nn", out_dtype=F32, tm=512, tn=1280, tk=D_MODEL, name=n + "mm_z")
    yc = _conv_fwd(z, P["conv_w32"], P["conv_b"], S, n + "conv_fwd")
    pre_a = _ln_silu_fwd(yc, P["conv_ln_g"], P["conv_ln_b"], n + "ln_fwd")
    pre_b = _na_fwd(z, P["na_q_g"], P["na_k_g"], P["na_bias"], S, n + "na_fwd")
    of, af, sf = _gla_fwd(z, P["a2_f"], P["gla_ab_f"], S, False, n + "gla_fwd_f")
    ob, ab, sb = _gla_fwd(z, P["a2_b"], P["gla_ab_b"], S, True, n + "gla_fwd_b")
    pre_c = _gla_norm_fwd(of, ob, P["gla_o_g"], n + "gla_norm_fwd")
    pre_d = _pool_fwd(z, P["pool_w_bf"], P["pool_scale"], S, n + "pool_fwd")
    pres = (pre_a, pre_b, pre_c, pre_d)
    y = _gate_fwd(pres, z, n + "gate_fwd")
    out = _matmul(y, P["w_out"], dims="nn", out_dtype=F32, tm=512, tn=1024, tk=D_MODEL, name=n + "mm_out", res=x)
    return out, dict(x=x, h=h, z=z, yc=yc, pres=pres, of=of, af=af, sf=sf, ob=ob, ab=ab, sb=sb, y=y)


def _layer_bwd(l, dout, dout_bf, sv, P, S):
    n = f"l{l}_"
    z = sv["z"]
    T = z.shape[0]
    dy = _matmul(dout_bf, P["w_out"], dims="nt", out_dtype=F32, tm=512, tn=1024, tk=D_MODEL, name=n + "mm_dy")
    d_w_out = _matmul(sv["y"], dout_bf, dims="tn", out_dtype=BF16, tm=512, tn=1024, tk=1024, name=n + "mm_dwout")
    dpa, dpb, dpc, dpd, dga, dgb, dgc, dgd = _gate_bwd(dy, sv["pres"], z, n + "gate_bwd")
    dyc, d_ln_g, d_ln_b = _ln_silu_bwd(sv["yc"], P["conv_ln_g"], P["conv_ln_b"], dpa, n + "ln_bwd")
    dval, dglu, d_cw, d_cb = _conv_bwd(z, P["conv_w32"], dyc, S, n + "conv_bwd")
    dq, dk, dv, dbias, d_qg, d_kg = _na_bwd(z, P["na_q_g"], P["na_k_g"], P["na_bias"], dpb, S, n + "na_bwd")
    d_rpb = _na_rpb_grad(dbias, n + "na_rpb")
    do, d_og = _gla_norm_bwd(sv["of"], sv["ob"], P["gla_o_g"], dpc, n + "gla_norm_bwd")
    part = _gla_bwd(z, P["a2_f"], P["gla_ab_f"], sv["af"], sv["sf"], do, None, S, False, n + "gla_bwd_f")
    d_a2f, d_abf = part[4], part[5]
    dcq, dck, dcv, dlr, d_a2b, d_abb = _gla_bwd(z, P["a2_b"], P["gla_ab_b"], sv["ab"], sv["sb"], do, part[:4], S, True,
                                                n + "gla_bwd_b")
    dd, d_pw, d_ps = _pool_bwd(z, P["pool_w_bf"], P["pool_scale"], dpd, S, n + "pool_bwd")
    dz = jnp.concatenate([dval, dglu, dga, dq, dk, dv, dgb, dcq, dck, dcv, dgc, dd, dgd, dlr,
                          jnp.zeros((T, NZ - LR_OFF - 128), BF16)], axis=1)
    dh = _matmul(dz, P["w_in"], dims="nt", out_dtype=F32, tm=512, tn=1024, tk=1280, name=n + "mm_dh")
    d_w_in = _matmul(sv["h"], dz, dims="tn", out_dtype=BF16, tm=512, tn=1280, tk=1024, name=n + "mm_dwin")
    dx, dx_bf, d_ng = _rmsnorm_bwd(sv["x"], P["norm_g"], dh, dout, n + "rms_bwd")
    grads = dict(norm_g=d_ng[0], w_in=d_w_in, conv_w=d_cw[:CONV_K], conv_b=d_cb[0], conv_ln_g=d_ln_g[0], conv_ln_b=d_ln_b[0],
                 na_q_g=d_qg.reshape(NA_HEADS, NA_DH), na_k_g=d_kg.reshape(NA_HEADS, NA_DH), na_rpb=d_rpb,
                 gla_a2_f=d_a2f[0:GLA_RANK], gla_ab_f=d_abf[0], gla_a2_b=d_a2b[GLA_RANK:2 * GLA_RANK], gla_ab_b=d_abb[0],
                 gla_o_g=d_og.reshape(GLA_HEADS, GLA_DV), pool_w=d_pw, pool_scale=d_ps[0], w_out=d_w_out)
    return dx, dx_bf, grads


def kernel(x, norm_g, w_in, conv_w, conv_b, conv_ln_g, conv_ln_b, na_q_g, na_k_g, na_rpb, gla_a2_f, gla_ab_f, gla_a2_b, gla_ab_b, gla_o_g, pool_w, pool_scale, w_out, loss_target, m_norm_g, m_w_in, m_conv_w, m_conv_b, m_conv_ln_g, m_conv_ln_b, m_na_q_g, m_na_k_g, m_na_rpb, m_gla_a2_f, m_gla_ab_f, m_gla_a2_b, m_gla_ab_b, m_gla_o_g, m_pool_w, m_pool_scale, m_w_out, v_norm_g, v_w_in, v_conv_w, v_conv_b, v_conv_ln_g, v_conv_ln_b, v_na_q_g, v_na_k_g, v_na_rpb, v_gla_a2_f, v_gla_ab_f, v_gla_a2_b, v_gla_ab_b, v_gla_o_g, v_pool_w, v_pool_scale, v_w_out):
    W = dict(norm_g=norm_g, w_in=w_in, conv_w=conv_w, conv_b=conv_b, conv_ln_g=conv_ln_g, conv_ln_b=conv_ln_b, na_q_g=na_q_g,
             na_k_g=na_k_g, na_rpb=na_rpb, gla_a2_f=gla_a2_f, gla_ab_f=gla_ab_f, gla_a2_b=gla_a2_b, gla_ab_b=gla_ab_b,
             gla_o_g=gla_o_g, pool_w=pool_w, pool_scale=pool_scale, w_out=w_out)
    M = dict(norm_g=m_norm_g, w_in=m_w_in, conv_w=m_conv_w, conv_b=m_conv_b, conv_ln_g=m_conv_ln_g, conv_ln_b=m_conv_ln_b,
             na_q_g=m_na_q_g, na_k_g=m_na_k_g, na_rpb=m_na_rpb, gla_a2_f=m_gla_a2_f, gla_ab_f=m_gla_ab_f, gla_a2_b=m_gla_a2_b,
             gla_ab_b=m_gla_ab_b, gla_o_g=m_gla_o_g, pool_w=m_pool_w, pool_scale=m_pool_scale, w_out=m_w_out)
    V = dict(norm_g=v_norm_g, w_in=v_w_in, conv_w=v_conv_w, conv_b=v_conv_b, conv_ln_g=v_conv_ln_g, conv_ln_b=v_conv_ln_b,
             na_q_g=v_na_q_g, na_k_g=v_na_k_g, na_rpb=v_na_rpb, gla_a2_f=v_gla_a2_f, gla_ab_f=v_gla_ab_f, gla_a2_b=v_gla_a2_b,
             gla_ab_b=v_gla_ab_b, gla_o_g=v_gla_o_g, pool_w=v_pool_w, pool_scale=v_pool_scale, w_out=v_w_out)
    E, S, D = x.shape
    T = E * S
    L = DEPTH
    xi, yi, _ = _coords()
    chip = 2 * xi + yi
    cw_sh, a2_sh = conv_w.shape[-1], gla_a2_f.shape[-1]

    small_sh = jnp.concatenate([
        jnp.pad(conv_w, ((0, 0), (0, 1), (0, 0))),
        jnp.pad(gla_a2_f, ((0, 0), (0, 0), (0, 128 - a2_sh))),
        jnp.pad(gla_a2_b, ((0, 0), (0, 0), (0, 128 - a2_sh)))], axis=1)
    g_win, g_wout, g_small = _gather_weights(w_in.astype(BF16), w_out.astype(BF16), small_sh, "gather_weights")
    w_in_full = _to_layout(jnp.transpose(g_win, (1, 2, 0, 3)).reshape(L, D, N_IN))
    w_out_full = jnp.transpose(g_wout, (1, 0, 2, 3)).reshape(L, D, D)
    conv_w_full = jnp.transpose(g_small[:, :, 0:32, :], (1, 2, 0, 3)).reshape(L, 32, 4 * cw_sh)
    a2f_full = jnp.transpose(g_small[:, :, 32:48, :a2_sh], (1, 2, 0, 3)).reshape(L, GLA_RANK, 4 * a2_sh)
    a2b_full = jnp.transpose(g_small[:, :, 48:64, :a2_sh], (1, 2, 0, 3)).reshape(L, GLA_RANK, 4 * a2_sh)

    params = []
    for l in range(L):
        params.append(dict(
            norm_g=norm_g[l][None], w_in=w_in_full[l], w_out=w_out_full[l], conv_w32=conv_w_full[l], conv_b=conv_b[l][None],
            conv_ln_g=conv_ln_g[l][None], conv_ln_b=conv_ln_b[l][None], na_q_g=na_q_g[l].reshape(1, GROUP_W),
            na_k_g=na_k_g[l].reshape(1, GROUP_W), na_bias=_na_bias(na_rpb[l]),
            a2_f=jnp.zeros((128, _HK), F32).at[0:GLA_RANK].set(a2f_full[l]),
            a2_b=jnp.zeros((128, _HK), F32).at[GLA_RANK:2 * GLA_RANK].set(a2b_full[l]),
            gla_ab_f=gla_ab_f[l][None], gla_ab_b=gla_ab_b[l][None], gla_o_g=gla_o_g[l].reshape(1, GROUP_W),
            pool_w_bf=pool_w[l].astype(BF16), pool_scale=pool_scale[l][None]))

    act = x.reshape(T, D)
    saved = []
    for l in range(L):
        act, sv = _layer_fwd(l, act, params[l], S)
        saved.append(sv)
    dact, dact_bf, loss_loc = _loss_head(act, loss_target.reshape(T, D), "loss_head")
    loss = lax.psum(loss_loc[0, 0], ("x", "y", "c"))
    grads = [None] * L
    for l in reversed(range(L)):
        dact, dact_bf, grads[l] = _layer_bwd(l, dact, dact_bf, saved[l], params[l], S)
    grad_x = dact.reshape(E, S, D)
    G = {k: jnp.stack([grads[l][k] for l in range(L)]) for k in WEIGHTS}

    p_win = jnp.transpose(_from_layout(G["w_in"]).reshape(L, D, 4, N_IN // 4), (2, 0, 1, 3))
    p_wout = jnp.transpose(G["w_out"].reshape(L, 4, D // 4, D), (1, 0, 2, 3))
    small_names = _REPL + _SHARD_SMALL
    small_g = _pack([G[k] for k in small_names])
    r_win, r_wout, r_small = _grad_exchange(p_win, p_wout, small_g, "grad_exchange")
    rows_in, cols_in = L * D, N_IN // 4
    rows_out, cols_out = L * (D // 4), D
    part_in = _sum_slots(r_win.reshape(4, rows_in, cols_in), "sum_w_in")
    part_out = _sum_slots(r_wout.reshape(4, rows_out, cols_out), "sum_w_out")
    sib_in, sib_out = _sibling_swap(part_in, part_out, "sibling_swap")

    res = {}
    res["w_in"] = [a.reshape(L, D, cols_in) for a in _adamw(
        w_in.reshape(rows_in, cols_in), (part_in, sib_in), m_w_in.reshape(rows_in, cols_in), v_w_in.reshape(rows_in, cols_in),
        256, "adamw_w_in")]
    res["w_out"] = [a.reshape(L, D // 4, D) for a in _adamw(
        w_out.reshape(rows_out, cols_out), (part_out, sib_out), m_w_out.reshape(rows_out, cols_out),
        v_w_out.reshape(rows_out, cols_out), 256, "adamw_w_out")]
    zeros_sh = [jnp.zeros(G[k].shape, F32) for k in _SHARD_SMALL]
    pk = lambda dct: _pack([dct[k] for k in _REPL] + zeros_sh)
    small_res = _adamw(pk(W), (r_small,), pk(M), pk(V), small_g.shape[0], "adamw_small")
    shapes = [G[k].shape for k in small_names]
    unp = [_unpack(a, shapes) for a in small_res]
    for i, k in enumerate(_REPL):
        res[k] = [u[i] for u in unp]
    g_sh = []
    for i, k in enumerate(_SHARD_SMALL):
        gfull = unp[0][len(_REPL) + i]
        wdt = W[k].shape[-1]
        g_sh.append(lax.dynamic_slice_in_dim(gfull, chip * wdt, wdt, axis=2))
    g_sh_p = _pack(g_sh)
    sh_res = _adamw(_pack([W[k] for k in _SHARD_SMALL]), (g_sh_p,), _pack([M[k] for k in _SHARD_SMALL]),
                    _pack([V[k] for k in _SHARD_SMALL]), g_sh_p.shape[0], "adamw_shard_small")
    shapes2 = [W[k].shape for k in _SHARD_SMALL]
    unp2 = [_unpack(a, shapes2) for a in sh_res]
    for i, k in enumerate(_SHARD_SMALL):
        res[k] = [u[i] for u in unp2]

    outs = [loss, grad_x]
    for j in range(4):
        outs += [res[k][j] for k in WEIGHTS]
    return tuple(outs)
```

```python
import functools

import numpy as np
import jax
import jax.numpy as jnp
from jax import lax
from jax.experimental import pallas as pl
from jax.experimental.pallas import tpu as pltpu

F32 = jnp.float32
BF16 = jnp.bfloat16
HI = lax.Precision.HIGHEST
MESH = pl.DeviceIdType.MESH

EPS = 1e-6
D_MODEL = 2048
GROUP_W = 512
SEQ = 2048
DEPTH = 2
N_IN = 6176
GRID_W = 64
CONV_K = 31
NA_HEADS = 8
NA_DH = 64
NA_ROWS = 8
NA_COLS = 16
GLA_HEADS = 4
GLA_DK = 64
GLA_DV = 128
GLA_RANK = 16
GLA_TAU = 16.0
CHUNK = 64
POOL_WINDOWS = (2, 4, 8, 16)
ADAM_LR, ADAM_B1, ADAM_B2, ADAM_EPS, ADAM_WD, ADAM_STEP = 0.001, 0.9, 0.999, 1e-08, 0.01, 10

A_VAL, A_GLU, A_GATE = 0, 512, 1024
B_Q, B_K, B_V, B_GATE = 1536, 2048, 2560, 3072
C_Q, C_K, C_V, C_GATE = 3584, 3840, 4096, 4608
D_VAL, D_GATE = 5120, 5632
LR_OFF = 6144
NZ = 6400
NEG = -1e30
VMEM_LIMIT = 56 * 1024 * 1024


def _cp(sem=None):
    return pltpu.CompilerParams(dimension_semantics=sem, vmem_limit_bytes=VMEM_LIMIT)


def _sigmoid(x):
    return 1.0 / (1.0 + jnp.exp(-x))


def _silu(x):
    return x * _sigmoid(x)


def _dsilu(x):
    s = _sigmoid(x)
    return s * (1.0 + x * (1.0 - s))


def _matmul(a, b, *, dims, out_dtype, tm, tn, tk, name, res=None):
    if dims == "nn":
        (M, K), N = a.shape, b.shape[1]
    elif dims == "nt":
        (M, K), N = a.shape, b.shape[0]
    else:
        (K, M), N = a.shape, b.shape[1]
    tm, tn, tk = min(tm, M), min(tn, N), min(tk, K)
    nk = K // tk
    assert M % tm == 0 and N % tn == 0 and K % tk == 0, (M, N, K, tm, tn, tk)
    dn = {"nn": (((1,), (0,)), ((), ())), "nt": (((1,), (1,)), ((), ())), "tn": (((0,), (0,)), ((), ()))}[dims]
    if dims == "tn":
        a_spec = pl.BlockSpec((tk, tm), lambda i, j, k: (k, i))
    else:
        a_spec = pl.BlockSpec((tm, tk), lambda i, j, k: (i, k))
    if dims == "nt":
        b_spec = pl.BlockSpec((tn, tk), lambda i, j, k: (j, k))
    else:
        b_spec = pl.BlockSpec((tk, tn), lambda i, j, k: (k, j))
    o_spec = pl.BlockSpec((tm, tn), lambda i, j, k: (i, j))
    has_res = res is not None

    def body(*refs):
        if has_res:
            a_ref, b_ref, r_ref, o_ref, acc = refs
        else:
            a_ref, b_ref, o_ref, acc = refs
        k = pl.program_id(2)

        @pl.when(k == 0)
        def _():
            acc[...] = jnp.zeros_like(acc)

        acc[...] += lax.dot_general(a_ref[...], b_ref[...], dn, preferred_element_type=F32)

        @pl.when(k == nk - 1)
        def _():
            r = acc[...]
            if has_res:
                r = r + r_ref[...]
            o_ref[...] = r.astype(o_ref.dtype)

    in_specs = [a_spec, b_spec] + ([o_spec] if has_res else [])
    args = (a, b) + ((res,) if has_res else ())
    return pl.pallas_call(
        body, out_shape=jax.ShapeDtypeStruct((M, N), out_dtype), grid=(M // tm, N // tn, nk),
        in_specs=in_specs, out_specs=o_spec, scratch_shapes=[pltpu.VMEM((tm, tn), F32)],
        name=name, compiler_params=_cp(("parallel", "parallel", "arbitrary")))(*args)


def _rmsnorm_fwd(x, g, name):
    T, D = x.shape
    tm = 256

    def body(x_ref, g_ref, h_ref):
        xv = x_ref[...]
        r = lax.rsqrt(jnp.mean(xv * xv, axis=-1, keepdims=True) + EPS)
        h_ref[...] = (xv * r * g_ref[...]).astype(h_ref.dtype)

    return pl.pallas_call(
        body, out_shape=jax.ShapeDtypeStruct((T, D), BF16), grid=(T // tm,),
        in_specs=[pl.BlockSpec((tm, D), lambda i: (i, 0)), pl.BlockSpec((1, D), lambda i: (0, 0))],
        out_specs=pl.BlockSpec((tm, D), lambda i: (i, 0)), name=name, compiler_params=_cp(("parallel",)))(x, g)


def _rmsnorm_bwd(x, g, dh, dres, name):
    T, D = x.shape
    tm = 256

    def body(x_ref, g_ref, dh_ref, dres_ref, dx_ref, dxb_ref, dg_ref):
        xv = x_ref[...]
        r = lax.rsqrt(jnp.mean(xv * xv, axis=-1, keepdims=True) + EPS)
        xh = xv * r
        dh_v = dh_ref[...]
        dxh = dh_v * g_ref[...]
        dx = r * (dxh - xh * jnp.mean(dxh * xh, axis=-1, keepdims=True)) + dres_ref[...]
        dx_ref[...] = dx
        dxb_ref[...] = dx.astype(BF16)

        @pl.when(pl.program_id(0) == 0)
        def _():
            dg_ref[...] = jnp.zeros_like(dg_ref)

        dg_ref[...] += jnp.sum(dh_v * xh, axis=0, keepdims=True)

    row = pl.BlockSpec((tm, D), lambda i: (i, 0))
    vec = pl.BlockSpec((1, D), lambda i: (0, 0))
    return pl.pallas_call(
        body, out_shape=(jax.ShapeDtypeStruct((T, D), F32), jax.ShapeDtypeStruct((T, D), BF16), jax.ShapeDtypeStruct((1, D), F32)),
        grid=(T // tm,), in_specs=[row, vec, row, row], out_specs=(row, row, vec), name=name,
        compiler_params=_cp(("arbitrary",)))(x, g, dh, dres)


def _loss_head(y, target, name):
    T, D = y.shape
    tm = 256

    def body(y_ref, t_ref, d_ref, db_ref, l_ref):
        e = y_ref[...] - t_ref[...]
        d = e * (1.0 / D)
        d_ref[...] = d
        db_ref[...] = d.astype(BF16)

        @pl.when(pl.program_id(0) == 0)
        def _():
            l_ref[...] = jnp.zeros_like(l_ref)

        row = jnp.sum(e * e, axis=-1, keepdims=True) * (0.5 / D)
        l_ref[...] += jnp.sum(row, axis=0, keepdims=True)

    row = pl.BlockSpec((tm, D), lambda i: (i, 0))
    return pl.pallas_call(
        body, out_shape=(jax.ShapeDtypeStruct((T, D), F32), jax.ShapeDtypeStruct((T, D), BF16),
                         jax.ShapeDtypeStruct((1, 1), F32)), grid=(T // tm,),
        in_specs=[row, row], out_specs=(row, row, pl.BlockSpec((1, 1), lambda i: (0, 0))),
        name=name, compiler_params=_cp(("arbitrary",)))(y, target)


_GATE_COLS = (A_GATE // GROUP_W, B_GATE // GROUP_W, C_GATE // GROUP_W, D_GATE // GROUP_W)


def _gate_fwd(pres, z, name):
    T = z.shape[0]
    tm = 256

    def body(pa, pb, pc, pd, ga, gb, gc, gd, y_ref):
        for n, (p, g) in enumerate(((pa, ga), (pb, gb), (pc, gc), (pd, gd))):
            y_ref[:, n * GROUP_W:(n + 1) * GROUP_W] = (p[...] * _silu(g[...])).astype(BF16)

    pre_spec = pl.BlockSpec((tm, GROUP_W), lambda i: (i, 0))
    gate_specs = [pl.BlockSpec((tm, GROUP_W), functools.partial(lambda i, c: (i, c), c=c)) for c in _GATE_COLS]
    return pl.pallas_call(
        body, out_shape=jax.ShapeDtypeStruct((T, 4 * GROUP_W), BF16), grid=(T // tm,),
        in_specs=[pre_spec] * 4 + gate_specs, out_specs=pl.BlockSpec((tm, 4 * GROUP_W), lambda i: (i, 0)),
        name=name, compiler_params=_cp(("parallel",)))(*pres, z, z, z, z)


def _gate_bwd(dy, pres, z, name):
    T = z.shape[0]
    tm = 256

    def body(dy_ref, pa, pb, pc, pd, ga, gb, gc, gd, dpa, dpb, dpc, dpd, dga, dgb, dgc, dgd):
        for n, (p, g, dp, dg) in enumerate(((pa, ga, dpa, dga), (pb, gb, dpb, dgb), (pc, gc, dpc, dgc), (pd, gd, dpd, dgd))):
            d = dy_ref[:, n * GROUP_W:(n + 1) * GROUP_W]
            gv = g[...]
            dp[...] = d * _silu(gv)
            dg[...] = (d * p[...] * _dsilu(gv)).astype(BF16)

    pre_spec = pl.BlockSpec((tm, GROUP_W), lambda i: (i, 0))
    gate_specs = [pl.BlockSpec((tm, GROUP_W), functools.partial(lambda i, c: (i, c), c=c)) for c in _GATE_COLS]
    outs = tuple([jax.ShapeDtypeStruct((T, GROUP_W), F32)] * 4 + [jax.ShapeDtypeStruct((T, GROUP_W), BF16)] * 4)
    return pl.pallas_call(
        body, out_shape=outs, grid=(T // tm,),
        in_specs=[pl.BlockSpec((tm, 4 * GROUP_W), lambda i: (i, 0))] + [pre_spec] * 4 + gate_specs,
        out_specs=tuple([pre_spec] * 8), name=name, compiler_params=_cp(("parallel",)))(dy, *pres, z, z, z, z)


_PAD = 16
_RC = 256


def _conv_fwd(z, conv_w32, conv_b, S, name):
    T = z.shape[0]
    E = T // S
    LW = 128

    def body(val_ref, glu_ref, w_ref, b_ref, y_ref, upad):
        upad[0:_PAD, :] = jnp.zeros((_PAD, LW), F32)
        upad[_PAD + S:_PAD + S + _PAD, :] = jnp.zeros((_PAD, LW), F32)
        upad[_PAD:_PAD + S, :] = val_ref[...] * _sigmoid(glu_ref[...])
        for r in range(S // _RC):
            acc = jnp.broadcast_to(b_ref[...], (_RC, LW))
            for k in range(CONV_K):
                st = r * _RC + k + 1
                acc = acc + upad[st:st + _RC, :] * w_ref[k:k + 1, :]
            y_ref[r * _RC:(r + 1) * _RC, :] = acc

    return pl.pallas_call(
        body, out_shape=jax.ShapeDtypeStruct((T, GROUP_W), F32), grid=(E, GROUP_W // LW),
        in_specs=[pl.BlockSpec((S, LW), lambda e, j: (e, A_VAL // LW + j)),
                  pl.BlockSpec((S, LW), lambda e, j: (e, A_GLU // LW + j)),
                  pl.BlockSpec((32, LW), lambda e, j: (0, j)),
                  pl.BlockSpec((1, LW), lambda e, j: (0, j))],
        out_specs=pl.BlockSpec((S, LW), lambda e, j: (e, j)),
        scratch_shapes=[pltpu.VMEM((S + 2 * _PAD, LW), F32)],
        name=name, compiler_params=_cp(("parallel", "parallel")))(z, z, conv_w32, conv_b)


def _conv_bwd(z, conv_w32, dyc, S, name):
    T = z.shape[0]
    E = T // S
    LW = 128

    def body(val_ref, glu_ref, w_ref, dy_ref, dval_ref, dglu_ref, dw_ref, db_ref, upad, dpad):
        e = pl.program_id(1)
        zeros = jnp.zeros((_PAD, LW), F32)
        upad[0:_PAD, :] = zeros
        upad[_PAD + S:_PAD + S + _PAD, :] = zeros
        dpad[0:_PAD, :] = zeros
        dpad[_PAD + S:_PAD + S + _PAD, :] = zeros
        upad[_PAD:_PAD + S, :] = val_ref[...] * _sigmoid(glu_ref[...])
        dpad[_PAD:_PAD + S, :] = dy_ref[...]

        @pl.when(e == 0)
        def _():
            dw_ref[...] = jnp.zeros_like(dw_ref)
            db_ref[...] = jnp.zeros_like(db_ref)

        db_ref[...] += jnp.sum(dy_ref[...], axis=0, keepdims=True)
        for r in range(S // _RC):
            dyr = dy_ref[r * _RC:(r + 1) * _RC, :]
            du = jnp.zeros((_RC, LW), F32)
            for k in range(CONV_K):
                st = r * _RC + k + 1
                dw_ref[k:k + 1, :] += jnp.sum(dyr * upad[st:st + _RC, :], axis=0, keepdims=True)
                sd = r * _RC + (CONV_K - 1 - k) + 1
                du = du + dpad[sd:sd + _RC, :] * w_ref[k:k + 1, :]
            sl = slice(r * _RC, (r + 1) * _RC)
            val = val_ref[sl, :]
            sg = _sigmoid(glu_ref[sl, :])
            dval_ref[sl, :] = (du * sg).astype(BF16)
            dglu_ref[sl, :] = (du * val * sg * (1.0 - sg)).astype(BF16)

    blk = pl.BlockSpec((S, LW), lambda j, e: (e, j))
    return pl.pallas_call(
        body, out_shape=(jax.ShapeDtypeStruct((T, GROUP_W), BF16), jax.ShapeDtypeStruct((T, GROUP_W), BF16),
                         jax.ShapeDtypeStruct((32, GROUP_W), F32), jax.ShapeDtypeStruct((1, GROUP_W), F32)),
        grid=(GROUP_W // LW, E),
        in_specs=[pl.BlockSpec((S, LW), lambda j, e: (e, A_VAL // LW + j)),
                  pl.BlockSpec((S, LW), lambda j, e: (e, A_GLU // LW + j)),
                  pl.BlockSpec((32, LW), lambda j, e: (0, j)), blk],
        out_specs=(blk, blk, pl.BlockSpec((32, LW), lambda j, e: (0, j)), pl.BlockSpec((1, LW), lambda j, e: (0, j))),
        scratch_shapes=[pltpu.VMEM((S + 2 * _PAD, LW), F32), pltpu.VMEM((S + 2 * _PAD, LW), F32)],
        name=name, compiler_params=_cp(("parallel", "arbitrary")))(z, z, conv_w32, dyc)


def _ln_silu_fwd(yc, g, b, name):
    T, C = yc.shape
    tm = 256

    def body(y_ref, g_ref, b_ref, o_ref):
        y = y_ref[...]
        mu = jnp.mean(y, axis=-1, keepdims=True)
        yc_ = y - mu
        r = lax.rsqrt(jnp.mean(yc_ * yc_, axis=-1, keepdims=True) + EPS)
        o_ref[...] = _silu(yc_ * r * g_ref[...] + b_ref[...])

    row = pl.BlockSpec((tm, C), lambda i: (i, 0))
    vec = pl.BlockSpec((1, C), lambda i: (0, 0))
    return pl.pallas_call(body, out_shape=jax.ShapeDtypeStruct((T, C), F32), grid=(T // tm,),
                          in_specs=[row, vec, vec], out_specs=row, name=name, compiler_params=_cp(("parallel",)))(yc, g, b)


def _ln_silu_bwd(yc, g, b, dpre, name):
    T, C = yc.shape
    tm = 256

    def body(y_ref, g_ref, b_ref, dp_ref, dy_ref, dg_ref, db_ref):
        y = y_ref[...]
        mu = jnp.mean(y, axis=-1, keepdims=True)
        yc_ = y - mu
        r = lax.rsqrt(jnp.mean(yc_ * yc_, axis=-1, keepdims=True) + EPS)
        xh = yc_ * r
        gv = g_ref[...]
        dln = dp_ref[...] * _dsilu(xh * gv + b_ref[...])
        dxh = dln * gv
        dy_ref[...] = r * (dxh - jnp.mean(dxh, axis=-1, keepdims=True) - xh * jnp.mean(dxh * xh, axis=-1, keepdims=True))

        @pl.when(pl.program_id(0) == 0)
        def _():
            dg_ref[...] = jnp.zeros_like(dg_ref)
            db_ref[...] = jnp.zeros_like(db_ref)

        dg_ref[...] += jnp.sum(dln * xh, axis=0, keepdims=True)
        db_ref[...] += jnp.sum(dln, axis=0, keepdims=True)

    row = pl.BlockSpec((tm, C), lambda i: (i, 0))
    vec = pl.BlockSpec((1, C), lambda i: (0, 0))
    return pl.pallas_call(
        body, out_shape=(jax.ShapeDtypeStruct((T, C), F32), jax.ShapeDtypeStruct((1, C), F32), jax.ShapeDtypeStruct((1, C), F32)),
        grid=(T // tm,), in_specs=[row, vec, vec, row], out_specs=(row, vec, vec), name=name,
        compiler_params=_cp(("arbitrary",)))(yc, g, b, dpre)


def _pool_counts(S, w, rows0, n):
    t = (lax.broadcasted_iota(jnp.int32, (n, 1), 0) + rows0)
    lo = jnp.maximum(t - w // 2, 0)
    hi = jnp.minimum(t + w // 2, S)
    return (hi - lo).astype(F32)


def _pool_fwd(z, pool_w, pool_scale, S, name):
    T = z.shape[0]
    E = T // S
    CG = 128

    def body(u_ref, w_ref, s_ref, o_ref, upad, dif):
        zeros = jnp.zeros((_PAD, GROUP_W), F32)
        upad[0:_PAD, :] = zeros
        upad[_PAD + S:_PAD + S + _PAD, :] = zeros
        upad[_PAD:_PAD + S, :] = u_ref[...]
        for gi, w in enumerate(POOL_WINDOWS):
            ls = slice(gi * CG, (gi + 1) * CG)
            for r in range(S // _RC):
                acc = jnp.zeros((_RC, CG), F32)
                for j in range(-(w // 2), w // 2):
                    st = _PAD + r * _RC + j
                    acc = acc + upad[st:st + _RC, ls]
                cnt = _pool_counts(S, w, r * _RC, _RC)
                dif[r * _RC:(r + 1) * _RC, :] = (acc / cnt - u_ref[r * _RC:(r + 1) * _RC, ls]).astype(BF16)
            yp = jnp.dot(dif[...], w_ref[gi], preferred_element_type=F32)
            o_ref[:, ls] = yp * s_ref[:, ls]

    return pl.pallas_call(
        body, out_shape=jax.ShapeDtypeStruct((T, GROUP_W), F32), grid=(E,),
        in_specs=[pl.BlockSpec((S, GROUP_W), lambda e: (e, D_VAL // GROUP_W)),
                  pl.BlockSpec((4, CG, CG), lambda e: (0, 0, 0)),
                  pl.BlockSpec((1, GROUP_W), lambda e: (0, 0))],
        out_specs=pl.BlockSpec((S, GROUP_W), lambda e: (e, 0)),
        scratch_shapes=[pltpu.VMEM((S + 2 * _PAD, GROUP_W), F32), pltpu.VMEM((S, CG), BF16)],
        name=name, compiler_params=_cp(("parallel",)))(z, pool_w, pool_scale)


def _pool_bwd(z, pool_w, pool_scale, dpre, S, name):
    T = z.shape[0]
    E = T // S
    CG = 128

    def body(u_ref, w_ref, s_ref, dp_ref, du_ref, dw_ref, ds_ref, upad, dif, qpad):
        zeros = jnp.zeros((_PAD, GROUP_W), F32)
        upad[0:_PAD, :] = zeros
        upad[_PAD + S:_PAD + S + _PAD, :] = zeros
        upad[_PAD:_PAD + S, :] = u_ref[...]
        zc = jnp.zeros((_PAD, CG), F32)
        qpad[0:_PAD, :] = zc
        qpad[_PAD + S:_PAD + S + _PAD, :] = zc

        @pl.when(pl.program_id(0) == 0)
        def _():
            dw_ref[...] = jnp.zeros_like(dw_ref)
            ds_ref[...] = jnp.zeros_like(ds_ref)

        for gi, w in enumerate(POOL_WINDOWS):
            ls = slice(gi * CG, (gi + 1) * CG)
            for r in range(S // _RC):
                acc = jnp.zeros((_RC, CG), F32)
                for j in range(-(w // 2), w // 2):
                    st = _PAD + r * _RC + j
                    acc = acc + upad[st:st + _RC, ls]
                cnt = _pool_counts(S, w, r * _RC, _RC)
                dif[r * _RC:(r + 1) * _RC, :] = (acc / cnt - u_ref[r * _RC:(r + 1) * _RC, ls]).astype(BF16)
            dp = dp_ref[:, ls]
            yp = jnp.dot(dif[...], w_ref[gi], preferred_element_type=F32)
            ds_ref[:, ls] += jnp.sum(dp * yp, axis=0, keepdims=True)
            dys = (dp * s_ref[:, ls]).astype(BF16)
            dw_ref[gi] += lax.dot_general(dif[...], dys, (((0,), (0,)), ((), ())), preferred_element_type=F32)
            dm = lax.dot_general(dys, w_ref[gi], (((1,), (1,)), ((), ())), preferred_element_type=F32)
            for r in range(S // _RC):
                cnt = _pool_counts(S, w, r * _RC, _RC)
                qpad[_PAD + r * _RC:_PAD + (r + 1) * _RC, :] = dm[r * _RC:(r + 1) * _RC, :] / cnt
            for r in range(S // _RC):
                acc = -dm[r * _RC:(r + 1) * _RC, :]
                for j in range(-(w // 2) + 1, w // 2 + 1):
                    st = _PAD + r * _RC + j
                    acc = acc + qpad[st:st + _RC, :]
                du_ref[r * _RC:(r + 1) * _RC, ls] = acc.astype(BF16)

    return pl.pallas_call(
        body, out_shape=(jax.ShapeDtypeStruct((T, GROUP_W), BF16), jax.ShapeDtypeStruct((4, CG, CG), F32),
                         jax.ShapeDtypeStruct((1, GROUP_W), F32)), grid=(E,),
        in_specs=[pl.BlockSpec((S, GROUP_W), lambda e: (e, D_VAL // GROUP_W)),
                  pl.BlockSpec((4, CG, CG), lambda e: (0, 0, 0)),
                  pl.BlockSpec((1, GROUP_W), lambda e: (0, 0)),
                  pl.BlockSpec((S, GROUP_W), lambda e: (e, 0))],
        out_specs=(pl.BlockSpec((S, GROUP_W), lambda e: (e, 0)), pl.BlockSpec((4, CG, CG), lambda e: (0, 0, 0)),
                   pl.BlockSpec((1, GROUP_W), lambda e: (0, 0))),
        scratch_shapes=[pltpu.VMEM((S + 2 * _PAD, GROUP_W), F32), pltpu.VMEM((S, CG), BF16),
                        pltpu.VMEM((S + 2 * _PAD, CG), F32)],
        name=name, compiler_params=_cp(("arbitrary",)))(z, pool_w, pool_scale, dpre)


def _na_tables():
    d = np.arange(NA_ROWS)[:, None]
    kr = np.arange(NA_ROWS)[None, :]
    ro = kr - d + (NA_ROWS - 1)
    qc = np.arange(GRID_W)[:, None]
    kc = np.arange(GRID_W)[None, :]
    cs = np.clip(qc - NA_COLS // 2, 0, GRID_W - NA_COLS)
    valid = (kc >= cs) & (kc < cs + NA_COLS)
    co = np.clip(kc - qc + (NA_COLS - 1), 0, 2 * NA_COLS - 2)
    return ro, co, valid


def _na_onehots():
    ro, co, valid = _na_tables()
    e_np = np.zeros((GRID_W, GRID_W, 128), np.float32)
    qi, ki = np.nonzero(valid)
    e_np[qi, ki, co[qi, ki]] = 1.0
    a_np = np.zeros((16, NA_ROWS * NA_ROWS), np.float32)
    a_np[ro.reshape(-1), np.arange(NA_ROWS * NA_ROWS)] = 1.0
    mask = np.where(valid, 0.0, NEG).astype(np.float32).reshape(1, GRID_W * GRID_W)
    return e_np.reshape(GRID_W * GRID_W, 128), a_np, mask


def _na_bias(rpb, name):
    e_np, a_np, mask = _na_onehots()
    H = NA_HEADS
    rp = jnp.pad(rpb, ((0, 0), (0, 1), (0, 128 - rpb.shape[2])))

    def body(r_ref, e_ref, at_ref, m_ref, o_ref):
        t = jnp.dot(at_ref[...], r_ref[0], precision=HI, preferred_element_type=F32)
        o_ref[0] = lax.dot_general(t, e_ref[...], (((1,), (1,)), ((), ())), precision=HI,
                                   preferred_element_type=F32) + m_ref[...]

    out = pl.pallas_call(
        body, out_shape=jax.ShapeDtypeStruct((H, NA_ROWS * NA_ROWS, GRID_W * GRID_W), F32), grid=(H,),
        in_specs=[pl.BlockSpec((1, 16, 128), lambda h: (h, 0, 0)),
                  pl.BlockSpec((GRID_W * GRID_W, 128), lambda h: (0, 0)),
                  pl.BlockSpec((NA_ROWS * NA_ROWS, 16), lambda h: (0, 0)),
                  pl.BlockSpec((1, GRID_W * GRID_W), lambda h: (0, 0))],
        out_specs=pl.BlockSpec((1, NA_ROWS * NA_ROWS, GRID_W * GRID_W), lambda h: (h, 0, 0)),
        name=name, compiler_params=_cp(("parallel",)))(rp, jnp.asarray(e_np), jnp.asarray(a_np.T), jnp.asarray(mask))
    t = out.reshape(H, NA_ROWS, NA_ROWS, GRID_W, GRID_W)
    return jnp.transpose(t, (0, 1, 3, 2, 4)).reshape(H, NA_ROWS, GRID_W, NA_ROWS * GRID_W)


def _seg_mean_matrix(width, seg):
    i = np.arange(width)
    return jnp.asarray((i[:, None] // seg == i[None, :] // seg).astype(np.float32) / seg)


def _na_fwd(z, qg, kg, bias, S, name):
    T = z.shape[0]
    E = T // S
    rows = S // GRID_W
    WIN = NA_ROWS * GRID_W
    seg = _seg_mean_matrix(128, NA_DH)

    def body(q_ref, k_ref, v_ref, qg_ref, kg_ref, bias_ref, seg_ref, o_ref, qs, ks, vs):
        for c in range(S // _RC):
            sl = slice(c * _RC, (c + 1) * _RC)
            q = q_ref[sl, :]
            k = k_ref[sl, :]
            qn = q * lax.rsqrt(jnp.dot(q * q, seg_ref[...], precision=HI, preferred_element_type=F32) + EPS) * qg_ref[...]
            kn = k * lax.rsqrt(jnp.dot(k * k, seg_ref[...], precision=HI, preferred_element_type=F32) + EPS) * kg_ref[...]
            v = v_ref[sl, :]
            for hh in range(2):
                ls = slice(hh * NA_DH, (hh + 1) * NA_DH)
                qs[hh, sl, :] = qn[:, ls].astype(BF16)
                ks[hh, sl, :] = kn[:, ls].astype(BF16)
                vs[hh, sl, :] = v[:, ls].astype(BF16)
        def row(r, carry):
            rs = jnp.clip(r - NA_ROWS // 2, 0, rows - NA_ROWS)
            q0 = pl.multiple_of(r * GRID_W, GRID_W)
            k0 = pl.multiple_of(rs * GRID_W, GRID_W)
            outs = []
            for hh in range(2):
                qr = qs[hh, pl.ds(q0, GRID_W), :]
                kw = ks[hh, pl.ds(k0, WIN), :]
                vw = vs[hh, pl.ds(k0, WIN), :]
                s = lax.dot_general(qr, kw, (((1,), (1,)), ((), ())), preferred_element_type=F32) * (NA_DH ** -0.5)
                s = s + bias_ref[hh, r - rs]
                p = jnp.exp(s - jnp.max(s, axis=-1, keepdims=True))
                l = jnp.sum(p, axis=-1, keepdims=True)
                outs.append(jnp.dot(p.astype(BF16), vw, preferred_element_type=F32) / l)
            o_ref[pl.ds(q0, GRID_W), :] = jnp.concatenate(outs, axis=1)
            return carry
        lax.fori_loop(0, rows, row, 0, unroll=2)

    LW = 128
    return pl.pallas_call(
        body, out_shape=jax.ShapeDtypeStruct((T, GROUP_W), F32), grid=(E, GROUP_W // LW),
        in_specs=[pl.BlockSpec((S, LW), lambda e, j: (e, B_Q // LW + j)),
                  pl.BlockSpec((S, LW), lambda e, j: (e, B_K // LW + j)),
                  pl.BlockSpec((S, LW), lambda e, j: (e, B_V // LW + j)),
                  pl.BlockSpec((1, LW), lambda e, j: (0, j)),
                  pl.BlockSpec((1, LW), lambda e, j: (0, j)),
                  pl.BlockSpec((2, NA_ROWS, GRID_W, WIN), lambda e, j: (j, 0, 0, 0)),
                  pl.BlockSpec((LW, LW), lambda e, j: (0, 0))],
        out_specs=pl.BlockSpec((S, LW), lambda e, j: (e, j)),
        scratch_shapes=[pltpu.VMEM((2, S, NA_DH), BF16)] * 3,
        name=name, compiler_params=_cp(("parallel", "parallel")))(z, z, z, qg, kg, bias, seg)


def _na_bwd(z, qg, kg, bias, do, S, name):
    T = z.shape[0]
    E = T // S
    rows = S // GRID_W
    WIN = NA_ROWS * GRID_W
    seg = _seg_mean_matrix(128, NA_DH)
    SC = NA_DH ** -0.5

    def body(q_ref, k_ref, v_ref, qg_ref, kg_ref, bias_ref, seg_ref, do_ref,
             dq_ref, dk_ref, dv_ref, dbias_ref, dqg_ref, dkg_ref, qs, ks, vs, dos, dqn, dkn, dvs):
        e = pl.program_id(1)

        @pl.when(e == 0)
        def _():
            dbias_ref[...] = jnp.zeros_like(dbias_ref)
            dqg_ref[...] = jnp.zeros_like(dqg_ref)
            dkg_ref[...] = jnp.zeros_like(dkg_ref)

        for c in range(S // _RC):
            sl = slice(c * _RC, (c + 1) * _RC)
            q = q_ref[sl, :]
            k = k_ref[sl, :]
            qn = q * lax.rsqrt(jnp.dot(q * q, seg_ref[...], precision=HI, preferred_element_type=F32) + EPS) * qg_ref[...]
            kn = k * lax.rsqrt(jnp.dot(k * k, seg_ref[...], precision=HI, preferred_element_type=F32) + EPS) * kg_ref[...]
            v = v_ref[sl, :]
            dd = do_ref[sl, :]
            for hh in range(2):
                ls = slice(hh * NA_DH, (hh + 1) * NA_DH)
                qs[hh, sl, :] = qn[:, ls].astype(BF16)
                ks[hh, sl, :] = kn[:, ls].astype(BF16)
                vs[hh, sl, :] = v[:, ls].astype(BF16)
                dos[hh, sl, :] = dd[:, ls].astype(BF16)
            dkn[sl, :] = jnp.zeros((_RC, 128), F32)
            dvs[sl, :] = jnp.zeros((_RC, 128), F32)

        def row(r, carry):
            rs = jnp.clip(r - NA_ROWS // 2, 0, rows - NA_ROWS)
            q0 = pl.multiple_of(r * GRID_W, GRID_W)
            k0 = pl.multiple_of(rs * GRID_W, GRID_W)
            dqs, dks, dvv = [], [], []
            for hh in range(2):
                qr = qs[hh, pl.ds(q0, GRID_W), :]
                dor = dos[hh, pl.ds(q0, GRID_W), :]
                kw = ks[hh, pl.ds(k0, WIN), :]
                vw = vs[hh, pl.ds(k0, WIN), :]
                s = lax.dot_general(qr, kw, (((1,), (1,)), ((), ())), preferred_element_type=F32) * SC
                s = s + bias_ref[hh, r - rs]
                p = jnp.exp(s - jnp.max(s, axis=-1, keepdims=True))
                p = p / jnp.sum(p, axis=-1, keepdims=True)
                dp = lax.dot_general(dor, vw, (((1,), (1,)), ((), ())), preferred_element_type=F32)
                ds = p * (dp - jnp.sum(p * dp, axis=-1, keepdims=True))
                dbias_ref[hh, r - rs] += ds
                dsb = ds.astype(BF16)
                dqs.append(jnp.dot(dsb, kw, preferred_element_type=F32) * SC)
                dks.append(lax.dot_general(dsb, qr, (((0,), (0,)), ((), ())), preferred_element_type=F32) * SC)
                dvv.append(lax.dot_general(p.astype(BF16), dor, (((0,), (0,)), ((), ())), preferred_element_type=F32))
            dqn[pl.ds(q0, GRID_W), :] = jnp.concatenate(dqs, axis=1)
            dkn[pl.ds(k0, WIN), :] += jnp.concatenate(dks, axis=1)
            dvs[pl.ds(k0, WIN), :] += jnp.concatenate(dvv, axis=1)
            return carry
        lax.fori_loop(0, rows, row, 0, unroll=2)

        for c in range(S // _RC):
            sl = slice(c * _RC, (c + 1) * _RC)
            for x_ref, g_ref, dn, dx_ref, dg_ref in ((q_ref, qg_ref, dqn, dq_ref, dqg_ref), (k_ref, kg_ref, dkn, dk_ref, dkg_ref)):
                x = x_ref[sl, :]
                r_ = lax.rsqrt(jnp.dot(x * x, seg_ref[...], precision=HI, preferred_element_type=F32) + EPS)
                xh = x * r_
                d = dn[sl, :]
                dxh = d * g_ref[...]
                mean = jnp.dot(dxh * xh, seg_ref[...], precision=HI, preferred_element_type=F32)
                dx_ref[sl, :] = (r_ * (dxh - xh * mean)).astype(BF16)
                dg_ref[...] += jnp.sum(d * xh, axis=0, keepdims=True)
            dv_ref[sl, :] = dvs[sl, :].astype(BF16)

    LW = 128
    blk = pl.BlockSpec((S, LW), lambda j, e: (e, j))
    vec = pl.BlockSpec((1, LW), lambda j, e: (0, j))
    bsp = pl.BlockSpec((2, NA_ROWS, GRID_W, WIN), lambda j, e: (j, 0, 0, 0))
    return pl.pallas_call(
        body, out_shape=(jax.ShapeDtypeStruct((T, GROUP_W), BF16),) * 3 + (
            jax.ShapeDtypeStruct((NA_HEADS, NA_ROWS, GRID_W, WIN), F32),
            jax.ShapeDtypeStruct((1, GROUP_W), F32), jax.ShapeDtypeStruct((1, GROUP_W), F32)),
        grid=(GROUP_W // LW, E),
        in_specs=[pl.BlockSpec((S, LW), lambda j, e: (e, B_Q // LW + j)),
                  pl.BlockSpec((S, LW), lambda j, e: (e, B_K // LW + j)),
                  pl.BlockSpec((S, LW), lambda j, e: (e, B_V // LW + j)),
                  vec, vec, bsp, pl.BlockSpec((LW, LW), lambda j, e: (0, 0)), blk],
        out_specs=(blk, blk, blk, bsp, vec, vec),
        scratch_shapes=[pltpu.VMEM((2, S, NA_DH), BF16)] * 4 + [pltpu.VMEM((S, LW), F32)] * 3,
        name=name, compiler_params=_cp(("parallel", "arbitrary")))(z, z, z, qg, kg, bias, seg, do)


def _na_rpb_grad(dbias, name):
    e_np, a_np, _ = _na_onehots()
    H = NA_HEADS
    x = dbias.reshape(H, NA_ROWS, GRID_W, NA_ROWS, GRID_W)
    x = jnp.transpose(x, (0, 1, 3, 2, 4)).reshape(H, NA_ROWS * NA_ROWS, GRID_W * GRID_W)

    def body(x_ref, e_ref, a_ref, o_ref):
        y = jnp.dot(x_ref[0], e_ref[...], precision=HI, preferred_element_type=F32)
        o_ref[0] = jnp.dot(a_ref[...], y, precision=HI, preferred_element_type=F32)

    out = pl.pallas_call(
        body, out_shape=jax.ShapeDtypeStruct((H, 16, 128), F32), grid=(H,),
        in_specs=[pl.BlockSpec((1, 64, GRID_W * GRID_W), lambda h: (h, 0, 0)),
                  pl.BlockSpec((GRID_W * GRID_W, 128), lambda h: (0, 0)),
                  pl.BlockSpec((16, 64), lambda h: (0, 0))],
        out_specs=pl.BlockSpec((1, 16, 128), lambda h: (h, 0, 0)),
        name=name, compiler_params=_cp(("parallel",)))(x, jnp.asarray(e_np), jnp.asarray(a_np))
    return out[:, :2 * NA_ROWS - 1, :2 * NA_COLS - 1]


_HK = GLA_HEADS * GLA_DK
_HV = GLA_HEADS * GLA_DV


def _gla_consts(reverse):
    i = np.arange(CHUNK)
    tri = (i[:, None] <= i[None, :]) if reverse else (i[:, None] >= i[None, :])
    j = np.arange(_HK)
    oseg = (j[:, None] // GLA_DK == j[None, :] // GLA_DK)
    return (jnp.asarray(tri.astype(np.float32)), jnp.asarray(tri.T.astype(np.float32)), jnp.asarray(oseg.astype(np.float32), BF16))


def _log_decay(lr, a2, ab):
    zg = jnp.dot(lr, a2, precision=HI, preferred_element_type=F32) + ab
    g = (jnp.minimum(zg, 0.0) - jnp.log(1.0 + jnp.exp(-jnp.abs(zg)))) * (1.0 / GLA_TAU)
    return zg, g


def _dotf(a, b, dn):
    return lax.dot_general(a, b, dn, precision=HI, preferred_element_type=F32)


def _dotb(a, b, dn):
    return lax.dot_general(a.astype(BF16), b.astype(BF16), dn, preferred_element_type=F32)


_COLS = 4


_NN = (((1,), (0,)), ((), ()))
_NT = (((1,), (1,)), ((), ()))
_TN = (((0,), (0,)), ((), ()))


def _gla_fwd(z, a2, ab, S, reverse, name):
    T = z.shape[0]
    E = T // S
    n = S // CHUNK
    tri, _, oseg = _gla_consts(reverse)

    def body(q_ref, k_ref, v_ref, lr_ref, a2_ref, ab_ref, tri_ref, oseg_ref, o_ref, a_ref, st_ref, st, b_s, q_s, k_s):
        @pl.when(pl.program_id(1) == 0)
        def _():
            st[...] = jnp.zeros_like(st)

        q = q_ref[...] * (GLA_DK ** -0.5)
        k = k_ref[...]
        v = v_ref[...]
        _, g = _log_decay(lr_ref[...], a2_ref[...], ab_ref[...])
        b = _dotf(tri_ref[...], g, _NN)
        bl_row = jnp.sum(g, axis=0, keepdims=True)
        bl_col = _dotf(g, jnp.ones((CHUNK, GLA_DV), F32), _TN)
        b_s[...] = b
        q_s[...] = q
        k_s[...] = k
        lane = lax.broadcasted_iota(jnp.int32, (1, _HK), 1) % GLA_DK

        def cols(jj, a):
            ts = []
            for u in range(_COLS):
                j = jj * _COLS + u
                bj = b_s[pl.ds(j, 1), :]
                kj = k_s[pl.ds(j, 1), :]
                ts.append((q_s[...] * jnp.exp(jnp.minimum(b_s[...] - bj, 0.0)) * kj).astype(BF16))
            r = jnp.dot(jnp.concatenate(ts, axis=0), oseg_ref[...], preferred_element_type=F32)
            for u in range(_COLS):
                a = jnp.where(lane == jj * _COLS + u, r[u * CHUNK:(u + 1) * CHUNK, :], a)
            return a

        a = lax.fori_loop(0, CHUNK // _COLS, cols, jnp.zeros((CHUNK, _HK), F32))
        rowi = lax.broadcasted_iota(jnp.int32, (CHUNK, 1), 0)
        keep = (rowi <= lane) if reverse else (rowi >= lane)
        a = jnp.where(keep, a, 0.0)
        a_ref[...] = a
        st_ref[0] = st[...]
        qb = q * jnp.exp(b)
        kd = k * jnp.exp(bl_row - b)
        for h in range(GLA_HEADS):
            ks_ = slice(h * GLA_DK, (h + 1) * GLA_DK)
            vs_ = slice(h * GLA_DV, (h + 1) * GLA_DV)
            s_h = st[ks_, :]
            o_ref[:, vs_] = _dotb(qb[:, ks_], s_h, _NN) + _dotb(a[:, ks_], v[:, vs_], _NN)
            st[ks_, :] = s_h * jnp.exp(bl_col[ks_, :]) + _dotb(kd[:, ks_], v[:, vs_], _TN)

    def rowblk(e, c):
        return e * n + ((n - 1 - c) if reverse else c)

    return pl.pallas_call(
        body, out_shape=(jax.ShapeDtypeStruct((T, _HV), F32), jax.ShapeDtypeStruct((T, _HK), F32),
                         jax.ShapeDtypeStruct((T // CHUNK, _HK, GLA_DV), F32)),
        grid=(E, n),
        in_specs=[pl.BlockSpec((CHUNK, _HK), lambda e, c: (rowblk(e, c), C_Q // _HK)),
                  pl.BlockSpec((CHUNK, _HK), lambda e, c: (rowblk(e, c), C_K // _HK)),
                  pl.BlockSpec((CHUNK, _HV), lambda e, c: (rowblk(e, c), C_V // _HV)),
                  pl.BlockSpec((CHUNK, 128), lambda e, c: (rowblk(e, c), LR_OFF // 128)),
                  pl.BlockSpec((128, _HK), lambda e, c: (0, 0)),
                  pl.BlockSpec((1, _HK), lambda e, c: (0, 0)),
                  pl.BlockSpec((CHUNK, CHUNK), lambda e, c: (0, 0)),
                  pl.BlockSpec((_HK, _HK), lambda e, c: (0, 0))],
        out_specs=(pl.BlockSpec((CHUNK, _HV), lambda e, c: (rowblk(e, c), 0)),
                   pl.BlockSpec((CHUNK, _HK), lambda e, c: (rowblk(e, c), 0)),
                   pl.BlockSpec((1, _HK, GLA_DV), lambda e, c: (rowblk(e, c), 0, 0))),
        scratch_shapes=[pltpu.VMEM((_HK, GLA_DV), F32)] + [pltpu.VMEM((CHUNK, _HK), F32)] * 3,
        name=name, compiler_params=_cp(("parallel", "arbitrary")))(z, z, z, z, a2, ab, tri, oseg)


def _gla_bwd(z, a2, ab, att, states, do, prev, S, reverse, name):
    T = z.shape[0]
    E = T // S
    n = S // CHUNK
    tri, tri_t, oseg = _gla_consts(reverse)
    has_prev = prev is not None
    odt = BF16 if has_prev else F32

    def body(*refs):
        (q_ref, k_ref, v_ref, lr_ref, a2_ref, ab_ref, tri_ref, trit_ref, oseg_ref, att_ref, st_ref, do_ref) = refs[:12]
        refs = refs[12:]
        if has_prev:
            pq_ref, pk_ref, pv_ref, pl_ref = refs[:4]
            refs = refs[4:]
        (dq_ref, dk_ref, dv_ref, dlr_ref, da2_ref, dab_ref, dst, b_s, q_s, k_s, da_s, dqb_s, dkd_s, dk3_s, dbn_s, dsp_s) = refs
        first = (pl.program_id(0) == 0) & (pl.program_id(1) == 0)

        @pl.when(first)
        def _():
            da2_ref[...] = jnp.zeros_like(da2_ref)
            dab_ref[...] = jnp.zeros_like(dab_ref)

        @pl.when(pl.program_id(1) == 0)
        def _():
            dst[...] = jnp.zeros_like(dst)

        q = q_ref[...] * (GLA_DK ** -0.5)
        k = k_ref[...]
        v = v_ref[...]
        lr = lr_ref[...]
        zg, g = _log_decay(lr, a2_ref[...], ab_ref[...])
        b = _dotf(tri_ref[...], g, _NN)
        bl_row = jnp.sum(g, axis=0, keepdims=True)
        bl_col = _dotf(g, jnp.ones((CHUNK, GLA_DV), F32), _TN)
        eb = jnp.exp(b)
        ekd = jnp.exp(bl_row - b)
        qb = q * eb
        kd = k * ekd
        b_s[...] = b
        q_s[...] = q
        k_s[...] = k
        att = att_ref[...]
        s_all = st_ref[0]
        dsn = dst[...]
        e_col = jnp.exp(bl_col)
        do = do_ref[...]
        lane = lax.broadcasted_iota(jnp.int32, (1, _HK), 1) % GLA_DK
        rowi = lax.broadcasted_iota(jnp.int32, (CHUNK, 1), 0)
        keep = (rowi <= lane) if reverse else (rowi >= lane)
        for h in range(GLA_HEADS):
            ks_ = slice(h * GLA_DK, (h + 1) * GLA_DK)
            vs_ = slice(h * GLA_DV, (h + 1) * GLA_DV)
            do_h = do[:, vs_]
            s_h = s_all[ks_, :]
            dsn_h = dsn[ks_, :]
            dqb_s[:, ks_] = _dotb(do_h, s_h, _NT)
            dsp_s[ks_, :] = _dotb(qb[:, ks_], do_h, _TN) + dsn_h * e_col[ks_, :]
            da_s[:, ks_] = _dotb(do_h, v[:, vs_], _NT)
            dv_h = _dotb(att[:, ks_], do_h, _TN) + _dotb(kd[:, ks_], dsn_h, _NN)
            if has_prev:
                dv_h = dv_h + pv_ref[:, vs_]
            dv_ref[:, vs_] = dv_h.astype(odt)
            dkd_s[:, ks_] = _dotb(v[:, vs_], dsn_h, _NT)
        da_s[...] = jnp.where(keep, da_s[...], 0.0)
        dqb = dqb_s[...]
        dkd = dkd_s[...]
        x = dsn * s_all * e_col
        dbl_row = _dotf(jnp.ones((8, GLA_DV), F32), x, _NT)[0:1, :] + jnp.sum(dkd * kd, axis=0, keepdims=True)

        def cols(jj, carry):
            dq3, db3 = carry
            sel = [jnp.where(lane == jj * _COLS + u, da_s[...], 0.0).astype(BF16) for u in range(_COLS)]
            dcols = jnp.dot(jnp.concatenate(sel, axis=0), oseg_ref[...], preferred_element_type=F32)
            for u in range(_COLS):
                j = jj * _COLS + u
                bj = b_s[pl.ds(j, 1), :]
                kj = k_s[pl.ds(j, 1), :]
                tm_ = dcols[u * CHUNK:(u + 1) * CHUNK, :] * jnp.exp(jnp.minimum(b_s[...] - bj, 0.0))
                dq3 = dq3 + tm_ * kj
                gq = tm_ * q_s[...]
                dk3_s[pl.ds(j, 1), :] = jnp.sum(gq, axis=0, keepdims=True)
                w = gq * kj
                dbn_s[pl.ds(j, 1), :] = jnp.sum(w, axis=0, keepdims=True)
                db3 = db3 + w
            return dq3, db3

        zero = jnp.zeros((CHUNK, _HK), F32)
        dq3, db3 = lax.fori_loop(0, CHUNK // _COLS, cols, (zero, zero))
        dq = (dqb * eb + dq3) * (GLA_DK ** -0.5)
        dk = dkd * ekd + dk3_s[...]
        db = dqb * qb - dkd * kd + db3 - dbn_s[...]
        dg = _dotf(trit_ref[...], db, _NN) + dbl_row
        sneg = 1.0 / (1.0 + jnp.exp(zg))
        dzg = dg * sneg * (1.0 / GLA_TAU)
        dlr = _dotf(dzg, a2_ref[...], _NT)
        da2_ref[...] += _dotf(lr, dzg, _TN)
        dab_ref[...] += jnp.sum(dzg, axis=0, keepdims=True)
        if has_prev:
            dq = dq + pq_ref[...]
            dk = dk + pk_ref[...]
            dlr = dlr + pl_ref[...]
        dq_ref[...] = dq.astype(odt)
        dk_ref[...] = dk.astype(odt)
        dlr_ref[...] = dlr.astype(odt)
        dst[...] = dsp_s[...]

    def rowblk(e, c):
        return e * n + (c if reverse else (n - 1 - c))

    hk = pl.BlockSpec((CHUNK, _HK), lambda e, c: (rowblk(e, c), 0))
    hv = pl.BlockSpec((CHUNK, _HV), lambda e, c: (rowblk(e, c), 0))
    l128 = pl.BlockSpec((CHUNK, 128), lambda e, c: (rowblk(e, c), 0))
    in_specs = [pl.BlockSpec((CHUNK, _HK), lambda e, c: (rowblk(e, c), C_Q // _HK)),
                pl.BlockSpec((CHUNK, _HK), lambda e, c: (rowblk(e, c), C_K // _HK)),
                pl.BlockSpec((CHUNK, _HV), lambda e, c: (rowblk(e, c), C_V // _HV)),
                pl.BlockSpec((CHUNK, 128), lambda e, c: (rowblk(e, c), LR_OFF // 128)),
                pl.BlockSpec((128, _HK), lambda e, c: (0, 0)),
                pl.BlockSpec((1, _HK), lambda e, c: (0, 0)),
                pl.BlockSpec((CHUNK, CHUNK), lambda e, c: (0, 0)),
                pl.BlockSpec((CHUNK, CHUNK), lambda e, c: (0, 0)),
                pl.BlockSpec((_HK, _HK), lambda e, c: (0, 0)),
                hk, pl.BlockSpec((1, _HK, GLA_DV), lambda e, c: (rowblk(e, c), 0, 0)), hv]
    args = [z, z, z, z, a2, ab, tri, tri_t, oseg, att, states, do]
    if has_prev:
        in_specs += [hk, hk, hv, l128]
        args += list(prev)
    return pl.pallas_call(
        body, out_shape=(jax.ShapeDtypeStruct((T, _HK), odt), jax.ShapeDtypeStruct((T, _HK), odt),
                         jax.ShapeDtypeStruct((T, _HV), odt), jax.ShapeDtypeStruct((T, 128), odt),
                         jax.ShapeDtypeStruct((128, _HK), F32), jax.ShapeDtypeStruct((1, _HK), F32)),
        grid=(E, n), in_specs=in_specs,
        out_specs=(hk, hk, hv, l128, pl.BlockSpec((128, _HK), lambda e, c: (0, 0)), pl.BlockSpec((1, _HK), lambda e, c: (0, 0))),
        scratch_shapes=[pltpu.VMEM((_HK, GLA_DV), F32)] + [pltpu.VMEM((CHUNK, _HK), F32)] * 8 + [pltpu.VMEM((_HK, GLA_DV), F32)],
        name=name, compiler_params=_cp(("arbitrary", "arbitrary")))(*args)


def _gla_norm_fwd(of, ob, og, name):
    T = of.shape[0]
    tm = 256

    def body(f_ref, b_ref, g_ref, o_ref):
        for h in range(GLA_HEADS):
            vs_ = slice(h * GLA_DV, (h + 1) * GLA_DV)
            o = f_ref[:, vs_] + b_ref[:, vs_]
            o_ref[:, vs_] = o * lax.rsqrt(jnp.mean(o * o, axis=-1, keepdims=True) + EPS) * g_ref[:, vs_]

    row = pl.BlockSpec((tm, _HV), lambda i: (i, 0))
    vec = pl.BlockSpec((1, _HV), lambda i: (0, 0))
    return pl.pallas_call(body, out_shape=jax.ShapeDtypeStruct((T, _HV), F32), grid=(T // tm,),
                          in_specs=[row, row, vec], out_specs=row, name=name, compiler_params=_cp(("parallel",)))(of, ob, og)


def _gla_norm_bwd(of, ob, og, dpre, name):
    T = of.shape[0]
    tm = 256

    def body(f_ref, b_ref, g_ref, dp_ref, do_ref, dg_ref):
        @pl.when(pl.program_id(0) == 0)
        def _():
            dg_ref[...] = jnp.zeros_like(dg_ref)

        for h in range(GLA_HEADS):
            vs_ = slice(h * GLA_DV, (h + 1) * GLA_DV)
            o = f_ref[:, vs_] + b_ref[:, vs_]
            r = lax.rsqrt(jnp.mean(o * o, axis=-1, keepdims=True) + EPS)
            xh = o * r
            dp = dp_ref[:, vs_]
            dxh = dp * g_ref[:, vs_]
            do_ref[:, vs_] = r * (dxh - xh * jnp.mean(dxh * xh, axis=-1, keepdims=True))
            dg_ref[:, vs_] += jnp.sum(dp * xh, axis=0, keepdims=True)

    row = pl.BlockSpec((tm, _HV), lambda i: (i, 0))
    vec = pl.BlockSpec((1, _HV), lambda i: (0, 0))
    return pl.pallas_call(
        body, out_shape=(jax.ShapeDtypeStruct((T, _HV), F32), jax.ShapeDtypeStruct((1, _HV), F32)), grid=(T // tm,),
        in_specs=[row, row, vec, row], out_specs=(row, vec), name=name, compiler_params=_cp(("arbitrary",)))(of, ob, og, dpre)


_ANY = pl.BlockSpec(memory_space=pl.ANY)


def _coords():
    return lax.axis_index("x"), lax.axis_index("y"), lax.axis_index("c")


def _other_chips(x, y):
    return ((1 - x, y), (x, 1 - y), (1 - x, 1 - y))


def _gather_weights(arrays, name):
    n = len(arrays)

    def body(*refs):
        srcs, dsts = refs[:n], refs[n:2 * n]
        send_sems, recv_sems, local_sems = refs[2 * n:]
        x, y, c = _coords()
        me = 2 * x + y
        loc = [pltpu.make_async_copy(s, d.at[me], local_sems.at[i]) for i, (s, d) in enumerate(zip(srcs, dsts))]
        for cp in loc:
            cp.start()
        ici = []
        for j, (px, py) in enumerate(_other_chips(x, y)):
            for i, (s, d) in enumerate(zip(srcs, dsts)):
                ici.append(pltpu.make_async_remote_copy(
                    src_ref=s.at[c], dst_ref=d.at[me, c], send_sem=send_sems.at[n * j + i], recv_sem=recv_sems.at[n * j + i],
                    device_id=(px, py, c), device_id_type=MESH))
        for cp in ici:
            cp.start()
        fwd = []
        for j, (px, py) in enumerate(_other_chips(x, y)):
            for i, d in enumerate(dsts):
                ici[n * j + i].wait_recv()
                half = d.at[2 * px + py, c]
                cp = pltpu.make_async_remote_copy(
                    src_ref=half, dst_ref=half, send_sem=send_sems.at[3 * n + n * j + i], recv_sem=recv_sems.at[3 * n + n * j + i],
                    device_id=(x, y, 1 - c), device_id_type=MESH)
                cp.start()
                fwd.append(cp)
        for cp in fwd:
            cp.wait_recv()
        for cp in ici + fwd:
            cp.wait_send()
        for cp in loc:
            cp.wait()

    return pl.pallas_call(
        body, out_shape=tuple(jax.ShapeDtypeStruct((4,) + a.shape, a.dtype) for a in arrays),
        in_specs=[_ANY] * n, out_specs=(_ANY,) * n,
        scratch_shapes=[pltpu.SemaphoreType.DMA((6 * n,)), pltpu.SemaphoreType.DMA((6 * n,)), pltpu.SemaphoreType.DMA((n,))],
        name=name)(*arrays)


def _sibling_exchange(layered, whole, name):
    nl, n = len(layered), len(layered) + len(whole)

    def body(*refs):
        srcs, dsts = refs[:n], refs[n:2 * n]
        send_sems, recv_sems = refs[2 * n:]
        x, y, c = _coords()
        rem = [pltpu.make_async_remote_copy(src_ref=(s.at[1 - c] if i < nl else s), dst_ref=d, send_sem=send_sems.at[i],
                                            recv_sem=recv_sems.at[i], device_id=(x, y, 1 - c), device_id_type=MESH)
               for i, (s, d) in enumerate(zip(srcs, dsts))]
        for cp in rem:
            cp.start()
        for cp in rem:
            cp.wait()

    outs = [jax.ShapeDtypeStruct(a.shape[1:], a.dtype) for a in layered] + [jax.ShapeDtypeStruct(a.shape, a.dtype) for a in whole]
    return pl.pallas_call(
        body, out_shape=tuple(outs), in_specs=[_ANY] * n, out_specs=(_ANY,) * n,
        scratch_shapes=[pltpu.SemaphoreType.DMA((n,)), pltpu.SemaphoreType.DMA((n,))], name=name)(*layered, *whole)


def _chip_exchange(scatter, bcast, name):
    ns, n = len(scatter), len(scatter) + len(bcast)

    def body(*refs):
        srcs, dsts = refs[:n], refs[n:2 * n]
        send_sems, recv_sems, local_sems = refs[2 * n:]
        x, y, c = _coords()
        me = 2 * x + y
        loc = [pltpu.make_async_copy((s.at[me] if i < ns else s), d.at[me], local_sems.at[i])
               for i, (s, d) in enumerate(zip(srcs, dsts))]
        for cp in loc:
            cp.start()
        rem = []
        for j, (px, py) in enumerate(_other_chips(x, y)):
            for i, (s, d) in enumerate(zip(srcs, dsts)):
                rem.append(pltpu.make_async_remote_copy(
                    src_ref=(s.at[2 * px + py] if i < ns else s), dst_ref=d.at[me], send_sem=send_sems.at[n * j + i],
                    recv_sem=recv_sems.at[n * j + i], device_id=(px, py, c), device_id_type=MESH))
        for cp in rem:
            cp.start()
        for cp in rem:
            cp.wait()
        for cp in loc:
            cp.wait()

    outs = [jax.ShapeDtypeStruct(a.shape, a.dtype) for a in scatter] + [jax.ShapeDtypeStruct((4,) + a.shape, a.dtype) for a in bcast]
    return pl.pallas_call(
        body, out_shape=tuple(outs), in_specs=[_ANY] * n, out_specs=(_ANY,) * n,
        scratch_shapes=[pltpu.SemaphoreType.DMA((3 * n,)), pltpu.SemaphoreType.DMA((3 * n,)), pltpu.SemaphoreType.DMA((n,))],
        name=name)(*scatter, *bcast)


def _sum_slots(r, name):
    n, R, C = r.shape
    tm = min(256, R)

    def body(r_ref, o_ref):
        acc = r_ref[0].astype(F32)
        for i in range(1, n):
            acc = acc + r_ref[i].astype(F32)
        o_ref[...] = acc

    return pl.pallas_call(body, out_shape=jax.ShapeDtypeStruct((R, C), F32), grid=(R // tm,),
                          in_specs=[pl.BlockSpec((n, tm, C), lambda i: (0, i, 0))], out_specs=pl.BlockSpec((tm, C), lambda i: (i, 0)),
                          name=name, compiler_params=_cp(("parallel",)))(r)


def _add2(a, b, out_dtype, tm, name):
    R, C = a.shape
    tm = min(tm, R)

    def body(a_ref, b_ref, o_ref):
        o_ref[...] = (a_ref[...].astype(F32) + b_ref[...].astype(F32)).astype(out_dtype)

    blk = pl.BlockSpec((tm, C), lambda i: (i, 0))
    return pl.pallas_call(body, out_shape=jax.ShapeDtypeStruct((R, C), out_dtype), grid=(R // tm,), in_specs=[blk, blk],
                          out_specs=blk, name=name, compiler_params=_cp(("parallel",)))(a, b)


def _adamw_math(w, g, m, v):
    m = ADAM_B1 * m + (1.0 - ADAM_B1) * g
    v = ADAM_B2 * v + (1.0 - ADAM_B2) * (g * g)
    m_hat = m / (1.0 - ADAM_B1 ** ADAM_STEP)
    v_hat = v / (1.0 - ADAM_B2 ** ADAM_STEP)
    delta = -ADAM_LR * (m_hat / (jnp.sqrt(v_hat) + ADAM_EPS) + ADAM_WD * w)
    return delta, m, v


def _adamw(w, gs, m, v, tm, name):
    R, C = w.shape

    def body(*refs):
        w_ref = refs[0]
        g_refs = refs[1:1 + len(gs)]
        m_ref, v_ref, g_out, d_out, m_out, v_out = refs[1 + len(gs):]
        g = None
        for gr in g_refs:
            parts = [gr[i] for i in range(gr.shape[0])] if len(gr.shape) == 3 else [gr[...]]
            for p in parts:
                g = p if g is None else g + p
        d, mn, vn = _adamw_math(w_ref[...], g, m_ref[...], v_ref[...])
        g_out[...] = g
        d_out[...] = d
        m_out[...] = mn
        v_out[...] = vn

    blk = pl.BlockSpec((tm, C), lambda i: (i, 0))
    g_specs = [pl.BlockSpec((g.shape[0], tm, C), lambda i: (0, i, 0)) if g.ndim == 3 else blk for g in gs]
    return pl.pallas_call(
        body, out_shape=tuple(jax.ShapeDtypeStruct((R, C), F32) for _ in range(4)), grid=(R // tm,),
        in_specs=[blk] + g_specs + [blk, blk], out_specs=(blk,) * 4, name=name,
        compiler_params=_cp(("parallel",)))(w, *gs, m, v)


WEIGHTS = ("norm_g", "w_in", "conv_w", "conv_b", "conv_ln_g", "conv_ln_b", "na_q_g", "na_k_g", "na_rpb", "gla_a2_f",
           "gla_ab_f", "gla_a2_b", "gla_ab_b", "gla_o_g", "pool_w", "pool_scale", "w_out")
_REPL = ("norm_g", "conv_b", "conv_ln_g", "conv_ln_b", "na_q_g", "na_k_g", "na_rpb", "gla_ab_f", "gla_ab_b", "gla_o_g",
         "pool_w", "pool_scale")
_SHARD_SMALL = ("conv_w", "gla_a2_f", "gla_a2_b")
_PACK_ROWS = 8 * 128


def _pack(arrs):
    flat = jnp.concatenate([a.reshape(-1) for a in arrs])
    n = -(-flat.shape[0] // _PACK_ROWS) * _PACK_ROWS
    return jnp.pad(flat, (0, n - flat.shape[0])).reshape(-1, 128)


def _unpack(p, shapes):
    flat = p.reshape(-1)
    out, o = [], 0
    for s in shapes:
        n = int(np.prod(s))
        out.append(flat[o:o + n].reshape(s))
        o += n
    return out


def _to_layout(w):
    pad = jnp.zeros(w.shape[:-1] + (NZ - N_IN,), w.dtype)
    return jnp.concatenate([w[..., :5120], w[..., 5152:6176], w[..., 5120:5152], pad], axis=-1)


def _from_layout(w):
    return jnp.concatenate([w[..., :5120], w[..., LR_OFF:LR_OFF + 32], w[..., 5120:LR_OFF]], axis=-1)


def _reduce_gradients(p_a, p_b, small_g, ci):
    two = lambda a: a.reshape(-1, a.shape[-1])
    s_a, s_b, s_small = _sibling_exchange((p_a, p_b), (small_g,), "grad_to_sibling")
    mine = lambda a: lax.dynamic_index_in_dim(a, ci, 0, keepdims=False)
    c_a = _add2(two(mine(p_a)), two(s_a), BF16, 256, "chip_sum_a").reshape(s_a.shape)
    c_b = _add2(two(mine(p_b)), two(s_b), BF16, 256, "chip_sum_b").reshape(s_b.shape)
    c_small = _add2(small_g, s_small, F32, small_g.shape[0], "chip_sum_small")
    r_a, r_b, r_small = _chip_exchange((c_a, c_b), (c_small,), "grad_to_owner")
    own_a = _sum_slots(r_a, "sum_a")
    own_b = _sum_slots(r_b, "sum_b")
    sib_a, sib_b = _sibling_exchange((), (own_a, own_b), "reduced_to_sibling")
    by_layer = lambda own, sib: jnp.where(ci == 0, jnp.stack([own, sib]), jnp.stack([sib, own]))
    return by_layer(own_a, sib_a), by_layer(own_b, sib_b), r_small


def _layer_fwd(l, x, P, S):
    n = f"l{l}_"
    h = _rmsnorm_fwd(x, P["norm_g"], n + "rms_fwd")
    z = _matmul(h, P["w_in"], dims="nn", out_dtype=F32, tm=512, tn=1280, tk=D_MODEL, name=n + "mm_z")
    yc = _conv_fwd(z, P["conv_w32"], P["conv_b"], S, n + "conv_fwd")
    pre_a = _ln_silu_fwd(yc, P["conv_ln_g"], P["conv_ln_b"], n + "ln_fwd")
    pre_b = _na_fwd(z, P["na_q_g"], P["na_k_g"], P["na_bias"], S, n + "na_fwd")
    of, af, sf = _gla_fwd(z, P["a2_f"], P["gla_ab_f"], S, False, n + "gla_fwd_f")
    ob, ab, sb = _gla_fwd(z, P["a2_b"], P["gla_ab_b"], S, True, n + "gla_fwd_b")
    pre_c = _gla_norm_fwd(of, ob, P["gla_o_g"], n + "gla_norm_fwd")
    pre_d = _pool_fwd(z, P["pool_w_bf"], P["pool_scale"], S, n + "pool_fwd")
    pres = (pre_a, pre_b, pre_c, pre_d)
    y = _gate_fwd(pres, z, n + "gate_fwd")
    out = _matmul(y, P["w_out"], dims="nn", out_dtype=F32, tm=512, tn=1024, tk=D_MODEL, name=n + "mm_out", res=x)
    return out, dict(x=x, h=h, z=z, yc=yc, pres=pres, of=of, af=af, sf=sf, ob=ob, ab=ab, sb=sb, y=y)


def _layer_bwd(l, dout, dout_bf, sv, P, S):
    n = f"l{l}_"
    z = sv["z"]
    T = z.shape[0]
    dy = _matmul(dout_bf, P["w_out"], dims="nt", out_dtype=F32, tm=512, tn=1024, tk=D_MODEL, name=n + "mm_dy")
    d_w_out = _matmul(sv["y"], dout_bf, dims="tn", out_dtype=BF16, tm=512, tn=1024, tk=1024, name=n + "mm_dwout")
    dpa, dpb, dpc, dpd, dga, dgb, dgc, dgd = _gate_bwd(dy, sv["pres"], z, n + "gate_bwd")
    dyc, d_ln_g, d_ln_b = _ln_silu_bwd(sv["yc"], P["conv_ln_g"], P["conv_ln_b"], dpa, n + "ln_bwd")
    dval, dglu, d_cw, d_cb = _conv_bwd(z, P["conv_w32"], dyc, S, n + "conv_bwd")
    dq, dk, dv, dbias, d_qg, d_kg = _na_bwd(z, P["na_q_g"], P["na_k_g"], P["na_bias"], dpb, S, n + "na_bwd")
    d_rpb = _na_rpb_grad(dbias, n + "na_rpb")
    do, d_og = _gla_norm_bwd(sv["of"], sv["ob"], P["gla_o_g"], dpc, n + "gla_norm_bwd")
    part = _gla_bwd(z, P["a2_f"], P["gla_ab_f"], sv["af"], sv["sf"], do, None, S, False, n + "gla_bwd_f")
    d_a2f, d_abf = part[4], part[5]
    dcq, dck, dcv, dlr, d_a2b, d_abb = _gla_bwd(z, P["a2_b"], P["gla_ab_b"], sv["ab"], sv["sb"], do, part[:4], S, True,
                                                n + "gla_bwd_b")
    dd, d_pw, d_ps = _pool_bwd(z, P["pool_w_bf"], P["pool_scale"], dpd, S, n + "pool_bwd")
    dz = jnp.concatenate([dval, dglu, dga, dq, dk, dv, dgb, dcq, dck, dcv, dgc, dd, dgd, dlr,
                          jnp.zeros((T, NZ - LR_OFF - 128), BF16)], axis=1)
    dh = _matmul(dz, P["w_in"], dims="nt", out_dtype=F32, tm=512, tn=1024, tk=1280, name=n + "mm_dh")
    d_w_in = _matmul(sv["h"], dz, dims="tn", out_dtype=BF16, tm=512, tn=1280, tk=1024, name=n + "mm_dwin")
    dx, dx_bf, d_ng = _rmsnorm_bwd(sv["x"], P["norm_g"], dh, dout, n + "rms_bwd")
    grads = dict(norm_g=d_ng[0], w_in=d_w_in, conv_w=d_cw[:CONV_K], conv_b=d_cb[0], conv_ln_g=d_ln_g[0], conv_ln_b=d_ln_b[0],
                 na_q_g=d_qg.reshape(NA_HEADS, NA_DH), na_k_g=d_kg.reshape(NA_HEADS, NA_DH), na_rpb=d_rpb,
                 gla_a2_f=d_a2f[0:GLA_RANK], gla_ab_f=d_abf[0], gla_a2_b=d_a2b[GLA_RANK:2 * GLA_RANK], gla_ab_b=d_abb[0],
                 gla_o_g=d_og.reshape(GLA_HEADS, GLA_DV), pool_w=d_pw, pool_scale=d_ps[0], w_out=d_w_out)
    return dx, dx_bf, grads


def kernel(x, norm_g, w_in, conv_w, conv_b, conv_ln_g, conv_ln_b, na_q_g, na_k_g, na_rpb, gla_a2_f, gla_ab_f, gla_a2_b, gla_ab_b, gla_o_g, pool_w, pool_scale, w_out, loss_target, m_norm_g, m_w_in, m_conv_w, m_conv_b, m_conv_ln_g, m_conv_ln_b, m_na_q_g, m_na_k_g, m_na_rpb, m_gla_a2_f, m_gla_ab_f, m_gla_a2_b, m_gla_ab_b, m_gla_o_g, m_pool_w, m_pool_scale, m_w_out, v_norm_g, v_w_in, v_conv_w, v_conv_b, v_conv_ln_g, v_conv_ln_b, v_na_q_g, v_na_k_g, v_na_rpb, v_gla_a2_f, v_gla_ab_f, v_gla_a2_b, v_gla_ab_b, v_gla_o_g, v_pool_w, v_pool_scale, v_w_out):
    W = dict(norm_g=norm_g, w_in=w_in, conv_w=conv_w, conv_b=conv_b, conv_ln_g=conv_ln_g, conv_ln_b=conv_ln_b, na_q_g=na_q_g,
             na_k_g=na_k_g, na_rpb=na_rpb, gla_a2_f=gla_a2_f, gla_ab_f=gla_ab_f, gla_a2_b=gla_a2_b, gla_ab_b=gla_ab_b,
             gla_o_g=gla_o_g, pool_w=pool_w, pool_scale=pool_scale, w_out=w_out)
    M = dict(norm_g=m_norm_g, w_in=m_w_in, conv_w=m_conv_w, conv_b=m_conv_b, conv_ln_g=m_conv_ln_g, conv_ln_b=m_conv_ln_b,
             na_q_g=m_na_q_g, na_k_g=m_na_k_g, na_rpb=m_na_rpb, gla_a2_f=m_gla_a2_f, gla_ab_f=m_gla_ab_f, gla_a2_b=m_gla_a2_b,
             gla_ab_b=m_gla_ab_b, gla_o_g=m_gla_o_g, pool_w=m_pool_w, pool_scale=m_pool_scale, w_out=m_w_out)
    V = dict(norm_g=v_norm_g, w_in=v_w_in, conv_w=v_conv_w, conv_b=v_conv_b, conv_ln_g=v_conv_ln_g, conv_ln_b=v_conv_ln_b,
             na_q_g=v_na_q_g, na_k_g=v_na_k_g, na_rpb=v_na_rpb, gla_a2_f=v_gla_a2_f, gla_ab_f=v_gla_ab_f, gla_a2_b=v_gla_a2_b,
             gla_ab_b=v_gla_ab_b, gla_o_g=v_gla_o_g, pool_w=v_pool_w, pool_scale=v_pool_scale, w_out=v_w_out)
    E, S, D = x.shape
    T = E * S
    L = DEPTH
    xi, yi, ci = _coords()
    chip = 2 * xi + yi
    cw_sh, a2_sh = conv_w.shape[-1], gla_a2_f.shape[-1]

    small_sh = jnp.concatenate([
        jnp.pad(conv_w, ((0, 0), (0, 1), (0, 0))),
        jnp.pad(gla_a2_f, ((0, 0), (0, 0), (0, 128 - a2_sh))),
        jnp.pad(gla_a2_b, ((0, 0), (0, 0), (0, 128 - a2_sh)))], axis=1)
    g_win, g_wout, g_small = _gather_weights((w_in.astype(BF16), w_out.astype(BF16), small_sh), "gather_weights")
    w_in_full = _to_layout(jnp.transpose(g_win, (1, 2, 0, 3)).reshape(L, D, N_IN))
    w_out_full = jnp.transpose(g_wout, (1, 0, 2, 3)).reshape(L, D, D)
    conv_w_full = jnp.transpose(g_small[:, :, 0:32, :], (1, 2, 0, 3)).reshape(L, 32, 4 * cw_sh)
    a2f_full = jnp.transpose(g_small[:, :, 32:48, :a2_sh], (1, 2, 0, 3)).reshape(L, GLA_RANK, 4 * a2_sh)
    a2b_full = jnp.transpose(g_small[:, :, 48:64, :a2_sh], (1, 2, 0, 3)).reshape(L, GLA_RANK, 4 * a2_sh)

    params = []
    for l in range(L):
        params.append(dict(
            norm_g=norm_g[l][None], w_in=w_in_full[l], w_out=w_out_full[l], conv_w32=conv_w_full[l], conv_b=conv_b[l][None],
            conv_ln_g=conv_ln_g[l][None], conv_ln_b=conv_ln_b[l][None], na_q_g=na_q_g[l].reshape(1, GROUP_W),
            na_k_g=na_k_g[l].reshape(1, GROUP_W), na_bias=_na_bias(na_rpb[l], f"l{l}_na_bias"),
            a2_f=jnp.zeros((128, _HK), F32).at[0:GLA_RANK].set(a2f_full[l]),
            a2_b=jnp.zeros((128, _HK), F32).at[GLA_RANK:2 * GLA_RANK].set(a2b_full[l]),
            gla_ab_f=gla_ab_f[l][None], gla_ab_b=gla_ab_b[l][None], gla_o_g=gla_o_g[l].reshape(1, GROUP_W),
            pool_w_bf=pool_w[l].astype(BF16), pool_scale=pool_scale[l][None]))

    act = x.reshape(T, D)
    saved = []
    for l in range(L):
        act, sv = _layer_fwd(l, act, params[l], S)
        saved.append(sv)
    dact, dact_bf, loss_loc = _loss_head(act, loss_target.reshape(T, D), "loss_head")
    loss = lax.psum(loss_loc[0, 0], ("x", "y", "c"))
    grads = [None] * L
    for l in reversed(range(L)):
        dact, dact_bf, grads[l] = _layer_bwd(l, dact, dact_bf, saved[l], params[l], S)
    grad_x = dact.reshape(E, S, D)
    G = {k: jnp.stack([grads[l][k] for l in range(L)]) for k in WEIGHTS}

    cols_in, cols_out = N_IN // 4, D
    p_win = jnp.transpose(_from_layout(G["w_in"]).reshape(L, D, 4, cols_in), (0, 2, 1, 3))
    p_wout = G["w_out"].reshape(L, 4, D // 4, D)
    small_names = _REPL + _SHARD_SMALL
    small_g = _pack([G[k] for k in small_names])
    g_in, g_out, r_small = _reduce_gradients(p_win, p_wout, small_g, ci)

    rows_in, rows_out = L * D, L * (D // 4)
    res = {}
    res["w_in"] = [a.reshape(L, D, cols_in) for a in _adamw(
        w_in.reshape(rows_in, cols_in), (g_in.reshape(rows_in, cols_in),), m_w_in.reshape(rows_in, cols_in),
        v_w_in.reshape(rows_in, cols_in), 256, "adamw_w_in")]
    res["w_out"] = [a.reshape(L, D // 4, D) for a in _adamw(
        w_out.reshape(rows_out, cols_out), (g_out.reshape(rows_out, cols_out),), m_w_out.reshape(rows_out, cols_out),
        v_w_out.reshape(rows_out, cols_out), 256, "adamw_w_out")]
    zeros_sh = [jnp.zeros(G[k].shape, F32) for k in _SHARD_SMALL]
    pk = lambda dct: _pack([dct[k] for k in _REPL] + zeros_sh)
    small_res = _adamw(pk(W), (r_small,), pk(M), pk(V), small_g.shape[0], "adamw_small")
    shapes = [G[k].shape for k in small_names]
    unp = [_unpack(a, shapes) for a in small_res]
    for i, k in enumerate(_REPL):
        res[k] = [u[i] for u in unp]
    g_sh = []
    for i, k in enumerate(_SHARD_SMALL):
        gfull = unp[0][len(_REPL) + i]
        wdt = W[k].shape[-1]
        g_sh.append(lax.dynamic_slice_in_dim(gfull, chip * wdt, wdt, axis=2))
    g_sh_p = _pack(g_sh)
    sh_res = _adamw(_pack([W[k] for k in _SHARD_SMALL]), (g_sh_p,), _pack([M[k] for k in _SHARD_SMALL]),
                    _pack([V[k] for k in _SHARD_SMALL]), g_sh_p.shape[0], "adamw_shard_small")
    shapes2 = [W[k].shape for k in _SHARD_SMALL]
    unp2 = [_unpack(a, shapes2) for a in sh_res]
    for i, k in enumerate(_SHARD_SMALL):
        res[k] = [u[i] for u in unp2]

    outs = [loss, grad_x]
    for j in range(4):
        outs += [res[k][j] for k in WEIGHTS]
    return tuple(outs)
```

```python
import functools

import numpy as np
import jax
import jax.numpy as jnp
from jax import lax
from jax.experimental import pallas as pl
from jax.experimental.pallas import tpu as pltpu

F32 = jnp.float32
BF16 = jnp.bfloat16
HI = lax.Precision.HIGHEST
MESH = pl.DeviceIdType.MESH

EPS = 1e-6
D_MODEL = 2048
GROUP_W = 512
SEQ = 2048
DEPTH = 2
N_IN = 6176
GRID_W = 64
CONV_K = 31
NA_HEADS = 8
NA_DH = 64
NA_ROWS = 8
NA_COLS = 16
GLA_HEADS = 4
GLA_DK = 64
GLA_DV = 128
GLA_RANK = 16
GLA_TAU = 16.0
CHUNK = 64
POOL_WINDOWS = (2, 4, 8, 16)
ADAM_LR, ADAM_B1, ADAM_B2, ADAM_EPS, ADAM_WD, ADAM_STEP = 0.001, 0.9, 0.999, 1e-08, 0.01, 10

A_VAL, A_GLU, A_GATE = 0, 512, 1024
B_Q, B_K, B_V, B_GATE = 1536, 2048, 2560, 3072
C_Q, C_K, C_V, C_GATE = 3584, 3840, 4096, 4608
D_VAL, D_GATE = 5120, 5632
LR_OFF = 6144
NZ = 6400
NEG = -1e30
VMEM_LIMIT = 56 * 1024 * 1024


def _cp(sem=None):
    return pltpu.CompilerParams(dimension_semantics=sem, vmem_limit_bytes=VMEM_LIMIT)


def _sigmoid(x):
    return 1.0 / (1.0 + jnp.exp(-x))


def _silu(x):
    return x * _sigmoid(x)


def _dsilu(x):
    s = _sigmoid(x)
    return s * (1.0 + x * (1.0 - s))


def _matmul(a, b, *, dims, out_dtype, tm, tn, tk, name, res=None):
    if dims == "nn":
        (M, K), N = a.shape, b.shape[1]
    elif dims == "nt":
        (M, K), N = a.shape, b.shape[0]
    else:
        (K, M), N = a.shape, b.shape[1]
    tm, tn, tk = min(tm, M), min(tn, N), min(tk, K)
    nk = K // tk
    assert M % tm == 0 and N % tn == 0 and K % tk == 0, (M, N, K, tm, tn, tk)
    dn = {"nn": (((1,), (0,)), ((), ())), "nt": (((1,), (1,)), ((), ())), "tn": (((0,), (0,)), ((), ()))}[dims]
    if dims == "tn":
        a_spec = pl.BlockSpec((tk, tm), lambda i, j, k: (k, i))
    else:
        a_spec = pl.BlockSpec((tm, tk), lambda i, j, k: (i, k))
    if dims == "nt":
        b_spec = pl.BlockSpec((tn, tk), lambda i, j, k: (j, k))
    else:
        b_spec = pl.BlockSpec((tk, tn), lambda i, j, k: (k, j))
    o_spec = pl.BlockSpec((tm, tn), lambda i, j, k: (i, j))
    has_res = res is not None

    def body(*refs):
        if has_res:
            a_ref, b_ref, r_ref, o_ref, acc = refs
        else:
            a_ref, b_ref, o_ref, acc = refs
        k = pl.program_id(2)

        @pl.when(k == 0)
        def _():
            acc[...] = jnp.zeros_like(acc)

        acc[...] += lax.dot_general(a_ref[...], b_ref[...], dn, preferred_element_type=F32)

        @pl.when(k == nk - 1)
        def _():
            r = acc[...]
            if has_res:
                r = r + r_ref[...]
            o_ref[...] = r.astype(o_ref.dtype)

    in_specs = [a_spec, b_spec] + ([o_spec] if has_res else [])
    args = (a, b) + ((res,) if has_res else ())
    return pl.pallas_call(
        body, out_shape=jax.ShapeDtypeStruct((M, N), out_dtype), grid=(M // tm, N // tn, nk),
        in_specs=in_specs, out_specs=o_spec, scratch_shapes=[pltpu.VMEM((tm, tn), F32)],
        name=name, compiler_params=_cp(("parallel", "parallel", "arbitrary")))(*args)


def _rmsnorm_fwd(x, g, name):
    T, D = x.shape
    tm = 256

    def body(x_ref, g_ref, h_ref):
        xv = x_ref[...]
        r = lax.rsqrt(jnp.mean(xv * xv, axis=-1, keepdims=True) + EPS)
        h_ref[...] = (xv * r * g_ref[...]).astype(h_ref.dtype)

    return pl.pallas_call(
        body, out_shape=jax.ShapeDtypeStruct((T, D), BF16), grid=(T // tm,),
        in_specs=[pl.BlockSpec((tm, D), lambda i: (i, 0)), pl.BlockSpec((1, D), lambda i: (0, 0))],
        out_specs=pl.BlockSpec((tm, D), lambda i: (i, 0)), name=name, compiler_params=_cp(("parallel",)))(x, g)


def _rmsnorm_bwd(x, g, dh, dres, name):
    T, D = x.shape
    tm = 256

    def body(x_ref, g_ref, dh_ref, dres_ref, dx_ref, dxb_ref, dg_ref):
        xv = x_ref[...]
        r = lax.rsqrt(jnp.mean(xv * xv, axis=-1, keepdims=True) + EPS)
        xh = xv * r
        dh_v = dh_ref[...]
        dxh = dh_v * g_ref[...]
        dx = r * (dxh - xh * jnp.mean(dxh * xh, axis=-1, keepdims=True)) + dres_ref[...]
        dx_ref[...] = dx
        dxb_ref[...] = dx.astype(BF16)

        @pl.when(pl.program_id(0) == 0)
        def _():
            dg_ref[...] = jnp.zeros_like(dg_ref)

        dg_ref[...] += jnp.sum(dh_v * xh, axis=0, keepdims=True)

    row = pl.BlockSpec((tm, D), lambda i: (i, 0))
    vec = pl.BlockSpec((1, D), lambda i: (0, 0))
    return pl.pallas_call(
        body, out_shape=(jax.ShapeDtypeStruct((T, D), F32), jax.ShapeDtypeStruct((T, D), BF16), jax.ShapeDtypeStruct((1, D), F32)),
        grid=(T // tm,), in_specs=[row, vec, row, row], out_specs=(row, row, vec), name=name,
        compiler_params=_cp(("arbitrary",)))(x, g, dh, dres)


def _loss_head(y, target, name):
    T, D = y.shape
    tm = 256

    def body(y_ref, t_ref, d_ref, db_ref, l_ref):
        e = y_ref[...] - t_ref[...]
        d = e * (1.0 / D)
        d_ref[...] = d
        db_ref[...] = d.astype(BF16)

        @pl.when(pl.program_id(0) == 0)
        def _():
            l_ref[...] = jnp.zeros_like(l_ref)

        row = jnp.sum(e * e, axis=-1, keepdims=True) * (0.5 / D)
        l_ref[...] += jnp.sum(row, axis=0, keepdims=True)

    row = pl.BlockSpec((tm, D), lambda i: (i, 0))
    return pl.pallas_call(
        body, out_shape=(jax.ShapeDtypeStruct((T, D), F32), jax.ShapeDtypeStruct((T, D), BF16),
                         jax.ShapeDtypeStruct((1, 1), F32)), grid=(T // tm,),
        in_specs=[row, row], out_specs=(row, row, pl.BlockSpec((1, 1), lambda i: (0, 0))),
        name=name, compiler_params=_cp(("arbitrary",)))(y, target)


_GATE_COLS = (A_GATE // GROUP_W, B_GATE // GROUP_W, C_GATE // GROUP_W, D_GATE // GROUP_W)


def _gate_fwd(pres, z, name):
    T = z.shape[0]
    tm = 256

    def body(pa, pb, pc, pd, ga, gb, gc, gd, y_ref):
        for n, (p, g) in enumerate(((pa, ga), (pb, gb), (pc, gc), (pd, gd))):
            y_ref[:, n * GROUP_W:(n + 1) * GROUP_W] = (p[...] * _silu(g[...])).astype(BF16)

    pre_spec = pl.BlockSpec((tm, GROUP_W), lambda i: (i, 0))
    gate_specs = [pl.BlockSpec((tm, GROUP_W), functools.partial(lambda i, c: (i, c), c=c)) for c in _GATE_COLS]
    return pl.pallas_call(
        body, out_shape=jax.ShapeDtypeStruct((T, 4 * GROUP_W), BF16), grid=(T // tm,),
        in_specs=[pre_spec] * 4 + gate_specs, out_specs=pl.BlockSpec((tm, 4 * GROUP_W), lambda i: (i, 0)),
        name=name, compiler_params=_cp(("parallel",)))(*pres, z, z, z, z)


def _gate_bwd(dy, pres, z, name):
    T = z.shape[0]
    tm = 256

    def body(dy_ref, pa, pb, pc, pd, ga, gb, gc, gd, dpa, dpb, dpc, dpd, dga, dgb, dgc, dgd):
        for n, (p, g, dp, dg) in enumerate(((pa, ga, dpa, dga), (pb, gb, dpb, dgb), (pc, gc, dpc, dgc), (pd, gd, dpd, dgd))):
            d = dy_ref[:, n * GROUP_W:(n + 1) * GROUP_W]
            gv = g[...]
            dp[...] = d * _silu(gv)
            dg[...] = (d * p[...] * _dsilu(gv)).astype(BF16)

    pre_spec = pl.BlockSpec((tm, GROUP_W), lambda i: (i, 0))
    gate_specs = [pl.BlockSpec((tm, GROUP_W), functools.partial(lambda i, c: (i, c), c=c)) for c in _GATE_COLS]
    outs = tuple([jax.ShapeDtypeStruct((T, GROUP_W), F32)] * 4 + [jax.ShapeDtypeStruct((T, GROUP_W), BF16)] * 4)
    return pl.pallas_call(
        body, out_shape=outs, grid=(T // tm,),
        in_specs=[pl.BlockSpec((tm, 4 * GROUP_W), lambda i: (i, 0))] + [pre_spec] * 4 + gate_specs,
        out_specs=tuple([pre_spec] * 8), name=name, compiler_params=_cp(("parallel",)))(dy, *pres, z, z, z, z)


_PAD = 16
_RC = 256


def _conv_fwd(z, conv_w32, conv_b, S, name):
    T = z.shape[0]
    E = T // S
    LW = 128

    def body(val_ref, glu_ref, w_ref, b_ref, y_ref, upad):
        upad[0:_PAD, :] = jnp.zeros((_PAD, LW), F32)
        upad[_PAD + S:_PAD + S + _PAD, :] = jnp.zeros((_PAD, LW), F32)
        upad[_PAD:_PAD + S, :] = val_ref[...] * _sigmoid(glu_ref[...])
        for r in range(S // _RC):
            acc = jnp.broadcast_to(b_ref[...], (_RC, LW))
            for k in range(CONV_K):
                st = r * _RC + k + 1
                acc = acc + upad[st:st + _RC, :] * w_ref[k:k + 1, :]
            y_ref[r * _RC:(r + 1) * _RC, :] = acc

    return pl.pallas_call(
        body, out_shape=jax.ShapeDtypeStruct((T, GROUP_W), F32), grid=(E, GROUP_W // LW),
        in_specs=[pl.BlockSpec((S, LW), lambda e, j: (e, A_VAL // LW + j)),
                  pl.BlockSpec((S, LW), lambda e, j: (e, A_GLU // LW + j)),
                  pl.BlockSpec((32, LW), lambda e, j: (0, j)),
                  pl.BlockSpec((1, LW), lambda e, j: (0, j))],
        out_specs=pl.BlockSpec((S, LW), lambda e, j: (e, j)),
        scratch_shapes=[pltpu.VMEM((S + 2 * _PAD, LW), F32)],
        name=name, compiler_params=_cp(("parallel", "parallel")))(z, z, conv_w32, conv_b)


def _conv_bwd(z, conv_w32, dyc, S, name):
    T = z.shape[0]
    E = T // S
    LW = 128

    def body(val_ref, glu_ref, w_ref, dy_ref, dval_ref, dglu_ref, dw_ref, db_ref, upad, dpad):
        e = pl.program_id(1)
        zeros = jnp.zeros((_PAD, LW), F32)
        upad[0:_PAD, :] = zeros
        upad[_PAD + S:_PAD + S + _PAD, :] = zeros
        dpad[0:_PAD, :] = zeros
        dpad[_PAD + S:_PAD + S + _PAD, :] = zeros
        upad[_PAD:_PAD + S, :] = val_ref[...] * _sigmoid(glu_ref[...])
        dpad[_PAD:_PAD + S, :] = dy_ref[...]

        @pl.when(e == 0)
        def _():
            dw_ref[...] = jnp.zeros_like(dw_ref)
            db_ref[...] = jnp.zeros_like(db_ref)

        db_ref[...] += jnp.sum(dy_ref[...], axis=0, keepdims=True)
        for r in range(S // _RC):
            dyr = dy_ref[r * _RC:(r + 1) * _RC, :]
            du = jnp.zeros((_RC, LW), F32)
            for k in range(CONV_K):
                st = r * _RC + k + 1
                dw_ref[k:k + 1, :] += jnp.sum(dyr * upad[st:st + _RC, :], axis=0, keepdims=True)
                sd = r * _RC + (CONV_K - 1 - k) + 1
                du = du + dpad[sd:sd + _RC, :] * w_ref[k:k + 1, :]
            sl = slice(r * _RC, (r + 1) * _RC)
            val = val_ref[sl, :]
            sg = _sigmoid(glu_ref[sl, :])
            dval_ref[sl, :] = (du * sg).astype(BF16)
            dglu_ref[sl, :] = (du * val * sg * (1.0 - sg)).astype(BF16)

    blk = pl.BlockSpec((S, LW), lambda j, e: (e, j))
    return pl.pallas_call(
        body, out_shape=(jax.ShapeDtypeStruct((T, GROUP_W), BF16), jax.ShapeDtypeStruct((T, GROUP_W), BF16),
                         jax.ShapeDtypeStruct((32, GROUP_W), F32), jax.ShapeDtypeStruct((1, GROUP_W), F32)),
        grid=(GROUP_W // LW, E),
        in_specs=[pl.BlockSpec((S, LW), lambda j, e: (e, A_VAL // LW + j)),
                  pl.BlockSpec((S, LW), lambda j, e: (e, A_GLU // LW + j)),
                  pl.BlockSpec((32, LW), lambda j, e: (0, j)), blk],
        out_specs=(blk, blk, pl.BlockSpec((32, LW), lambda j, e: (0, j)), pl.BlockSpec((1, LW), lambda j, e: (0, j))),
        scratch_shapes=[pltpu.VMEM((S + 2 * _PAD, LW), F32), pltpu.VMEM((S + 2 * _PAD, LW), F32)],
        name=name, compiler_params=_cp(("parallel", "arbitrary")))(z, z, conv_w32, dyc)


def _ln_silu_fwd(yc, g, b, name):
    T, C = yc.shape
    tm = 256

    def body(y_ref, g_ref, b_ref, o_ref):
        y = y_ref[...]
        mu = jnp.mean(y, axis=-1, keepdims=True)
        yc_ = y - mu
        r = lax.rsqrt(jnp.mean(yc_ * yc_, axis=-1, keepdims=True) + EPS)
        o_ref[...] = _silu(yc_ * r * g_ref[...] + b_ref[...])

    row = pl.BlockSpec((tm, C), lambda i: (i, 0))
    vec = pl.BlockSpec((1, C), lambda i: (0, 0))
    return pl.pallas_call(body, out_shape=jax.ShapeDtypeStruct((T, C), F32), grid=(T // tm,),
                          in_specs=[row, vec, vec], out_specs=row, name=name, compiler_params=_cp(("parallel",)))(yc, g, b)


def _ln_silu_bwd(yc, g, b, dpre, name):
    T, C = yc.shape
    tm = 256

    def body(y_ref, g_ref, b_ref, dp_ref, dy_ref, dg_ref, db_ref):
        y = y_ref[...]
        mu = jnp.mean(y, axis=-1, keepdims=True)
        yc_ = y - mu
        r = lax.rsqrt(jnp.mean(yc_ * yc_, axis=-1, keepdims=True) + EPS)
        xh = yc_ * r
        gv = g_ref[...]
        dln = dp_ref[...] * _dsilu(xh * gv + b_ref[...])
        dxh = dln * gv
        dy_ref[...] = r * (dxh - jnp.mean(dxh, axis=-1, keepdims=True) - xh * jnp.mean(dxh * xh, axis=-1, keepdims=True))

        @pl.when(pl.program_id(0) == 0)
        def _():
            dg_ref[...] = jnp.zeros_like(dg_ref)
            db_ref[...] = jnp.zeros_like(db_ref)

        dg_ref[...] += jnp.sum(dln * xh, axis=0, keepdims=True)
        db_ref[...] += jnp.sum(dln, axis=0, keepdims=True)

    row = pl.BlockSpec((tm, C), lambda i: (i, 0))
    vec = pl.BlockSpec((1, C), lambda i: (0, 0))
    return pl.pallas_call(
        body, out_shape=(jax.ShapeDtypeStruct((T, C), F32), jax.ShapeDtypeStruct((1, C), F32), jax.ShapeDtypeStruct((1, C), F32)),
        grid=(T // tm,), in_specs=[row, vec, vec, row], out_specs=(row, vec, vec), name=name,
        compiler_params=_cp(("arbitrary",)))(yc, g, b, dpre)


def _pool_counts(S, w, rows0, n):
    t = (lax.broadcasted_iota(jnp.int32, (n, 1), 0) + rows0)
    lo = jnp.maximum(t - w // 2, 0)
    hi = jnp.minimum(t + w // 2, S)
    return (hi - lo).astype(F32)


def _pool_fwd(z, pool_w, pool_scale, S, name):
    T = z.shape[0]
    E = T // S
    CG = 128

    def body(u_ref, w_ref, s_ref, o_ref, upad, dif):
        zeros = jnp.zeros((_PAD, GROUP_W), F32)
        upad[0:_PAD, :] = zeros
        upad[_PAD + S:_PAD + S + _PAD, :] = zeros
        upad[_PAD:_PAD + S, :] = u_ref[...]
        for gi, w in enumerate(POOL_WINDOWS):
            ls = slice(gi * CG, (gi + 1) * CG)
            for r in range(S // _RC):
                acc = jnp.zeros((_RC, CG), F32)
                for j in range(-(w // 2), w // 2):
                    st = _PAD + r * _RC + j
                    acc = acc + upad[st:st + _RC, ls]
                cnt = _pool_counts(S, w, r * _RC, _RC)
                dif[r * _RC:(r + 1) * _RC, :] = (acc / cnt - u_ref[r * _RC:(r + 1) * _RC, ls]).astype(BF16)
            yp = jnp.dot(dif[...], w_ref[gi], preferred_element_type=F32)
            o_ref[:, ls] = yp * s_ref[:, ls]

    return pl.pallas_call(
        body, out_shape=jax.ShapeDtypeStruct((T, GROUP_W), F32), grid=(E,),
        in_specs=[pl.BlockSpec((S, GROUP_W), lambda e: (e, D_VAL // GROUP_W)),
                  pl.BlockSpec((4, CG, CG), lambda e: (0, 0, 0)),
                  pl.BlockSpec((1, GROUP_W), lambda e: (0, 0))],
        out_specs=pl.BlockSpec((S, GROUP_W), lambda e: (e, 0)),
        scratch_shapes=[pltpu.VMEM((S + 2 * _PAD, GROUP_W), F32), pltpu.VMEM((S, CG), BF16)],
        name=name, compiler_params=_cp(("parallel",)))(z, pool_w, pool_scale)


def _pool_bwd(z, pool_w, pool_scale, dpre, S, name):
    T = z.shape[0]
    E = T // S
    CG = 128

    def body(u_ref, w_ref, s_ref, dp_ref, du_ref, dw_ref, ds_ref, upad, dif, qpad):
        zeros = jnp.zeros((_PAD, GROUP_W), F32)
        upad[0:_PAD, :] = zeros
        upad[_PAD + S:_PAD + S + _PAD, :] = zeros
        upad[_PAD:_PAD + S, :] = u_ref[...]
        zc = jnp.zeros((_PAD, CG), F32)
        qpad[0:_PAD, :] = zc
        qpad[_PAD + S:_PAD + S + _PAD, :] = zc

        @pl.when(pl.program_id(0) == 0)
        def _():
            dw_ref[...] = jnp.zeros_like(dw_ref)
            ds_ref[...] = jnp.zeros_like(ds_ref)

        for gi, w in enumerate(POOL_WINDOWS):
            ls = slice(gi * CG, (gi + 1) * CG)
            for r in range(S // _RC):
                acc = jnp.zeros((_RC, CG), F32)
                for j in range(-(w // 2), w // 2):
                    st = _PAD + r * _RC + j
                    acc = acc + upad[st:st + _RC, ls]
                cnt = _pool_counts(S, w, r * _RC, _RC)
                dif[r * _RC:(r + 1) * _RC, :] = (acc / cnt - u_ref[r * _RC:(r + 1) * _RC, ls]).astype(BF16)
            dp = dp_ref[:, ls]
            yp = jnp.dot(dif[...], w_ref[gi], preferred_element_type=F32)
            ds_ref[:, ls] += jnp.sum(dp * yp, axis=0, keepdims=True)
            dys = (dp * s_ref[:, ls]).astype(BF16)
            dw_ref[gi] += lax.dot_general(dif[...], dys, (((0,), (0,)), ((), ())), preferred_element_type=F32)
            dm = lax.dot_general(dys, w_ref[gi], (((1,), (1,)), ((), ())), preferred_element_type=F32)
            for r in range(S // _RC):
                cnt = _pool_counts(S, w, r * _RC, _RC)
                qpad[_PAD + r * _RC:_PAD + (r + 1) * _RC, :] = dm[r * _RC:(r + 1) * _RC, :] / cnt
            for r in range(S // _RC):
                acc = -dm[r * _RC:(r + 1) * _RC, :]
                for j in range(-(w // 2) + 1, w // 2 + 1):
                    st = _PAD + r * _RC + j
                    acc = acc + qpad[st:st + _RC, :]
                du_ref[r * _RC:(r + 1) * _RC, ls] = acc.astype(BF16)

    return pl.pallas_call(
        body, out_shape=(jax.ShapeDtypeStruct((T, GROUP_W), BF16), jax.ShapeDtypeStruct((4, CG, CG), F32),
                         jax.ShapeDtypeStruct((1, GROUP_W), F32)), grid=(E,),
        in_specs=[pl.BlockSpec((S, GROUP_W), lambda e: (e, D_VAL // GROUP_W)),
                  pl.BlockSpec((4, CG, CG), lambda e: (0, 0, 0)),
                  pl.BlockSpec((1, GROUP_W), lambda e: (0, 0)),
                  pl.BlockSpec((S, GROUP_W), lambda e: (e, 0))],
        out_specs=(pl.BlockSpec((S, GROUP_W), lambda e: (e, 0)), pl.BlockSpec((4, CG, CG), lambda e: (0, 0, 0)),
                   pl.BlockSpec((1, GROUP_W), lambda e: (0, 0))),
        scratch_shapes=[pltpu.VMEM((S + 2 * _PAD, GROUP_W), F32), pltpu.VMEM((S, CG), BF16),
                        pltpu.VMEM((S + 2 * _PAD, CG), F32)],
        name=name, compiler_params=_cp(("arbitrary",)))(z, pool_w, pool_scale, dpre)


def _na_tables():
    d = np.arange(NA_ROWS)[:, None]
    kr = np.arange(NA_ROWS)[None, :]
    ro = kr - d + (NA_ROWS - 1)
    qc = np.arange(GRID_W)[:, None]
    kc = np.arange(GRID_W)[None, :]
    cs = np.clip(qc - NA_COLS // 2, 0, GRID_W - NA_COLS)
    valid = (kc >= cs) & (kc < cs + NA_COLS)
    co = np.clip(kc - qc + (NA_COLS - 1), 0, 2 * NA_COLS - 2)
    return ro, co, valid


def _na_onehots():
    ro, co, valid = _na_tables()
    e_np = np.zeros((GRID_W, GRID_W, 128), np.float32)
    qi, ki = np.nonzero(valid)
    e_np[qi, ki, co[qi, ki]] = 1.0
    a_np = np.zeros((16, NA_ROWS * NA_ROWS), np.float32)
    a_np[ro.reshape(-1), np.arange(NA_ROWS * NA_ROWS)] = 1.0
    mask = np.where(valid, 0.0, NEG).astype(np.float32).reshape(1, GRID_W * GRID_W)
    return e_np.reshape(GRID_W * GRID_W, 128), a_np, mask


def _na_bias(rpb, name):
    e_np, a_np, mask = _na_onehots()
    H = NA_HEADS
    rp = jnp.pad(rpb, ((0, 0), (0, 1), (0, 128 - rpb.shape[2])))

    def body(r_ref, e_ref, at_ref, m_ref, o_ref):
        t = jnp.dot(at_ref[...], r_ref[0], precision=HI, preferred_element_type=F32)
        o_ref[0] = lax.dot_general(t, e_ref[...], (((1,), (1,)), ((), ())), precision=HI,
                                   preferred_element_type=F32) + m_ref[...]

    out = pl.pallas_call(
        body, out_shape=jax.ShapeDtypeStruct((H, NA_ROWS * NA_ROWS, GRID_W * GRID_W), F32), grid=(H,),
        in_specs=[pl.BlockSpec((1, 16, 128), lambda h: (h, 0, 0)),
                  pl.BlockSpec((GRID_W * GRID_W, 128), lambda h: (0, 0)),
                  pl.BlockSpec((NA_ROWS * NA_ROWS, 16), lambda h: (0, 0)),
                  pl.BlockSpec((1, GRID_W * GRID_W), lambda h: (0, 0))],
        out_specs=pl.BlockSpec((1, NA_ROWS * NA_ROWS, GRID_W * GRID_W), lambda h: (h, 0, 0)),
        name=name, compiler_params=_cp(("parallel",)))(rp, jnp.asarray(e_np), jnp.asarray(a_np.T), jnp.asarray(mask))
    t = out.reshape(H, NA_ROWS, NA_ROWS, GRID_W, GRID_W)
    return jnp.transpose(t, (0, 1, 3, 2, 4)).reshape(H, NA_ROWS, GRID_W, NA_ROWS * GRID_W)


def _seg_mean_matrix(width, seg):
    i = np.arange(width)
    return jnp.asarray((i[:, None] // seg == i[None, :] // seg).astype(np.float32) / seg)


def _na_fwd(z, qg, kg, bias, S, name):
    T = z.shape[0]
    E = T // S
    rows = S // GRID_W
    WIN = NA_ROWS * GRID_W
    seg = _seg_mean_matrix(128, NA_DH)

    def body(q_ref, k_ref, v_ref, qg_ref, kg_ref, bias_ref, seg_ref, o_ref, qs, ks, vs):
        for c in range(S // _RC):
            sl = slice(c * _RC, (c + 1) * _RC)
            q = q_ref[sl, :]
            k = k_ref[sl, :]
            qn = q * lax.rsqrt(jnp.dot(q * q, seg_ref[...], precision=HI, preferred_element_type=F32) + EPS) * qg_ref[...]
            kn = k * lax.rsqrt(jnp.dot(k * k, seg_ref[...], precision=HI, preferred_element_type=F32) + EPS) * kg_ref[...]
            v = v_ref[sl, :]
            for hh in range(2):
                ls = slice(hh * NA_DH, (hh + 1) * NA_DH)
                qs[hh, sl, :] = qn[:, ls].astype(BF16)
                ks[hh, sl, :] = kn[:, ls].astype(BF16)
                vs[hh, sl, :] = v[:, ls].astype(BF16)
        def row(r, carry):
            rs = jnp.clip(r - NA_ROWS // 2, 0, rows - NA_ROWS)
            q0 = pl.multiple_of(r * GRID_W, GRID_W)
            k0 = pl.multiple_of(rs * GRID_W, GRID_W)
            outs = []
            for hh in range(2):
                qr = qs[hh, pl.ds(q0, GRID_W), :]
                kw = ks[hh, pl.ds(k0, WIN), :]
                vw = vs[hh, pl.ds(k0, WIN), :]
                s = lax.dot_general(qr, kw, (((1,), (1,)), ((), ())), preferred_element_type=F32) * (NA_DH ** -0.5)
                s = s + bias_ref[hh, r - rs]
                p = jnp.exp(s - jnp.max(s, axis=-1, keepdims=True))
                l = jnp.sum(p, axis=-1, keepdims=True)
                outs.append(jnp.dot(p.astype(BF16), vw, preferred_element_type=F32) / l)
            o_ref[pl.ds(q0, GRID_W), :] = jnp.concatenate(outs, axis=1)
            return carry
        lax.fori_loop(0, rows, row, 0, unroll=2)

    LW = 128
    return pl.pallas_call(
        body, out_shape=jax.ShapeDtypeStruct((T, GROUP_W), F32), grid=(E, GROUP_W // LW),
        in_specs=[pl.BlockSpec((S, LW), lambda e, j: (e, B_Q // LW + j)),
                  pl.BlockSpec((S, LW), lambda e, j: (e, B_K // LW + j)),
                  pl.BlockSpec((S, LW), lambda e, j: (e, B_V // LW + j)),
                  pl.BlockSpec((1, LW), lambda e, j: (0, j)),
                  pl.BlockSpec((1, LW), lambda e, j: (0, j)),
                  pl.BlockSpec((2, NA_ROWS, GRID_W, WIN), lambda e, j: (j, 0, 0, 0)),
                  pl.BlockSpec((LW, LW), lambda e, j: (0, 0))],
        out_specs=pl.BlockSpec((S, LW), lambda e, j: (e, j)),
        scratch_shapes=[pltpu.VMEM((2, S, NA_DH), BF16)] * 3,
        name=name, compiler_params=_cp(("parallel", "parallel")))(z, z, z, qg, kg, bias, seg)


def _na_bwd(z, qg, kg, bias, do, S, name):
    T = z.shape[0]
    E = T // S
    rows = S // GRID_W
    WIN = NA_ROWS * GRID_W
    seg = _seg_mean_matrix(128, NA_DH)
    SC = NA_DH ** -0.5

    def body(q_ref, k_ref, v_ref, qg_ref, kg_ref, bias_ref, seg_ref, do_ref,
             dq_ref, dk_ref, dv_ref, dbias_ref, dqg_ref, dkg_ref, qs, ks, vs, dos, dqn, dkn, dvs):
        e = pl.program_id(1)

        @pl.when(e == 0)
        def _():
            dbias_ref[...] = jnp.zeros_like(dbias_ref)
            dqg_ref[...] = jnp.zeros_like(dqg_ref)
            dkg_ref[...] = jnp.zeros_like(dkg_ref)

        for c in range(S // _RC):
            sl = slice(c * _RC, (c + 1) * _RC)
            q = q_ref[sl, :]
            k = k_ref[sl, :]
            qn = q * lax.rsqrt(jnp.dot(q * q, seg_ref[...], precision=HI, preferred_element_type=F32) + EPS) * qg_ref[...]
            kn = k * lax.rsqrt(jnp.dot(k * k, seg_ref[...], precision=HI, preferred_element_type=F32) + EPS) * kg_ref[...]
            v = v_ref[sl, :]
            dd = do_ref[sl, :]
            for hh in range(2):
                ls = slice(hh * NA_DH, (hh + 1) * NA_DH)
                qs[hh, sl, :] = qn[:, ls].astype(BF16)
                ks[hh, sl, :] = kn[:, ls].astype(BF16)
                vs[hh, sl, :] = v[:, ls].astype(BF16)
                dos[hh, sl, :] = dd[:, ls].astype(BF16)
            dkn[sl, :] = jnp.zeros((_RC, 128), F32)
            dvs[sl, :] = jnp.zeros((_RC, 128), F32)

        def row(r, carry):
            rs = jnp.clip(r - NA_ROWS // 2, 0, rows - NA_ROWS)
            q0 = pl.multiple_of(r * GRID_W, GRID_W)
            k0 = pl.multiple_of(rs * GRID_W, GRID_W)
            dqs, dks, dvv = [], [], []
            for hh in range(2):
                qr = qs[hh, pl.ds(q0, GRID_W), :]
                dor = dos[hh, pl.ds(q0, GRID_W), :]
                kw = ks[hh, pl.ds(k0, WIN), :]
                vw = vs[hh, pl.ds(k0, WIN), :]
                s = lax.dot_general(qr, kw, (((1,), (1,)), ((), ())), preferred_element_type=F32) * SC
                s = s + bias_ref[hh, r - rs]
                p = jnp.exp(s - jnp.max(s, axis=-1, keepdims=True))
                p = p / jnp.sum(p, axis=-1, keepdims=True)
                dp = lax.dot_general(dor, vw, (((1,), (1,)), ((), ())), preferred_element_type=F32)
                ds = p * (dp - jnp.sum(p * dp, axis=-1, keepdims=True))
                dbias_ref[hh, r - rs] += ds
                dsb = ds.astype(BF16)
                dqs.append(jnp.dot(dsb, kw, preferred_element_type=F32) * SC)
                dks.append(lax.dot_general(dsb, qr, (((0,), (0,)), ((), ())), preferred_element_type=F32) * SC)
                dvv.append(lax.dot_general(p.astype(BF16), dor, (((0,), (0,)), ((), ())), preferred_element_type=F32))
            dqn[pl.ds(q0, GRID_W), :] = jnp.concatenate(dqs, axis=1)
            dkn[pl.ds(k0, WIN), :] += jnp.concatenate(dks, axis=1)
            dvs[pl.ds(k0, WIN), :] += jnp.concatenate(dvv, axis=1)
            return carry
        lax.fori_loop(0, rows, row, 0, unroll=2)

        for c in range(S // _RC):
            sl = slice(c * _RC, (c + 1) * _RC)
            for x_ref, g_ref, dn, dx_ref, dg_ref in ((q_ref, qg_ref, dqn, dq_ref, dqg_ref), (k_ref, kg_ref, dkn, dk_ref, dkg_ref)):
                x = x_ref[sl, :]
                r_ = lax.rsqrt(jnp.dot(x * x, seg_ref[...], precision=HI, preferred_element_type=F32) + EPS)
                xh = x * r_
                d = dn[sl, :]
                dxh = d * g_ref[...]
                mean = jnp.dot(dxh * xh, seg_ref[...], precision=HI, preferred_element_type=F32)
                dx_ref[sl, :] = (r_ * (dxh - xh * mean)).astype(BF16)
                dg_ref[...] += jnp.sum(d * xh, axis=0, keepdims=True)
            dv_ref[sl, :] = dvs[sl, :].astype(BF16)

    LW = 128
    blk = pl.BlockSpec((S, LW), lambda j, e: (e, j))
    vec = pl.BlockSpec((1, LW), lambda j, e: (0, j))
    bsp = pl.BlockSpec((2, NA_ROWS, GRID_W, WIN), lambda j, e: (j, 0, 0, 0))
    return pl.pallas_call(
        body, out_shape=(jax.ShapeDtypeStruct((T, GROUP_W), BF16),) * 3 + (
            jax.ShapeDtypeStruct((NA_HEADS, NA_ROWS, GRID_W, WIN), F32),
            jax.ShapeDtypeStruct((1, GROUP_W), F32), jax.ShapeDtypeStruct((1, GROUP_W), F32)),
        grid=(GROUP_W // LW, E),
        in_specs=[pl.BlockSpec((S, LW), lambda j, e: (e, B_Q // LW + j)),
                  pl.BlockSpec((S, LW), lambda j, e: (e, B_K // LW + j)),
                  pl.BlockSpec((S, LW), lambda j, e: (e, B_V // LW + j)),
                  vec, vec, bsp, pl.BlockSpec((LW, LW), lambda j, e: (0, 0)), blk],
        out_specs=(blk, blk, blk, bsp, vec, vec),
        scratch_shapes=[pltpu.VMEM((2, S, NA_DH), BF16)] * 4 + [pltpu.VMEM((S, LW), F32)] * 3,
        name=name, compiler_params=_cp(("parallel", "arbitrary")))(z, z, z, qg, kg, bias, seg, do)


def _na_rpb_grad(dbias, name):
    e_np, a_np, _ = _na_onehots()
    H = NA_HEADS
    x = dbias.reshape(H, NA_ROWS, GRID_W, NA_ROWS, GRID_W)
    x = jnp.transpose(x, (0, 1, 3, 2, 4)).reshape(H, NA_ROWS * NA_ROWS, GRID_W * GRID_W)

    def body(x_ref, e_ref, a_ref, o_ref):
        y = jnp.dot(x_ref[0], e_ref[...], precision=HI, preferred_element_type=F32)
        o_ref[0] = jnp.dot(a_ref[...], y, precision=HI, preferred_element_type=F32)

    out = pl.pallas_call(
        body, out_shape=jax.ShapeDtypeStruct((H, 16, 128), F32), grid=(H,),
        in_specs=[pl.BlockSpec((1, 64, GRID_W * GRID_W), lambda h: (h, 0, 0)),
                  pl.BlockSpec((GRID_W * GRID_W, 128), lambda h: (0, 0)),
                  pl.BlockSpec((16, 64), lambda h: (0, 0))],
        out_specs=pl.BlockSpec((1, 16, 128), lambda h: (h, 0, 0)),
        name=name, compiler_params=_cp(("parallel",)))(x, jnp.asarray(e_np), jnp.asarray(a_np))
    return out[:, :2 * NA_ROWS - 1, :2 * NA_COLS - 1]


_HK = GLA_HEADS * GLA_DK
_HV = GLA_HEADS * GLA_DV


def _gla_consts(reverse):
    i = np.arange(CHUNK)
    tri = (i[:, None] <= i[None, :]) if reverse else (i[:, None] >= i[None, :])
    j = np.arange(_HK)
    oseg = (j[:, None] // GLA_DK == j[None, :] // GLA_DK)
    return (jnp.asarray(tri.astype(np.float32)), jnp.asarray(tri.T.astype(np.float32)), jnp.asarray(oseg.astype(np.float32), BF16))


def _log_decay(lr, a2, ab):
    zg = jnp.dot(lr, a2, precision=HI, preferred_element_type=F32) + ab
    g = (jnp.minimum(zg, 0.0) - jnp.log(1.0 + jnp.exp(-jnp.abs(zg)))) * (1.0 / GLA_TAU)
    return zg, g


def _dotf(a, b, dn):
    return lax.dot_general(a, b, dn, precision=HI, preferred_element_type=F32)


def _dotb(a, b, dn):
    return lax.dot_general(a.astype(BF16), b.astype(BF16), dn, preferred_element_type=F32)


_COLS = 4
_SUB = 16
_NSUB = CHUNK // _SUB


def _gla_cross_blocks(reverse):
    return range(0, _NSUB - 1) if reverse else range(1, _NSUB)


def _gla_cross_terms(s, reverse, b_s, q_s, k_s, oseg_ref):
    r0 = s * _SUB
    ref = r0 + (_SUB - 1 if reverse else 0)
    bref = b_s[ref:ref + 1, :]
    rowj = lax.broadcasted_iota(jnp.int32, (CHUNK, 1), 0)
    seen = (rowj >= r0 + _SUB) if reverse else (rowj < r0)
    ek = jnp.where(seen, jnp.exp(jnp.minimum(bref - b_s[...], 0.0)), 0.0)
    kt = k_s[...] * ek
    eq = jnp.exp(jnp.minimum(b_s[r0:r0 + _SUB, :] - bref, 0.0))
    qt = q_s[r0:r0 + _SUB, :] * eq
    nmat = jnp.concatenate([kt.astype(BF16)] * GLA_HEADS, axis=0) * oseg_ref[...]
    return qt, eq, kt, ek, nmat


_NN = (((1,), (0,)), ((), ()))
_NT = (((1,), (1,)), ((), ()))
_TN = (((0,), (0,)), ((), ()))


def _gla_fwd(z, a2, ab, S, reverse, name):
    T = z.shape[0]
    E = T // S
    n = S // CHUNK
    tri, _, oseg = _gla_consts(reverse)

    def body(q_ref, k_ref, v_ref, lr_ref, a2_ref, ab_ref, tri_ref, oseg_ref, o_ref, a_ref, st_ref, st, b_s, q_s, k_s):
        @pl.when(pl.program_id(1) == 0)
        def _():
            st[...] = jnp.zeros_like(st)

        q = q_ref[...] * (GLA_DK ** -0.5)
        k = k_ref[...]
        v = v_ref[...]
        _, g = _log_decay(lr_ref[...], a2_ref[...], ab_ref[...])
        b = _dotf(tri_ref[...], g, _NN)
        bl_row = jnp.sum(g, axis=0, keepdims=True)
        bl_col = _dotf(g, jnp.ones((CHUNK, GLA_DV), F32), _TN)
        b_s[...] = b
        q_s[...] = q
        k_s[...] = k
        lane = lax.broadcasted_iota(jnp.int32, (1, _HK), 1) % GLA_DK

        rowi = lax.broadcasted_iota(jnp.int32, (CHUNK, 1), 0)
        blk0 = (rowi // _SUB) * _SUB

        def cols(jj, a):
            ts = []
            for u in range(_COLS):
                jp = jj * _COLS + u
                tiles = []
                for s in range(_NSUB):
                    rs_ = slice(s * _SUB, (s + 1) * _SUB)
                    bj = b_s[pl.ds(s * _SUB + jp, 1), :]
                    kj = k_s[pl.ds(s * _SUB + jp, 1), :]
                    tiles.append(q_s[rs_, :] * jnp.exp(jnp.minimum(b_s[rs_, :] - bj, 0.0)) * kj)
                ts.append(jnp.concatenate(tiles, axis=0).astype(BF16))
            r = jnp.dot(jnp.concatenate(ts, axis=0), oseg_ref[...], preferred_element_type=F32)
            for u in range(_COLS):
                a = jnp.where(lane == blk0 + (jj * _COLS + u), r[u * CHUNK:(u + 1) * CHUNK, :], a)
            return a

        a = lax.fori_loop(0, _SUB // _COLS, cols, jnp.zeros((CHUNK, _HK), F32))
        keep = (rowi <= lane) if reverse else (rowi >= lane)
        a = jnp.where(keep, a, 0.0)
        cross = []
        for s in range(_NSUB):
            if s in _gla_cross_blocks(reverse):
                qt, _, _, _, nmat = _gla_cross_terms(s, reverse, b_s, q_s, k_s, oseg_ref)
                cross.append(lax.dot_general(qt.astype(BF16), nmat, _NT, preferred_element_type=F32))
            else:
                cross.append(jnp.zeros((_SUB, _HK), F32))
        a = a + jnp.concatenate(cross, axis=0)
        a_ref[...] = a
        st_ref[0] = st[...]
        qb = q * jnp.exp(b)
        kd = k * jnp.exp(bl_row - b)
        for h in range(GLA_HEADS):
            ks_ = slice(h * GLA_DK, (h + 1) * GLA_DK)
            vs_ = slice(h * GLA_DV, (h + 1) * GLA_DV)
            s_h = st[ks_, :]
            o_ref[:, vs_] = _dotb(qb[:, ks_], s_h, _NN) + _dotb(a[:, ks_], v[:, vs_], _NN)
            st[ks_, :] = s_h * jnp.exp(bl_col[ks_, :]) + _dotb(kd[:, ks_], v[:, vs_], _TN)

    def rowblk(e, c):
        return e * n + ((n - 1 - c) if reverse else c)

    return pl.pallas_call(
        body, out_shape=(jax.ShapeDtypeStruct((T, _HV), F32), jax.ShapeDtypeStruct((T, _HK), F32),
                         jax.ShapeDtypeStruct((T // CHUNK, _HK, GLA_DV), F32)),
        grid=(E, n),
        in_specs=[pl.BlockSpec((CHUNK, _HK), lambda e, c: (rowblk(e, c), C_Q // _HK)),
                  pl.BlockSpec((CHUNK, _HK), lambda e, c: (rowblk(e, c), C_K // _HK)),
                  pl.BlockSpec((CHUNK, _HV), lambda e, c: (rowblk(e, c), C_V // _HV)),
                  pl.BlockSpec((CHUNK, 128), lambda e, c: (rowblk(e, c), LR_OFF // 128)),
                  pl.BlockSpec((128, _HK), lambda e, c: (0, 0)),
                  pl.BlockSpec((1, _HK), lambda e, c: (0, 0)),
                  pl.BlockSpec((CHUNK, CHUNK), lambda e, c: (0, 0)),
                  pl.BlockSpec((_HK, _HK), lambda e, c: (0, 0))],
        out_specs=(pl.BlockSpec((CHUNK, _HV), lambda e, c: (rowblk(e, c), 0)),
                   pl.BlockSpec((CHUNK, _HK), lambda e, c: (rowblk(e, c), 0)),
                   pl.BlockSpec((1, _HK, GLA_DV), lambda e, c: (rowblk(e, c), 0, 0))),
        scratch_shapes=[pltpu.VMEM((_HK, GLA_DV), F32)] + [pltpu.VMEM((CHUNK, _HK), F32)] * 3,
        name=name, compiler_params=_cp(("parallel", "arbitrary")))(z, z, z, z, a2, ab, tri, oseg)


def _gla_bwd(z, a2, ab, att, states, do, prev, S, reverse, name):
    T = z.shape[0]
    E = T // S
    n = S // CHUNK
    tri, tri_t, oseg = _gla_consts(reverse)
    has_prev = prev is not None
    odt = BF16 if has_prev else F32

    def body(*refs):
        (q_ref, k_ref, v_ref, lr_ref, a2_ref, ab_ref, tri_ref, trit_ref, oseg_ref, att_ref, st_ref, do_ref) = refs[:12]
        refs = refs[12:]
        if has_prev:
            pq_ref, pk_ref, pv_ref, pl_ref = refs[:4]
            refs = refs[4:]
        (dq_ref, dk_ref, dv_ref, dlr_ref, da2_ref, dab_ref, dst, b_s, q_s, k_s, da_s, dqb_s, dkd_s, dk3_s, dbn_s, dsp_s) = refs
        first = (pl.program_id(0) == 0) & (pl.program_id(1) == 0)

        @pl.when(first)
        def _():
            da2_ref[...] = jnp.zeros_like(da2_ref)
            dab_ref[...] = jnp.zeros_like(dab_ref)

        @pl.when(pl.program_id(1) == 0)
        def _():
            dst[...] = jnp.zeros_like(dst)

        q = q_ref[...] * (GLA_DK ** -0.5)
        k = k_ref[...]
        v = v_ref[...]
        lr = lr_ref[...]
        zg, g = _log_decay(lr, a2_ref[...], ab_ref[...])
        b = _dotf(tri_ref[...], g, _NN)
        bl_row = jnp.sum(g, axis=0, keepdims=True)
        bl_col = _dotf(g, jnp.ones((CHUNK, GLA_DV), F32), _TN)
        eb = jnp.exp(b)
        ekd = jnp.exp(bl_row - b)
        qb = q * eb
        kd = k * ekd
        b_s[...] = b
        q_s[...] = q
        k_s[...] = k
        att = att_ref[...]
        s_all = st_ref[0]
        dsn = dst[...]
        e_col = jnp.exp(bl_col)
        do = do_ref[...]
        lane = lax.broadcasted_iota(jnp.int32, (1, _HK), 1) % GLA_DK
        rowi = lax.broadcasted_iota(jnp.int32, (CHUNK, 1), 0)
        keep = (rowi <= lane) if reverse else (rowi >= lane)
        for h in range(GLA_HEADS):
            ks_ = slice(h * GLA_DK, (h + 1) * GLA_DK)
            vs_ = slice(h * GLA_DV, (h + 1) * GLA_DV)
            do_h = do[:, vs_]
            s_h = s_all[ks_, :]
            dsn_h = dsn[ks_, :]
            dqb_s[:, ks_] = _dotb(do_h, s_h, _NT)
            dsp_s[ks_, :] = _dotb(qb[:, ks_], do_h, _TN) + dsn_h * e_col[ks_, :]
            da_s[:, ks_] = _dotb(do_h, v[:, vs_], _NT)
            dv_h = _dotb(att[:, ks_], do_h, _TN) + _dotb(kd[:, ks_], dsn_h, _NN)
            if has_prev:
                dv_h = dv_h + pv_ref[:, vs_]
            dv_ref[:, vs_] = dv_h.astype(odt)
            dkd_s[:, ks_] = _dotb(v[:, vs_], dsn_h, _NT)
        da_s[...] = jnp.where(keep, da_s[...], 0.0)
        dqb = dqb_s[...]
        dkd = dkd_s[...]
        x = dsn * s_all * e_col
        dbl_row = _dotf(jnp.ones((8, GLA_DV), F32), x, _NT)[0:1, :] + jnp.sum(dkd * kd, axis=0, keepdims=True)

        blk0 = (rowi // _SUB) * _SUB

        def cols(jj, carry):
            dq3, db3 = list(carry[:_NSUB]), list(carry[_NSUB:])
            sel = [jnp.where(lane == blk0 + (jj * _COLS + u), da_s[...], 0.0).astype(BF16) for u in range(_COLS)]
            dcols = jnp.dot(jnp.concatenate(sel, axis=0), oseg_ref[...], preferred_element_type=F32)
            for u in range(_COLS):
                jp = jj * _COLS + u
                for s in range(_NSUB):
                    rs_ = slice(s * _SUB, (s + 1) * _SUB)
                    bj = b_s[pl.ds(s * _SUB + jp, 1), :]
                    kj = k_s[pl.ds(s * _SUB + jp, 1), :]
                    tm_ = dcols[u * CHUNK + s * _SUB:u * CHUNK + (s + 1) * _SUB, :] * jnp.exp(jnp.minimum(b_s[rs_, :] - bj, 0.0))
                    dq3[s] = dq3[s] + tm_ * kj
                    gq = tm_ * q_s[rs_, :]
                    dk3_s[pl.ds(s * _SUB + jp, 1), :] = jnp.sum(gq, axis=0, keepdims=True)
                    w = gq * kj
                    dbn_s[pl.ds(s * _SUB + jp, 1), :] = jnp.sum(w, axis=0, keepdims=True)
                    db3[s] = db3[s] + w
            return tuple(dq3) + tuple(db3)

        zero = jnp.zeros((_SUB, _HK), F32)
        acc = lax.fori_loop(0, _SUB // _COLS, cols, (zero,) * (2 * _NSUB))
        dq3 = jnp.concatenate(acc[:_NSUB], axis=0)
        db3 = jnp.concatenate(acc[_NSUB:], axis=0)
        head = lax.broadcasted_iota(jnp.int32, (1, _HK), 1) // GLA_DK
        dq_x, db_x = [], []
        dk_x = jnp.zeros((CHUNK, _HK), F32)
        db_k = jnp.zeros((CHUNK, _HK), F32)
        for s in range(_NSUB):
            if s not in _gla_cross_blocks(reverse):
                dq_x.append(zero)
                db_x.append(zero)
                continue
            r0 = s * _SUB
            qt, eq, kt, ek, nmat = _gla_cross_terms(s, reverse, b_s, q_s, k_s, oseg_ref)
            seen = (lane >= r0 + _SUB) if reverse else (lane < r0)
            dax = jnp.where(seen, da_s[r0:r0 + _SUB, :], 0.0).astype(BF16)
            dqt = jnp.dot(dax, nmat, preferred_element_type=F32)
            full = lax.dot_general(dax, qt.astype(BF16), _TN, preferred_element_type=F32)
            dkt = full[0:CHUNK, :]
            for h in range(1, GLA_HEADS):
                dkt = jnp.where(head == h, full[h * CHUNK:(h + 1) * CHUNK, :], dkt)
            dq_x.append(dqt * eq)
            db_x.append(dqt * qt)
            dk_x = dk_x + dkt * ek
            db_k = db_k + dkt * kt
        dq = (dqb * eb + dq3 + jnp.concatenate(dq_x, axis=0)) * (GLA_DK ** -0.5)
        dk = dkd * ekd + dk3_s[...] + dk_x
        db = dqb * qb - dkd * kd + db3 - dbn_s[...] + jnp.concatenate(db_x, axis=0) - db_k
        dg = _dotf(trit_ref[...], db, _NN) + dbl_row
        sneg = 1.0 / (1.0 + jnp.exp(zg))
        dzg = dg * sneg * (1.0 / GLA_TAU)
        dlr = _dotf(dzg, a2_ref[...], _NT)
        da2_ref[...] += _dotf(lr, dzg, _TN)
        dab_ref[...] += jnp.sum(dzg, axis=0, keepdims=True)
        if has_prev:
            dq = dq + pq_ref[...]
            dk = dk + pk_ref[...]
            dlr = dlr + pl_ref[...]
        dq_ref[...] = dq.astype(odt)
        dk_ref[...] = dk.astype(odt)
        dlr_ref[...] = dlr.astype(odt)
        dst[...] = dsp_s[...]

    def rowblk(e, c):
        return e * n + (c if reverse else (n - 1 - c))

    hk = pl.BlockSpec((CHUNK, _HK), lambda e, c: (rowblk(e, c), 0))
    hv = pl.BlockSpec((CHUNK, _HV), lambda e, c: (rowblk(e, c), 0))
    l128 = pl.BlockSpec((CHUNK, 128), lambda e, c: (rowblk(e, c), 0))
    in_specs = [pl.BlockSpec((CHUNK, _HK), lambda e, c: (rowblk(e, c), C_Q // _HK)),
                pl.BlockSpec((CHUNK, _HK), lambda e, c: (rowblk(e, c), C_K // _HK)),
                pl.BlockSpec((CHUNK, _HV), lambda e, c: (rowblk(e, c), C_V // _HV)),
                pl.BlockSpec((CHUNK, 128), lambda e, c: (rowblk(e, c), LR_OFF // 128)),
                pl.BlockSpec((128, _HK), lambda e, c: (0, 0)),
                pl.BlockSpec((1, _HK), lambda e, c: (0, 0)),
                pl.BlockSpec((CHUNK, CHUNK), lambda e, c: (0, 0)),
                pl.BlockSpec((CHUNK, CHUNK), lambda e, c: (0, 0)),
                pl.BlockSpec((_HK, _HK), lambda e, c: (0, 0)),
                hk, pl.BlockSpec((1, _HK, GLA_DV), lambda e, c: (rowblk(e, c), 0, 0)), hv]
    args = [z, z, z, z, a2, ab, tri, tri_t, oseg, att, states, do]
    if has_prev:
        in_specs += [hk, hk, hv, l128]
        args += list(prev)
    return pl.pallas_call(
        body, out_shape=(jax.ShapeDtypeStruct((T, _HK), odt), jax.ShapeDtypeStruct((T, _HK), odt),
                         jax.ShapeDtypeStruct((T, _HV), odt), jax.ShapeDtypeStruct((T, 128), odt),
                         jax.ShapeDtypeStruct((128, _HK), F32), jax.ShapeDtypeStruct((1, _HK), F32)),
        grid=(E, n), in_specs=in_specs,
        out_specs=(hk, hk, hv, l128, pl.BlockSpec((128, _HK), lambda e, c: (0, 0)), pl.BlockSpec((1, _HK), lambda e, c: (0, 0))),
        scratch_shapes=[pltpu.VMEM((_HK, GLA_DV), F32)] + [pltpu.VMEM((CHUNK, _HK), F32)] * 8 + [pltpu.VMEM((_HK, GLA_DV), F32)],
        name=name, compiler_params=_cp(("arbitrary", "arbitrary")))(*args)


def _gla_norm_fwd(of, ob, og, name):
    T = of.shape[0]
    tm = 256

    def body(f_ref, b_ref, g_ref, o_ref):
        for h in range(GLA_HEADS):
            vs_ = slice(h * GLA_DV, (h + 1) * GLA_DV)
            o = f_ref[:, vs_] + b_ref[:, vs_]
            o_ref[:, vs_] = o * lax.rsqrt(jnp.mean(o * o, axis=-1, keepdims=True) + EPS) * g_ref[:, vs_]

    row = pl.BlockSpec((tm, _HV), lambda i: (i, 0))
    vec = pl.BlockSpec((1, _HV), lambda i: (0, 0))
    return pl.pallas_call(body, out_shape=jax.ShapeDtypeStruct((T, _HV), F32), grid=(T // tm,),
                          in_specs=[row, row, vec], out_specs=row, name=name, compiler_params=_cp(("parallel",)))(of, ob, og)


def _gla_norm_bwd(of, ob, og, dpre, name):
    T = of.shape[0]
    tm = 256

    def body(f_ref, b_ref, g_ref, dp_ref, do_ref, dg_ref):
        @pl.when(pl.program_id(0) == 0)
        def _():
            dg_ref[...] = jnp.zeros_like(dg_ref)

        for h in range(GLA_HEADS):
            vs_ = slice(h * GLA_DV, (h + 1) * GLA_DV)
            o = f_ref[:, vs_] + b_ref[:, vs_]
            r = lax.rsqrt(jnp.mean(o * o, axis=-1, keepdims=True) + EPS)
            xh = o * r
            dp = dp_ref[:, vs_]
            dxh = dp * g_ref[:, vs_]
            do_ref[:, vs_] = r * (dxh - xh * jnp.mean(dxh * xh, axis=-1, keepdims=True))
            dg_ref[:, vs_] += jnp.sum(dp * xh, axis=0, keepdims=True)

    row = pl.BlockSpec((tm, _HV), lambda i: (i, 0))
    vec = pl.BlockSpec((1, _HV), lambda i: (0, 0))
    return pl.pallas_call(
        body, out_shape=(jax.ShapeDtypeStruct((T, _HV), F32), jax.ShapeDtypeStruct((1, _HV), F32)), grid=(T // tm,),
        in_specs=[row, row, vec, row], out_specs=(row, vec), name=name, compiler_params=_cp(("arbitrary",)))(of, ob, og, dpre)


_ANY = pl.BlockSpec(memory_space=pl.ANY)


def _coords():
    return lax.axis_index("x"), lax.axis_index("y"), lax.axis_index("c")


def _other_chips(x, y):
    return ((1 - x, y), (x, 1 - y), (1 - x, 1 - y))


def _gather_weights(arrays, name):
    n = len(arrays)

    def body(*refs):
        srcs, dsts = refs[:n], refs[n:2 * n]
        send_sems, recv_sems, local_sems = refs[2 * n:]
        x, y, c = _coords()
        me = 2 * x + y
        loc = [pltpu.make_async_copy(s, d.at[me], local_sems.at[i]) for i, (s, d) in enumerate(zip(srcs, dsts))]
        for cp in loc:
            cp.start()
        ici = []
        for j, (px, py) in enumerate(_other_chips(x, y)):
            for i, (s, d) in enumerate(zip(srcs, dsts)):
                ici.append(pltpu.make_async_remote_copy(
                    src_ref=s.at[c], dst_ref=d.at[me, c], send_sem=send_sems.at[n * j + i], recv_sem=recv_sems.at[n * j + i],
                    device_id=(px, py, c), device_id_type=MESH))
        for cp in ici:
            cp.start()
        fwd = []
        for j, (px, py) in enumerate(_other_chips(x, y)):
            for i, d in enumerate(dsts):
                ici[n * j + i].wait_recv()
                half = d.at[2 * px + py, c]
                cp = pltpu.make_async_remote_copy(
                    src_ref=half, dst_ref=half, send_sem=send_sems.at[3 * n + n * j + i], recv_sem=recv_sems.at[3 * n + n * j + i],
                    device_id=(x, y, 1 - c), device_id_type=MESH)
                cp.start()
                fwd.append(cp)
        for cp in fwd:
            cp.wait_recv()
        for cp in ici + fwd:
            cp.wait_send()
        for cp in loc:
            cp.wait()

    return pl.pallas_call(
        body, out_shape=tuple(jax.ShapeDtypeStruct((4,) + a.shape, a.dtype) for a in arrays),
        in_specs=[_ANY] * n, out_specs=(_ANY,) * n,
        scratch_shapes=[pltpu.SemaphoreType.DMA((6 * n,)), pltpu.SemaphoreType.DMA((6 * n,)), pltpu.SemaphoreType.DMA((n,))],
        name=name)(*arrays)


def _sibling_exchange(layered, whole, name):
    nl, n = len(layered), len(layered) + len(whole)

    def body(*refs):
        srcs, dsts = refs[:n], refs[n:2 * n]
        send_sems, recv_sems = refs[2 * n:]
        x, y, c = _coords()
        rem = [pltpu.make_async_remote_copy(src_ref=(s.at[1 - c] if i < nl else s), dst_ref=d, send_sem=send_sems.at[i],
                                            recv_sem=recv_sems.at[i], device_id=(x, y, 1 - c), device_id_type=MESH)
               for i, (s, d) in enumerate(zip(srcs, dsts))]
        for cp in rem:
            cp.start()
        for cp in rem:
            cp.wait()

    outs = [jax.ShapeDtypeStruct(a.shape[1:], a.dtype) for a in layered] + [jax.ShapeDtypeStruct(a.shape, a.dtype) for a in whole]
    return pl.pallas_call(
        body, out_shape=tuple(outs), in_specs=[_ANY] * n, out_specs=(_ANY,) * n,
        scratch_shapes=[pltpu.SemaphoreType.DMA((n,)), pltpu.SemaphoreType.DMA((n,))], name=name)(*layered, *whole)


def _chip_exchange(scatter, bcast, name):
    ns, n = len(scatter), len(scatter) + len(bcast)

    def body(*refs):
        srcs, dsts = refs[:n], refs[n:2 * n]
        send_sems, recv_sems, local_sems = refs[2 * n:]
        x, y, c = _coords()
        me = 2 * x + y
        loc = [pltpu.make_async_copy((s.at[me] if i < ns else s), d.at[me], local_sems.at[i])
               for i, (s, d) in enumerate(zip(srcs, dsts))]
        for cp in loc:
            cp.start()
        rem = []
        for j, (px, py) in enumerate(_other_chips(x, y)):
            for i, (s, d) in enumerate(zip(srcs, dsts)):
                rem.append(pltpu.make_async_remote_copy(
                    src_ref=(s.at[2 * px + py] if i < ns else s), dst_ref=d.at[me], send_sem=send_sems.at[n * j + i],
                    recv_sem=recv_sems.at[n * j + i], device_id=(px, py, c), device_id_type=MESH))
        for cp in rem:
            cp.start()
        for cp in rem:
            cp.wait()
        for cp in loc:
            cp.wait()

    outs = [jax.ShapeDtypeStruct(a.shape, a.dtype) for a in scatter] + [jax.ShapeDtypeStruct((4,) + a.shape, a.dtype) for a in bcast]
    return pl.pallas_call(
        body, out_shape=tuple(outs), in_specs=[_ANY] * n, out_specs=(_ANY,) * n,
        scratch_shapes=[pltpu.SemaphoreType.DMA((3 * n,)), pltpu.SemaphoreType.DMA((3 * n,)), pltpu.SemaphoreType.DMA((n,))],
        name=name)(*scatter, *bcast)


def _sum_slots(r, name):
    n, R, C = r.shape
    tm = min(256, R)

    def body(r_ref, o_ref):
        acc = r_ref[0].astype(F32)
        for i in range(1, n):
            acc = acc + r_ref[i].astype(F32)
        o_ref[...] = acc

    return pl.pallas_call(body, out_shape=jax.ShapeDtypeStruct((R, C), F32), grid=(R // tm,),
                          in_specs=[pl.BlockSpec((n, tm, C), lambda i: (0, i, 0))], out_specs=pl.BlockSpec((tm, C), lambda i: (i, 0)),
                          name=name, compiler_params=_cp(("parallel",)))(r)


def _add2(a, b, out_dtype, tm, name):
    R, C = a.shape
    tm = min(tm, R)

    def body(a_ref, b_ref, o_ref):
        o_ref[...] = (a_ref[...].astype(F32) + b_ref[...].astype(F32)).astype(out_dtype)

    blk = pl.BlockSpec((tm, C), lambda i: (i, 0))
    return pl.pallas_call(body, out_shape=jax.ShapeDtypeStruct((R, C), out_dtype), grid=(R // tm,), in_specs=[blk, blk],
                          out_specs=blk, name=name, compiler_params=_cp(("parallel",)))(a, b)


def _adamw_math(w, g, m, v):
    m = ADAM_B1 * m + (1.0 - ADAM_B1) * g
    v = ADAM_B2 * v + (1.0 - ADAM_B2) * (g * g)
    m_hat = m / (1.0 - ADAM_B1 ** ADAM_STEP)
    v_hat = v / (1.0 - ADAM_B2 ** ADAM_STEP)
    delta = -ADAM_LR * (m_hat / (jnp.sqrt(v_hat) + ADAM_EPS) + ADAM_WD * w)
    return delta, m, v


def _adamw(w, gs, m, v, tm, name):
    R, C = w.shape

    def body(*refs):
        w_ref = refs[0]
        g_refs = refs[1:1 + len(gs)]
        m_ref, v_ref, g_out, d_out, m_out, v_out = refs[1 + len(gs):]
        g = None
        for gr in g_refs:
            parts = [gr[i] for i in range(gr.shape[0])] if len(gr.shape) == 3 else [gr[...]]
            for p in parts:
                g = p if g is None else g + p
        d, mn, vn = _adamw_math(w_ref[...], g, m_ref[...], v_ref[...])
        g_out[...] = g
        d_out[...] = d
        m_out[...] = mn
        v_out[...] = vn

    blk = pl.BlockSpec((tm, C), lambda i: (i, 0))
    g_specs = [pl.BlockSpec((g.shape[0], tm, C), lambda i: (0, i, 0)) if g.ndim == 3 else blk for g in gs]
    return pl.pallas_call(
        body, out_shape=tuple(jax.ShapeDtypeStruct((R, C), F32) for _ in range(4)), grid=(R // tm,),
        in_specs=[blk] + g_specs + [blk, blk], out_specs=(blk,) * 4, name=name,
        compiler_params=_cp(("parallel",)))(w, *gs, m, v)


WEIGHTS = ("norm_g", "w_in", "conv_w", "conv_b", "conv_ln_g", "conv_ln_b", "na_q_g", "na_k_g", "na_rpb", "gla_a2_f",
           "gla_ab_f", "gla_a2_b", "gla_ab_b", "gla_o_g", "pool_w", "pool_scale", "w_out")
_REPL = ("norm_g", "conv_b", "conv_ln_g", "conv_ln_b", "na_q_g", "na_k_g", "na_rpb", "gla_ab_f", "gla_ab_b", "gla_o_g",
         "pool_w", "pool_scale")
_SHARD_SMALL = ("conv_w", "gla_a2_f", "gla_a2_b")
_PACK_ROWS = 8 * 128


def _pack(arrs):
    flat = jnp.concatenate([a.reshape(-1) for a in arrs])
    n = -(-flat.shape[0] // _PACK_ROWS) * _PACK_ROWS
    return jnp.pad(flat, (0, n - flat.shape[0])).reshape(-1, 128)


def _unpack(p, shapes):
    flat = p.reshape(-1)
    out, o = [], 0
    for s in shapes:
        n = int(np.prod(s))
        out.append(flat[o:o + n].reshape(s))
        o += n
    return out


def _to_layout(w):
    pad = jnp.zeros(w.shape[:-1] + (NZ - N_IN,), w.dtype)
    return jnp.concatenate([w[..., :5120], w[..., 5152:6176], w[..., 5120:5152], pad], axis=-1)


def _from_layout(w):
    return jnp.concatenate([w[..., :5120], w[..., LR_OFF:LR_OFF + 32], w[..., 5120:LR_OFF]], axis=-1)


def _reduce_gradients(p_a, p_b, small_g, ci):
    two = lambda a: a.reshape(-1, a.shape[-1])
    s_a, s_b, s_small = _sibling_exchange((p_a, p_b), (small_g,), "grad_to_sibling")
    mine = lambda a: lax.dynamic_index_in_dim(a, ci, 0, keepdims=False)
    c_a = _add2(two(mine(p_a)), two(s_a), BF16, 256, "chip_sum_a").reshape(s_a.shape)
    c_b = _add2(two(mine(p_b)), two(s_b), BF16, 256, "chip_sum_b").reshape(s_b.shape)
    c_small = _add2(small_g, s_small, F32, small_g.shape[0], "chip_sum_small")
    r_a, r_b, r_small = _chip_exchange((c_a, c_b), (c_small,), "grad_to_owner")
    own_a = _sum_slots(r_a, "sum_a")
    own_b = _sum_slots(r_b, "sum_b")
    sib_a, sib_b = _sibling_exchange((), (own_a, own_b), "reduced_to_sibling")
    by_layer = lambda own, sib: jnp.where(ci == 0, jnp.stack([own, sib]), jnp.stack([sib, own]))
    return by_layer(own_a, sib_a), by_layer(own_b, sib_b), r_small


def _layer_fwd(l, x, P, S):
    n = f"l{l}_"
    h = _rmsnorm_fwd(x, P["norm_g"], n + "rms_fwd")
    z = _matmul(h, P["w_in"], dims="nn", out_dtype=F32, tm=512, tn=1280, tk=D_MODEL, name=n + "mm_z")
    yc = _conv_fwd(z, P["conv_w32"], P["conv_b"], S, n + "conv_fwd")
    pre_a = _ln_silu_fwd(yc, P["conv_ln_g"], P["conv_ln_b"], n + "ln_fwd")
    pre_b = _na_fwd(z, P["na_q_g"], P["na_k_g"], P["na_bias"], S, n + "na_fwd")
    of, af, sf = _gla_fwd(z, P["a2_f"], P["gla_ab_f"], S, False, n + "gla_fwd_f")
    ob, ab, sb = _gla_fwd(z, P["a2_b"], P["gla_ab_b"], S, True, n + "gla_fwd_b")
    pre_c = _gla_norm_fwd(of, ob, P["gla_o_g"], n + "gla_norm_fwd")
    pre_d = _pool_fwd(z, P["pool_w_bf"], P["pool_scale"], S, n + "pool_fwd")
    pres = (pre_a, pre_b, pre_c, pre_d)
    y = _gate_fwd(pres, z, n + "gate_fwd")
    out = _matmul(y, P["w_out"], dims="nn", out_dtype=F32, tm=512, tn=1024, tk=D_MODEL, name=n + "mm_out", res=x)
    return out, dict(x=x, h=h, z=z, yc=yc, pres=pres, of=of, af=af, sf=sf, ob=ob, ab=ab, sb=sb, y=y)


def _layer_bwd(l, dout, dout_bf, sv, P, S):
    n = f"l{l}_"
    z = sv["z"]
    T = z.shape[0]
    dy = _matmul(dout_bf, P["w_out"], dims="nt", out_dtype=F32, tm=512, tn=1024, tk=D_MODEL, name=n + "mm_dy")
    d_w_out = _matmul(sv["y"], dout_bf, dims="tn", out_dtype=BF16, tm=512, tn=1024, tk=1024, name=n + "mm_dwout")
    dpa, dpb, dpc, dpd, dga, dgb, dgc, dgd = _gate_bwd(dy, sv["pres"], z, n + "gate_bwd")
    dyc, d_ln_g, d_ln_b = _ln_silu_bwd(sv["yc"], P["conv_ln_g"], P["conv_ln_b"], dpa, n + "ln_bwd")
    dval, dglu, d_cw, d_cb = _conv_bwd(z, P["conv_w32"], dyc, S, n + "conv_bwd")
    dq, dk, dv, dbias, d_qg, d_kg = _na_bwd(z, P["na_q_g"], P["na_k_g"], P["na_bias"], dpb, S, n + "na_bwd")
    d_rpb = _na_rpb_grad(dbias, n + "na_rpb")
    do, d_og = _gla_norm_bwd(sv["of"], sv["ob"], P["gla_o_g"], dpc, n + "gla_norm_bwd")
    part = _gla_bwd(z, P["a2_f"], P["gla_ab_f"], sv["af"], sv["sf"], do, None, S, False, n + "gla_bwd_f")
    d_a2f, d_abf = part[4], part[5]
    dcq, dck, dcv, dlr, d_a2b, d_abb = _gla_bwd(z, P["a2_b"], P["gla_ab_b"], sv["ab"], sv["sb"], do, part[:4], S, True,
                                                n + "gla_bwd_b")
    dd, d_pw, d_ps = _pool_bwd(z, P["pool_w_bf"], P["pool_scale"], dpd, S, n + "pool_bwd")
    dz = jnp.concatenate([dval, dglu, dga, dq, dk, dv, dgb, dcq, dck, dcv, dgc, dd, dgd, dlr,
                          jnp.zeros((T, NZ - LR_OFF - 128), BF16)], axis=1)
    dh = _matmul(dz, P["w_in"], dims="nt", out_dtype=F32, tm=512, tn=1024, tk=1280, name=n + "mm_dh")
    d_w_in = _matmul(sv["h"], dz, dims="tn", out_dtype=BF16, tm=512, tn=1280, tk=1024, name=n + "mm_dwin")
    dx, dx_bf, d_ng = _rmsnorm_bwd(sv["x"], P["norm_g"], dh, dout, n + "rms_bwd")
    grads = dict(norm_g=d_ng[0], w_in=d_w_in, conv_w=d_cw[:CONV_K], conv_b=d_cb[0], conv_ln_g=d_ln_g[0], conv_ln_b=d_ln_b[0],
                 na_q_g=d_qg.reshape(NA_HEADS, NA_DH), na_k_g=d_kg.reshape(NA_HEADS, NA_DH), na_rpb=d_rpb,
                 gla_a2_f=d_a2f[0:GLA_RANK], gla_ab_f=d_abf[0], gla_a2_b=d_a2b[GLA_RANK:2 * GLA_RANK], gla_ab_b=d_abb[0],
                 gla_o_g=d_og.reshape(GLA_HEADS, GLA_DV), pool_w=d_pw, pool_scale=d_ps[0], w_out=d_w_out)
    return dx, dx_bf, grads


def kernel(x, norm_g, w_in, conv_w, conv_b, conv_ln_g, conv_ln_b, na_q_g, na_k_g, na_rpb, gla_a2_f, gla_ab_f, gla_a2_b, gla_ab_b, gla_o_g, pool_w, pool_scale, w_out, loss_target, m_norm_g, m_w_in, m_conv_w, m_conv_b, m_conv_ln_g, m_conv_ln_b, m_na_q_g, m_na_k_g, m_na_rpb, m_gla_a2_f, m_gla_ab_f, m_gla_a2_b, m_gla_ab_b, m_gla_o_g, m_pool_w, m_pool_scale, m_w_out, v_norm_g, v_w_in, v_conv_w, v_conv_b, v_conv_ln_g, v_conv_ln_b, v_na_q_g, v_na_k_g, v_na_rpb, v_gla_a2_f, v_gla_ab_f, v_gla_a2_b, v_gla_ab_b, v_gla_o_g, v_pool_w, v_pool_scale, v_w_out):
    W = dict(norm_g=norm_g, w_in=w_in, conv_w=conv_w, conv_b=conv_b, conv_ln_g=conv_ln_g, conv_ln_b=conv_ln_b, na_q_g=na_q_g,
             na_k_g=na_k_g, na_rpb=na_rpb, gla_a2_f=gla_a2_f, gla_ab_f=gla_ab_f, gla_a2_b=gla_a2_b, gla_ab_b=gla_ab_b,
             gla_o_g=gla_o_g, pool_w=pool_w, pool_scale=pool_scale, w_out=w_out)
    M = dict(norm_g=m_norm_g, w_in=m_w_in, conv_w=m_conv_w, conv_b=m_conv_b, conv_ln_g=m_conv_ln_g, conv_ln_b=m_conv_ln_b,
             na_q_g=m_na_q_g, na_k_g=m_na_k_g, na_rpb=m_na_rpb, gla_a2_f=m_gla_a2_f, gla_ab_f=m_gla_ab_f, gla_a2_b=m_gla_a2_b,
             gla_ab_b=m_gla_ab_b, gla_o_g=m_gla_o_g, pool_w=m_pool_w, pool_scale=m_pool_scale, w_out=m_w_out)
    V = dict(norm_g=v_norm_g, w_in=v_w_in, conv_w=v_conv_w, conv_b=v_conv_b, conv_ln_g=v_conv_ln_g, conv_ln_b=v_conv_ln_b,
             na_q_g=v_na_q_g, na_k_g=v_na_k_g, na_rpb=v_na_rpb, gla_a2_f=v_gla_a2_f, gla_ab_f=v_gla_ab_f, gla_a2_b=v_gla_a2_b,
             gla_ab_b=v_gla_ab_b, gla_o_g=v_gla_o_g, pool_w=v_pool_w, pool_scale=v_pool_scale, w_out=v_w_out)
    E, S, D = x.shape
    T = E * S
    L = DEPTH
    xi, yi, ci = _coords()
    chip = 2 * xi + yi
    cw_sh, a2_sh = conv_w.shape[-1], gla_a2_f.shape[-1]

    small_sh = jnp.concatenate([
        jnp.pad(conv_w, ((0, 0), (0, 1), (0, 0))),
        jnp.pad(gla_a2_f, ((0, 0), (0, 0), (0, 128 - a2_sh))),
        jnp.pad(gla_a2_b, ((0, 0), (0, 0), (0, 128 - a2_sh)))], axis=1)
    g_win, g_wout, g_small = _gather_weights((w_in.astype(BF16), w_out.astype(BF16), small_sh), "gather_weights")
    w_in_full = _to_layout(jnp.transpose(g_win, (1, 2, 0, 3)).reshape(L, D, N_IN))
    w_out_full = jnp.transpose(g_wout, (1, 0, 2, 3)).reshape(L, D, D)
    conv_w_full = jnp.transpose(g_small[:, :, 0:32, :], (1, 2, 0, 3)).reshape(L, 32, 4 * cw_sh)
    a2f_full = jnp.transpose(g_small[:, :, 32:48, :a2_sh], (1, 2, 0, 3)).reshape(L, GLA_RANK, 4 * a2_sh)
    a2b_full = jnp.transpose(g_small[:, :, 48:64, :a2_sh], (1, 2, 0, 3)).reshape(L, GLA_RANK, 4 * a2_sh)

    params = []
    for l in range(L):
        params.append(dict(
            norm_g=norm_g[l][None], w_in=w_in_full[l], w_out=w_out_full[l], conv_w32=conv_w_full[l], conv_b=conv_b[l][None],
            conv_ln_g=conv_ln_g[l][None], conv_ln_b=conv_ln_b[l][None], na_q_g=na_q_g[l].reshape(1, GROUP_W),
            na_k_g=na_k_g[l].reshape(1, GROUP_W), na_bias=_na_bias(na_rpb[l], f"l{l}_na_bias"),
            a2_f=jnp.zeros((128, _HK), F32).at[0:GLA_RANK].set(a2f_full[l]),
            a2_b=jnp.zeros((128, _HK), F32).at[GLA_RANK:2 * GLA_RANK].set(a2b_full[l]),
            gla_ab_f=gla_ab_f[l][None], gla_ab_b=gla_ab_b[l][None], gla_o_g=gla_o_g[l].reshape(1, GROUP_W),
            pool_w_bf=pool_w[l].astype(BF16), pool_scale=pool_scale[l][None]))

    act = x.reshape(T, D)
    saved = []
    for l in range(L):
        act, sv = _layer_fwd(l, act, params[l], S)
        saved.append(sv)
    dact, dact_bf, loss_loc = _loss_head(act, loss_target.reshape(T, D), "loss_head")
    loss = lax.psum(loss_loc[0, 0], ("x", "y", "c"))
    grads = [None] * L
    for l in reversed(range(L)):
        dact, dact_bf, grads[l] = _layer_bwd(l, dact, dact_bf, saved[l], params[l], S)
    grad_x = dact.reshape(E, S, D)
    G = {k: jnp.stack([grads[l][k] for l in range(L)]) for k in WEIGHTS}

    cols_in, cols_out = N_IN // 4, D
    p_win = jnp.transpose(_from_layout(G["w_in"]).reshape(L, D, 4, cols_in), (0, 2, 1, 3))
    p_wout = G["w_out"].reshape(L, 4, D // 4, D)
    small_names = _REPL + _SHARD_SMALL
    small_g = _pack([G[k] for k in small_names])
    g_in, g_out, r_small = _reduce_gradients(p_win, p_wout, small_g, ci)

    rows_in, rows_out = L * D, L * (D // 4)
    res = {}
    res["w_in"] = [a.reshape(L, D, cols_in) for a in _adamw(
        w_in.reshape(rows_in, cols_in), (g_in.reshape(rows_in, cols_in),), m_w_in.reshape(rows_in, cols_in),
        v_w_in.reshape(rows_in, cols_in), 256, "adamw_w_in")]
    res["w_out"] = [a.reshape(L, D // 4, D) for a in _adamw(
        w_out.reshape(rows_out, cols_out), (g_out.reshape(rows_out, cols_out),), m_w_out.reshape(rows_out, cols_out),
        v_w_out.reshape(rows_out, cols_out), 256, "adamw_w_out")]
    zeros_sh = [jnp.zeros(G[k].shape, F32) for k in _SHARD_SMALL]
    pk = lambda dct: _pack([dct[k] for k in _REPL] + zeros_sh)
    small_res = _adamw(pk(W), (r_small,), pk(M), pk(V), small_g.shape[0], "adamw_small")
    shapes = [G[k].shape for k in small_names]
    unp = [_unpack(a, shapes) for a in small_res]
    for i, k in enumerate(_REPL):
        res[k] = [u[i] for u in unp]
    g_sh = []
    for i, k in enumerate(_SHARD_SMALL):
        gfull = unp[0][len(_REPL) + i]
        wdt = W[k].shape[-1]
        g_sh.append(lax.dynamic_slice_in_dim(gfull, chip * wdt, wdt, axis=2))
    g_sh_p = _pack(g_sh)
    sh_res = _adamw(_pack([W[k] for k in _SHARD_SMALL]), (g_sh_p,), _pack([M[k] for k in _SHARD_SMALL]),
                    _pack([V[k] for k in _SHARD_SMALL]), g_sh_p.shape[0], "adamw_shard_small")
    shapes2 = [W[k].shape for k in _SHARD_SMALL]
    unp2 = [_unpack(a, shapes2) for a in sh_res]
    for i, k in enumerate(_SHARD_SMALL):
        res[k] = [u[i] for u in unp2]

    outs = [loss, grad_x]
    for j in range(4):
        outs += [res[k][j] for k in WEIGHTS]
    return tuple(outs)
```

```python
import functools

import numpy as np
import jax
import jax.numpy as jnp
from jax import lax
from jax.experimental import pallas as pl
from jax.experimental.pallas import tpu as pltpu

F32 = jnp.float32
BF16 = jnp.bfloat16
HI = lax.Precision.HIGHEST
MESH = pl.DeviceIdType.MESH

EPS = 1e-6
D_MODEL = 2048
GROUP_W = 512
SEQ = 2048
DEPTH = 2
N_IN = 6176
GRID_W = 64
CONV_K = 31
NA_HEADS = 8
NA_DH = 64
NA_ROWS = 8
NA_COLS = 16
GLA_HEADS = 4
GLA_DK = 64
GLA_DV = 128
GLA_RANK = 16
GLA_TAU = 16.0
CHUNK = 64
POOL_WINDOWS = (2, 4, 8, 16)
ADAM_LR, ADAM_B1, ADAM_B2, ADAM_EPS, ADAM_WD, ADAM_STEP = 0.001, 0.9, 0.999, 1e-08, 0.01, 10

A_VAL, A_GLU, A_GATE = 0, 512, 1024
B_Q, B_K, B_V, B_GATE = 1536, 2048, 2560, 3072
C_Q, C_K, C_V, C_GATE = 3584, 3840, 4096, 4608
D_VAL, D_GATE = 5120, 5632
LR_OFF = 6144
NZ = 6400
NEG = -1e30
VMEM_LIMIT = 56 * 1024 * 1024


def _cp(sem=None):
    return pltpu.CompilerParams(dimension_semantics=sem, vmem_limit_bytes=VMEM_LIMIT)


def _sigmoid(x):
    return 1.0 / (1.0 + jnp.exp(-x))


def _silu(x):
    return x * _sigmoid(x)


def _dsilu(x):
    s = _sigmoid(x)
    return s * (1.0 + x * (1.0 - s))


def _matmul(a, b, *, dims, out_dtype, tm, tn, tk, name, res=None):
    if dims == "nn":
        (M, K), N = a.shape, b.shape[1]
    elif dims == "nt":
        (M, K), N = a.shape, b.shape[0]
    else:
        (K, M), N = a.shape, b.shape[1]
    tm, tn, tk = min(tm, M), min(tn, N), min(tk, K)
    nk = K // tk
    assert M % tm == 0 and N % tn == 0 and K % tk == 0, (M, N, K, tm, tn, tk)
    dn = {"nn": (((1,), (0,)), ((), ())), "nt": (((1,), (1,)), ((), ())), "tn": (((0,), (0,)), ((), ()))}[dims]
    if dims == "tn":
        a_spec = pl.BlockSpec((tk, tm), lambda i, j, k: (k, i))
    else:
        a_spec = pl.BlockSpec((tm, tk), lambda i, j, k: (i, k))
    if dims == "nt":
        b_spec = pl.BlockSpec((tn, tk), lambda i, j, k: (j, k))
    else:
        b_spec = pl.BlockSpec((tk, tn), lambda i, j, k: (k, j))
    o_spec = pl.BlockSpec((tm, tn), lambda i, j, k: (i, j))
    has_res = res is not None

    def body(*refs):
        if has_res:
            a_ref, b_ref, r_ref, o_ref, acc = refs
        else:
            a_ref, b_ref, o_ref, acc = refs
        k = pl.program_id(2)

        @pl.when(k == 0)
        def _():
            acc[...] = jnp.zeros_like(acc)

        acc[...] += lax.dot_general(a_ref[...], b_ref[...], dn, preferred_element_type=F32)

        @pl.when(k == nk - 1)
        def _():
            r = acc[...]
            if has_res:
                r = r + r_ref[...]
            o_ref[...] = r.astype(o_ref.dtype)

    in_specs = [a_spec, b_spec] + ([o_spec] if has_res else [])
    args = (a, b) + ((res,) if has_res else ())
    return pl.pallas_call(
        body, out_shape=jax.ShapeDtypeStruct((M, N), out_dtype), grid=(M // tm, N // tn, nk),
        in_specs=in_specs, out_specs=o_spec, scratch_shapes=[pltpu.VMEM((tm, tn), F32)],
        name=name, compiler_params=_cp(("parallel", "parallel", "arbitrary")))(*args)


def _rmsnorm_fwd(x, g, name):
    T, D = x.shape
    tm = 256

    def body(x_ref, g_ref, h_ref):
        xv = x_ref[...]
        r = lax.rsqrt(jnp.mean(xv * xv, axis=-1, keepdims=True) + EPS)
        h_ref[...] = (xv * r * g_ref[...]).astype(h_ref.dtype)

    return pl.pallas_call(
        body, out_shape=jax.ShapeDtypeStruct((T, D), BF16), grid=(T // tm,),
        in_specs=[pl.BlockSpec((tm, D), lambda i: (i, 0)), pl.BlockSpec((1, D), lambda i: (0, 0))],
        out_specs=pl.BlockSpec((tm, D), lambda i: (i, 0)), name=name, compiler_params=_cp(("parallel",)))(x, g)


def _rmsnorm_bwd(x, g, dh, dres, name):
    T, D = x.shape
    tm = 256

    def body(x_ref, g_ref, dh_ref, dres_ref, dx_ref, dxb_ref, dg_ref):
        xv = x_ref[...]
        r = lax.rsqrt(jnp.mean(xv * xv, axis=-1, keepdims=True) + EPS)
        xh = xv * r
        dh_v = dh_ref[...]
        dxh = dh_v * g_ref[...]
        dx = r * (dxh - xh * jnp.mean(dxh * xh, axis=-1, keepdims=True)) + dres_ref[...]
        dx_ref[...] = dx
        dxb_ref[...] = dx.astype(BF16)

        @pl.when(pl.program_id(0) == 0)
        def _():
            dg_ref[...] = jnp.zeros_like(dg_ref)

        dg_ref[...] += jnp.sum(dh_v * xh, axis=0, keepdims=True)

    row = pl.BlockSpec((tm, D), lambda i: (i, 0))
    vec = pl.BlockSpec((1, D), lambda i: (0, 0))
    return pl.pallas_call(
        body, out_shape=(jax.ShapeDtypeStruct((T, D), F32), jax.ShapeDtypeStruct((T, D), BF16), jax.ShapeDtypeStruct((1, D), F32)),
        grid=(T // tm,), in_specs=[row, vec, row, row], out_specs=(row, row, vec), name=name,
        compiler_params=_cp(("arbitrary",)))(x, g, dh, dres)


def _loss_head(y, target, name):
    T, D = y.shape
    tm = 256

    def body(y_ref, t_ref, d_ref, db_ref, l_ref):
        e = y_ref[...] - t_ref[...]
        d = e * (1.0 / D)
        d_ref[...] = d
        db_ref[...] = d.astype(BF16)

        @pl.when(pl.program_id(0) == 0)
        def _():
            l_ref[...] = jnp.zeros_like(l_ref)

        row = jnp.sum(e * e, axis=-1, keepdims=True) * (0.5 / D)
        l_ref[...] += jnp.sum(row, axis=0, keepdims=True)

    row = pl.BlockSpec((tm, D), lambda i: (i, 0))
    return pl.pallas_call(
        body, out_shape=(jax.ShapeDtypeStruct((T, D), F32), jax.ShapeDtypeStruct((T, D), BF16),
                         jax.ShapeDtypeStruct((1, 1), F32)), grid=(T // tm,),
        in_specs=[row, row], out_specs=(row, row, pl.BlockSpec((1, 1), lambda i: (0, 0))),
        name=name, compiler_params=_cp(("arbitrary",)))(y, target)


_GATE_COLS = (A_GATE // GROUP_W, B_GATE // GROUP_W, C_GATE // GROUP_W, D_GATE // GROUP_W)


def _gate_fwd(pres, z, name):
    T = z.shape[0]
    tm = 256

    def body(pa, pb, pc, pd, ga, gb, gc, gd, y_ref):
        for n, (p, g) in enumerate(((pa, ga), (pb, gb), (pc, gc), (pd, gd))):
            y_ref[:, n * GROUP_W:(n + 1) * GROUP_W] = (p[...] * _silu(g[...])).astype(BF16)

    pre_spec = pl.BlockSpec((tm, GROUP_W), lambda i: (i, 0))
    gate_specs = [pl.BlockSpec((tm, GROUP_W), functools.partial(lambda i, c: (i, c), c=c)) for c in _GATE_COLS]
    return pl.pallas_call(
        body, out_shape=jax.ShapeDtypeStruct((T, 4 * GROUP_W), BF16), grid=(T // tm,),
        in_specs=[pre_spec] * 4 + gate_specs, out_specs=pl.BlockSpec((tm, 4 * GROUP_W), lambda i: (i, 0)),
        name=name, compiler_params=_cp(("parallel",)))(*pres, z, z, z, z)


def _gate_bwd(dy, pres, z, name):
    T = z.shape[0]
    tm = 256

    def body(dy_ref, pa, pb, pc, pd, ga, gb, gc, gd, dpa, dpb, dpc, dpd, dga, dgb, dgc, dgd):
        for n, (p, g, dp, dg) in enumerate(((pa, ga, dpa, dga), (pb, gb, dpb, dgb), (pc, gc, dpc, dgc), (pd, gd, dpd, dgd))):
            d = dy_ref[:, n * GROUP_W:(n + 1) * GROUP_W]
            gv = g[...]
            dp[...] = d * _silu(gv)
            dg[...] = (d * p[...] * _dsilu(gv)).astype(BF16)

    pre_spec = pl.BlockSpec((tm, GROUP_W), lambda i: (i, 0))
    gate_specs = [pl.BlockSpec((tm, GROUP_W), functools.partial(lambda i, c: (i, c), c=c)) for c in _GATE_COLS]
    outs = tuple([jax.ShapeDtypeStruct((T, GROUP_W), F32)] * 4 + [jax.ShapeDtypeStruct((T, GROUP_W), BF16)] * 4)
    return pl.pallas_call(
        body, out_shape=outs, grid=(T // tm,),
        in_specs=[pl.BlockSpec((tm, 4 * GROUP_W), lambda i: (i, 0))] + [pre_spec] * 4 + gate_specs,
        out_specs=tuple([pre_spec] * 8), name=name, compiler_params=_cp(("parallel",)))(dy, *pres, z, z, z, z)


_PAD = 16
_RC = 256


def _conv_fwd(z, conv_w32, conv_b, S, name):
    T = z.shape[0]
    E = T // S
    LW = 128

    def body(val_ref, glu_ref, w_ref, b_ref, y_ref, upad):
        upad[0:_PAD, :] = jnp.zeros((_PAD, LW), F32)
        upad[_PAD + S:_PAD + S + _PAD, :] = jnp.zeros((_PAD, LW), F32)
        upad[_PAD:_PAD + S, :] = val_ref[...] * _sigmoid(glu_ref[...])
        for r in range(S // _RC):
            acc = jnp.broadcast_to(b_ref[...], (_RC, LW))
            for k in range(CONV_K):
                st = r * _RC + k + 1
                acc = acc + upad[st:st + _RC, :] * w_ref[k:k + 1, :]
            y_ref[r * _RC:(r + 1) * _RC, :] = acc

    return pl.pallas_call(
        body, out_shape=jax.ShapeDtypeStruct((T, GROUP_W), F32), grid=(E, GROUP_W // LW),
        in_specs=[pl.BlockSpec((S, LW), lambda e, j: (e, A_VAL // LW + j)),
                  pl.BlockSpec((S, LW), lambda e, j: (e, A_GLU // LW + j)),
                  pl.BlockSpec((32, LW), lambda e, j: (0, j)),
                  pl.BlockSpec((1, LW), lambda e, j: (0, j))],
        out_specs=pl.BlockSpec((S, LW), lambda e, j: (e, j)),
        scratch_shapes=[pltpu.VMEM((S + 2 * _PAD, LW), F32)],
        name=name, compiler_params=_cp(("parallel", "parallel")))(z, z, conv_w32, conv_b)


def _conv_bwd(z, conv_w32, dyc, S, name):
    T = z.shape[0]
    E = T // S
    LW = 128

    def body(val_ref, glu_ref, w_ref, dy_ref, dval_ref, dglu_ref, dw_ref, db_ref, upad, dpad):
        e = pl.program_id(1)
        zeros = jnp.zeros((_PAD, LW), F32)
        upad[0:_PAD, :] = zeros
        upad[_PAD + S:_PAD + S + _PAD, :] = zeros
        dpad[0:_PAD, :] = zeros
        dpad[_PAD + S:_PAD + S + _PAD, :] = zeros
        upad[_PAD:_PAD + S, :] = val_ref[...] * _sigmoid(glu_ref[...])
        dpad[_PAD:_PAD + S, :] = dy_ref[...]

        @pl.when(e == 0)
        def _():
            dw_ref[...] = jnp.zeros_like(dw_ref)
            db_ref[...] = jnp.zeros_like(db_ref)

        db_ref[...] += jnp.sum(dy_ref[...], axis=0, keepdims=True)
        for r in range(S // _RC):
            dyr = dy_ref[r * _RC:(r + 1) * _RC, :]
            du = jnp.zeros((_RC, LW), F32)
            for k in range(CONV_K):
                st = r * _RC + k + 1
                dw_ref[k:k + 1, :] += jnp.sum(dyr * upad[st:st + _RC, :], axis=0, keepdims=True)
                sd = r * _RC + (CONV_K - 1 - k) + 1
                du = du + dpad[sd:sd + _RC, :] * w_ref[k:k + 1, :]
            sl = slice(r * _RC, (r + 1) * _RC)
            val = val_ref[sl, :]
            sg = _sigmoid(glu_ref[sl, :])
            dval_ref[sl, :] = (du * sg).astype(BF16)
            dglu_ref[sl, :] = (du * val * sg * (1.0 - sg)).astype(BF16)

    blk = pl.BlockSpec((S, LW), lambda j, e: (e, j))
    return pl.pallas_call(
        body, out_shape=(jax.ShapeDtypeStruct((T, GROUP_W), BF16), jax.ShapeDtypeStruct((T, GROUP_W), BF16),
                         jax.ShapeDtypeStruct((32, GROUP_W), F32), jax.ShapeDtypeStruct((1, GROUP_W), F32)),
        grid=(GROUP_W // LW, E),
        in_specs=[pl.BlockSpec((S, LW), lambda j, e: (e, A_VAL // LW + j)),
                  pl.BlockSpec((S, LW), lambda j, e: (e, A_GLU // LW + j)),
                  pl.BlockSpec((32, LW), lambda j, e: (0, j)), blk],
        out_specs=(blk, blk, pl.BlockSpec((32, LW), lambda j, e: (0, j)), pl.BlockSpec((1, LW), lambda j, e: (0, j))),
        scratch_shapes=[pltpu.VMEM((S + 2 * _PAD, LW), F32), pltpu.VMEM((S + 2 * _PAD, LW), F32)],
        name=name, compiler_params=_cp(("parallel", "arbitrary")))(z, z, conv_w32, dyc)


def _ln_silu_fwd(yc, g, b, name):
    T, C = yc.shape
    tm = 256

    def body(y_ref, g_ref, b_ref, o_ref):
        y = y_ref[...]
        mu = jnp.mean(y, axis=-1, keepdims=True)
        yc_ = y - mu
        r = lax.rsqrt(jnp.mean(yc_ * yc_, axis=-1, keepdims=True) + EPS)
        o_ref[...] = _silu(yc_ * r * g_ref[...] + b_ref[...])

    row = pl.BlockSpec((tm, C), lambda i: (i, 0))
    vec = pl.BlockSpec((1, C), lambda i: (0, 0))
    return pl.pallas_call(body, out_shape=jax.ShapeDtypeStruct((T, C), F32), grid=(T // tm,),
                          in_specs=[row, vec, vec], out_specs=row, name=name, compiler_params=_cp(("parallel",)))(yc, g, b)


def _ln_silu_bwd(yc, g, b, dpre, name):
    T, C = yc.shape
    tm = 256

    def body(y_ref, g_ref, b_ref, dp_ref, dy_ref, dg_ref, db_ref):
        y = y_ref[...]
        mu = jnp.mean(y, axis=-1, keepdims=True)
        yc_ = y - mu
        r = lax.rsqrt(jnp.mean(yc_ * yc_, axis=-1, keepdims=True) + EPS)
        xh = yc_ * r
        gv = g_ref[...]
        dln = dp_ref[...] * _dsilu(xh * gv + b_ref[...])
        dxh = dln * gv
        dy_ref[...] = r * (dxh - jnp.mean(dxh, axis=-1, keepdims=True) - xh * jnp.mean(dxh * xh, axis=-1, keepdims=True))

        @pl.when(pl.program_id(0) == 0)
        def _():
            dg_ref[...] = jnp.zeros_like(dg_ref)
            db_ref[...] = jnp.zeros_like(db_ref)

        dg_ref[...] += jnp.sum(dln * xh, axis=0, keepdims=True)
        db_ref[...] += jnp.sum(dln, axis=0, keepdims=True)

    row = pl.BlockSpec((tm, C), lambda i: (i, 0))
    vec = pl.BlockSpec((1, C), lambda i: (0, 0))
    return pl.pallas_call(
        body, out_shape=(jax.ShapeDtypeStruct((T, C), F32), jax.ShapeDtypeStruct((1, C), F32), jax.ShapeDtypeStruct((1, C), F32)),
        grid=(T // tm,), in_specs=[row, vec, vec, row], out_specs=(row, vec, vec), name=name,
        compiler_params=_cp(("arbitrary",)))(yc, g, b, dpre)


def _pool_counts(S, w, rows0, n):
    t = (lax.broadcasted_iota(jnp.int32, (n, 1), 0) + rows0)
    lo = jnp.maximum(t - w // 2, 0)
    hi = jnp.minimum(t + w // 2, S)
    return (hi - lo).astype(F32)


def _pool_fwd(z, pool_w, pool_scale, S, name):
    T = z.shape[0]
    E = T // S
    CG = 128

    def body(u_ref, w_ref, s_ref, o_ref, upad, dif):
        zeros = jnp.zeros((_PAD, GROUP_W), F32)
        upad[0:_PAD, :] = zeros
        upad[_PAD + S:_PAD + S + _PAD, :] = zeros
        upad[_PAD:_PAD + S, :] = u_ref[...]
        for gi, w in enumerate(POOL_WINDOWS):
            ls = slice(gi * CG, (gi + 1) * CG)
            for r in range(S // _RC):
                acc = jnp.zeros((_RC, CG), F32)
                for j in range(-(w // 2), w // 2):
                    st = _PAD + r * _RC + j
                    acc = acc + upad[st:st + _RC, ls]
                cnt = _pool_counts(S, w, r * _RC, _RC)
                dif[r * _RC:(r + 1) * _RC, :] = (acc / cnt - u_ref[r * _RC:(r + 1) * _RC, ls]).astype(BF16)
            yp = jnp.dot(dif[...], w_ref[gi], preferred_element_type=F32)
            o_ref[:, ls] = yp * s_ref[:, ls]

    return pl.pallas_call(
        body, out_shape=jax.ShapeDtypeStruct((T, GROUP_W), F32), grid=(E,),
        in_specs=[pl.BlockSpec((S, GROUP_W), lambda e: (e, D_VAL // GROUP_W)),
                  pl.BlockSpec((4, CG, CG), lambda e: (0, 0, 0)),
                  pl.BlockSpec((1, GROUP_W), lambda e: (0, 0))],
        out_specs=pl.BlockSpec((S, GROUP_W), lambda e: (e, 0)),
        scratch_shapes=[pltpu.VMEM((S + 2 * _PAD, GROUP_W), F32), pltpu.VMEM((S, CG), BF16)],
        name=name, compiler_params=_cp(("parallel",)))(z, pool_w, pool_scale)


def _pool_bwd(z, pool_w, pool_scale, dpre, S, name):
    T = z.shape[0]
    E = T // S
    CG = 128

    def body(u_ref, w_ref, s_ref, dp_ref, du_ref, dw_ref, ds_ref, upad, dif, qpad):
        zeros = jnp.zeros((_PAD, GROUP_W), F32)
        upad[0:_PAD, :] = zeros
        upad[_PAD + S:_PAD + S + _PAD, :] = zeros
        upad[_PAD:_PAD + S, :] = u_ref[...]
        zc = jnp.zeros((_PAD, CG), F32)
        qpad[0:_PAD, :] = zc
        qpad[_PAD + S:_PAD + S + _PAD, :] = zc

        @pl.when(pl.program_id(0) == 0)
        def _():
            dw_ref[...] = jnp.zeros_like(dw_ref)
            ds_ref[...] = jnp.zeros_like(ds_ref)

        for gi, w in enumerate(POOL_WINDOWS):
            ls = slice(gi * CG, (gi + 1) * CG)
            for r in range(S // _RC):
                acc = jnp.zeros((_RC, CG), F32)
                for j in range(-(w // 2), w // 2):
                    st = _PAD + r * _RC + j
                    acc = acc + upad[st:st + _RC, ls]
                cnt = _pool_counts(S, w, r * _RC, _RC)
                dif[r * _RC:(r + 1) * _RC, :] = (acc / cnt - u_ref[r * _RC:(r + 1) * _RC, ls]).astype(BF16)
            dp = dp_ref[:, ls]
            yp = jnp.dot(dif[...], w_ref[gi], preferred_element_type=F32)
            ds_ref[:, ls] += jnp.sum(dp * yp, axis=0, keepdims=True)
            dys = (dp * s_ref[:, ls]).astype(BF16)
            dw_ref[gi] += lax.dot_general(dif[...], dys, (((0,), (0,)), ((), ())), preferred_element_type=F32)
            dm = lax.dot_general(dys, w_ref[gi], (((1,), (1,)), ((), ())), preferred_element_type=F32)
            for r in range(S // _RC):
                cnt = _pool_counts(S, w, r * _RC, _RC)
                qpad[_PAD + r * _RC:_PAD + (r + 1) * _RC, :] = dm[r * _RC:(r + 1) * _RC, :] / cnt
            for r in range(S // _RC):
                acc = -dm[r * _RC:(r + 1) * _RC, :]
                for j in range(-(w // 2) + 1, w // 2 + 1):
                    st = _PAD + r * _RC + j
                    acc = acc + qpad[st:st + _RC, :]
                du_ref[r * _RC:(r + 1) * _RC, ls] = acc.astype(BF16)

    return pl.pallas_call(
        body, out_shape=(jax.ShapeDtypeStruct((T, GROUP_W), BF16), jax.ShapeDtypeStruct((4, CG, CG), F32),
                         jax.ShapeDtypeStruct((1, GROUP_W), F32)), grid=(E,),
        in_specs=[pl.BlockSpec((S, GROUP_W), lambda e: (e, D_VAL // GROUP_W)),
                  pl.BlockSpec((4, CG, CG), lambda e: (0, 0, 0)),
                  pl.BlockSpec((1, GROUP_W), lambda e: (0, 0)),
                  pl.BlockSpec((S, GROUP_W), lambda e: (e, 0))],
        out_specs=(pl.BlockSpec((S, GROUP_W), lambda e: (e, 0)), pl.BlockSpec((4, CG, CG), lambda e: (0, 0, 0)),
                   pl.BlockSpec((1, GROUP_W), lambda e: (0, 0))),
        scratch_shapes=[pltpu.VMEM((S + 2 * _PAD, GROUP_W), F32), pltpu.VMEM((S, CG), BF16),
                        pltpu.VMEM((S + 2 * _PAD, CG), F32)],
        name=name, compiler_params=_cp(("arbitrary",)))(z, pool_w, pool_scale, dpre)


def _na_tables():
    d = np.arange(NA_ROWS)[:, None]
    kr = np.arange(NA_ROWS)[None, :]
    ro = kr - d + (NA_ROWS - 1)
    qc = np.arange(GRID_W)[:, None]
    kc = np.arange(GRID_W)[None, :]
    cs = np.clip(qc - NA_COLS // 2, 0, GRID_W - NA_COLS)
    valid = (kc >= cs) & (kc < cs + NA_COLS)
    co = np.clip(kc - qc + (NA_COLS - 1), 0, 2 * NA_COLS - 2)
    return ro, co, valid


def _na_onehots():
    ro, co, valid = _na_tables()
    e_np = np.zeros((GRID_W, GRID_W, 128), np.float32)
    qi, ki = np.nonzero(valid)
    e_np[qi, ki, co[qi, ki]] = 1.0
    a_np = np.zeros((16, NA_ROWS * NA_ROWS), np.float32)
    a_np[ro.reshape(-1), np.arange(NA_ROWS * NA_ROWS)] = 1.0
    mask = np.where(valid, 0.0, NEG).astype(np.float32).reshape(1, GRID_W * GRID_W)
    return e_np.reshape(GRID_W * GRID_W, 128), a_np, mask


def _na_bias(rpb, name):
    e_np, a_np, mask = _na_onehots()
    H = NA_HEADS
    rp = jnp.pad(rpb, ((0, 0), (0, 1), (0, 128 - rpb.shape[2])))

    def body(r_ref, e_ref, at_ref, m_ref, o_ref):
        t = jnp.dot(at_ref[...], r_ref[0], precision=HI, preferred_element_type=F32)
        o_ref[0] = lax.dot_general(t, e_ref[...], (((1,), (1,)), ((), ())), precision=HI,
                                   preferred_element_type=F32) + m_ref[...]

    out = pl.pallas_call(
        body, out_shape=jax.ShapeDtypeStruct((H, NA_ROWS * NA_ROWS, GRID_W * GRID_W), F32), grid=(H,),
        in_specs=[pl.BlockSpec((1, 16, 128), lambda h: (h, 0, 0)),
                  pl.BlockSpec((GRID_W * GRID_W, 128), lambda h: (0, 0)),
                  pl.BlockSpec((NA_ROWS * NA_ROWS, 16), lambda h: (0, 0)),
                  pl.BlockSpec((1, GRID_W * GRID_W), lambda h: (0, 0))],
        out_specs=pl.BlockSpec((1, NA_ROWS * NA_ROWS, GRID_W * GRID_W), lambda h: (h, 0, 0)),
        name=name, compiler_params=_cp(("parallel",)))(rp, jnp.asarray(e_np), jnp.asarray(a_np.T), jnp.asarray(mask))
    t = out.reshape(H, NA_ROWS, NA_ROWS, GRID_W, GRID_W)
    return jnp.transpose(t, (0, 1, 3, 2, 4)).reshape(H, NA_ROWS, GRID_W, NA_ROWS * GRID_W)


def _seg_mean_matrix(width, seg):
    i = np.arange(width)
    return jnp.asarray((i[:, None] // seg == i[None, :] // seg).astype(np.float32) / seg)


def _na_fwd(z, qg, kg, bias, S, name):
    T = z.shape[0]
    E = T // S
    rows = S // GRID_W
    WIN = NA_ROWS * GRID_W
    seg = _seg_mean_matrix(128, NA_DH)

    def body(q_ref, k_ref, v_ref, qg_ref, kg_ref, bias_ref, seg_ref, o_ref, qs, ks, vs):
        for c in range(S // _RC):
            sl = slice(c * _RC, (c + 1) * _RC)
            q = q_ref[sl, :]
            k = k_ref[sl, :]
            qn = q * lax.rsqrt(jnp.dot(q * q, seg_ref[...], precision=HI, preferred_element_type=F32) + EPS) * qg_ref[...]
            kn = k * lax.rsqrt(jnp.dot(k * k, seg_ref[...], precision=HI, preferred_element_type=F32) + EPS) * kg_ref[...]
            v = v_ref[sl, :]
            for hh in range(2):
                ls = slice(hh * NA_DH, (hh + 1) * NA_DH)
                qs[hh, sl, :] = qn[:, ls].astype(BF16)
                ks[hh, sl, :] = kn[:, ls].astype(BF16)
                vs[hh, sl, :] = v[:, ls].astype(BF16)
        def row(r, carry):
            rs = jnp.clip(r - NA_ROWS // 2, 0, rows - NA_ROWS)
            q0 = pl.multiple_of(r * GRID_W, GRID_W)
            k0 = pl.multiple_of(rs * GRID_W, GRID_W)
            outs = []
            for hh in range(2):
                qr = qs[hh, pl.ds(q0, GRID_W), :]
                kw = ks[hh, pl.ds(k0, WIN), :]
                vw = vs[hh, pl.ds(k0, WIN), :]
                s = lax.dot_general(qr, kw, (((1,), (1,)), ((), ())), preferred_element_type=F32) * (NA_DH ** -0.5)
                s = s + bias_ref[hh, r - rs]
                p = jnp.exp(s - jnp.max(s, axis=-1, keepdims=True))
                l = jnp.sum(p, axis=-1, keepdims=True)
                outs.append(jnp.dot(p.astype(BF16), vw, preferred_element_type=F32) / l)
            o_ref[pl.ds(q0, GRID_W), :] = jnp.concatenate(outs, axis=1)
            return carry
        lax.fori_loop(0, rows, row, 0, unroll=2)

    LW = 128
    return pl.pallas_call(
        body, out_shape=jax.ShapeDtypeStruct((T, GROUP_W), F32), grid=(E, GROUP_W // LW),
        in_specs=[pl.BlockSpec((S, LW), lambda e, j: (e, B_Q // LW + j)),
                  pl.BlockSpec((S, LW), lambda e, j: (e, B_K // LW + j)),
                  pl.BlockSpec((S, LW), lambda e, j: (e, B_V // LW + j)),
                  pl.BlockSpec((1, LW), lambda e, j: (0, j)),
                  pl.BlockSpec((1, LW), lambda e, j: (0, j)),
                  pl.BlockSpec((2, NA_ROWS, GRID_W, WIN), lambda e, j: (j, 0, 0, 0)),
                  pl.BlockSpec((LW, LW), lambda e, j: (0, 0))],
        out_specs=pl.BlockSpec((S, LW), lambda e, j: (e, j)),
        scratch_shapes=[pltpu.VMEM((2, S, NA_DH), BF16)] * 3,
        name=name, compiler_params=_cp(("parallel", "parallel")))(z, z, z, qg, kg, bias, seg)


def _na_bwd(z, qg, kg, bias, do, S, name):
    T = z.shape[0]
    E = T // S
    rows = S // GRID_W
    WIN = NA_ROWS * GRID_W
    seg = _seg_mean_matrix(128, NA_DH)
    SC = NA_DH ** -0.5

    def body(q_ref, k_ref, v_ref, qg_ref, kg_ref, bias_ref, seg_ref, do_ref,
             dq_ref, dk_ref, dv_ref, dbias_ref, dqg_ref, dkg_ref, qs, ks, vs, dos, dqn, dkn, dvs, akt, avt):
        e = pl.program_id(1)

        @pl.when(e == 0)
        def _():
            dbias_ref[...] = jnp.zeros_like(dbias_ref)
            dqg_ref[...] = jnp.zeros_like(dqg_ref)
            dkg_ref[...] = jnp.zeros_like(dkg_ref)

        for c in range(S // _RC):
            sl = slice(c * _RC, (c + 1) * _RC)
            q = q_ref[sl, :]
            k = k_ref[sl, :]
            qn = q * lax.rsqrt(jnp.dot(q * q, seg_ref[...], precision=HI, preferred_element_type=F32) + EPS) * qg_ref[...]
            kn = k * lax.rsqrt(jnp.dot(k * k, seg_ref[...], precision=HI, preferred_element_type=F32) + EPS) * kg_ref[...]
            v = v_ref[sl, :]
            dd = do_ref[sl, :]
            for hh in range(2):
                ls = slice(hh * NA_DH, (hh + 1) * NA_DH)
                qs[hh, sl, :] = qn[:, ls].astype(BF16)
                ks[hh, sl, :] = kn[:, ls].astype(BF16)
                vs[hh, sl, :] = v[:, ls].astype(BF16)
                dos[hh, sl, :] = dd[:, ls].astype(BF16)
        akt[...] = jnp.zeros_like(akt)
        avt[...] = jnp.zeros_like(avt)

        def row(r, carry):
            rs = jnp.clip(r - NA_ROWS // 2, 0, rows - NA_ROWS)
            q0 = pl.multiple_of(r * GRID_W, GRID_W)
            k0 = pl.multiple_of(rs * GRID_W, GRID_W)
            par = rs % 2
            t0 = (rs + par) // 2
            dqs = []
            for hh in range(2):
                qr = qs[hh, pl.ds(q0, GRID_W), :]
                dor = dos[hh, pl.ds(q0, GRID_W), :]
                kw = ks[hh, pl.ds(k0, WIN), :]
                vw = vs[hh, pl.ds(k0, WIN), :]
                s = lax.dot_general(qr, kw, (((1,), (1,)), ((), ())), preferred_element_type=F32) * SC
                s = s + bias_ref[hh, r - rs]
                p = jnp.exp(s - jnp.max(s, axis=-1, keepdims=True))
                p = p / jnp.sum(p, axis=-1, keepdims=True)
                dp = lax.dot_general(dor, vw, (((1,), (1,)), ((), ())), preferred_element_type=F32)
                ds = p * (dp - jnp.sum(p * dp, axis=-1, keepdims=True))
                dbias_ref[hh, r - rs] += ds
                dsb = ds.astype(BF16)
                dqs.append(jnp.dot(dsb, kw, preferred_element_type=F32) * SC)
                dkt = lax.dot_general(qr, dsb, (((0,), (0,)), ((), ())), preferred_element_type=F32) * SC
                dvt = lax.dot_general(dor, p.astype(BF16), (((0,), (0,)), ((), ())), preferred_element_type=F32)
                akt[hh, par, pl.ds(t0, WIN // 128)] += jnp.stack([dkt[:, 128 * i:128 * (i + 1)] for i in range(WIN // 128)])
                avt[hh, par, pl.ds(t0, WIN // 128)] += jnp.stack([dvt[:, 128 * i:128 * (i + 1)] for i in range(WIN // 128)])
            dqn[pl.ds(q0, GRID_W), :] = jnp.concatenate(dqs, axis=1)
            return carry
        lax.fori_loop(0, rows, row, 0, unroll=2)

        for hh in range(2):
            ls = slice(hh * NA_DH, (hh + 1) * NA_DH)
            for i in range(S // 128):
                for acc, dst in ((akt, dkn), (avt, dvs)):
                    odd = jnp.concatenate([acc[hh, 1, i][:, NA_DH:], acc[hh, 1, i + 1][:, :NA_DH]], axis=1)
                    dst[128 * i:128 * (i + 1), ls] = (acc[hh, 0, i] + odd).T

        for c in range(S // _RC):
            sl = slice(c * _RC, (c + 1) * _RC)
            for x_ref, g_ref, dn, dx_ref, dg_ref in ((q_ref, qg_ref, dqn, dq_ref, dqg_ref), (k_ref, kg_ref, dkn, dk_ref, dkg_ref)):
                x = x_ref[sl, :]
                r_ = lax.rsqrt(jnp.dot(x * x, seg_ref[...], precision=HI, preferred_element_type=F32) + EPS)
                xh = x * r_
                d = dn[sl, :]
                dxh = d * g_ref[...]
                mean = jnp.dot(dxh * xh, seg_ref[...], precision=HI, preferred_element_type=F32)
                dx_ref[sl, :] = (r_ * (dxh - xh * mean)).astype(BF16)
                dg_ref[...] += jnp.sum(d * xh, axis=0, keepdims=True)
            dv_ref[sl, :] = dvs[sl, :].astype(BF16)

    LW = 128
    blk = pl.BlockSpec((S, LW), lambda j, e: (e, j))
    vec = pl.BlockSpec((1, LW), lambda j, e: (0, j))
    bsp = pl.BlockSpec((2, NA_ROWS, GRID_W, WIN), lambda j, e: (j, 0, 0, 0))
    return pl.pallas_call(
        body, out_shape=(jax.ShapeDtypeStruct((T, GROUP_W), BF16),) * 3 + (
            jax.ShapeDtypeStruct((NA_HEADS, NA_ROWS, GRID_W, WIN), F32),
            jax.ShapeDtypeStruct((1, GROUP_W), F32), jax.ShapeDtypeStruct((1, GROUP_W), F32)),
        grid=(GROUP_W // LW, E),
        in_specs=[pl.BlockSpec((S, LW), lambda j, e: (e, B_Q // LW + j)),
                  pl.BlockSpec((S, LW), lambda j, e: (e, B_K // LW + j)),
                  pl.BlockSpec((S, LW), lambda j, e: (e, B_V // LW + j)),
                  vec, vec, bsp, pl.BlockSpec((LW, LW), lambda j, e: (0, 0)), blk],
        out_specs=(blk, blk, blk, bsp, vec, vec),
        scratch_shapes=[pltpu.VMEM((2, S, NA_DH), BF16)] * 4 + [pltpu.VMEM((S, LW), F32)] * 3
        + [pltpu.VMEM((2, 2, S // 128 + 1, NA_DH, 128), F32)] * 2,
        name=name, compiler_params=_cp(("parallel", "arbitrary")))(z, z, z, qg, kg, bias, seg, do)


def _na_rpb_grad(dbias, name):
    e_np, a_np, _ = _na_onehots()
    H = NA_HEADS
    x = dbias.reshape(H, NA_ROWS, GRID_W, NA_ROWS, GRID_W)
    x = jnp.transpose(x, (0, 1, 3, 2, 4)).reshape(H, NA_ROWS * NA_ROWS, GRID_W * GRID_W)

    def body(x_ref, e_ref, a_ref, o_ref):
        y = jnp.dot(x_ref[0], e_ref[...], precision=HI, preferred_element_type=F32)
        o_ref[0] = jnp.dot(a_ref[...], y, precision=HI, preferred_element_type=F32)

    out = pl.pallas_call(
        body, out_shape=jax.ShapeDtypeStruct((H, 16, 128), F32), grid=(H,),
        in_specs=[pl.BlockSpec((1, 64, GRID_W * GRID_W), lambda h: (h, 0, 0)),
                  pl.BlockSpec((GRID_W * GRID_W, 128), lambda h: (0, 0)),
                  pl.BlockSpec((16, 64), lambda h: (0, 0))],
        out_specs=pl.BlockSpec((1, 16, 128), lambda h: (h, 0, 0)),
        name=name, compiler_params=_cp(("parallel",)))(x, jnp.asarray(e_np), jnp.asarray(a_np))
    return out[:, :2 * NA_ROWS - 1, :2 * NA_COLS - 1]


_HK = GLA_HEADS * GLA_DK
_HV = GLA_HEADS * GLA_DV


def _gla_consts(reverse):
    i = np.arange(CHUNK)
    tri = (i[:, None] <= i[None, :]) if reverse else (i[:, None] >= i[None, :])
    j = np.arange(_HK)
    oseg = (j[:, None] // GLA_DK == j[None, :] // GLA_DK)
    return (jnp.asarray(tri.astype(np.float32)), jnp.asarray(tri.T.astype(np.float32)), jnp.asarray(oseg.astype(np.float32), BF16))


def _log_decay(lr, a2, ab):
    zg = jnp.dot(lr, a2, precision=HI, preferred_element_type=F32) + ab
    g = (jnp.minimum(zg, 0.0) - jnp.log(1.0 + jnp.exp(-jnp.abs(zg)))) * (1.0 / GLA_TAU)
    return zg, g


def _dotf(a, b, dn):
    return lax.dot_general(a, b, dn, precision=HI, preferred_element_type=F32)


def _dotb(a, b, dn):
    return lax.dot_general(a.astype(BF16), b.astype(BF16), dn, preferred_element_type=F32)


_COLS = 4
_SUB = 16
_NSUB = CHUNK // _SUB


def _gla_cross_blocks(reverse):
    return range(0, _NSUB - 1) if reverse else range(1, _NSUB)


def _gla_cross_terms(s, reverse, b_s, q_s, k_s, oseg_ref):
    r0 = s * _SUB
    ref = r0 + (_SUB - 1 if reverse else 0)
    bref = b_s[ref:ref + 1, :]
    rowj = lax.broadcasted_iota(jnp.int32, (CHUNK, 1), 0)
    seen = (rowj >= r0 + _SUB) if reverse else (rowj < r0)
    ek = jnp.where(seen, jnp.exp(jnp.minimum(bref - b_s[...], 0.0)), 0.0)
    kt = k_s[...] * ek
    eq = jnp.exp(jnp.minimum(b_s[r0:r0 + _SUB, :] - bref, 0.0))
    qt = q_s[r0:r0 + _SUB, :] * eq
    nmat = jnp.concatenate([kt.astype(BF16)] * GLA_HEADS, axis=0) * oseg_ref[...]
    return qt, eq, kt, ek, nmat


_NN = (((1,), (0,)), ((), ()))
_NT = (((1,), (1,)), ((), ()))
_TN = (((0,), (0,)), ((), ()))


def _gla_fwd(z, a2, ab, S, reverse, name):
    T = z.shape[0]
    E = T // S
    n = S // CHUNK
    tri, _, oseg = _gla_consts(reverse)

    def body(q_ref, k_ref, v_ref, lr_ref, a2_ref, ab_ref, tri_ref, oseg_ref, o_ref, a_ref, st_ref, st, b_s, q_s, k_s):
        @pl.when(pl.program_id(1) == 0)
        def _():
            st[...] = jnp.zeros_like(st)

        q = q_ref[...] * (GLA_DK ** -0.5)
        k = k_ref[...]
        v = v_ref[...]
        _, g = _log_decay(lr_ref[...], a2_ref[...], ab_ref[...])
        b = _dotf(tri_ref[...], g, _NN)
        bl_row = jnp.sum(g, axis=0, keepdims=True)
        bl_col = _dotf(g, jnp.ones((CHUNK, GLA_DV), F32), _TN)
        b_s[...] = b
        q_s[...] = q
        k_s[...] = k
        lane = lax.broadcasted_iota(jnp.int32, (1, _HK), 1) % GLA_DK

        rowi = lax.broadcasted_iota(jnp.int32, (CHUNK, 1), 0)
        blk0 = (rowi // _SUB) * _SUB

        def cols(jj, a):
            ts = []
            for u in range(_COLS):
                jp = jj * _COLS + u
                tiles = []
                for s in range(_NSUB):
                    rs_ = slice(s * _SUB, (s + 1) * _SUB)
                    bj = b_s[pl.ds(s * _SUB + jp, 1), :]
                    kj = k_s[pl.ds(s * _SUB + jp, 1), :]
                    tiles.append(q_s[rs_, :] * jnp.exp(jnp.minimum(b_s[rs_, :] - bj, 0.0)) * kj)
                ts.append(jnp.concatenate(tiles, axis=0).astype(BF16))
            r = jnp.dot(jnp.concatenate(ts, axis=0), oseg_ref[...], preferred_element_type=F32)
            for u in range(_COLS):
                a = jnp.where(lane == blk0 + (jj * _COLS + u), r[u * CHUNK:(u + 1) * CHUNK, :], a)
            return a

        a = lax.fori_loop(0, _SUB // _COLS, cols, jnp.zeros((CHUNK, _HK), F32))
        keep = (rowi <= lane) if reverse else (rowi >= lane)
        a = jnp.where(keep, a, 0.0)
        cross = []
        for s in range(_NSUB):
            if s in _gla_cross_blocks(reverse):
                qt, _, _, _, nmat = _gla_cross_terms(s, reverse, b_s, q_s, k_s, oseg_ref)
                cross.append(lax.dot_general(qt.astype(BF16), nmat, _NT, preferred_element_type=F32))
            else:
                cross.append(jnp.zeros((_SUB, _HK), F32))
        a = a + jnp.concatenate(cross, axis=0)
        a_ref[...] = a
        st_ref[0] = st[...]
        qb = q * jnp.exp(b)
        kd = k * jnp.exp(bl_row - b)
        for h in range(GLA_HEADS):
            ks_ = slice(h * GLA_DK, (h + 1) * GLA_DK)
            vs_ = slice(h * GLA_DV, (h + 1) * GLA_DV)
            s_h = st[ks_, :]
            o_ref[:, vs_] = _dotb(qb[:, ks_], s_h, _NN) + _dotb(a[:, ks_], v[:, vs_], _NN)
            st[ks_, :] = s_h * jnp.exp(bl_col[ks_, :]) + _dotb(kd[:, ks_], v[:, vs_], _TN)

    def rowblk(e, c):
        return e * n + ((n - 1 - c) if reverse else c)

    return pl.pallas_call(
        body, out_shape=(jax.ShapeDtypeStruct((T, _HV), F32), jax.ShapeDtypeStruct((T, _HK), F32),
                         jax.ShapeDtypeStruct((T // CHUNK, _HK, GLA_DV), F32)),
        grid=(E, n),
        in_specs=[pl.BlockSpec((CHUNK, _HK), lambda e, c: (rowblk(e, c), C_Q // _HK)),
                  pl.BlockSpec((CHUNK, _HK), lambda e, c: (rowblk(e, c), C_K // _HK)),
                  pl.BlockSpec((CHUNK, _HV), lambda e, c: (rowblk(e, c), C_V // _HV)),
                  pl.BlockSpec((CHUNK, 128), lambda e, c: (rowblk(e, c), LR_OFF // 128)),
                  pl.BlockSpec((128, _HK), lambda e, c: (0, 0)),
                  pl.BlockSpec((1, _HK), lambda e, c: (0, 0)),
                  pl.BlockSpec((CHUNK, CHUNK), lambda e, c: (0, 0)),
                  pl.BlockSpec((_HK, _HK), lambda e, c: (0, 0))],
        out_specs=(pl.BlockSpec((CHUNK, _HV), lambda e, c: (rowblk(e, c), 0)),
                   pl.BlockSpec((CHUNK, _HK), lambda e, c: (rowblk(e, c), 0)),
                   pl.BlockSpec((1, _HK, GLA_DV), lambda e, c: (rowblk(e, c), 0, 0))),
        scratch_shapes=[pltpu.VMEM((_HK, GLA_DV), F32)] + [pltpu.VMEM((CHUNK, _HK), F32)] * 3,
        name=name, compiler_params=_cp(("parallel", "arbitrary")))(z, z, z, z, a2, ab, tri, oseg)


def _gla_bwd(z, a2, ab, att, states, do, prev, S, reverse, name):
    T = z.shape[0]
    E = T // S
    n = S // CHUNK
    tri, tri_t, oseg = _gla_consts(reverse)
    has_prev = prev is not None
    odt = BF16 if has_prev else F32

    def body(*refs):
        (q_ref, k_ref, v_ref, lr_ref, a2_ref, ab_ref, tri_ref, trit_ref, oseg_ref, att_ref, st_ref, do_ref) = refs[:12]
        refs = refs[12:]
        if has_prev:
            pq_ref, pk_ref, pv_ref, pl_ref = refs[:4]
            refs = refs[4:]
        (dq_ref, dk_ref, dv_ref, dlr_ref, da2_ref, dab_ref, dst, b_s, q_s, k_s, da_s, dqb_s, dkd_s, dk3_s, dbn_s, dsp_s) = refs
        first = (pl.program_id(0) == 0) & (pl.program_id(1) == 0)

        @pl.when(first)
        def _():
            da2_ref[...] = jnp.zeros_like(da2_ref)
            dab_ref[...] = jnp.zeros_like(dab_ref)

        @pl.when(pl.program_id(1) == 0)
        def _():
            dst[...] = jnp.zeros_like(dst)

        q = q_ref[...] * (GLA_DK ** -0.5)
        k = k_ref[...]
        v = v_ref[...]
        lr = lr_ref[...]
        zg, g = _log_decay(lr, a2_ref[...], ab_ref[...])
        b = _dotf(tri_ref[...], g, _NN)
        bl_row = jnp.sum(g, axis=0, keepdims=True)
        bl_col = _dotf(g, jnp.ones((CHUNK, GLA_DV), F32), _TN)
        eb = jnp.exp(b)
        ekd = jnp.exp(bl_row - b)
        qb = q * eb
        kd = k * ekd
        b_s[...] = b
        q_s[...] = q
        k_s[...] = k
        att = att_ref[...]
        s_all = st_ref[0]
        dsn = dst[...]
        e_col = jnp.exp(bl_col)
        do = do_ref[...]
        lane = lax.broadcasted_iota(jnp.int32, (1, _HK), 1) % GLA_DK
        rowi = lax.broadcasted_iota(jnp.int32, (CHUNK, 1), 0)
        keep = (rowi <= lane) if reverse else (rowi >= lane)
        for h in range(GLA_HEADS):
            ks_ = slice(h * GLA_DK, (h + 1) * GLA_DK)
            vs_ = slice(h * GLA_DV, (h + 1) * GLA_DV)
            do_h = do[:, vs_]
            s_h = s_all[ks_, :]
            dsn_h = dsn[ks_, :]
            dqb_s[:, ks_] = _dotb(do_h, s_h, _NT)
            dsp_s[ks_, :] = _dotb(qb[:, ks_], do_h, _TN) + dsn_h * e_col[ks_, :]
            da_s[:, ks_] = _dotb(do_h, v[:, vs_], _NT)
            dv_h = _dotb(att[:, ks_], do_h, _TN) + _dotb(kd[:, ks_], dsn_h, _NN)
            if has_prev:
                dv_h = dv_h + pv_ref[:, vs_]
            dv_ref[:, vs_] = dv_h.astype(odt)
            dkd_s[:, ks_] = _dotb(v[:, vs_], dsn_h, _NT)
        da_s[...] = jnp.where(keep, da_s[...], 0.0)
        dqb = dqb_s[...]
        dkd = dkd_s[...]
        x = dsn * s_all * e_col
        dbl_row = _dotf(jnp.ones((8, GLA_DV), F32), x, _NT)[0:1, :] + jnp.sum(dkd * kd, axis=0, keepdims=True)

        blk0 = (rowi // _SUB) * _SUB

        def cols(jj, carry):
            dq3, db3 = list(carry[:_NSUB]), list(carry[_NSUB:])
            sel = [jnp.where(lane == blk0 + (jj * _COLS + u), da_s[...], 0.0).astype(BF16) for u in range(_COLS)]
            dcols = jnp.dot(jnp.concatenate(sel, axis=0), oseg_ref[...], preferred_element_type=F32)
            for u in range(_COLS):
                jp = jj * _COLS + u
                for s in range(_NSUB):
                    rs_ = slice(s * _SUB, (s + 1) * _SUB)
                    bj = b_s[pl.ds(s * _SUB + jp, 1), :]
                    kj = k_s[pl.ds(s * _SUB + jp, 1), :]
                    tm_ = dcols[u * CHUNK + s * _SUB:u * CHUNK + (s + 1) * _SUB, :] * jnp.exp(jnp.minimum(b_s[rs_, :] - bj, 0.0))
                    dq3[s] = dq3[s] + tm_ * kj
                    gq = tm_ * q_s[rs_, :]
                    dk3_s[pl.ds(s * _SUB + jp, 1), :] = jnp.sum(gq, axis=0, keepdims=True)
                    w = gq * kj
                    dbn_s[pl.ds(s * _SUB + jp, 1), :] = jnp.sum(w, axis=0, keepdims=True)
                    db3[s] = db3[s] + w
            return tuple(dq3) + tuple(db3)

        zero = jnp.zeros((_SUB, _HK), F32)
        acc = lax.fori_loop(0, _SUB // _COLS, cols, (zero,) * (2 * _NSUB))
        dq3 = jnp.concatenate(acc[:_NSUB], axis=0)
        db3 = jnp.concatenate(acc[_NSUB:], axis=0)
        head = lax.broadcasted_iota(jnp.int32, (1, _HK), 1) // GLA_DK
        dq_x, db_x = [], []
        dk_x = jnp.zeros((CHUNK, _HK), F32)
        db_k = jnp.zeros((CHUNK, _HK), F32)
        for s in range(_NSUB):
            if s not in _gla_cross_blocks(reverse):
                dq_x.append(zero)
                db_x.append(zero)
                continue
            r0 = s * _SUB
            qt, eq, kt, ek, nmat = _gla_cross_terms(s, reverse, b_s, q_s, k_s, oseg_ref)
            seen = (lane >= r0 + _SUB) if reverse else (lane < r0)
            dax = jnp.where(seen, da_s[r0:r0 + _SUB, :], 0.0).astype(BF16)
            dqt = jnp.dot(dax, nmat, preferred_element_type=F32)
            full = lax.dot_general(dax, qt.astype(BF16), _TN, preferred_element_type=F32)
            dkt = full[0:CHUNK, :]
            for h in range(1, GLA_HEADS):
                dkt = jnp.where(head == h, full[h * CHUNK:(h + 1) * CHUNK, :], dkt)
            dq_x.append(dqt * eq)
            db_x.append(dqt * qt)
            dk_x = dk_x + dkt * ek
            db_k = db_k + dkt * kt
        dq = (dqb * eb + dq3 + jnp.concatenate(dq_x, axis=0)) * (GLA_DK ** -0.5)
        dk = dkd * ekd + dk3_s[...] + dk_x
        db = dqb * qb - dkd * kd + db3 - dbn_s[...] + jnp.concatenate(db_x, axis=0) - db_k
        dg = _dotf(trit_ref[...], db, _NN) + dbl_row
        sneg = 1.0 / (1.0 + jnp.exp(zg))
        dzg = dg * sneg * (1.0 / GLA_TAU)
        dlr = _dotf(dzg, a2_ref[...], _NT)
        da2_ref[...] += _dotf(lr, dzg, _TN)
        dab_ref[...] += jnp.sum(dzg, axis=0, keepdims=True)
        if has_prev:
            dq = dq + pq_ref[...]
            dk = dk + pk_ref[...]
            dlr = dlr + pl_ref[...]
        dq_ref[...] = dq.astype(odt)
        dk_ref[...] = dk.astype(odt)
        dlr_ref[...] = dlr.astype(odt)
        dst[...] = dsp_s[...]

    def rowblk(e, c):
        return e * n + (c if reverse else (n - 1 - c))

    hk = pl.BlockSpec((CHUNK, _HK), lambda e, c: (rowblk(e, c), 0))
    hv = pl.BlockSpec((CHUNK, _HV), lambda e, c: (rowblk(e, c), 0))
    l128 = pl.BlockSpec((CHUNK, 128), lambda e, c: (rowblk(e, c), 0))
    in_specs = [pl.BlockSpec((CHUNK, _HK), lambda e, c: (rowblk(e, c), C_Q // _HK)),
                pl.BlockSpec((CHUNK, _HK), lambda e, c: (rowblk(e, c), C_K // _HK)),
                pl.BlockSpec((CHUNK, _HV), lambda e, c: (rowblk(e, c), C_V // _HV)),
                pl.BlockSpec((CHUNK, 128), lambda e, c: (rowblk(e, c), LR_OFF // 128)),
                pl.BlockSpec((128, _HK), lambda e, c: (0, 0)),
                pl.BlockSpec((1, _HK), lambda e, c: (0, 0)),
                pl.BlockSpec((CHUNK, CHUNK), lambda e, c: (0, 0)),
                pl.BlockSpec((CHUNK, CHUNK), lambda e, c: (0, 0)),
                pl.BlockSpec((_HK, _HK), lambda e, c: (0, 0)),
                hk, pl.BlockSpec((1, _HK, GLA_DV), lambda e, c: (rowblk(e, c), 0, 0)), hv]
    args = [z, z, z, z, a2, ab, tri, tri_t, oseg, att, states, do]
    if has_prev:
        in_specs += [hk, hk, hv, l128]
        args += list(prev)
    return pl.pallas_call(
        body, out_shape=(jax.ShapeDtypeStruct((T, _HK), odt), jax.ShapeDtypeStruct((T, _HK), odt),
                         jax.ShapeDtypeStruct((T, _HV), odt), jax.ShapeDtypeStruct((T, 128), odt),
                         jax.ShapeDtypeStruct((128, _HK), F32), jax.ShapeDtypeStruct((1, _HK), F32)),
        grid=(E, n), in_specs=in_specs,
        out_specs=(hk, hk, hv, l128, pl.BlockSpec((128, _HK), lambda e, c: (0, 0)), pl.BlockSpec((1, _HK), lambda e, c: (0, 0))),
        scratch_shapes=[pltpu.VMEM((_HK, GLA_DV), F32)] + [pltpu.VMEM((CHUNK, _HK), F32)] * 8 + [pltpu.VMEM((_HK, GLA_DV), F32)],
        name=name, compiler_params=_cp(("arbitrary", "arbitrary")))(*args)


def _gla_norm_fwd(of, ob, og, name):
    T = of.shape[0]
    tm = 256

    def body(f_ref, b_ref, g_ref, o_ref):
        for h in range(GLA_HEADS):
            vs_ = slice(h * GLA_DV, (h + 1) * GLA_DV)
            o = f_ref[:, vs_] + b_ref[:, vs_]
            o_ref[:, vs_] = o * lax.rsqrt(jnp.mean(o * o, axis=-1, keepdims=True) + EPS) * g_ref[:, vs_]

    row = pl.BlockSpec((tm, _HV), lambda i: (i, 0))
    vec = pl.BlockSpec((1, _HV), lambda i: (0, 0))
    return pl.pallas_call(body, out_shape=jax.ShapeDtypeStruct((T, _HV), F32), grid=(T // tm,),
                          in_specs=[row, row, vec], out_specs=row, name=name, compiler_params=_cp(("parallel",)))(of, ob, og)


def _gla_norm_bwd(of, ob, og, dpre, name):
    T = of.shape[0]
    tm = 256

    def body(f_ref, b_ref, g_ref, dp_ref, do_ref, dg_ref):
        @pl.when(pl.program_id(0) == 0)
        def _():
            dg_ref[...] = jnp.zeros_like(dg_ref)

        for h in range(GLA_HEADS):
            vs_ = slice(h * GLA_DV, (h + 1) * GLA_DV)
            o = f_ref[:, vs_] + b_ref[:, vs_]
            r = lax.rsqrt(jnp.mean(o * o, axis=-1, keepdims=True) + EPS)
            xh = o * r
            dp = dp_ref[:, vs_]
            dxh = dp * g_ref[:, vs_]
            do_ref[:, vs_] = r * (dxh - xh * jnp.mean(dxh * xh, axis=-1, keepdims=True))
            dg_ref[:, vs_] += jnp.sum(dp * xh, axis=0, keepdims=True)

    row = pl.BlockSpec((tm, _HV), lambda i: (i, 0))
    vec = pl.BlockSpec((1, _HV), lambda i: (0, 0))
    return pl.pallas_call(
        body, out_shape=(jax.ShapeDtypeStruct((T, _HV), F32), jax.ShapeDtypeStruct((1, _HV), F32)), grid=(T // tm,),
        in_specs=[row, row, vec, row], out_specs=(row, vec), name=name, compiler_params=_cp(("arbitrary",)))(of, ob, og, dpre)


_ANY = pl.BlockSpec(memory_space=pl.ANY)


def _coords():
    return lax.axis_index("x"), lax.axis_index("y"), lax.axis_index("c")


def _other_chips(x, y):
    return ((1 - x, y), (x, 1 - y), (1 - x, 1 - y))


def _gather_weights(arrays, name):
    n = len(arrays)

    def body(*refs):
        srcs, dsts = refs[:n], refs[n:2 * n]
        send_sems, recv_sems, local_sems = refs[2 * n:]
        x, y, c = _coords()
        me = 2 * x + y
        loc = [pltpu.make_async_copy(s, d.at[me], local_sems.at[i]) for i, (s, d) in enumerate(zip(srcs, dsts))]
        for cp in loc:
            cp.start()
        ici = []
        for j, (px, py) in enumerate(_other_chips(x, y)):
            for i, (s, d) in enumerate(zip(srcs, dsts)):
                ici.append(pltpu.make_async_remote_copy(
                    src_ref=s.at[c], dst_ref=d.at[me, c], send_sem=send_sems.at[n * j + i], recv_sem=recv_sems.at[n * j + i],
                    device_id=(px, py, c), device_id_type=MESH))
        for cp in ici:
            cp.start()
        fwd = []
        for j, (px, py) in enumerate(_other_chips(x, y)):
            for i, d in enumerate(dsts):
                ici[n * j + i].wait_recv()
                half = d.at[2 * px + py, c]
                cp = pltpu.make_async_remote_copy(
                    src_ref=half, dst_ref=half, send_sem=send_sems.at[3 * n + n * j + i], recv_sem=recv_sems.at[3 * n + n * j + i],
                    device_id=(x, y, 1 - c), device_id_type=MESH)
                cp.start()
                fwd.append(cp)
        for cp in fwd:
            cp.wait_recv()
        for cp in ici + fwd:
            cp.wait_send()
        for cp in loc:
            cp.wait()

    return pl.pallas_call(
        body, out_shape=tuple(jax.ShapeDtypeStruct((4,) + a.shape, a.dtype) for a in arrays),
        in_specs=[_ANY] * n, out_specs=(_ANY,) * n,
        scratch_shapes=[pltpu.SemaphoreType.DMA((6 * n,)), pltpu.SemaphoreType.DMA((6 * n,)), pltpu.SemaphoreType.DMA((n,))],
        name=name)(*arrays)


def _sibling_exchange(layered, whole, name):
    nl, n = len(layered), len(layered) + len(whole)

    def body(*refs):
        srcs, dsts = refs[:n], refs[n:2 * n]
        send_sems, recv_sems = refs[2 * n:]
        x, y, c = _coords()
        rem = [pltpu.make_async_remote_copy(src_ref=(s.at[1 - c] if i < nl else s), dst_ref=d, send_sem=send_sems.at[i],
                                            recv_sem=recv_sems.at[i], device_id=(x, y, 1 - c), device_id_type=MESH)
               for i, (s, d) in enumerate(zip(srcs, dsts))]
        for cp in rem:
            cp.start()
        for cp in rem:
            cp.wait()

    outs = [jax.ShapeDtypeStruct(a.shape[1:], a.dtype) for a in layered] + [jax.ShapeDtypeStruct(a.shape, a.dtype) for a in whole]
    return pl.pallas_call(
        body, out_shape=tuple(outs), in_specs=[_ANY] * n, out_specs=(_ANY,) * n,
        scratch_shapes=[pltpu.SemaphoreType.DMA((n,)), pltpu.SemaphoreType.DMA((n,))], name=name)(*layered, *whole)


def _chip_exchange(scatter, bcast, name):
    ns, n = len(scatter), len(scatter) + len(bcast)

    def body(*refs):
        srcs, dsts = refs[:n], refs[n:2 * n]
        send_sems, recv_sems, local_sems = refs[2 * n:]
        x, y, c = _coords()
        me = 2 * x + y
        loc = [pltpu.make_async_copy((s.at[me] if i < ns else s), d.at[me], local_sems.at[i])
               for i, (s, d) in enumerate(zip(srcs, dsts))]
        for cp in loc:
            cp.start()
        rem = []
        for j, (px, py) in enumerate(_other_chips(x, y)):
            for i, (s, d) in enumerate(zip(srcs, dsts)):
                rem.append(pltpu.make_async_remote_copy(
                    src_ref=(s.at[2 * px + py] if i < ns else s), dst_ref=d.at[me], send_sem=send_sems.at[n * j + i],
                    recv_sem=recv_sems.at[n * j + i], device_id=(px, py, c), device_id_type=MESH))
        for cp in rem:
            cp.start()
        for cp in rem:
            cp.wait()
        for cp in loc:
            cp.wait()

    outs = [jax.ShapeDtypeStruct(a.shape, a.dtype) for a in scatter] + [jax.ShapeDtypeStruct((4,) + a.shape, a.dtype) for a in bcast]
    return pl.pallas_call(
        body, out_shape=tuple(outs), in_specs=[_ANY] * n, out_specs=(_ANY,) * n,
        scratch_shapes=[pltpu.SemaphoreType.DMA((3 * n,)), pltpu.SemaphoreType.DMA((3 * n,)), pltpu.SemaphoreType.DMA((n,))],
        name=name)(*scatter, *bcast)


def _sum_slots(r, name):
    n, R, C = r.shape
    tm = min(256, R)

    def body(r_ref, o_ref):
        acc = r_ref[0].astype(F32)
        for i in range(1, n):
            acc = acc + r_ref[i].astype(F32)
        o_ref[...] = acc

    return pl.pallas_call(body, out_shape=jax.ShapeDtypeStruct((R, C), F32), grid=(R // tm,),
                          in_specs=[pl.BlockSpec((n, tm, C), lambda i: (0, i, 0))], out_specs=pl.BlockSpec((tm, C), lambda i: (i, 0)),
                          name=name, compiler_params=_cp(("parallel",)))(r)


def _add2(a, b, out_dtype, tm, name):
    R, C = a.shape
    tm = min(tm, R)

    def body(a_ref, b_ref, o_ref):
        o_ref[...] = (a_ref[...].astype(F32) + b_ref[...].astype(F32)).astype(out_dtype)

    blk = pl.BlockSpec((tm, C), lambda i: (i, 0))
    return pl.pallas_call(body, out_shape=jax.ShapeDtypeStruct((R, C), out_dtype), grid=(R // tm,), in_specs=[blk, blk],
                          out_specs=blk, name=name, compiler_params=_cp(("parallel",)))(a, b)


def _adamw_math(w, g, m, v):
    m = ADAM_B1 * m + (1.0 - ADAM_B1) * g
    v = ADAM_B2 * v + (1.0 - ADAM_B2) * (g * g)
    m_hat = m / (1.0 - ADAM_B1 ** ADAM_STEP)
    v_hat = v / (1.0 - ADAM_B2 ** ADAM_STEP)
    delta = -ADAM_LR * (m_hat / (jnp.sqrt(v_hat) + ADAM_EPS) + ADAM_WD * w)
    return delta, m, v


def _adamw(w, gs, m, v, tm, name):
    R, C = w.shape

    def body(*refs):
        w_ref = refs[0]
        g_refs = refs[1:1 + len(gs)]
        m_ref, v_ref, g_out, d_out, m_out, v_out = refs[1 + len(gs):]
        g = None
        for gr in g_refs:
            parts = [gr[i] for i in range(gr.shape[0])] if len(gr.shape) == 3 else [gr[...]]
            for p in parts:
                g = p if g is None else g + p
        d, mn, vn = _adamw_math(w_ref[...], g, m_ref[...], v_ref[...])
        g_out[...] = g
        d_out[...] = d
        m_out[...] = mn
        v_out[...] = vn

    blk = pl.BlockSpec((tm, C), lambda i: (i, 0))
    g_specs = [pl.BlockSpec((g.shape[0], tm, C), lambda i: (0, i, 0)) if g.ndim == 3 else blk for g in gs]
    return pl.pallas_call(
        body, out_shape=tuple(jax.ShapeDtypeStruct((R, C), F32) for _ in range(4)), grid=(R // tm,),
        in_specs=[blk] + g_specs + [blk, blk], out_specs=(blk,) * 4, name=name,
        compiler_params=_cp(("parallel",)))(w, *gs, m, v)


WEIGHTS = ("norm_g", "w_in", "conv_w", "conv_b", "conv_ln_g", "conv_ln_b", "na_q_g", "na_k_g", "na_rpb", "gla_a2_f",
           "gla_ab_f", "gla_a2_b", "gla_ab_b", "gla_o_g", "pool_w", "pool_scale", "w_out")
_REPL = ("norm_g", "conv_b", "conv_ln_g", "conv_ln_b", "na_q_g", "na_k_g", "na_rpb", "gla_ab_f", "gla_ab_b", "gla_o_g",
         "pool_w", "pool_scale")
_SHARD_SMALL = ("conv_w", "gla_a2_f", "gla_a2_b")
_PACK_ROWS = 8 * 128


def _pack(arrs):
    flat = jnp.concatenate([a.reshape(-1) for a in arrs])
    n = -(-flat.shape[0] // _PACK_ROWS) * _PACK_ROWS
    return jnp.pad(flat, (0, n - flat.shape[0])).reshape(-1, 128)


def _unpack(p, shapes):
    flat = p.reshape(-1)
    out, o = [], 0
    for s in shapes:
        n = int(np.prod(s))
        out.append(flat[o:o + n].reshape(s))
        o += n
    return out


def _to_layout(w):
    pad = jnp.zeros(w.shape[:-1] + (NZ - N_IN,), w.dtype)
    return jnp.concatenate([w[..., :5120], w[..., 5152:6176], w[..., 5120:5152], pad], axis=-1)


def _from_layout(w):
    return jnp.concatenate([w[..., :5120], w[..., LR_OFF:LR_OFF + 32], w[..., 5120:LR_OFF]], axis=-1)


def _reduce_gradients(p_a, p_b, small_g, ci):
    two = lambda a: a.reshape(-1, a.shape[-1])
    s_a, s_b, s_small = _sibling_exchange((p_a, p_b), (small_g,), "grad_to_sibling")
    mine = lambda a: lax.dynamic_index_in_dim(a, ci, 0, keepdims=False)
    c_a = _add2(two(mine(p_a)), two(s_a), BF16, 256, "chip_sum_a").reshape(s_a.shape)
    c_b = _add2(two(mine(p_b)), two(s_b), BF16, 256, "chip_sum_b").reshape(s_b.shape)
    c_small = _add2(small_g, s_small, F32, small_g.shape[0], "chip_sum_small")
    r_a, r_b, r_small = _chip_exchange((c_a, c_b), (c_small,), "grad_to_owner")
    own_a = _sum_slots(r_a, "sum_a")
    own_b = _sum_slots(r_b, "sum_b")
    sib_a, sib_b = _sibling_exchange((), (own_a, own_b), "reduced_to_sibling")
    by_layer = lambda own, sib: jnp.where(ci == 0, jnp.stack([own, sib]), jnp.stack([sib, own]))
    return by_layer(own_a, sib_a), by_layer(own_b, sib_b), r_small


def _layer_fwd(l, x, P, S):
    n = f"l{l}_"
    h = _rmsnorm_fwd(x, P["norm_g"], n + "rms_fwd")
    z = _matmul(h, P["w_in"], dims="nn", out_dtype=F32, tm=512, tn=1280, tk=D_MODEL, name=n + "mm_z")
    yc = _conv_fwd(z, P["conv_w32"], P["conv_b"], S, n + "conv_fwd")
    pre_a = _ln_silu_fwd(yc, P["conv_ln_g"], P["conv_ln_b"], n + "ln_fwd")
    pre_b = _na_fwd(z, P["na_q_g"], P["na_k_g"], P["na_bias"], S, n + "na_fwd")
    of, af, sf = _gla_fwd(z, P["a2_f"], P["gla_ab_f"], S, False, n + "gla_fwd_f")
    ob, ab, sb = _gla_fwd(z, P["a2_b"], P["gla_ab_b"], S, True, n + "gla_fwd_b")
    pre_c = _gla_norm_fwd(of, ob, P["gla_o_g"], n + "gla_norm_fwd")
    pre_d = _pool_fwd(z, P["pool_w_bf"], P["pool_scale"], S, n + "pool_fwd")
    pres = (pre_a, pre_b, pre_c, pre_d)
    y = _gate_fwd(pres, z, n + "gate_fwd")
    out = _matmul(y, P["w_out"], dims="nn", out_dtype=F32, tm=512, tn=1024, tk=D_MODEL, name=n + "mm_out", res=x)
    return out, dict(x=x, h=h, z=z, yc=yc, pres=pres, of=of, af=af, sf=sf, ob=ob, ab=ab, sb=sb, y=y)


def _layer_bwd(l, dout, dout_bf, sv, P, S):
    n = f"l{l}_"
    z = sv["z"]
    T = z.shape[0]
    dy = _matmul(dout_bf, P["w_out"], dims="nt", out_dtype=F32, tm=512, tn=1024, tk=D_MODEL, name=n + "mm_dy")
    d_w_out = _matmul(sv["y"], dout_bf, dims="tn", out_dtype=BF16, tm=512, tn=1024, tk=1024, name=n + "mm_dwout")
    dpa, dpb, dpc, dpd, dga, dgb, dgc, dgd = _gate_bwd(dy, sv["pres"], z, n + "gate_bwd")
    dyc, d_ln_g, d_ln_b = _ln_silu_bwd(sv["yc"], P["conv_ln_g"], P["conv_ln_b"], dpa, n + "ln_bwd")
    dval, dglu, d_cw, d_cb = _conv_bwd(z, P["conv_w32"], dyc, S, n + "conv_bwd")
    dq, dk, dv, dbias, d_qg, d_kg = _na_bwd(z, P["na_q_g"], P["na_k_g"], P["na_bias"], dpb, S, n + "na_bwd")
    d_rpb = _na_rpb_grad(dbias, n + "na_rpb")
    do, d_og = _gla_norm_bwd(sv["of"], sv["ob"], P["gla_o_g"], dpc, n + "gla_norm_bwd")
    part = _gla_bwd(z, P["a2_f"], P["gla_ab_f"], sv["af"], sv["sf"], do, None, S, False, n + "gla_bwd_f")
    d_a2f, d_abf = part[4], part[5]
    dcq, dck, dcv, dlr, d_a2b, d_abb = _gla_bwd(z, P["a2_b"], P["gla_ab_b"], sv["ab"], sv["sb"], do, part[:4], S, True,
                                                n + "gla_bwd_b")
    dd, d_pw, d_ps = _pool_bwd(z, P["pool_w_bf"], P["pool_scale"], dpd, S, n + "pool_bwd")
    dz = jnp.concatenate([dval, dglu, dga, dq, dk, dv, dgb, dcq, dck, dcv, dgc, dd, dgd, dlr,
                          jnp.zeros((T, NZ - LR_OFF - 128), BF16)], axis=1)
    dh = _matmul(dz, P["w_in"], dims="nt", out_dtype=F32, tm=512, tn=1024, tk=1280, name=n + "mm_dh")
    d_w_in = _matmul(sv["h"], dz, dims="tn", out_dtype=BF16, tm=512, tn=1280, tk=1024, name=n + "mm_dwin")
    dx, dx_bf, d_ng = _rmsnorm_bwd(sv["x"], P["norm_g"], dh, dout, n + "rms_bwd")
    grads = dict(norm_g=d_ng[0], w_in=d_w_in, conv_w=d_cw[:CONV_K], conv_b=d_cb[0], conv_ln_g=d_ln_g[0], conv_ln_b=d_ln_b[0],
                 na_q_g=d_qg.reshape(NA_HEADS, NA_DH), na_k_g=d_kg.reshape(NA_HEADS, NA_DH), na_rpb=d_rpb,
                 gla_a2_f=d_a2f[0:GLA_RANK], gla_ab_f=d_abf[0], gla_a2_b=d_a2b[GLA_RANK:2 * GLA_RANK], gla_ab_b=d_abb[0],
                 gla_o_g=d_og.reshape(GLA_HEADS, GLA_DV), pool_w=d_pw, pool_scale=d_ps[0], w_out=d_w_out)
    return dx, dx_bf, grads


def kernel(x, norm_g, w_in, conv_w, conv_b, conv_ln_g, conv_ln_b, na_q_g, na_k_g, na_rpb, gla_a2_f, gla_ab_f, gla_a2_b, gla_ab_b, gla_o_g, pool_w, pool_scale, w_out, loss_target, m_norm_g, m_w_in, m_conv_w, m_conv_b, m_conv_ln_g, m_conv_ln_b, m_na_q_g, m_na_k_g, m_na_rpb, m_gla_a2_f, m_gla_ab_f, m_gla_a2_b, m_gla_ab_b, m_gla_o_g, m_pool_w, m_pool_scale, m_w_out, v_norm_g, v_w_in, v_conv_w, v_conv_b, v_conv_ln_g, v_conv_ln_b, v_na_q_g, v_na_k_g, v_na_rpb, v_gla_a2_f, v_gla_ab_f, v_gla_a2_b, v_gla_ab_b, v_gla_o_g, v_pool_w, v_pool_scale, v_w_out):
    W = dict(norm_g=norm_g, w_in=w_in, conv_w=conv_w, conv_b=conv_b, conv_ln_g=conv_ln_g, conv_ln_b=conv_ln_b, na_q_g=na_q_g,
             na_k_g=na_k_g, na_rpb=na_rpb, gla_a2_f=gla_a2_f, gla_ab_f=gla_ab_f, gla_a2_b=gla_a2_b, gla_ab_b=gla_ab_b,
             gla_o_g=gla_o_g, pool_w=pool_w, pool_scale=pool_scale, w_out=w_out)
    M = dict(norm_g=m_norm_g, w_in=m_w_in, conv_w=m_conv_w, conv_b=m_conv_b, conv_ln_g=m_conv_ln_g, conv_ln_b=m_conv_ln_b,
             na_q_g=m_na_q_g, na_k_g=m_na_k_g, na_rpb=m_na_rpb, gla_a2_f=m_gla_a2_f, gla_ab_f=m_gla_ab_f, gla_a2_b=m_gla_a2_b,
             gla_ab_b=m_gla_ab_b, gla_o_g=m_gla_o_g, pool_w=m_pool_w, pool_scale=m_pool_scale, w_out=m_w_out)
    V = dict(norm_g=v_norm_g, w_in=v_w_in, conv_w=v_conv_w, conv_b=v_conv_b, conv_ln_g=v_conv_ln_g, conv_ln_b=v_conv_ln_b,
             na_q_g=v_na_q_g, na_k_g=v_na_k_g, na_rpb=v_na_rpb, gla_a2_f=v_gla_a2_f, gla_ab_f=v_gla_ab_f, gla_a2_b=v_gla_a2_b,
             gla_ab_b=v_gla_ab_b, gla_o_g=v_gla_o_g, pool_w=v_pool_w, pool_scale=v_pool_scale, w_out=v_w_out)
    E, S, D = x.shape
    T = E * S
    L = DEPTH
    xi, yi, ci = _coords()
    chip = 2 * xi + yi
    cw_sh, a2_sh = conv_w.shape[-1], gla_a2_f.shape[-1]

    small_sh = jnp.concatenate([
        jnp.pad(conv_w, ((0, 0), (0, 1), (0, 0))),
        jnp.pad(gla_a2_f, ((0, 0), (0, 0), (0, 128 - a2_sh))),
        jnp.pad(gla_a2_b, ((0, 0), (0, 0), (0, 128 - a2_sh)))], axis=1)
    g_win, g_wout, g_small = _gather_weights((w_in.astype(BF16), w_out.astype(BF16), small_sh), "gather_weights")
    w_in_full = _to_layout(jnp.transpose(g_win, (1, 2, 0, 3)).reshape(L, D, N_IN))
    w_out_full = jnp.transpose(g_wout, (1, 0, 2, 3)).reshape(L, D, D)
    conv_w_full = jnp.transpose(g_small[:, :, 0:32, :], (1, 2, 0, 3)).reshape(L, 32, 4 * cw_sh)
    a2f_full = jnp.transpose(g_small[:, :, 32:48, :a2_sh], (1, 2, 0, 3)).reshape(L, GLA_RANK, 4 * a2_sh)
    a2b_full = jnp.transpose(g_small[:, :, 48:64, :a2_sh], (1, 2, 0, 3)).reshape(L, GLA_RANK, 4 * a2_sh)

    params = []
    for l in range(L):
        params.append(dict(
            norm_g=norm_g[l][None], w_in=w_in_full[l], w_out=w_out_full[l], conv_w32=conv_w_full[l], conv_b=conv_b[l][None],
            conv_ln_g=conv_ln_g[l][None], conv_ln_b=conv_ln_b[l][None], na_q_g=na_q_g[l].reshape(1, GROUP_W),
            na_k_g=na_k_g[l].reshape(1, GROUP_W), na_bias=_na_bias(na_rpb[l], f"l{l}_na_bias"),
            a2_f=jnp.zeros((128, _HK), F32).at[0:GLA_RANK].set(a2f_full[l]),
            a2_b=jnp.zeros((128, _HK), F32).at[GLA_RANK:2 * GLA_RANK].set(a2b_full[l]),
            gla_ab_f=gla_ab_f[l][None], gla_ab_b=gla_ab_b[l][None], gla_o_g=gla_o_g[l].reshape(1, GROUP_W),
            pool_w_bf=pool_w[l].astype(BF16), pool_scale=pool_scale[l][None]))

    act = x.reshape(T, D)
    saved = []
    for l in range(L):
        act, sv = _layer_fwd(l, act, params[l], S)
        saved.append(sv)
    dact, dact_bf, loss_loc = _loss_head(act, loss_target.reshape(T, D), "loss_head")
    loss = lax.psum(loss_loc[0, 0], ("x", "y", "c"))
    grads = [None] * L
    for l in reversed(range(L)):
        dact, dact_bf, grads[l] = _layer_bwd(l, dact, dact_bf, saved[l], params[l], S)
    grad_x = dact.reshape(E, S, D)
    G = {k: jnp.stack([grads[l][k] for l in range(L)]) for k in WEIGHTS}

    cols_in, cols_out = N_IN // 4, D
    p_win = jnp.transpose(_from_layout(G["w_in"]).reshape(L, D, 4, cols_in), (0, 2, 1, 3))
    p_wout = G["w_out"].reshape(L, 4, D // 4, D)
    small_names = _REPL + _SHARD_SMALL
    small_g = _pack([G[k] for k in small_names])
    g_in, g_out, r_small = _reduce_gradients(p_win, p_wout, small_g, ci)

    rows_in, rows_out = L * D, L * (D // 4)
    res = {}
    res["w_in"] = [a.reshape(L, D, cols_in) for a in _adamw(
        w_in.reshape(rows_in, cols_in), (g_in.reshape(rows_in, cols_in),), m_w_in.reshape(rows_in, cols_in),
        v_w_in.reshape(rows_in, cols_in), 256, "adamw_w_in")]
    res["w_out"] = [a.reshape(L, D // 4, D) for a in _adamw(
        w_out.reshape(rows_out, cols_out), (g_out.reshape(rows_out, cols_out),), m_w_out.reshape(rows_out, cols_out),
        v_w_out.reshape(rows_out, cols_out), 256, "adamw_w_out")]
    zeros_sh = [jnp.zeros(G[k].shape, F32) for k in _SHARD_SMALL]
    pk = lambda dct: _pack([dct[k] for k in _REPL] + zeros_sh)
    small_res = _adamw(pk(W), (r_small,), pk(M), pk(V), small_g.shape[0], "adamw_small")
    shapes = [G[k].shape for k in small_names]
    unp = [_unpack(a, shapes) for a in small_res]
    for i, k in enumerate(_REPL):
        res[k] = [u[i] for u in unp]
    g_sh = []
    for i, k in enumerate(_SHARD_SMALL):
        gfull = unp[0][len(_REPL) + i]
        wdt = W[k].shape[-1]
        g_sh.append(lax.dynamic_slice_in_dim(gfull, chip * wdt, wdt, axis=2))
    g_sh_p = _pack(g_sh)
    sh_res = _adamw(_pack([W[k] for k in _SHARD_SMALL]), (g_sh_p,), _pack([M[k] for k in _SHARD_SMALL]),
                    _pack([V[k] for k in _SHARD_SMALL]), g_sh_p.shape[0], "adamw_shard_small")
    shapes2 = [W[k].shape for k in _SHARD_SMALL]
    unp2 = [_unpack(a, shapes2) for a in sh_res]
    for i, k in enumerate(_SHARD_SMALL):
        res[k] = [u[i] for u in unp2]

    outs = [loss, grad_x]
    for j in range(4):
        outs += [res[k][j] for k in WEIGHTS]
    return tuple(outs)
```

```python
import functools

import numpy as np
import jax
import jax.numpy as jnp
from jax import lax
from jax.experimental import pallas as pl
from jax.experimental.pallas import tpu as pltpu

F32 = jnp.float32
BF16 = jnp.bfloat16
HI = lax.Precision.HIGHEST
MESH = pl.DeviceIdType.MESH

EPS = 1e-6
D_MODEL = 2048
GROUP_W = 512
SEQ = 2048
DEPTH = 2
N_IN = 6176
GRID_W = 64
CONV_K = 31
NA_HEADS = 8
NA_DH = 64
NA_ROWS = 8
NA_COLS = 16
GLA_HEADS = 4
GLA_DK = 64
GLA_DV = 128
GLA_RANK = 16
GLA_TAU = 16.0
CHUNK = 64
POOL_WINDOWS = (2, 4, 8, 16)
ADAM_LR, ADAM_B1, ADAM_B2, ADAM_EPS, ADAM_WD, ADAM_STEP = 0.001, 0.9, 0.999, 1e-08, 0.01, 10

A_VAL, A_GLU, A_GATE = 0, 512, 1024
B_Q, B_K, B_V, B_GATE = 1536, 2048, 2560, 3072
C_Q, C_K, C_V, C_GATE = 3584, 3840, 4096, 4608
D_VAL, D_GATE = 5120, 5632
LR_OFF = 6144
NZ = 6400
NEG = -1e30
VMEM_LIMIT = 56 * 1024 * 1024


def _cp(sem=None):
    return pltpu.CompilerParams(dimension_semantics=sem, vmem_limit_bytes=VMEM_LIMIT)


def _sigmoid(x):
    return 1.0 / (1.0 + jnp.exp(-x))


def _silu(x):
    return x * _sigmoid(x)


def _dsilu(x):
    s = _sigmoid(x)
    return s * (1.0 + x * (1.0 - s))


def _matmul(a, b, *, dims, out_dtype, tm, tn, tk, name, res=None):
    if dims == "nn":
        (M, K), N = a.shape, b.shape[1]
    elif dims == "nt":
        (M, K), N = a.shape, b.shape[0]
    else:
        (K, M), N = a.shape, b.shape[1]
    tm, tn, tk = min(tm, M), min(tn, N), min(tk, K)
    nk = K // tk
    assert M % tm == 0 and N % tn == 0 and K % tk == 0, (M, N, K, tm, tn, tk)
    dn = {"nn": (((1,), (0,)), ((), ())), "nt": (((1,), (1,)), ((), ())), "tn": (((0,), (0,)), ((), ()))}[dims]
    if dims == "tn":
        a_spec = pl.BlockSpec((tk, tm), lambda i, j, k: (k, i))
    else:
        a_spec = pl.BlockSpec((tm, tk), lambda i, j, k: (i, k))
    if dims == "nt":
        b_spec = pl.BlockSpec((tn, tk), lambda i, j, k: (j, k))
    else:
        b_spec = pl.BlockSpec((tk, tn), lambda i, j, k: (k, j))
    o_spec = pl.BlockSpec((tm, tn), lambda i, j, k: (i, j))
    has_res = res is not None

    def body(*refs):
        if has_res:
            a_ref, b_ref, r_ref, o_ref, acc = refs
        else:
            a_ref, b_ref, o_ref, acc = refs
        k = pl.program_id(2)

        @pl.when(k == 0)
        def _():
            acc[...] = jnp.zeros_like(acc)

        acc[...] += lax.dot_general(a_ref[...], b_ref[...], dn, preferred_element_type=F32)

        @pl.when(k == nk - 1)
        def _():
            r = acc[...]
            if has_res:
                r = r + r_ref[...]
            o_ref[...] = r.astype(o_ref.dtype)

    in_specs = [a_spec, b_spec] + ([o_spec] if has_res else [])
    args = (a, b) + ((res,) if has_res else ())
    return pl.pallas_call(
        body, out_shape=jax.ShapeDtypeStruct((M, N), out_dtype), grid=(M // tm, N // tn, nk),
        in_specs=in_specs, out_specs=o_spec, scratch_shapes=[pltpu.VMEM((tm, tn), F32)],
        name=name, compiler_params=_cp(("parallel", "parallel", "arbitrary")))(*args)


def _rmsnorm_fwd(x, g, name):
    T, D = x.shape
    tm = 256

    def body(x_ref, g_ref, h_ref):
        xv = x_ref[...]
        r = lax.rsqrt(jnp.mean(xv * xv, axis=-1, keepdims=True) + EPS)
        h_ref[...] = (xv * r * g_ref[...]).astype(h_ref.dtype)

    return pl.pallas_call(
        body, out_shape=jax.ShapeDtypeStruct((T, D), BF16), grid=(T // tm,),
        in_specs=[pl.BlockSpec((tm, D), lambda i: (i, 0)), pl.BlockSpec((1, D), lambda i: (0, 0))],
        out_specs=pl.BlockSpec((tm, D), lambda i: (i, 0)), name=name, compiler_params=_cp(("parallel",)))(x, g)


def _rmsnorm_bwd(x, g, dh, dres, name):
    T, D = x.shape
    tm = 256

    def body(x_ref, g_ref, dh_ref, dres_ref, dx_ref, dxb_ref, dg_ref):
        xv = x_ref[...]
        r = lax.rsqrt(jnp.mean(xv * xv, axis=-1, keepdims=True) + EPS)
        xh = xv * r
        dh_v = dh_ref[...]
        dxh = dh_v * g_ref[...]
        dx = r * (dxh - xh * jnp.mean(dxh * xh, axis=-1, keepdims=True)) + dres_ref[...]
        dx_ref[...] = dx
        dxb_ref[...] = dx.astype(BF16)

        @pl.when(pl.program_id(0) == 0)
        def _():
            dg_ref[...] = jnp.zeros_like(dg_ref)

        dg_ref[...] += jnp.sum(dh_v * xh, axis=0, keepdims=True)

    row = pl.BlockSpec((tm, D), lambda i: (i, 0))
    vec = pl.BlockSpec((1, D), lambda i: (0, 0))
    return pl.pallas_call(
        body, out_shape=(jax.ShapeDtypeStruct((T, D), F32), jax.ShapeDtypeStruct((T, D), BF16), jax.ShapeDtypeStruct((1, D), F32)),
        grid=(T // tm,), in_specs=[row, vec, row, row], out_specs=(row, row, vec), name=name,
        compiler_params=_cp(("arbitrary",)))(x, g, dh, dres)


def _loss_head(y, target, name):
    T, D = y.shape
    tm = 256

    def body(y_ref, t_ref, d_ref, db_ref, l_ref):
        e = y_ref[...] - t_ref[...]
        d = e * (1.0 / D)
        d_ref[...] = d
        db_ref[...] = d.astype(BF16)

        @pl.when(pl.program_id(0) == 0)
        def _():
            l_ref[...] = jnp.zeros_like(l_ref)

        row = jnp.sum(e * e, axis=-1, keepdims=True) * (0.5 / D)
        l_ref[...] += jnp.sum(row, axis=0, keepdims=True)

    row = pl.BlockSpec((tm, D), lambda i: (i, 0))
    return pl.pallas_call(
        body, out_shape=(jax.ShapeDtypeStruct((T, D), F32), jax.ShapeDtypeStruct((T, D), BF16),
                         jax.ShapeDtypeStruct((1, 1), F32)), grid=(T // tm,),
        in_specs=[row, row], out_specs=(row, row, pl.BlockSpec((1, 1), lambda i: (0, 0))),
        name=name, compiler_params=_cp(("arbitrary",)))(y, target)


_GATE_COLS = (A_GATE // GROUP_W, B_GATE // GROUP_W, C_GATE // GROUP_W, D_GATE // GROUP_W)


def _gate_fwd(pres, z, name):
    T = z.shape[0]
    tm = 256

    def body(pa, pb, pc, pd, ga, gb, gc, gd, y_ref):
        for n, (p, g) in enumerate(((pa, ga), (pb, gb), (pc, gc), (pd, gd))):
            y_ref[:, n * GROUP_W:(n + 1) * GROUP_W] = (p[...] * _silu(g[...])).astype(BF16)

    pre_spec = pl.BlockSpec((tm, GROUP_W), lambda i: (i, 0))
    gate_specs = [pl.BlockSpec((tm, GROUP_W), functools.partial(lambda i, c: (i, c), c=c)) for c in _GATE_COLS]
    return pl.pallas_call(
        body, out_shape=jax.ShapeDtypeStruct((T, 4 * GROUP_W), BF16), grid=(T // tm,),
        in_specs=[pre_spec] * 4 + gate_specs, out_specs=pl.BlockSpec((tm, 4 * GROUP_W), lambda i: (i, 0)),
        name=name, compiler_params=_cp(("parallel",)))(*pres, z, z, z, z)


def _gate_bwd(dy, pres, z, name):
    T = z.shape[0]
    tm = 256

    def body(dy_ref, pa, pb, pc, pd, ga, gb, gc, gd, dpa, dpb, dpc, dpd, dga, dgb, dgc, dgd):
        for n, (p, g, dp, dg) in enumerate(((pa, ga, dpa, dga), (pb, gb, dpb, dgb), (pc, gc, dpc, dgc), (pd, gd, dpd, dgd))):
            d = dy_ref[:, n * GROUP_W:(n + 1) * GROUP_W]
            gv = g[...]
            dp[...] = d * _silu(gv)
            dg[...] = (d * p[...] * _dsilu(gv)).astype(BF16)

    pre_spec = pl.BlockSpec((tm, GROUP_W), lambda i: (i, 0))
    gate_specs = [pl.BlockSpec((tm, GROUP_W), functools.partial(lambda i, c: (i, c), c=c)) for c in _GATE_COLS]
    outs = tuple([jax.ShapeDtypeStruct((T, GROUP_W), F32)] * 4 + [jax.ShapeDtypeStruct((T, GROUP_W), BF16)] * 4)
    return pl.pallas_call(
        body, out_shape=outs, grid=(T // tm,),
        in_specs=[pl.BlockSpec((tm, 4 * GROUP_W), lambda i: (i, 0))] + [pre_spec] * 4 + gate_specs,
        out_specs=tuple([pre_spec] * 8), name=name, compiler_params=_cp(("parallel",)))(dy, *pres, z, z, z, z)


_PAD = 16
_RC = 256


def _conv_fwd(z, conv_w32, conv_b, S, name):
    T = z.shape[0]
    E = T // S
    LW = 128

    def body(val_ref, glu_ref, w_ref, b_ref, y_ref, upad):
        upad[0:_PAD, :] = jnp.zeros((_PAD, LW), F32)
        upad[_PAD + S:_PAD + S + _PAD, :] = jnp.zeros((_PAD, LW), F32)
        upad[_PAD:_PAD + S, :] = val_ref[...] * _sigmoid(glu_ref[...])
        for r in range(S // _RC):
            acc = jnp.broadcast_to(b_ref[...], (_RC, LW))
            for k in range(CONV_K):
                st = r * _RC + k + 1
                acc = acc + upad[st:st + _RC, :] * w_ref[k:k + 1, :]
            y_ref[r * _RC:(r + 1) * _RC, :] = acc

    return pl.pallas_call(
        body, out_shape=jax.ShapeDtypeStruct((T, GROUP_W), F32), grid=(E, GROUP_W // LW),
        in_specs=[pl.BlockSpec((S, LW), lambda e, j: (e, A_VAL // LW + j)),
                  pl.BlockSpec((S, LW), lambda e, j: (e, A_GLU // LW + j)),
                  pl.BlockSpec((32, LW), lambda e, j: (0, j)),
                  pl.BlockSpec((1, LW), lambda e, j: (0, j))],
        out_specs=pl.BlockSpec((S, LW), lambda e, j: (e, j)),
        scratch_shapes=[pltpu.VMEM((S + 2 * _PAD, LW), F32)],
        name=name, compiler_params=_cp(("parallel", "parallel")))(z, z, conv_w32, conv_b)


def _conv_bwd(z, conv_w32, dyc, S, name):
    T = z.shape[0]
    E = T // S
    LW = 128

    def body(val_ref, glu_ref, w_ref, dy_ref, dval_ref, dglu_ref, dw_ref, db_ref, upad, dpad):
        e = pl.program_id(1)
        zeros = jnp.zeros((_PAD, LW), F32)
        upad[0:_PAD, :] = zeros
        upad[_PAD + S:_PAD + S + _PAD, :] = zeros
        dpad[0:_PAD, :] = zeros
        dpad[_PAD + S:_PAD + S + _PAD, :] = zeros
        upad[_PAD:_PAD + S, :] = val_ref[...] * _sigmoid(glu_ref[...])
        dpad[_PAD:_PAD + S, :] = dy_ref[...]

        @pl.when(e == 0)
        def _():
            dw_ref[...] = jnp.zeros_like(dw_ref)
            db_ref[...] = jnp.zeros_like(db_ref)

        db_ref[...] += jnp.sum(dy_ref[...], axis=0, keepdims=True)
        for r in range(S // _RC):
            dyr = dy_ref[r * _RC:(r + 1) * _RC, :]
            du = jnp.zeros((_RC, LW), F32)
            for k in range(CONV_K):
                st = r * _RC + k + 1
                dw_ref[k:k + 1, :] += jnp.sum(dyr * upad[st:st + _RC, :], axis=0, keepdims=True)
                sd = r * _RC + (CONV_K - 1 - k) + 1
                du = du + dpad[sd:sd + _RC, :] * w_ref[k:k + 1, :]
            sl = slice(r * _RC, (r + 1) * _RC)
            val = val_ref[sl, :]
            sg = _sigmoid(glu_ref[sl, :])
            dval_ref[sl, :] = (du * sg).astype(BF16)
            dglu_ref[sl, :] = (du * val * sg * (1.0 - sg)).astype(BF16)

    blk = pl.BlockSpec((S, LW), lambda j, e: (e, j))
    return pl.pallas_call(
        body, out_shape=(jax.ShapeDtypeStruct((T, GROUP_W), BF16), jax.ShapeDtypeStruct((T, GROUP_W), BF16),
                         jax.ShapeDtypeStruct((32, GROUP_W), F32), jax.ShapeDtypeStruct((1, GROUP_W), F32)),
        grid=(GROUP_W // LW, E),
        in_specs=[pl.BlockSpec((S, LW), lambda j, e: (e, A_VAL // LW + j)),
                  pl.BlockSpec((S, LW), lambda j, e: (e, A_GLU // LW + j)),
                  pl.BlockSpec((32, LW), lambda j, e: (0, j)), blk],
        out_specs=(blk, blk, pl.BlockSpec((32, LW), lambda j, e: (0, j)), pl.BlockSpec((1, LW), lambda j, e: (0, j))),
        scratch_shapes=[pltpu.VMEM((S + 2 * _PAD, LW), F32), pltpu.VMEM((S + 2 * _PAD, LW), F32)],
        name=name, compiler_params=_cp(("parallel", "arbitrary")))(z, z, conv_w32, dyc)


def _ln_silu_fwd(yc, g, b, name):
    T, C = yc.shape
    tm = 256

    def body(y_ref, g_ref, b_ref, o_ref):
        y = y_ref[...]
        mu = jnp.mean(y, axis=-1, keepdims=True)
        yc_ = y - mu
        r = lax.rsqrt(jnp.mean(yc_ * yc_, axis=-1, keepdims=True) + EPS)
        o_ref[...] = _silu(yc_ * r * g_ref[...] + b_ref[...])

    row = pl.BlockSpec((tm, C), lambda i: (i, 0))
    vec = pl.BlockSpec((1, C), lambda i: (0, 0))
    return pl.pallas_call(body, out_shape=jax.ShapeDtypeStruct((T, C), F32), grid=(T // tm,),
                          in_specs=[row, vec, vec], out_specs=row, name=name, compiler_params=_cp(("parallel",)))(yc, g, b)


def _ln_silu_bwd(yc, g, b, dpre, name):
    T, C = yc.shape
    tm = 256

    def body(y_ref, g_ref, b_ref, dp_ref, dy_ref, dg_ref, db_ref):
        y = y_ref[...]
        mu = jnp.mean(y, axis=-1, keepdims=True)
        yc_ = y - mu
        r = lax.rsqrt(jnp.mean(yc_ * yc_, axis=-1, keepdims=True) + EPS)
        xh = yc_ * r
        gv = g_ref[...]
        dln = dp_ref[...] * _dsilu(xh * gv + b_ref[...])
        dxh = dln * gv
        dy_ref[...] = r * (dxh - jnp.mean(dxh, axis=-1, keepdims=True) - xh * jnp.mean(dxh * xh, axis=-1, keepdims=True))

        @pl.when(pl.program_id(0) == 0)
        def _():
            dg_ref[...] = jnp.zeros_like(dg_ref)
            db_ref[...] = jnp.zeros_like(db_ref)

        dg_ref[...] += jnp.sum(dln * xh, axis=0, keepdims=True)
        db_ref[...] += jnp.sum(dln, axis=0, keepdims=True)

    row = pl.BlockSpec((tm, C), lambda i: (i, 0))
    vec = pl.BlockSpec((1, C), lambda i: (0, 0))
    return pl.pallas_call(
        body, out_shape=(jax.ShapeDtypeStruct((T, C), F32), jax.ShapeDtypeStruct((1, C), F32), jax.ShapeDtypeStruct((1, C), F32)),
        grid=(T // tm,), in_specs=[row, vec, vec, row], out_specs=(row, vec, vec), name=name,
        compiler_params=_cp(("arbitrary",)))(yc, g, b, dpre)


def _pool_counts(S, w, rows0, n):
    t = (lax.broadcasted_iota(jnp.int32, (n, 1), 0) + rows0)
    lo = jnp.maximum(t - w // 2, 0)
    hi = jnp.minimum(t + w // 2, S)
    return (hi - lo).astype(F32)


def _pool_fwd(z, pool_w, pool_scale, S, name):
    T = z.shape[0]
    E = T // S
    CG = 128

    def body(u_ref, w_ref, s_ref, o_ref, upad, dif):
        zeros = jnp.zeros((_PAD, GROUP_W), F32)
        upad[0:_PAD, :] = zeros
        upad[_PAD + S:_PAD + S + _PAD, :] = zeros
        upad[_PAD:_PAD + S, :] = u_ref[...]
        for gi, w in enumerate(POOL_WINDOWS):
            ls = slice(gi * CG, (gi + 1) * CG)
            for r in range(S // _RC):
                acc = jnp.zeros((_RC, CG), F32)
                for j in range(-(w // 2), w // 2):
                    st = _PAD + r * _RC + j
                    acc = acc + upad[st:st + _RC, ls]
                cnt = _pool_counts(S, w, r * _RC, _RC)
                dif[r * _RC:(r + 1) * _RC, :] = (acc / cnt - u_ref[r * _RC:(r + 1) * _RC, ls]).astype(BF16)
            yp = jnp.dot(dif[...], w_ref[gi], preferred_element_type=F32)
            o_ref[:, ls] = yp * s_ref[:, ls]

    return pl.pallas_call(
        body, out_shape=jax.ShapeDtypeStruct((T, GROUP_W), F32), grid=(E,),
        in_specs=[pl.BlockSpec((S, GROUP_W), lambda e: (e, D_VAL // GROUP_W)),
                  pl.BlockSpec((4, CG, CG), lambda e: (0, 0, 0)),
                  pl.BlockSpec((1, GROUP_W), lambda e: (0, 0))],
        out_specs=pl.BlockSpec((S, GROUP_W), lambda e: (e, 0)),
        scratch_shapes=[pltpu.VMEM((S + 2 * _PAD, GROUP_W), F32), pltpu.VMEM((S, CG), BF16)],
        name=name, compiler_params=_cp(("parallel",)))(z, pool_w, pool_scale)


def _pool_bwd(z, pool_w, pool_scale, dpre, S, name):
    T = z.shape[0]
    E = T // S
    CG = 128

    def body(u_ref, w_ref, s_ref, dp_ref, du_ref, dw_ref, ds_ref, upad, dif, qpad):
        zeros = jnp.zeros((_PAD, GROUP_W), F32)
        upad[0:_PAD, :] = zeros
        upad[_PAD + S:_PAD + S + _PAD, :] = zeros
        upad[_PAD:_PAD + S, :] = u_ref[...]
        zc = jnp.zeros((_PAD, CG), F32)
        qpad[0:_PAD, :] = zc
        qpad[_PAD + S:_PAD + S + _PAD, :] = zc

        @pl.when(pl.program_id(0) == 0)
        def _():
            dw_ref[...] = jnp.zeros_like(dw_ref)
            ds_ref[...] = jnp.zeros_like(ds_ref)

        for gi, w in enumerate(POOL_WINDOWS):
            ls = slice(gi * CG, (gi + 1) * CG)
            for r in range(S // _RC):
                acc = jnp.zeros((_RC, CG), F32)
                for j in range(-(w // 2), w // 2):
                    st = _PAD + r * _RC + j
                    acc = acc + upad[st:st + _RC, ls]
                cnt = _pool_counts(S, w, r * _RC, _RC)
                dif[r * _RC:(r + 1) * _RC, :] = (acc / cnt - u_ref[r * _RC:(r + 1) * _RC, ls]).astype(BF16)
            dp = dp_ref[:, ls]
            yp = jnp.dot(dif[...], w_ref[gi], preferred_element_type=F32)
            ds_ref[:, ls] += jnp.sum(dp * yp, axis=0, keepdims=True)
            dys = (dp * s_ref[:, ls]).astype(BF16)
            dw_ref[gi] += lax.dot_general(dif[...], dys, (((0,), (0,)), ((), ())), preferred_element_type=F32)
            dm = lax.dot_general(dys, w_ref[gi], (((1,), (1,)), ((), ())), preferred_element_type=F32)
            for r in range(S // _RC):
                cnt = _pool_counts(S, w, r * _RC, _RC)
                qpad[_PAD + r * _RC:_PAD + (r + 1) * _RC, :] = dm[r * _RC:(r + 1) * _RC, :] / cnt
            for r in range(S // _RC):
                acc = -dm[r * _RC:(r + 1) * _RC, :]
                for j in range(-(w // 2) + 1, w // 2 + 1):
                    st = _PAD + r * _RC + j
                    acc = acc + qpad[st:st + _RC, :]
                du_ref[r * _RC:(r + 1) * _RC, ls] = acc.astype(BF16)

    return pl.pallas_call(
        body, out_shape=(jax.ShapeDtypeStruct((T, GROUP_W), BF16), jax.ShapeDtypeStruct((4, CG, CG), F32),
                         jax.ShapeDtypeStruct((1, GROUP_W), F32)), grid=(E,),
        in_specs=[pl.BlockSpec((S, GROUP_W), lambda e: (e, D_VAL // GROUP_W)),
                  pl.BlockSpec((4, CG, CG), lambda e: (0, 0, 0)),
                  pl.BlockSpec((1, GROUP_W), lambda e: (0, 0)),
                  pl.BlockSpec((S, GROUP_W), lambda e: (e, 0))],
        out_specs=(pl.BlockSpec((S, GROUP_W), lambda e: (e, 0)), pl.BlockSpec((4, CG, CG), lambda e: (0, 0, 0)),
                   pl.BlockSpec((1, GROUP_W), lambda e: (0, 0))),
        scratch_shapes=[pltpu.VMEM((S + 2 * _PAD, GROUP_W), F32), pltpu.VMEM((S, CG), BF16),
                        pltpu.VMEM((S + 2 * _PAD, CG), F32)],
        name=name, compiler_params=_cp(("arbitrary",)))(z, pool_w, pool_scale, dpre)


def _na_tables():
    d = np.arange(NA_ROWS)[:, None]
    kr = np.arange(NA_ROWS)[None, :]
    ro = kr - d + (NA_ROWS - 1)
    qc = np.arange(GRID_W)[:, None]
    kc = np.arange(GRID_W)[None, :]
    cs = np.clip(qc - NA_COLS // 2, 0, GRID_W - NA_COLS)
    valid = (kc >= cs) & (kc < cs + NA_COLS)
    co = np.clip(kc - qc + (NA_COLS - 1), 0, 2 * NA_COLS - 2)
    return ro, co, valid


def _na_onehots():
    ro, co, valid = _na_tables()
    e_np = np.zeros((GRID_W, GRID_W, 128), np.float32)
    qi, ki = np.nonzero(valid)
    e_np[qi, ki, co[qi, ki]] = 1.0
    a_np = np.zeros((16, NA_ROWS * NA_ROWS), np.float32)
    a_np[ro.reshape(-1), np.arange(NA_ROWS * NA_ROWS)] = 1.0
    mask = np.where(valid, 0.0, NEG).astype(np.float32).reshape(1, GRID_W * GRID_W)
    return e_np.reshape(GRID_W * GRID_W, 128), a_np, mask


def _na_bias(rpb, name):
    e_np, a_np, mask = _na_onehots()
    H = NA_HEADS
    rp = jnp.pad(rpb, ((0, 0), (0, 1), (0, 128 - rpb.shape[2])))

    def body(r_ref, e_ref, at_ref, m_ref, o_ref):
        t = jnp.dot(at_ref[...], r_ref[0], precision=HI, preferred_element_type=F32)
        o_ref[0] = lax.dot_general(t, e_ref[...], (((1,), (1,)), ((), ())), precision=HI,
                                   preferred_element_type=F32) + m_ref[...]

    out = pl.pallas_call(
        body, out_shape=jax.ShapeDtypeStruct((H, NA_ROWS * NA_ROWS, GRID_W * GRID_W), F32), grid=(H,),
        in_specs=[pl.BlockSpec((1, 16, 128), lambda h: (h, 0, 0)),
                  pl.BlockSpec((GRID_W * GRID_W, 128), lambda h: (0, 0)),
                  pl.BlockSpec((NA_ROWS * NA_ROWS, 16), lambda h: (0, 0)),
                  pl.BlockSpec((1, GRID_W * GRID_W), lambda h: (0, 0))],
        out_specs=pl.BlockSpec((1, NA_ROWS * NA_ROWS, GRID_W * GRID_W), lambda h: (h, 0, 0)),
        name=name, compiler_params=_cp(("parallel",)))(rp, jnp.asarray(e_np), jnp.asarray(a_np.T), jnp.asarray(mask))
    t = out.reshape(H, NA_ROWS, NA_ROWS, GRID_W, GRID_W)
    return jnp.transpose(t, (0, 1, 3, 2, 4)).reshape(H, NA_ROWS, GRID_W, NA_ROWS * GRID_W)


def _seg_mean_matrix(width, seg):
    i = np.arange(width)
    return jnp.asarray((i[:, None] // seg == i[None, :] // seg).astype(np.float32) / seg)


def _na_fwd(z, qg, kg, bias, S, name):
    T = z.shape[0]
    E = T // S
    rows = S // GRID_W
    WIN = NA_ROWS * GRID_W
    seg = _seg_mean_matrix(128, NA_DH)

    def body(q_ref, k_ref, v_ref, qg_ref, kg_ref, bias_ref, seg_ref, o_ref, qs, ks, vs, s_all, p_all):
        for c in range(S // _RC):
            sl = slice(c * _RC, (c + 1) * _RC)
            q = q_ref[sl, :]
            k = k_ref[sl, :]
            qn = q * lax.rsqrt(jnp.dot(q * q, seg_ref[...], precision=HI, preferred_element_type=F32) + EPS) * qg_ref[...]
            kn = k * lax.rsqrt(jnp.dot(k * k, seg_ref[...], precision=HI, preferred_element_type=F32) + EPS) * kg_ref[...]
            v = v_ref[sl, :]
            for hh in range(2):
                ls = slice(hh * NA_DH, (hh + 1) * NA_DH)
                qs[hh, sl, :] = qn[:, ls].astype(BF16)
                ks[hh, sl, :] = kn[:, ls].astype(BF16)
                vs[hh, sl, :] = v[:, ls].astype(BF16)
        def where(r):
            rs = jnp.clip(r - NA_ROWS // 2, 0, rows - NA_ROWS)
            return rs, pl.multiple_of(r * GRID_W, GRID_W), pl.multiple_of(rs * GRID_W, GRID_W)

        def scores(r, carry):
            rs, q0, k0 = where(r)
            for hh in range(2):
                s = lax.dot_general(qs[hh, pl.ds(q0, GRID_W), :], ks[hh, pl.ds(k0, WIN), :], (((1,), (1,)), ((), ())),
                                    preferred_element_type=F32) * (NA_DH ** -0.5)
                s_all[hh, pl.ds(q0, GRID_W), :] = s + bias_ref[hh, r - rs]
            return carry
        lax.fori_loop(0, rows, scores, 0, unroll=4)

        def soft(r, carry):
            _, q0, _ = where(r)
            for hh in range(2):
                s = s_all[hh, pl.ds(q0, GRID_W), :]
                p = jnp.exp(s - jnp.max(s, axis=-1, keepdims=True))
                p_all[hh, pl.ds(q0, GRID_W), :] = (p * (1.0 / jnp.sum(p, axis=-1, keepdims=True))).astype(BF16)
            return carry
        lax.fori_loop(0, rows, soft, 0, unroll=2)

        def outp(r, carry):
            _, q0, k0 = where(r)
            outs = [jnp.dot(p_all[hh, pl.ds(q0, GRID_W), :], vs[hh, pl.ds(k0, WIN), :], preferred_element_type=F32)
                    for hh in range(2)]
            o_ref[pl.ds(q0, GRID_W), :] = jnp.concatenate(outs, axis=1)
            return carry
        lax.fori_loop(0, rows, outp, 0, unroll=4)

    LW = 128
    return pl.pallas_call(
        body, out_shape=jax.ShapeDtypeStruct((T, GROUP_W), F32), grid=(E, GROUP_W // LW),
        in_specs=[pl.BlockSpec((S, LW), lambda e, j: (e, B_Q // LW + j)),
                  pl.BlockSpec((S, LW), lambda e, j: (e, B_K // LW + j)),
                  pl.BlockSpec((S, LW), lambda e, j: (e, B_V // LW + j)),
                  pl.BlockSpec((1, LW), lambda e, j: (0, j)),
                  pl.BlockSpec((1, LW), lambda e, j: (0, j)),
                  pl.BlockSpec((2, NA_ROWS, GRID_W, WIN), lambda e, j: (j, 0, 0, 0)),
                  pl.BlockSpec((LW, LW), lambda e, j: (0, 0))],
        out_specs=pl.BlockSpec((S, LW), lambda e, j: (e, j)),
        scratch_shapes=[pltpu.VMEM((2, S, NA_DH), BF16)] * 3 + [pltpu.VMEM((2, S, WIN), F32), pltpu.VMEM((2, S, WIN), BF16)],
        name=name, compiler_params=_cp(("parallel", "parallel")))(z, z, z, qg, kg, bias, seg)


def _na_bwd(z, qg, kg, bias, do, S, name):
    T = z.shape[0]
    E = T // S
    rows = S // GRID_W
    WIN = NA_ROWS * GRID_W
    seg = _seg_mean_matrix(128, NA_DH)
    SC = NA_DH ** -0.5

    def body(q_ref, k_ref, v_ref, qg_ref, kg_ref, bias_ref, seg_ref, do_ref,
             dq_ref, dk_ref, dv_ref, dbias_ref, dqg_ref, dkg_ref, qs, ks, vs, dos, dqn, dkn, dvs, akt, avt,
             s_all, dp_all, p_all, ds_all):
        e = pl.program_id(1)

        @pl.when(e == 0)
        def _():
            dbias_ref[...] = jnp.zeros_like(dbias_ref)
            dqg_ref[...] = jnp.zeros_like(dqg_ref)
            dkg_ref[...] = jnp.zeros_like(dkg_ref)

        for c in range(S // _RC):
            sl = slice(c * _RC, (c + 1) * _RC)
            q = q_ref[sl, :]
            k = k_ref[sl, :]
            qn = q * lax.rsqrt(jnp.dot(q * q, seg_ref[...], precision=HI, preferred_element_type=F32) + EPS) * qg_ref[...]
            kn = k * lax.rsqrt(jnp.dot(k * k, seg_ref[...], precision=HI, preferred_element_type=F32) + EPS) * kg_ref[...]
            v = v_ref[sl, :]
            dd = do_ref[sl, :]
            for hh in range(2):
                ls = slice(hh * NA_DH, (hh + 1) * NA_DH)
                qs[hh, sl, :] = qn[:, ls].astype(BF16)
                ks[hh, sl, :] = kn[:, ls].astype(BF16)
                vs[hh, sl, :] = v[:, ls].astype(BF16)
                dos[hh, sl, :] = dd[:, ls].astype(BF16)
        akt[...] = jnp.zeros_like(akt)
        avt[...] = jnp.zeros_like(avt)

        def where(r):
            rs = jnp.clip(r - NA_ROWS // 2, 0, rows - NA_ROWS)
            return rs, pl.multiple_of(r * GRID_W, GRID_W), pl.multiple_of(rs * GRID_W, GRID_W)

        for hh in range(2):
            ls = slice(hh * NA_DH, (hh + 1) * NA_DH)

            def products(r, carry, hh=hh):
                rs, q0, k0 = where(r)
                s = lax.dot_general(qs[hh, pl.ds(q0, GRID_W), :], ks[hh, pl.ds(k0, WIN), :], (((1,), (1,)), ((), ())),
                                    preferred_element_type=F32) * SC
                s_all[pl.ds(q0, GRID_W), :] = s + bias_ref[hh, r - rs]
                dp_all[pl.ds(q0, GRID_W), :] = lax.dot_general(dos[hh, pl.ds(q0, GRID_W), :], vs[hh, pl.ds(k0, WIN), :],
                                                               (((1,), (1,)), ((), ())), preferred_element_type=F32)
                return carry
            lax.fori_loop(0, rows, products, 0, unroll=4)

            def soft(r, carry, hh=hh):
                rs, q0, _ = where(r)
                s = s_all[pl.ds(q0, GRID_W), :]
                p = jnp.exp(s - jnp.max(s, axis=-1, keepdims=True))
                p = p * (1.0 / jnp.sum(p, axis=-1, keepdims=True))
                dp = dp_all[pl.ds(q0, GRID_W), :]
                ds = p * (dp - jnp.sum(p * dp, axis=-1, keepdims=True))
                dbias_ref[hh, r - rs] += ds
                p_all[pl.ds(q0, GRID_W), :] = p.astype(BF16)
                ds_all[pl.ds(q0, GRID_W), :] = ds.astype(BF16)
                return carry
            lax.fori_loop(0, rows, soft, 0, unroll=2)

            def grads(r, carry, hh=hh, ls=ls):
                rs, q0, k0 = where(r)
                par = rs % 2
                t0 = (rs + par) // 2
                qr = qs[hh, pl.ds(q0, GRID_W), :]
                dor = dos[hh, pl.ds(q0, GRID_W), :]
                dsb = ds_all[pl.ds(q0, GRID_W), :]
                dqn[pl.ds(q0, GRID_W), ls] = jnp.dot(dsb, ks[hh, pl.ds(k0, WIN), :], preferred_element_type=F32) * SC
                dkt = lax.dot_general(qr, dsb, (((0,), (0,)), ((), ())), preferred_element_type=F32) * SC
                dvt = lax.dot_general(dor, p_all[pl.ds(q0, GRID_W), :], (((0,), (0,)), ((), ())), preferred_element_type=F32)
                akt[hh, par, pl.ds(t0, WIN // 128)] += jnp.stack([dkt[:, 128 * i:128 * (i + 1)] for i in range(WIN // 128)])
                avt[hh, par, pl.ds(t0, WIN // 128)] += jnp.stack([dvt[:, 128 * i:128 * (i + 1)] for i in range(WIN // 128)])
                return carry
            lax.fori_loop(0, rows, grads, 0, unroll=4)

        for hh in range(2):
            ls = slice(hh * NA_DH, (hh + 1) * NA_DH)
            for i in range(S // 128):
                for acc, dst in ((akt, dkn), (avt, dvs)):
                    odd = jnp.concatenate([acc[hh, 1, i][:, NA_DH:], acc[hh, 1, i + 1][:, :NA_DH]], axis=1)
                    dst[128 * i:128 * (i + 1), ls] = (acc[hh, 0, i] + odd).T

        for c in range(S // _RC):
            sl = slice(c * _RC, (c + 1) * _RC)
            for x_ref, g_ref, dn, dx_ref, dg_ref in ((q_ref, qg_ref, dqn, dq_ref, dqg_ref), (k_ref, kg_ref, dkn, dk_ref, dkg_ref)):
                x = x_ref[sl, :]
                r_ = lax.rsqrt(jnp.dot(x * x, seg_ref[...], precision=HI, preferred_element_type=F32) + EPS)
                xh = x * r_
                d = dn[sl, :]
                dxh = d * g_ref[...]
                mean = jnp.dot(dxh * xh, seg_ref[...], precision=HI, preferred_element_type=F32)
                dx_ref[sl, :] = (r_ * (dxh - xh * mean)).astype(BF16)
                dg_ref[...] += jnp.sum(d * xh, axis=0, keepdims=True)
            dv_ref[sl, :] = dvs[sl, :].astype(BF16)

    LW = 128
    blk = pl.BlockSpec((S, LW), lambda j, e: (e, j))
    vec = pl.BlockSpec((1, LW), lambda j, e: (0, j))
    bsp = pl.BlockSpec((2, NA_ROWS, GRID_W, WIN), lambda j, e: (j, 0, 0, 0))
    return pl.pallas_call(
        body, out_shape=(jax.ShapeDtypeStruct((T, GROUP_W), BF16),) * 3 + (
            jax.ShapeDtypeStruct((NA_HEADS, NA_ROWS, GRID_W, WIN), F32),
            jax.ShapeDtypeStruct((1, GROUP_W), F32), jax.ShapeDtypeStruct((1, GROUP_W), F32)),
        grid=(GROUP_W // LW, E),
        in_specs=[pl.BlockSpec((S, LW), lambda j, e: (e, B_Q // LW + j)),
                  pl.BlockSpec((S, LW), lambda j, e: (e, B_K // LW + j)),
                  pl.BlockSpec((S, LW), lambda j, e: (e, B_V // LW + j)),
                  vec, vec, bsp, pl.BlockSpec((LW, LW), lambda j, e: (0, 0)), blk],
        out_specs=(blk, blk, blk, bsp, vec, vec),
        scratch_shapes=[pltpu.VMEM((2, S, NA_DH), BF16)] * 4 + [pltpu.VMEM((S, LW), F32)] * 3
        + [pltpu.VMEM((2, 2, S // 128 + 1, NA_DH, 128), F32)] * 2
        + [pltpu.VMEM((S, WIN), F32)] * 2 + [pltpu.VMEM((S, WIN), BF16)] * 2,
        name=name, compiler_params=_cp(("parallel", "arbitrary")))(z, z, z, qg, kg, bias, seg, do)


def _na_rpb_grad(dbias, name):
    e_np, a_np, _ = _na_onehots()
    H = NA_HEADS
    x = dbias.reshape(H, NA_ROWS, GRID_W, NA_ROWS, GRID_W)
    x = jnp.transpose(x, (0, 1, 3, 2, 4)).reshape(H, NA_ROWS * NA_ROWS, GRID_W * GRID_W)

    def body(x_ref, e_ref, a_ref, o_ref):
        y = jnp.dot(x_ref[0], e_ref[...], precision=HI, preferred_element_type=F32)
        o_ref[0] = jnp.dot(a_ref[...], y, precision=HI, preferred_element_type=F32)

    out = pl.pallas_call(
        body, out_shape=jax.ShapeDtypeStruct((H, 16, 128), F32), grid=(H,),
        in_specs=[pl.BlockSpec((1, 64, GRID_W * GRID_W), lambda h: (h, 0, 0)),
                  pl.BlockSpec((GRID_W * GRID_W, 128), lambda h: (0, 0)),
                  pl.BlockSpec((16, 64), lambda h: (0, 0))],
        out_specs=pl.BlockSpec((1, 16, 128), lambda h: (h, 0, 0)),
        name=name, compiler_params=_cp(("parallel",)))(x, jnp.asarray(e_np), jnp.asarray(a_np))
    return out[:, :2 * NA_ROWS - 1, :2 * NA_COLS - 1]


_HK = GLA_HEADS * GLA_DK
_HV = GLA_HEADS * GLA_DV


def _gla_consts(reverse):
    i = np.arange(CHUNK)
    tri = (i[:, None] <= i[None, :]) if reverse else (i[:, None] >= i[None, :])
    j = np.arange(_HK)
    oseg = (j[:, None] // GLA_DK == j[None, :] // GLA_DK)
    return (jnp.asarray(tri.astype(np.float32)), jnp.asarray(tri.T.astype(np.float32)), jnp.asarray(oseg.astype(np.float32), BF16))


def _log_decay(lr, a2, ab):
    zg = jnp.dot(lr, a2, precision=HI, preferred_element_type=F32) + ab
    g = (jnp.minimum(zg, 0.0) - jnp.log(1.0 + jnp.exp(-jnp.abs(zg)))) * (1.0 / GLA_TAU)
    return zg, g


def _dotf(a, b, dn):
    return lax.dot_general(a, b, dn, precision=HI, preferred_element_type=F32)


def _dotb(a, b, dn):
    return lax.dot_general(a.astype(BF16), b.astype(BF16), dn, preferred_element_type=F32)


_COLS = 4
_SUB = 16
_NSUB = CHUNK // _SUB


def _gla_cross_blocks(reverse):
    return range(0, _NSUB - 1) if reverse else range(1, _NSUB)


def _gla_cross_terms(s, reverse, b_s, q_s, k_s, oseg_ref):
    r0 = s * _SUB
    ref = r0 + (_SUB - 1 if reverse else 0)
    bref = b_s[ref:ref + 1, :]
    rowj = lax.broadcasted_iota(jnp.int32, (CHUNK, 1), 0)
    seen = (rowj >= r0 + _SUB) if reverse else (rowj < r0)
    ek = jnp.where(seen, jnp.exp(jnp.minimum(bref - b_s[...], 0.0)), 0.0)
    kt = k_s[...] * ek
    eq = jnp.exp(jnp.minimum(b_s[r0:r0 + _SUB, :] - bref, 0.0))
    qt = q_s[r0:r0 + _SUB, :] * eq
    nmat = jnp.concatenate([kt.astype(BF16)] * GLA_HEADS, axis=0) * oseg_ref[...]
    return qt, eq, kt, ek, nmat


_NN = (((1,), (0,)), ((), ()))
_NT = (((1,), (1,)), ((), ()))
_TN = (((0,), (0,)), ((), ()))


_DT = 256


def _gla_block_tri(reverse):
    i = np.arange(_DT)
    same = i[:, None] // CHUNK == i[None, :] // CHUNK
    tri = (i[:, None] <= i[None, :]) if reverse else (i[:, None] >= i[None, :])
    return (tri & same).astype(np.float32)


def _gla_decay_fwd(z, a2, ab, reverse, name):
    T = z.shape[0]
    nc = _DT // CHUNK

    def body(lr_ref, a2_ref, ab_ref, m_ref, b_ref, ec_ref):
        _, g = _log_decay(lr_ref[...], a2_ref[...], ab_ref[...])
        b_ref[...] = _dotf(m_ref[...], g, _NN)
        for c in range(nc):
            ec_ref[c] = jnp.exp(_dotf(g[c * CHUNK:(c + 1) * CHUNK, :], jnp.ones((CHUNK, GLA_DV), F32), _TN))

    return pl.pallas_call(
        body, out_shape=(jax.ShapeDtypeStruct((T, _HK), F32), jax.ShapeDtypeStruct((T // CHUNK, _HK, GLA_DV), F32)),
        grid=(T // _DT,),
        in_specs=[pl.BlockSpec((_DT, 128), lambda i: (i, LR_OFF // 128)),
                  pl.BlockSpec((128, _HK), lambda i: (0, 0)),
                  pl.BlockSpec((1, _HK), lambda i: (0, 0)),
                  pl.BlockSpec((_DT, _DT), lambda i: (0, 0))],
        out_specs=(pl.BlockSpec((_DT, _HK), lambda i: (i, 0)), pl.BlockSpec((nc, _HK, GLA_DV), lambda i: (i, 0, 0))),
        name=name, compiler_params=_cp(("parallel",)))(z, a2, ab, jnp.asarray(_gla_block_tri(reverse)))


def _gla_decay_bwd(z, a2_f, ab_f, a2_b, ab_b, db_f, db_b, name):
    T = z.shape[0]

    def body(lr_ref, a2f_ref, abf_ref, a2b_ref, abb_ref, mf_ref, mb_ref, dbf_ref, dbb_ref,
             dlr_ref, da2f_ref, dabf_ref, da2b_ref, dabb_ref):
        @pl.when(pl.program_id(0) == 0)
        def _():
            for r in (da2f_ref, dabf_ref, da2b_ref, dabb_ref):
                r[...] = jnp.zeros_like(r)

        lr = lr_ref[...]
        dlr = jnp.zeros((_DT, 128), F32)
        for a2_ref, ab_ref, mt_ref, db_ref, da2_ref, dab_ref in ((a2f_ref, abf_ref, mf_ref, dbf_ref, da2f_ref, dabf_ref),
                                                                 (a2b_ref, abb_ref, mb_ref, dbb_ref, da2b_ref, dabb_ref)):
            zg, _ = _log_decay(lr, a2_ref[...], ab_ref[...])
            dg = _dotf(mt_ref[...], db_ref[...], _NN)
            dzg = dg * (1.0 / (1.0 + jnp.exp(zg))) * (1.0 / GLA_TAU)
            dlr = dlr + _dotf(dzg, a2_ref[...], _NT)
            da2_ref[...] += _dotf(lr, dzg, _TN)
            dab_ref[...] += jnp.sum(dzg, axis=0, keepdims=True)
        dlr_ref[...] = dlr.astype(BF16)

    a2s = pl.BlockSpec((128, _HK), lambda i: (0, 0))
    abs_ = pl.BlockSpec((1, _HK), lambda i: (0, 0))
    ms = pl.BlockSpec((_DT, _DT), lambda i: (0, 0))
    row = pl.BlockSpec((_DT, _HK), lambda i: (i, 0))
    return pl.pallas_call(
        body, out_shape=(jax.ShapeDtypeStruct((T, 128), BF16), jax.ShapeDtypeStruct((128, _HK), F32), jax.ShapeDtypeStruct((1, _HK), F32),
                         jax.ShapeDtypeStruct((128, _HK), F32), jax.ShapeDtypeStruct((1, _HK), F32)),
        grid=(T // _DT,),
        in_specs=[pl.BlockSpec((_DT, 128), lambda i: (i, LR_OFF // 128)), a2s, abs_, a2s, abs_, ms, ms, row, row],
        out_specs=(pl.BlockSpec((_DT, 128), lambda i: (i, 0)), a2s, abs_, a2s, abs_),
        name=name, compiler_params=_cp(("arbitrary",)))(
            z, a2_f, ab_f, a2_b, ab_b, jnp.asarray(_gla_block_tri(False).T), jnp.asarray(_gla_block_tri(True).T), db_f, db_b)


def _gla_fwd(z, b_all, ecol, S, reverse, name):
    T = z.shape[0]
    E = T // S
    n = S // CHUNK
    _, _, oseg = _gla_consts(reverse)
    last = 0 if reverse else CHUNK - 1

    def body(q_ref, k_ref, v_ref, b_ref, ec_ref, oseg_ref, o_ref, a_ref, st_ref, st, b_s, q_s, k_s):
        @pl.when(pl.program_id(1) == 0)
        def _():
            st[...] = jnp.zeros_like(st)

        q = q_ref[...] * (GLA_DK ** -0.5)
        k = k_ref[...]
        v = v_ref[...]
        b = b_ref[...]
        bl_row = b_ref[last:last + 1, :]
        e_col = ec_ref[0]
        b_s[...] = b
        q_s[...] = q
        k_s[...] = k
        lane = lax.broadcasted_iota(jnp.int32, (1, _HK), 1) % GLA_DK

        rowi = lax.broadcasted_iota(jnp.int32, (CHUNK, 1), 0)
        blk0 = (rowi // _SUB) * _SUB

        def cols(jj, a):
            ts = []
            for u in range(_COLS):
                jp = jj * _COLS + u
                tiles = []
                for s in range(_NSUB):
                    rs_ = slice(s * _SUB, (s + 1) * _SUB)
                    bj = b_s[pl.ds(s * _SUB + jp, 1), :]
                    kj = k_s[pl.ds(s * _SUB + jp, 1), :]
                    tiles.append(q_s[rs_, :] * jnp.exp(jnp.minimum(b_s[rs_, :] - bj, 0.0)) * kj)
                ts.append(jnp.concatenate(tiles, axis=0).astype(BF16))
            r = jnp.dot(jnp.concatenate(ts, axis=0), oseg_ref[...], preferred_element_type=F32)
            for u in range(_COLS):
                a = jnp.where(lane == blk0 + (jj * _COLS + u), r[u * CHUNK:(u + 1) * CHUNK, :], a)
            return a

        a = lax.fori_loop(0, _SUB // _COLS, cols, jnp.zeros((CHUNK, _HK), F32))
        keep = (rowi <= lane) if reverse else (rowi >= lane)
        a = jnp.where(keep, a, 0.0)
        cross = []
        for s in range(_NSUB):
            if s in _gla_cross_blocks(reverse):
                qt, _, _, _, nmat = _gla_cross_terms(s, reverse, b_s, q_s, k_s, oseg_ref)
                cross.append(lax.dot_general(qt.astype(BF16), nmat, _NT, preferred_element_type=F32))
            else:
                cross.append(jnp.zeros((_SUB, _HK), F32))
        a = a + jnp.concatenate(cross, axis=0)
        a_ref[...] = a
        st_ref[0] = st[...]
        qb = q * jnp.exp(b)
        kd = k * jnp.exp(bl_row - b)
        for h in range(GLA_HEADS):
            ks_ = slice(h * GLA_DK, (h + 1) * GLA_DK)
            vs_ = slice(h * GLA_DV, (h + 1) * GLA_DV)
            s_h = st[ks_, :]
            o_ref[:, vs_] = _dotb(qb[:, ks_], s_h, _NN) + _dotb(a[:, ks_], v[:, vs_], _NN)
            st[ks_, :] = s_h * e_col[ks_, :] + _dotb(kd[:, ks_], v[:, vs_], _TN)

    def rowblk(e, c):
        return e * n + ((n - 1 - c) if reverse else c)

    return pl.pallas_call(
        body, out_shape=(jax.ShapeDtypeStruct((T, _HV), F32), jax.ShapeDtypeStruct((T, _HK), F32),
                         jax.ShapeDtypeStruct((T // CHUNK, _HK, GLA_DV), F32)),
        grid=(E, n),
        in_specs=[pl.BlockSpec((CHUNK, _HK), lambda e, c: (rowblk(e, c), C_Q // _HK)),
                  pl.BlockSpec((CHUNK, _HK), lambda e, c: (rowblk(e, c), C_K // _HK)),
                  pl.BlockSpec((CHUNK, _HV), lambda e, c: (rowblk(e, c), C_V // _HV)),
                  pl.BlockSpec((CHUNK, _HK), lambda e, c: (rowblk(e, c), 0)),
                  pl.BlockSpec((1, _HK, GLA_DV), lambda e, c: (rowblk(e, c), 0, 0)),
                  pl.BlockSpec((_HK, _HK), lambda e, c: (0, 0))],
        out_specs=(pl.BlockSpec((CHUNK, _HV), lambda e, c: (rowblk(e, c), 0)),
                   pl.BlockSpec((CHUNK, _HK), lambda e, c: (rowblk(e, c), 0)),
                   pl.BlockSpec((1, _HK, GLA_DV), lambda e, c: (rowblk(e, c), 0, 0))),
        scratch_shapes=[pltpu.VMEM((_HK, GLA_DV), F32)] + [pltpu.VMEM((CHUNK, _HK), F32)] * 3,
        name=name, compiler_params=_cp(("parallel", "arbitrary")))(z, z, z, b_all, ecol, oseg)


def _gla_bwd(z, b_all, ecol, att, states, do, prev, S, reverse, name):
    T = z.shape[0]
    E = T // S
    n = S // CHUNK
    _, _, oseg = _gla_consts(reverse)
    has_prev = prev is not None
    odt = BF16 if has_prev else F32
    last = 0 if reverse else CHUNK - 1

    def body(*refs):
        (q_ref, k_ref, v_ref, b_ref, ec_ref, oseg_ref, att_ref, st_ref, do_ref) = refs[:9]
        refs = refs[9:]
        if has_prev:
            pq_ref, pk_ref, pv_ref = refs[:3]
            refs = refs[3:]
        (dq_ref, dk_ref, dv_ref, db_ref, dst, b_s, q_s, k_s, da_s, dqb_s, dkd_s, dk3_s, dbn_s, dsp_s) = refs

        @pl.when(pl.program_id(1) == 0)
        def _():
            dst[...] = jnp.zeros_like(dst)

        q = q_ref[...] * (GLA_DK ** -0.5)
        k = k_ref[...]
        v = v_ref[...]
        b = b_ref[...]
        bl_row = b_ref[last:last + 1, :]
        eb = jnp.exp(b)
        ekd = jnp.exp(bl_row - b)
        qb = q * eb
        kd = k * ekd
        b_s[...] = b
        q_s[...] = q
        k_s[...] = k
        att = att_ref[...]
        s_all = st_ref[0]
        dsn = dst[...]
        e_col = ec_ref[0]
        do = do_ref[...]
        lane = lax.broadcasted_iota(jnp.int32, (1, _HK), 1) % GLA_DK
        rowi = lax.broadcasted_iota(jnp.int32, (CHUNK, 1), 0)
        keep = (rowi <= lane) if reverse else (rowi >= lane)
        for h in range(GLA_HEADS):
            ks_ = slice(h * GLA_DK, (h + 1) * GLA_DK)
            vs_ = slice(h * GLA_DV, (h + 1) * GLA_DV)
            do_h = do[:, vs_]
            s_h = s_all[ks_, :]
            dsn_h = dsn[ks_, :]
            dqb_s[:, ks_] = _dotb(do_h, s_h, _NT)
            dsp_s[ks_, :] = _dotb(qb[:, ks_], do_h, _TN) + dsn_h * e_col[ks_, :]
            da_s[:, ks_] = _dotb(do_h, v[:, vs_], _NT)
            dv_h = _dotb(att[:, ks_], do_h, _TN) + _dotb(kd[:, ks_], dsn_h, _NN)
            if has_prev:
                dv_h = dv_h + pv_ref[:, vs_]
            dv_ref[:, vs_] = dv_h.astype(odt)
            dkd_s[:, ks_] = _dotb(v[:, vs_], dsn_h, _NT)
        da_s[...] = jnp.where(keep, da_s[...], 0.0)
        dqb = dqb_s[...]
        dkd = dkd_s[...]
        x = dsn * s_all * e_col
        dbl_row = _dotf(jnp.ones((8, GLA_DV), F32), x, _NT)[0:1, :] + jnp.sum(dkd * kd, axis=0, keepdims=True)

        blk0 = (rowi // _SUB) * _SUB

        def cols(jj, carry):
            dq3, db3 = list(carry[:_NSUB]), list(carry[_NSUB:])
            sel = [jnp.where(lane == blk0 + (jj * _COLS + u), da_s[...], 0.0).astype(BF16) for u in range(_COLS)]
            dcols = jnp.dot(jnp.concatenate(sel, axis=0), oseg_ref[...], preferred_element_type=F32)
            for u in range(_COLS):
                jp = jj * _COLS + u
                for s in range(_NSUB):
                    rs_ = slice(s * _SUB, (s + 1) * _SUB)
                    bj = b_s[pl.ds(s * _SUB + jp, 1), :]
                    kj = k_s[pl.ds(s * _SUB + jp, 1), :]
                    tm_ = dcols[u * CHUNK + s * _SUB:u * CHUNK + (s + 1) * _SUB, :] * jnp.exp(jnp.minimum(b_s[rs_, :] - bj, 0.0))
                    dq3[s] = dq3[s] + tm_ * kj
                    gq = tm_ * q_s[rs_, :]
                    dk3_s[pl.ds(s * _SUB + jp, 1), :] = jnp.sum(gq, axis=0, keepdims=True)
                    w = gq * kj
                    dbn_s[pl.ds(s * _SUB + jp, 1), :] = jnp.sum(w, axis=0, keepdims=True)
                    db3[s] = db3[s] + w
            return tuple(dq3) + tuple(db3)

        zero = jnp.zeros((_SUB, _HK), F32)
        acc = lax.fori_loop(0, _SUB // _COLS, cols, (zero,) * (2 * _NSUB))
        dq3 = jnp.concatenate(acc[:_NSUB], axis=0)
        db3 = jnp.concatenate(acc[_NSUB:], axis=0)
        head = lax.broadcasted_iota(jnp.int32, (1, _HK), 1) // GLA_DK
        dq_x, db_x = [], []
        dk_x = jnp.zeros((CHUNK, _HK), F32)
        db_k = jnp.zeros((CHUNK, _HK), F32)
        for s in range(_NSUB):
            if s not in _gla_cross_blocks(reverse):
                dq_x.append(zero)
                db_x.append(zero)
                continue
            r0 = s * _SUB
            qt, eq, kt, ek, nmat = _gla_cross_terms(s, reverse, b_s, q_s, k_s, oseg_ref)
            seen = (lane >= r0 + _SUB) if reverse else (lane < r0)
            dax = jnp.where(seen, da_s[r0:r0 + _SUB, :], 0.0).astype(BF16)
            dqt = jnp.dot(dax, nmat, preferred_element_type=F32)
            full = lax.dot_general(dax, qt.astype(BF16), _TN, preferred_element_type=F32)
            dkt = full[0:CHUNK, :]
            for h in range(1, GLA_HEADS):
                dkt = jnp.where(head == h, full[h * CHUNK:(h + 1) * CHUNK, :], dkt)
            dq_x.append(dqt * eq)
            db_x.append(dqt * qt)
            dk_x = dk_x + dkt * ek
            db_k = db_k + dkt * kt
        dq = (dqb * eb + dq3 + jnp.concatenate(dq_x, axis=0)) * (GLA_DK ** -0.5)
        dk = dkd * ekd + dk3_s[...] + dk_x
        db = dqb * qb - dkd * kd + db3 - dbn_s[...] + jnp.concatenate(db_x, axis=0) - db_k
        db_ref[...] = jnp.where(rowi == last, db + dbl_row, db)
        if has_prev:
            dq = dq + pq_ref[...]
            dk = dk + pk_ref[...]
        dq_ref[...] = dq.astype(odt)
        dk_ref[...] = dk.astype(odt)
        dst[...] = dsp_s[...]

    def rowblk(e, c):
        return e * n + (c if reverse else (n - 1 - c))

    hk = pl.BlockSpec((CHUNK, _HK), lambda e, c: (rowblk(e, c), 0))
    hv = pl.BlockSpec((CHUNK, _HV), lambda e, c: (rowblk(e, c), 0))
    stb = pl.BlockSpec((1, _HK, GLA_DV), lambda e, c: (rowblk(e, c), 0, 0))
    in_specs = [pl.BlockSpec((CHUNK, _HK), lambda e, c: (rowblk(e, c), C_Q // _HK)),
                pl.BlockSpec((CHUNK, _HK), lambda e, c: (rowblk(e, c), C_K // _HK)),
                pl.BlockSpec((CHUNK, _HV), lambda e, c: (rowblk(e, c), C_V // _HV)),
                hk, stb, pl.BlockSpec((_HK, _HK), lambda e, c: (0, 0)), hk, stb, hv]
    args = [z, z, z, b_all, ecol, oseg, att, states, do]
    if has_prev:
        in_specs += [hk, hk, hv]
        args += list(prev)
    return pl.pallas_call(
        body, out_shape=(jax.ShapeDtypeStruct((T, _HK), odt), jax.ShapeDtypeStruct((T, _HK), odt),
                         jax.ShapeDtypeStruct((T, _HV), odt), jax.ShapeDtypeStruct((T, _HK), F32)),
        grid=(E, n), in_specs=in_specs, out_specs=(hk, hk, hv, hk),
        scratch_shapes=[pltpu.VMEM((_HK, GLA_DV), F32)] + [pltpu.VMEM((CHUNK, _HK), F32)] * 8 + [pltpu.VMEM((_HK, GLA_DV), F32)],
        name=name, compiler_params=_cp(("parallel", "arbitrary")))(*args)


def _gla_norm_fwd(of, ob, og, name):
    T = of.shape[0]
    tm = 256

    def body(f_ref, b_ref, g_ref, o_ref):
        for h in range(GLA_HEADS):
            vs_ = slice(h * GLA_DV, (h + 1) * GLA_DV)
            o = f_ref[:, vs_] + b_ref[:, vs_]
            o_ref[:, vs_] = o * lax.rsqrt(jnp.mean(o * o, axis=-1, keepdims=True) + EPS) * g_ref[:, vs_]

    row = pl.BlockSpec((tm, _HV), lambda i: (i, 0))
    vec = pl.BlockSpec((1, _HV), lambda i: (0, 0))
    return pl.pallas_call(body, out_shape=jax.ShapeDtypeStruct((T, _HV), F32), grid=(T // tm,),
                          in_specs=[row, row, vec], out_specs=row, name=name, compiler_params=_cp(("parallel",)))(of, ob, og)


def _gla_norm_bwd(of, ob, og, dpre, name):
    T = of.shape[0]
    tm = 256

    def body(f_ref, b_ref, g_ref, dp_ref, do_ref, dg_ref):
        @pl.when(pl.program_id(0) == 0)
        def _():
            dg_ref[...] = jnp.zeros_like(dg_ref)

        for h in range(GLA_HEADS):
            vs_ = slice(h * GLA_DV, (h + 1) * GLA_DV)
            o = f_ref[:, vs_] + b_ref[:, vs_]
            r = lax.rsqrt(jnp.mean(o * o, axis=-1, keepdims=True) + EPS)
            xh = o * r
            dp = dp_ref[:, vs_]
            dxh = dp * g_ref[:, vs_]
            do_ref[:, vs_] = r * (dxh - xh * jnp.mean(dxh * xh, axis=-1, keepdims=True))
            dg_ref[:, vs_] += jnp.sum(dp * xh, axis=0, keepdims=True)

    row = pl.BlockSpec((tm, _HV), lambda i: (i, 0))
    vec = pl.BlockSpec((1, _HV), lambda i: (0, 0))
    return pl.pallas_call(
        body, out_shape=(jax.ShapeDtypeStruct((T, _HV), F32), jax.ShapeDtypeStruct((1, _HV), F32)), grid=(T // tm,),
        in_specs=[row, row, vec, row], out_specs=(row, vec), name=name, compiler_params=_cp(("arbitrary",)))(of, ob, og, dpre)


_ANY = pl.BlockSpec(memory_space=pl.ANY)


def _coords():
    return lax.axis_index("x"), lax.axis_index("y"), lax.axis_index("c")


def _other_chips(x, y):
    return ((1 - x, y), (x, 1 - y), (1 - x, 1 - y))


def _gather_weights(arrays, chunks, name):
    n = len(arrays)
    pieces = []
    for k in range(max(chunks)):
        for i, a in enumerate(arrays):
            if k < chunks[i]:
                rc = a.shape[1] // chunks[i]
                pieces.append((i, k * rc, rc))
    m = len(pieces)

    def body(*refs):
        srcs, dsts = refs[:n], refs[n:2 * n]
        send_sems, recv_sems, local_sems = refs[2 * n:]
        x, y, c = _coords()
        me = 2 * x + y
        loc = [pltpu.make_async_copy(s, d.at[me], local_sems.at[i]) for i, (s, d) in enumerate(zip(srcs, dsts))]
        for cp in loc:
            cp.start()
        ici = []
        for p, (i, r0, rc) in enumerate(pieces):
            for j, (px, py) in enumerate(_other_chips(x, y)):
                ici.append(pltpu.make_async_remote_copy(
                    src_ref=srcs[i].at[c, pl.ds(r0, rc)], dst_ref=dsts[i].at[me, c, pl.ds(r0, rc)],
                    send_sem=send_sems.at[3 * p + j], recv_sem=recv_sems.at[3 * p + j],
                    device_id=(px, py, c), device_id_type=MESH))
        for cp in ici:
            cp.start()
        fwd = []
        for p, (i, r0, rc) in enumerate(pieces):
            for j, (px, py) in enumerate(_other_chips(x, y)):
                ici[3 * p + j].wait_recv()
                part = dsts[i].at[2 * px + py, c, pl.ds(r0, rc)]
                cp = pltpu.make_async_remote_copy(
                    src_ref=part, dst_ref=part, send_sem=send_sems.at[3 * m + 3 * p + j], recv_sem=recv_sems.at[3 * m + 3 * p + j],
                    device_id=(x, y, 1 - c), device_id_type=MESH)
                cp.start()
                fwd.append(cp)
        for cp in fwd:
            cp.wait_recv()
        for cp in ici + fwd:
            cp.wait_send()
        for cp in loc:
            cp.wait()

    return pl.pallas_call(
        body, out_shape=tuple(jax.ShapeDtypeStruct((4,) + a.shape, a.dtype) for a in arrays),
        in_specs=[_ANY] * n, out_specs=(_ANY,) * n,
        scratch_shapes=[pltpu.SemaphoreType.DMA((6 * m,)), pltpu.SemaphoreType.DMA((6 * m,)), pltpu.SemaphoreType.DMA((n,))],
        name=name)(*arrays)


def _sibling_exchange(layered, whole, name):
    nl, n = len(layered), len(layered) + len(whole)

    def body(*refs):
        srcs, dsts = refs[:n], refs[n:2 * n]
        send_sems, recv_sems = refs[2 * n:]
        x, y, c = _coords()
        rem = [pltpu.make_async_remote_copy(src_ref=(s.at[1 - c] if i < nl else s), dst_ref=d, send_sem=send_sems.at[i],
                                            recv_sem=recv_sems.at[i], device_id=(x, y, 1 - c), device_id_type=MESH)
               for i, (s, d) in enumerate(zip(srcs, dsts))]
        for cp in rem:
            cp.start()
        for cp in rem:
            cp.wait()

    outs = [jax.ShapeDtypeStruct(a.shape[1:], a.dtype) for a in layered] + [jax.ShapeDtypeStruct(a.shape, a.dtype) for a in whole]
    return pl.pallas_call(
        body, out_shape=tuple(outs), in_specs=[_ANY] * n, out_specs=(_ANY,) * n,
        scratch_shapes=[pltpu.SemaphoreType.DMA((n,)), pltpu.SemaphoreType.DMA((n,))], name=name)(*layered, *whole)


def _chip_exchange(scatter, bcast, name):
    ns, n = len(scatter), len(scatter) + len(bcast)

    def body(*refs):
        srcs, dsts = refs[:n], refs[n:2 * n]
        send_sems, recv_sems, local_sems = refs[2 * n:]
        x, y, c = _coords()
        me = 2 * x + y
        loc = [pltpu.make_async_copy((s.at[me] if i < ns else s), d.at[me], local_sems.at[i])
               for i, (s, d) in enumerate(zip(srcs, dsts))]
        for cp in loc:
            cp.start()
        rem = []
        for j, (px, py) in enumerate(_other_chips(x, y)):
            for i, (s, d) in enumerate(zip(srcs, dsts)):
                rem.append(pltpu.make_async_remote_copy(
                    src_ref=(s.at[2 * px + py] if i < ns else s), dst_ref=d.at[me], send_sem=send_sems.at[n * j + i],
                    recv_sem=recv_sems.at[n * j + i], device_id=(px, py, c), device_id_type=MESH))
        for cp in rem:
            cp.start()
        for cp in rem:
            cp.wait()
        for cp in loc:
            cp.wait()

    outs = [jax.ShapeDtypeStruct(a.shape, a.dtype) for a in scatter] + [jax.ShapeDtypeStruct((4,) + a.shape, a.dtype) for a in bcast]
    return pl.pallas_call(
        body, out_shape=tuple(outs), in_specs=[_ANY] * n, out_specs=(_ANY,) * n,
        scratch_shapes=[pltpu.SemaphoreType.DMA((3 * n,)), pltpu.SemaphoreType.DMA((3 * n,)), pltpu.SemaphoreType.DMA((n,))],
        name=name)(*scatter, *bcast)


def _sum_slots(r, name):
    n, R, C = r.shape
    tm = min(256, R)

    def body(r_ref, o_ref):
        acc = r_ref[0].astype(F32)
        for i in range(1, n):
            acc = acc + r_ref[i].astype(F32)
        o_ref[...] = acc

    return pl.pallas_call(body, out_shape=jax.ShapeDtypeStruct((R, C), F32), grid=(R // tm,),
                          in_specs=[pl.BlockSpec((n, tm, C), lambda i: (0, i, 0))], out_specs=pl.BlockSpec((tm, C), lambda i: (i, 0)),
                          name=name, compiler_params=_cp(("parallel",)))(r)


def _add2(a, b, out_dtype, tm, name):
    R, C = a.shape
    tm = min(tm, R)

    def body(a_ref, b_ref, o_ref):
        o_ref[...] = (a_ref[...].astype(F32) + b_ref[...].astype(F32)).astype(out_dtype)

    blk = pl.BlockSpec((tm, C), lambda i: (i, 0))
    return pl.pallas_call(body, out_shape=jax.ShapeDtypeStruct((R, C), out_dtype), grid=(R // tm,), in_specs=[blk, blk],
                          out_specs=blk, name=name, compiler_params=_cp(("parallel",)))(a, b)


def _adamw_math(w, g, m, v):
    m = ADAM_B1 * m + (1.0 - ADAM_B1) * g
    v = ADAM_B2 * v + (1.0 - ADAM_B2) * (g * g)
    m_hat = m / (1.0 - ADAM_B1 ** ADAM_STEP)
    v_hat = v / (1.0 - ADAM_B2 ** ADAM_STEP)
    delta = -ADAM_LR * (m_hat / (jnp.sqrt(v_hat) + ADAM_EPS) + ADAM_WD * w)
    return delta, m, v


def _adamw(w, gs, m, v, tm, name):
    R, C = w.shape

    def body(*refs):
        w_ref = refs[0]
        g_refs = refs[1:1 + len(gs)]
        m_ref, v_ref, g_out, d_out, m_out, v_out = refs[1 + len(gs):]
        g = None
        for gr in g_refs:
            parts = [gr[i] for i in range(gr.shape[0])] if len(gr.shape) == 3 else [gr[...]]
            for p in parts:
                g = p if g is None else g + p
        d, mn, vn = _adamw_math(w_ref[...], g, m_ref[...], v_ref[...])
        g_out[...] = g
        d_out[...] = d
        m_out[...] = mn
        v_out[...] = vn

    blk = pl.BlockSpec((tm, C), lambda i: (i, 0))
    g_specs = [pl.BlockSpec((g.shape[0], tm, C), lambda i: (0, i, 0)) if g.ndim == 3 else blk for g in gs]
    return pl.pallas_call(
        body, out_shape=tuple(jax.ShapeDtypeStruct((R, C), F32) for _ in range(4)), grid=(R // tm,),
        in_specs=[blk] + g_specs + [blk, blk], out_specs=(blk,) * 4, name=name,
        compiler_params=_cp(("parallel",)))(w, *gs, m, v)


WEIGHTS = ("norm_g", "w_in", "conv_w", "conv_b", "conv_ln_g", "conv_ln_b", "na_q_g", "na_k_g", "na_rpb", "gla_a2_f",
           "gla_ab_f", "gla_a2_b", "gla_ab_b", "gla_o_g", "pool_w", "pool_scale", "w_out")
_REPL = ("norm_g", "conv_b", "conv_ln_g", "conv_ln_b", "na_q_g", "na_k_g", "na_rpb", "gla_ab_f", "gla_ab_b", "gla_o_g",
         "pool_w", "pool_scale")
_SHARD_SMALL = ("conv_w", "gla_a2_f", "gla_a2_b")
_PACK_ROWS = 8 * 128


def _pack(arrs):
    flat = jnp.concatenate([a.reshape(-1) for a in arrs])
    n = -(-flat.shape[0] // _PACK_ROWS) * _PACK_ROWS
    return jnp.pad(flat, (0, n - flat.shape[0])).reshape(-1, 128)


def _unpack(p, shapes):
    flat = p.reshape(-1)
    out, o = [], 0
    for s in shapes:
        n = int(np.prod(s))
        out.append(flat[o:o + n].reshape(s))
        o += n
    return out


def _to_layout(w):
    pad = jnp.zeros(w.shape[:-1] + (NZ - N_IN,), w.dtype)
    return jnp.concatenate([w[..., :5120], w[..., 5152:6176], w[..., 5120:5152], pad], axis=-1)


def _from_layout(w):
    return jnp.concatenate([w[..., :5120], w[..., LR_OFF:LR_OFF + 32], w[..., 5120:LR_OFF]], axis=-1)


def _reduce_gradients(p_a, p_b, small_g, ci):
    two = lambda a: a.reshape(-1, a.shape[-1])
    s_a, s_b, s_small = _sibling_exchange((p_a, p_b), (small_g,), "grad_to_sibling")
    mine = lambda a: lax.dynamic_index_in_dim(a, ci, 0, keepdims=False)
    c_a = _add2(two(mine(p_a)), two(s_a), BF16, 256, "chip_sum_a").reshape(s_a.shape)
    c_b = _add2(two(mine(p_b)), two(s_b), BF16, 256, "chip_sum_b").reshape(s_b.shape)
    c_small = _add2(small_g, s_small, F32, small_g.shape[0], "chip_sum_small")
    r_a, r_b, r_small = _chip_exchange((c_a, c_b), (c_small,), "grad_to_owner")
    own_a = _sum_slots(r_a, "sum_a")
    own_b = _sum_slots(r_b, "sum_b")
    sib_a, sib_b = _sibling_exchange((), (own_a, own_b), "reduced_to_sibling")
    by_layer = lambda own, sib: jnp.where(ci == 0, jnp.stack([own, sib]), jnp.stack([sib, own]))
    return by_layer(own_a, sib_a), by_layer(own_b, sib_b), r_small


def _layer_fwd(l, x, P, S):
    n = f"l{l}_"
    h = _rmsnorm_fwd(x, P["norm_g"], n + "rms_fwd")
    z = _matmul(h, P["w_in"], dims="nn", out_dtype=F32, tm=512, tn=1280, tk=D_MODEL, name=n + "mm_z")
    yc = _conv_fwd(z, P["conv_w32"], P["conv_b"], S, n + "conv_fwd")
    pre_a = _ln_silu_fwd(yc, P["conv_ln_g"], P["conv_ln_b"], n + "ln_fwd")
    pre_b = _na_fwd(z, P["na_q_g"], P["na_k_g"], P["na_bias"], S, n + "na_fwd")
    bf, ecf = _gla_decay_fwd(z, P["a2_f"], P["gla_ab_f"], False, n + "gla_decay_f")
    bb, ecb = _gla_decay_fwd(z, P["a2_b"], P["gla_ab_b"], True, n + "gla_decay_b")
    of, af, sf = _gla_fwd(z, bf, ecf, S, False, n + "gla_fwd_f")
    ob, ab, sb = _gla_fwd(z, bb, ecb, S, True, n + "gla_fwd_b")
    pre_c = _gla_norm_fwd(of, ob, P["gla_o_g"], n + "gla_norm_fwd")
    pre_d = _pool_fwd(z, P["pool_w_bf"], P["pool_scale"], S, n + "pool_fwd")
    pres = (pre_a, pre_b, pre_c, pre_d)
    y = _gate_fwd(pres, z, n + "gate_fwd")
    out = _matmul(y, P["w_out"], dims="nn", out_dtype=F32, tm=512, tn=1024, tk=D_MODEL, name=n + "mm_out", res=x)
    return out, dict(x=x, h=h, z=z, yc=yc, pres=pres, of=of, af=af, sf=sf, ob=ob, ab=ab, sb=sb, y=y, bf=bf, ecf=ecf, bb=bb, ecb=ecb)


def _layer_bwd(l, dout, dout_bf, sv, P, S):
    n = f"l{l}_"
    z = sv["z"]
    T = z.shape[0]
    dy = _matmul(dout_bf, P["w_out"], dims="nt", out_dtype=F32, tm=512, tn=1024, tk=D_MODEL, name=n + "mm_dy")
    d_w_out = _matmul(sv["y"], dout_bf, dims="tn", out_dtype=BF16, tm=512, tn=1024, tk=1024, name=n + "mm_dwout")
    dpa, dpb, dpc, dpd, dga, dgb, dgc, dgd = _gate_bwd(dy, sv["pres"], z, n + "gate_bwd")
    dyc, d_ln_g, d_ln_b = _ln_silu_bwd(sv["yc"], P["conv_ln_g"], P["conv_ln_b"], dpa, n + "ln_bwd")
    dval, dglu, d_cw, d_cb = _conv_bwd(z, P["conv_w32"], dyc, S, n + "conv_bwd")
    dq, dk, dv, dbias, d_qg, d_kg = _na_bwd(z, P["na_q_g"], P["na_k_g"], P["na_bias"], dpb, S, n + "na_bwd")
    d_rpb = _na_rpb_grad(dbias, n + "na_rpb")
    do, d_og = _gla_norm_bwd(sv["of"], sv["ob"], P["gla_o_g"], dpc, n + "gla_norm_bwd")
    part = _gla_bwd(z, sv["bf"], sv["ecf"], sv["af"], sv["sf"], do, None, S, False, n + "gla_bwd_f")
    dcq, dck, dcv, db_b = _gla_bwd(z, sv["bb"], sv["ecb"], sv["ab"], sv["sb"], do, part[:3], S, True, n + "gla_bwd_b")
    dlr, d_a2f, d_abf, d_a2b, d_abb = _gla_decay_bwd(z, P["a2_f"], P["gla_ab_f"], P["a2_b"], P["gla_ab_b"], part[3], db_b,
                                                     n + "gla_decay_bwd")
    dd, d_pw, d_ps = _pool_bwd(z, P["pool_w_bf"], P["pool_scale"], dpd, S, n + "pool_bwd")
    dz = jnp.concatenate([dval, dglu, dga, dq, dk, dv, dgb, dcq, dck, dcv, dgc, dd, dgd, dlr,
                          jnp.zeros((T, NZ - LR_OFF - 128), BF16)], axis=1)
    dh = _matmul(dz, P["w_in"], dims="nt", out_dtype=F32, tm=512, tn=1024, tk=1280, name=n + "mm_dh")
    d_w_in = _matmul(sv["h"], dz, dims="tn", out_dtype=BF16, tm=512, tn=1280, tk=1024, name=n + "mm_dwin")
    dx, dx_bf, d_ng = _rmsnorm_bwd(sv["x"], P["norm_g"], dh, dout, n + "rms_bwd")
    grads = dict(norm_g=d_ng[0], w_in=d_w_in, conv_w=d_cw[:CONV_K], conv_b=d_cb[0], conv_ln_g=d_ln_g[0], conv_ln_b=d_ln_b[0],
                 na_q_g=d_qg.reshape(NA_HEADS, NA_DH), na_k_g=d_kg.reshape(NA_HEADS, NA_DH), na_rpb=d_rpb,
                 gla_a2_f=d_a2f[0:GLA_RANK], gla_ab_f=d_abf[0], gla_a2_b=d_a2b[GLA_RANK:2 * GLA_RANK], gla_ab_b=d_abb[0],
                 gla_o_g=d_og.reshape(GLA_HEADS, GLA_DV), pool_w=d_pw, pool_scale=d_ps[0], w_out=d_w_out)
    return dx, dx_bf, grads


def kernel(x, norm_g, w_in, conv_w, conv_b, conv_ln_g, conv_ln_b, na_q_g, na_k_g, na_rpb, gla_a2_f, gla_ab_f, gla_a2_b, gla_ab_b, gla_o_g, pool_w, pool_scale, w_out, loss_target, m_norm_g, m_w_in, m_conv_w, m_conv_b, m_conv_ln_g, m_conv_ln_b, m_na_q_g, m_na_k_g, m_na_rpb, m_gla_a2_f, m_gla_ab_f, m_gla_a2_b, m_gla_ab_b, m_gla_o_g, m_pool_w, m_pool_scale, m_w_out, v_norm_g, v_w_in, v_conv_w, v_conv_b, v_conv_ln_g, v_conv_ln_b, v_na_q_g, v_na_k_g, v_na_rpb, v_gla_a2_f, v_gla_ab_f, v_gla_a2_b, v_gla_ab_b, v_gla_o_g, v_pool_w, v_pool_scale, v_w_out):
    W = dict(norm_g=norm_g, w_in=w_in, conv_w=conv_w, conv_b=conv_b, conv_ln_g=conv_ln_g, conv_ln_b=conv_ln_b, na_q_g=na_q_g,
             na_k_g=na_k_g, na_rpb=na_rpb, gla_a2_f=gla_a2_f, gla_ab_f=gla_ab_f, gla_a2_b=gla_a2_b, gla_ab_b=gla_ab_b,
             gla_o_g=gla_o_g, pool_w=pool_w, pool_scale=pool_scale, w_out=w_out)
    M = dict(norm_g=m_norm_g, w_in=m_w_in, conv_w=m_conv_w, conv_b=m_conv_b, conv_ln_g=m_conv_ln_g, conv_ln_b=m_conv_ln_b,
             na_q_g=m_na_q_g, na_k_g=m_na_k_g, na_rpb=m_na_rpb, gla_a2_f=m_gla_a2_f, gla_ab_f=m_gla_ab_f, gla_a2_b=m_gla_a2_b,
             gla_ab_b=m_gla_ab_b, gla_o_g=m_gla_o_g, pool_w=m_pool_w, pool_scale=m_pool_scale, w_out=m_w_out)
    V = dict(norm_g=v_norm_g, w_in=v_w_in, conv_w=v_conv_w, conv_b=v_conv_b, conv_ln_g=v_conv_ln_g, conv_ln_b=v_conv_ln_b,
             na_q_g=v_na_q_g, na_k_g=v_na_k_g, na_rpb=v_na_rpb, gla_a2_f=v_gla_a2_f, gla_ab_f=v_gla_ab_f, gla_a2_b=v_gla_a2_b,
             gla_ab_b=v_gla_ab_b, gla_o_g=v_gla_o_g, pool_w=v_pool_w, pool_scale=v_pool_scale, w_out=v_w_out)
    E, S, D = x.shape
    T = E * S
    L = DEPTH
    xi, yi, ci = _coords()
    chip = 2 * xi + yi
    cw_sh, a2_sh = conv_w.shape[-1], gla_a2_f.shape[-1]

    small_sh = jnp.concatenate([
        jnp.pad(conv_w, ((0, 0), (0, 1), (0, 0))),
        jnp.pad(gla_a2_f, ((0, 0), (0, 0), (0, 128 - a2_sh))),
        jnp.pad(gla_a2_b, ((0, 0), (0, 0), (0, 128 - a2_sh)))], axis=1)
    g_win, g_wout, g_small = _gather_weights((w_in.astype(BF16), w_out.astype(BF16), small_sh), (4, 2, 1), "gather_weights")
    w_in_full = _to_layout(jnp.transpose(g_win, (1, 2, 0, 3)).reshape(L, D, N_IN))
    w_out_full = jnp.transpose(g_wout, (1, 0, 2, 3)).reshape(L, D, D)
    conv_w_full = jnp.transpose(g_small[:, :, 0:32, :], (1, 2, 0, 3)).reshape(L, 32, 4 * cw_sh)
    a2f_full = jnp.transpose(g_small[:, :, 32:48, :a2_sh], (1, 2, 0, 3)).reshape(L, GLA_RANK, 4 * a2_sh)
    a2b_full = jnp.transpose(g_small[:, :, 48:64, :a2_sh], (1, 2, 0, 3)).reshape(L, GLA_RANK, 4 * a2_sh)

    params = []
    for l in range(L):
        params.append(dict(
            norm_g=norm_g[l][None], w_in=w_in_full[l], w_out=w_out_full[l], conv_w32=conv_w_full[l], conv_b=conv_b[l][None],
            conv_ln_g=conv_ln_g[l][None], conv_ln_b=conv_ln_b[l][None], na_q_g=na_q_g[l].reshape(1, GROUP_W),
            na_k_g=na_k_g[l].reshape(1, GROUP_W), na_bias=_na_bias(na_rpb[l], f"l{l}_na_bias"),
            a2_f=jnp.zeros((128, _HK), F32).at[0:GLA_RANK].set(a2f_full[l]),
            a2_b=jnp.zeros((128, _HK), F32).at[GLA_RANK:2 * GLA_RANK].set(a2b_full[l]),
            gla_ab_f=gla_ab_f[l][None], gla_ab_b=gla_ab_b[l][None], gla_o_g=gla_o_g[l].reshape(1, GROUP_W),
            pool_w_bf=pool_w[l].astype(BF16), pool_scale=pool_scale[l][None]))

    act = x.reshape(T, D)
    saved = []
    for l in range(L):
        act, sv = _layer_fwd(l, act, params[l], S)
        saved.append(sv)
    dact, dact_bf, loss_loc = _loss_head(act, loss_target.reshape(T, D), "loss_head")
    loss = lax.psum(loss_loc[0, 0], ("x", "y", "c"))
    grads = [None] * L
    for l in reversed(range(L)):
        dact, dact_bf, grads[l] = _layer_bwd(l, dact, dact_bf, saved[l], params[l], S)
    grad_x = dact.reshape(E, S, D)
    G = {k: jnp.stack([grads[l][k] for l in range(L)]) for k in WEIGHTS}

    cols_in, cols_out = N_IN // 4, D
    p_win = jnp.transpose(_from_layout(G["w_in"]).reshape(L, D, 4, cols_in), (0, 2, 1, 3))
    p_wout = G["w_out"].reshape(L, 4, D // 4, D)
    small_names = _REPL + _SHARD_SMALL
    small_g = _pack([G[k] for k in small_names])
    g_in, g_out, r_small = _reduce_gradients(p_win, p_wout, small_g, ci)

    rows_in, rows_out = L * D, L * (D // 4)
    res = {}
    res["w_in"] = [a.reshape(L, D, cols_in) for a in _adamw(
        w_in.reshape(rows_in, cols_in), (g_in.reshape(rows_in, cols_in),), m_w_in.reshape(rows_in, cols_in),
        v_w_in.reshape(rows_in, cols_in), 256, "adamw_w_in")]
    res["w_out"] = [a.reshape(L, D // 4, D) for a in _adamw(
        w_out.reshape(rows_out, cols_out), (g_out.reshape(rows_out, cols_out),), m_w_out.reshape(rows_out, cols_out),
        v_w_out.reshape(rows_out, cols_out), 256, "adamw_w_out")]
    zeros_sh = [jnp.zeros(G[k].shape, F32) for k in _SHARD_SMALL]
    pk = lambda dct: _pack([dct[k] for k in _REPL] + zeros_sh)
    small_res = _adamw(pk(W), (r_small,), pk(M), pk(V), small_g.shape[0], "adamw_small")
    shapes = [G[k].shape for k in small_names]
    unp = [_unpack(a, shapes) for a in small_res]
    for i, k in enumerate(_REPL):
        res[k] = [u[i] for u in unp]
    g_sh = []
    for i, k in enumerate(_SHARD_SMALL):
        gfull = unp[0][len(_REPL) + i]
        wdt = W[k].shape[-1]
        g_sh.append(lax.dynamic_slice_in_dim(gfull, chip * wdt, wdt, axis=2))
    g_sh_p = _pack(g_sh)
    sh_res = _adamw(_pack([W[k] for k in _SHARD_SMALL]), (g_sh_p,), _pack([M[k] for k in _SHARD_SMALL]),
                    _pack([V[k] for k in _SHARD_SMALL]), g_sh_p.shape[0], "adamw_shard_small")
    shapes2 = [W[k].shape for k in _SHARD_SMALL]
    unp2 = [_unpack(a, shapes2) for a in sh_res]
    for i, k in enumerate(_SHARD_SMALL):
        res[k] = [u[i] for u in unp2]

    outs = [loss, grad_x]
    for j in range(4):
        outs += [res[k][j] for k in WEIGHTS]
    return tuple(outs)
```

```python
import functools

import numpy as np
import jax
import jax.numpy as jnp
from jax import lax
from jax.experimental import pallas as pl
from jax.experimental.pallas import tpu as pltpu

F32 = jnp.float32
BF16 = jnp.bfloat16
HI = lax.Precision.HIGHEST
MESH = pl.DeviceIdType.MESH

EPS = 1e-6
D_MODEL = 2048
GROUP_W = 512
SEQ = 2048
DEPTH = 2
N_IN = 6176
GRID_W = 64
CONV_K = 31
NA_HEADS = 8
NA_DH = 64
NA_ROWS = 8
NA_COLS = 16
GLA_HEADS = 4
GLA_DK = 64
GLA_DV = 128
GLA_RANK = 16
GLA_TAU = 16.0
CHUNK = 64
POOL_WINDOWS = (2, 4, 8, 16)
ADAM_LR, ADAM_B1, ADAM_B2, ADAM_EPS, ADAM_WD, ADAM_STEP = 0.001, 0.9, 0.999, 1e-08, 0.01, 10

A_VAL, A_GLU, A_GATE = 0, 512, 1024
B_Q, B_K, B_V, B_GATE = 1536, 2048, 2560, 3072
C_Q, C_K, C_V, C_GATE = 3584, 3840, 4096, 4608
D_VAL, D_GATE = 5120, 5632
LR_OFF = 6144
NZ = 6400
NEG = -1e30
VMEM_LIMIT = 56 * 1024 * 1024


def _cp(sem=None):
    return pltpu.CompilerParams(dimension_semantics=sem, vmem_limit_bytes=VMEM_LIMIT)


def _sigmoid(x):
    return 1.0 / (1.0 + jnp.exp(-x))


def _silu(x):
    return x * _sigmoid(x)


def _dsilu(x):
    s = _sigmoid(x)
    return s * (1.0 + x * (1.0 - s))


def _matmul(a, b, *, dims, out_dtype, tm, tn, tk, name, res=None):
    if dims == "nn":
        (M, K), N = a.shape, b.shape[1]
    elif dims == "nt":
        (M, K), N = a.shape, b.shape[0]
    else:
        (K, M), N = a.shape, b.shape[1]
    tm, tn, tk = min(tm, M), min(tn, N), min(tk, K)
    nk = K // tk
    assert M % tm == 0 and N % tn == 0 and K % tk == 0, (M, N, K, tm, tn, tk)
    dn = {"nn": (((1,), (0,)), ((), ())), "nt": (((1,), (1,)), ((), ())), "tn": (((0,), (0,)), ((), ()))}[dims]
    if dims == "tn":
        a_spec = pl.BlockSpec((tk, tm), lambda i, j, k: (k, i))
    else:
        a_spec = pl.BlockSpec((tm, tk), lambda i, j, k: (i, k))
    if dims == "nt":
        b_spec = pl.BlockSpec((tn, tk), lambda i, j, k: (j, k))
    else:
        b_spec = pl.BlockSpec((tk, tn), lambda i, j, k: (k, j))
    o_spec = pl.BlockSpec((tm, tn), lambda i, j, k: (i, j))
    has_res = res is not None

    def body(*refs):
        if has_res:
            a_ref, b_ref, r_ref, o_ref, acc = refs
        else:
            a_ref, b_ref, o_ref, acc = refs
        k = pl.program_id(2)

        @pl.when(k == 0)
        def _():
            acc[...] = jnp.zeros_like(acc)

        acc[...] += lax.dot_general(a_ref[...], b_ref[...], dn, preferred_element_type=F32)

        @pl.when(k == nk - 1)
        def _():
            r = acc[...]
            if has_res:
                r = r + r_ref[...]
            o_ref[...] = r.astype(o_ref.dtype)

    in_specs = [a_spec, b_spec] + ([o_spec] if has_res else [])
    args = (a, b) + ((res,) if has_res else ())
    return pl.pallas_call(
        body, out_shape=jax.ShapeDtypeStruct((M, N), out_dtype), grid=(M // tm, N // tn, nk),
        in_specs=in_specs, out_specs=o_spec, scratch_shapes=[pltpu.VMEM((tm, tn), F32)],
        name=name, compiler_params=_cp(("parallel", "parallel", "arbitrary")))(*args)


def _concat_cols(pieces, width, name):
    T = pieces[0].shape[0]
    tm = min(512, T)
    dt = pieces[0].dtype
    offs = np.cumsum([0] + [p.shape[1] for p in pieces])

    def body(*refs):
        o_ref = refs[-1]
        for p_ref, a, b in zip(refs[:-1], offs[:-1], offs[1:]):
            o_ref[:, a:b] = p_ref[...]
        if offs[-1] < width:
            o_ref[:, offs[-1]:width] = jnp.zeros((tm, width - offs[-1]), dt)

    return pl.pallas_call(
        body, out_shape=jax.ShapeDtypeStruct((T, width), dt), grid=(T // tm,),
        in_specs=[pl.BlockSpec((tm, p.shape[1]), lambda i: (i, 0)) for p in pieces],
        out_specs=pl.BlockSpec((tm, width), lambda i: (i, 0)), name=name, compiler_params=_cp(("parallel",)))(*pieces)


def _rmsnorm_fwd(x, g, name):
    T, D = x.shape
    tm = 256

    def body(x_ref, g_ref, h_ref):
        xv = x_ref[...]
        r = lax.rsqrt(jnp.mean(xv * xv, axis=-1, keepdims=True) + EPS)
        h_ref[...] = (xv * r * g_ref[...]).astype(h_ref.dtype)

    return pl.pallas_call(
        body, out_shape=jax.ShapeDtypeStruct((T, D), BF16), grid=(T // tm,),
        in_specs=[pl.BlockSpec((tm, D), lambda i: (i, 0)), pl.BlockSpec((1, D), lambda i: (0, 0))],
        out_specs=pl.BlockSpec((tm, D), lambda i: (i, 0)), name=name, compiler_params=_cp(("parallel",)))(x, g)


def _rmsnorm_bwd(x, g, dh, dres, name):
    T, D = x.shape
    tm = 256

    def body(x_ref, g_ref, dh_ref, dres_ref, dx_ref, dxb_ref, dg_ref):
        xv = x_ref[...]
        r = lax.rsqrt(jnp.mean(xv * xv, axis=-1, keepdims=True) + EPS)
        xh = xv * r
        dh_v = dh_ref[...]
        dxh = dh_v * g_ref[...]
        dx = r * (dxh - xh * jnp.mean(dxh * xh, axis=-1, keepdims=True)) + dres_ref[...]
        dx_ref[...] = dx
        dxb_ref[...] = dx.astype(BF16)

        @pl.when(pl.program_id(0) == 0)
        def _():
            dg_ref[...] = jnp.zeros_like(dg_ref)

        dg_ref[...] += jnp.sum(dh_v * xh, axis=0, keepdims=True)

    row = pl.BlockSpec((tm, D), lambda i: (i, 0))
    vec = pl.BlockSpec((1, D), lambda i: (0, 0))
    return pl.pallas_call(
        body, out_shape=(jax.ShapeDtypeStruct((T, D), F32), jax.ShapeDtypeStruct((T, D), BF16), jax.ShapeDtypeStruct((1, D), F32)),
        grid=(T // tm,), in_specs=[row, vec, row, row], out_specs=(row, row, vec), name=name,
        compiler_params=_cp(("arbitrary",)))(x, g, dh, dres)


def _loss_head(y, target, name):
    T, D = y.shape
    tm = 256

    def body(y_ref, t_ref, d_ref, db_ref, l_ref):
        e = y_ref[...] - t_ref[...]
        d = e * (1.0 / D)
        d_ref[...] = d
        db_ref[...] = d.astype(BF16)

        @pl.when(pl.program_id(0) == 0)
        def _():
            l_ref[...] = jnp.zeros_like(l_ref)

        row = jnp.sum(e * e, axis=-1, keepdims=True) * (0.5 / D)
        l_ref[...] += jnp.sum(row, axis=0, keepdims=True)

    row = pl.BlockSpec((tm, D), lambda i: (i, 0))
    return pl.pallas_call(
        body, out_shape=(jax.ShapeDtypeStruct((T, D), F32), jax.ShapeDtypeStruct((T, D), BF16),
                         jax.ShapeDtypeStruct((1, 1), F32)), grid=(T // tm,),
        in_specs=[row, row], out_specs=(row, row, pl.BlockSpec((1, 1), lambda i: (0, 0))),
        name=name, compiler_params=_cp(("arbitrary",)))(y, target)


_GATE_COLS = (A_GATE // GROUP_W, B_GATE // GROUP_W, C_GATE // GROUP_W, D_GATE // GROUP_W)


def _gate_fwd(pres, z, name):
    T = z.shape[0]
    tm = 256

    def body(pa, pb, pc, pd, ga, gb, gc, gd, y_ref):
        for n, (p, g) in enumerate(((pa, ga), (pb, gb), (pc, gc), (pd, gd))):
            y_ref[:, n * GROUP_W:(n + 1) * GROUP_W] = (p[...] * _silu(g[...])).astype(BF16)

    pre_spec = pl.BlockSpec((tm, GROUP_W), lambda i: (i, 0))
    gate_specs = [pl.BlockSpec((tm, GROUP_W), functools.partial(lambda i, c: (i, c), c=c)) for c in _GATE_COLS]
    return pl.pallas_call(
        body, out_shape=jax.ShapeDtypeStruct((T, 4 * GROUP_W), BF16), grid=(T // tm,),
        in_specs=[pre_spec] * 4 + gate_specs, out_specs=pl.BlockSpec((tm, 4 * GROUP_W), lambda i: (i, 0)),
        name=name, compiler_params=_cp(("parallel",)))(*pres, z, z, z, z)


def _gate_bwd(dy, pres, z, name):
    T = z.shape[0]
    tm = 256

    def body(dy_ref, pa, pb, pc, pd, ga, gb, gc, gd, dpa, dpb, dpc, dpd, dga, dgb, dgc, dgd):
        for n, (p, g, dp, dg) in enumerate(((pa, ga, dpa, dga), (pb, gb, dpb, dgb), (pc, gc, dpc, dgc), (pd, gd, dpd, dgd))):
            d = dy_ref[:, n * GROUP_W:(n + 1) * GROUP_W]
            gv = g[...]
            dp[...] = d * _silu(gv)
            dg[...] = (d * p[...] * _dsilu(gv)).astype(BF16)

    pre_spec = pl.BlockSpec((tm, GROUP_W), lambda i: (i, 0))
    gate_specs = [pl.BlockSpec((tm, GROUP_W), functools.partial(lambda i, c: (i, c), c=c)) for c in _GATE_COLS]
    outs = tuple([jax.ShapeDtypeStruct((T, GROUP_W), F32)] * 4 + [jax.ShapeDtypeStruct((T, GROUP_W), BF16)] * 4)
    return pl.pallas_call(
        body, out_shape=outs, grid=(T // tm,),
        in_specs=[pl.BlockSpec((tm, 4 * GROUP_W), lambda i: (i, 0))] + [pre_spec] * 4 + gate_specs,
        out_specs=tuple([pre_spec] * 8), name=name, compiler_params=_cp(("parallel",)))(dy, *pres, z, z, z, z)


_PAD = 16
_RC = 256


def _conv_fwd(z, conv_w32, conv_b, S, name):
    T = z.shape[0]
    E = T // S
    LW = 128

    def body(val_ref, glu_ref, w_ref, b_ref, y_ref, upad):
        upad[0:_PAD, :] = jnp.zeros((_PAD, LW), F32)
        upad[_PAD + S:_PAD + S + _PAD, :] = jnp.zeros((_PAD, LW), F32)
        upad[_PAD:_PAD + S, :] = val_ref[...] * _sigmoid(glu_ref[...])
        for r in range(S // _RC):
            acc = jnp.broadcast_to(b_ref[...], (_RC, LW))
            for k in range(CONV_K):
                st = r * _RC + k + 1
                acc = acc + upad[st:st + _RC, :] * w_ref[k:k + 1, :]
            y_ref[r * _RC:(r + 1) * _RC, :] = acc

    return pl.pallas_call(
        body, out_shape=jax.ShapeDtypeStruct((T, GROUP_W), F32), grid=(E, GROUP_W // LW),
        in_specs=[pl.BlockSpec((S, LW), lambda e, j: (e, A_VAL // LW + j)),
                  pl.BlockSpec((S, LW), lambda e, j: (e, A_GLU // LW + j)),
                  pl.BlockSpec((32, LW), lambda e, j: (0, j)),
                  pl.BlockSpec((1, LW), lambda e, j: (0, j))],
        out_specs=pl.BlockSpec((S, LW), lambda e, j: (e, j)),
        scratch_shapes=[pltpu.VMEM((S + 2 * _PAD, LW), F32)],
        name=name, compiler_params=_cp(("parallel", "parallel")))(z, z, conv_w32, conv_b)


def _conv_bwd(z, conv_w32, dyc, S, name):
    T = z.shape[0]
    E = T // S
    LW = 128

    def body(val_ref, glu_ref, w_ref, dy_ref, dval_ref, dglu_ref, dw_ref, db_ref, upad, dpad):
        e = pl.program_id(1)
        zeros = jnp.zeros((_PAD, LW), F32)
        upad[0:_PAD, :] = zeros
        upad[_PAD + S:_PAD + S + _PAD, :] = zeros
        dpad[0:_PAD, :] = zeros
        dpad[_PAD + S:_PAD + S + _PAD, :] = zeros
        upad[_PAD:_PAD + S, :] = val_ref[...] * _sigmoid(glu_ref[...])
        dpad[_PAD:_PAD + S, :] = dy_ref[...]

        @pl.when(e == 0)
        def _():
            dw_ref[...] = jnp.zeros_like(dw_ref)
            db_ref[...] = jnp.zeros_like(db_ref)

        db_ref[...] += jnp.sum(dy_ref[...], axis=0, keepdims=True)
        for r in range(S // _RC):
            dyr = dy_ref[r * _RC:(r + 1) * _RC, :]
            du = jnp.zeros((_RC, LW), F32)
            for k in range(CONV_K):
                st = r * _RC + k + 1
                dw_ref[k:k + 1, :] += jnp.sum(dyr * upad[st:st + _RC, :], axis=0, keepdims=True)
                sd = r * _RC + (CONV_K - 1 - k) + 1
                du = du + dpad[sd:sd + _RC, :] * w_ref[k:k + 1, :]
            sl = slice(r * _RC, (r + 1) * _RC)
            val = val_ref[sl, :]
            sg = _sigmoid(glu_ref[sl, :])
            dval_ref[sl, :] = (du * sg).astype(BF16)
            dglu_ref[sl, :] = (du * val * sg * (1.0 - sg)).astype(BF16)

    blk = pl.BlockSpec((S, LW), lambda j, e: (e, j))
    return pl.pallas_call(
        body, out_shape=(jax.ShapeDtypeStruct((T, GROUP_W), BF16), jax.ShapeDtypeStruct((T, GROUP_W), BF16),
                         jax.ShapeDtypeStruct((32, GROUP_W), F32), jax.ShapeDtypeStruct((1, GROUP_W), F32)),
        grid=(GROUP_W // LW, E),
        in_specs=[pl.BlockSpec((S, LW), lambda j, e: (e, A_VAL // LW + j)),
                  pl.BlockSpec((S, LW), lambda j, e: (e, A_GLU // LW + j)),
                  pl.BlockSpec((32, LW), lambda j, e: (0, j)), blk],
        out_specs=(blk, blk, pl.BlockSpec((32, LW), lambda j, e: (0, j)), pl.BlockSpec((1, LW), lambda j, e: (0, j))),
        scratch_shapes=[pltpu.VMEM((S + 2 * _PAD, LW), F32), pltpu.VMEM((S + 2 * _PAD, LW), F32)],
        name=name, compiler_params=_cp(("parallel", "arbitrary")))(z, z, conv_w32, dyc)


def _ln_silu_fwd(yc, g, b, name):
    T, C = yc.shape
    tm = 256

    def body(y_ref, g_ref, b_ref, o_ref):
        y = y_ref[...]
        mu = jnp.mean(y, axis=-1, keepdims=True)
        yc_ = y - mu
        r = lax.rsqrt(jnp.mean(yc_ * yc_, axis=-1, keepdims=True) + EPS)
        o_ref[...] = _silu(yc_ * r * g_ref[...] + b_ref[...])

    row = pl.BlockSpec((tm, C), lambda i: (i, 0))
    vec = pl.BlockSpec((1, C), lambda i: (0, 0))
    return pl.pallas_call(body, out_shape=jax.ShapeDtypeStruct((T, C), F32), grid=(T // tm,),
                          in_specs=[row, vec, vec], out_specs=row, name=name, compiler_params=_cp(("parallel",)))(yc, g, b)


def _ln_silu_bwd(yc, g, b, dpre, name):
    T, C = yc.shape
    tm = 256

    def body(y_ref, g_ref, b_ref, dp_ref, dy_ref, dg_ref, db_ref):
        y = y_ref[...]
        mu = jnp.mean(y, axis=-1, keepdims=True)
        yc_ = y - mu
        r = lax.rsqrt(jnp.mean(yc_ * yc_, axis=-1, keepdims=True) + EPS)
        xh = yc_ * r
        gv = g_ref[...]
        dln = dp_ref[...] * _dsilu(xh * gv + b_ref[...])
        dxh = dln * gv
        dy_ref[...] = r * (dxh - jnp.mean(dxh, axis=-1, keepdims=True) - xh * jnp.mean(dxh * xh, axis=-1, keepdims=True))

        @pl.when(pl.program_id(0) == 0)
        def _():
            dg_ref[...] = jnp.zeros_like(dg_ref)
            db_ref[...] = jnp.zeros_like(db_ref)

        dg_ref[...] += jnp.sum(dln * xh, axis=0, keepdims=True)
        db_ref[...] += jnp.sum(dln, axis=0, keepdims=True)

    row = pl.BlockSpec((tm, C), lambda i: (i, 0))
    vec = pl.BlockSpec((1, C), lambda i: (0, 0))
    return pl.pallas_call(
        body, out_shape=(jax.ShapeDtypeStruct((T, C), F32), jax.ShapeDtypeStruct((1, C), F32), jax.ShapeDtypeStruct((1, C), F32)),
        grid=(T // tm,), in_specs=[row, vec, vec, row], out_specs=(row, vec, vec), name=name,
        compiler_params=_cp(("arbitrary",)))(yc, g, b, dpre)


def _pool_counts(S, w, rows0, n):
    t = (lax.broadcasted_iota(jnp.int32, (n, 1), 0) + rows0)
    lo = jnp.maximum(t - w // 2, 0)
    hi = jnp.minimum(t + w // 2, S)
    return (hi - lo).astype(F32)


def _pool_fwd(z, pool_w, pool_scale, S, name):
    T = z.shape[0]
    E = T // S
    CG = 128

    def body(u_ref, w_ref, s_ref, o_ref, upad, dif):
        zeros = jnp.zeros((_PAD, GROUP_W), F32)
        upad[0:_PAD, :] = zeros
        upad[_PAD + S:_PAD + S + _PAD, :] = zeros
        upad[_PAD:_PAD + S, :] = u_ref[...]
        for gi, w in enumerate(POOL_WINDOWS):
            ls = slice(gi * CG, (gi + 1) * CG)
            for r in range(S // _RC):
                acc = jnp.zeros((_RC, CG), F32)
                for j in range(-(w // 2), w // 2):
                    st = _PAD + r * _RC + j
                    acc = acc + upad[st:st + _RC, ls]
                cnt = _pool_counts(S, w, r * _RC, _RC)
                dif[r * _RC:(r + 1) * _RC, :] = (acc / cnt - u_ref[r * _RC:(r + 1) * _RC, ls]).astype(BF16)
            yp = jnp.dot(dif[...], w_ref[gi], preferred_element_type=F32)
            o_ref[:, ls] = yp * s_ref[:, ls]

    return pl.pallas_call(
        body, out_shape=jax.ShapeDtypeStruct((T, GROUP_W), F32), grid=(E,),
        in_specs=[pl.BlockSpec((S, GROUP_W), lambda e: (e, D_VAL // GROUP_W)),
                  pl.BlockSpec((4, CG, CG), lambda e: (0, 0, 0)),
                  pl.BlockSpec((1, GROUP_W), lambda e: (0, 0))],
        out_specs=pl.BlockSpec((S, GROUP_W), lambda e: (e, 0)),
        scratch_shapes=[pltpu.VMEM((S + 2 * _PAD, GROUP_W), F32), pltpu.VMEM((S, CG), BF16)],
        name=name, compiler_params=_cp(("parallel",)))(z, pool_w, pool_scale)


def _pool_bwd(z, pool_w, pool_scale, dpre, S, name):
    T = z.shape[0]
    E = T // S
    CG = 128

    def body(u_ref, w_ref, s_ref, dp_ref, du_ref, dw_ref, ds_ref, upad, dif, qpad):
        zeros = jnp.zeros((_PAD, GROUP_W), F32)
        upad[0:_PAD, :] = zeros
        upad[_PAD + S:_PAD + S + _PAD, :] = zeros
        upad[_PAD:_PAD + S, :] = u_ref[...]
        zc = jnp.zeros((_PAD, CG), F32)
        qpad[0:_PAD, :] = zc
        qpad[_PAD + S:_PAD + S + _PAD, :] = zc

        @pl.when(pl.program_id(0) == 0)
        def _():
            dw_ref[...] = jnp.zeros_like(dw_ref)
            ds_ref[...] = jnp.zeros_like(ds_ref)

        for gi, w in enumerate(POOL_WINDOWS):
            ls = slice(gi * CG, (gi + 1) * CG)
            for r in range(S // _RC):
                acc = jnp.zeros((_RC, CG), F32)
                for j in range(-(w // 2), w // 2):
                    st = _PAD + r * _RC + j
                    acc = acc + upad[st:st + _RC, ls]
                cnt = _pool_counts(S, w, r * _RC, _RC)
                dif[r * _RC:(r + 1) * _RC, :] = (acc / cnt - u_ref[r * _RC:(r + 1) * _RC, ls]).astype(BF16)
            dp = dp_ref[:, ls]
            yp = jnp.dot(dif[...], w_ref[gi], preferred_element_type=F32)
            ds_ref[:, ls] += jnp.sum(dp * yp, axis=0, keepdims=True)
            dys = (dp * s_ref[:, ls]).astype(BF16)
            dw_ref[gi] += lax.dot_general(dif[...], dys, (((0,), (0,)), ((), ())), preferred_element_type=F32)
            dm = lax.dot_general(dys, w_ref[gi], (((1,), (1,)), ((), ())), preferred_element_type=F32)
            for r in range(S // _RC):
                cnt = _pool_counts(S, w, r * _RC, _RC)
                qpad[_PAD + r * _RC:_PAD + (r + 1) * _RC, :] = dm[r * _RC:(r + 1) * _RC, :] / cnt
            for r in range(S // _RC):
                acc = -dm[r * _RC:(r + 1) * _RC, :]
                for j in range(-(w // 2) + 1, w // 2 + 1):
                    st = _PAD + r * _RC + j
                    acc = acc + qpad[st:st + _RC, :]
                du_ref[r * _RC:(r + 1) * _RC, ls] = acc.astype(BF16)

    return pl.pallas_call(
        body, out_shape=(jax.ShapeDtypeStruct((T, GROUP_W), BF16), jax.ShapeDtypeStruct((4, CG, CG), F32),
                         jax.ShapeDtypeStruct((1, GROUP_W), F32)), grid=(E,),
        in_specs=[pl.BlockSpec((S, GROUP_W), lambda e: (e, D_VAL // GROUP_W)),
                  pl.BlockSpec((4, CG, CG), lambda e: (0, 0, 0)),
                  pl.BlockSpec((1, GROUP_W), lambda e: (0, 0)),
                  pl.BlockSpec((S, GROUP_W), lambda e: (e, 0))],
        out_specs=(pl.BlockSpec((S, GROUP_W), lambda e: (e, 0)), pl.BlockSpec((4, CG, CG), lambda e: (0, 0, 0)),
                   pl.BlockSpec((1, GROUP_W), lambda e: (0, 0))),
        scratch_shapes=[pltpu.VMEM((S + 2 * _PAD, GROUP_W), F32), pltpu.VMEM((S, CG), BF16),
                        pltpu.VMEM((S + 2 * _PAD, CG), F32)],
        name=name, compiler_params=_cp(("arbitrary",)))(z, pool_w, pool_scale, dpre)


def _na_tables():
    d = np.arange(NA_ROWS)[:, None]
    kr = np.arange(NA_ROWS)[None, :]
    ro = kr - d + (NA_ROWS - 1)
    qc = np.arange(GRID_W)[:, None]
    kc = np.arange(GRID_W)[None, :]
    cs = np.clip(qc - NA_COLS // 2, 0, GRID_W - NA_COLS)
    valid = (kc >= cs) & (kc < cs + NA_COLS)
    co = np.clip(kc - qc + (NA_COLS - 1), 0, 2 * NA_COLS - 2)
    return ro, co, valid


def _na_onehots():
    ro, co, valid = _na_tables()
    e_np = np.zeros((GRID_W, GRID_W, 128), np.float32)
    qi, ki = np.nonzero(valid)
    e_np[qi, ki, co[qi, ki]] = 1.0
    a_np = np.zeros((16, NA_ROWS * NA_ROWS), np.float32)
    a_np[ro.reshape(-1), np.arange(NA_ROWS * NA_ROWS)] = 1.0
    mask = np.where(valid, 0.0, NEG).astype(np.float32).reshape(1, GRID_W * GRID_W)
    return e_np.reshape(GRID_W * GRID_W, 128), a_np, mask


def _na_bias(rpb, name):
    e_np, a_np, mask = _na_onehots()
    H = NA_HEADS
    rp = jnp.pad(rpb, ((0, 0), (0, 1), (0, 128 - rpb.shape[2])))

    def body(r_ref, e_ref, at_ref, m_ref, o_ref):
        t = jnp.dot(at_ref[...], r_ref[0], precision=HI, preferred_element_type=F32)
        o_ref[0] = lax.dot_general(t, e_ref[...], (((1,), (1,)), ((), ())), precision=HI,
                                   preferred_element_type=F32) + m_ref[...]

    out = pl.pallas_call(
        body, out_shape=jax.ShapeDtypeStruct((H, NA_ROWS * NA_ROWS, GRID_W * GRID_W), F32), grid=(H,),
        in_specs=[pl.BlockSpec((1, 16, 128), lambda h: (h, 0, 0)),
                  pl.BlockSpec((GRID_W * GRID_W, 128), lambda h: (0, 0)),
                  pl.BlockSpec((NA_ROWS * NA_ROWS, 16), lambda h: (0, 0)),
                  pl.BlockSpec((1, GRID_W * GRID_W), lambda h: (0, 0))],
        out_specs=pl.BlockSpec((1, NA_ROWS * NA_ROWS, GRID_W * GRID_W), lambda h: (h, 0, 0)),
        name=name, compiler_params=_cp(("parallel",)))(rp, jnp.asarray(e_np), jnp.asarray(a_np.T), jnp.asarray(mask))
    t = out.reshape(H, NA_ROWS, NA_ROWS, GRID_W, GRID_W)
    return jnp.transpose(t, (0, 1, 3, 2, 4)).reshape(H, NA_ROWS, GRID_W, NA_ROWS * GRID_W)


def _seg_mean_matrix(width, seg):
    i = np.arange(width)
    return jnp.asarray((i[:, None] // seg == i[None, :] // seg).astype(np.float32) / seg)


def _na_fwd(z, qg, kg, bias, S, name):
    T = z.shape[0]
    E = T // S
    rows = S // GRID_W
    WIN = NA_ROWS * GRID_W
    seg = _seg_mean_matrix(128, NA_DH)

    def body(q_ref, k_ref, v_ref, qg_ref, kg_ref, bias_ref, seg_ref, o_ref, qs, ks, vs, s_all, p_all):
        for c in range(S // _RC):
            sl = slice(c * _RC, (c + 1) * _RC)
            q = q_ref[sl, :]
            k = k_ref[sl, :]
            qn = q * lax.rsqrt(jnp.dot(q * q, seg_ref[...], precision=HI, preferred_element_type=F32) + EPS) * qg_ref[...]
            kn = k * lax.rsqrt(jnp.dot(k * k, seg_ref[...], precision=HI, preferred_element_type=F32) + EPS) * kg_ref[...]
            v = v_ref[sl, :]
            for hh in range(2):
                ls = slice(hh * NA_DH, (hh + 1) * NA_DH)
                qs[hh, sl, :] = qn[:, ls].astype(BF16)
                ks[hh, sl, :] = kn[:, ls].astype(BF16)
                vs[hh, sl, :] = v[:, ls].astype(BF16)
        def where(r):
            rs = jnp.clip(r - NA_ROWS // 2, 0, rows - NA_ROWS)
            return rs, pl.multiple_of(r * GRID_W, GRID_W), pl.multiple_of(rs * GRID_W, GRID_W)

        def scores(r, carry):
            rs, q0, k0 = where(r)
            for hh in range(2):
                s = lax.dot_general(qs[hh, pl.ds(q0, GRID_W), :], ks[hh, pl.ds(k0, WIN), :], (((1,), (1,)), ((), ())),
                                    preferred_element_type=F32) * (NA_DH ** -0.5)
                s_all[hh, pl.ds(q0, GRID_W), :] = s + bias_ref[hh, r - rs]
            return carry
        lax.fori_loop(0, rows, scores, 0, unroll=4)

        def soft(r, carry):
            _, q0, _ = where(r)
            for hh in range(2):
                s = s_all[hh, pl.ds(q0, GRID_W), :]
                p = jnp.exp(s - jnp.max(s, axis=-1, keepdims=True))
                p_all[hh, pl.ds(q0, GRID_W), :] = (p * (1.0 / jnp.sum(p, axis=-1, keepdims=True))).astype(BF16)
            return carry
        lax.fori_loop(0, rows, soft, 0, unroll=2)

        def outp(r, carry):
            _, q0, k0 = where(r)
            outs = [jnp.dot(p_all[hh, pl.ds(q0, GRID_W), :], vs[hh, pl.ds(k0, WIN), :], preferred_element_type=F32)
                    for hh in range(2)]
            o_ref[pl.ds(q0, GRID_W), :] = jnp.concatenate(outs, axis=1)
            return carry
        lax.fori_loop(0, rows, outp, 0, unroll=4)

    LW = 128
    return pl.pallas_call(
        body, out_shape=jax.ShapeDtypeStruct((T, GROUP_W), F32), grid=(E, GROUP_W // LW),
        in_specs=[pl.BlockSpec((S, LW), lambda e, j: (e, B_Q // LW + j)),
                  pl.BlockSpec((S, LW), lambda e, j: (e, B_K // LW + j)),
                  pl.BlockSpec((S, LW), lambda e, j: (e, B_V // LW + j)),
                  pl.BlockSpec((1, LW), lambda e, j: (0, j)),
                  pl.BlockSpec((1, LW), lambda e, j: (0, j)),
                  pl.BlockSpec((2, NA_ROWS, GRID_W, WIN), lambda e, j: (j, 0, 0, 0)),
                  pl.BlockSpec((LW, LW), lambda e, j: (0, 0))],
        out_specs=pl.BlockSpec((S, LW), lambda e, j: (e, j)),
        scratch_shapes=[pltpu.VMEM((2, S, NA_DH), BF16)] * 3 + [pltpu.VMEM((2, S, WIN), F32), pltpu.VMEM((2, S, WIN), BF16)],
        name=name, compiler_params=_cp(("parallel", "parallel")))(z, z, z, qg, kg, bias, seg)


def _na_bwd(z, qg, kg, bias, do, S, name):
    T = z.shape[0]
    E = T // S
    rows = S // GRID_W
    WIN = NA_ROWS * GRID_W
    seg = _seg_mean_matrix(128, NA_DH)
    SC = NA_DH ** -0.5

    def body(q_ref, k_ref, v_ref, qg_ref, kg_ref, bias_ref, seg_ref, do_ref,
             dq_ref, dk_ref, dv_ref, dbias_ref, dqg_ref, dkg_ref, qs, ks, vs, dos, dqn, dkn, dvs, akt, avt,
             s_all, dp_all, p_all, ds_all):
        e = pl.program_id(1)

        @pl.when(e == 0)
        def _():
            dbias_ref[...] = jnp.zeros_like(dbias_ref)
            dqg_ref[...] = jnp.zeros_like(dqg_ref)
            dkg_ref[...] = jnp.zeros_like(dkg_ref)

        for c in range(S // _RC):
            sl = slice(c * _RC, (c + 1) * _RC)
            q = q_ref[sl, :]
            k = k_ref[sl, :]
            qn = q * lax.rsqrt(jnp.dot(q * q, seg_ref[...], precision=HI, preferred_element_type=F32) + EPS) * qg_ref[...]
            kn = k * lax.rsqrt(jnp.dot(k * k, seg_ref[...], precision=HI, preferred_element_type=F32) + EPS) * kg_ref[...]
            v = v_ref[sl, :]
            dd = do_ref[sl, :]
            for hh in range(2):
                ls = slice(hh * NA_DH, (hh + 1) * NA_DH)
                qs[hh, sl, :] = qn[:, ls].astype(BF16)
                ks[hh, sl, :] = kn[:, ls].astype(BF16)
                vs[hh, sl, :] = v[:, ls].astype(BF16)
                dos[hh, sl, :] = dd[:, ls].astype(BF16)
        akt[...] = jnp.zeros_like(akt)
        avt[...] = jnp.zeros_like(avt)

        def where(r):
            rs = jnp.clip(r - NA_ROWS // 2, 0, rows - NA_ROWS)
            return rs, pl.multiple_of(r * GRID_W, GRID_W), pl.multiple_of(rs * GRID_W, GRID_W)

        for hh in range(2):
            ls = slice(hh * NA_DH, (hh + 1) * NA_DH)

            def products(r, carry, hh=hh):
                rs, q0, k0 = where(r)
                s = lax.dot_general(qs[hh, pl.ds(q0, GRID_W), :], ks[hh, pl.ds(k0, WIN), :], (((1,), (1,)), ((), ())),
                                    preferred_element_type=F32) * SC
                s_all[pl.ds(q0, GRID_W), :] = s + bias_ref[hh, r - rs]
                dp_all[pl.ds(q0, GRID_W), :] = lax.dot_general(dos[hh, pl.ds(q0, GRID_W), :], vs[hh, pl.ds(k0, WIN), :],
                                                               (((1,), (1,)), ((), ())), preferred_element_type=F32)
                return carry
            lax.fori_loop(0, rows, products, 0, unroll=4)

            def soft(r, carry, hh=hh):
                rs, q0, _ = where(r)
                s = s_all[pl.ds(q0, GRID_W), :]
                p = jnp.exp(s - jnp.max(s, axis=-1, keepdims=True))
                p = p * (1.0 / jnp.sum(p, axis=-1, keepdims=True))
                dp = dp_all[pl.ds(q0, GRID_W), :]
                ds = p * (dp - jnp.sum(p * dp, axis=-1, keepdims=True))
                dbias_ref[hh, r - rs] += ds
                p_all[pl.ds(q0, GRID_W), :] = p.astype(BF16)
                ds_all[pl.ds(q0, GRID_W), :] = ds.astype(BF16)
                return carry
            lax.fori_loop(0, rows, soft, 0, unroll=2)

            def grads(r, carry, hh=hh, ls=ls):
                rs, q0, k0 = where(r)
                par = rs % 2
                t0 = (rs + par) // 2
                qr = qs[hh, pl.ds(q0, GRID_W), :]
                dor = dos[hh, pl.ds(q0, GRID_W), :]
                dsb = ds_all[pl.ds(q0, GRID_W), :]
                dqn[pl.ds(q0, GRID_W), ls] = jnp.dot(dsb, ks[hh, pl.ds(k0, WIN), :], preferred_element_type=F32) * SC
                dkt = lax.dot_general(qr, dsb, (((0,), (0,)), ((), ())), preferred_element_type=F32) * SC
                dvt = lax.dot_general(dor, p_all[pl.ds(q0, GRID_W), :], (((0,), (0,)), ((), ())), preferred_element_type=F32)
                akt[hh, par, pl.ds(t0, WIN // 128)] += jnp.stack([dkt[:, 128 * i:128 * (i + 1)] for i in range(WIN // 128)])
                avt[hh, par, pl.ds(t0, WIN // 128)] += jnp.stack([dvt[:, 128 * i:128 * (i + 1)] for i in range(WIN // 128)])
                return carry
            lax.fori_loop(0, rows, grads, 0, unroll=4)

        for hh in range(2):
            ls = slice(hh * NA_DH, (hh + 1) * NA_DH)
            for i in range(S // 128):
                for acc, dst in ((akt, dkn), (avt, dvs)):
                    odd = jnp.concatenate([acc[hh, 1, i][:, NA_DH:], acc[hh, 1, i + 1][:, :NA_DH]], axis=1)
                    dst[128 * i:128 * (i + 1), ls] = (acc[hh, 0, i] + odd).T

        for c in range(S // _RC):
            sl = slice(c * _RC, (c + 1) * _RC)
            for x_ref, g_ref, dn, dx_ref, dg_ref in ((q_ref, qg_ref, dqn, dq_ref, dqg_ref), (k_ref, kg_ref, dkn, dk_ref, dkg_ref)):
                x = x_ref[sl, :]
                r_ = lax.rsqrt(jnp.dot(x * x, seg_ref[...], precision=HI, preferred_element_type=F32) + EPS)
                xh = x * r_
                d = dn[sl, :]
                dxh = d * g_ref[...]
                mean = jnp.dot(dxh * xh, seg_ref[...], precision=HI, preferred_element_type=F32)
                dx_ref[sl, :] = (r_ * (dxh - xh * mean)).astype(BF16)
                dg_ref[...] += jnp.sum(d * xh, axis=0, keepdims=True)
            dv_ref[sl, :] = dvs[sl, :].astype(BF16)

    LW = 128
    blk = pl.BlockSpec((S, LW), lambda j, e: (e, j))
    vec = pl.BlockSpec((1, LW), lambda j, e: (0, j))
    bsp = pl.BlockSpec((2, NA_ROWS, GRID_W, WIN), lambda j, e: (j, 0, 0, 0))
    return pl.pallas_call(
        body, out_shape=(jax.ShapeDtypeStruct((T, GROUP_W), BF16),) * 3 + (
            jax.ShapeDtypeStruct((NA_HEADS, NA_ROWS, GRID_W, WIN), F32),
            jax.ShapeDtypeStruct((1, GROUP_W), F32), jax.ShapeDtypeStruct((1, GROUP_W), F32)),
        grid=(GROUP_W // LW, E),
        in_specs=[pl.BlockSpec((S, LW), lambda j, e: (e, B_Q // LW + j)),
                  pl.BlockSpec((S, LW), lambda j, e: (e, B_K // LW + j)),
                  pl.BlockSpec((S, LW), lambda j, e: (e, B_V // LW + j)),
                  vec, vec, bsp, pl.BlockSpec((LW, LW), lambda j, e: (0, 0)), blk],
        out_specs=(blk, blk, blk, bsp, vec, vec),
        scratch_shapes=[pltpu.VMEM((2, S, NA_DH), BF16)] * 4 + [pltpu.VMEM((S, LW), F32)] * 3
        + [pltpu.VMEM((2, 2, S // 128 + 1, NA_DH, 128), F32)] * 2
        + [pltpu.VMEM((S, WIN), F32)] * 2 + [pltpu.VMEM((S, WIN), BF16)] * 2,
        name=name, compiler_params=_cp(("parallel", "arbitrary")))(z, z, z, qg, kg, bias, seg, do)


def _na_rpb_grad(dbias, name):
    e_np, a_np, _ = _na_onehots()
    H = NA_HEADS
    x = dbias.reshape(H, NA_ROWS, GRID_W, NA_ROWS, GRID_W)
    x = jnp.transpose(x, (0, 1, 3, 2, 4)).reshape(H, NA_ROWS * NA_ROWS, GRID_W * GRID_W)

    def body(x_ref, e_ref, a_ref, o_ref):
        y = jnp.dot(x_ref[0], e_ref[...], precision=HI, preferred_element_type=F32)
        o_ref[0] = jnp.dot(a_ref[...], y, precision=HI, preferred_element_type=F32)

    out = pl.pallas_call(
        body, out_shape=jax.ShapeDtypeStruct((H, 16, 128), F32), grid=(H,),
        in_specs=[pl.BlockSpec((1, 64, GRID_W * GRID_W), lambda h: (h, 0, 0)),
                  pl.BlockSpec((GRID_W * GRID_W, 128), lambda h: (0, 0)),
                  pl.BlockSpec((16, 64), lambda h: (0, 0))],
        out_specs=pl.BlockSpec((1, 16, 128), lambda h: (h, 0, 0)),
        name=name, compiler_params=_cp(("parallel",)))(x, jnp.asarray(e_np), jnp.asarray(a_np))
    return out[:, :2 * NA_ROWS - 1, :2 * NA_COLS - 1]


_HK = GLA_HEADS * GLA_DK
_HV = GLA_HEADS * GLA_DV


def _gla_consts(reverse):
    i = np.arange(CHUNK)
    tri = (i[:, None] <= i[None, :]) if reverse else (i[:, None] >= i[None, :])
    j = np.arange(_HK)
    oseg = (j[:, None] // GLA_DK == j[None, :] // GLA_DK)
    return (jnp.asarray(tri.astype(np.float32)), jnp.asarray(tri.T.astype(np.float32)), jnp.asarray(oseg.astype(np.float32), BF16))


def _log_decay(lr, a2, ab):
    zg = jnp.dot(lr, a2, precision=HI, preferred_element_type=F32) + ab
    g = (jnp.minimum(zg, 0.0) - jnp.log(1.0 + jnp.exp(-jnp.abs(zg)))) * (1.0 / GLA_TAU)
    return zg, g


def _dotf(a, b, dn):
    return lax.dot_general(a, b, dn, precision=HI, preferred_element_type=F32)


def _dotb(a, b, dn):
    return lax.dot_general(a.astype(BF16), b.astype(BF16), dn, preferred_element_type=F32)


_COLS = 4
_SUB = 16
_NSUB = CHUNK // _SUB


def _gla_cross_blocks(reverse):
    return range(0, _NSUB - 1) if reverse else range(1, _NSUB)


def _gla_cross_terms(s, reverse, b_s, q_s, k_s, oseg_ref):
    r0 = s * _SUB
    ref = r0 + (_SUB - 1 if reverse else 0)
    bref = b_s[ref:ref + 1, :]
    rowj = lax.broadcasted_iota(jnp.int32, (CHUNK, 1), 0)
    seen = (rowj >= r0 + _SUB) if reverse else (rowj < r0)
    ek = jnp.where(seen, jnp.exp(jnp.minimum(bref - b_s[...], 0.0)), 0.0)
    kt = k_s[...] * ek
    eq = jnp.exp(jnp.minimum(b_s[r0:r0 + _SUB, :] - bref, 0.0))
    qt = q_s[r0:r0 + _SUB, :] * eq
    nmat = jnp.concatenate([kt.astype(BF16)] * GLA_HEADS, axis=0) * oseg_ref[...]
    return qt, eq, kt, ek, nmat


_NN = (((1,), (0,)), ((), ()))
_NT = (((1,), (1,)), ((), ()))
_TN = (((0,), (0,)), ((), ()))


_DT = 256


def _gla_block_tri(reverse):
    i = np.arange(_DT)
    same = i[:, None] // CHUNK == i[None, :] // CHUNK
    tri = (i[:, None] <= i[None, :]) if reverse else (i[:, None] >= i[None, :])
    return (tri & same).astype(np.float32)


def _gla_decay_fwd(z, a2, ab, reverse, name):
    T = z.shape[0]
    nc = _DT // CHUNK

    def body(lr_ref, a2_ref, ab_ref, m_ref, b_ref, ec_ref):
        _, g = _log_decay(lr_ref[...], a2_ref[...], ab_ref[...])
        b_ref[...] = _dotf(m_ref[...], g, _NN)
        for c in range(nc):
            ec_ref[c] = jnp.exp(_dotf(g[c * CHUNK:(c + 1) * CHUNK, :], jnp.ones((CHUNK, GLA_DV), F32), _TN))

    return pl.pallas_call(
        body, out_shape=(jax.ShapeDtypeStruct((T, _HK), F32), jax.ShapeDtypeStruct((T // CHUNK, _HK, GLA_DV), F32)),
        grid=(T // _DT,),
        in_specs=[pl.BlockSpec((_DT, 128), lambda i: (i, LR_OFF // 128)),
                  pl.BlockSpec((128, _HK), lambda i: (0, 0)),
                  pl.BlockSpec((1, _HK), lambda i: (0, 0)),
                  pl.BlockSpec((_DT, _DT), lambda i: (0, 0))],
        out_specs=(pl.BlockSpec((_DT, _HK), lambda i: (i, 0)), pl.BlockSpec((nc, _HK, GLA_DV), lambda i: (i, 0, 0))),
        name=name, compiler_params=_cp(("parallel",)))(z, a2, ab, jnp.asarray(_gla_block_tri(reverse)))


def _gla_decay_bwd(z, a2_f, ab_f, a2_b, ab_b, db_f, db_b, name):
    T = z.shape[0]

    def body(lr_ref, a2f_ref, abf_ref, a2b_ref, abb_ref, mf_ref, mb_ref, dbf_ref, dbb_ref,
             dlr_ref, da2f_ref, dabf_ref, da2b_ref, dabb_ref):
        @pl.when(pl.program_id(0) == 0)
        def _():
            for r in (da2f_ref, dabf_ref, da2b_ref, dabb_ref):
                r[...] = jnp.zeros_like(r)

        lr = lr_ref[...]
        dlr = jnp.zeros((_DT, 128), F32)
        for a2_ref, ab_ref, mt_ref, db_ref, da2_ref, dab_ref in ((a2f_ref, abf_ref, mf_ref, dbf_ref, da2f_ref, dabf_ref),
                                                                 (a2b_ref, abb_ref, mb_ref, dbb_ref, da2b_ref, dabb_ref)):
            zg, _ = _log_decay(lr, a2_ref[...], ab_ref[...])
            dg = _dotf(mt_ref[...], db_ref[...], _NN)
            dzg = dg * (1.0 / (1.0 + jnp.exp(zg))) * (1.0 / GLA_TAU)
            dlr = dlr + _dotf(dzg, a2_ref[...], _NT)
            da2_ref[...] += _dotf(lr, dzg, _TN)
            dab_ref[...] += jnp.sum(dzg, axis=0, keepdims=True)
        dlr_ref[...] = dlr.astype(BF16)

    a2s = pl.BlockSpec((128, _HK), lambda i: (0, 0))
    abs_ = pl.BlockSpec((1, _HK), lambda i: (0, 0))
    ms = pl.BlockSpec((_DT, _DT), lambda i: (0, 0))
    row = pl.BlockSpec((_DT, _HK), lambda i: (i, 0))
    return pl.pallas_call(
        body, out_shape=(jax.ShapeDtypeStruct((T, 128), BF16), jax.ShapeDtypeStruct((128, _HK), F32), jax.ShapeDtypeStruct((1, _HK), F32),
                         jax.ShapeDtypeStruct((128, _HK), F32), jax.ShapeDtypeStruct((1, _HK), F32)),
        grid=(T // _DT,),
        in_specs=[pl.BlockSpec((_DT, 128), lambda i: (i, LR_OFF // 128)), a2s, abs_, a2s, abs_, ms, ms, row, row],
        out_specs=(pl.BlockSpec((_DT, 128), lambda i: (i, 0)), a2s, abs_, a2s, abs_),
        name=name, compiler_params=_cp(("arbitrary",)))(
            z, a2_f, ab_f, a2_b, ab_b, jnp.asarray(_gla_block_tri(False).T), jnp.asarray(_gla_block_tri(True).T), db_f, db_b)


def _gla_fwd(z, b_all, ecol, S, reverse, name):
    T = z.shape[0]
    E = T // S
    n = S // CHUNK
    _, _, oseg = _gla_consts(reverse)
    last = 0 if reverse else CHUNK - 1

    def body(q_ref, k_ref, v_ref, b_ref, ec_ref, oseg_ref, o_ref, a_ref, st_ref, st, b_s, q_s, k_s):
        @pl.when(pl.program_id(1) == 0)
        def _():
            st[...] = jnp.zeros_like(st)

        q = q_ref[...] * (GLA_DK ** -0.5)
        k = k_ref[...]
        v = v_ref[...]
        b = b_ref[...]
        bl_row = b_ref[last:last + 1, :]
        e_col = ec_ref[0]
        b_s[...] = b
        q_s[...] = q
        k_s[...] = k
        lane = lax.broadcasted_iota(jnp.int32, (1, _HK), 1) % GLA_DK

        rowi = lax.broadcasted_iota(jnp.int32, (CHUNK, 1), 0)
        blk0 = (rowi // _SUB) * _SUB

        def cols(jj, a):
            ts = []
            for u in range(_COLS):
                jp = jj * _COLS + u
                tiles = []
                for s in range(_NSUB):
                    rs_ = slice(s * _SUB, (s + 1) * _SUB)
                    bj = b_s[pl.ds(s * _SUB + jp, 1), :]
                    kj = k_s[pl.ds(s * _SUB + jp, 1), :]
                    tiles.append(q_s[rs_, :] * jnp.exp(jnp.minimum(b_s[rs_, :] - bj, 0.0)) * kj)
                ts.append(jnp.concatenate(tiles, axis=0).astype(BF16))
            r = jnp.dot(jnp.concatenate(ts, axis=0), oseg_ref[...], preferred_element_type=F32)
            for u in range(_COLS):
                a = jnp.where(lane == blk0 + (jj * _COLS + u), r[u * CHUNK:(u + 1) * CHUNK, :], a)
            return a

        a = lax.fori_loop(0, _SUB // _COLS, cols, jnp.zeros((CHUNK, _HK), F32))
        keep = (rowi <= lane) if reverse else (rowi >= lane)
        a = jnp.where(keep, a, 0.0)
        cross = []
        for s in range(_NSUB):
            if s in _gla_cross_blocks(reverse):
                qt, _, _, _, nmat = _gla_cross_terms(s, reverse, b_s, q_s, k_s, oseg_ref)
                cross.append(lax.dot_general(qt.astype(BF16), nmat, _NT, preferred_element_type=F32))
            else:
                cross.append(jnp.zeros((_SUB, _HK), F32))
        a = a + jnp.concatenate(cross, axis=0)
        a_ref[...] = a
        st_ref[0] = st[...]
        qb = q * jnp.exp(b)
        kd = k * jnp.exp(bl_row - b)
        for h in range(GLA_HEADS):
            ks_ = slice(h * GLA_DK, (h + 1) * GLA_DK)
            vs_ = slice(h * GLA_DV, (h + 1) * GLA_DV)
            s_h = st[ks_, :]
            o_ref[:, vs_] = _dotb(qb[:, ks_], s_h, _NN) + _dotb(a[:, ks_], v[:, vs_], _NN)
            st[ks_, :] = s_h * e_col[ks_, :] + _dotb(kd[:, ks_], v[:, vs_], _TN)

    def rowblk(e, c):
        return e * n + ((n - 1 - c) if reverse else c)

    return pl.pallas_call(
        body, out_shape=(jax.ShapeDtypeStruct((T, _HV), F32), jax.ShapeDtypeStruct((T, _HK), F32),
                         jax.ShapeDtypeStruct((T // CHUNK, _HK, GLA_DV), F32)),
        grid=(E, n),
        in_specs=[pl.BlockSpec((CHUNK, _HK), lambda e, c: (rowblk(e, c), C_Q // _HK)),
                  pl.BlockSpec((CHUNK, _HK), lambda e, c: (rowblk(e, c), C_K // _HK)),
                  pl.BlockSpec((CHUNK, _HV), lambda e, c: (rowblk(e, c), C_V // _HV)),
                  pl.BlockSpec((CHUNK, _HK), lambda e, c: (rowblk(e, c), 0)),
                  pl.BlockSpec((1, _HK, GLA_DV), lambda e, c: (rowblk(e, c), 0, 0)),
                  pl.BlockSpec((_HK, _HK), lambda e, c: (0, 0))],
        out_specs=(pl.BlockSpec((CHUNK, _HV), lambda e, c: (rowblk(e, c), 0)),
                   pl.BlockSpec((CHUNK, _HK), lambda e, c: (rowblk(e, c), 0)),
                   pl.BlockSpec((1, _HK, GLA_DV), lambda e, c: (rowblk(e, c), 0, 0))),
        scratch_shapes=[pltpu.VMEM((_HK, GLA_DV), F32)] + [pltpu.VMEM((CHUNK, _HK), F32)] * 3,
        name=name, compiler_params=_cp(("parallel", "arbitrary")))(z, z, z, b_all, ecol, oseg)


def _gla_bwd(z, b_all, ecol, att, states, do, prev, S, reverse, name):
    T = z.shape[0]
    E = T // S
    n = S // CHUNK
    _, _, oseg = _gla_consts(reverse)
    has_prev = prev is not None
    odt = BF16 if has_prev else F32
    last = 0 if reverse else CHUNK - 1

    def body(*refs):
        (q_ref, k_ref, v_ref, b_ref, ec_ref, oseg_ref, att_ref, st_ref, do_ref) = refs[:9]
        refs = refs[9:]
        if has_prev:
            pq_ref, pk_ref, pv_ref = refs[:3]
            refs = refs[3:]
        (dq_ref, dk_ref, dv_ref, db_ref, dst, b_s, q_s, k_s, da_s, dqb_s, dkd_s, dk3_s, dbn_s, dsp_s) = refs

        @pl.when(pl.program_id(1) == 0)
        def _():
            dst[...] = jnp.zeros_like(dst)

        q = q_ref[...] * (GLA_DK ** -0.5)
        k = k_ref[...]
        v = v_ref[...]
        b = b_ref[...]
        bl_row = b_ref[last:last + 1, :]
        eb = jnp.exp(b)
        ekd = jnp.exp(bl_row - b)
        qb = q * eb
        kd = k * ekd
        b_s[...] = b
        q_s[...] = q
        k_s[...] = k
        att = att_ref[...]
        s_all = st_ref[0]
        dsn = dst[...]
        e_col = ec_ref[0]
        do = do_ref[...]
        lane = lax.broadcasted_iota(jnp.int32, (1, _HK), 1) % GLA_DK
        rowi = lax.broadcasted_iota(jnp.int32, (CHUNK, 1), 0)
        keep = (rowi <= lane) if reverse else (rowi >= lane)
        for h in range(GLA_HEADS):
            ks_ = slice(h * GLA_DK, (h + 1) * GLA_DK)
            vs_ = slice(h * GLA_DV, (h + 1) * GLA_DV)
            do_h = do[:, vs_]
            s_h = s_all[ks_, :]
            dsn_h = dsn[ks_, :]
            dqb_s[:, ks_] = _dotb(do_h, s_h, _NT)
            dsp_s[ks_, :] = _dotb(qb[:, ks_], do_h, _TN) + dsn_h * e_col[ks_, :]
            da_s[:, ks_] = _dotb(do_h, v[:, vs_], _NT)
            dv_h = _dotb(att[:, ks_], do_h, _TN) + _dotb(kd[:, ks_], dsn_h, _NN)
            if has_prev:
                dv_h = dv_h + pv_ref[:, vs_]
            dv_ref[:, vs_] = dv_h.astype(odt)
            dkd_s[:, ks_] = _dotb(v[:, vs_], dsn_h, _NT)
        da_s[...] = jnp.where(keep, da_s[...], 0.0)
        dqb = dqb_s[...]
        dkd = dkd_s[...]
        x = dsn * s_all * e_col
        dbl_row = _dotf(jnp.ones((8, GLA_DV), F32), x, _NT)[0:1, :] + jnp.sum(dkd * kd, axis=0, keepdims=True)

        blk0 = (rowi // _SUB) * _SUB

        def cols(jj, carry):
            dq3, db3 = list(carry[:_NSUB]), list(carry[_NSUB:])
            sel = [jnp.where(lane == blk0 + (jj * _COLS + u), da_s[...], 0.0).astype(BF16) for u in range(_COLS)]
            dcols = jnp.dot(jnp.concatenate(sel, axis=0), oseg_ref[...], preferred_element_type=F32)
            for u in range(_COLS):
                jp = jj * _COLS + u
                for s in range(_NSUB):
                    rs_ = slice(s * _SUB, (s + 1) * _SUB)
                    bj = b_s[pl.ds(s * _SUB + jp, 1), :]
                    kj = k_s[pl.ds(s * _SUB + jp, 1), :]
                    tm_ = dcols[u * CHUNK + s * _SUB:u * CHUNK + (s + 1) * _SUB, :] * jnp.exp(jnp.minimum(b_s[rs_, :] - bj, 0.0))
                    dq3[s] = dq3[s] + tm_ * kj
                    gq = tm_ * q_s[rs_, :]
                    dk3_s[pl.ds(s * _SUB + jp, 1), :] = jnp.sum(gq, axis=0, keepdims=True)
                    w = gq * kj
                    dbn_s[pl.ds(s * _SUB + jp, 1), :] = jnp.sum(w, axis=0, keepdims=True)
                    db3[s] = db3[s] + w
            return tuple(dq3) + tuple(db3)

        zero = jnp.zeros((_SUB, _HK), F32)
        acc = lax.fori_loop(0, _SUB // _COLS, cols, (zero,) * (2 * _NSUB))
        dq3 = jnp.concatenate(acc[:_NSUB], axis=0)
        db3 = jnp.concatenate(acc[_NSUB:], axis=0)
        head = lax.broadcasted_iota(jnp.int32, (1, _HK), 1) // GLA_DK
        dq_x, db_x = [], []
        dk_x = jnp.zeros((CHUNK, _HK), F32)
        db_k = jnp.zeros((CHUNK, _HK), F32)
        for s in range(_NSUB):
            if s not in _gla_cross_blocks(reverse):
                dq_x.append(zero)
                db_x.append(zero)
                continue
            r0 = s * _SUB
            qt, eq, kt, ek, nmat = _gla_cross_terms(s, reverse, b_s, q_s, k_s, oseg_ref)
            seen = (lane >= r0 + _SUB) if reverse else (lane < r0)
            dax = jnp.where(seen, da_s[r0:r0 + _SUB, :], 0.0).astype(BF16)
            dqt = jnp.dot(dax, nmat, preferred_element_type=F32)
            full = lax.dot_general(dax, qt.astype(BF16), _TN, preferred_element_type=F32)
            dkt = full[0:CHUNK, :]
            for h in range(1, GLA_HEADS):
                dkt = jnp.where(head == h, full[h * CHUNK:(h + 1) * CHUNK, :], dkt)
            dq_x.append(dqt * eq)
            db_x.append(dqt * qt)
            dk_x = dk_x + dkt * ek
            db_k = db_k + dkt * kt
        dq = (dqb * eb + dq3 + jnp.concatenate(dq_x, axis=0)) * (GLA_DK ** -0.5)
        dk = dkd * ekd + dk3_s[...] + dk_x
        db = dqb * qb - dkd * kd + db3 - dbn_s[...] + jnp.concatenate(db_x, axis=0) - db_k
        db_ref[...] = jnp.where(rowi == last, db + dbl_row, db)
        if has_prev:
            dq = dq + pq_ref[...]
            dk = dk + pk_ref[...]
        dq_ref[...] = dq.astype(odt)
        dk_ref[...] = dk.astype(odt)
        dst[...] = dsp_s[...]

    def rowblk(e, c):
        return e * n + (c if reverse else (n - 1 - c))

    hk = pl.BlockSpec((CHUNK, _HK), lambda e, c: (rowblk(e, c), 0))
    hv = pl.BlockSpec((CHUNK, _HV), lambda e, c: (rowblk(e, c), 0))
    stb = pl.BlockSpec((1, _HK, GLA_DV), lambda e, c: (rowblk(e, c), 0, 0))
    in_specs = [pl.BlockSpec((CHUNK, _HK), lambda e, c: (rowblk(e, c), C_Q // _HK)),
                pl.BlockSpec((CHUNK, _HK), lambda e, c: (rowblk(e, c), C_K // _HK)),
                pl.BlockSpec((CHUNK, _HV), lambda e, c: (rowblk(e, c), C_V // _HV)),
                hk, stb, pl.BlockSpec((_HK, _HK), lambda e, c: (0, 0)), hk, stb, hv]
    args = [z, z, z, b_all, ecol, oseg, att, states, do]
    if has_prev:
        in_specs += [hk, hk, hv]
        args += list(prev)
    return pl.pallas_call(
        body, out_shape=(jax.ShapeDtypeStruct((T, _HK), odt), jax.ShapeDtypeStruct((T, _HK), odt),
                         jax.ShapeDtypeStruct((T, _HV), odt), jax.ShapeDtypeStruct((T, _HK), F32)),
        grid=(E, n), in_specs=in_specs, out_specs=(hk, hk, hv, hk),
        scratch_shapes=[pltpu.VMEM((_HK, GLA_DV), F32)] + [pltpu.VMEM((CHUNK, _HK), F32)] * 8 + [pltpu.VMEM((_HK, GLA_DV), F32)],
        name=name, compiler_params=_cp(("parallel", "arbitrary")))(*args)


def _gla_norm_fwd(of, ob, og, name):
    T = of.shape[0]
    tm = 256

    def body(f_ref, b_ref, g_ref, o_ref):
        for h in range(GLA_HEADS):
            vs_ = slice(h * GLA_DV, (h + 1) * GLA_DV)
            o = f_ref[:, vs_] + b_ref[:, vs_]
            o_ref[:, vs_] = o * lax.rsqrt(jnp.mean(o * o, axis=-1, keepdims=True) + EPS) * g_ref[:, vs_]

    row = pl.BlockSpec((tm, _HV), lambda i: (i, 0))
    vec = pl.BlockSpec((1, _HV), lambda i: (0, 0))
    return pl.pallas_call(body, out_shape=jax.ShapeDtypeStruct((T, _HV), F32), grid=(T // tm,),
                          in_specs=[row, row, vec], out_specs=row, name=name, compiler_params=_cp(("parallel",)))(of, ob, og)


def _gla_norm_bwd(of, ob, og, dpre, name):
    T = of.shape[0]
    tm = 256

    def body(f_ref, b_ref, g_ref, dp_ref, do_ref, dg_ref):
        @pl.when(pl.program_id(0) == 0)
        def _():
            dg_ref[...] = jnp.zeros_like(dg_ref)

        for h in range(GLA_HEADS):
            vs_ = slice(h * GLA_DV, (h + 1) * GLA_DV)
            o = f_ref[:, vs_] + b_ref[:, vs_]
            r = lax.rsqrt(jnp.mean(o * o, axis=-1, keepdims=True) + EPS)
            xh = o * r
            dp = dp_ref[:, vs_]
            dxh = dp * g_ref[:, vs_]
            do_ref[:, vs_] = r * (dxh - xh * jnp.mean(dxh * xh, axis=-1, keepdims=True))
            dg_ref[:, vs_] += jnp.sum(dp * xh, axis=0, keepdims=True)

    row = pl.BlockSpec((tm, _HV), lambda i: (i, 0))
    vec = pl.BlockSpec((1, _HV), lambda i: (0, 0))
    return pl.pallas_call(
        body, out_shape=(jax.ShapeDtypeStruct((T, _HV), F32), jax.ShapeDtypeStruct((1, _HV), F32)), grid=(T // tm,),
        in_specs=[row, row, vec, row], out_specs=(row, vec), name=name, compiler_params=_cp(("arbitrary",)))(of, ob, og, dpre)


_ANY = pl.BlockSpec(memory_space=pl.ANY)


def _coords():
    return lax.axis_index("x"), lax.axis_index("y"), lax.axis_index("c")


def _other_chips(x, y):
    return ((1 - x, y), (x, 1 - y), (1 - x, 1 - y))


def _gather_weights(arrays, chunks, name):
    n = len(arrays)
    pieces = []
    for k in range(max(chunks)):
        for i, a in enumerate(arrays):
            if k < chunks[i]:
                rc = a.shape[1] // chunks[i]
                pieces.append((i, k * rc, rc))
    m = len(pieces)

    def body(*refs):
        srcs, dsts = refs[:n], refs[n:2 * n]
        send_sems, recv_sems, local_sems = refs[2 * n:]
        x, y, c = _coords()
        me = 2 * x + y
        loc = [pltpu.make_async_copy(s, d.at[me], local_sems.at[i]) for i, (s, d) in enumerate(zip(srcs, dsts))]
        for cp in loc:
            cp.start()
        ici = []
        for p, (i, r0, rc) in enumerate(pieces):
            for j, (px, py) in enumerate(_other_chips(x, y)):
                ici.append(pltpu.make_async_remote_copy(
                    src_ref=srcs[i].at[c, pl.ds(r0, rc)], dst_ref=dsts[i].at[me, c, pl.ds(r0, rc)],
                    send_sem=send_sems.at[3 * p + j], recv_sem=recv_sems.at[3 * p + j],
                    device_id=(px, py, c), device_id_type=MESH))
        for cp in ici:
            cp.start()
        fwd = []
        for p, (i, r0, rc) in enumerate(pieces):
            for j, (px, py) in enumerate(_other_chips(x, y)):
                ici[3 * p + j].wait_recv()
                part = dsts[i].at[2 * px + py, c, pl.ds(r0, rc)]
                cp = pltpu.make_async_remote_copy(
                    src_ref=part, dst_ref=part, send_sem=send_sems.at[3 * m + 3 * p + j], recv_sem=recv_sems.at[3 * m + 3 * p + j],
                    device_id=(x, y, 1 - c), device_id_type=MESH)
                cp.start()
                fwd.append(cp)
        for cp in fwd:
            cp.wait_recv()
        for cp in ici + fwd:
            cp.wait_send()
        for cp in loc:
            cp.wait()

    return pl.pallas_call(
        body, out_shape=tuple(jax.ShapeDtypeStruct((4,) + a.shape, a.dtype) for a in arrays),
        in_specs=[_ANY] * n, out_specs=(_ANY,) * n,
        scratch_shapes=[pltpu.SemaphoreType.DMA((6 * m,)), pltpu.SemaphoreType.DMA((6 * m,)), pltpu.SemaphoreType.DMA((n,))],
        name=name)(*arrays)


def _sibling_exchange(layered, whole, name):
    nl, n = len(layered), len(layered) + len(whole)

    def body(*refs):
        srcs, dsts = refs[:n], refs[n:2 * n]
        send_sems, recv_sems = refs[2 * n:]
        x, y, c = _coords()
        rem = [pltpu.make_async_remote_copy(src_ref=(s.at[1 - c] if i < nl else s), dst_ref=d, send_sem=send_sems.at[i],
                                            recv_sem=recv_sems.at[i], device_id=(x, y, 1 - c), device_id_type=MESH)
               for i, (s, d) in enumerate(zip(srcs, dsts))]
        for cp in rem:
            cp.start()
        for cp in rem:
            cp.wait()

    outs = [jax.ShapeDtypeStruct(a.shape[1:], a.dtype) for a in layered] + [jax.ShapeDtypeStruct(a.shape, a.dtype) for a in whole]
    return pl.pallas_call(
        body, out_shape=tuple(outs), in_specs=[_ANY] * n, out_specs=(_ANY,) * n,
        scratch_shapes=[pltpu.SemaphoreType.DMA((n,)), pltpu.SemaphoreType.DMA((n,))], name=name)(*layered, *whole)


def _chip_exchange(scatter, bcast, name):
    ns, n = len(scatter), len(scatter) + len(bcast)

    def body(*refs):
        srcs, dsts = refs[:n], refs[n:2 * n]
        send_sems, recv_sems, local_sems = refs[2 * n:]
        x, y, c = _coords()
        me = 2 * x + y
        loc = [pltpu.make_async_copy((s.at[me] if i < ns else s), d.at[me], local_sems.at[i])
               for i, (s, d) in enumerate(zip(srcs, dsts))]
        for cp in loc:
            cp.start()
        rem = []
        for j, (px, py) in enumerate(_other_chips(x, y)):
            for i, (s, d) in enumerate(zip(srcs, dsts)):
                rem.append(pltpu.make_async_remote_copy(
                    src_ref=(s.at[2 * px + py] if i < ns else s), dst_ref=d.at[me], send_sem=send_sems.at[n * j + i],
                    recv_sem=recv_sems.at[n * j + i], device_id=(px, py, c), device_id_type=MESH))
        for cp in rem:
            cp.start()
        for cp in rem:
            cp.wait()
        for cp in loc:
            cp.wait()

    outs = [jax.ShapeDtypeStruct(a.shape, a.dtype) for a in scatter] + [jax.ShapeDtypeStruct((4,) + a.shape, a.dtype) for a in bcast]
    return pl.pallas_call(
        body, out_shape=tuple(outs), in_specs=[_ANY] * n, out_specs=(_ANY,) * n,
        scratch_shapes=[pltpu.SemaphoreType.DMA((3 * n,)), pltpu.SemaphoreType.DMA((3 * n,)), pltpu.SemaphoreType.DMA((n,))],
        name=name)(*scatter, *bcast)


def _sum_slots(r, name):
    n, R, C = r.shape
    tm = min(256, R)

    def body(r_ref, o_ref):
        acc = r_ref[0].astype(F32)
        for i in range(1, n):
            acc = acc + r_ref[i].astype(F32)
        o_ref[...] = acc

    return pl.pallas_call(body, out_shape=jax.ShapeDtypeStruct((R, C), F32), grid=(R // tm,),
                          in_specs=[pl.BlockSpec((n, tm, C), lambda i: (0, i, 0))], out_specs=pl.BlockSpec((tm, C), lambda i: (i, 0)),
                          name=name, compiler_params=_cp(("parallel",)))(r)


def _add2(a, b, out_dtype, tm, name):
    R, C = a.shape
    tm = min(tm, R)

    def body(a_ref, b_ref, o_ref):
        o_ref[...] = (a_ref[...].astype(F32) + b_ref[...].astype(F32)).astype(out_dtype)

    blk = pl.BlockSpec((tm, C), lambda i: (i, 0))
    return pl.pallas_call(body, out_shape=jax.ShapeDtypeStruct((R, C), out_dtype), grid=(R // tm,), in_specs=[blk, blk],
                          out_specs=blk, name=name, compiler_params=_cp(("parallel",)))(a, b)


def _adamw_math(w, g, m, v):
    m = ADAM_B1 * m + (1.0 - ADAM_B1) * g
    v = ADAM_B2 * v + (1.0 - ADAM_B2) * (g * g)
    m_hat = m / (1.0 - ADAM_B1 ** ADAM_STEP)
    v_hat = v / (1.0 - ADAM_B2 ** ADAM_STEP)
    delta = -ADAM_LR * (m_hat / (jnp.sqrt(v_hat) + ADAM_EPS) + ADAM_WD * w)
    return delta, m, v


def _adamw(w, gs, m, v, tm, name):
    R, C = w.shape

    def body(*refs):
        w_ref = refs[0]
        g_refs = refs[1:1 + len(gs)]
        m_ref, v_ref, g_out, d_out, m_out, v_out = refs[1 + len(gs):]
        g = None
        for gr in g_refs:
            parts = [gr[i] for i in range(gr.shape[0])] if len(gr.shape) == 3 else [gr[...]]
            for p in parts:
                g = p if g is None else g + p
        d, mn, vn = _adamw_math(w_ref[...], g, m_ref[...], v_ref[...])
        g_out[...] = g
        d_out[...] = d
        m_out[...] = mn
        v_out[...] = vn

    blk = pl.BlockSpec((tm, C), lambda i: (i, 0))
    g_specs = [pl.BlockSpec((g.shape[0], tm, C), lambda i: (0, i, 0)) if g.ndim == 3 else blk for g in gs]
    return pl.pallas_call(
        body, out_shape=tuple(jax.ShapeDtypeStruct((R, C), F32) for _ in range(4)), grid=(R // tm,),
        in_specs=[blk] + g_specs + [blk, blk], out_specs=(blk,) * 4, name=name,
        compiler_params=_cp(("parallel",)))(w, *gs, m, v)


WEIGHTS = ("norm_g", "w_in", "conv_w", "conv_b", "conv_ln_g", "conv_ln_b", "na_q_g", "na_k_g", "na_rpb", "gla_a2_f",
           "gla_ab_f", "gla_a2_b", "gla_ab_b", "gla_o_g", "pool_w", "pool_scale", "w_out")
_REPL = ("norm_g", "conv_b", "conv_ln_g", "conv_ln_b", "na_q_g", "na_k_g", "na_rpb", "gla_ab_f", "gla_ab_b", "gla_o_g",
         "pool_w", "pool_scale")
_SHARD_SMALL = ("conv_w", "gla_a2_f", "gla_a2_b")
_PACK_ROWS = 8 * 128


def _pack(arrs):
    flat = jnp.concatenate([a.reshape(-1) for a in arrs])
    n = -(-flat.shape[0] // _PACK_ROWS) * _PACK_ROWS
    return jnp.pad(flat, (0, n - flat.shape[0])).reshape(-1, 128)


def _unpack(p, shapes):
    flat = p.reshape(-1)
    out, o = [], 0
    for s in shapes:
        n = int(np.prod(s))
        out.append(flat[o:o + n].reshape(s))
        o += n
    return out


def _to_layout(w):
    pad = jnp.zeros(w.shape[:-1] + (NZ - N_IN,), w.dtype)
    return jnp.concatenate([w[..., :5120], w[..., 5152:6176], w[..., 5120:5152], pad], axis=-1)


def _from_layout(w):
    return jnp.concatenate([w[..., :5120], w[..., LR_OFF:LR_OFF + 32], w[..., 5120:LR_OFF]], axis=-1)


def _reduce_gradients(p_a, p_b, small_g, ci):
    two = lambda a: a.reshape(-1, a.shape[-1])
    s_a, s_b, s_small = _sibling_exchange((p_a, p_b), (small_g,), "grad_to_sibling")
    mine = lambda a: lax.dynamic_index_in_dim(a, ci, 0, keepdims=False)
    c_a = _add2(two(mine(p_a)), two(s_a), BF16, 256, "chip_sum_a").reshape(s_a.shape)
    c_b = _add2(two(mine(p_b)), two(s_b), BF16, 256, "chip_sum_b").reshape(s_b.shape)
    c_small = _add2(small_g, s_small, F32, small_g.shape[0], "chip_sum_small")
    r_a, r_b, r_small = _chip_exchange((c_a, c_b), (c_small,), "grad_to_owner")
    own_a = _sum_slots(r_a, "sum_a")
    own_b = _sum_slots(r_b, "sum_b")
    sib_a, sib_b = _sibling_exchange((), (own_a, own_b), "reduced_to_sibling")
    by_layer = lambda own, sib: jnp.where(ci == 0, jnp.stack([own, sib]), jnp.stack([sib, own]))
    return by_layer(own_a, sib_a), by_layer(own_b, sib_b), r_small


def _layer_fwd(l, x, P, S):
    n = f"l{l}_"
    h = _rmsnorm_fwd(x, P["norm_g"], n + "rms_fwd")
    z = _matmul(h, P["w_in"], dims="nn", out_dtype=F32, tm=1024, tn=1280, tk=D_MODEL, name=n + "mm_z")
    yc = _conv_fwd(z, P["conv_w32"], P["conv_b"], S, n + "conv_fwd")
    pre_a = _ln_silu_fwd(yc, P["conv_ln_g"], P["conv_ln_b"], n + "ln_fwd")
    pre_b = _na_fwd(z, P["na_q_g"], P["na_k_g"], P["na_bias"], S, n + "na_fwd")
    bf, ecf = _gla_decay_fwd(z, P["a2_f"], P["gla_ab_f"], False, n + "gla_decay_f")
    bb, ecb = _gla_decay_fwd(z, P["a2_b"], P["gla_ab_b"], True, n + "gla_decay_b")
    of, af, sf = _gla_fwd(z, bf, ecf, S, False, n + "gla_fwd_f")
    ob, ab, sb = _gla_fwd(z, bb, ecb, S, True, n + "gla_fwd_b")
    pre_c = _gla_norm_fwd(of, ob, P["gla_o_g"], n + "gla_norm_fwd")
    pre_d = _pool_fwd(z, P["pool_w_bf"], P["pool_scale"], S, n + "pool_fwd")
    pres = (pre_a, pre_b, pre_c, pre_d)
    y = _gate_fwd(pres, z, n + "gate_fwd")
    out = _matmul(y, P["w_out"], dims="nn", out_dtype=F32, tm=512, tn=2048, tk=D_MODEL, name=n + "mm_out", res=x)
    return out, dict(x=x, h=h, z=z, yc=yc, pres=pres, of=of, af=af, sf=sf, ob=ob, ab=ab, sb=sb, y=y, bf=bf, ecf=ecf, bb=bb, ecb=ecb)


def _layer_bwd(l, dout, dout_bf, sv, P, S):
    n = f"l{l}_"
    z = sv["z"]
    T = z.shape[0]
    dy = _matmul(dout_bf, P["w_out_t"], dims="nn", out_dtype=F32, tm=512, tn=2048, tk=D_MODEL, name=n + "mm_dy")
    d_w_out = _matmul(sv["y"], dout_bf, dims="tn", out_dtype=BF16, tm=1024, tn=2048, tk=512, name=n + "mm_dwout")
    dpa, dpb, dpc, dpd, dga, dgb, dgc, dgd = _gate_bwd(dy, sv["pres"], z, n + "gate_bwd")
    dyc, d_ln_g, d_ln_b = _ln_silu_bwd(sv["yc"], P["conv_ln_g"], P["conv_ln_b"], dpa, n + "ln_bwd")
    dval, dglu, d_cw, d_cb = _conv_bwd(z, P["conv_w32"], dyc, S, n + "conv_bwd")
    dq, dk, dv, dbias, d_qg, d_kg = _na_bwd(z, P["na_q_g"], P["na_k_g"], P["na_bias"], dpb, S, n + "na_bwd")
    d_rpb = _na_rpb_grad(dbias, n + "na_rpb")
    do, d_og = _gla_norm_bwd(sv["of"], sv["ob"], P["gla_o_g"], dpc, n + "gla_norm_bwd")
    part = _gla_bwd(z, sv["bf"], sv["ecf"], sv["af"], sv["sf"], do, None, S, False, n + "gla_bwd_f")
    dcq, dck, dcv, db_b = _gla_bwd(z, sv["bb"], sv["ecb"], sv["ab"], sv["sb"], do, part[:3], S, True, n + "gla_bwd_b")
    dlr, d_a2f, d_abf, d_a2b, d_abb = _gla_decay_bwd(z, P["a2_f"], P["gla_ab_f"], P["a2_b"], P["gla_ab_b"], part[3], db_b,
                                                     n + "gla_decay_bwd")
    dd, d_pw, d_ps = _pool_bwd(z, P["pool_w_bf"], P["pool_scale"], dpd, S, n + "pool_bwd")
    dz = _concat_cols([dval, dglu, dga, dq, dk, dv, dgb, dcq, dck, dcv, dgc, dd, dgd, dlr], NZ, n + "dz_concat")
    dh = _matmul(dz, P["w_in_t"], dims="nn", out_dtype=F32, tm=1024, tn=1024, tk=3200, name=n + "mm_dh")
    d_w_in = _matmul(sv["h"], dz, dims="tn", out_dtype=BF16, tm=1024, tn=1280, tk=1024, name=n + "mm_dwin")
    dx, dx_bf, d_ng = _rmsnorm_bwd(sv["x"], P["norm_g"], dh, dout, n + "rms_bwd")
    grads = dict(norm_g=d_ng[0], w_in=d_w_in, conv_w=d_cw[:CONV_K], conv_b=d_cb[0], conv_ln_g=d_ln_g[0], conv_ln_b=d_ln_b[0],
                 na_q_g=d_qg.reshape(NA_HEADS, NA_DH), na_k_g=d_kg.reshape(NA_HEADS, NA_DH), na_rpb=d_rpb,
                 gla_a2_f=d_a2f[0:GLA_RANK], gla_ab_f=d_abf[0], gla_a2_b=d_a2b[GLA_RANK:2 * GLA_RANK], gla_ab_b=d_abb[0],
                 gla_o_g=d_og.reshape(GLA_HEADS, GLA_DV), pool_w=d_pw, pool_scale=d_ps[0], w_out=d_w_out)
    return dx, dx_bf, grads


def kernel(x, norm_g, w_in, conv_w, conv_b, conv_ln_g, conv_ln_b, na_q_g, na_k_g, na_rpb, gla_a2_f, gla_ab_f, gla_a2_b, gla_ab_b, gla_o_g, pool_w, pool_scale, w_out, loss_target, m_norm_g, m_w_in, m_conv_w, m_conv_b, m_conv_ln_g, m_conv_ln_b, m_na_q_g, m_na_k_g, m_na_rpb, m_gla_a2_f, m_gla_ab_f, m_gla_a2_b, m_gla_ab_b, m_gla_o_g, m_pool_w, m_pool_scale, m_w_out, v_norm_g, v_w_in, v_conv_w, v_conv_b, v_conv_ln_g, v_conv_ln_b, v_na_q_g, v_na_k_g, v_na_rpb, v_gla_a2_f, v_gla_ab_f, v_gla_a2_b, v_gla_ab_b, v_gla_o_g, v_pool_w, v_pool_scale, v_w_out):
    W = dict(norm_g=norm_g, w_in=w_in, conv_w=conv_w, conv_b=conv_b, conv_ln_g=conv_ln_g, conv_ln_b=conv_ln_b, na_q_g=na_q_g,
             na_k_g=na_k_g, na_rpb=na_rpb, gla_a2_f=gla_a2_f, gla_ab_f=gla_ab_f, gla_a2_b=gla_a2_b, gla_ab_b=gla_ab_b,
             gla_o_g=gla_o_g, pool_w=pool_w, pool_scale=pool_scale, w_out=w_out)
    M = dict(norm_g=m_norm_g, w_in=m_w_in, conv_w=m_conv_w, conv_b=m_conv_b, conv_ln_g=m_conv_ln_g, conv_ln_b=m_conv_ln_b,
             na_q_g=m_na_q_g, na_k_g=m_na_k_g, na_rpb=m_na_rpb, gla_a2_f=m_gla_a2_f, gla_ab_f=m_gla_ab_f, gla_a2_b=m_gla_a2_b,
             gla_ab_b=m_gla_ab_b, gla_o_g=m_gla_o_g, pool_w=m_pool_w, pool_scale=m_pool_scale, w_out=m_w_out)
    V = dict(norm_g=v_norm_g, w_in=v_w_in, conv_w=v_conv_w, conv_b=v_conv_b, conv_ln_g=v_conv_ln_g, conv_ln_b=v_conv_ln_b,
             na_q_g=v_na_q_g, na_k_g=v_na_k_g, na_rpb=v_na_rpb, gla_a2_f=v_gla_a2_f, gla_ab_f=v_gla_ab_f, gla_a2_b=v_gla_a2_b,
             gla_ab_b=v_gla_ab_b, gla_o_g=v_gla_o_g, pool_w=v_pool_w, pool_scale=v_pool_scale, w_out=v_w_out)
    E, S, D = x.shape
    T = E * S
    L = DEPTH
    xi, yi, ci = _coords()
    chip = 2 * xi + yi
    cw_sh, a2_sh = conv_w.shape[-1], gla_a2_f.shape[-1]

    small_sh = jnp.concatenate([
        jnp.pad(conv_w, ((0, 0), (0, 1), (0, 0))),
        jnp.pad(gla_a2_f, ((0, 0), (0, 0), (0, 128 - a2_sh))),
        jnp.pad(gla_a2_b, ((0, 0), (0, 0), (0, 128 - a2_sh)))], axis=1)
    g_win, g_wout, g_small = _gather_weights((w_in.astype(BF16), w_out.astype(BF16), small_sh), (4, 2, 1), "gather_weights")
    w_in_full = _to_layout(jnp.transpose(g_win, (1, 2, 0, 3)).reshape(L, D, N_IN))
    w_out_full = jnp.transpose(g_wout, (1, 0, 2, 3)).reshape(L, D, D)
    conv_w_full = jnp.transpose(g_small[:, :, 0:32, :], (1, 2, 0, 3)).reshape(L, 32, 4 * cw_sh)
    a2f_full = jnp.transpose(g_small[:, :, 32:48, :a2_sh], (1, 2, 0, 3)).reshape(L, GLA_RANK, 4 * a2_sh)
    a2b_full = jnp.transpose(g_small[:, :, 48:64, :a2_sh], (1, 2, 0, 3)).reshape(L, GLA_RANK, 4 * a2_sh)

    params = []
    for l in range(L):
        params.append(dict(
            norm_g=norm_g[l][None], w_in=w_in_full[l], w_out=w_out_full[l], w_in_t=w_in_full[l].T, w_out_t=w_out_full[l].T,
            conv_w32=conv_w_full[l], conv_b=conv_b[l][None],
            conv_ln_g=conv_ln_g[l][None], conv_ln_b=conv_ln_b[l][None], na_q_g=na_q_g[l].reshape(1, GROUP_W),
            na_k_g=na_k_g[l].reshape(1, GROUP_W), na_bias=_na_bias(na_rpb[l], f"l{l}_na_bias"),
            a2_f=jnp.zeros((128, _HK), F32).at[0:GLA_RANK].set(a2f_full[l]),
            a2_b=jnp.zeros((128, _HK), F32).at[GLA_RANK:2 * GLA_RANK].set(a2b_full[l]),
            gla_ab_f=gla_ab_f[l][None], gla_ab_b=gla_ab_b[l][None], gla_o_g=gla_o_g[l].reshape(1, GROUP_W),
            pool_w_bf=pool_w[l].astype(BF16), pool_scale=pool_scale[l][None]))

    act = x.reshape(T, D)
    saved = []
    for l in range(L):
        act, sv = _layer_fwd(l, act, params[l], S)
        saved.append(sv)
    dact, dact_bf, loss_loc = _loss_head(act, loss_target.reshape(T, D), "loss_head")
    loss = lax.psum(loss_loc[0, 0], ("x", "y", "c"))
    grads = [None] * L
    for l in reversed(range(L)):
        dact, dact_bf, grads[l] = _layer_bwd(l, dact, dact_bf, saved[l], params[l], S)
    grad_x = dact.reshape(E, S, D)
    G = {k: jnp.stack([grads[l][k] for l in range(L)]) for k in WEIGHTS}

    cols_in, cols_out = N_IN // 4, D
    p_win = jnp.transpose(_from_layout(G["w_in"]).reshape(L, D, 4, cols_in), (0, 2, 1, 3))
    p_wout = G["w_out"].reshape(L, 4, D // 4, D)
    small_names = _REPL + _SHARD_SMALL
    small_g = _pack([G[k] for k in small_names])
    g_in, g_out, r_small = _reduce_gradients(p_win, p_wout, small_g, ci)

    rows_in, rows_out = L * D, L * (D // 4)
    res = {}
    res["w_in"] = [a.reshape(L, D, cols_in) for a in _adamw(
        w_in.reshape(rows_in, cols_in), (g_in.reshape(rows_in, cols_in),), m_w_in.reshape(rows_in, cols_in),
        v_w_in.reshape(rows_in, cols_in), 256, "adamw_w_in")]
    res["w_out"] = [a.reshape(L, D // 4, D) for a in _adamw(
        w_out.reshape(rows_out, cols_out), (g_out.reshape(rows_out, cols_out),), m_w_out.reshape(rows_out, cols_out),
        v_w_out.reshape(rows_out, cols_out), 256, "adamw_w_out")]
    zeros_sh = [jnp.zeros(G[k].shape, F32) for k in _SHARD_SMALL]
    pk = lambda dct: _pack([dct[k] for k in _REPL] + zeros_sh)
    small_res = _adamw(pk(W), (r_small,), pk(M), pk(V), small_g.shape[0], "adamw_small")
    shapes = [G[k].shape for k in small_names]
    unp = [_unpack(a, shapes) for a in small_res]
    for i, k in enumerate(_REPL):
        res[k] = [u[i] for u in unp]
    g_sh = []
    for i, k in enumerate(_SHARD_SMALL):
        gfull = unp[0][len(_REPL) + i]
        wdt = W[k].shape[-1]
        g_sh.append(lax.dynamic_slice_in_dim(gfull, chip * wdt, wdt, axis=2))
    g_sh_p = _pack(g_sh)
    sh_res = _adamw(_pack([W[k] for k in _SHARD_SMALL]), (g_sh_p,), _pack([M[k] for k in _SHARD_SMALL]),
                    _pack([V[k] for k in _SHARD_SMALL]), g_sh_p.shape[0], "adamw_shard_small")
    shapes2 = [W[k].shape for k in _SHARD_SMALL]
    unp2 = [_unpack(a, shapes2) for a in sh_res]
    for i, k in enumerate(_SHARD_SMALL):
        res[k] = [u[i] for u in unp2]

    outs = [loss, grad_x]
    for j in range(4):
        outs += [res[k][j] for k in WEIGHTS]
    return tuple(outs)
```

```python
import functools

import numpy as np
import jax
import jax.numpy as jnp
from jax import lax
from jax.experimental import pallas as pl
from jax.experimental.pallas import tpu as pltpu

F32 = jnp.float32
BF16 = jnp.bfloat16
HI = lax.Precision.HIGHEST
MESH = pl.DeviceIdType.MESH

EPS = 1e-6
D_MODEL = 2048
GROUP_W = 512
SEQ = 2048
DEPTH = 2
N_IN = 6176
GRID_W = 64
CONV_K = 31
NA_HEADS = 8
NA_DH = 64
NA_ROWS = 8
NA_COLS = 16
GLA_HEADS = 4
GLA_DK = 64
GLA_DV = 128
GLA_RANK = 16
GLA_TAU = 16.0
CHUNK = 64
POOL_WINDOWS = (2, 4, 8, 16)
ADAM_LR, ADAM_B1, ADAM_B2, ADAM_EPS, ADAM_WD, ADAM_STEP = 0.001, 0.9, 0.999, 1e-08, 0.01, 10

A_VAL, A_GLU, A_GATE = 0, 512, 1024
B_Q, B_K, B_V, B_GATE = 1536, 2048, 2560, 3072
C_Q, C_K, C_V, C_GATE = 3584, 3840, 4096, 4608
D_VAL, D_GATE = 5120, 5632
LR_OFF = 6144
NZ = 6400
NEG = -1e30
VMEM_LIMIT = 56 * 1024 * 1024


def _cp(sem=None):
    return pltpu.CompilerParams(dimension_semantics=sem, vmem_limit_bytes=VMEM_LIMIT)


def _sigmoid(x):
    return 1.0 / (1.0 + jnp.exp(-x))


def _silu(x):
    return x * _sigmoid(x)


def _dsilu(x):
    s = _sigmoid(x)
    return s * (1.0 + x * (1.0 - s))


def _matmul(a, b, *, dims, out_dtype, tm, tn, tk, name, res=None):
    if dims == "nn":
        (M, K), N = a.shape, b.shape[1]
    elif dims == "nt":
        (M, K), N = a.shape, b.shape[0]
    else:
        (K, M), N = a.shape, b.shape[1]
    tm, tn, tk = min(tm, M), min(tn, N), min(tk, K)
    nk = K // tk
    assert M % tm == 0 and N % tn == 0 and K % tk == 0, (M, N, K, tm, tn, tk)
    dn = {"nn": (((1,), (0,)), ((), ())), "nt": (((1,), (1,)), ((), ())), "tn": (((0,), (0,)), ((), ()))}[dims]
    if dims == "tn":
        a_spec = pl.BlockSpec((tk, tm), lambda i, j, k: (k, i))
    else:
        a_spec = pl.BlockSpec((tm, tk), lambda i, j, k: (i, k))
    if dims == "nt":
        b_spec = pl.BlockSpec((tn, tk), lambda i, j, k: (j, k))
    else:
        b_spec = pl.BlockSpec((tk, tn), lambda i, j, k: (k, j))
    o_spec = pl.BlockSpec((tm, tn), lambda i, j, k: (i, j))
    has_res = res is not None

    def body(*refs):
        if has_res:
            a_ref, b_ref, r_ref, o_ref, acc = refs
        else:
            a_ref, b_ref, o_ref, acc = refs
        k = pl.program_id(2)

        @pl.when(k == 0)
        def _():
            acc[...] = jnp.zeros_like(acc)

        acc[...] += lax.dot_general(a_ref[...], b_ref[...], dn, preferred_element_type=F32)

        @pl.when(k == nk - 1)
        def _():
            r = acc[...]
            if has_res:
                r = r + r_ref[...]
            o_ref[...] = r.astype(o_ref.dtype)

    in_specs = [a_spec, b_spec] + ([o_spec] if has_res else [])
    args = (a, b) + ((res,) if has_res else ())
    return pl.pallas_call(
        body, out_shape=jax.ShapeDtypeStruct((M, N), out_dtype), grid=(M // tm, N // tn, nk),
        in_specs=in_specs, out_specs=o_spec, scratch_shapes=[pltpu.VMEM((tm, tn), F32)],
        name=name, compiler_params=_cp(("parallel", "parallel", "arbitrary")))(*args)


def _concat_cols(pieces, width, name):
    T = pieces[0].shape[0]
    tm = min(512, T)
    dt = pieces[0].dtype
    offs = np.cumsum([0] + [p.shape[1] for p in pieces])

    def body(*refs):
        o_ref = refs[-1]
        for p_ref, a, b in zip(refs[:-1], offs[:-1], offs[1:]):
            o_ref[:, a:b] = p_ref[...]
        if offs[-1] < width:
            o_ref[:, offs[-1]:width] = jnp.zeros((tm, width - offs[-1]), dt)

    return pl.pallas_call(
        body, out_shape=jax.ShapeDtypeStruct((T, width), dt), grid=(T // tm,),
        in_specs=[pl.BlockSpec((tm, p.shape[1]), lambda i: (i, 0)) for p in pieces],
        out_specs=pl.BlockSpec((tm, width), lambda i: (i, 0)), name=name, compiler_params=_cp(("parallel",)))(*pieces)


def _rmsnorm_fwd(x, g, name):
    T, D = x.shape
    tm = 256

    def body(x_ref, g_ref, h_ref):
        xv = x_ref[...]
        r = lax.rsqrt(jnp.mean(xv * xv, axis=-1, keepdims=True) + EPS)
        h_ref[...] = (xv * r * g_ref[...]).astype(h_ref.dtype)

    return pl.pallas_call(
        body, out_shape=jax.ShapeDtypeStruct((T, D), BF16), grid=(T // tm,),
        in_specs=[pl.BlockSpec((tm, D), lambda i: (i, 0)), pl.BlockSpec((1, D), lambda i: (0, 0))],
        out_specs=pl.BlockSpec((tm, D), lambda i: (i, 0)), name=name, compiler_params=_cp(("parallel",)))(x, g)


def _rmsnorm_bwd(x, g, dh, dres, name):
    T, D = x.shape
    tm = 256

    def body(x_ref, g_ref, dh_ref, dres_ref, dx_ref, dxb_ref, dg_ref):
        xv = x_ref[...]
        r = lax.rsqrt(jnp.mean(xv * xv, axis=-1, keepdims=True) + EPS)
        xh = xv * r
        dh_v = dh_ref[...]
        dxh = dh_v * g_ref[...]
        dx = r * (dxh - xh * jnp.mean(dxh * xh, axis=-1, keepdims=True)) + dres_ref[...]
        dx_ref[...] = dx
        dxb_ref[...] = dx.astype(BF16)

        @pl.when(pl.program_id(0) == 0)
        def _():
            dg_ref[...] = jnp.zeros_like(dg_ref)

        dg_ref[...] += jnp.sum(dh_v * xh, axis=0, keepdims=True)

    row = pl.BlockSpec((tm, D), lambda i: (i, 0))
    vec = pl.BlockSpec((1, D), lambda i: (0, 0))
    return pl.pallas_call(
        body, out_shape=(jax.ShapeDtypeStruct((T, D), F32), jax.ShapeDtypeStruct((T, D), BF16), jax.ShapeDtypeStruct((1, D), F32)),
        grid=(T // tm,), in_specs=[row, vec, row, row], out_specs=(row, row, vec), name=name,
        compiler_params=_cp(("arbitrary",)))(x, g, dh, dres)


def _loss_head(y, target, name):
    T, D = y.shape
    tm = 256

    def body(y_ref, t_ref, d_ref, db_ref, l_ref):
        e = y_ref[...] - t_ref[...]
        d = e * (1.0 / D)
        d_ref[...] = d
        db_ref[...] = d.astype(BF16)

        @pl.when(pl.program_id(0) == 0)
        def _():
            l_ref[...] = jnp.zeros_like(l_ref)

        row = jnp.sum(e * e, axis=-1, keepdims=True) * (0.5 / D)
        l_ref[...] += jnp.sum(row, axis=0, keepdims=True)

    row = pl.BlockSpec((tm, D), lambda i: (i, 0))
    return pl.pallas_call(
        body, out_shape=(jax.ShapeDtypeStruct((T, D), F32), jax.ShapeDtypeStruct((T, D), BF16),
                         jax.ShapeDtypeStruct((1, 1), F32)), grid=(T // tm,),
        in_specs=[row, row], out_specs=(row, row, pl.BlockSpec((1, 1), lambda i: (0, 0))),
        name=name, compiler_params=_cp(("arbitrary",)))(y, target)


_GATE_COLS = (A_GATE // GROUP_W, B_GATE // GROUP_W, C_GATE // GROUP_W, D_GATE // GROUP_W)


def _gate_fwd(pres, z, name):
    T = z.shape[0]
    tm = 256

    def body(pa, pb, pc, pd, ga, gb, gc, gd, y_ref):
        for n, (p, g) in enumerate(((pa, ga), (pb, gb), (pc, gc), (pd, gd))):
            y_ref[:, n * GROUP_W:(n + 1) * GROUP_W] = (p[...] * _silu(g[...])).astype(BF16)

    pre_spec = pl.BlockSpec((tm, GROUP_W), lambda i: (i, 0))
    gate_specs = [pl.BlockSpec((tm, GROUP_W), functools.partial(lambda i, c: (i, c), c=c)) for c in _GATE_COLS]
    return pl.pallas_call(
        body, out_shape=jax.ShapeDtypeStruct((T, 4 * GROUP_W), BF16), grid=(T // tm,),
        in_specs=[pre_spec] * 4 + gate_specs, out_specs=pl.BlockSpec((tm, 4 * GROUP_W), lambda i: (i, 0)),
        name=name, compiler_params=_cp(("parallel",)))(*pres, z, z, z, z)


def _gate_bwd(dy, pres, z, name):
    T = z.shape[0]
    tm = 256

    def body(dy_ref, pa, pb, pc, pd, ga, gb, gc, gd, dpa, dpb, dpc, dpd, dga, dgb, dgc, dgd):
        for n, (p, g, dp, dg) in enumerate(((pa, ga, dpa, dga), (pb, gb, dpb, dgb), (pc, gc, dpc, dgc), (pd, gd, dpd, dgd))):
            d = dy_ref[:, n * GROUP_W:(n + 1) * GROUP_W]
            gv = g[...]
            dp[...] = d * _silu(gv)
            dg[...] = (d * p[...] * _dsilu(gv)).astype(BF16)

    pre_spec = pl.BlockSpec((tm, GROUP_W), lambda i: (i, 0))
    gate_specs = [pl.BlockSpec((tm, GROUP_W), functools.partial(lambda i, c: (i, c), c=c)) for c in _GATE_COLS]
    outs = tuple([jax.ShapeDtypeStruct((T, GROUP_W), F32)] * 4 + [jax.ShapeDtypeStruct((T, GROUP_W), BF16)] * 4)
    return pl.pallas_call(
        body, out_shape=outs, grid=(T // tm,),
        in_specs=[pl.BlockSpec((tm, 4 * GROUP_W), lambda i: (i, 0))] + [pre_spec] * 4 + gate_specs,
        out_specs=tuple([pre_spec] * 8), name=name, compiler_params=_cp(("parallel",)))(dy, *pres, z, z, z, z)


_PAD = 16
_RC = 256


def _conv_fwd(z, conv_w32, conv_b, S, name):
    T = z.shape[0]
    E = T // S
    LW = 128

    def body(val_ref, glu_ref, w_ref, b_ref, y_ref, upad):
        upad[0:_PAD, :] = jnp.zeros((_PAD, LW), F32)
        upad[_PAD + S:_PAD + S + _PAD, :] = jnp.zeros((_PAD, LW), F32)
        upad[_PAD:_PAD + S, :] = val_ref[...] * _sigmoid(glu_ref[...])
        for r in range(S // _RC):
            acc = jnp.broadcast_to(b_ref[...], (_RC, LW))
            for k in range(CONV_K):
                st = r * _RC + k + 1
                acc = acc + upad[st:st + _RC, :] * w_ref[k:k + 1, :]
            y_ref[r * _RC:(r + 1) * _RC, :] = acc

    return pl.pallas_call(
        body, out_shape=jax.ShapeDtypeStruct((T, GROUP_W), F32), grid=(E, GROUP_W // LW),
        in_specs=[pl.BlockSpec((S, LW), lambda e, j: (e, A_VAL // LW + j)),
                  pl.BlockSpec((S, LW), lambda e, j: (e, A_GLU // LW + j)),
                  pl.BlockSpec((32, LW), lambda e, j: (0, j)),
                  pl.BlockSpec((1, LW), lambda e, j: (0, j))],
        out_specs=pl.BlockSpec((S, LW), lambda e, j: (e, j)),
        scratch_shapes=[pltpu.VMEM((S + 2 * _PAD, LW), F32)],
        name=name, compiler_params=_cp(("parallel", "parallel")))(z, z, conv_w32, conv_b)


def _conv_bwd(z, conv_w32, dyc, S, name):
    T = z.shape[0]
    E = T // S
    LW = 128

    def body(val_ref, glu_ref, w_ref, dy_ref, dval_ref, dglu_ref, dw_ref, db_ref, upad, dpad):
        e = pl.program_id(1)
        zeros = jnp.zeros((_PAD, LW), F32)
        upad[0:_PAD, :] = zeros
        upad[_PAD + S:_PAD + S + _PAD, :] = zeros
        dpad[0:_PAD, :] = zeros
        dpad[_PAD + S:_PAD + S + _PAD, :] = zeros
        upad[_PAD:_PAD + S, :] = val_ref[...] * _sigmoid(glu_ref[...])
        dpad[_PAD:_PAD + S, :] = dy_ref[...]

        @pl.when(e == 0)
        def _():
            dw_ref[...] = jnp.zeros_like(dw_ref)
            db_ref[...] = jnp.zeros_like(db_ref)

        db_ref[...] += jnp.sum(dy_ref[...], axis=0, keepdims=True)
        for r in range(S // _RC):
            dyr = dy_ref[r * _RC:(r + 1) * _RC, :]
            du = jnp.zeros((_RC, LW), F32)
            for k in range(CONV_K):
                st = r * _RC + k + 1
                dw_ref[k:k + 1, :] += jnp.sum(dyr * upad[st:st + _RC, :], axis=0, keepdims=True)
                sd = r * _RC + (CONV_K - 1 - k) + 1
                du = du + dpad[sd:sd + _RC, :] * w_ref[k:k + 1, :]
            sl = slice(r * _RC, (r + 1) * _RC)
            val = val_ref[sl, :]
            sg = _sigmoid(glu_ref[sl, :])
            dval_ref[sl, :] = (du * sg).astype(BF16)
            dglu_ref[sl, :] = (du * val * sg * (1.0 - sg)).astype(BF16)

    blk = pl.BlockSpec((S, LW), lambda j, e: (e, j))
    return pl.pallas_call(
        body, out_shape=(jax.ShapeDtypeStruct((T, GROUP_W), BF16), jax.ShapeDtypeStruct((T, GROUP_W), BF16),
                         jax.ShapeDtypeStruct((32, GROUP_W), F32), jax.ShapeDtypeStruct((1, GROUP_W), F32)),
        grid=(GROUP_W // LW, E),
        in_specs=[pl.BlockSpec((S, LW), lambda j, e: (e, A_VAL // LW + j)),
                  pl.BlockSpec((S, LW), lambda j, e: (e, A_GLU // LW + j)),
                  pl.BlockSpec((32, LW), lambda j, e: (0, j)), blk],
        out_specs=(blk, blk, pl.BlockSpec((32, LW), lambda j, e: (0, j)), pl.BlockSpec((1, LW), lambda j, e: (0, j))),
        scratch_shapes=[pltpu.VMEM((S + 2 * _PAD, LW), F32), pltpu.VMEM((S + 2 * _PAD, LW), F32)],
        name=name, compiler_params=_cp(("parallel", "arbitrary")))(z, z, conv_w32, dyc)


def _ln_silu_fwd(yc, g, b, name):
    T, C = yc.shape
    tm = 256

    def body(y_ref, g_ref, b_ref, o_ref):
        y = y_ref[...]
        mu = jnp.mean(y, axis=-1, keepdims=True)
        yc_ = y - mu
        r = lax.rsqrt(jnp.mean(yc_ * yc_, axis=-1, keepdims=True) + EPS)
        o_ref[...] = _silu(yc_ * r * g_ref[...] + b_ref[...])

    row = pl.BlockSpec((tm, C), lambda i: (i, 0))
    vec = pl.BlockSpec((1, C), lambda i: (0, 0))
    return pl.pallas_call(body, out_shape=jax.ShapeDtypeStruct((T, C), F32), grid=(T // tm,),
                          in_specs=[row, vec, vec], out_specs=row, name=name, compiler_params=_cp(("parallel",)))(yc, g, b)


def _ln_silu_bwd(yc, g, b, dpre, name):
    T, C = yc.shape
    tm = 256

    def body(y_ref, g_ref, b_ref, dp_ref, dy_ref, dg_ref, db_ref):
        y = y_ref[...]
        mu = jnp.mean(y, axis=-1, keepdims=True)
        yc_ = y - mu
        r = lax.rsqrt(jnp.mean(yc_ * yc_, axis=-1, keepdims=True) + EPS)
        xh = yc_ * r
        gv = g_ref[...]
        dln = dp_ref[...] * _dsilu(xh * gv + b_ref[...])
        dxh = dln * gv
        dy_ref[...] = r * (dxh - jnp.mean(dxh, axis=-1, keepdims=True) - xh * jnp.mean(dxh * xh, axis=-1, keepdims=True))

        @pl.when(pl.program_id(0) == 0)
        def _():
            dg_ref[...] = jnp.zeros_like(dg_ref)
            db_ref[...] = jnp.zeros_like(db_ref)

        dg_ref[...] += jnp.sum(dln * xh, axis=0, keepdims=True)
        db_ref[...] += jnp.sum(dln, axis=0, keepdims=True)

    row = pl.BlockSpec((tm, C), lambda i: (i, 0))
    vec = pl.BlockSpec((1, C), lambda i: (0, 0))
    return pl.pallas_call(
        body, out_shape=(jax.ShapeDtypeStruct((T, C), F32), jax.ShapeDtypeStruct((1, C), F32), jax.ShapeDtypeStruct((1, C), F32)),
        grid=(T // tm,), in_specs=[row, vec, vec, row], out_specs=(row, vec, vec), name=name,
        compiler_params=_cp(("arbitrary",)))(yc, g, b, dpre)


def _pool_counts(S, w, rows0, n):
    t = (lax.broadcasted_iota(jnp.int32, (n, 1), 0) + rows0)
    lo = jnp.maximum(t - w // 2, 0)
    hi = jnp.minimum(t + w // 2, S)
    return (hi - lo).astype(F32)


def _pool_fwd(z, pool_w, pool_scale, S, name):
    T = z.shape[0]
    E = T // S
    CG = 128

    def body(u_ref, w_ref, s_ref, o_ref, upad, dif):
        zeros = jnp.zeros((_PAD, GROUP_W), F32)
        upad[0:_PAD, :] = zeros
        upad[_PAD + S:_PAD + S + _PAD, :] = zeros
        upad[_PAD:_PAD + S, :] = u_ref[...]
        for gi, w in enumerate(POOL_WINDOWS):
            ls = slice(gi * CG, (gi + 1) * CG)
            for r in range(S // _RC):
                acc = jnp.zeros((_RC, CG), F32)
                for j in range(-(w // 2), w // 2):
                    st = _PAD + r * _RC + j
                    acc = acc + upad[st:st + _RC, ls]
                cnt = _pool_counts(S, w, r * _RC, _RC)
                dif[r * _RC:(r + 1) * _RC, :] = (acc / cnt - u_ref[r * _RC:(r + 1) * _RC, ls]).astype(BF16)
            yp = jnp.dot(dif[...], w_ref[gi], preferred_element_type=F32)
            o_ref[:, ls] = yp * s_ref[:, ls]

    return pl.pallas_call(
        body, out_shape=jax.ShapeDtypeStruct((T, GROUP_W), F32), grid=(E,),
        in_specs=[pl.BlockSpec((S, GROUP_W), lambda e: (e, D_VAL // GROUP_W)),
                  pl.BlockSpec((4, CG, CG), lambda e: (0, 0, 0)),
                  pl.BlockSpec((1, GROUP_W), lambda e: (0, 0))],
        out_specs=pl.BlockSpec((S, GROUP_W), lambda e: (e, 0)),
        scratch_shapes=[pltpu.VMEM((S + 2 * _PAD, GROUP_W), F32), pltpu.VMEM((S, CG), BF16)],
        name=name, compiler_params=_cp(("parallel",)))(z, pool_w, pool_scale)


def _pool_bwd(z, pool_w, pool_scale, dpre, S, name):
    T = z.shape[0]
    E = T // S
    CG = 128

    def body(u_ref, w_ref, s_ref, dp_ref, du_ref, dw_ref, ds_ref, upad, dif, qpad):
        zeros = jnp.zeros((_PAD, GROUP_W), F32)
        upad[0:_PAD, :] = zeros
        upad[_PAD + S:_PAD + S + _PAD, :] = zeros
        upad[_PAD:_PAD + S, :] = u_ref[...]
        zc = jnp.zeros((_PAD, CG), F32)
        qpad[0:_PAD, :] = zc
        qpad[_PAD + S:_PAD + S + _PAD, :] = zc

        @pl.when(pl.program_id(0) == 0)
        def _():
            dw_ref[...] = jnp.zeros_like(dw_ref)
            ds_ref[...] = jnp.zeros_like(ds_ref)

        for gi, w in enumerate(POOL_WINDOWS):
            ls = slice(gi * CG, (gi + 1) * CG)
            for r in range(S // _RC):
                acc = jnp.zeros((_RC, CG), F32)
                for j in range(-(w // 2), w // 2):
                    st = _PAD + r * _RC + j
                    acc = acc + upad[st:st + _RC, ls]
                cnt = _pool_counts(S, w, r * _RC, _RC)
                dif[r * _RC:(r + 1) * _RC, :] = (acc / cnt - u_ref[r * _RC:(r + 1) * _RC, ls]).astype(BF16)
            dp = dp_ref[:, ls]
            yp = jnp.dot(dif[...], w_ref[gi], preferred_element_type=F32)
            ds_ref[:, ls] += jnp.sum(dp * yp, axis=0, keepdims=True)
            dys = (dp * s_ref[:, ls]).astype(BF16)
            dw_ref[gi] += lax.dot_general(dif[...], dys, (((0,), (0,)), ((), ())), preferred_element_type=F32)
            dm = lax.dot_general(dys, w_ref[gi], (((1,), (1,)), ((), ())), preferred_element_type=F32)
            for r in range(S // _RC):
                cnt = _pool_counts(S, w, r * _RC, _RC)
                qpad[_PAD + r * _RC:_PAD + (r + 1) * _RC, :] = dm[r * _RC:(r + 1) * _RC, :] / cnt
            for r in range(S // _RC):
                acc = -dm[r * _RC:(r + 1) * _RC, :]
                for j in range(-(w // 2) + 1, w // 2 + 1):
                    st = _PAD + r * _RC + j
                    acc = acc + qpad[st:st + _RC, :]
                du_ref[r * _RC:(r + 1) * _RC, ls] = acc.astype(BF16)

    return pl.pallas_call(
        body, out_shape=(jax.ShapeDtypeStruct((T, GROUP_W), BF16), jax.ShapeDtypeStruct((4, CG, CG), F32),
                         jax.ShapeDtypeStruct((1, GROUP_W), F32)), grid=(E,),
        in_specs=[pl.BlockSpec((S, GROUP_W), lambda e: (e, D_VAL // GROUP_W)),
                  pl.BlockSpec((4, CG, CG), lambda e: (0, 0, 0)),
                  pl.BlockSpec((1, GROUP_W), lambda e: (0, 0)),
                  pl.BlockSpec((S, GROUP_W), lambda e: (e, 0))],
        out_specs=(pl.BlockSpec((S, GROUP_W), lambda e: (e, 0)), pl.BlockSpec((4, CG, CG), lambda e: (0, 0, 0)),
                   pl.BlockSpec((1, GROUP_W), lambda e: (0, 0))),
        scratch_shapes=[pltpu.VMEM((S + 2 * _PAD, GROUP_W), F32), pltpu.VMEM((S, CG), BF16),
                        pltpu.VMEM((S + 2 * _PAD, CG), F32)],
        name=name, compiler_params=_cp(("arbitrary",)))(z, pool_w, pool_scale, dpre)


def _na_tables():
    d = np.arange(NA_ROWS)[:, None]
    kr = np.arange(NA_ROWS)[None, :]
    ro = kr - d + (NA_ROWS - 1)
    qc = np.arange(GRID_W)[:, None]
    kc = np.arange(GRID_W)[None, :]
    cs = np.clip(qc - NA_COLS // 2, 0, GRID_W - NA_COLS)
    valid = (kc >= cs) & (kc < cs + NA_COLS)
    co = np.clip(kc - qc + (NA_COLS - 1), 0, 2 * NA_COLS - 2)
    return ro, co, valid


def _na_onehots():
    ro, co, valid = _na_tables()
    e_np = np.zeros((GRID_W, GRID_W, 128), np.float32)
    qi, ki = np.nonzero(valid)
    e_np[qi, ki, co[qi, ki]] = 1.0
    a_np = np.zeros((16, NA_ROWS * NA_ROWS), np.float32)
    a_np[ro.reshape(-1), np.arange(NA_ROWS * NA_ROWS)] = 1.0
    mask = np.where(valid, 0.0, NEG).astype(np.float32).reshape(1, GRID_W * GRID_W)
    return e_np.reshape(GRID_W * GRID_W, 128), a_np, mask


def _na_bias(rpb, name):
    e_np, a_np, mask = _na_onehots()
    H = NA_HEADS
    rp = jnp.pad(rpb, ((0, 0), (0, 1), (0, 128 - rpb.shape[2])))

    def body(r_ref, e_ref, at_ref, m_ref, o_ref):
        t = jnp.dot(at_ref[...], r_ref[0], precision=HI, preferred_element_type=F32)
        o_ref[0] = lax.dot_general(t, e_ref[...], (((1,), (1,)), ((), ())), precision=HI,
                                   preferred_element_type=F32) + m_ref[...]

    out = pl.pallas_call(
        body, out_shape=jax.ShapeDtypeStruct((H, NA_ROWS * NA_ROWS, GRID_W * GRID_W), F32), grid=(H,),
        in_specs=[pl.BlockSpec((1, 16, 128), lambda h: (h, 0, 0)),
                  pl.BlockSpec((GRID_W * GRID_W, 128), lambda h: (0, 0)),
                  pl.BlockSpec((NA_ROWS * NA_ROWS, 16), lambda h: (0, 0)),
                  pl.BlockSpec((1, GRID_W * GRID_W), lambda h: (0, 0))],
        out_specs=pl.BlockSpec((1, NA_ROWS * NA_ROWS, GRID_W * GRID_W), lambda h: (h, 0, 0)),
        name=name, compiler_params=_cp(("parallel",)))(rp, jnp.asarray(e_np), jnp.asarray(a_np.T), jnp.asarray(mask))
    t = out.reshape(H, NA_ROWS, NA_ROWS, GRID_W, GRID_W)
    return jnp.transpose(t, (0, 1, 3, 2, 4)).reshape(H, NA_ROWS, GRID_W, NA_ROWS * GRID_W)


def _seg_mean_matrix(width, seg):
    i = np.arange(width)
    return jnp.asarray((i[:, None] // seg == i[None, :] // seg).astype(np.float32) / seg)


def _na_fwd(z, qg, kg, bias, S, name):
    T = z.shape[0]
    E = T // S
    rows = S // GRID_W
    WIN = NA_ROWS * GRID_W
    seg = _seg_mean_matrix(128, NA_DH)

    def body(q_ref, k_ref, v_ref, qg_ref, kg_ref, bias_ref, seg_ref, o_ref, qs, ks, vs, s_all, p_all):
        for c in range(S // _RC):
            sl = slice(c * _RC, (c + 1) * _RC)
            q = q_ref[sl, :]
            k = k_ref[sl, :]
            qn = q * lax.rsqrt(jnp.dot(q * q, seg_ref[...], precision=HI, preferred_element_type=F32) + EPS) * qg_ref[...]
            kn = k * lax.rsqrt(jnp.dot(k * k, seg_ref[...], precision=HI, preferred_element_type=F32) + EPS) * kg_ref[...]
            v = v_ref[sl, :]
            for hh in range(2):
                ls = slice(hh * NA_DH, (hh + 1) * NA_DH)
                qs[hh, sl, :] = qn[:, ls].astype(BF16)
                ks[hh, sl, :] = kn[:, ls].astype(BF16)
                vs[hh, sl, :] = v[:, ls].astype(BF16)
        def where(r):
            rs = jnp.clip(r - NA_ROWS // 2, 0, rows - NA_ROWS)
            return rs, pl.multiple_of(r * GRID_W, GRID_W), pl.multiple_of(rs * GRID_W, GRID_W)

        def scores(r, carry):
            rs, q0, k0 = where(r)
            for hh in range(2):
                s = lax.dot_general(qs[hh, pl.ds(q0, GRID_W), :], ks[hh, pl.ds(k0, WIN), :], (((1,), (1,)), ((), ())),
                                    preferred_element_type=F32) * (NA_DH ** -0.5)
                s_all[hh, pl.ds(q0, GRID_W), :] = s + bias_ref[hh, r - rs]
            return carry
        lax.fori_loop(0, rows, scores, 0, unroll=4)

        def soft(r, carry):
            _, q0, _ = where(r)
            for hh in range(2):
                s = s_all[hh, pl.ds(q0, GRID_W), :]
                p = jnp.exp(s - jnp.max(s, axis=-1, keepdims=True))
                p_all[hh, pl.ds(q0, GRID_W), :] = (p * (1.0 / jnp.sum(p, axis=-1, keepdims=True))).astype(BF16)
            return carry
        lax.fori_loop(0, rows, soft, 0, unroll=2)

        def outp(r, carry):
            _, q0, k0 = where(r)
            outs = [jnp.dot(p_all[hh, pl.ds(q0, GRID_W), :], vs[hh, pl.ds(k0, WIN), :], preferred_element_type=F32)
                    for hh in range(2)]
            o_ref[pl.ds(q0, GRID_W), :] = jnp.concatenate(outs, axis=1)
            return carry
        lax.fori_loop(0, rows, outp, 0, unroll=4)

    LW = 128
    return pl.pallas_call(
        body, out_shape=jax.ShapeDtypeStruct((T, GROUP_W), F32), grid=(E, GROUP_W // LW),
        in_specs=[pl.BlockSpec((S, LW), lambda e, j: (e, B_Q // LW + j)),
                  pl.BlockSpec((S, LW), lambda e, j: (e, B_K // LW + j)),
                  pl.BlockSpec((S, LW), lambda e, j: (e, B_V // LW + j)),
                  pl.BlockSpec((1, LW), lambda e, j: (0, j)),
                  pl.BlockSpec((1, LW), lambda e, j: (0, j)),
                  pl.BlockSpec((2, NA_ROWS, GRID_W, WIN), lambda e, j: (j, 0, 0, 0)),
                  pl.BlockSpec((LW, LW), lambda e, j: (0, 0))],
        out_specs=pl.BlockSpec((S, LW), lambda e, j: (e, j)),
        scratch_shapes=[pltpu.VMEM((2, S, NA_DH), BF16)] * 3 + [pltpu.VMEM((2, S, WIN), F32), pltpu.VMEM((2, S, WIN), BF16)],
        name=name, compiler_params=_cp(("parallel", "parallel")))(z, z, z, qg, kg, bias, seg)


def _na_bwd(z, qg, kg, bias, do, S, name):
    T = z.shape[0]
    E = T // S
    rows = S // GRID_W
    WIN = NA_ROWS * GRID_W
    seg = _seg_mean_matrix(128, NA_DH)
    SC = NA_DH ** -0.5

    def body(q_ref, k_ref, v_ref, qg_ref, kg_ref, bias_ref, seg_ref, do_ref,
             dq_ref, dk_ref, dv_ref, dbias_ref, dqg_ref, dkg_ref, qs, ks, vs, dos, dqn, dkn, dvs, akt, avt,
             s_all, dp_all, p_all, ds_all):
        e = pl.program_id(1)

        @pl.when(e == 0)
        def _():
            dbias_ref[...] = jnp.zeros_like(dbias_ref)
            dqg_ref[...] = jnp.zeros_like(dqg_ref)
            dkg_ref[...] = jnp.zeros_like(dkg_ref)

        for c in range(S // _RC):
            sl = slice(c * _RC, (c + 1) * _RC)
            q = q_ref[sl, :]
            k = k_ref[sl, :]
            qn = q * lax.rsqrt(jnp.dot(q * q, seg_ref[...], precision=HI, preferred_element_type=F32) + EPS) * qg_ref[...]
            kn = k * lax.rsqrt(jnp.dot(k * k, seg_ref[...], precision=HI, preferred_element_type=F32) + EPS) * kg_ref[...]
            v = v_ref[sl, :]
            dd = do_ref[sl, :]
            for hh in range(2):
                ls = slice(hh * NA_DH, (hh + 1) * NA_DH)
                qs[hh, sl, :] = qn[:, ls].astype(BF16)
                ks[hh, sl, :] = kn[:, ls].astype(BF16)
                vs[hh, sl, :] = v[:, ls].astype(BF16)
                dos[hh, sl, :] = dd[:, ls].astype(BF16)
        akt[...] = jnp.zeros_like(akt)
        avt[...] = jnp.zeros_like(avt)

        def where(r):
            rs = jnp.clip(r - NA_ROWS // 2, 0, rows - NA_ROWS)
            return rs, pl.multiple_of(r * GRID_W, GRID_W), pl.multiple_of(rs * GRID_W, GRID_W)

        for hh in range(2):
            ls = slice(hh * NA_DH, (hh + 1) * NA_DH)

            def products(r, carry, hh=hh):
                rs, q0, k0 = where(r)
                s = lax.dot_general(qs[hh, pl.ds(q0, GRID_W), :], ks[hh, pl.ds(k0, WIN), :], (((1,), (1,)), ((), ())),
                                    preferred_element_type=F32) * SC
                s_all[pl.ds(q0, GRID_W), :] = s + bias_ref[hh, r - rs]
                dp_all[pl.ds(q0, GRID_W), :] = lax.dot_general(dos[hh, pl.ds(q0, GRID_W), :], vs[hh, pl.ds(k0, WIN), :],
                                                               (((1,), (1,)), ((), ())), preferred_element_type=F32)
                return carry
            lax.fori_loop(0, rows, products, 0, unroll=4)

            def soft(r, carry, hh=hh):
                rs, q0, _ = where(r)
                s = s_all[pl.ds(q0, GRID_W), :]
                p = jnp.exp(s - jnp.max(s, axis=-1, keepdims=True))
                p = p * (1.0 / jnp.sum(p, axis=-1, keepdims=True))
                dp = dp_all[pl.ds(q0, GRID_W), :]
                ds = p * (dp - jnp.sum(p * dp, axis=-1, keepdims=True))
                dbias_ref[hh, r - rs] += ds
                p_all[pl.ds(q0, GRID_W), :] = p.astype(BF16)
                ds_all[pl.ds(q0, GRID_W), :] = ds.astype(BF16)
                return carry
            lax.fori_loop(0, rows, soft, 0, unroll=2)

            def grads(r, carry, hh=hh, ls=ls):
                rs, q0, k0 = where(r)
                par = rs % 2
                t0 = (rs + par) // 2
                qr = qs[hh, pl.ds(q0, GRID_W), :]
                dor = dos[hh, pl.ds(q0, GRID_W), :]
                dsb = ds_all[pl.ds(q0, GRID_W), :]
                dqn[pl.ds(q0, GRID_W), ls] = jnp.dot(dsb, ks[hh, pl.ds(k0, WIN), :], preferred_element_type=F32) * SC
                dkt = lax.dot_general(qr, dsb, (((0,), (0,)), ((), ())), preferred_element_type=F32) * SC
                dvt = lax.dot_general(dor, p_all[pl.ds(q0, GRID_W), :], (((0,), (0,)), ((), ())), preferred_element_type=F32)
                akt[hh, par, pl.ds(t0, WIN // 128)] += jnp.stack([dkt[:, 128 * i:128 * (i + 1)] for i in range(WIN // 128)])
                avt[hh, par, pl.ds(t0, WIN // 128)] += jnp.stack([dvt[:, 128 * i:128 * (i + 1)] for i in range(WIN // 128)])
                return carry
            lax.fori_loop(0, rows, grads, 0, unroll=4)

        for hh in range(2):
            ls = slice(hh * NA_DH, (hh + 1) * NA_DH)
            for i in range(S // 128):
                for acc, dst in ((akt, dkn), (avt, dvs)):
                    odd = jnp.concatenate([acc[hh, 1, i][:, NA_DH:], acc[hh, 1, i + 1][:, :NA_DH]], axis=1)
                    dst[128 * i:128 * (i + 1), ls] = (acc[hh, 0, i] + odd).T

        for c in range(S // _RC):
            sl = slice(c * _RC, (c + 1) * _RC)
            for x_ref, g_ref, dn, dx_ref, dg_ref in ((q_ref, qg_ref, dqn, dq_ref, dqg_ref), (k_ref, kg_ref, dkn, dk_ref, dkg_ref)):
                x = x_ref[sl, :]
                r_ = lax.rsqrt(jnp.dot(x * x, seg_ref[...], precision=HI, preferred_element_type=F32) + EPS)
                xh = x * r_
                d = dn[sl, :]
                dxh = d * g_ref[...]
                mean = jnp.dot(dxh * xh, seg_ref[...], precision=HI, preferred_element_type=F32)
                dx_ref[sl, :] = (r_ * (dxh - xh * mean)).astype(BF16)
                dg_ref[...] += jnp.sum(d * xh, axis=0, keepdims=True)
            dv_ref[sl, :] = dvs[sl, :].astype(BF16)

    LW = 128
    blk = pl.BlockSpec((S, LW), lambda j, e: (e, j))
    vec = pl.BlockSpec((1, LW), lambda j, e: (0, j))
    bsp = pl.BlockSpec((2, NA_ROWS, GRID_W, WIN), lambda j, e: (j, 0, 0, 0))
    return pl.pallas_call(
        body, out_shape=(jax.ShapeDtypeStruct((T, GROUP_W), BF16),) * 3 + (
            jax.ShapeDtypeStruct((NA_HEADS, NA_ROWS, GRID_W, WIN), F32),
            jax.ShapeDtypeStruct((1, GROUP_W), F32), jax.ShapeDtypeStruct((1, GROUP_W), F32)),
        grid=(GROUP_W // LW, E),
        in_specs=[pl.BlockSpec((S, LW), lambda j, e: (e, B_Q // LW + j)),
                  pl.BlockSpec((S, LW), lambda j, e: (e, B_K // LW + j)),
                  pl.BlockSpec((S, LW), lambda j, e: (e, B_V // LW + j)),
                  vec, vec, bsp, pl.BlockSpec((LW, LW), lambda j, e: (0, 0)), blk],
        out_specs=(blk, blk, blk, bsp, vec, vec),
        scratch_shapes=[pltpu.VMEM((2, S, NA_DH), BF16)] * 4 + [pltpu.VMEM((S, LW), F32)] * 3
        + [pltpu.VMEM((2, 2, S // 128 + 1, NA_DH, 128), F32)] * 2
        + [pltpu.VMEM((S, WIN), F32)] * 2 + [pltpu.VMEM((S, WIN), BF16)] * 2,
        name=name, compiler_params=_cp(("parallel", "arbitrary")))(z, z, z, qg, kg, bias, seg, do)


def _na_rpb_grad(dbias, name):
    e_np, a_np, _ = _na_onehots()
    H = NA_HEADS
    x = dbias.reshape(H, NA_ROWS, GRID_W, NA_ROWS, GRID_W)
    x = jnp.transpose(x, (0, 1, 3, 2, 4)).reshape(H, NA_ROWS * NA_ROWS, GRID_W * GRID_W)

    def body(x_ref, e_ref, a_ref, o_ref):
        y = jnp.dot(x_ref[0], e_ref[...], precision=HI, preferred_element_type=F32)
        o_ref[0] = jnp.dot(a_ref[...], y, precision=HI, preferred_element_type=F32)

    out = pl.pallas_call(
        body, out_shape=jax.ShapeDtypeStruct((H, 16, 128), F32), grid=(H,),
        in_specs=[pl.BlockSpec((1, 64, GRID_W * GRID_W), lambda h: (h, 0, 0)),
                  pl.BlockSpec((GRID_W * GRID_W, 128), lambda h: (0, 0)),
                  pl.BlockSpec((16, 64), lambda h: (0, 0))],
        out_specs=pl.BlockSpec((1, 16, 128), lambda h: (h, 0, 0)),
        name=name, compiler_params=_cp(("parallel",)))(x, jnp.asarray(e_np), jnp.asarray(a_np))
    return out[:, :2 * NA_ROWS - 1, :2 * NA_COLS - 1]


_HK = GLA_HEADS * GLA_DK
_HV = GLA_HEADS * GLA_DV


def _gla_consts(reverse):
    i = np.arange(CHUNK)
    tri = (i[:, None] <= i[None, :]) if reverse else (i[:, None] >= i[None, :])
    j = np.arange(_HK)
    oseg = (j[:, None] // GLA_DK == j[None, :] // GLA_DK)
    return (jnp.asarray(tri.astype(np.float32)), jnp.asarray(tri.T.astype(np.float32)), jnp.asarray(oseg.astype(np.float32), BF16))


def _log_decay(lr, a2, ab):
    zg = jnp.dot(lr, a2, precision=HI, preferred_element_type=F32) + ab
    g = (jnp.minimum(zg, 0.0) - jnp.log(1.0 + jnp.exp(-jnp.abs(zg)))) * (1.0 / GLA_TAU)
    return zg, g


def _dotf(a, b, dn):
    return lax.dot_general(a, b, dn, precision=HI, preferred_element_type=F32)


def _dotb(a, b, dn):
    return lax.dot_general(a.astype(BF16), b.astype(BF16), dn, preferred_element_type=F32)


_COLS = 4
_SUB = 16
_NSUB = CHUNK // _SUB


def _gla_cross_blocks(reverse):
    return range(0, _NSUB - 1) if reverse else range(1, _NSUB)


def _gla_cross_terms(s, reverse, b_s, q_s, k_s, oseg_ref):
    r0 = s * _SUB
    ref = r0 + (_SUB - 1 if reverse else 0)
    bref = b_s[ref:ref + 1, :]
    rowj = lax.broadcasted_iota(jnp.int32, (CHUNK, 1), 0)
    seen = (rowj >= r0 + _SUB) if reverse else (rowj < r0)
    ek = jnp.where(seen, jnp.exp(jnp.minimum(bref - b_s[...], 0.0)), 0.0)
    kt = k_s[...] * ek
    eq = jnp.exp(jnp.minimum(b_s[r0:r0 + _SUB, :] - bref, 0.0))
    qt = q_s[r0:r0 + _SUB, :] * eq
    nmat = jnp.concatenate([kt.astype(BF16)] * GLA_HEADS, axis=0) * oseg_ref[...]
    return qt, eq, kt, ek, nmat


_NN = (((1,), (0,)), ((), ()))
_NT = (((1,), (1,)), ((), ()))
_TN = (((0,), (0,)), ((), ()))


_DT = 256


def _gla_block_tri(reverse):
    i = np.arange(_DT)
    same = i[:, None] // CHUNK == i[None, :] // CHUNK
    tri = (i[:, None] <= i[None, :]) if reverse else (i[:, None] >= i[None, :])
    return (tri & same).astype(np.float32)


def _gla_decay_fwd(z, a2, ab, reverse, name):
    T = z.shape[0]
    nc = _DT // CHUNK

    def body(lr_ref, a2_ref, ab_ref, m_ref, b_ref, ec_ref):
        _, g = _log_decay(lr_ref[...], a2_ref[...], ab_ref[...])
        b_ref[...] = _dotf(m_ref[...], g, _NN)
        for c in range(nc):
            ec_ref[c] = jnp.exp(_dotf(g[c * CHUNK:(c + 1) * CHUNK, :], jnp.ones((CHUNK, GLA_DV), F32), _TN))

    return pl.pallas_call(
        body, out_shape=(jax.ShapeDtypeStruct((T, _HK), F32), jax.ShapeDtypeStruct((T // CHUNK, _HK, GLA_DV), F32)),
        grid=(T // _DT,),
        in_specs=[pl.BlockSpec((_DT, 128), lambda i: (i, LR_OFF // 128)),
                  pl.BlockSpec((128, _HK), lambda i: (0, 0)),
                  pl.BlockSpec((1, _HK), lambda i: (0, 0)),
                  pl.BlockSpec((_DT, _DT), lambda i: (0, 0))],
        out_specs=(pl.BlockSpec((_DT, _HK), lambda i: (i, 0)), pl.BlockSpec((nc, _HK, GLA_DV), lambda i: (i, 0, 0))),
        name=name, compiler_params=_cp(("parallel",)))(z, a2, ab, jnp.asarray(_gla_block_tri(reverse)))


def _gla_decay_bwd(z, a2_f, ab_f, a2_b, ab_b, db_f, db_b, name):
    T = z.shape[0]

    def body(lr_ref, a2f_ref, abf_ref, a2b_ref, abb_ref, mf_ref, mb_ref, dbf_ref, dbb_ref,
             dlr_ref, da2f_ref, dabf_ref, da2b_ref, dabb_ref):
        @pl.when(pl.program_id(0) == 0)
        def _():
            for r in (da2f_ref, dabf_ref, da2b_ref, dabb_ref):
                r[...] = jnp.zeros_like(r)

        lr = lr_ref[...]
        dlr = jnp.zeros((_DT, 128), F32)
        for a2_ref, ab_ref, mt_ref, db_ref, da2_ref, dab_ref in ((a2f_ref, abf_ref, mf_ref, dbf_ref, da2f_ref, dabf_ref),
                                                                 (a2b_ref, abb_ref, mb_ref, dbb_ref, da2b_ref, dabb_ref)):
            zg, _ = _log_decay(lr, a2_ref[...], ab_ref[...])
            dg = _dotf(mt_ref[...], db_ref[...], _NN)
            dzg = dg * (1.0 / (1.0 + jnp.exp(zg))) * (1.0 / GLA_TAU)
            dlr = dlr + _dotf(dzg, a2_ref[...], _NT)
            da2_ref[...] += _dotf(lr, dzg, _TN)
            dab_ref[...] += jnp.sum(dzg, axis=0, keepdims=True)
        dlr_ref[...] = dlr.astype(BF16)

    a2s = pl.BlockSpec((128, _HK), lambda i: (0, 0))
    abs_ = pl.BlockSpec((1, _HK), lambda i: (0, 0))
    ms = pl.BlockSpec((_DT, _DT), lambda i: (0, 0))
    row = pl.BlockSpec((_DT, _HK), lambda i: (i, 0))
    return pl.pallas_call(
        body, out_shape=(jax.ShapeDtypeStruct((T, 128), BF16), jax.ShapeDtypeStruct((128, _HK), F32), jax.ShapeDtypeStruct((1, _HK), F32),
                         jax.ShapeDtypeStruct((128, _HK), F32), jax.ShapeDtypeStruct((1, _HK), F32)),
        grid=(T // _DT,),
        in_specs=[pl.BlockSpec((_DT, 128), lambda i: (i, LR_OFF // 128)), a2s, abs_, a2s, abs_, ms, ms, row, row],
        out_specs=(pl.BlockSpec((_DT, 128), lambda i: (i, 0)), a2s, abs_, a2s, abs_),
        name=name, compiler_params=_cp(("arbitrary",)))(
            z, a2_f, ab_f, a2_b, ab_b, jnp.asarray(_gla_block_tri(False).T), jnp.asarray(_gla_block_tri(True).T), db_f, db_b)


def _gla_fwd(z, b_all, ecol, S, reverse, name):
    T = z.shape[0]
    E = T // S
    n = S // CHUNK
    _, _, oseg = _gla_consts(reverse)
    last = 0 if reverse else CHUNK - 1

    def body(q_ref, k_ref, v_ref, b_ref, ec_ref, oseg_ref, o_ref, a_ref, st_ref, st, b_s, q_s, k_s):
        @pl.when(pl.program_id(1) == 0)
        def _():
            st[...] = jnp.zeros_like(st)

        q = q_ref[...] * (GLA_DK ** -0.5)
        k = k_ref[...]
        v = v_ref[...]
        b = b_ref[...]
        bl_row = b_ref[last:last + 1, :]
        e_col = ec_ref[0]
        b_s[...] = b
        q_s[...] = q
        k_s[...] = k
        lane = lax.broadcasted_iota(jnp.int32, (1, _HK), 1) % GLA_DK

        rowi = lax.broadcasted_iota(jnp.int32, (CHUNK, 1), 0)
        blk0 = (rowi // _SUB) * _SUB

        def cols(jj, a):
            ts = []
            for u in range(_COLS):
                jp = jj * _COLS + u
                tiles = []
                for s in range(_NSUB):
                    rs_ = slice(s * _SUB, (s + 1) * _SUB)
                    bj = b_s[pl.ds(s * _SUB + jp, 1), :]
                    kj = k_s[pl.ds(s * _SUB + jp, 1), :]
                    tiles.append(q_s[rs_, :] * jnp.exp(jnp.minimum(b_s[rs_, :] - bj, 0.0)) * kj)
                ts.append(jnp.concatenate(tiles, axis=0).astype(BF16))
            r = jnp.dot(jnp.concatenate(ts, axis=0), oseg_ref[...], preferred_element_type=F32)
            for u in range(_COLS):
                a = jnp.where(lane == blk0 + (jj * _COLS + u), r[u * CHUNK:(u + 1) * CHUNK, :], a)
            return a

        a = lax.fori_loop(0, _SUB // _COLS, cols, jnp.zeros((CHUNK, _HK), F32))
        keep = (rowi <= lane) if reverse else (rowi >= lane)
        a = jnp.where(keep, a, 0.0)
        cross = []
        for s in range(_NSUB):
            if s in _gla_cross_blocks(reverse):
                qt, _, _, _, nmat = _gla_cross_terms(s, reverse, b_s, q_s, k_s, oseg_ref)
                cross.append(lax.dot_general(qt.astype(BF16), nmat, _NT, preferred_element_type=F32))
            else:
                cross.append(jnp.zeros((_SUB, _HK), F32))
        a = a + jnp.concatenate(cross, axis=0)
        a_ref[...] = a
        st_ref[0] = st[...]
        qb = q * jnp.exp(b)
        kd = k * jnp.exp(bl_row - b)
        for h in range(GLA_HEADS):
            ks_ = slice(h * GLA_DK, (h + 1) * GLA_DK)
            vs_ = slice(h * GLA_DV, (h + 1) * GLA_DV)
            s_h = st[ks_, :]
            o_ref[:, vs_] = _dotb(qb[:, ks_], s_h, _NN) + _dotb(a[:, ks_], v[:, vs_], _NN)
            st[ks_, :] = s_h * e_col[ks_, :] + _dotb(kd[:, ks_], v[:, vs_], _TN)

    def rowblk(e, c):
        return e * n + ((n - 1 - c) if reverse else c)

    return pl.pallas_call(
        body, out_shape=(jax.ShapeDtypeStruct((T, _HV), F32), jax.ShapeDtypeStruct((T, _HK), F32),
                         jax.ShapeDtypeStruct((T // CHUNK, _HK, GLA_DV), F32)),
        grid=(E, n),
        in_specs=[pl.BlockSpec((CHUNK, _HK), lambda e, c: (rowblk(e, c), C_Q // _HK)),
                  pl.BlockSpec((CHUNK, _HK), lambda e, c: (rowblk(e, c), C_K // _HK)),
                  pl.BlockSpec((CHUNK, _HV), lambda e, c: (rowblk(e, c), C_V // _HV)),
                  pl.BlockSpec((CHUNK, _HK), lambda e, c: (rowblk(e, c), 0)),
                  pl.BlockSpec((1, _HK, GLA_DV), lambda e, c: (rowblk(e, c), 0, 0)),
                  pl.BlockSpec((_HK, _HK), lambda e, c: (0, 0))],
        out_specs=(pl.BlockSpec((CHUNK, _HV), lambda e, c: (rowblk(e, c), 0)),
                   pl.BlockSpec((CHUNK, _HK), lambda e, c: (rowblk(e, c), 0)),
                   pl.BlockSpec((1, _HK, GLA_DV), lambda e, c: (rowblk(e, c), 0, 0))),
        scratch_shapes=[pltpu.VMEM((_HK, GLA_DV), F32)] + [pltpu.VMEM((CHUNK, _HK), F32)] * 3,
        name=name, compiler_params=_cp(("parallel", "arbitrary")))(z, z, z, b_all, ecol, oseg)


def _gla_bwd(z, b_all, ecol, att, states, do, prev, S, reverse, name):
    T = z.shape[0]
    E = T // S
    n = S // CHUNK
    _, _, oseg = _gla_consts(reverse)
    has_prev = prev is not None
    odt = BF16 if has_prev else F32
    last = 0 if reverse else CHUNK - 1

    def body(*refs):
        (q_ref, k_ref, v_ref, b_ref, ec_ref, oseg_ref, att_ref, st_ref, do_ref) = refs[:9]
        refs = refs[9:]
        if has_prev:
            pq_ref, pk_ref, pv_ref = refs[:3]
            refs = refs[3:]
        (dq_ref, dk_ref, dv_ref, db_ref, dst, b_s, q_s, k_s, da_s, dqb_s, dkd_s, dk3_s, dbn_s, dsp_s) = refs

        @pl.when(pl.program_id(1) == 0)
        def _():
            dst[...] = jnp.zeros_like(dst)

        q = q_ref[...] * (GLA_DK ** -0.5)
        k = k_ref[...]
        v = v_ref[...]
        b = b_ref[...]
        bl_row = b_ref[last:last + 1, :]
        eb = jnp.exp(b)
        ekd = jnp.exp(bl_row - b)
        qb = q * eb
        kd = k * ekd
        b_s[...] = b
        q_s[...] = q
        k_s[...] = k
        att = att_ref[...]
        s_all = st_ref[0]
        dsn = dst[...]
        e_col = ec_ref[0]
        do = do_ref[...]
        lane = lax.broadcasted_iota(jnp.int32, (1, _HK), 1) % GLA_DK
        rowi = lax.broadcasted_iota(jnp.int32, (CHUNK, 1), 0)
        keep = (rowi <= lane) if reverse else (rowi >= lane)
        for h in range(GLA_HEADS):
            ks_ = slice(h * GLA_DK, (h + 1) * GLA_DK)
            vs_ = slice(h * GLA_DV, (h + 1) * GLA_DV)
            do_h = do[:, vs_]
            s_h = s_all[ks_, :]
            dsn_h = dsn[ks_, :]
            dqb_s[:, ks_] = _dotb(do_h, s_h, _NT)
            dsp_s[ks_, :] = _dotb(qb[:, ks_], do_h, _TN) + dsn_h * e_col[ks_, :]
            da_s[:, ks_] = _dotb(do_h, v[:, vs_], _NT)
            dv_h = _dotb(att[:, ks_], do_h, _TN) + _dotb(kd[:, ks_], dsn_h, _NN)
            if has_prev:
                dv_h = dv_h + pv_ref[:, vs_]
            dv_ref[:, vs_] = dv_h.astype(odt)
            dkd_s[:, ks_] = _dotb(v[:, vs_], dsn_h, _NT)
        da_s[...] = jnp.where(keep, da_s[...], 0.0)
        dqb = dqb_s[...]
        dkd = dkd_s[...]
        x = dsn * s_all * e_col
        dbl_row = _dotf(jnp.ones((8, GLA_DV), F32), x, _NT)[0:1, :] + jnp.sum(dkd * kd, axis=0, keepdims=True)

        blk0 = (rowi // _SUB) * _SUB

        def cols(jj, carry):
            dq3, db3 = list(carry[:_NSUB]), list(carry[_NSUB:])
            sel = [jnp.where(lane == blk0 + (jj * _COLS + u), da_s[...], 0.0).astype(BF16) for u in range(_COLS)]
            dcols = jnp.dot(jnp.concatenate(sel, axis=0), oseg_ref[...], preferred_element_type=F32)
            for u in range(_COLS):
                jp = jj * _COLS + u
                for s in range(_NSUB):
                    rs_ = slice(s * _SUB, (s + 1) * _SUB)
                    bj = b_s[pl.ds(s * _SUB + jp, 1), :]
                    kj = k_s[pl.ds(s * _SUB + jp, 1), :]
                    tm_ = dcols[u * CHUNK + s * _SUB:u * CHUNK + (s + 1) * _SUB, :] * jnp.exp(jnp.minimum(b_s[rs_, :] - bj, 0.0))
                    dq3[s] = dq3[s] + tm_ * kj
                    gq = tm_ * q_s[rs_, :]
                    dk3_s[pl.ds(s * _SUB + jp, 1), :] = jnp.sum(gq, axis=0, keepdims=True)
                    w = gq * kj
                    dbn_s[pl.ds(s * _SUB + jp, 1), :] = jnp.sum(w, axis=0, keepdims=True)
                    db3[s] = db3[s] + w
            return tuple(dq3) + tuple(db3)

        zero = jnp.zeros((_SUB, _HK), F32)
        acc = lax.fori_loop(0, _SUB // _COLS, cols, (zero,) * (2 * _NSUB))
        dq3 = jnp.concatenate(acc[:_NSUB], axis=0)
        db3 = jnp.concatenate(acc[_NSUB:], axis=0)
        head = lax.broadcasted_iota(jnp.int32, (1, _HK), 1) // GLA_DK
        dq_x, db_x = [], []
        dk_x = jnp.zeros((CHUNK, _HK), F32)
        db_k = jnp.zeros((CHUNK, _HK), F32)
        for s in range(_NSUB):
            if s not in _gla_cross_blocks(reverse):
                dq_x.append(zero)
                db_x.append(zero)
                continue
            r0 = s * _SUB
            qt, eq, kt, ek, nmat = _gla_cross_terms(s, reverse, b_s, q_s, k_s, oseg_ref)
            seen = (lane >= r0 + _SUB) if reverse else (lane < r0)
            dax = jnp.where(seen, da_s[r0:r0 + _SUB, :], 0.0).astype(BF16)
            dqt = jnp.dot(dax, nmat, preferred_element_type=F32)
            full = lax.dot_general(dax, qt.astype(BF16), _TN, preferred_element_type=F32)
            dkt = full[0:CHUNK, :]
            for h in range(1, GLA_HEADS):
                dkt = jnp.where(head == h, full[h * CHUNK:(h + 1) * CHUNK, :], dkt)
            dq_x.append(dqt * eq)
            db_x.append(dqt * qt)
            dk_x = dk_x + dkt * ek
            db_k = db_k + dkt * kt
        dq = (dqb * eb + dq3 + jnp.concatenate(dq_x, axis=0)) * (GLA_DK ** -0.5)
        dk = dkd * ekd + dk3_s[...] + dk_x
        db = dqb * qb - dkd * kd + db3 - dbn_s[...] + jnp.concatenate(db_x, axis=0) - db_k
        db_ref[...] = jnp.where(rowi == last, db + dbl_row, db)
        if has_prev:
            dq = dq + pq_ref[...]
            dk = dk + pk_ref[...]
        dq_ref[...] = dq.astype(odt)
        dk_ref[...] = dk.astype(odt)
        dst[...] = dsp_s[...]

    def rowblk(e, c):
        return e * n + (c if reverse else (n - 1 - c))

    hk = pl.BlockSpec((CHUNK, _HK), lambda e, c: (rowblk(e, c), 0))
    hv = pl.BlockSpec((CHUNK, _HV), lambda e, c: (rowblk(e, c), 0))
    stb = pl.BlockSpec((1, _HK, GLA_DV), lambda e, c: (rowblk(e, c), 0, 0))
    in_specs = [pl.BlockSpec((CHUNK, _HK), lambda e, c: (rowblk(e, c), C_Q // _HK)),
                pl.BlockSpec((CHUNK, _HK), lambda e, c: (rowblk(e, c), C_K // _HK)),
                pl.BlockSpec((CHUNK, _HV), lambda e, c: (rowblk(e, c), C_V // _HV)),
                hk, stb, pl.BlockSpec((_HK, _HK), lambda e, c: (0, 0)), hk, stb, hv]
    args = [z, z, z, b_all, ecol, oseg, att, states, do]
    if has_prev:
        in_specs += [hk, hk, hv]
        args += list(prev)
    return pl.pallas_call(
        body, out_shape=(jax.ShapeDtypeStruct((T, _HK), odt), jax.ShapeDtypeStruct((T, _HK), odt),
                         jax.ShapeDtypeStruct((T, _HV), odt), jax.ShapeDtypeStruct((T, _HK), F32)),
        grid=(E, n), in_specs=in_specs, out_specs=(hk, hk, hv, hk),
        scratch_shapes=[pltpu.VMEM((_HK, GLA_DV), F32)] + [pltpu.VMEM((CHUNK, _HK), F32)] * 8 + [pltpu.VMEM((_HK, GLA_DV), F32)],
        name=name, compiler_params=_cp(("parallel", "arbitrary")))(*args)


def _gla_norm_fwd(of, ob, og, name):
    T = of.shape[0]
    tm = 256

    def body(f_ref, b_ref, g_ref, o_ref):
        for h in range(GLA_HEADS):
            vs_ = slice(h * GLA_DV, (h + 1) * GLA_DV)
            o = f_ref[:, vs_] + b_ref[:, vs_]
            o_ref[:, vs_] = o * lax.rsqrt(jnp.mean(o * o, axis=-1, keepdims=True) + EPS) * g_ref[:, vs_]

    row = pl.BlockSpec((tm, _HV), lambda i: (i, 0))
    vec = pl.BlockSpec((1, _HV), lambda i: (0, 0))
    return pl.pallas_call(body, out_shape=jax.ShapeDtypeStruct((T, _HV), F32), grid=(T // tm,),
                          in_specs=[row, row, vec], out_specs=row, name=name, compiler_params=_cp(("parallel",)))(of, ob, og)


def _gla_norm_bwd(of, ob, og, dpre, name):
    T = of.shape[0]
    tm = 256

    def body(f_ref, b_ref, g_ref, dp_ref, do_ref, dg_ref):
        @pl.when(pl.program_id(0) == 0)
        def _():
            dg_ref[...] = jnp.zeros_like(dg_ref)

        for h in range(GLA_HEADS):
            vs_ = slice(h * GLA_DV, (h + 1) * GLA_DV)
            o = f_ref[:, vs_] + b_ref[:, vs_]
            r = lax.rsqrt(jnp.mean(o * o, axis=-1, keepdims=True) + EPS)
            xh = o * r
            dp = dp_ref[:, vs_]
            dxh = dp * g_ref[:, vs_]
            do_ref[:, vs_] = r * (dxh - xh * jnp.mean(dxh * xh, axis=-1, keepdims=True))
            dg_ref[:, vs_] += jnp.sum(dp * xh, axis=0, keepdims=True)

    row = pl.BlockSpec((tm, _HV), lambda i: (i, 0))
    vec = pl.BlockSpec((1, _HV), lambda i: (0, 0))
    return pl.pallas_call(
        body, out_shape=(jax.ShapeDtypeStruct((T, _HV), F32), jax.ShapeDtypeStruct((1, _HV), F32)), grid=(T // tm,),
        in_specs=[row, row, vec, row], out_specs=(row, vec), name=name, compiler_params=_cp(("arbitrary",)))(of, ob, og, dpre)


_ANY = pl.BlockSpec(memory_space=pl.ANY)


def _coords():
    return lax.axis_index("x"), lax.axis_index("y"), lax.axis_index("c")


def _other_chips(x, y):
    return ((1 - x, y), (x, 1 - y), (1 - x, 1 - y))


def _gather_weights(arrays, chunks, name):
    n = len(arrays)
    pieces = []
    for k in range(max(chunks)):
        for i, a in enumerate(arrays):
            if k < chunks[i]:
                rc = a.shape[1] // chunks[i]
                pieces.append((i, k * rc, rc))
    m = len(pieces)

    def body(*refs):
        srcs, dsts = refs[:n], refs[n:2 * n]
        send_sems, recv_sems, local_sems = refs[2 * n:]
        x, y, c = _coords()
        me = 2 * x + y
        loc = [pltpu.make_async_copy(s, d.at[me], local_sems.at[i]) for i, (s, d) in enumerate(zip(srcs, dsts))]
        for cp in loc:
            cp.start()
        ici = []
        for p, (i, r0, rc) in enumerate(pieces):
            for j, (px, py) in enumerate(_other_chips(x, y)):
                ici.append(pltpu.make_async_remote_copy(
                    src_ref=srcs[i].at[c, pl.ds(r0, rc)], dst_ref=dsts[i].at[me, c, pl.ds(r0, rc)],
                    send_sem=send_sems.at[3 * p + j], recv_sem=recv_sems.at[3 * p + j],
                    device_id=(px, py, c), device_id_type=MESH))
        for cp in ici:
            cp.start()
        fwd = []
        for p, (i, r0, rc) in enumerate(pieces):
            for j, (px, py) in enumerate(_other_chips(x, y)):
                ici[3 * p + j].wait_recv()
                part = dsts[i].at[2 * px + py, c, pl.ds(r0, rc)]
                cp = pltpu.make_async_remote_copy(
                    src_ref=part, dst_ref=part, send_sem=send_sems.at[3 * m + 3 * p + j], recv_sem=recv_sems.at[3 * m + 3 * p + j],
                    device_id=(x, y, 1 - c), device_id_type=MESH)
                cp.start()
                fwd.append(cp)
        for cp in fwd:
            cp.wait_recv()
        for cp in ici + fwd:
            cp.wait_send()
        for cp in loc:
            cp.wait()

    return pl.pallas_call(
        body, out_shape=tuple(jax.ShapeDtypeStruct((4,) + a.shape, a.dtype) for a in arrays),
        in_specs=[_ANY] * n, out_specs=(_ANY,) * n,
        scratch_shapes=[pltpu.SemaphoreType.DMA((6 * m,)), pltpu.SemaphoreType.DMA((6 * m,)), pltpu.SemaphoreType.DMA((n,))],
        name=name)(*arrays)


def _sibling_exchange(layered, whole, name):
    nl, n = len(layered), len(layered) + len(whole)

    def body(*refs):
        srcs, dsts = refs[:n], refs[n:2 * n]
        send_sems, recv_sems = refs[2 * n:]
        x, y, c = _coords()
        rem = [pltpu.make_async_remote_copy(src_ref=(s.at[1 - c] if i < nl else s), dst_ref=d, send_sem=send_sems.at[i],
                                            recv_sem=recv_sems.at[i], device_id=(x, y, 1 - c), device_id_type=MESH)
               for i, (s, d) in enumerate(zip(srcs, dsts))]
        for cp in rem:
            cp.start()
        for cp in rem:
            cp.wait()

    outs = [jax.ShapeDtypeStruct(a.shape[1:], a.dtype) for a in layered] + [jax.ShapeDtypeStruct(a.shape, a.dtype) for a in whole]
    return pl.pallas_call(
        body, out_shape=tuple(outs), in_specs=[_ANY] * n, out_specs=(_ANY,) * n,
        scratch_shapes=[pltpu.SemaphoreType.DMA((n,)), pltpu.SemaphoreType.DMA((n,))], name=name)(*layered, *whole)


def _chip_exchange(scatter, bcast, name):
    ns, n = len(scatter), len(scatter) + len(bcast)

    def body(*refs):
        srcs, dsts = refs[:n], refs[n:2 * n]
        send_sems, recv_sems, local_sems = refs[2 * n:]
        x, y, c = _coords()
        me = 2 * x + y
        loc = [pltpu.make_async_copy((s.at[me] if i < ns else s), d.at[me], local_sems.at[i])
               for i, (s, d) in enumerate(zip(srcs, dsts))]
        for cp in loc:
            cp.start()
        rem = []
        for j, (px, py) in enumerate(_other_chips(x, y)):
            for i, (s, d) in enumerate(zip(srcs, dsts)):
                rem.append(pltpu.make_async_remote_copy(
                    src_ref=(s.at[2 * px + py] if i < ns else s), dst_ref=d.at[me], send_sem=send_sems.at[n * j + i],
                    recv_sem=recv_sems.at[n * j + i], device_id=(px, py, c), device_id_type=MESH))
        for cp in rem:
            cp.start()
        for cp in rem:
            cp.wait()
        for cp in loc:
            cp.wait()

    outs = [jax.ShapeDtypeStruct(a.shape, a.dtype) for a in scatter] + [jax.ShapeDtypeStruct((4,) + a.shape, a.dtype) for a in bcast]
    return pl.pallas_call(
        body, out_shape=tuple(outs), in_specs=[_ANY] * n, out_specs=(_ANY,) * n,
        scratch_shapes=[pltpu.SemaphoreType.DMA((3 * n,)), pltpu.SemaphoreType.DMA((3 * n,)), pltpu.SemaphoreType.DMA((n,))],
        name=name)(*scatter, *bcast)


_EW_BLOCK_BYTES = 2 * 1024 * 1024


def _tile2d(R, C):
    if R % 256 == 0 and 256 * C * 4 <= _EW_BLOCK_BYTES:
        return 256, C
    bc = 256 if C % 256 == 0 else C
    for br in range(R, 0, -1):
        if R % br == 0 and (br % 8 == 0 or br == R) and br * bc * 4 <= _EW_BLOCK_BYTES:
            return br, bc
    return R, bc


def _sum_slots(r, name):
    n, R, C = r.shape
    br, bc = _tile2d(R, C)

    def body(r_ref, o_ref):
        acc = r_ref[0].astype(F32)
        for i in range(1, n):
            acc = acc + r_ref[i].astype(F32)
        o_ref[...] = acc

    return pl.pallas_call(body, out_shape=jax.ShapeDtypeStruct((R, C), F32), grid=(R // br, C // bc),
                          in_specs=[pl.BlockSpec((n, br, bc), lambda i, j: (0, i, j))],
                          out_specs=pl.BlockSpec((br, bc), lambda i, j: (i, j)),
                          name=name, compiler_params=_cp(("parallel", "parallel")))(r)


def _add2(a, b, out_dtype, name):
    R, C = a.shape
    br, bc = _tile2d(R, C)

    def body(a_ref, b_ref, o_ref):
        o_ref[...] = (a_ref[...].astype(F32) + b_ref[...].astype(F32)).astype(out_dtype)

    blk = pl.BlockSpec((br, bc), lambda i, j: (i, j))
    return pl.pallas_call(body, out_shape=jax.ShapeDtypeStruct((R, C), out_dtype), grid=(R // br, C // bc), in_specs=[blk, blk],
                          out_specs=blk, name=name, compiler_params=_cp(("parallel", "parallel")))(a, b)


def _adamw_math(w, g, m, v):
    m = ADAM_B1 * m + (1.0 - ADAM_B1) * g
    v = ADAM_B2 * v + (1.0 - ADAM_B2) * (g * g)
    m_hat = m / (1.0 - ADAM_B1 ** ADAM_STEP)
    v_hat = v / (1.0 - ADAM_B2 ** ADAM_STEP)
    delta = -ADAM_LR * (m_hat / (jnp.sqrt(v_hat) + ADAM_EPS) + ADAM_WD * w)
    return delta, m, v


def _adamw(w, gs, m, v, name):
    R, C = w.shape
    br, bc = _tile2d(R, C)

    def body(*refs):
        w_ref = refs[0]
        g_refs = refs[1:1 + len(gs)]
        m_ref, v_ref, g_out, d_out, m_out, v_out = refs[1 + len(gs):]
        g = None
        for gr in g_refs:
            parts = [gr[i] for i in range(gr.shape[0])] if len(gr.shape) == 3 else [gr[...]]
            for p in parts:
                g = p if g is None else g + p
        d, mn, vn = _adamw_math(w_ref[...], g, m_ref[...], v_ref[...])
        g_out[...] = g
        d_out[...] = d
        m_out[...] = mn
        v_out[...] = vn

    blk = pl.BlockSpec((br, bc), lambda i, j: (i, j))
    g_specs = [pl.BlockSpec((g.shape[0], br, bc), lambda i, j: (0, i, j)) if g.ndim == 3 else blk for g in gs]
    return pl.pallas_call(
        body, out_shape=tuple(jax.ShapeDtypeStruct((R, C), F32) for _ in range(4)), grid=(R // br, C // bc),
        in_specs=[blk] + g_specs + [blk, blk], out_specs=(blk,) * 4, name=name,
        compiler_params=_cp(("parallel", "parallel")))(w, *gs, m, v)


WEIGHTS = ("norm_g", "w_in", "conv_w", "conv_b", "conv_ln_g", "conv_ln_b", "na_q_g", "na_k_g", "na_rpb", "gla_a2_f",
           "gla_ab_f", "gla_a2_b", "gla_ab_b", "gla_o_g", "pool_w", "pool_scale", "w_out")
_REPL = ("norm_g", "conv_b", "conv_ln_g", "conv_ln_b", "na_q_g", "na_k_g", "na_rpb", "gla_ab_f", "gla_ab_b", "gla_o_g",
         "pool_w", "pool_scale")
_SHARD_SMALL = ("conv_w", "gla_a2_f", "gla_a2_b")
_PACK_ROWS = 8 * 128


def _pack(arrs):
    flat = jnp.concatenate([a.reshape(-1) for a in arrs])
    n = -(-flat.shape[0] // _PACK_ROWS) * _PACK_ROWS
    return jnp.pad(flat, (0, n - flat.shape[0])).reshape(-1, 128)


def _unpack(p, shapes):
    flat = p.reshape(-1)
    out, o = [], 0
    for s in shapes:
        n = int(np.prod(s))
        out.append(flat[o:o + n].reshape(s))
        o += n
    return out


def _to_layout_rows(w):
    pad = jnp.zeros((w.shape[0], NZ - N_IN, w.shape[2]), w.dtype)
    return jnp.concatenate([w[:, :5120], w[:, 5152:6176], w[:, 5120:5152], pad], axis=1)


def _from_layout_rows(w):
    return jnp.concatenate([w[:, :5120], w[:, LR_OFF:LR_OFF + 32], w[:, 5120:LR_OFF]], axis=1)


def _reduce_gradients(p_a, p_b, small_g, ci):
    two = lambda a: a.reshape(-1, a.shape[-1])
    s_a, s_b, s_small = _sibling_exchange((p_a, p_b), (small_g,), "grad_to_sibling")
    mine = lambda a: lax.dynamic_index_in_dim(a, ci, 0, keepdims=False)
    c_a = _add2(two(mine(p_a)), two(s_a), BF16, "chip_sum_a").reshape(s_a.shape)
    c_b = _add2(two(mine(p_b)), two(s_b), BF16, "chip_sum_b").reshape(s_b.shape)
    c_small = _add2(small_g, s_small, F32, "chip_sum_small")
    r_a, r_b, r_small = _chip_exchange((c_a, c_b), (c_small,), "grad_to_owner")
    own_a = _sum_slots(r_a, "sum_a")
    own_b = _sum_slots(r_b, "sum_b")
    sib_a, sib_b = _sibling_exchange((), (own_a, own_b), "reduced_to_sibling")
    by_layer = lambda own, sib: jnp.where(ci == 0, jnp.stack([own, sib]), jnp.stack([sib, own]))
    return by_layer(own_a, sib_a), by_layer(own_b, sib_b), r_small


def _layer_fwd(l, x, P, S):
    n = f"l{l}_"
    h = _rmsnorm_fwd(x, P["norm_g"], n + "rms_fwd")
    z = _matmul(h, P["w_in"], dims="nn", out_dtype=F32, tm=1024, tn=1280, tk=D_MODEL, name=n + "mm_z")
    yc = _conv_fwd(z, P["conv_w32"], P["conv_b"], S, n + "conv_fwd")
    pre_a = _ln_silu_fwd(yc, P["conv_ln_g"], P["conv_ln_b"], n + "ln_fwd")
    pre_b = _na_fwd(z, P["na_q_g"], P["na_k_g"], P["na_bias"], S, n + "na_fwd")
    bf, ecf = _gla_decay_fwd(z, P["a2_f"], P["gla_ab_f"], False, n + "gla_decay_f")
    bb, ecb = _gla_decay_fwd(z, P["a2_b"], P["gla_ab_b"], True, n + "gla_decay_b")
    of, af, sf = _gla_fwd(z, bf, ecf, S, False, n + "gla_fwd_f")
    ob, ab, sb = _gla_fwd(z, bb, ecb, S, True, n + "gla_fwd_b")
    pre_c = _gla_norm_fwd(of, ob, P["gla_o_g"], n + "gla_norm_fwd")
    pre_d = _pool_fwd(z, P["pool_w_bf"], P["pool_scale"], S, n + "pool_fwd")
    pres = (pre_a, pre_b, pre_c, pre_d)
    y = _gate_fwd(pres, z, n + "gate_fwd")
    out = _matmul(y, P["w_out"], dims="nn", out_dtype=F32, tm=512, tn=2048, tk=D_MODEL, name=n + "mm_out", res=x)
    return out, dict(x=x, h=h, z=z, yc=yc, pres=pres, of=of, af=af, sf=sf, ob=ob, ab=ab, sb=sb, y=y, bf=bf, ecf=ecf, bb=bb, ecb=ecb)


def _layer_bwd(l, dout, dout_bf, sv, P, S):
    n = f"l{l}_"
    z = sv["z"]
    T = z.shape[0]
    dy = _matmul(dout_bf, P["w_out_t"], dims="nn", out_dtype=F32, tm=512, tn=2048, tk=D_MODEL, name=n + "mm_dy")
    d_w_out = _matmul(sv["y"], dout_bf, dims="tn", out_dtype=BF16, tm=1024, tn=2048, tk=512, name=n + "mm_dwout")
    dpa, dpb, dpc, dpd, dga, dgb, dgc, dgd = _gate_bwd(dy, sv["pres"], z, n + "gate_bwd")
    dyc, d_ln_g, d_ln_b = _ln_silu_bwd(sv["yc"], P["conv_ln_g"], P["conv_ln_b"], dpa, n + "ln_bwd")
    dval, dglu, d_cw, d_cb = _conv_bwd(z, P["conv_w32"], dyc, S, n + "conv_bwd")
    dq, dk, dv, dbias, d_qg, d_kg = _na_bwd(z, P["na_q_g"], P["na_k_g"], P["na_bias"], dpb, S, n + "na_bwd")
    d_rpb = _na_rpb_grad(dbias, n + "na_rpb")
    do, d_og = _gla_norm_bwd(sv["of"], sv["ob"], P["gla_o_g"], dpc, n + "gla_norm_bwd")
    part = _gla_bwd(z, sv["bf"], sv["ecf"], sv["af"], sv["sf"], do, None, S, False, n + "gla_bwd_f")
    dcq, dck, dcv, db_b = _gla_bwd(z, sv["bb"], sv["ecb"], sv["ab"], sv["sb"], do, part[:3], S, True, n + "gla_bwd_b")
    dlr, d_a2f, d_abf, d_a2b, d_abb = _gla_decay_bwd(z, P["a2_f"], P["gla_ab_f"], P["a2_b"], P["gla_ab_b"], part[3], db_b,
                                                     n + "gla_decay_bwd")
    dd, d_pw, d_ps = _pool_bwd(z, P["pool_w_bf"], P["pool_scale"], dpd, S, n + "pool_bwd")
    dz = _concat_cols([dval, dglu, dga, dq, dk, dv, dgb, dcq, dck, dcv, dgc, dd, dgd, dlr], NZ, n + "dz_concat")
    dh = _matmul(dz, P["w_in_t"], dims="nn", out_dtype=F32, tm=1024, tn=1024, tk=3200, name=n + "mm_dh")
    d_w_in = _matmul(dz, sv["h"], dims="tn", out_dtype=BF16, tm=1280, tn=1024, tk=1024, name=n + "mm_dwin")
    dx, dx_bf, d_ng = _rmsnorm_bwd(sv["x"], P["norm_g"], dh, dout, n + "rms_bwd")
    grads = dict(norm_g=d_ng[0], w_in=d_w_in, conv_w=d_cw[:CONV_K], conv_b=d_cb[0], conv_ln_g=d_ln_g[0], conv_ln_b=d_ln_b[0],
                 na_q_g=d_qg.reshape(NA_HEADS, NA_DH), na_k_g=d_kg.reshape(NA_HEADS, NA_DH), na_rpb=d_rpb,
                 gla_a2_f=d_a2f[0:GLA_RANK], gla_ab_f=d_abf[0], gla_a2_b=d_a2b[GLA_RANK:2 * GLA_RANK], gla_ab_b=d_abb[0],
                 gla_o_g=d_og.reshape(GLA_HEADS, GLA_DV), pool_w=d_pw, pool_scale=d_ps[0], w_out=d_w_out)
    return dx, dx_bf, grads


def kernel(x, norm_g, w_in, conv_w, conv_b, conv_ln_g, conv_ln_b, na_q_g, na_k_g, na_rpb, gla_a2_f, gla_ab_f, gla_a2_b, gla_ab_b, gla_o_g, pool_w, pool_scale, w_out, loss_target, m_norm_g, m_w_in, m_conv_w, m_conv_b, m_conv_ln_g, m_conv_ln_b, m_na_q_g, m_na_k_g, m_na_rpb, m_gla_a2_f, m_gla_ab_f, m_gla_a2_b, m_gla_ab_b, m_gla_o_g, m_pool_w, m_pool_scale, m_w_out, v_norm_g, v_w_in, v_conv_w, v_conv_b, v_conv_ln_g, v_conv_ln_b, v_na_q_g, v_na_k_g, v_na_rpb, v_gla_a2_f, v_gla_ab_f, v_gla_a2_b, v_gla_ab_b, v_gla_o_g, v_pool_w, v_pool_scale, v_w_out):
    W = dict(norm_g=norm_g, w_in=w_in, conv_w=conv_w, conv_b=conv_b, conv_ln_g=conv_ln_g, conv_ln_b=conv_ln_b, na_q_g=na_q_g,
             na_k_g=na_k_g, na_rpb=na_rpb, gla_a2_f=gla_a2_f, gla_ab_f=gla_ab_f, gla_a2_b=gla_a2_b, gla_ab_b=gla_ab_b,
             gla_o_g=gla_o_g, pool_w=pool_w, pool_scale=pool_scale, w_out=w_out)
    M = dict(norm_g=m_norm_g, w_in=m_w_in, conv_w=m_conv_w, conv_b=m_conv_b, conv_ln_g=m_conv_ln_g, conv_ln_b=m_conv_ln_b,
             na_q_g=m_na_q_g, na_k_g=m_na_k_g, na_rpb=m_na_rpb, gla_a2_f=m_gla_a2_f, gla_ab_f=m_gla_ab_f, gla_a2_b=m_gla_a2_b,
             gla_ab_b=m_gla_ab_b, gla_o_g=m_gla_o_g, pool_w=m_pool_w, pool_scale=m_pool_scale, w_out=m_w_out)
    V = dict(norm_g=v_norm_g, w_in=v_w_in, conv_w=v_conv_w, conv_b=v_conv_b, conv_ln_g=v_conv_ln_g, conv_ln_b=v_conv_ln_b,
             na_q_g=v_na_q_g, na_k_g=v_na_k_g, na_rpb=v_na_rpb, gla_a2_f=v_gla_a2_f, gla_ab_f=v_gla_ab_f, gla_a2_b=v_gla_a2_b,
             gla_ab_b=v_gla_ab_b, gla_o_g=v_gla_o_g, pool_w=v_pool_w, pool_scale=v_pool_scale, w_out=v_w_out)
    E, S, D = x.shape
    T = E * S
    L = DEPTH
    xi, yi, ci = _coords()
    chip = 2 * xi + yi
    cw_sh, a2_sh = conv_w.shape[-1], gla_a2_f.shape[-1]

    small_sh = jnp.concatenate([
        jnp.pad(conv_w, ((0, 0), (0, 1), (0, 0))),
        jnp.pad(gla_a2_f, ((0, 0), (0, 0), (0, 128 - a2_sh))),
        jnp.pad(gla_a2_b, ((0, 0), (0, 0), (0, 128 - a2_sh)))], axis=1)
    w_in_tr, m_w_in_tr, v_w_in_tr = (jnp.transpose(a, (0, 2, 1)) for a in (w_in, m_w_in, v_w_in))
    g_win, g_wout, g_small = _gather_weights((w_in_tr.astype(BF16), w_out.astype(BF16), small_sh), (1, 2, 1), "gather_weights")
    w_in_t_full = _to_layout_rows(jnp.transpose(g_win, (1, 0, 2, 3)).reshape(L, N_IN, D))
    w_in_full = jnp.transpose(w_in_t_full, (0, 2, 1))
    w_out_full = jnp.transpose(g_wout, (1, 0, 2, 3)).reshape(L, D, D)
    conv_w_full = jnp.transpose(g_small[:, :, 0:32, :], (1, 2, 0, 3)).reshape(L, 32, 4 * cw_sh)
    a2f_full = jnp.transpose(g_small[:, :, 32:48, :a2_sh], (1, 2, 0, 3)).reshape(L, GLA_RANK, 4 * a2_sh)
    a2b_full = jnp.transpose(g_small[:, :, 48:64, :a2_sh], (1, 2, 0, 3)).reshape(L, GLA_RANK, 4 * a2_sh)

    params = []
    for l in range(L):
        params.append(dict(
            norm_g=norm_g[l][None], w_in=w_in_full[l], w_out=w_out_full[l], w_in_t=w_in_t_full[l], w_out_t=w_out_full[l].T,
            conv_w32=conv_w_full[l], conv_b=conv_b[l][None],
            conv_ln_g=conv_ln_g[l][None], conv_ln_b=conv_ln_b[l][None], na_q_g=na_q_g[l].reshape(1, GROUP_W),
            na_k_g=na_k_g[l].reshape(1, GROUP_W), na_bias=_na_bias(na_rpb[l], f"l{l}_na_bias"),
            a2_f=jnp.zeros((128, _HK), F32).at[0:GLA_RANK].set(a2f_full[l]),
            a2_b=jnp.zeros((128, _HK), F32).at[GLA_RANK:2 * GLA_RANK].set(a2b_full[l]),
            gla_ab_f=gla_ab_f[l][None], gla_ab_b=gla_ab_b[l][None], gla_o_g=gla_o_g[l].reshape(1, GROUP_W),
            pool_w_bf=pool_w[l].astype(BF16), pool_scale=pool_scale[l][None]))

    act = x.reshape(T, D)
    saved = []
    for l in range(L):
        act, sv = _layer_fwd(l, act, params[l], S)
        saved.append(sv)
    dact, dact_bf, loss_loc = _loss_head(act, loss_target.reshape(T, D), "loss_head")
    loss = lax.psum(loss_loc[0, 0], ("x", "y", "c"))
    grads = [None] * L
    for l in reversed(range(L)):
        dact, dact_bf, grads[l] = _layer_bwd(l, dact, dact_bf, saved[l], params[l], S)
    grad_x = dact.reshape(E, S, D)
    G = {k: jnp.stack([grads[l][k] for l in range(L)]) for k in WEIGHTS}

    cols_in, cols_out = N_IN // 4, D
    p_win = _from_layout_rows(G["w_in"]).reshape(L, 4, cols_in, D)
    p_wout = G["w_out"].reshape(L, 4, D // 4, D)
    small_names = _REPL + _SHARD_SMALL
    small_g = _pack([G[k] for k in small_names])
    g_in, g_out, r_small = _reduce_gradients(p_win, p_wout, small_g, ci)

    rows_in, rows_out = L * cols_in, L * (D // 4)
    res = {}
    res["w_in"] = [jnp.transpose(a.reshape(L, cols_in, D), (0, 2, 1)) for a in _adamw(
        w_in_tr.reshape(rows_in, D), (g_in.reshape(rows_in, D),), m_w_in_tr.reshape(rows_in, D),
        v_w_in_tr.reshape(rows_in, D), "adamw_w_in")]
    res["w_out"] = [a.reshape(L, D // 4, D) for a in _adamw(
        w_out.reshape(rows_out, cols_out), (g_out.reshape(rows_out, cols_out),), m_w_out.reshape(rows_out, cols_out),
        v_w_out.reshape(rows_out, cols_out), "adamw_w_out")]
    zeros_sh = [jnp.zeros(G[k].shape, F32) for k in _SHARD_SMALL]
    pk = lambda dct: _pack([dct[k] for k in _REPL] + zeros_sh)
    small_res = _adamw(pk(W), (r_small,), pk(M), pk(V), "adamw_small")
    shapes = [G[k].shape for k in small_names]
    unp = [_unpack(a, shapes) for a in small_res]
    for i, k in enumerate(_REPL):
        res[k] = [u[i] for u in unp]
    g_sh = []
    for i, k in enumerate(_SHARD_SMALL):
        gfull = unp[0][len(_REPL) + i]
        wdt = W[k].shape[-1]
        g_sh.append(lax.dynamic_slice_in_dim(gfull, chip * wdt, wdt, axis=2))
    g_sh_p = _pack(g_sh)
    sh_res = _adamw(_pack([W[k] for k in _SHARD_SMALL]), (g_sh_p,), _pack([M[k] for k in _SHARD_SMALL]),
                    _pack([V[k] for k in _SHARD_SMALL]), "adamw_shard_small")
    shapes2 = [W[k].shape for k in _SHARD_SMALL]
    unp2 = [_unpack(a, shapes2) for a in sh_res]
    for i, k in enumerate(_SHARD_SMALL):
        res[k] = [u[i] for u in unp2]

    outs = [loss, grad_x]
    for j in range(4):
        outs += [res[k][j] for k in WEIGHTS]
    return tuple(outs)
```

```python
import functools

import numpy as np
import jax
import jax.numpy as jnp
from jax import lax
from jax.experimental import pallas as pl
from jax.experimental.pallas import tpu as pltpu

F32 = jnp.float32
BF16 = jnp.bfloat16
HI = lax.Precision.HIGHEST
MESH = pl.DeviceIdType.MESH

EPS = 1e-6
D_MODEL = 2048
GROUP_W = 512
SEQ = 2048
DEPTH = 2
N_IN = 6176
GRID_W = 64
CONV_K = 31
NA_HEADS = 8
NA_DH = 64
NA_ROWS = 8
NA_COLS = 16
GLA_HEADS = 4
GLA_DK = 64
GLA_DV = 128
GLA_RANK = 16
GLA_TAU = 16.0
CHUNK = 64
POOL_WINDOWS = (2, 4, 8, 16)
ADAM_LR, ADAM_B1, ADAM_B2, ADAM_EPS, ADAM_WD, ADAM_STEP = 0.001, 0.9, 0.999, 1e-08, 0.01, 10

A_VAL, A_GLU, A_GATE = 0, 512, 1024
B_Q, B_K, B_V, B_GATE = 1536, 2048, 2560, 3072
C_Q, C_K, C_V, C_GATE = 3584, 3840, 4096, 4608
D_VAL, D_GATE = 5120, 5632
LR_OFF = 6144
NZ = 6400
NEG = -1e30
VMEM_LIMIT = 56 * 1024 * 1024


def _cp(sem=None):
    return pltpu.CompilerParams(dimension_semantics=sem, vmem_limit_bytes=VMEM_LIMIT)


def _sigmoid(x):
    return 1.0 / (1.0 + jnp.exp(-x))


def _silu(x):
    return x * _sigmoid(x)


def _dsilu(x):
    s = _sigmoid(x)
    return s * (1.0 + x * (1.0 - s))


def _matmul(a, b, *, dims, out_dtype, tm, tn, tk, name, res=None):
    if dims == "nn":
        (M, K), N = a.shape, b.shape[1]
    elif dims == "nt":
        (M, K), N = a.shape, b.shape[0]
    else:
        (K, M), N = a.shape, b.shape[1]
    tm, tn, tk = min(tm, M), min(tn, N), min(tk, K)
    nk = K // tk
    assert M % tm == 0 and N % tn == 0 and K % tk == 0, (M, N, K, tm, tn, tk)
    dn = {"nn": (((1,), (0,)), ((), ())), "nt": (((1,), (1,)), ((), ())), "tn": (((0,), (0,)), ((), ()))}[dims]
    if dims == "tn":
        a_spec = pl.BlockSpec((tk, tm), lambda i, j, k: (k, i))
    else:
        a_spec = pl.BlockSpec((tm, tk), lambda i, j, k: (i, k))
    if dims == "nt":
        b_spec = pl.BlockSpec((tn, tk), lambda i, j, k: (j, k))
    else:
        b_spec = pl.BlockSpec((tk, tn), lambda i, j, k: (k, j))
    o_spec = pl.BlockSpec((tm, tn), lambda i, j, k: (i, j))
    has_res = res is not None

    def body(*refs):
        if has_res:
            a_ref, b_ref, r_ref, o_ref, acc = refs
        else:
            a_ref, b_ref, o_ref, acc = refs
        k = pl.program_id(2)

        @pl.when(k == 0)
        def _():
            acc[...] = jnp.zeros_like(acc)

        acc[...] += lax.dot_general(a_ref[...], b_ref[...], dn, preferred_element_type=F32)

        @pl.when(k == nk - 1)
        def _():
            r = acc[...]
            if has_res:
                r = r + r_ref[...]
            o_ref[...] = r.astype(o_ref.dtype)

    in_specs = [a_spec, b_spec] + ([o_spec] if has_res else [])
    args = (a, b) + ((res,) if has_res else ())
    return pl.pallas_call(
        body, out_shape=jax.ShapeDtypeStruct((M, N), out_dtype), grid=(M // tm, N // tn, nk),
        in_specs=in_specs, out_specs=o_spec, scratch_shapes=[pltpu.VMEM((tm, tn), F32)],
        name=name, compiler_params=_cp(("parallel", "parallel", "arbitrary")))(*args)


def _concat_cols(pieces, width, name):
    T = pieces[0].shape[0]
    tm = min(512, T)
    dt = pieces[0].dtype
    offs = np.cumsum([0] + [p.shape[1] for p in pieces])

    def body(*refs):
        o_ref = refs[-1]
        for p_ref, a, b in zip(refs[:-1], offs[:-1], offs[1:]):
            o_ref[:, a:b] = p_ref[...]
        if offs[-1] < width:
            o_ref[:, offs[-1]:width] = jnp.zeros((tm, width - offs[-1]), dt)

    return pl.pallas_call(
        body, out_shape=jax.ShapeDtypeStruct((T, width), dt), grid=(T // tm,),
        in_specs=[pl.BlockSpec((tm, p.shape[1]), lambda i: (i, 0)) for p in pieces],
        out_specs=pl.BlockSpec((tm, width), lambda i: (i, 0)), name=name, compiler_params=_cp(("parallel",)))(*pieces)


def _rmsnorm_fwd(x, g, name):
    T, D = x.shape
    tm = 256

    def body(x_ref, g_ref, h_ref):
        xv = x_ref[...]
        r = lax.rsqrt(jnp.mean(xv * xv, axis=-1, keepdims=True) + EPS)
        h_ref[...] = (xv * r * g_ref[...]).astype(h_ref.dtype)

    return pl.pallas_call(
        body, out_shape=jax.ShapeDtypeStruct((T, D), BF16), grid=(T // tm,),
        in_specs=[pl.BlockSpec((tm, D), lambda i: (i, 0)), pl.BlockSpec((1, D), lambda i: (0, 0))],
        out_specs=pl.BlockSpec((tm, D), lambda i: (i, 0)), name=name, compiler_params=_cp(("parallel",)))(x, g)


def _rmsnorm_bwd(x, g, dh, dres, name):
    T, D = x.shape
    tm = 256

    def body(x_ref, g_ref, dh_ref, dres_ref, dx_ref, dxb_ref, dg_ref):
        xv = x_ref[...]
        r = lax.rsqrt(jnp.mean(xv * xv, axis=-1, keepdims=True) + EPS)
        xh = xv * r
        dh_v = dh_ref[...]
        dxh = dh_v * g_ref[...]
        dx = r * (dxh - xh * jnp.mean(dxh * xh, axis=-1, keepdims=True)) + dres_ref[...]
        dx_ref[...] = dx
        dxb_ref[...] = dx.astype(BF16)

        @pl.when(pl.program_id(0) == 0)
        def _():
            dg_ref[...] = jnp.zeros_like(dg_ref)

        dg_ref[...] += jnp.sum(dh_v * xh, axis=0, keepdims=True)

    row = pl.BlockSpec((tm, D), lambda i: (i, 0))
    vec = pl.BlockSpec((1, D), lambda i: (0, 0))
    return pl.pallas_call(
        body, out_shape=(jax.ShapeDtypeStruct((T, D), F32), jax.ShapeDtypeStruct((T, D), BF16), jax.ShapeDtypeStruct((1, D), F32)),
        grid=(T // tm,), in_specs=[row, vec, row, row], out_specs=(row, row, vec), name=name,
        compiler_params=_cp(("arbitrary",)))(x, g, dh, dres)


def _loss_head(y, target, name):
    T, D = y.shape
    tm = 256

    def body(y_ref, t_ref, d_ref, db_ref, l_ref):
        e = y_ref[...] - t_ref[...]
        d = e * (1.0 / D)
        d_ref[...] = d
        db_ref[...] = d.astype(BF16)

        @pl.when(pl.program_id(0) == 0)
        def _():
            l_ref[...] = jnp.zeros_like(l_ref)

        row = jnp.sum(e * e, axis=-1, keepdims=True) * (0.5 / D)
        l_ref[...] += jnp.sum(row, axis=0, keepdims=True)

    row = pl.BlockSpec((tm, D), lambda i: (i, 0))
    return pl.pallas_call(
        body, out_shape=(jax.ShapeDtypeStruct((T, D), F32), jax.ShapeDtypeStruct((T, D), BF16),
                         jax.ShapeDtypeStruct((1, 1), F32)), grid=(T // tm,),
        in_specs=[row, row], out_specs=(row, row, pl.BlockSpec((1, 1), lambda i: (0, 0))),
        name=name, compiler_params=_cp(("arbitrary",)))(y, target)


_GATE_COLS = (A_GATE // GROUP_W, B_GATE // GROUP_W, C_GATE // GROUP_W, D_GATE // GROUP_W)


def _gate_fwd(pres, z, name):
    T = z.shape[0]
    tm = 256

    def body(pa, pb, pc, pd, ga, gb, gc, gd, y_ref):
        for n, (p, g) in enumerate(((pa, ga), (pb, gb), (pc, gc), (pd, gd))):
            y_ref[:, n * GROUP_W:(n + 1) * GROUP_W] = (p[...] * _silu(g[...])).astype(BF16)

    pre_spec = pl.BlockSpec((tm, GROUP_W), lambda i: (i, 0))
    gate_specs = [pl.BlockSpec((tm, GROUP_W), functools.partial(lambda i, c: (i, c), c=c)) for c in _GATE_COLS]
    return pl.pallas_call(
        body, out_shape=jax.ShapeDtypeStruct((T, 4 * GROUP_W), BF16), grid=(T // tm,),
        in_specs=[pre_spec] * 4 + gate_specs, out_specs=pl.BlockSpec((tm, 4 * GROUP_W), lambda i: (i, 0)),
        name=name, compiler_params=_cp(("parallel",)))(*pres, z, z, z, z)


def _gate_bwd(dy, pres, z, name):
    T = z.shape[0]
    tm = 256

    def body(dy_ref, pa, pb, pc, pd, ga, gb, gc, gd, dpa, dpb, dpc, dpd, dga, dgb, dgc, dgd):
        for n, (p, g, dp, dg) in enumerate(((pa, ga, dpa, dga), (pb, gb, dpb, dgb), (pc, gc, dpc, dgc), (pd, gd, dpd, dgd))):
            d = dy_ref[:, n * GROUP_W:(n + 1) * GROUP_W]
            gv = g[...]
            dp[...] = d * _silu(gv)
            dg[...] = (d * p[...] * _dsilu(gv)).astype(BF16)

    pre_spec = pl.BlockSpec((tm, GROUP_W), lambda i: (i, 0))
    gate_specs = [pl.BlockSpec((tm, GROUP_W), functools.partial(lambda i, c: (i, c), c=c)) for c in _GATE_COLS]
    outs = tuple([jax.ShapeDtypeStruct((T, GROUP_W), F32)] * 4 + [jax.ShapeDtypeStruct((T, GROUP_W), BF16)] * 4)
    return pl.pallas_call(
        body, out_shape=outs, grid=(T // tm,),
        in_specs=[pl.BlockSpec((tm, 4 * GROUP_W), lambda i: (i, 0))] + [pre_spec] * 4 + gate_specs,
        out_specs=tuple([pre_spec] * 8), name=name, compiler_params=_cp(("parallel",)))(dy, *pres, z, z, z, z)


_OP_TM = 256


def _out_proj_fwd(pres, z, w_out, x, target, name):
    T, D = x.shape
    tm = min(_OP_TM, T)
    with_loss = target is not None

    def body(*refs):
        pa, pb, pc, pd, ga, gb, gc, gd, w_ref, x_ref = refs[:10]
        refs = refs[10:]
        if with_loss:
            t_ref, y_ref, d_ref, db_ref, l_ref = refs
        else:
            y_ref, o_ref = refs
        for n_, (p, g) in enumerate(((pa, ga), (pb, gb), (pc, gc), (pd, gd))):
            y_ref[:, n_ * GROUP_W:(n_ + 1) * GROUP_W] = (p[...] * _silu(g[...])).astype(BF16)
        out = jnp.dot(y_ref[...], w_ref[...], preferred_element_type=F32) + x_ref[...]
        if with_loss:
            e = out - t_ref[...]
            d = e * (1.0 / D)
            d_ref[...] = d
            db_ref[...] = d.astype(BF16)

            @pl.when(pl.program_id(0) == 0)
            def _():
                l_ref[...] = jnp.zeros_like(l_ref)

            l_ref[...] += jnp.sum(jnp.sum(e * e, axis=-1, keepdims=True) * (0.5 / D), axis=0, keepdims=True)
        else:
            o_ref[...] = out

    pre_spec = pl.BlockSpec((tm, GROUP_W), lambda i: (i, 0))
    gate_specs = [pl.BlockSpec((tm, GROUP_W), functools.partial(lambda i, c: (i, c), c=c)) for c in _GATE_COLS]
    row = pl.BlockSpec((tm, D), lambda i: (i, 0))
    w_spec = pl.BlockSpec((4 * GROUP_W, D), lambda i: (0, 0))
    in_specs = [pre_spec] * 4 + gate_specs + [w_spec, row]
    args = list(pres) + [z, z, z, z, w_out, x]
    if with_loss:
        in_specs.append(row)
        args.append(target)
        out_shape = (jax.ShapeDtypeStruct((T, D), BF16), jax.ShapeDtypeStruct((T, D), F32), jax.ShapeDtypeStruct((T, D), BF16),
                     jax.ShapeDtypeStruct((1, 1), F32))
        out_specs = (row, row, row, pl.BlockSpec((1, 1), lambda i: (0, 0)))
    else:
        out_shape = (jax.ShapeDtypeStruct((T, D), BF16), jax.ShapeDtypeStruct((T, D), F32))
        out_specs = (row, row)
    return pl.pallas_call(body, out_shape=out_shape, grid=(T // tm,), in_specs=in_specs, out_specs=out_specs, name=name,
                          compiler_params=_cp(("arbitrary",)))(*args)


def _out_proj_bwd(dout_bf, w_out_t, pres, z, name):
    T, D = dout_bf.shape
    tm = min(_OP_TM, T)

    def body(do_ref, w_ref, pa, pb, pc, pd, ga, gb, gc, gd, dpa, dpb, dpc, dpd, dga, dgb, dgc, dgd):
        dy = jnp.dot(do_ref[...], w_ref[...], preferred_element_type=F32)
        for n_, (p, g, dp, dg) in enumerate(((pa, ga, dpa, dga), (pb, gb, dpb, dgb), (pc, gc, dpc, dgc), (pd, gd, dpd, dgd))):
            d = dy[:, n_ * GROUP_W:(n_ + 1) * GROUP_W]
            gv = g[...]
            dp[...] = d * _silu(gv)
            dg[...] = (d * p[...] * _dsilu(gv)).astype(BF16)

    pre_spec = pl.BlockSpec((tm, GROUP_W), lambda i: (i, 0))
    gate_specs = [pl.BlockSpec((tm, GROUP_W), functools.partial(lambda i, c: (i, c), c=c)) for c in _GATE_COLS]
    outs = tuple([jax.ShapeDtypeStruct((T, GROUP_W), F32)] * 4 + [jax.ShapeDtypeStruct((T, GROUP_W), BF16)] * 4)
    return pl.pallas_call(
        body, out_shape=outs, grid=(T // tm,),
        in_specs=[pl.BlockSpec((tm, D), lambda i: (i, 0)), pl.BlockSpec((D, 4 * GROUP_W), lambda i: (0, 0))] + [pre_spec] * 4 + gate_specs,
        out_specs=tuple([pre_spec] * 8), name=name, compiler_params=_cp(("parallel",)))(dout_bf, w_out_t, *pres, z, z, z, z)


_PAD = 16
_RC = 256


def _conv_fwd(z, conv_w32, conv_b, S, name):
    T = z.shape[0]
    E = T // S
    LW = 128

    def body(val_ref, glu_ref, w_ref, b_ref, y_ref, upad):
        upad[0:_PAD, :] = jnp.zeros((_PAD, LW), F32)
        upad[_PAD + S:_PAD + S + _PAD, :] = jnp.zeros((_PAD, LW), F32)
        upad[_PAD:_PAD + S, :] = val_ref[...] * _sigmoid(glu_ref[...])
        for r in range(S // _RC):
            acc = jnp.broadcast_to(b_ref[...], (_RC, LW))
            for k in range(CONV_K):
                st = r * _RC + k + 1
                acc = acc + upad[st:st + _RC, :] * w_ref[k:k + 1, :]
            y_ref[r * _RC:(r + 1) * _RC, :] = acc

    return pl.pallas_call(
        body, out_shape=jax.ShapeDtypeStruct((T, GROUP_W), F32), grid=(E, GROUP_W // LW),
        in_specs=[pl.BlockSpec((S, LW), lambda e, j: (e, A_VAL // LW + j)),
                  pl.BlockSpec((S, LW), lambda e, j: (e, A_GLU // LW + j)),
                  pl.BlockSpec((32, LW), lambda e, j: (0, j)),
                  pl.BlockSpec((1, LW), lambda e, j: (0, j))],
        out_specs=pl.BlockSpec((S, LW), lambda e, j: (e, j)),
        scratch_shapes=[pltpu.VMEM((S + 2 * _PAD, LW), F32)],
        name=name, compiler_params=_cp(("parallel", "parallel")))(z, z, conv_w32, conv_b)


def _conv_bwd(z, conv_w32, dyc, S, name):
    T = z.shape[0]
    E = T // S
    LW = 128

    def body(val_ref, glu_ref, w_ref, dy_ref, dval_ref, dglu_ref, dw_ref, db_ref, upad, dpad):
        e = pl.program_id(1)
        zeros = jnp.zeros((_PAD, LW), F32)
        upad[0:_PAD, :] = zeros
        upad[_PAD + S:_PAD + S + _PAD, :] = zeros
        dpad[0:_PAD, :] = zeros
        dpad[_PAD + S:_PAD + S + _PAD, :] = zeros
        upad[_PAD:_PAD + S, :] = val_ref[...] * _sigmoid(glu_ref[...])
        dpad[_PAD:_PAD + S, :] = dy_ref[...]

        @pl.when(e == 0)
        def _():
            dw_ref[...] = jnp.zeros_like(dw_ref)
            db_ref[...] = jnp.zeros_like(db_ref)

        db_ref[...] += jnp.sum(dy_ref[...], axis=0, keepdims=True)
        for r in range(S // _RC):
            dyr = dy_ref[r * _RC:(r + 1) * _RC, :]
            du = jnp.zeros((_RC, LW), F32)
            for k in range(CONV_K):
                st = r * _RC + k + 1
                dw_ref[k:k + 1, :] += jnp.sum(dyr * upad[st:st + _RC, :], axis=0, keepdims=True)
                sd = r * _RC + (CONV_K - 1 - k) + 1
                du = du + dpad[sd:sd + _RC, :] * w_ref[k:k + 1, :]
            sl = slice(r * _RC, (r + 1) * _RC)
            val = val_ref[sl, :]
            sg = _sigmoid(glu_ref[sl, :])
            dval_ref[sl, :] = (du * sg).astype(BF16)
            dglu_ref[sl, :] = (du * val * sg * (1.0 - sg)).astype(BF16)

    blk = pl.BlockSpec((S, LW), lambda j, e: (e, j))
    return pl.pallas_call(
        body, out_shape=(jax.ShapeDtypeStruct((T, GROUP_W), BF16), jax.ShapeDtypeStruct((T, GROUP_W), BF16),
                         jax.ShapeDtypeStruct((32, GROUP_W), F32), jax.ShapeDtypeStruct((1, GROUP_W), F32)),
        grid=(GROUP_W // LW, E),
        in_specs=[pl.BlockSpec((S, LW), lambda j, e: (e, A_VAL // LW + j)),
                  pl.BlockSpec((S, LW), lambda j, e: (e, A_GLU // LW + j)),
                  pl.BlockSpec((32, LW), lambda j, e: (0, j)), blk],
        out_specs=(blk, blk, pl.BlockSpec((32, LW), lambda j, e: (0, j)), pl.BlockSpec((1, LW), lambda j, e: (0, j))),
        scratch_shapes=[pltpu.VMEM((S + 2 * _PAD, LW), F32), pltpu.VMEM((S + 2 * _PAD, LW), F32)],
        name=name, compiler_params=_cp(("parallel", "arbitrary")))(z, z, conv_w32, dyc)


def _ln_silu_fwd(yc, g, b, name):
    T, C = yc.shape
    tm = 256

    def body(y_ref, g_ref, b_ref, o_ref):
        y = y_ref[...]
        mu = jnp.mean(y, axis=-1, keepdims=True)
        yc_ = y - mu
        r = lax.rsqrt(jnp.mean(yc_ * yc_, axis=-1, keepdims=True) + EPS)
        o_ref[...] = _silu(yc_ * r * g_ref[...] + b_ref[...])

    row = pl.BlockSpec((tm, C), lambda i: (i, 0))
    vec = pl.BlockSpec((1, C), lambda i: (0, 0))
    return pl.pallas_call(body, out_shape=jax.ShapeDtypeStruct((T, C), F32), grid=(T // tm,),
                          in_specs=[row, vec, vec], out_specs=row, name=name, compiler_params=_cp(("parallel",)))(yc, g, b)


def _ln_silu_bwd(yc, g, b, dpre, name):
    T, C = yc.shape
    tm = 256

    def body(y_ref, g_ref, b_ref, dp_ref, dy_ref, dg_ref, db_ref):
        y = y_ref[...]
        mu = jnp.mean(y, axis=-1, keepdims=True)
        yc_ = y - mu
        r = lax.rsqrt(jnp.mean(yc_ * yc_, axis=-1, keepdims=True) + EPS)
        xh = yc_ * r
        gv = g_ref[...]
        dln = dp_ref[...] * _dsilu(xh * gv + b_ref[...])
        dxh = dln * gv
        dy_ref[...] = r * (dxh - jnp.mean(dxh, axis=-1, keepdims=True) - xh * jnp.mean(dxh * xh, axis=-1, keepdims=True))

        @pl.when(pl.program_id(0) == 0)
        def _():
            dg_ref[...] = jnp.zeros_like(dg_ref)
            db_ref[...] = jnp.zeros_like(db_ref)

        dg_ref[...] += jnp.sum(dln * xh, axis=0, keepdims=True)
        db_ref[...] += jnp.sum(dln, axis=0, keepdims=True)

    row = pl.BlockSpec((tm, C), lambda i: (i, 0))
    vec = pl.BlockSpec((1, C), lambda i: (0, 0))
    return pl.pallas_call(
        body, out_shape=(jax.ShapeDtypeStruct((T, C), F32), jax.ShapeDtypeStruct((1, C), F32), jax.ShapeDtypeStruct((1, C), F32)),
        grid=(T // tm,), in_specs=[row, vec, vec, row], out_specs=(row, vec, vec), name=name,
        compiler_params=_cp(("arbitrary",)))(yc, g, b, dpre)


def _pool_counts(S, w, rows0, n):
    t = (lax.broadcasted_iota(jnp.int32, (n, 1), 0) + rows0)
    lo = jnp.maximum(t - w // 2, 0)
    hi = jnp.minimum(t + w // 2, S)
    return (hi - lo).astype(F32)


def _pool_fwd(z, pool_w, pool_scale, S, name):
    T = z.shape[0]
    E = T // S
    CG = 128

    def body(u_ref, w_ref, s_ref, o_ref, upad, dif):
        zeros = jnp.zeros((_PAD, GROUP_W), F32)
        upad[0:_PAD, :] = zeros
        upad[_PAD + S:_PAD + S + _PAD, :] = zeros
        upad[_PAD:_PAD + S, :] = u_ref[...]
        for gi, w in enumerate(POOL_WINDOWS):
            ls = slice(gi * CG, (gi + 1) * CG)
            for r in range(S // _RC):
                acc = jnp.zeros((_RC, CG), F32)
                for j in range(-(w // 2), w // 2):
                    st = _PAD + r * _RC + j
                    acc = acc + upad[st:st + _RC, ls]
                cnt = _pool_counts(S, w, r * _RC, _RC)
                dif[r * _RC:(r + 1) * _RC, :] = (acc / cnt - u_ref[r * _RC:(r + 1) * _RC, ls]).astype(BF16)
            yp = jnp.dot(dif[...], w_ref[gi], preferred_element_type=F32)
            o_ref[:, ls] = yp * s_ref[:, ls]

    return pl.pallas_call(
        body, out_shape=jax.ShapeDtypeStruct((T, GROUP_W), F32), grid=(E,),
        in_specs=[pl.BlockSpec((S, GROUP_W), lambda e: (e, D_VAL // GROUP_W)),
                  pl.BlockSpec((4, CG, CG), lambda e: (0, 0, 0)),
                  pl.BlockSpec((1, GROUP_W), lambda e: (0, 0))],
        out_specs=pl.BlockSpec((S, GROUP_W), lambda e: (e, 0)),
        scratch_shapes=[pltpu.VMEM((S + 2 * _PAD, GROUP_W), F32), pltpu.VMEM((S, CG), BF16)],
        name=name, compiler_params=_cp(("parallel",)))(z, pool_w, pool_scale)


def _pool_bwd(z, pool_w, pool_scale, dpre, S, name):
    T = z.shape[0]
    E = T // S
    CG = 128

    def body(u_ref, w_ref, s_ref, dp_ref, du_ref, dw_ref, ds_ref, upad, dif, qpad):
        zeros = jnp.zeros((_PAD, GROUP_W), F32)
        upad[0:_PAD, :] = zeros
        upad[_PAD + S:_PAD + S + _PAD, :] = zeros
        upad[_PAD:_PAD + S, :] = u_ref[...]
        zc = jnp.zeros((_PAD, CG), F32)
        qpad[0:_PAD, :] = zc
        qpad[_PAD + S:_PAD + S + _PAD, :] = zc

        @pl.when(pl.program_id(0) == 0)
        def _():
            dw_ref[...] = jnp.zeros_like(dw_ref)
            ds_ref[...] = jnp.zeros_like(ds_ref)

        for gi, w in enumerate(POOL_WINDOWS):
            ls = slice(gi * CG, (gi + 1) * CG)
            for r in range(S // _RC):
                acc = jnp.zeros((_RC, CG), F32)
                for j in range(-(w // 2), w // 2):
                    st = _PAD + r * _RC + j
                    acc = acc + upad[st:st + _RC, ls]
                cnt = _pool_counts(S, w, r * _RC, _RC)
                dif[r * _RC:(r + 1) * _RC, :] = (acc / cnt - u_ref[r * _RC:(r + 1) * _RC, ls]).astype(BF16)
            dp = dp_ref[:, ls]
            yp = jnp.dot(dif[...], w_ref[gi], preferred_element_type=F32)
            ds_ref[:, ls] += jnp.sum(dp * yp, axis=0, keepdims=True)
            dys = (dp * s_ref[:, ls]).astype(BF16)
            dw_ref[gi] += lax.dot_general(dif[...], dys, (((0,), (0,)), ((), ())), preferred_element_type=F32)
            dm = lax.dot_general(dys, w_ref[gi], (((1,), (1,)), ((), ())), preferred_element_type=F32)
            for r in range(S // _RC):
                cnt = _pool_counts(S, w, r * _RC, _RC)
                qpad[_PAD + r * _RC:_PAD + (r + 1) * _RC, :] = dm[r * _RC:(r + 1) * _RC, :] / cnt
            for r in range(S // _RC):
                acc = -dm[r * _RC:(r + 1) * _RC, :]
                for j in range(-(w // 2) + 1, w // 2 + 1):
                    st = _PAD + r * _RC + j
                    acc = acc + qpad[st:st + _RC, :]
                du_ref[r * _RC:(r + 1) * _RC, ls] = acc.astype(BF16)

    return pl.pallas_call(
        body, out_shape=(jax.ShapeDtypeStruct((T, GROUP_W), BF16), jax.ShapeDtypeStruct((4, CG, CG), F32),
                         jax.ShapeDtypeStruct((1, GROUP_W), F32)), grid=(E,),
        in_specs=[pl.BlockSpec((S, GROUP_W), lambda e: (e, D_VAL // GROUP_W)),
                  pl.BlockSpec((4, CG, CG), lambda e: (0, 0, 0)),
                  pl.BlockSpec((1, GROUP_W), lambda e: (0, 0)),
                  pl.BlockSpec((S, GROUP_W), lambda e: (e, 0))],
        out_specs=(pl.BlockSpec((S, GROUP_W), lambda e: (e, 0)), pl.BlockSpec((4, CG, CG), lambda e: (0, 0, 0)),
                   pl.BlockSpec((1, GROUP_W), lambda e: (0, 0))),
        scratch_shapes=[pltpu.VMEM((S + 2 * _PAD, GROUP_W), F32), pltpu.VMEM((S, CG), BF16),
                        pltpu.VMEM((S + 2 * _PAD, CG), F32)],
        name=name, compiler_params=_cp(("arbitrary",)))(z, pool_w, pool_scale, dpre)


def _na_tables():
    d = np.arange(NA_ROWS)[:, None]
    kr = np.arange(NA_ROWS)[None, :]
    ro = kr - d + (NA_ROWS - 1)
    qc = np.arange(GRID_W)[:, None]
    kc = np.arange(GRID_W)[None, :]
    cs = np.clip(qc - NA_COLS // 2, 0, GRID_W - NA_COLS)
    valid = (kc >= cs) & (kc < cs + NA_COLS)
    co = np.clip(kc - qc + (NA_COLS - 1), 0, 2 * NA_COLS - 2)
    return ro, co, valid


def _na_onehots():
    ro, co, valid = _na_tables()
    e_np = np.zeros((GRID_W, GRID_W, 128), np.float32)
    qi, ki = np.nonzero(valid)
    e_np[qi, ki, co[qi, ki]] = 1.0
    a_np = np.zeros((16, NA_ROWS * NA_ROWS), np.float32)
    a_np[ro.reshape(-1), np.arange(NA_ROWS * NA_ROWS)] = 1.0
    mask = np.where(valid, 0.0, NEG).astype(np.float32).reshape(1, GRID_W * GRID_W)
    return e_np.reshape(GRID_W * GRID_W, 128), a_np, mask


def _na_bias(rpb, name):
    e_np, a_np, mask = _na_onehots()
    H = NA_HEADS
    rp = jnp.pad(rpb, ((0, 0), (0, 1), (0, 128 - rpb.shape[2])))

    def body(r_ref, e_ref, at_ref, m_ref, o_ref):
        t = jnp.dot(at_ref[...], r_ref[0], precision=HI, preferred_element_type=F32)
        o_ref[0] = lax.dot_general(t, e_ref[...], (((1,), (1,)), ((), ())), precision=HI,
                                   preferred_element_type=F32) + m_ref[...]

    out = pl.pallas_call(
        body, out_shape=jax.ShapeDtypeStruct((H, NA_ROWS * NA_ROWS, GRID_W * GRID_W), F32), grid=(H,),
        in_specs=[pl.BlockSpec((1, 16, 128), lambda h: (h, 0, 0)),
                  pl.BlockSpec((GRID_W * GRID_W, 128), lambda h: (0, 0)),
                  pl.BlockSpec((NA_ROWS * NA_ROWS, 16), lambda h: (0, 0)),
                  pl.BlockSpec((1, GRID_W * GRID_W), lambda h: (0, 0))],
        out_specs=pl.BlockSpec((1, NA_ROWS * NA_ROWS, GRID_W * GRID_W), lambda h: (h, 0, 0)),
        name=name, compiler_params=_cp(("parallel",)))(rp, jnp.asarray(e_np), jnp.asarray(a_np.T), jnp.asarray(mask))
    t = out.reshape(H, NA_ROWS, NA_ROWS, GRID_W, GRID_W)
    return jnp.transpose(t, (0, 1, 3, 2, 4)).reshape(H, NA_ROWS, GRID_W, NA_ROWS * GRID_W)


def _seg_mean_matrix(width, seg):
    i = np.arange(width)
    return jnp.asarray((i[:, None] // seg == i[None, :] // seg).astype(np.float32) / seg)


def _na_fwd(z, qg, kg, bias, S, name):
    T = z.shape[0]
    E = T // S
    rows = S // GRID_W
    WIN = NA_ROWS * GRID_W
    seg = _seg_mean_matrix(128, NA_DH)

    def body(q_ref, k_ref, v_ref, qg_ref, kg_ref, bias_ref, seg_ref, o_ref, qs, ks, vs, s_all, p_all):
        for c in range(S // _RC):
            sl = slice(c * _RC, (c + 1) * _RC)
            q = q_ref[sl, :]
            k = k_ref[sl, :]
            qn = q * lax.rsqrt(jnp.dot(q * q, seg_ref[...], precision=HI, preferred_element_type=F32) + EPS) * qg_ref[...]
            kn = k * lax.rsqrt(jnp.dot(k * k, seg_ref[...], precision=HI, preferred_element_type=F32) + EPS) * kg_ref[...]
            v = v_ref[sl, :]
            for hh in range(2):
                ls = slice(hh * NA_DH, (hh + 1) * NA_DH)
                qs[hh, sl, :] = qn[:, ls].astype(BF16)
                ks[hh, sl, :] = kn[:, ls].astype(BF16)
                vs[hh, sl, :] = v[:, ls].astype(BF16)
        def where(r):
            rs = jnp.clip(r - NA_ROWS // 2, 0, rows - NA_ROWS)
            return rs, pl.multiple_of(r * GRID_W, GRID_W), pl.multiple_of(rs * GRID_W, GRID_W)

        def scores(r, carry):
            rs, q0, k0 = where(r)
            for hh in range(2):
                s = lax.dot_general(qs[hh, pl.ds(q0, GRID_W), :], ks[hh, pl.ds(k0, WIN), :], (((1,), (1,)), ((), ())),
                                    preferred_element_type=F32) * (NA_DH ** -0.5)
                s_all[hh, pl.ds(q0, GRID_W), :] = s + bias_ref[hh, r - rs]
            return carry
        lax.fori_loop(0, rows, scores, 0, unroll=4)

        def soft(r, carry):
            _, q0, _ = where(r)
            for hh in range(2):
                s = s_all[hh, pl.ds(q0, GRID_W), :]
                p = jnp.exp(s - jnp.max(s, axis=-1, keepdims=True))
                p_all[hh, pl.ds(q0, GRID_W), :] = (p * (1.0 / jnp.sum(p, axis=-1, keepdims=True))).astype(BF16)
            return carry
        lax.fori_loop(0, rows, soft, 0, unroll=2)

        def outp(r, carry):
            _, q0, k0 = where(r)
            outs = [jnp.dot(p_all[hh, pl.ds(q0, GRID_W), :], vs[hh, pl.ds(k0, WIN), :], preferred_element_type=F32)
                    for hh in range(2)]
            o_ref[pl.ds(q0, GRID_W), :] = jnp.concatenate(outs, axis=1)
            return carry
        lax.fori_loop(0, rows, outp, 0, unroll=4)

    LW = 128
    return pl.pallas_call(
        body, out_shape=jax.ShapeDtypeStruct((T, GROUP_W), F32), grid=(E, GROUP_W // LW),
        in_specs=[pl.BlockSpec((S, LW), lambda e, j: (e, B_Q // LW + j)),
                  pl.BlockSpec((S, LW), lambda e, j: (e, B_K // LW + j)),
                  pl.BlockSpec((S, LW), lambda e, j: (e, B_V // LW + j)),
                  pl.BlockSpec((1, LW), lambda e, j: (0, j)),
                  pl.BlockSpec((1, LW), lambda e, j: (0, j)),
                  pl.BlockSpec((2, NA_ROWS, GRID_W, WIN), lambda e, j: (j, 0, 0, 0)),
                  pl.BlockSpec((LW, LW), lambda e, j: (0, 0))],
        out_specs=pl.BlockSpec((S, LW), lambda e, j: (e, j)),
        scratch_shapes=[pltpu.VMEM((2, S, NA_DH), BF16)] * 3 + [pltpu.VMEM((2, S, WIN), F32), pltpu.VMEM((2, S, WIN), BF16)],
        name=name, compiler_params=_cp(("parallel", "parallel")))(z, z, z, qg, kg, bias, seg)


def _na_bwd(z, qg, kg, bias, do, S, name):
    T = z.shape[0]
    E = T // S
    rows = S // GRID_W
    WIN = NA_ROWS * GRID_W
    seg = _seg_mean_matrix(128, NA_DH)
    SC = NA_DH ** -0.5

    def body(q_ref, k_ref, v_ref, qg_ref, kg_ref, bias_ref, seg_ref, do_ref,
             dq_ref, dk_ref, dv_ref, dbias_ref, dqg_ref, dkg_ref, qs, ks, vs, dos, dqn, dkn, dvs, akt, avt,
             s_all, dp_all, p_all, ds_all):
        e = pl.program_id(1)

        @pl.when(e == 0)
        def _():
            dbias_ref[...] = jnp.zeros_like(dbias_ref)
            dqg_ref[...] = jnp.zeros_like(dqg_ref)
            dkg_ref[...] = jnp.zeros_like(dkg_ref)

        for c in range(S // _RC):
            sl = slice(c * _RC, (c + 1) * _RC)
            q = q_ref[sl, :]
            k = k_ref[sl, :]
            qn = q * lax.rsqrt(jnp.dot(q * q, seg_ref[...], precision=HI, preferred_element_type=F32) + EPS) * qg_ref[...]
            kn = k * lax.rsqrt(jnp.dot(k * k, seg_ref[...], precision=HI, preferred_element_type=F32) + EPS) * kg_ref[...]
            v = v_ref[sl, :]
            dd = do_ref[sl, :]
            for hh in range(2):
                ls = slice(hh * NA_DH, (hh + 1) * NA_DH)
                qs[hh, sl, :] = qn[:, ls].astype(BF16)
                ks[hh, sl, :] = kn[:, ls].astype(BF16)
                vs[hh, sl, :] = v[:, ls].astype(BF16)
                dos[hh, sl, :] = dd[:, ls].astype(BF16)
        akt[...] = jnp.zeros_like(akt)
        avt[...] = jnp.zeros_like(avt)

        def where(r):
            rs = jnp.clip(r - NA_ROWS // 2, 0, rows - NA_ROWS)
            return rs, pl.multiple_of(r * GRID_W, GRID_W), pl.multiple_of(rs * GRID_W, GRID_W)

        for hh in range(2):
            ls = slice(hh * NA_DH, (hh + 1) * NA_DH)

            def products(r, carry, hh=hh):
                rs, q0, k0 = where(r)
                s = lax.dot_general(qs[hh, pl.ds(q0, GRID_W), :], ks[hh, pl.ds(k0, WIN), :], (((1,), (1,)), ((), ())),
                                    preferred_element_type=F32) * SC
                s_all[pl.ds(q0, GRID_W), :] = s + bias_ref[hh, r - rs]
                dp_all[pl.ds(q0, GRID_W), :] = lax.dot_general(dos[hh, pl.ds(q0, GRID_W), :], vs[hh, pl.ds(k0, WIN), :],
                                                               (((1,), (1,)), ((), ())), preferred_element_type=F32)
                return carry
            lax.fori_loop(0, rows, products, 0, unroll=4)

            def soft(r, carry, hh=hh):
                rs, q0, _ = where(r)
                s = s_all[pl.ds(q0, GRID_W), :]
                p = jnp.exp(s - jnp.max(s, axis=-1, keepdims=True))
                p = p * (1.0 / jnp.sum(p, axis=-1, keepdims=True))
                dp = dp_all[pl.ds(q0, GRID_W), :]
                ds = p * (dp - jnp.sum(p * dp, axis=-1, keepdims=True))
                dbias_ref[hh, r - rs] += ds
                p_all[pl.ds(q0, GRID_W), :] = p.astype(BF16)
                ds_all[pl.ds(q0, GRID_W), :] = ds.astype(BF16)
                return carry
            lax.fori_loop(0, rows, soft, 0, unroll=2)

            def grads(r, carry, hh=hh, ls=ls):
                rs, q0, k0 = where(r)
                par = rs % 2
                t0 = (rs + par) // 2
                qr = qs[hh, pl.ds(q0, GRID_W), :]
                dor = dos[hh, pl.ds(q0, GRID_W), :]
                dsb = ds_all[pl.ds(q0, GRID_W), :]
                dqn[pl.ds(q0, GRID_W), ls] = jnp.dot(dsb, ks[hh, pl.ds(k0, WIN), :], preferred_element_type=F32) * SC
                dkt = lax.dot_general(qr, dsb, (((0,), (0,)), ((), ())), preferred_element_type=F32) * SC
                dvt = lax.dot_general(dor, p_all[pl.ds(q0, GRID_W), :], (((0,), (0,)), ((), ())), preferred_element_type=F32)
                akt[hh, par, pl.ds(t0, WIN // 128)] += jnp.stack([dkt[:, 128 * i:128 * (i + 1)] for i in range(WIN // 128)])
                avt[hh, par, pl.ds(t0, WIN // 128)] += jnp.stack([dvt[:, 128 * i:128 * (i + 1)] for i in range(WIN // 128)])
                return carry
            lax.fori_loop(0, rows, grads, 0, unroll=4)

        for hh in range(2):
            ls = slice(hh * NA_DH, (hh + 1) * NA_DH)
            for i in range(S // 128):
                for acc, dst in ((akt, dkn), (avt, dvs)):
                    odd = jnp.concatenate([acc[hh, 1, i][:, NA_DH:], acc[hh, 1, i + 1][:, :NA_DH]], axis=1)
                    dst[128 * i:128 * (i + 1), ls] = (acc[hh, 0, i] + odd).T

        for c in range(S // _RC):
            sl = slice(c * _RC, (c + 1) * _RC)
            for x_ref, g_ref, dn, dx_ref, dg_ref in ((q_ref, qg_ref, dqn, dq_ref, dqg_ref), (k_ref, kg_ref, dkn, dk_ref, dkg_ref)):
                x = x_ref[sl, :]
                r_ = lax.rsqrt(jnp.dot(x * x, seg_ref[...], precision=HI, preferred_element_type=F32) + EPS)
                xh = x * r_
                d = dn[sl, :]
                dxh = d * g_ref[...]
                mean = jnp.dot(dxh * xh, seg_ref[...], precision=HI, preferred_element_type=F32)
                dx_ref[sl, :] = (r_ * (dxh - xh * mean)).astype(BF16)
                dg_ref[...] += jnp.sum(d * xh, axis=0, keepdims=True)
            dv_ref[sl, :] = dvs[sl, :].astype(BF16)

    LW = 128
    blk = pl.BlockSpec((S, LW), lambda j, e: (e, j))
    vec = pl.BlockSpec((1, LW), lambda j, e: (0, j))
    bsp = pl.BlockSpec((2, NA_ROWS, GRID_W, WIN), lambda j, e: (j, 0, 0, 0))
    return pl.pallas_call(
        body, out_shape=(jax.ShapeDtypeStruct((T, GROUP_W), BF16),) * 3 + (
            jax.ShapeDtypeStruct((NA_HEADS, NA_ROWS, GRID_W, WIN), F32),
            jax.ShapeDtypeStruct((1, GROUP_W), F32), jax.ShapeDtypeStruct((1, GROUP_W), F32)),
        grid=(GROUP_W // LW, E),
        in_specs=[pl.BlockSpec((S, LW), lambda j, e: (e, B_Q // LW + j)),
                  pl.BlockSpec((S, LW), lambda j, e: (e, B_K // LW + j)),
                  pl.BlockSpec((S, LW), lambda j, e: (e, B_V // LW + j)),
                  vec, vec, bsp, pl.BlockSpec((LW, LW), lambda j, e: (0, 0)), blk],
        out_specs=(blk, blk, blk, bsp, vec, vec),
        scratch_shapes=[pltpu.VMEM((2, S, NA_DH), BF16)] * 4 + [pltpu.VMEM((S, LW), F32)] * 3
        + [pltpu.VMEM((2, 2, S // 128 + 1, NA_DH, 128), F32)] * 2
        + [pltpu.VMEM((S, WIN), F32)] * 2 + [pltpu.VMEM((S, WIN), BF16)] * 2,
        name=name, compiler_params=_cp(("parallel", "arbitrary")))(z, z, z, qg, kg, bias, seg, do)


def _na_rpb_grad(dbias, name):
    e_np, a_np, _ = _na_onehots()
    H = NA_HEADS
    x = dbias.reshape(H, NA_ROWS, GRID_W, NA_ROWS, GRID_W)
    x = jnp.transpose(x, (0, 1, 3, 2, 4)).reshape(H, NA_ROWS * NA_ROWS, GRID_W * GRID_W)

    def body(x_ref, e_ref, a_ref, o_ref):
        y = jnp.dot(x_ref[0], e_ref[...], precision=HI, preferred_element_type=F32)
        o_ref[0] = jnp.dot(a_ref[...], y, precision=HI, preferred_element_type=F32)

    out = pl.pallas_call(
        body, out_shape=jax.ShapeDtypeStruct((H, 16, 128), F32), grid=(H,),
        in_specs=[pl.BlockSpec((1, 64, GRID_W * GRID_W), lambda h: (h, 0, 0)),
                  pl.BlockSpec((GRID_W * GRID_W, 128), lambda h: (0, 0)),
                  pl.BlockSpec((16, 64), lambda h: (0, 0))],
        out_specs=pl.BlockSpec((1, 16, 128), lambda h: (h, 0, 0)),
        name=name, compiler_params=_cp(("parallel",)))(x, jnp.asarray(e_np), jnp.asarray(a_np))
    return out[:, :2 * NA_ROWS - 1, :2 * NA_COLS - 1]


_HK = GLA_HEADS * GLA_DK
_HV = GLA_HEADS * GLA_DV


def _gla_consts(reverse):
    i = np.arange(CHUNK)
    tri = (i[:, None] <= i[None, :]) if reverse else (i[:, None] >= i[None, :])
    j = np.arange(_HK)
    oseg = (j[:, None] // GLA_DK == j[None, :] // GLA_DK)
    return (jnp.asarray(tri.astype(np.float32)), jnp.asarray(tri.T.astype(np.float32)), jnp.asarray(oseg.astype(np.float32), BF16))


def _log_decay(lr, a2, ab):
    zg = jnp.dot(lr, a2, precision=HI, preferred_element_type=F32) + ab
    g = (jnp.minimum(zg, 0.0) - jnp.log(1.0 + jnp.exp(-jnp.abs(zg)))) * (1.0 / GLA_TAU)
    return zg, g


def _dotf(a, b, dn):
    return lax.dot_general(a, b, dn, precision=HI, preferred_element_type=F32)


def _dotb(a, b, dn):
    return lax.dot_general(a.astype(BF16), b.astype(BF16), dn, preferred_element_type=F32)


_COLS = 4
_SUB = 16
_NSUB = CHUNK // _SUB


def _gla_cross_blocks(reverse):
    return range(0, _NSUB - 1) if reverse else range(1, _NSUB)


def _gla_cross_terms(s, reverse, b_s, q_s, k_s, oseg_ref):
    r0 = s * _SUB
    ref = r0 + (_SUB - 1 if reverse else 0)
    bref = b_s[ref:ref + 1, :]
    rowj = lax.broadcasted_iota(jnp.int32, (CHUNK, 1), 0)
    seen = (rowj >= r0 + _SUB) if reverse else (rowj < r0)
    ek = jnp.where(seen, jnp.exp(jnp.minimum(bref - b_s[...], 0.0)), 0.0)
    kt = k_s[...] * ek
    eq = jnp.exp(jnp.minimum(b_s[r0:r0 + _SUB, :] - bref, 0.0))
    qt = q_s[r0:r0 + _SUB, :] * eq
    nmat = jnp.concatenate([kt.astype(BF16)] * GLA_HEADS, axis=0) * oseg_ref[...]
    return qt, eq, kt, ek, nmat


_NN = (((1,), (0,)), ((), ()))
_NT = (((1,), (1,)), ((), ()))
_TN = (((0,), (0,)), ((), ()))


_DT = 256


def _gla_block_tri(reverse):
    i = np.arange(_DT)
    same = i[:, None] // CHUNK == i[None, :] // CHUNK
    tri = (i[:, None] <= i[None, :]) if reverse else (i[:, None] >= i[None, :])
    return (tri & same).astype(np.float32)


def _gla_decay_fwd(z, a2, ab, reverse, name):
    T = z.shape[0]
    nc = _DT // CHUNK

    def body(lr_ref, a2_ref, ab_ref, m_ref, b_ref, ec_ref):
        _, g = _log_decay(lr_ref[...], a2_ref[...], ab_ref[...])
        b_ref[...] = _dotf(m_ref[...], g, _NN)
        for c in range(nc):
            ec_ref[c] = jnp.exp(_dotf(g[c * CHUNK:(c + 1) * CHUNK, :], jnp.ones((CHUNK, GLA_DV), F32), _TN))

    return pl.pallas_call(
        body, out_shape=(jax.ShapeDtypeStruct((T, _HK), F32), jax.ShapeDtypeStruct((T // CHUNK, _HK, GLA_DV), F32)),
        grid=(T // _DT,),
        in_specs=[pl.BlockSpec((_DT, 128), lambda i: (i, LR_OFF // 128)),
                  pl.BlockSpec((128, _HK), lambda i: (0, 0)),
                  pl.BlockSpec((1, _HK), lambda i: (0, 0)),
                  pl.BlockSpec((_DT, _DT), lambda i: (0, 0))],
        out_specs=(pl.BlockSpec((_DT, _HK), lambda i: (i, 0)), pl.BlockSpec((nc, _HK, GLA_DV), lambda i: (i, 0, 0))),
        name=name, compiler_params=_cp(("parallel",)))(z, a2, ab, jnp.asarray(_gla_block_tri(reverse)))


def _gla_decay_bwd(z, a2_f, ab_f, a2_b, ab_b, db_f, db_b, name):
    T = z.shape[0]

    def body(lr_ref, a2f_ref, abf_ref, a2b_ref, abb_ref, mf_ref, mb_ref, dbf_ref, dbb_ref,
             dlr_ref, da2f_ref, dabf_ref, da2b_ref, dabb_ref):
        @pl.when(pl.program_id(0) == 0)
        def _():
            for r in (da2f_ref, dabf_ref, da2b_ref, dabb_ref):
                r[...] = jnp.zeros_like(r)

        lr = lr_ref[...]
        dlr = jnp.zeros((_DT, 128), F32)
        for a2_ref, ab_ref, mt_ref, db_ref, da2_ref, dab_ref in ((a2f_ref, abf_ref, mf_ref, dbf_ref, da2f_ref, dabf_ref),
                                                                 (a2b_ref, abb_ref, mb_ref, dbb_ref, da2b_ref, dabb_ref)):
            zg, _ = _log_decay(lr, a2_ref[...], ab_ref[...])
            dg = _dotf(mt_ref[...], db_ref[...], _NN)
            dzg = dg * (1.0 / (1.0 + jnp.exp(zg))) * (1.0 / GLA_TAU)
            dlr = dlr + _dotf(dzg, a2_ref[...], _NT)
            da2_ref[...] += _dotf(lr, dzg, _TN)
            dab_ref[...] += jnp.sum(dzg, axis=0, keepdims=True)
        dlr_ref[...] = dlr.astype(BF16)

    a2s = pl.BlockSpec((128, _HK), lambda i: (0, 0))
    abs_ = pl.BlockSpec((1, _HK), lambda i: (0, 0))
    ms = pl.BlockSpec((_DT, _DT), lambda i: (0, 0))
    row = pl.BlockSpec((_DT, _HK), lambda i: (i, 0))
    return pl.pallas_call(
        body, out_shape=(jax.ShapeDtypeStruct((T, 128), BF16), jax.ShapeDtypeStruct((128, _HK), F32), jax.ShapeDtypeStruct((1, _HK), F32),
                         jax.ShapeDtypeStruct((128, _HK), F32), jax.ShapeDtypeStruct((1, _HK), F32)),
        grid=(T // _DT,),
        in_specs=[pl.BlockSpec((_DT, 128), lambda i: (i, LR_OFF // 128)), a2s, abs_, a2s, abs_, ms, ms, row, row],
        out_specs=(pl.BlockSpec((_DT, 128), lambda i: (i, 0)), a2s, abs_, a2s, abs_),
        name=name, compiler_params=_cp(("arbitrary",)))(
            z, a2_f, ab_f, a2_b, ab_b, jnp.asarray(_gla_block_tri(False).T), jnp.asarray(_gla_block_tri(True).T), db_f, db_b)


def _gla_fwd(z, b_all, ecol, S, reverse, name):
    T = z.shape[0]
    E = T // S
    n = S // CHUNK
    _, _, oseg = _gla_consts(reverse)
    last = 0 if reverse else CHUNK - 1

    def body(q_ref, k_ref, v_ref, b_ref, ec_ref, oseg_ref, o_ref, a_ref, st_ref, st, b_s, q_s, k_s):
        @pl.when(pl.program_id(1) == 0)
        def _():
            st[...] = jnp.zeros_like(st)

        q = q_ref[...] * (GLA_DK ** -0.5)
        k = k_ref[...]
        v = v_ref[...]
        b = b_ref[...]
        bl_row = b_ref[last:last + 1, :]
        e_col = ec_ref[0]
        b_s[...] = b
        q_s[...] = q
        k_s[...] = k
        lane = lax.broadcasted_iota(jnp.int32, (1, _HK), 1) % GLA_DK

        rowi = lax.broadcasted_iota(jnp.int32, (CHUNK, 1), 0)
        blk0 = (rowi // _SUB) * _SUB

        def cols(jj, a):
            ts = []
            for u in range(_COLS):
                jp = jj * _COLS + u
                tiles = []
                for s in range(_NSUB):
                    rs_ = slice(s * _SUB, (s + 1) * _SUB)
                    bj = b_s[pl.ds(s * _SUB + jp, 1), :]
                    kj = k_s[pl.ds(s * _SUB + jp, 1), :]
                    tiles.append(q_s[rs_, :] * jnp.exp(jnp.minimum(b_s[rs_, :] - bj, 0.0)) * kj)
                ts.append(jnp.concatenate(tiles, axis=0).astype(BF16))
            r = jnp.dot(jnp.concatenate(ts, axis=0), oseg_ref[...], preferred_element_type=F32)
            for u in range(_COLS):
                a = jnp.where(lane == blk0 + (jj * _COLS + u), r[u * CHUNK:(u + 1) * CHUNK, :], a)
            return a

        a = lax.fori_loop(0, _SUB // _COLS, cols, jnp.zeros((CHUNK, _HK), F32))
        keep = (rowi <= lane) if reverse else (rowi >= lane)
        a = jnp.where(keep, a, 0.0)
        cross = []
        for s in range(_NSUB):
            if s in _gla_cross_blocks(reverse):
                qt, _, _, _, nmat = _gla_cross_terms(s, reverse, b_s, q_s, k_s, oseg_ref)
                cross.append(lax.dot_general(qt.astype(BF16), nmat, _NT, preferred_element_type=F32))
            else:
                cross.append(jnp.zeros((_SUB, _HK), F32))
        a = a + jnp.concatenate(cross, axis=0)
        a_ref[...] = a
        st_ref[0] = st[...]
        qb = q * jnp.exp(b)
        kd = k * jnp.exp(bl_row - b)
        for h in range(GLA_HEADS):
            ks_ = slice(h * GLA_DK, (h + 1) * GLA_DK)
            vs_ = slice(h * GLA_DV, (h + 1) * GLA_DV)
            s_h = st[ks_, :]
            o_ref[:, vs_] = _dotb(qb[:, ks_], s_h, _NN) + _dotb(a[:, ks_], v[:, vs_], _NN)
            st[ks_, :] = s_h * e_col[ks_, :] + _dotb(kd[:, ks_], v[:, vs_], _TN)

    def rowblk(e, c):
        return e * n + ((n - 1 - c) if reverse else c)

    return pl.pallas_call(
        body, out_shape=(jax.ShapeDtypeStruct((T, _HV), F32), jax.ShapeDtypeStruct((T, _HK), F32),
                         jax.ShapeDtypeStruct((T // CHUNK, _HK, GLA_DV), F32)),
        grid=(E, n),
        in_specs=[pl.BlockSpec((CHUNK, _HK), lambda e, c: (rowblk(e, c), C_Q // _HK)),
                  pl.BlockSpec((CHUNK, _HK), lambda e, c: (rowblk(e, c), C_K // _HK)),
                  pl.BlockSpec((CHUNK, _HV), lambda e, c: (rowblk(e, c), C_V // _HV)),
                  pl.BlockSpec((CHUNK, _HK), lambda e, c: (rowblk(e, c), 0)),
                  pl.BlockSpec((1, _HK, GLA_DV), lambda e, c: (rowblk(e, c), 0, 0)),
                  pl.BlockSpec((_HK, _HK), lambda e, c: (0, 0))],
        out_specs=(pl.BlockSpec((CHUNK, _HV), lambda e, c: (rowblk(e, c), 0)),
                   pl.BlockSpec((CHUNK, _HK), lambda e, c: (rowblk(e, c), 0)),
                   pl.BlockSpec((1, _HK, GLA_DV), lambda e, c: (rowblk(e, c), 0, 0))),
        scratch_shapes=[pltpu.VMEM((_HK, GLA_DV), F32)] + [pltpu.VMEM((CHUNK, _HK), F32)] * 3,
        name=name, compiler_params=_cp(("parallel", "arbitrary")))(z, z, z, b_all, ecol, oseg)


def _gla_bwd(z, b_all, ecol, att, states, do, prev, S, reverse, name):
    T = z.shape[0]
    E = T // S
    n = S // CHUNK
    _, _, oseg = _gla_consts(reverse)
    has_prev = prev is not None
    odt = BF16 if has_prev else F32
    last = 0 if reverse else CHUNK - 1

    def body(*refs):
        (q_ref, k_ref, v_ref, b_ref, ec_ref, oseg_ref, att_ref, st_ref, do_ref) = refs[:9]
        refs = refs[9:]
        if has_prev:
            pq_ref, pk_ref, pv_ref = refs[:3]
            refs = refs[3:]
        (dq_ref, dk_ref, dv_ref, db_ref, dst, b_s, q_s, k_s, da_s, dqb_s, dkd_s, dk3_s, dbn_s, dsp_s) = refs

        @pl.when(pl.program_id(1) == 0)
        def _():
            dst[...] = jnp.zeros_like(dst)

        q = q_ref[...] * (GLA_DK ** -0.5)
        k = k_ref[...]
        v = v_ref[...]
        b = b_ref[...]
        bl_row = b_ref[last:last + 1, :]
        eb = jnp.exp(b)
        ekd = jnp.exp(bl_row - b)
        qb = q * eb
        kd = k * ekd
        b_s[...] = b
        q_s[...] = q
        k_s[...] = k
        att = att_ref[...]
        s_all = st_ref[0]
        dsn = dst[...]
        e_col = ec_ref[0]
        do = do_ref[...]
        lane = lax.broadcasted_iota(jnp.int32, (1, _HK), 1) % GLA_DK
        rowi = lax.broadcasted_iota(jnp.int32, (CHUNK, 1), 0)
        keep = (rowi <= lane) if reverse else (rowi >= lane)
        for h in range(GLA_HEADS):
            ks_ = slice(h * GLA_DK, (h + 1) * GLA_DK)
            vs_ = slice(h * GLA_DV, (h + 1) * GLA_DV)
            do_h = do[:, vs_]
            s_h = s_all[ks_, :]
            dsn_h = dsn[ks_, :]
            dqb_s[:, ks_] = _dotb(do_h, s_h, _NT)
            dsp_s[ks_, :] = _dotb(qb[:, ks_], do_h, _TN) + dsn_h * e_col[ks_, :]
            da_s[:, ks_] = _dotb(do_h, v[:, vs_], _NT)
            dv_h = _dotb(att[:, ks_], do_h, _TN) + _dotb(kd[:, ks_], dsn_h, _NN)
            if has_prev:
                dv_h = dv_h + pv_ref[:, vs_]
            dv_ref[:, vs_] = dv_h.astype(odt)
            dkd_s[:, ks_] = _dotb(v[:, vs_], dsn_h, _NT)
        da_s[...] = jnp.where(keep, da_s[...], 0.0)
        dqb = dqb_s[...]
        dkd = dkd_s[...]
        x = dsn * s_all * e_col
        dbl_row = _dotf(jnp.ones((8, GLA_DV), F32), x, _NT)[0:1, :] + jnp.sum(dkd * kd, axis=0, keepdims=True)

        blk0 = (rowi // _SUB) * _SUB

        def cols(jj, carry):
            dq3, db3 = list(carry[:_NSUB]), list(carry[_NSUB:])
            sel = [jnp.where(lane == blk0 + (jj * _COLS + u), da_s[...], 0.0).astype(BF16) for u in range(_COLS)]
            dcols = jnp.dot(jnp.concatenate(sel, axis=0), oseg_ref[...], preferred_element_type=F32)
            for u in range(_COLS):
                jp = jj * _COLS + u
                for s in range(_NSUB):
                    rs_ = slice(s * _SUB, (s + 1) * _SUB)
                    bj = b_s[pl.ds(s * _SUB + jp, 1), :]
                    kj = k_s[pl.ds(s * _SUB + jp, 1), :]
                    tm_ = dcols[u * CHUNK + s * _SUB:u * CHUNK + (s + 1) * _SUB, :] * jnp.exp(jnp.minimum(b_s[rs_, :] - bj, 0.0))
                    dq3[s] = dq3[s] + tm_ * kj
                    gq = tm_ * q_s[rs_, :]
                    dk3_s[pl.ds(s * _SUB + jp, 1), :] = jnp.sum(gq, axis=0, keepdims=True)
                    w = gq * kj
                    dbn_s[pl.ds(s * _SUB + jp, 1), :] = jnp.sum(w, axis=0, keepdims=True)
                    db3[s] = db3[s] + w
            return tuple(dq3) + tuple(db3)

        zero = jnp.zeros((_SUB, _HK), F32)
        acc = lax.fori_loop(0, _SUB // _COLS, cols, (zero,) * (2 * _NSUB))
        dq3 = jnp.concatenate(acc[:_NSUB], axis=0)
        db3 = jnp.concatenate(acc[_NSUB:], axis=0)
        head = lax.broadcasted_iota(jnp.int32, (1, _HK), 1) // GLA_DK
        dq_x, db_x = [], []
        dk_x = jnp.zeros((CHUNK, _HK), F32)
        db_k = jnp.zeros((CHUNK, _HK), F32)
        for s in range(_NSUB):
            if s not in _gla_cross_blocks(reverse):
                dq_x.append(zero)
                db_x.append(zero)
                continue
            r0 = s * _SUB
            qt, eq, kt, ek, nmat = _gla_cross_terms(s, reverse, b_s, q_s, k_s, oseg_ref)
            seen = (lane >= r0 + _SUB) if reverse else (lane < r0)
            dax = jnp.where(seen, da_s[r0:r0 + _SUB, :], 0.0).astype(BF16)
            dqt = jnp.dot(dax, nmat, preferred_element_type=F32)
            full = lax.dot_general(dax, qt.astype(BF16), _TN, preferred_element_type=F32)
            dkt = full[0:CHUNK, :]
            for h in range(1, GLA_HEADS):
                dkt = jnp.where(head == h, full[h * CHUNK:(h + 1) * CHUNK, :], dkt)
            dq_x.append(dqt * eq)
            db_x.append(dqt * qt)
            dk_x = dk_x + dkt * ek
            db_k = db_k + dkt * kt
        dq = (dqb * eb + dq3 + jnp.concatenate(dq_x, axis=0)) * (GLA_DK ** -0.5)
        dk = dkd * ekd + dk3_s[...] + dk_x
        db = dqb * qb - dkd * kd + db3 - dbn_s[...] + jnp.concatenate(db_x, axis=0) - db_k
        db_ref[...] = jnp.where(rowi == last, db + dbl_row, db)
        if has_prev:
            dq = dq + pq_ref[...]
            dk = dk + pk_ref[...]
        dq_ref[...] = dq.astype(odt)
        dk_ref[...] = dk.astype(odt)
        dst[...] = dsp_s[...]

    def rowblk(e, c):
        return e * n + (c if reverse else (n - 1 - c))

    hk = pl.BlockSpec((CHUNK, _HK), lambda e, c: (rowblk(e, c), 0))
    hv = pl.BlockSpec((CHUNK, _HV), lambda e, c: (rowblk(e, c), 0))
    stb = pl.BlockSpec((1, _HK, GLA_DV), lambda e, c: (rowblk(e, c), 0, 0))
    in_specs = [pl.BlockSpec((CHUNK, _HK), lambda e, c: (rowblk(e, c), C_Q // _HK)),
                pl.BlockSpec((CHUNK, _HK), lambda e, c: (rowblk(e, c), C_K // _HK)),
                pl.BlockSpec((CHUNK, _HV), lambda e, c: (rowblk(e, c), C_V // _HV)),
                hk, stb, pl.BlockSpec((_HK, _HK), lambda e, c: (0, 0)), hk, stb, hv]
    args = [z, z, z, b_all, ecol, oseg, att, states, do]
    if has_prev:
        in_specs += [hk, hk, hv]
        args += list(prev)
    return pl.pallas_call(
        body, out_shape=(jax.ShapeDtypeStruct((T, _HK), odt), jax.ShapeDtypeStruct((T, _HK), odt),
                         jax.ShapeDtypeStruct((T, _HV), odt), jax.ShapeDtypeStruct((T, _HK), F32)),
        grid=(E, n), in_specs=in_specs, out_specs=(hk, hk, hv, hk),
        scratch_shapes=[pltpu.VMEM((_HK, GLA_DV), F32)] + [pltpu.VMEM((CHUNK, _HK), F32)] * 8 + [pltpu.VMEM((_HK, GLA_DV), F32)],
        name=name, compiler_params=_cp(("parallel", "arbitrary")))(*args)


def _gla_norm_fwd(of, ob, og, name):
    T = of.shape[0]
    tm = 256

    def body(f_ref, b_ref, g_ref, o_ref):
        for h in range(GLA_HEADS):
            vs_ = slice(h * GLA_DV, (h + 1) * GLA_DV)
            o = f_ref[:, vs_] + b_ref[:, vs_]
            o_ref[:, vs_] = o * lax.rsqrt(jnp.mean(o * o, axis=-1, keepdims=True) + EPS) * g_ref[:, vs_]

    row = pl.BlockSpec((tm, _HV), lambda i: (i, 0))
    vec = pl.BlockSpec((1, _HV), lambda i: (0, 0))
    return pl.pallas_call(body, out_shape=jax.ShapeDtypeStruct((T, _HV), F32), grid=(T // tm,),
                          in_specs=[row, row, vec], out_specs=row, name=name, compiler_params=_cp(("parallel",)))(of, ob, og)


def _gla_norm_bwd(of, ob, og, dpre, name):
    T = of.shape[0]
    tm = 256

    def body(f_ref, b_ref, g_ref, dp_ref, do_ref, dg_ref):
        @pl.when(pl.program_id(0) == 0)
        def _():
            dg_ref[...] = jnp.zeros_like(dg_ref)

        for h in range(GLA_HEADS):
            vs_ = slice(h * GLA_DV, (h + 1) * GLA_DV)
            o = f_ref[:, vs_] + b_ref[:, vs_]
            r = lax.rsqrt(jnp.mean(o * o, axis=-1, keepdims=True) + EPS)
            xh = o * r
            dp = dp_ref[:, vs_]
            dxh = dp * g_ref[:, vs_]
            do_ref[:, vs_] = r * (dxh - xh * jnp.mean(dxh * xh, axis=-1, keepdims=True))
            dg_ref[:, vs_] += jnp.sum(dp * xh, axis=0, keepdims=True)

    row = pl.BlockSpec((tm, _HV), lambda i: (i, 0))
    vec = pl.BlockSpec((1, _HV), lambda i: (0, 0))
    return pl.pallas_call(
        body, out_shape=(jax.ShapeDtypeStruct((T, _HV), F32), jax.ShapeDtypeStruct((1, _HV), F32)), grid=(T // tm,),
        in_specs=[row, row, vec, row], out_specs=(row, vec), name=name, compiler_params=_cp(("arbitrary",)))(of, ob, og, dpre)


_ANY = pl.BlockSpec(memory_space=pl.ANY)


def _coords():
    return lax.axis_index("x"), lax.axis_index("y"), lax.axis_index("c")


def _other_chips(x, y):
    return ((1 - x, y), (x, 1 - y), (1 - x, 1 - y))


def _gather_weights(arrays, chunks, name):
    n = len(arrays)
    pieces = []
    for k in range(max(chunks)):
        for i, a in enumerate(arrays):
            if k < chunks[i]:
                rc = a.shape[1] // chunks[i]
                pieces.append((i, k * rc, rc))
    m = len(pieces)

    def body(*refs):
        srcs, dsts = refs[:n], refs[n:2 * n]
        send_sems, recv_sems, local_sems = refs[2 * n:]
        x, y, c = _coords()
        me = 2 * x + y
        loc = [pltpu.make_async_copy(s, d.at[me], local_sems.at[i]) for i, (s, d) in enumerate(zip(srcs, dsts))]
        for cp in loc:
            cp.start()
        ici = []
        for p, (i, r0, rc) in enumerate(pieces):
            for j, (px, py) in enumerate(_other_chips(x, y)):
                ici.append(pltpu.make_async_remote_copy(
                    src_ref=srcs[i].at[c, pl.ds(r0, rc)], dst_ref=dsts[i].at[me, c, pl.ds(r0, rc)],
                    send_sem=send_sems.at[3 * p + j], recv_sem=recv_sems.at[3 * p + j],
                    device_id=(px, py, c), device_id_type=MESH))
        for cp in ici:
            cp.start()
        fwd = []
        for p, (i, r0, rc) in enumerate(pieces):
            for j, (px, py) in enumerate(_other_chips(x, y)):
                ici[3 * p + j].wait_recv()
                part = dsts[i].at[2 * px + py, c, pl.ds(r0, rc)]
                cp = pltpu.make_async_remote_copy(
                    src_ref=part, dst_ref=part, send_sem=send_sems.at[3 * m + 3 * p + j], recv_sem=recv_sems.at[3 * m + 3 * p + j],
                    device_id=(x, y, 1 - c), device_id_type=MESH)
                cp.start()
                fwd.append(cp)
        for cp in fwd:
            cp.wait_recv()
        for cp in ici + fwd:
            cp.wait_send()
        for cp in loc:
            cp.wait()

    return pl.pallas_call(
        body, out_shape=tuple(jax.ShapeDtypeStruct((4,) + a.shape, a.dtype) for a in arrays),
        in_specs=[_ANY] * n, out_specs=(_ANY,) * n,
        scratch_shapes=[pltpu.SemaphoreType.DMA((6 * m,)), pltpu.SemaphoreType.DMA((6 * m,)), pltpu.SemaphoreType.DMA((n,))],
        name=name)(*arrays)


def _sibling_exchange(layered, whole, name):
    nl, n = len(layered), len(layered) + len(whole)

    def body(*refs):
        srcs, dsts = refs[:n], refs[n:2 * n]
        send_sems, recv_sems = refs[2 * n:]
        x, y, c = _coords()
        rem = [pltpu.make_async_remote_copy(src_ref=(s.at[1 - c] if i < nl else s), dst_ref=d, send_sem=send_sems.at[i],
                                            recv_sem=recv_sems.at[i], device_id=(x, y, 1 - c), device_id_type=MESH)
               for i, (s, d) in enumerate(zip(srcs, dsts))]
        for cp in rem:
            cp.start()
        for cp in rem:
            cp.wait()

    outs = [jax.ShapeDtypeStruct(a.shape[1:], a.dtype) for a in layered] + [jax.ShapeDtypeStruct(a.shape, a.dtype) for a in whole]
    return pl.pallas_call(
        body, out_shape=tuple(outs), in_specs=[_ANY] * n, out_specs=(_ANY,) * n,
        scratch_shapes=[pltpu.SemaphoreType.DMA((n,)), pltpu.SemaphoreType.DMA((n,))], name=name)(*layered, *whole)


def _chip_exchange(scatter, bcast, name):
    ns, n = len(scatter), len(scatter) + len(bcast)

    def body(*refs):
        srcs, dsts = refs[:n], refs[n:2 * n]
        send_sems, recv_sems, local_sems = refs[2 * n:]
        x, y, c = _coords()
        me = 2 * x + y
        loc = [pltpu.make_async_copy((s.at[me] if i < ns else s), d.at[me], local_sems.at[i])
               for i, (s, d) in enumerate(zip(srcs, dsts))]
        for cp in loc:
            cp.start()
        rem = []
        for j, (px, py) in enumerate(_other_chips(x, y)):
            for i, (s, d) in enumerate(zip(srcs, dsts)):
                rem.append(pltpu.make_async_remote_copy(
                    src_ref=(s.at[2 * px + py] if i < ns else s), dst_ref=d.at[me], send_sem=send_sems.at[n * j + i],
                    recv_sem=recv_sems.at[n * j + i], device_id=(px, py, c), device_id_type=MESH))
        for cp in rem:
            cp.start()
        for cp in rem:
            cp.wait()
        for cp in loc:
            cp.wait()

    outs = [jax.ShapeDtypeStruct(a.shape, a.dtype) for a in scatter] + [jax.ShapeDtypeStruct((4,) + a.shape, a.dtype) for a in bcast]
    return pl.pallas_call(
        body, out_shape=tuple(outs), in_specs=[_ANY] * n, out_specs=(_ANY,) * n,
        scratch_shapes=[pltpu.SemaphoreType.DMA((3 * n,)), pltpu.SemaphoreType.DMA((3 * n,)), pltpu.SemaphoreType.DMA((n,))],
        name=name)(*scatter, *bcast)


_EW_BLOCK_BYTES = 2 * 1024 * 1024


def _tile2d(R, C):
    if R % 256 == 0 and 256 * C * 4 <= _EW_BLOCK_BYTES:
        return 256, C
    bc = 256 if C % 256 == 0 else C
    for br in range(R, 0, -1):
        if R % br == 0 and (br % 8 == 0 or br == R) and br * bc * 4 <= _EW_BLOCK_BYTES:
            return br, bc
    return R, bc


def _sum_slots(r, name):
    n, R, C = r.shape
    br, bc = _tile2d(R, C)

    def body(r_ref, o_ref):
        acc = r_ref[0].astype(F32)
        for i in range(1, n):
            acc = acc + r_ref[i].astype(F32)
        o_ref[...] = acc

    return pl.pallas_call(body, out_shape=jax.ShapeDtypeStruct((R, C), F32), grid=(R // br, C // bc),
                          in_specs=[pl.BlockSpec((n, br, bc), lambda i, j: (0, i, j))],
                          out_specs=pl.BlockSpec((br, bc), lambda i, j: (i, j)),
                          name=name, compiler_params=_cp(("parallel", "parallel")))(r)


_SMEM = pl.BlockSpec(memory_space=pltpu.SMEM)


def _add2(a, b, out_dtype, name, pick=None):
    R, C = b.shape
    br, bc = _tile2d(R, C)
    blk = pl.BlockSpec((br, bc), lambda i, j: (i, j))
    if pick is None:
        def body(a_ref, b_ref, o_ref):
            o_ref[...] = (a_ref[...].astype(F32) + b_ref[...].astype(F32)).astype(out_dtype)
        in_specs, args = [blk, blk], (a, b)
    else:
        def body(c_ref, a_ref, b_ref, o_ref):
            av = jnp.where(c_ref[0] == 0, a_ref[0], a_ref[1])
            o_ref[...] = (av.astype(F32) + b_ref[...].astype(F32)).astype(out_dtype)
        in_specs = [_SMEM, pl.BlockSpec((2, br, bc), lambda i, j: (0, i, j)), blk]
        args = (pick.reshape(1).astype(jnp.int32), a, b)
    return pl.pallas_call(body, out_shape=jax.ShapeDtypeStruct((R, C), out_dtype), grid=(R // br, C // bc), in_specs=in_specs,
                          out_specs=blk, name=name, compiler_params=_cp(("parallel", "parallel")))(*args)


def _adamw_math(w, g, m, v):
    m = ADAM_B1 * m + (1.0 - ADAM_B1) * g
    v = ADAM_B2 * v + (1.0 - ADAM_B2) * (g * g)
    m_hat = m / (1.0 - ADAM_B1 ** ADAM_STEP)
    v_hat = v / (1.0 - ADAM_B2 ** ADAM_STEP)
    delta = -ADAM_LR * (m_hat / (jnp.sqrt(v_hat) + ADAM_EPS) + ADAM_WD * w)
    return delta, m, v


def _adamw(w, gs, m, v, name, pick=None):
    R, C = w.shape
    br, bc = _tile2d(R, C)
    blk = pl.BlockSpec((br, bc), lambda i, j: (i, j))
    if pick is None:
        g_specs = [pl.BlockSpec((g.shape[0], br, bc), lambda i, j: (0, i, j)) if g.ndim == 3 else blk for g in gs]
        lead = ()
    else:
        nb = (R // 2) // br
        assert nb * br * 2 == R
        g_specs = [pl.BlockSpec((br, bc), lambda i, j: (i % nb, j))] * 2
        lead = (pick.reshape(1).astype(jnp.int32),)

    def body(*refs):
        if pick is not None:
            c_ref, refs = refs[0], refs[1:]
        w_ref = refs[0]
        g_refs = refs[1:1 + len(gs)]
        m_ref, v_ref, g_out, d_out, m_out, v_out = refs[1 + len(gs):]
        if pick is None:
            g = None
            for gr in g_refs:
                parts = [gr[i] for i in range(gr.shape[0])] if len(gr.shape) == 3 else [gr[...]]
                for p in parts:
                    g = p if g is None else g + p
        else:
            g = jnp.where(pl.program_id(0) // nb == c_ref[0], g_refs[0][...], g_refs[1][...])
        d, mn, vn = _adamw_math(w_ref[...], g, m_ref[...], v_ref[...])
        g_out[...] = g
        d_out[...] = d
        m_out[...] = mn
        v_out[...] = vn

    return pl.pallas_call(
        body, out_shape=tuple(jax.ShapeDtypeStruct((R, C), F32) for _ in range(4)), grid=(R // br, C // bc),
        in_specs=[_SMEM] * len(lead) + [blk] + g_specs + [blk, blk], out_specs=(blk,) * 4, name=name,
        compiler_params=_cp(("parallel", "parallel")))(*lead, w, *gs, m, v)


WEIGHTS = ("norm_g", "w_in", "conv_w", "conv_b", "conv_ln_g", "conv_ln_b", "na_q_g", "na_k_g", "na_rpb", "gla_a2_f",
           "gla_ab_f", "gla_a2_b", "gla_ab_b", "gla_o_g", "pool_w", "pool_scale", "w_out")
_REPL = ("norm_g", "conv_b", "conv_ln_g", "conv_ln_b", "na_q_g", "na_k_g", "na_rpb", "gla_ab_f", "gla_ab_b", "gla_o_g",
         "pool_w", "pool_scale")
_SHARD_SMALL = ("conv_w", "gla_a2_f", "gla_a2_b")
_PACK_ROWS = 8 * 128


def _pack(arrs):
    flat = jnp.concatenate([a.reshape(-1) for a in arrs])
    n = -(-flat.shape[0] // _PACK_ROWS) * _PACK_ROWS
    return jnp.pad(flat, (0, n - flat.shape[0])).reshape(-1, 128)


def _unpack(p, shapes):
    flat = p.reshape(-1)
    out, o = [], 0
    for s in shapes:
        n = int(np.prod(s))
        out.append(flat[o:o + n].reshape(s))
        o += n
    return out


def _to_layout_rows(w):
    pad = jnp.zeros((w.shape[0], NZ - N_IN, w.shape[2]), w.dtype)
    return jnp.concatenate([w[:, :5120], w[:, 5152:6176], w[:, 5120:5152], pad], axis=1)


def _from_layout_rows(w):
    return jnp.concatenate([w[:, :5120], w[:, LR_OFF:LR_OFF + 32], w[:, 5120:LR_OFF]], axis=1)


def _reduce_gradients(p_a, p_b, small_g, ci):
    s_a, s_b, s_small = _sibling_exchange((p_a, p_b), (small_g,), "grad_to_sibling")
    flat = lambda a: a.reshape(a.shape[0], -1, a.shape[-1])
    c_a = _add2(flat(p_a), s_a.reshape(-1, s_a.shape[-1]), BF16, "chip_sum_a", pick=ci).reshape(s_a.shape)
    c_b = _add2(flat(p_b), s_b.reshape(-1, s_b.shape[-1]), BF16, "chip_sum_b", pick=ci).reshape(s_b.shape)
    c_small = _add2(small_g, s_small, F32, "chip_sum_small")
    r_a, r_b, r_small = _chip_exchange((c_a, c_b), (c_small,), "grad_to_owner")
    own_a = _sum_slots(r_a, "sum_a")
    own_b = _sum_slots(r_b, "sum_b")
    sib_a, sib_b = _sibling_exchange((), (own_a, own_b), "reduced_to_sibling")
    return (own_a, sib_a), (own_b, sib_b), r_small


def _layer_fwd(l, x, P, S, target=None):
    n = f"l{l}_"
    h = _rmsnorm_fwd(x, P["norm_g"], n + "rms_fwd")
    z = _matmul(h, P["w_in"], dims="nn", out_dtype=F32, tm=1024, tn=1280, tk=D_MODEL, name=n + "mm_z")
    yc = _conv_fwd(z, P["conv_w32"], P["conv_b"], S, n + "conv_fwd")
    pre_a = _ln_silu_fwd(yc, P["conv_ln_g"], P["conv_ln_b"], n + "ln_fwd")
    pre_b = _na_fwd(z, P["na_q_g"], P["na_k_g"], P["na_bias"], S, n + "na_fwd")
    bf, ecf = _gla_decay_fwd(z, P["a2_f"], P["gla_ab_f"], False, n + "gla_decay_f")
    bb, ecb = _gla_decay_fwd(z, P["a2_b"], P["gla_ab_b"], True, n + "gla_decay_b")
    of, af, sf = _gla_fwd(z, bf, ecf, S, False, n + "gla_fwd_f")
    ob, ab, sb = _gla_fwd(z, bb, ecb, S, True, n + "gla_fwd_b")
    pre_c = _gla_norm_fwd(of, ob, P["gla_o_g"], n + "gla_norm_fwd")
    pre_d = _pool_fwd(z, P["pool_w_bf"], P["pool_scale"], S, n + "pool_fwd")
    pres = (pre_a, pre_b, pre_c, pre_d)
    res = _out_proj_fwd(pres, z, P["w_out"], x, target, n + "out_proj")
    y, out = res[0], (res[1] if target is None else res[1:])
    return out, dict(x=x, h=h, z=z, yc=yc, pres=pres, of=of, af=af, sf=sf, ob=ob, ab=ab, sb=sb, y=y, bf=bf, ecf=ecf, bb=bb, ecb=ecb)


def _layer_bwd(l, dout, dout_bf, sv, P, S):
    n = f"l{l}_"
    z = sv["z"]
    T = z.shape[0]
    d_w_out = _matmul(sv["y"], dout_bf, dims="tn", out_dtype=BF16, tm=1024, tn=2048, tk=512, name=n + "mm_dwout")
    dpa, dpb, dpc, dpd, dga, dgb, dgc, dgd = _out_proj_bwd(dout_bf, P["w_out_t"], sv["pres"], z, n + "out_proj_bwd")
    dyc, d_ln_g, d_ln_b = _ln_silu_bwd(sv["yc"], P["conv_ln_g"], P["conv_ln_b"], dpa, n + "ln_bwd")
    dval, dglu, d_cw, d_cb = _conv_bwd(z, P["conv_w32"], dyc, S, n + "conv_bwd")
    dq, dk, dv, dbias, d_qg, d_kg = _na_bwd(z, P["na_q_g"], P["na_k_g"], P["na_bias"], dpb, S, n + "na_bwd")
    d_rpb = _na_rpb_grad(dbias, n + "na_rpb")
    do, d_og = _gla_norm_bwd(sv["of"], sv["ob"], P["gla_o_g"], dpc, n + "gla_norm_bwd")
    part = _gla_bwd(z, sv["bf"], sv["ecf"], sv["af"], sv["sf"], do, None, S, False, n + "gla_bwd_f")
    dcq, dck, dcv, db_b = _gla_bwd(z, sv["bb"], sv["ecb"], sv["ab"], sv["sb"], do, part[:3], S, True, n + "gla_bwd_b")
    dlr, d_a2f, d_abf, d_a2b, d_abb = _gla_decay_bwd(z, P["a2_f"], P["gla_ab_f"], P["a2_b"], P["gla_ab_b"], part[3], db_b,
                                                     n + "gla_decay_bwd")
    dd, d_pw, d_ps = _pool_bwd(z, P["pool_w_bf"], P["pool_scale"], dpd, S, n + "pool_bwd")
    dz = _concat_cols([dval, dglu, dga, dq, dk, dv, dgb, dcq, dck, dcv, dgc, dd, dgd, dlr], NZ, n + "dz_concat")
    dh = _matmul(dz, P["w_in_t"], dims="nn", out_dtype=F32, tm=1024, tn=1024, tk=3200, name=n + "mm_dh")
    d_w_in = _matmul(dz, sv["h"], dims="tn", out_dtype=BF16, tm=1280, tn=1024, tk=1024, name=n + "mm_dwin")
    dx, dx_bf, d_ng = _rmsnorm_bwd(sv["x"], P["norm_g"], dh, dout, n + "rms_bwd")
    grads = dict(norm_g=d_ng[0], w_in=d_w_in, conv_w=d_cw[:CONV_K], conv_b=d_cb[0], conv_ln_g=d_ln_g[0], conv_ln_b=d_ln_b[0],
                 na_q_g=d_qg.reshape(NA_HEADS, NA_DH), na_k_g=d_kg.reshape(NA_HEADS, NA_DH), na_rpb=d_rpb,
                 gla_a2_f=d_a2f[0:GLA_RANK], gla_ab_f=d_abf[0], gla_a2_b=d_a2b[GLA_RANK:2 * GLA_RANK], gla_ab_b=d_abb[0],
                 gla_o_g=d_og.reshape(GLA_HEADS, GLA_DV), pool_w=d_pw, pool_scale=d_ps[0], w_out=d_w_out)
    return dx, dx_bf, grads


def kernel(x, norm_g, w_in, conv_w, conv_b, conv_ln_g, conv_ln_b, na_q_g, na_k_g, na_rpb, gla_a2_f, gla_ab_f, gla_a2_b, gla_ab_b, gla_o_g, pool_w, pool_scale, w_out, loss_target, m_norm_g, m_w_in, m_conv_w, m_conv_b, m_conv_ln_g, m_conv_ln_b, m_na_q_g, m_na_k_g, m_na_rpb, m_gla_a2_f, m_gla_ab_f, m_gla_a2_b, m_gla_ab_b, m_gla_o_g, m_pool_w, m_pool_scale, m_w_out, v_norm_g, v_w_in, v_conv_w, v_conv_b, v_conv_ln_g, v_conv_ln_b, v_na_q_g, v_na_k_g, v_na_rpb, v_gla_a2_f, v_gla_ab_f, v_gla_a2_b, v_gla_ab_b, v_gla_o_g, v_pool_w, v_pool_scale, v_w_out):
    W = dict(norm_g=norm_g, w_in=w_in, conv_w=conv_w, conv_b=conv_b, conv_ln_g=conv_ln_g, conv_ln_b=conv_ln_b, na_q_g=na_q_g,
             na_k_g=na_k_g, na_rpb=na_rpb, gla_a2_f=gla_a2_f, gla_ab_f=gla_ab_f, gla_a2_b=gla_a2_b, gla_ab_b=gla_ab_b,
             gla_o_g=gla_o_g, pool_w=pool_w, pool_scale=pool_scale, w_out=w_out)
    M = dict(norm_g=m_norm_g, w_in=m_w_in, conv_w=m_conv_w, conv_b=m_conv_b, conv_ln_g=m_conv_ln_g, conv_ln_b=m_conv_ln_b,
             na_q_g=m_na_q_g, na_k_g=m_na_k_g, na_rpb=m_na_rpb, gla_a2_f=m_gla_a2_f, gla_ab_f=m_gla_ab_f, gla_a2_b=m_gla_a2_b,
             gla_ab_b=m_gla_ab_b, gla_o_g=m_gla_o_g, pool_w=m_pool_w, pool_scale=m_pool_scale, w_out=m_w_out)
    V = dict(norm_g=v_norm_g, w_in=v_w_in, conv_w=v_conv_w, conv_b=v_conv_b, conv_ln_g=v_conv_ln_g, conv_ln_b=v_conv_ln_b,
             na_q_g=v_na_q_g, na_k_g=v_na_k_g, na_rpb=v_na_rpb, gla_a2_f=v_gla_a2_f, gla_ab_f=v_gla_ab_f, gla_a2_b=v_gla_a2_b,
             gla_ab_b=v_gla_ab_b, gla_o_g=v_gla_o_g, pool_w=v_pool_w, pool_scale=v_pool_scale, w_out=v_w_out)
    E, S, D = x.shape
    T = E * S
    L = DEPTH
    xi, yi, ci = _coords()
    chip = 2 * xi + yi
    cw_sh, a2_sh = conv_w.shape[-1], gla_a2_f.shape[-1]

    small_sh = jnp.concatenate([
        jnp.pad(conv_w, ((0, 0), (0, 1), (0, 0))),
        jnp.pad(gla_a2_f, ((0, 0), (0, 0), (0, 128 - a2_sh))),
        jnp.pad(gla_a2_b, ((0, 0), (0, 0), (0, 128 - a2_sh)))], axis=1)
    w_in_tr, m_w_in_tr, v_w_in_tr = (jnp.transpose(a, (0, 2, 1)) for a in (w_in, m_w_in, v_w_in))
    g_win, g_wout, g_small = _gather_weights((w_in_tr.astype(BF16), w_out.astype(BF16), small_sh), (1, 2, 1), "gather_weights")
    w_in_t_full = _to_layout_rows(jnp.transpose(g_win, (1, 0, 2, 3)).reshape(L, N_IN, D))
    w_in_full = jnp.transpose(w_in_t_full, (0, 2, 1))
    w_out_full = jnp.transpose(g_wout, (1, 0, 2, 3)).reshape(L, D, D)
    conv_w_full = jnp.transpose(g_small[:, :, 0:32, :], (1, 2, 0, 3)).reshape(L, 32, 4 * cw_sh)
    a2f_full = jnp.transpose(g_small[:, :, 32:48, :a2_sh], (1, 2, 0, 3)).reshape(L, GLA_RANK, 4 * a2_sh)
    a2b_full = jnp.transpose(g_small[:, :, 48:64, :a2_sh], (1, 2, 0, 3)).reshape(L, GLA_RANK, 4 * a2_sh)

    params = []
    for l in range(L):
        params.append(dict(
            norm_g=norm_g[l][None], w_in=w_in_full[l], w_out=w_out_full[l], w_in_t=w_in_t_full[l], w_out_t=w_out_full[l].T,
            conv_w32=conv_w_full[l], conv_b=conv_b[l][None],
            conv_ln_g=conv_ln_g[l][None], conv_ln_b=conv_ln_b[l][None], na_q_g=na_q_g[l].reshape(1, GROUP_W),
            na_k_g=na_k_g[l].reshape(1, GROUP_W), na_bias=_na_bias(na_rpb[l], f"l{l}_na_bias"),
            a2_f=jnp.zeros((128, _HK), F32).at[0:GLA_RANK].set(a2f_full[l]),
            a2_b=jnp.zeros((128, _HK), F32).at[GLA_RANK:2 * GLA_RANK].set(a2b_full[l]),
            gla_ab_f=gla_ab_f[l][None], gla_ab_b=gla_ab_b[l][None], gla_o_g=gla_o_g[l].reshape(1, GROUP_W),
            pool_w_bf=pool_w[l].astype(BF16), pool_scale=pool_scale[l][None]))

    act = x.reshape(T, D)
    saved = []
    for l in range(L):
        act, sv = _layer_fwd(l, act, params[l], S, loss_target.reshape(T, D) if l == L - 1 else None)
        saved.append(sv)
    dact, dact_bf, loss_loc = act
    loss = lax.psum(loss_loc[0, 0], ("x", "y", "c"))
    grads = [None] * L
    for l in reversed(range(L)):
        dact, dact_bf, grads[l] = _layer_bwd(l, dact, dact_bf, saved[l], params[l], S)
    grad_x = dact.reshape(E, S, D)
    G = {k: jnp.stack([grads[l][k] for l in range(L)]) for k in WEIGHTS}

    cols_in, cols_out = N_IN // 4, D
    p_win = _from_layout_rows(G["w_in"]).reshape(L, 4, cols_in, D)
    p_wout = G["w_out"].reshape(L, 4, D // 4, D)
    small_names = _REPL + _SHARD_SMALL
    small_g = _pack([G[k] for k in small_names])
    g_in, g_out, r_small = _reduce_gradients(p_win, p_wout, small_g, ci)

    rows_in, rows_out = L * cols_in, L * (D // 4)
    res = {}
    res["w_in"] = [jnp.transpose(a.reshape(L, cols_in, D), (0, 2, 1)) for a in _adamw(
        w_in_tr.reshape(rows_in, D), g_in, m_w_in_tr.reshape(rows_in, D), v_w_in_tr.reshape(rows_in, D), "adamw_w_in", pick=ci)]
    res["w_out"] = [a.reshape(L, D // 4, D) for a in _adamw(
        w_out.reshape(rows_out, cols_out), g_out, m_w_out.reshape(rows_out, cols_out),
        v_w_out.reshape(rows_out, cols_out), "adamw_w_out", pick=ci)]
    zeros_sh = [jnp.zeros(G[k].shape, F32) for k in _SHARD_SMALL]
    pk = lambda dct: _pack([dct[k] for k in _REPL] + zeros_sh)
    small_res = _adamw(pk(W), (r_small,), pk(M), pk(V), "adamw_small")
    shapes = [G[k].shape for k in small_names]
    unp = [_unpack(a, shapes) for a in small_res]
    for i, k in enumerate(_REPL):
        res[k] = [u[i] for u in unp]
    g_sh = []
    for i, k in enumerate(_SHARD_SMALL):
        gfull = unp[0][len(_REPL) + i]
        wdt = W[k].shape[-1]
        g_sh.append(lax.dynamic_slice_in_dim(gfull, chip * wdt, wdt, axis=2))
    g_sh_p = _pack(g_sh)
    sh_res = _adamw(_pack([W[k] for k in _SHARD_SMALL]), (g_sh_p,), _pack([M[k] for k in _SHARD_SMALL]),
                    _pack([V[k] for k in _SHARD_SMALL]), "adamw_shard_small")
    shapes2 = [W[k].shape for k in _SHARD_SMALL]
    unp2 = [_unpack(a, shapes2) for a in sh_res]
    for i, k in enumerate(_SHARD_SMALL):
        res[k] = [u[i] for u in unp2]

    outs = [loss, grad_x]
    for j in range(4):
        outs += [res[k][j] for k in WEIGHTS]
    return tuple(outs)
```

```python
import functools

import numpy as np
import jax
import jax.numpy as jnp
from jax import lax
from jax.experimental import pallas as pl
from jax.experimental.pallas import tpu as pltpu

F32 = jnp.float32
BF16 = jnp.bfloat16
HI = lax.Precision.HIGHEST
MESH = pl.DeviceIdType.MESH

EPS = 1e-6
D_MODEL = 2048
GROUP_W = 512
SEQ = 2048
DEPTH = 2
N_IN = 6176
GRID_W = 64
CONV_K = 31
NA_HEADS = 8
NA_DH = 64
NA_ROWS = 8
NA_COLS = 16
GLA_HEADS = 4
GLA_DK = 64
GLA_DV = 128
GLA_RANK = 16
GLA_TAU = 16.0
CHUNK = 64
POOL_WINDOWS = (2, 4, 8, 16)
ADAM_LR, ADAM_B1, ADAM_B2, ADAM_EPS, ADAM_WD, ADAM_STEP = 0.001, 0.9, 0.999, 1e-08, 0.01, 10

A_VAL, A_GLU, A_GATE = 0, 512, 1024
B_Q, B_K, B_V, B_GATE = 1536, 2048, 2560, 3072
C_Q, C_K, C_V, C_GATE = 3584, 3840, 4096, 4608
D_VAL, D_GATE = 5120, 5632
LR_OFF = 6144
NZ = 6400
NEG = -1e30
VMEM_LIMIT = 56 * 1024 * 1024


def _cp(sem=None):
    return pltpu.CompilerParams(dimension_semantics=sem, vmem_limit_bytes=VMEM_LIMIT)


def _sigmoid(x):
    return 1.0 / (1.0 + jnp.exp(-x))


def _silu(x):
    return x * _sigmoid(x)


def _dsilu(x):
    s = _sigmoid(x)
    return s * (1.0 + x * (1.0 - s))


def _matmul(a, b, *, dims, out_dtype, tm, tn, tk, name, res=None):
    if dims == "nn":
        (M, K), N = a.shape, b.shape[1]
    elif dims == "nt":
        (M, K), N = a.shape, b.shape[0]
    else:
        (K, M), N = a.shape, b.shape[1]
    tm, tn, tk = min(tm, M), min(tn, N), min(tk, K)
    nk = K // tk
    assert M % tm == 0 and N % tn == 0 and K % tk == 0, (M, N, K, tm, tn, tk)
    dn = {"nn": (((1,), (0,)), ((), ())), "nt": (((1,), (1,)), ((), ())), "tn": (((0,), (0,)), ((), ()))}[dims]
    if dims == "tn":
        a_spec = pl.BlockSpec((tk, tm), lambda i, j, k: (k, i))
    else:
        a_spec = pl.BlockSpec((tm, tk), lambda i, j, k: (i, k))
    if dims == "nt":
        b_spec = pl.BlockSpec((tn, tk), lambda i, j, k: (j, k))
    else:
        b_spec = pl.BlockSpec((tk, tn), lambda i, j, k: (k, j))
    o_spec = pl.BlockSpec((tm, tn), lambda i, j, k: (i, j))
    has_res = res is not None

    def body(*refs):
        if has_res:
            a_ref, b_ref, r_ref, o_ref, acc = refs
        else:
            a_ref, b_ref, o_ref, acc = refs
        k = pl.program_id(2)

        @pl.when(k == 0)
        def _():
            acc[...] = jnp.zeros_like(acc)

        acc[...] += lax.dot_general(a_ref[...], b_ref[...], dn, preferred_element_type=F32)

        @pl.when(k == nk - 1)
        def _():
            r = acc[...]
            if has_res:
                r = r + r_ref[...]
            o_ref[...] = r.astype(o_ref.dtype)

    in_specs = [a_spec, b_spec] + ([o_spec] if has_res else [])
    args = (a, b) + ((res,) if has_res else ())
    return pl.pallas_call(
        body, out_shape=jax.ShapeDtypeStruct((M, N), out_dtype), grid=(M // tm, N // tn, nk),
        in_specs=in_specs, out_specs=o_spec, scratch_shapes=[pltpu.VMEM((tm, tn), F32)],
        name=name, compiler_params=_cp(("parallel", "parallel", "arbitrary")))(*args)


def _concat_cols(pieces, width, name):
    T = pieces[0].shape[0]
    tm = min(512, T)
    dt = pieces[0].dtype
    offs = np.cumsum([0] + [p.shape[1] for p in pieces])

    def body(*refs):
        o_ref = refs[-1]
        for p_ref, a, b in zip(refs[:-1], offs[:-1], offs[1:]):
            o_ref[:, a:b] = p_ref[...]
        if offs[-1] < width:
            o_ref[:, offs[-1]:width] = jnp.zeros((tm, width - offs[-1]), dt)

    return pl.pallas_call(
        body, out_shape=jax.ShapeDtypeStruct((T, width), dt), grid=(T // tm,),
        in_specs=[pl.BlockSpec((tm, p.shape[1]), lambda i: (i, 0)) for p in pieces],
        out_specs=pl.BlockSpec((tm, width), lambda i: (i, 0)), name=name, compiler_params=_cp(("parallel",)))(*pieces)


def _rmsnorm_fwd(x, g, name):
    T, D = x.shape
    tm = 256

    def body(x_ref, g_ref, h_ref):
        xv = x_ref[...]
        r = lax.rsqrt(jnp.mean(xv * xv, axis=-1, keepdims=True) + EPS)
        h_ref[...] = (xv * r * g_ref[...]).astype(h_ref.dtype)

    return pl.pallas_call(
        body, out_shape=jax.ShapeDtypeStruct((T, D), BF16), grid=(T // tm,),
        in_specs=[pl.BlockSpec((tm, D), lambda i: (i, 0)), pl.BlockSpec((1, D), lambda i: (0, 0))],
        out_specs=pl.BlockSpec((tm, D), lambda i: (i, 0)), name=name, compiler_params=_cp(("parallel",)))(x, g)


def _rmsnorm_bwd(x, g, dh, dres, name):
    T, D = x.shape
    tm = 256

    def body(x_ref, g_ref, dh_ref, dres_ref, dx_ref, dxb_ref, dg_ref):
        xv = x_ref[...]
        r = lax.rsqrt(jnp.mean(xv * xv, axis=-1, keepdims=True) + EPS)
        xh = xv * r
        dh_v = dh_ref[...]
        dxh = dh_v * g_ref[...]
        dx = r * (dxh - xh * jnp.mean(dxh * xh, axis=-1, keepdims=True)) + dres_ref[...]
        dx_ref[...] = dx
        dxb_ref[...] = dx.astype(BF16)

        @pl.when(pl.program_id(0) == 0)
        def _():
            dg_ref[...] = jnp.zeros_like(dg_ref)

        dg_ref[...] += jnp.sum(dh_v * xh, axis=0, keepdims=True)

    row = pl.BlockSpec((tm, D), lambda i: (i, 0))
    vec = pl.BlockSpec((1, D), lambda i: (0, 0))
    return pl.pallas_call(
        body, out_shape=(jax.ShapeDtypeStruct((T, D), F32), jax.ShapeDtypeStruct((T, D), BF16), jax.ShapeDtypeStruct((1, D), F32)),
        grid=(T // tm,), in_specs=[row, vec, row, row], out_specs=(row, row, vec), name=name,
        compiler_params=_cp(("arbitrary",)))(x, g, dh, dres)


_GATE_COLS = (A_GATE // GROUP_W, B_GATE // GROUP_W, C_GATE // GROUP_W, D_GATE // GROUP_W)
_OP_TM = 256


def _out_proj_fwd(pres, z, w_out, x, target, name):
    T, D = x.shape
    tm = min(_OP_TM, T)
    with_loss = target is not None

    def body(*refs):
        pa, pb, pc, pd, ga, gb, gc, gd, w_ref, x_ref = refs[:10]
        refs = refs[10:]
        if with_loss:
            t_ref, y_ref, d_ref, db_ref, l_ref = refs
        else:
            y_ref, o_ref = refs
        for n_, (p, g) in enumerate(((pa, ga), (pb, gb), (pc, gc), (pd, gd))):
            y_ref[:, n_ * GROUP_W:(n_ + 1) * GROUP_W] = (p[...] * _silu(g[...])).astype(BF16)
        out = jnp.dot(y_ref[...], w_ref[...], preferred_element_type=F32) + x_ref[...]
        if with_loss:
            e = out - t_ref[...]
            d = e * (1.0 / D)
            d_ref[...] = d
            db_ref[...] = d.astype(BF16)

            @pl.when(pl.program_id(0) == 0)
            def _():
                l_ref[...] = jnp.zeros_like(l_ref)

            l_ref[...] += jnp.sum(jnp.sum(e * e, axis=-1, keepdims=True) * (0.5 / D), axis=0, keepdims=True)
        else:
            o_ref[...] = out

    pre_spec = pl.BlockSpec((tm, GROUP_W), lambda i: (i, 0))
    gate_specs = [pl.BlockSpec((tm, GROUP_W), functools.partial(lambda i, c: (i, c), c=c)) for c in _GATE_COLS]
    row = pl.BlockSpec((tm, D), lambda i: (i, 0))
    w_spec = pl.BlockSpec((4 * GROUP_W, D), lambda i: (0, 0))
    in_specs = [pre_spec] * 4 + gate_specs + [w_spec, row]
    args = list(pres) + [z, z, z, z, w_out, x]
    if with_loss:
        in_specs.append(row)
        args.append(target)
        out_shape = (jax.ShapeDtypeStruct((T, D), BF16), jax.ShapeDtypeStruct((T, D), F32), jax.ShapeDtypeStruct((T, D), BF16),
                     jax.ShapeDtypeStruct((1, 1), F32))
        out_specs = (row, row, row, pl.BlockSpec((1, 1), lambda i: (0, 0)))
    else:
        out_shape = (jax.ShapeDtypeStruct((T, D), BF16), jax.ShapeDtypeStruct((T, D), F32))
        out_specs = (row, row)
    return pl.pallas_call(body, out_shape=out_shape, grid=(T // tm,), in_specs=in_specs, out_specs=out_specs, name=name,
                          compiler_params=_cp(("arbitrary",)))(*args)


def _out_proj_bwd(dout_bf, w_out_t, pres, z, name):
    T, D = dout_bf.shape
    tm = min(_OP_TM, T)

    def body(do_ref, w_ref, pa, pb, pc, pd, ga, gb, gc, gd, dpa, dpb, dpc, dpd, dga, dgb, dgc, dgd):
        dy = jnp.dot(do_ref[...], w_ref[...], preferred_element_type=F32)
        for n_, (p, g, dp, dg) in enumerate(((pa, ga, dpa, dga), (pb, gb, dpb, dgb), (pc, gc, dpc, dgc), (pd, gd, dpd, dgd))):
            d = dy[:, n_ * GROUP_W:(n_ + 1) * GROUP_W]
            gv = g[...]
            dp[...] = d * _silu(gv)
            dg[...] = (d * p[...] * _dsilu(gv)).astype(BF16)

    pre_spec = pl.BlockSpec((tm, GROUP_W), lambda i: (i, 0))
    gate_specs = [pl.BlockSpec((tm, GROUP_W), functools.partial(lambda i, c: (i, c), c=c)) for c in _GATE_COLS]
    outs = tuple([jax.ShapeDtypeStruct((T, GROUP_W), F32)] * 4 + [jax.ShapeDtypeStruct((T, GROUP_W), BF16)] * 4)
    return pl.pallas_call(
        body, out_shape=outs, grid=(T // tm,),
        in_specs=[pl.BlockSpec((tm, D), lambda i: (i, 0)), pl.BlockSpec((D, 4 * GROUP_W), lambda i: (0, 0))] + [pre_spec] * 4 + gate_specs,
        out_specs=tuple([pre_spec] * 8), name=name, compiler_params=_cp(("parallel",)))(dout_bf, w_out_t, *pres, z, z, z, z)


_PAD = 16
_RC = 256


def _conv_fwd(z, conv_w32, conv_b, S, name):
    T = z.shape[0]
    E = T // S
    LW = 128

    def body(val_ref, glu_ref, w_ref, b_ref, y_ref, upad):
        upad[0:_PAD, :] = jnp.zeros((_PAD, LW), F32)
        upad[_PAD + S:_PAD + S + _PAD, :] = jnp.zeros((_PAD, LW), F32)
        upad[_PAD:_PAD + S, :] = val_ref[...] * _sigmoid(glu_ref[...])
        for r in range(S // _RC):
            acc = jnp.broadcast_to(b_ref[...], (_RC, LW))
            for k in range(CONV_K):
                st = r * _RC + k + 1
                acc = acc + upad[st:st + _RC, :] * w_ref[k:k + 1, :]
            y_ref[r * _RC:(r + 1) * _RC, :] = acc

    return pl.pallas_call(
        body, out_shape=jax.ShapeDtypeStruct((T, GROUP_W), F32), grid=(E, GROUP_W // LW),
        in_specs=[pl.BlockSpec((S, LW), lambda e, j: (e, A_VAL // LW + j)),
                  pl.BlockSpec((S, LW), lambda e, j: (e, A_GLU // LW + j)),
                  pl.BlockSpec((32, LW), lambda e, j: (0, j)),
                  pl.BlockSpec((1, LW), lambda e, j: (0, j))],
        out_specs=pl.BlockSpec((S, LW), lambda e, j: (e, j)),
        scratch_shapes=[pltpu.VMEM((S + 2 * _PAD, LW), F32)],
        name=name, compiler_params=_cp(("parallel", "parallel")))(z, z, conv_w32, conv_b)


def _conv_bwd(z, conv_w32, dyc, S, name):
    T = z.shape[0]
    E = T // S
    LW = 128

    def body(val_ref, glu_ref, w_ref, dy_ref, dval_ref, dglu_ref, dw_ref, db_ref, upad, dpad):
        e = pl.program_id(1)
        zeros = jnp.zeros((_PAD, LW), F32)
        upad[0:_PAD, :] = zeros
        upad[_PAD + S:_PAD + S + _PAD, :] = zeros
        dpad[0:_PAD, :] = zeros
        dpad[_PAD + S:_PAD + S + _PAD, :] = zeros
        upad[_PAD:_PAD + S, :] = val_ref[...] * _sigmoid(glu_ref[...])
        dpad[_PAD:_PAD + S, :] = dy_ref[...]

        @pl.when(e == 0)
        def _():
            dw_ref[...] = jnp.zeros_like(dw_ref)
            db_ref[...] = jnp.zeros_like(db_ref)

        db_ref[...] += jnp.sum(dy_ref[...], axis=0, keepdims=True)
        for r in range(S // _RC):
            dyr = dy_ref[r * _RC:(r + 1) * _RC, :]
            du = jnp.zeros((_RC, LW), F32)
            for k in range(CONV_K):
                st = r * _RC + k + 1
                dw_ref[k:k + 1, :] += jnp.sum(dyr * upad[st:st + _RC, :], axis=0, keepdims=True)
                sd = r * _RC + (CONV_K - 1 - k) + 1
                du = du + dpad[sd:sd + _RC, :] * w_ref[k:k + 1, :]
            sl = slice(r * _RC, (r + 1) * _RC)
            val = val_ref[sl, :]
            sg = _sigmoid(glu_ref[sl, :])
            dval_ref[sl, :] = (du * sg).astype(BF16)
            dglu_ref[sl, :] = (du * val * sg * (1.0 - sg)).astype(BF16)

    blk = pl.BlockSpec((S, LW), lambda j, e: (e, j))
    return pl.pallas_call(
        body, out_shape=(jax.ShapeDtypeStruct((T, GROUP_W), BF16), jax.ShapeDtypeStruct((T, GROUP_W), BF16),
                         jax.ShapeDtypeStruct((32, GROUP_W), F32), jax.ShapeDtypeStruct((1, GROUP_W), F32)),
        grid=(GROUP_W // LW, E),
        in_specs=[pl.BlockSpec((S, LW), lambda j, e: (e, A_VAL // LW + j)),
                  pl.BlockSpec((S, LW), lambda j, e: (e, A_GLU // LW + j)),
                  pl.BlockSpec((32, LW), lambda j, e: (0, j)), blk],
        out_specs=(blk, blk, pl.BlockSpec((32, LW), lambda j, e: (0, j)), pl.BlockSpec((1, LW), lambda j, e: (0, j))),
        scratch_shapes=[pltpu.VMEM((S + 2 * _PAD, LW), F32), pltpu.VMEM((S + 2 * _PAD, LW), F32)],
        name=name, compiler_params=_cp(("parallel", "arbitrary")))(z, z, conv_w32, dyc)


def _ln_silu_fwd(yc, g, b, name):
    T, C = yc.shape
    tm = 256

    def body(y_ref, g_ref, b_ref, o_ref):
        y = y_ref[...]
        mu = jnp.mean(y, axis=-1, keepdims=True)
        yc_ = y - mu
        r = lax.rsqrt(jnp.mean(yc_ * yc_, axis=-1, keepdims=True) + EPS)
        o_ref[...] = _silu(yc_ * r * g_ref[...] + b_ref[...])

    row = pl.BlockSpec((tm, C), lambda i: (i, 0))
    vec = pl.BlockSpec((1, C), lambda i: (0, 0))
    return pl.pallas_call(body, out_shape=jax.ShapeDtypeStruct((T, C), F32), grid=(T // tm,),
                          in_specs=[row, vec, vec], out_specs=row, name=name, compiler_params=_cp(("parallel",)))(yc, g, b)


def _ln_silu_bwd(yc, g, b, dpre, name):
    T, C = yc.shape
    tm = 256

    def body(y_ref, g_ref, b_ref, dp_ref, dy_ref, dg_ref, db_ref):
        y = y_ref[...]
        mu = jnp.mean(y, axis=-1, keepdims=True)
        yc_ = y - mu
        r = lax.rsqrt(jnp.mean(yc_ * yc_, axis=-1, keepdims=True) + EPS)
        xh = yc_ * r
        gv = g_ref[...]
        dln = dp_ref[...] * _dsilu(xh * gv + b_ref[...])
        dxh = dln * gv
        dy_ref[...] = r * (dxh - jnp.mean(dxh, axis=-1, keepdims=True) - xh * jnp.mean(dxh * xh, axis=-1, keepdims=True))

        @pl.when(pl.program_id(0) == 0)
        def _():
            dg_ref[...] = jnp.zeros_like(dg_ref)
            db_ref[...] = jnp.zeros_like(db_ref)

        dg_ref[...] += jnp.sum(dln * xh, axis=0, keepdims=True)
        db_ref[...] += jnp.sum(dln, axis=0, keepdims=True)

    row = pl.BlockSpec((tm, C), lambda i: (i, 0))
    vec = pl.BlockSpec((1, C), lambda i: (0, 0))
    return pl.pallas_call(
        body, out_shape=(jax.ShapeDtypeStruct((T, C), F32), jax.ShapeDtypeStruct((1, C), F32), jax.ShapeDtypeStruct((1, C), F32)),
        grid=(T // tm,), in_specs=[row, vec, vec, row], out_specs=(row, vec, vec), name=name,
        compiler_params=_cp(("arbitrary",)))(yc, g, b, dpre)


def _pool_counts(S, w, rows0, n):
    t = (lax.broadcasted_iota(jnp.int32, (n, 1), 0) + rows0)
    lo = jnp.maximum(t - w // 2, 0)
    hi = jnp.minimum(t + w // 2, S)
    return (hi - lo).astype(F32)


def _pool_fwd(z, pool_w, pool_scale, S, name):
    T = z.shape[0]
    E = T // S
    CG = 128

    def body(u_ref, w_ref, s_ref, o_ref, upad, dif):
        zeros = jnp.zeros((_PAD, GROUP_W), F32)
        upad[0:_PAD, :] = zeros
        upad[_PAD + S:_PAD + S + _PAD, :] = zeros
        upad[_PAD:_PAD + S, :] = u_ref[...]
        for gi, w in enumerate(POOL_WINDOWS):
            ls = slice(gi * CG, (gi + 1) * CG)
            for r in range(S // _RC):
                acc = jnp.zeros((_RC, CG), F32)
                for j in range(-(w // 2), w // 2):
                    st = _PAD + r * _RC + j
                    acc = acc + upad[st:st + _RC, ls]
                cnt = _pool_counts(S, w, r * _RC, _RC)
                dif[r * _RC:(r + 1) * _RC, :] = (acc / cnt - u_ref[r * _RC:(r + 1) * _RC, ls]).astype(BF16)
            yp = jnp.dot(dif[...], w_ref[gi], preferred_element_type=F32)
            o_ref[:, ls] = yp * s_ref[:, ls]

    return pl.pallas_call(
        body, out_shape=jax.ShapeDtypeStruct((T, GROUP_W), F32), grid=(E,),
        in_specs=[pl.BlockSpec((S, GROUP_W), lambda e: (e, D_VAL // GROUP_W)),
                  pl.BlockSpec((4, CG, CG), lambda e: (0, 0, 0)),
                  pl.BlockSpec((1, GROUP_W), lambda e: (0, 0))],
        out_specs=pl.BlockSpec((S, GROUP_W), lambda e: (e, 0)),
        scratch_shapes=[pltpu.VMEM((S + 2 * _PAD, GROUP_W), F32), pltpu.VMEM((S, CG), BF16)],
        name=name, compiler_params=_cp(("parallel",)))(z, pool_w, pool_scale)


def _pool_bwd(z, pool_w, pool_scale, dpre, S, name):
    T = z.shape[0]
    E = T // S
    CG = 128

    def body(u_ref, w_ref, s_ref, dp_ref, du_ref, dw_ref, ds_ref, upad, dif, qpad):
        zeros = jnp.zeros((_PAD, GROUP_W), F32)
        upad[0:_PAD, :] = zeros
        upad[_PAD + S:_PAD + S + _PAD, :] = zeros
        upad[_PAD:_PAD + S, :] = u_ref[...]
        zc = jnp.zeros((_PAD, CG), F32)
        qpad[0:_PAD, :] = zc
        qpad[_PAD + S:_PAD + S + _PAD, :] = zc

        @pl.when(pl.program_id(0) == 0)
        def _():
            dw_ref[...] = jnp.zeros_like(dw_ref)
            ds_ref[...] = jnp.zeros_like(ds_ref)

        for gi, w in enumerate(POOL_WINDOWS):
            ls = slice(gi * CG, (gi + 1) * CG)
            for r in range(S // _RC):
                acc = jnp.zeros((_RC, CG), F32)
                for j in range(-(w // 2), w // 2):
                    st = _PAD + r * _RC + j
                    acc = acc + upad[st:st + _RC, ls]
                cnt = _pool_counts(S, w, r * _RC, _RC)
                dif[r * _RC:(r + 1) * _RC, :] = (acc / cnt - u_ref[r * _RC:(r + 1) * _RC, ls]).astype(BF16)
            dp = dp_ref[:, ls]
            yp = jnp.dot(dif[...], w_ref[gi], preferred_element_type=F32)
            ds_ref[:, ls] += jnp.sum(dp * yp, axis=0, keepdims=True)
            dys = (dp * s_ref[:, ls]).astype(BF16)
            dw_ref[gi] += lax.dot_general(dif[...], dys, (((0,), (0,)), ((), ())), preferred_element_type=F32)
            dm = lax.dot_general(dys, w_ref[gi], (((1,), (1,)), ((), ())), preferred_element_type=F32)
            for r in range(S // _RC):
                cnt = _pool_counts(S, w, r * _RC, _RC)
                qpad[_PAD + r * _RC:_PAD + (r + 1) * _RC, :] = dm[r * _RC:(r + 1) * _RC, :] / cnt
            for r in range(S // _RC):
                acc = -dm[r * _RC:(r + 1) * _RC, :]
                for j in range(-(w // 2) + 1, w // 2 + 1):
                    st = _PAD + r * _RC + j
                    acc = acc + qpad[st:st + _RC, :]
                du_ref[r * _RC:(r + 1) * _RC, ls] = acc.astype(BF16)

    return pl.pallas_call(
        body, out_shape=(jax.ShapeDtypeStruct((T, GROUP_W), BF16), jax.ShapeDtypeStruct((4, CG, CG), F32),
                         jax.ShapeDtypeStruct((1, GROUP_W), F32)), grid=(E,),
        in_specs=[pl.BlockSpec((S, GROUP_W), lambda e: (e, D_VAL // GROUP_W)),
                  pl.BlockSpec((4, CG, CG), lambda e: (0, 0, 0)),
                  pl.BlockSpec((1, GROUP_W), lambda e: (0, 0)),
                  pl.BlockSpec((S, GROUP_W), lambda e: (e, 0))],
        out_specs=(pl.BlockSpec((S, GROUP_W), lambda e: (e, 0)), pl.BlockSpec((4, CG, CG), lambda e: (0, 0, 0)),
                   pl.BlockSpec((1, GROUP_W), lambda e: (0, 0))),
        scratch_shapes=[pltpu.VMEM((S + 2 * _PAD, GROUP_W), F32), pltpu.VMEM((S, CG), BF16),
                        pltpu.VMEM((S + 2 * _PAD, CG), F32)],
        name=name, compiler_params=_cp(("arbitrary",)))(z, pool_w, pool_scale, dpre)


def _na_tables():
    d = np.arange(NA_ROWS)[:, None]
    kr = np.arange(NA_ROWS)[None, :]
    ro = kr - d + (NA_ROWS - 1)
    qc = np.arange(GRID_W)[:, None]
    kc = np.arange(GRID_W)[None, :]
    cs = np.clip(qc - NA_COLS // 2, 0, GRID_W - NA_COLS)
    valid = (kc >= cs) & (kc < cs + NA_COLS)
    co = np.clip(kc - qc + (NA_COLS - 1), 0, 2 * NA_COLS - 2)
    return ro, co, valid


def _na_onehots():
    ro, co, valid = _na_tables()
    e_np = np.zeros((GRID_W, GRID_W, 128), np.float32)
    qi, ki = np.nonzero(valid)
    e_np[qi, ki, co[qi, ki]] = 1.0
    a_np = np.zeros((16, NA_ROWS * NA_ROWS), np.float32)
    a_np[ro.reshape(-1), np.arange(NA_ROWS * NA_ROWS)] = 1.0
    mask = np.where(valid, 0.0, NEG).astype(np.float32).reshape(1, GRID_W * GRID_W)
    return e_np.reshape(GRID_W * GRID_W, 128), a_np, mask


def _na_bias(rpb, name):
    e_np, a_np, mask = _na_onehots()
    H = NA_HEADS
    rp = jnp.pad(rpb, ((0, 0), (0, 1), (0, 128 - rpb.shape[2])))

    def body(r_ref, e_ref, at_ref, m_ref, o_ref):
        t = jnp.dot(at_ref[...], r_ref[0], precision=HI, preferred_element_type=F32)
        o_ref[0] = lax.dot_general(t, e_ref[...], (((1,), (1,)), ((), ())), precision=HI,
                                   preferred_element_type=F32) + m_ref[...]

    out = pl.pallas_call(
        body, out_shape=jax.ShapeDtypeStruct((H, NA_ROWS * NA_ROWS, GRID_W * GRID_W), F32), grid=(H,),
        in_specs=[pl.BlockSpec((1, 16, 128), lambda h: (h, 0, 0)),
                  pl.BlockSpec((GRID_W * GRID_W, 128), lambda h: (0, 0)),
                  pl.BlockSpec((NA_ROWS * NA_ROWS, 16), lambda h: (0, 0)),
                  pl.BlockSpec((1, GRID_W * GRID_W), lambda h: (0, 0))],
        out_specs=pl.BlockSpec((1, NA_ROWS * NA_ROWS, GRID_W * GRID_W), lambda h: (h, 0, 0)),
        name=name, compiler_params=_cp(("parallel",)))(rp, jnp.asarray(e_np), jnp.asarray(a_np.T), jnp.asarray(mask))
    t = out.reshape(H, NA_ROWS, NA_ROWS, GRID_W, GRID_W)
    return jnp.transpose(t, (0, 1, 3, 2, 4)).reshape(H, NA_ROWS, GRID_W, NA_ROWS * GRID_W)


def _seg_mean_matrix(width, seg):
    i = np.arange(width)
    return jnp.asarray((i[:, None] // seg == i[None, :] // seg).astype(np.float32) / seg, BF16)


def _seg_mean(x, seg_ref):
    hi = x.astype(BF16)
    lo = (x - hi.astype(F32)).astype(BF16)
    return (jnp.dot(hi, seg_ref[...], preferred_element_type=F32) + jnp.dot(lo, seg_ref[...], preferred_element_type=F32))


def _na_fwd(z, qg, kg, bias, S, name):
    T = z.shape[0]
    E = T // S
    rows = S // GRID_W
    WIN = NA_ROWS * GRID_W
    seg = _seg_mean_matrix(128, NA_DH)

    def body(q_ref, k_ref, v_ref, qg_ref, kg_ref, bias_ref, seg_ref, o_ref, qs, ks, vs, s_all, p_all):
        for c in range(S // _RC):
            sl = slice(c * _RC, (c + 1) * _RC)
            q = q_ref[sl, :]
            k = k_ref[sl, :]
            qn = q * lax.rsqrt(_seg_mean(q * q, seg_ref) + EPS) * qg_ref[...]
            kn = k * lax.rsqrt(_seg_mean(k * k, seg_ref) + EPS) * kg_ref[...]
            v = v_ref[sl, :]
            for hh in range(2):
                ls = slice(hh * NA_DH, (hh + 1) * NA_DH)
                qs[hh, sl, :] = qn[:, ls].astype(BF16)
                ks[hh, sl, :] = kn[:, ls].astype(BF16)
                vs[hh, sl, :] = v[:, ls].astype(BF16)
        def where(r):
            rs = jnp.clip(r - NA_ROWS // 2, 0, rows - NA_ROWS)
            return rs, pl.multiple_of(r * GRID_W, GRID_W), pl.multiple_of(rs * GRID_W, GRID_W)

        def scores(r, carry):
            rs, q0, k0 = where(r)
            for hh in range(2):
                s = lax.dot_general(qs[hh, pl.ds(q0, GRID_W), :], ks[hh, pl.ds(k0, WIN), :], (((1,), (1,)), ((), ())),
                                    preferred_element_type=F32) * (NA_DH ** -0.5)
                s_all[hh, pl.ds(q0, GRID_W), :] = s + bias_ref[hh, r - rs]
            return carry
        lax.fori_loop(0, rows, scores, 0, unroll=4)

        def soft(r, carry):
            _, q0, _ = where(r)
            for hh in range(2):
                s = s_all[hh, pl.ds(q0, GRID_W), :]
                p = jnp.exp(s - jnp.max(s, axis=-1, keepdims=True))
                p_all[hh, pl.ds(q0, GRID_W), :] = (p * (1.0 / jnp.sum(p, axis=-1, keepdims=True))).astype(BF16)
            return carry
        lax.fori_loop(0, rows, soft, 0, unroll=2)

        def outp(r, carry):
            _, q0, k0 = where(r)
            outs = [jnp.dot(p_all[hh, pl.ds(q0, GRID_W), :], vs[hh, pl.ds(k0, WIN), :], preferred_element_type=F32)
                    for hh in range(2)]
            o_ref[pl.ds(q0, GRID_W), :] = jnp.concatenate(outs, axis=1)
            return carry
        lax.fori_loop(0, rows, outp, 0, unroll=4)

    LW = 128
    return pl.pallas_call(
        body, out_shape=jax.ShapeDtypeStruct((T, GROUP_W), F32), grid=(E, GROUP_W // LW),
        in_specs=[pl.BlockSpec((S, LW), lambda e, j: (e, B_Q // LW + j)),
                  pl.BlockSpec((S, LW), lambda e, j: (e, B_K // LW + j)),
                  pl.BlockSpec((S, LW), lambda e, j: (e, B_V // LW + j)),
                  pl.BlockSpec((1, LW), lambda e, j: (0, j)),
                  pl.BlockSpec((1, LW), lambda e, j: (0, j)),
                  pl.BlockSpec((2, NA_ROWS, GRID_W, WIN), lambda e, j: (j, 0, 0, 0)),
                  pl.BlockSpec((LW, LW), lambda e, j: (0, 0))],
        out_specs=pl.BlockSpec((S, LW), lambda e, j: (e, j)),
        scratch_shapes=[pltpu.VMEM((2, S, NA_DH), BF16)] * 3 + [pltpu.VMEM((2, S, WIN), F32), pltpu.VMEM((2, S, WIN), BF16)],
        name=name, compiler_params=_cp(("parallel", "parallel")))(z, z, z, qg, kg, bias, seg)


def _na_bwd(z, qg, kg, bias, do, S, name):
    T = z.shape[0]
    E = T // S
    rows = S // GRID_W
    WIN = NA_ROWS * GRID_W
    seg = _seg_mean_matrix(128, NA_DH)
    SC = NA_DH ** -0.5

    def body(q_ref, k_ref, v_ref, qg_ref, kg_ref, bias_ref, seg_ref, do_ref,
             dq_ref, dk_ref, dv_ref, dbias_ref, dqg_ref, dkg_ref, qs, ks, vs, dos, dqn, dkn, dvs, akt, avt,
             s_all, dp_all, p_all, ds_all):
        e = pl.program_id(1)

        @pl.when(e == 0)
        def _():
            dbias_ref[...] = jnp.zeros_like(dbias_ref)
            dqg_ref[...] = jnp.zeros_like(dqg_ref)
            dkg_ref[...] = jnp.zeros_like(dkg_ref)

        for c in range(S // _RC):
            sl = slice(c * _RC, (c + 1) * _RC)
            q = q_ref[sl, :]
            k = k_ref[sl, :]
            qn = q * lax.rsqrt(_seg_mean(q * q, seg_ref) + EPS) * qg_ref[...]
            kn = k * lax.rsqrt(_seg_mean(k * k, seg_ref) + EPS) * kg_ref[...]
            v = v_ref[sl, :]
            dd = do_ref[sl, :]
            for hh in range(2):
                ls = slice(hh * NA_DH, (hh + 1) * NA_DH)
                qs[hh, sl, :] = qn[:, ls].astype(BF16)
                ks[hh, sl, :] = kn[:, ls].astype(BF16)
                vs[hh, sl, :] = v[:, ls].astype(BF16)
                dos[hh, sl, :] = dd[:, ls].astype(BF16)
        akt[...] = jnp.zeros_like(akt)
        avt[...] = jnp.zeros_like(avt)

        def where(r):
            rs = jnp.clip(r - NA_ROWS // 2, 0, rows - NA_ROWS)
            return rs, pl.multiple_of(r * GRID_W, GRID_W), pl.multiple_of(rs * GRID_W, GRID_W)

        for hh in range(2):
            ls = slice(hh * NA_DH, (hh + 1) * NA_DH)

            def products(r, carry, hh=hh):
                rs, q0, k0 = where(r)
                s = lax.dot_general(qs[hh, pl.ds(q0, GRID_W), :], ks[hh, pl.ds(k0, WIN), :], (((1,), (1,)), ((), ())),
                                    preferred_element_type=F32) * SC
                s_all[pl.ds(q0, GRID_W), :] = s + bias_ref[hh, r - rs]
                dp_all[pl.ds(q0, GRID_W), :] = lax.dot_general(dos[hh, pl.ds(q0, GRID_W), :], vs[hh, pl.ds(k0, WIN), :],
                                                               (((1,), (1,)), ((), ())), preferred_element_type=F32)
                return carry
            lax.fori_loop(0, rows, products, 0, unroll=4)

            def soft(r, carry, hh=hh):
                rs, q0, _ = where(r)
                s = s_all[pl.ds(q0, GRID_W), :]
                p = jnp.exp(s - jnp.max(s, axis=-1, keepdims=True))
                p = p * (1.0 / jnp.sum(p, axis=-1, keepdims=True))
                dp = dp_all[pl.ds(q0, GRID_W), :]
                ds = p * (dp - jnp.sum(p * dp, axis=-1, keepdims=True))
                dbias_ref[hh, r - rs] += ds
                p_all[pl.ds(q0, GRID_W), :] = p.astype(BF16)
                ds_all[pl.ds(q0, GRID_W), :] = ds.astype(BF16)
                return carry
            lax.fori_loop(0, rows, soft, 0, unroll=2)

            def grads(r, carry, hh=hh, ls=ls):
                rs, q0, k0 = where(r)
                par = rs % 2
                t0 = (rs + par) // 2
                qr = qs[hh, pl.ds(q0, GRID_W), :]
                dor = dos[hh, pl.ds(q0, GRID_W), :]
                dsb = ds_all[pl.ds(q0, GRID_W), :]
                dqn[pl.ds(q0, GRID_W), ls] = jnp.dot(dsb, ks[hh, pl.ds(k0, WIN), :], preferred_element_type=F32) * SC
                dkt = lax.dot_general(qr, dsb, (((0,), (0,)), ((), ())), preferred_element_type=F32) * SC
                dvt = lax.dot_general(dor, p_all[pl.ds(q0, GRID_W), :], (((0,), (0,)), ((), ())), preferred_element_type=F32)
                akt[hh, par, pl.ds(t0, WIN // 128)] += jnp.stack([dkt[:, 128 * i:128 * (i + 1)] for i in range(WIN // 128)])
                avt[hh, par, pl.ds(t0, WIN // 128)] += jnp.stack([dvt[:, 128 * i:128 * (i + 1)] for i in range(WIN // 128)])
                return carry
            lax.fori_loop(0, rows, grads, 0, unroll=4)

        for hh in range(2):
            ls = slice(hh * NA_DH, (hh + 1) * NA_DH)
            for i in range(S // 128):
                for acc, dst in ((akt, dkn), (avt, dvs)):
                    odd = jnp.concatenate([acc[hh, 1, i][:, NA_DH:], acc[hh, 1, i + 1][:, :NA_DH]], axis=1)
                    dst[128 * i:128 * (i + 1), ls] = (acc[hh, 0, i] + odd).T

        for c in range(S // _RC):
            sl = slice(c * _RC, (c + 1) * _RC)
            for x_ref, g_ref, dn, dx_ref, dg_ref in ((q_ref, qg_ref, dqn, dq_ref, dqg_ref), (k_ref, kg_ref, dkn, dk_ref, dkg_ref)):
                x = x_ref[sl, :]
                r_ = lax.rsqrt(_seg_mean(x * x, seg_ref) + EPS)
                xh = x * r_
                d = dn[sl, :]
                dxh = d * g_ref[...]
                mean = _seg_mean(dxh * xh, seg_ref)
                dx_ref[sl, :] = (r_ * (dxh - xh * mean)).astype(BF16)
                dg_ref[...] += jnp.sum(d * xh, axis=0, keepdims=True)
            dv_ref[sl, :] = dvs[sl, :].astype(BF16)

    LW = 128
    blk = pl.BlockSpec((S, LW), lambda j, e: (e, j))
    vec = pl.BlockSpec((1, LW), lambda j, e: (0, j))
    bsp = pl.BlockSpec((2, NA_ROWS, GRID_W, WIN), lambda j, e: (j, 0, 0, 0))
    return pl.pallas_call(
        body, out_shape=(jax.ShapeDtypeStruct((T, GROUP_W), BF16),) * 3 + (
            jax.ShapeDtypeStruct((NA_HEADS, NA_ROWS, GRID_W, WIN), F32),
            jax.ShapeDtypeStruct((1, GROUP_W), F32), jax.ShapeDtypeStruct((1, GROUP_W), F32)),
        grid=(GROUP_W // LW, E),
        in_specs=[pl.BlockSpec((S, LW), lambda j, e: (e, B_Q // LW + j)),
                  pl.BlockSpec((S, LW), lambda j, e: (e, B_K // LW + j)),
                  pl.BlockSpec((S, LW), lambda j, e: (e, B_V // LW + j)),
                  vec, vec, bsp, pl.BlockSpec((LW, LW), lambda j, e: (0, 0)), blk],
        out_specs=(blk, blk, blk, bsp, vec, vec),
        scratch_shapes=[pltpu.VMEM((2, S, NA_DH), BF16)] * 4 + [pltpu.VMEM((S, LW), F32)] * 3
        + [pltpu.VMEM((2, 2, S // 128 + 1, NA_DH, 128), F32)] * 2
        + [pltpu.VMEM((S, WIN), F32)] * 2 + [pltpu.VMEM((S, WIN), BF16)] * 2,
        name=name, compiler_params=_cp(("parallel", "arbitrary")))(z, z, z, qg, kg, bias, seg, do)


def _na_rpb_grad(dbias, name):
    e_np, a_np, _ = _na_onehots()
    H = NA_HEADS
    x = dbias.reshape(H, NA_ROWS, GRID_W, NA_ROWS, GRID_W)
    x = jnp.transpose(x, (0, 1, 3, 2, 4)).reshape(H, NA_ROWS * NA_ROWS, GRID_W * GRID_W)

    def body(x_ref, e_ref, a_ref, o_ref):
        y = jnp.dot(x_ref[0], e_ref[...], precision=HI, preferred_element_type=F32)
        o_ref[0] = jnp.dot(a_ref[...], y, precision=HI, preferred_element_type=F32)

    out = pl.pallas_call(
        body, out_shape=jax.ShapeDtypeStruct((H, 16, 128), F32), grid=(H,),
        in_specs=[pl.BlockSpec((1, 64, GRID_W * GRID_W), lambda h: (h, 0, 0)),
                  pl.BlockSpec((GRID_W * GRID_W, 128), lambda h: (0, 0)),
                  pl.BlockSpec((16, 64), lambda h: (0, 0))],
        out_specs=pl.BlockSpec((1, 16, 128), lambda h: (h, 0, 0)),
        name=name, compiler_params=_cp(("parallel",)))(x, jnp.asarray(e_np), jnp.asarray(a_np))
    return out[:, :2 * NA_ROWS - 1, :2 * NA_COLS - 1]


_HK = GLA_HEADS * GLA_DK
_HV = GLA_HEADS * GLA_DV


def _gla_consts(reverse):
    i = np.arange(CHUNK)
    tri = (i[:, None] <= i[None, :]) if reverse else (i[:, None] >= i[None, :])
    j = np.arange(_HK)
    oseg = (j[:, None] // GLA_DK == j[None, :] // GLA_DK)
    return (jnp.asarray(tri.astype(np.float32)), jnp.asarray(tri.T.astype(np.float32)), jnp.asarray(oseg.astype(np.float32), BF16))


def _log_decay(lr, a2, ab):
    zg = jnp.dot(lr, a2, precision=HI, preferred_element_type=F32) + ab
    g = (jnp.minimum(zg, 0.0) - jnp.log(1.0 + jnp.exp(-jnp.abs(zg)))) * (1.0 / GLA_TAU)
    return zg, g


def _dotf(a, b, dn):
    return lax.dot_general(a, b, dn, precision=HI, preferred_element_type=F32)


def _dotb(a, b, dn):
    return lax.dot_general(a.astype(BF16), b.astype(BF16), dn, preferred_element_type=F32)


_COLS = 4
_SUB = 16
_NSUB = CHUNK // _SUB


def _gla_cross_blocks(reverse):
    return range(0, _NSUB - 1) if reverse else range(1, _NSUB)


def _gla_cross_terms(s, reverse, b_s, q_s, k_s, oseg_ref):
    r0 = s * _SUB
    ref = r0 + (_SUB - 1 if reverse else 0)
    bref = b_s[ref:ref + 1, :]
    rowj = lax.broadcasted_iota(jnp.int32, (CHUNK, 1), 0)
    seen = (rowj >= r0 + _SUB) if reverse else (rowj < r0)
    ek = jnp.where(seen, jnp.exp(jnp.minimum(bref - b_s[...], 0.0)), 0.0)
    kt = k_s[...] * ek
    eq = jnp.exp(jnp.minimum(b_s[r0:r0 + _SUB, :] - bref, 0.0))
    qt = q_s[r0:r0 + _SUB, :] * eq
    nmat = jnp.concatenate([kt.astype(BF16)] * GLA_HEADS, axis=0) * oseg_ref[...]
    return qt, eq, kt, ek, nmat


_NN = (((1,), (0,)), ((), ()))
_NT = (((1,), (1,)), ((), ()))
_TN = (((0,), (0,)), ((), ()))


_DT = 256


def _gla_block_tri(reverse):
    i = np.arange(_DT)
    same = i[:, None] // CHUNK == i[None, :] // CHUNK
    tri = (i[:, None] <= i[None, :]) if reverse else (i[:, None] >= i[None, :])
    return (tri & same).astype(np.float32)


def _gla_decay_fwd(z, a2, ab, reverse, name):
    T = z.shape[0]
    nc = _DT // CHUNK

    def body(lr_ref, a2_ref, ab_ref, m_ref, b_ref, ec_ref):
        _, g = _log_decay(lr_ref[...], a2_ref[...], ab_ref[...])
        b_ref[...] = _dotf(m_ref[...], g, _NN)
        for c in range(nc):
            ec_ref[c] = jnp.exp(_dotf(g[c * CHUNK:(c + 1) * CHUNK, :], jnp.ones((CHUNK, GLA_DV), F32), _TN))

    return pl.pallas_call(
        body, out_shape=(jax.ShapeDtypeStruct((T, _HK), F32), jax.ShapeDtypeStruct((T // CHUNK, _HK, GLA_DV), F32)),
        grid=(T // _DT,),
        in_specs=[pl.BlockSpec((_DT, 128), lambda i: (i, LR_OFF // 128)),
                  pl.BlockSpec((128, _HK), lambda i: (0, 0)),
                  pl.BlockSpec((1, _HK), lambda i: (0, 0)),
                  pl.BlockSpec((_DT, _DT), lambda i: (0, 0))],
        out_specs=(pl.BlockSpec((_DT, _HK), lambda i: (i, 0)), pl.BlockSpec((nc, _HK, GLA_DV), lambda i: (i, 0, 0))),
        name=name, compiler_params=_cp(("parallel",)))(z, a2, ab, jnp.asarray(_gla_block_tri(reverse)))


def _gla_decay_bwd(z, a2_f, ab_f, a2_b, ab_b, db_f, db_b, name):
    T = z.shape[0]

    def body(lr_ref, a2f_ref, abf_ref, a2b_ref, abb_ref, mf_ref, mb_ref, dbf_ref, dbb_ref,
             dlr_ref, da2f_ref, dabf_ref, da2b_ref, dabb_ref):
        @pl.when(pl.program_id(0) == 0)
        def _():
            for r in (da2f_ref, dabf_ref, da2b_ref, dabb_ref):
                r[...] = jnp.zeros_like(r)

        lr = lr_ref[...]
        dlr = jnp.zeros((_DT, 128), F32)
        for a2_ref, ab_ref, mt_ref, db_ref, da2_ref, dab_ref in ((a2f_ref, abf_ref, mf_ref, dbf_ref, da2f_ref, dabf_ref),
                                                                 (a2b_ref, abb_ref, mb_ref, dbb_ref, da2b_ref, dabb_ref)):
            zg, _ = _log_decay(lr, a2_ref[...], ab_ref[...])
            dg = _dotf(mt_ref[...], db_ref[...], _NN)
            dzg = dg * (1.0 / (1.0 + jnp.exp(zg))) * (1.0 / GLA_TAU)
            dlr = dlr + _dotf(dzg, a2_ref[...], _NT)
            da2_ref[...] += _dotf(lr, dzg, _TN)
            dab_ref[...] += jnp.sum(dzg, axis=0, keepdims=True)
        dlr_ref[...] = dlr.astype(BF16)

    a2s = pl.BlockSpec((128, _HK), lambda i: (0, 0))
    abs_ = pl.BlockSpec((1, _HK), lambda i: (0, 0))
    ms = pl.BlockSpec((_DT, _DT), lambda i: (0, 0))
    row = pl.BlockSpec((_DT, _HK), lambda i: (i, 0))
    return pl.pallas_call(
        body, out_shape=(jax.ShapeDtypeStruct((T, 128), BF16), jax.ShapeDtypeStruct((128, _HK), F32), jax.ShapeDtypeStruct((1, _HK), F32),
                         jax.ShapeDtypeStruct((128, _HK), F32), jax.ShapeDtypeStruct((1, _HK), F32)),
        grid=(T // _DT,),
        in_specs=[pl.BlockSpec((_DT, 128), lambda i: (i, LR_OFF // 128)), a2s, abs_, a2s, abs_, ms, ms, row, row],
        out_specs=(pl.BlockSpec((_DT, 128), lambda i: (i, 0)), a2s, abs_, a2s, abs_),
        name=name, compiler_params=_cp(("arbitrary",)))(
            z, a2_f, ab_f, a2_b, ab_b, jnp.asarray(_gla_block_tri(False).T), jnp.asarray(_gla_block_tri(True).T), db_f, db_b)


def _gla_fwd(z, b_all, ecol, S, reverse, name):
    T = z.shape[0]
    E = T // S
    n = S // CHUNK
    _, _, oseg = _gla_consts(reverse)
    last = 0 if reverse else CHUNK - 1

    def body(q_ref, k_ref, v_ref, b_ref, ec_ref, oseg_ref, o_ref, a_ref, st_ref, st, b_s, q_s, k_s):
        @pl.when(pl.program_id(1) == 0)
        def _():
            st[...] = jnp.zeros_like(st)

        q = q_ref[...] * (GLA_DK ** -0.5)
        k = k_ref[...]
        v = v_ref[...]
        b = b_ref[...]
        bl_row = b_ref[last:last + 1, :]
        e_col = ec_ref[0]
        b_s[...] = b
        q_s[...] = q
        k_s[...] = k
        lane = lax.broadcasted_iota(jnp.int32, (1, _HK), 1) % GLA_DK

        rowi = lax.broadcasted_iota(jnp.int32, (CHUNK, 1), 0)
        blk0 = (rowi // _SUB) * _SUB

        def cols(jj, a):
            ts = []
            for u in range(_COLS):
                jp = jj * _COLS + u
                tiles = []
                for s in range(_NSUB):
                    rs_ = slice(s * _SUB, (s + 1) * _SUB)
                    bj = b_s[pl.ds(s * _SUB + jp, 1), :]
                    kj = k_s[pl.ds(s * _SUB + jp, 1), :]
                    tiles.append(q_s[rs_, :] * jnp.exp(jnp.minimum(b_s[rs_, :] - bj, 0.0)) * kj)
                ts.append(jnp.concatenate(tiles, axis=0).astype(BF16))
            r = jnp.dot(jnp.concatenate(ts, axis=0), oseg_ref[...], preferred_element_type=F32)
            for u in range(_COLS):
                a = jnp.where(lane == blk0 + (jj * _COLS + u), r[u * CHUNK:(u + 1) * CHUNK, :], a)
            return a

        a = lax.fori_loop(0, _SUB // _COLS, cols, jnp.zeros((CHUNK, _HK), F32))
        keep = (rowi <= lane) if reverse else (rowi >= lane)
        a = jnp.where(keep, a, 0.0)
        cross = []
        for s in range(_NSUB):
            if s in _gla_cross_blocks(reverse):
                qt, _, _, _, nmat = _gla_cross_terms(s, reverse, b_s, q_s, k_s, oseg_ref)
                cross.append(lax.dot_general(qt.astype(BF16), nmat, _NT, preferred_element_type=F32))
            else:
                cross.append(jnp.zeros((_SUB, _HK), F32))
        a = a + jnp.concatenate(cross, axis=0)
        a_ref[...] = a
        st_ref[0] = st[...]
        qb = q * jnp.exp(b)
        kd = k * jnp.exp(bl_row - b)
        for h in range(GLA_HEADS):
            ks_ = slice(h * GLA_DK, (h + 1) * GLA_DK)
            vs_ = slice(h * GLA_DV, (h + 1) * GLA_DV)
            s_h = st[ks_, :]
            o_ref[:, vs_] = _dotb(qb[:, ks_], s_h, _NN) + _dotb(a[:, ks_], v[:, vs_], _NN)
            st[ks_, :] = s_h * e_col[ks_, :] + _dotb(kd[:, ks_], v[:, vs_], _TN)

    def rowblk(e, c):
        return e * n + ((n - 1 - c) if reverse else c)

    return pl.pallas_call(
        body, out_shape=(jax.ShapeDtypeStruct((T, _HV), F32), jax.ShapeDtypeStruct((T, _HK), F32),
                         jax.ShapeDtypeStruct((T // CHUNK, _HK, GLA_DV), F32)),
        grid=(E, n),
        in_specs=[pl.BlockSpec((CHUNK, _HK), lambda e, c: (rowblk(e, c), C_Q // _HK)),
                  pl.BlockSpec((CHUNK, _HK), lambda e, c: (rowblk(e, c), C_K // _HK)),
                  pl.BlockSpec((CHUNK, _HV), lambda e, c: (rowblk(e, c), C_V // _HV)),
                  pl.BlockSpec((CHUNK, _HK), lambda e, c: (rowblk(e, c), 0)),
                  pl.BlockSpec((1, _HK, GLA_DV), lambda e, c: (rowblk(e, c), 0, 0)),
                  pl.BlockSpec((_HK, _HK), lambda e, c: (0, 0))],
        out_specs=(pl.BlockSpec((CHUNK, _HV), lambda e, c: (rowblk(e, c), 0)),
                   pl.BlockSpec((CHUNK, _HK), lambda e, c: (rowblk(e, c), 0)),
                   pl.BlockSpec((1, _HK, GLA_DV), lambda e, c: (rowblk(e, c), 0, 0))),
        scratch_shapes=[pltpu.VMEM((_HK, GLA_DV), F32)] + [pltpu.VMEM((CHUNK, _HK), F32)] * 3,
        name=name, compiler_params=_cp(("parallel", "arbitrary")))(z, z, z, b_all, ecol, oseg)


def _gla_bwd(z, b_all, ecol, att, states, do, prev, S, reverse, name):
    T = z.shape[0]
    E = T // S
    n = S // CHUNK
    _, _, oseg = _gla_consts(reverse)
    has_prev = prev is not None
    odt = BF16 if has_prev else F32
    last = 0 if reverse else CHUNK - 1

    def body(*refs):
        (q_ref, k_ref, v_ref, b_ref, ec_ref, oseg_ref, att_ref, st_ref, do_ref) = refs[:9]
        refs = refs[9:]
        if has_prev:
            pq_ref, pk_ref, pv_ref = refs[:3]
            refs = refs[3:]
        (dq_ref, dk_ref, dv_ref, db_ref, dst, b_s, q_s, k_s, da_s, dqb_s, dkd_s, dk3_s, dbn_s, dsp_s) = refs

        @pl.when(pl.program_id(1) == 0)
        def _():
            dst[...] = jnp.zeros_like(dst)

        q = q_ref[...] * (GLA_DK ** -0.5)
        k = k_ref[...]
        v = v_ref[...]
        b = b_ref[...]
        bl_row = b_ref[last:last + 1, :]
        eb = jnp.exp(b)
        ekd = jnp.exp(bl_row - b)
        qb = q * eb
        kd = k * ekd
        b_s[...] = b
        q_s[...] = q
        k_s[...] = k
        att = att_ref[...]
        s_all = st_ref[0]
        dsn = dst[...]
        e_col = ec_ref[0]
        do = do_ref[...]
        lane = lax.broadcasted_iota(jnp.int32, (1, _HK), 1) % GLA_DK
        rowi = lax.broadcasted_iota(jnp.int32, (CHUNK, 1), 0)
        keep = (rowi <= lane) if reverse else (rowi >= lane)
        for h in range(GLA_HEADS):
            ks_ = slice(h * GLA_DK, (h + 1) * GLA_DK)
            vs_ = slice(h * GLA_DV, (h + 1) * GLA_DV)
            do_h = do[:, vs_]
            s_h = s_all[ks_, :]
            dsn_h = dsn[ks_, :]
            dqb_s[:, ks_] = _dotb(do_h, s_h, _NT)
            dsp_s[ks_, :] = _dotb(qb[:, ks_], do_h, _TN) + dsn_h * e_col[ks_, :]
            da_s[:, ks_] = _dotb(do_h, v[:, vs_], _NT)
            dv_h = _dotb(att[:, ks_], do_h, _TN) + _dotb(kd[:, ks_], dsn_h, _NN)
            if has_prev:
                dv_h = dv_h + pv_ref[:, vs_]
            dv_ref[:, vs_] = dv_h.astype(odt)
            dkd_s[:, ks_] = _dotb(v[:, vs_], dsn_h, _NT)
        da_s[...] = jnp.where(keep, da_s[...], 0.0)
        dqb = dqb_s[...]
        dkd = dkd_s[...]
        x = dsn * s_all * e_col
        dbl_row = _dotf(jnp.ones((8, GLA_DV), F32), x, _NT)[0:1, :] + jnp.sum(dkd * kd, axis=0, keepdims=True)

        blk0 = (rowi // _SUB) * _SUB

        def cols(jj, carry):
            dq3, db3 = list(carry[:_NSUB]), list(carry[_NSUB:])
            sel = [jnp.where(lane == blk0 + (jj * _COLS + u), da_s[...], 0.0).astype(BF16) for u in range(_COLS)]
            dcols = jnp.dot(jnp.concatenate(sel, axis=0), oseg_ref[...], preferred_element_type=F32)
            for u in range(_COLS):
                jp = jj * _COLS + u
                for s in range(_NSUB):
                    rs_ = slice(s * _SUB, (s + 1) * _SUB)
                    bj = b_s[pl.ds(s * _SUB + jp, 1), :]
                    kj = k_s[pl.ds(s * _SUB + jp, 1), :]
                    tm_ = dcols[u * CHUNK + s * _SUB:u * CHUNK + (s + 1) * _SUB, :] * jnp.exp(jnp.minimum(b_s[rs_, :] - bj, 0.0))
                    dq3[s] = dq3[s] + tm_ * kj
                    gq = tm_ * q_s[rs_, :]
                    dk3_s[pl.ds(s * _SUB + jp, 1), :] = jnp.sum(gq, axis=0, keepdims=True)
                    w = gq * kj
                    dbn_s[pl.ds(s * _SUB + jp, 1), :] = jnp.sum(w, axis=0, keepdims=True)
                    db3[s] = db3[s] + w
            return tuple(dq3) + tuple(db3)

        zero = jnp.zeros((_SUB, _HK), F32)
        acc = lax.fori_loop(0, _SUB // _COLS, cols, (zero,) * (2 * _NSUB))
        dq3 = jnp.concatenate(acc[:_NSUB], axis=0)
        db3 = jnp.concatenate(acc[_NSUB:], axis=0)
        head = lax.broadcasted_iota(jnp.int32, (1, _HK), 1) // GLA_DK
        dq_x, db_x = [], []
        dk_x = jnp.zeros((CHUNK, _HK), F32)
        db_k = jnp.zeros((CHUNK, _HK), F32)
        for s in range(_NSUB):
            if s not in _gla_cross_blocks(reverse):
                dq_x.append(zero)
                db_x.append(zero)
                continue
            r0 = s * _SUB
            qt, eq, kt, ek, nmat = _gla_cross_terms(s, reverse, b_s, q_s, k_s, oseg_ref)
            seen = (lane >= r0 + _SUB) if reverse else (lane < r0)
            dax = jnp.where(seen, da_s[r0:r0 + _SUB, :], 0.0).astype(BF16)
            dqt = jnp.dot(dax, nmat, preferred_element_type=F32)
            full = lax.dot_general(dax, qt.astype(BF16), _TN, preferred_element_type=F32)
            dkt = full[0:CHUNK, :]
            for h in range(1, GLA_HEADS):
                dkt = jnp.where(head == h, full[h * CHUNK:(h + 1) * CHUNK, :], dkt)
            dq_x.append(dqt * eq)
            db_x.append(dqt * qt)
            dk_x = dk_x + dkt * ek
            db_k = db_k + dkt * kt
        dq = (dqb * eb + dq3 + jnp.concatenate(dq_x, axis=0)) * (GLA_DK ** -0.5)
        dk = dkd * ekd + dk3_s[...] + dk_x
        db = dqb * qb - dkd * kd + db3 - dbn_s[...] + jnp.concatenate(db_x, axis=0) - db_k
        db_ref[...] = jnp.where(rowi == last, db + dbl_row, db)
        if has_prev:
            dq = dq + pq_ref[...]
            dk = dk + pk_ref[...]
        dq_ref[...] = dq.astype(odt)
        dk_ref[...] = dk.astype(odt)
        dst[...] = dsp_s[...]

    def rowblk(e, c):
        return e * n + (c if reverse else (n - 1 - c))

    hk = pl.BlockSpec((CHUNK, _HK), lambda e, c: (rowblk(e, c), 0))
    hv = pl.BlockSpec((CHUNK, _HV), lambda e, c: (rowblk(e, c), 0))
    stb = pl.BlockSpec((1, _HK, GLA_DV), lambda e, c: (rowblk(e, c), 0, 0))
    in_specs = [pl.BlockSpec((CHUNK, _HK), lambda e, c: (rowblk(e, c), C_Q // _HK)),
                pl.BlockSpec((CHUNK, _HK), lambda e, c: (rowblk(e, c), C_K // _HK)),
                pl.BlockSpec((CHUNK, _HV), lambda e, c: (rowblk(e, c), C_V // _HV)),
                hk, stb, pl.BlockSpec((_HK, _HK), lambda e, c: (0, 0)), hk, stb, hv]
    args = [z, z, z, b_all, ecol, oseg, att, states, do]
    if has_prev:
        in_specs += [hk, hk, hv]
        args += list(prev)
    return pl.pallas_call(
        body, out_shape=(jax.ShapeDtypeStruct((T, _HK), odt), jax.ShapeDtypeStruct((T, _HK), odt),
                         jax.ShapeDtypeStruct((T, _HV), odt), jax.ShapeDtypeStruct((T, _HK), F32)),
        grid=(E, n), in_specs=in_specs, out_specs=(hk, hk, hv, hk),
        scratch_shapes=[pltpu.VMEM((_HK, GLA_DV), F32)] + [pltpu.VMEM((CHUNK, _HK), F32)] * 8 + [pltpu.VMEM((_HK, GLA_DV), F32)],
        name=name, compiler_params=_cp(("parallel", "arbitrary")))(*args)


def _gla_norm_fwd(of, ob, og, name):
    T = of.shape[0]
    tm = 256

    def body(f_ref, b_ref, g_ref, o_ref):
        for h in range(GLA_HEADS):
            vs_ = slice(h * GLA_DV, (h + 1) * GLA_DV)
            o = f_ref[:, vs_] + b_ref[:, vs_]
            o_ref[:, vs_] = o * lax.rsqrt(jnp.mean(o * o, axis=-1, keepdims=True) + EPS) * g_ref[:, vs_]

    row = pl.BlockSpec((tm, _HV), lambda i: (i, 0))
    vec = pl.BlockSpec((1, _HV), lambda i: (0, 0))
    return pl.pallas_call(body, out_shape=jax.ShapeDtypeStruct((T, _HV), F32), grid=(T // tm,),
                          in_specs=[row, row, vec], out_specs=row, name=name, compiler_params=_cp(("parallel",)))(of, ob, og)


def _gla_norm_bwd(of, ob, og, dpre, name):
    T = of.shape[0]
    tm = 256

    def body(f_ref, b_ref, g_ref, dp_ref, do_ref, dg_ref):
        @pl.when(pl.program_id(0) == 0)
        def _():
            dg_ref[...] = jnp.zeros_like(dg_ref)

        for h in range(GLA_HEADS):
            vs_ = slice(h * GLA_DV, (h + 1) * GLA_DV)
            o = f_ref[:, vs_] + b_ref[:, vs_]
            r = lax.rsqrt(jnp.mean(o * o, axis=-1, keepdims=True) + EPS)
            xh = o * r
            dp = dp_ref[:, vs_]
            dxh = dp * g_ref[:, vs_]
            do_ref[:, vs_] = r * (dxh - xh * jnp.mean(dxh * xh, axis=-1, keepdims=True))
            dg_ref[:, vs_] += jnp.sum(dp * xh, axis=0, keepdims=True)

    row = pl.BlockSpec((tm, _HV), lambda i: (i, 0))
    vec = pl.BlockSpec((1, _HV), lambda i: (0, 0))
    return pl.pallas_call(
        body, out_shape=(jax.ShapeDtypeStruct((T, _HV), F32), jax.ShapeDtypeStruct((1, _HV), F32)), grid=(T // tm,),
        in_specs=[row, row, vec, row], out_specs=(row, vec), name=name, compiler_params=_cp(("arbitrary",)))(of, ob, og, dpre)


_ANY = pl.BlockSpec(memory_space=pl.ANY)


def _coords():
    return lax.axis_index("x"), lax.axis_index("y"), lax.axis_index("c")


def _other_chips(x, y):
    return ((1 - x, y), (x, 1 - y), (1 - x, 1 - y))


def _gather_weights(arrays, chunks, name):
    n = len(arrays)
    pieces = []
    for k in range(max(chunks)):
        for i, a in enumerate(arrays):
            if k < chunks[i]:
                rc = a.shape[1] // chunks[i]
                pieces.append((i, k * rc, rc))
    m = len(pieces)

    def body(*refs):
        srcs, dsts = refs[:n], refs[n:2 * n]
        send_sems, recv_sems, local_sems = refs[2 * n:]
        x, y, c = _coords()
        me = 2 * x + y
        loc = [pltpu.make_async_copy(s, d.at[me], local_sems.at[i]) for i, (s, d) in enumerate(zip(srcs, dsts))]
        for cp in loc:
            cp.start()
        ici = []
        for p, (i, r0, rc) in enumerate(pieces):
            for j, (px, py) in enumerate(_other_chips(x, y)):
                ici.append(pltpu.make_async_remote_copy(
                    src_ref=srcs[i].at[c, pl.ds(r0, rc)], dst_ref=dsts[i].at[me, c, pl.ds(r0, rc)],
                    send_sem=send_sems.at[3 * p + j], recv_sem=recv_sems.at[3 * p + j],
                    device_id=(px, py, c), device_id_type=MESH))
        for cp in ici:
            cp.start()
        fwd = []
        for p, (i, r0, rc) in enumerate(pieces):
            for j, (px, py) in enumerate(_other_chips(x, y)):
                ici[3 * p + j].wait_recv()
                part = dsts[i].at[2 * px + py, c, pl.ds(r0, rc)]
                cp = pltpu.make_async_remote_copy(
                    src_ref=part, dst_ref=part, send_sem=send_sems.at[3 * m + 3 * p + j], recv_sem=recv_sems.at[3 * m + 3 * p + j],
                    device_id=(x, y, 1 - c), device_id_type=MESH)
                cp.start()
                fwd.append(cp)
        for cp in fwd:
            cp.wait_recv()
        for cp in ici + fwd:
            cp.wait_send()
        for cp in loc:
            cp.wait()

    return pl.pallas_call(
        body, out_shape=tuple(jax.ShapeDtypeStruct((4,) + a.shape, a.dtype) for a in arrays),
        in_specs=[_ANY] * n, out_specs=(_ANY,) * n,
        scratch_shapes=[pltpu.SemaphoreType.DMA((6 * m,)), pltpu.SemaphoreType.DMA((6 * m,)), pltpu.SemaphoreType.DMA((n,))],
        name=name)(*arrays)


def _sibling_exchange(layered, whole, name):
    nl, n = len(layered), len(layered) + len(whole)

    def body(*refs):
        srcs, dsts = refs[:n], refs[n:2 * n]
        send_sems, recv_sems = refs[2 * n:]
        x, y, c = _coords()
        rem = [pltpu.make_async_remote_copy(src_ref=(s.at[1 - c] if i < nl else s), dst_ref=d, send_sem=send_sems.at[i],
                                            recv_sem=recv_sems.at[i], device_id=(x, y, 1 - c), device_id_type=MESH)
               for i, (s, d) in enumerate(zip(srcs, dsts))]
        for cp in rem:
            cp.start()
        for cp in rem:
            cp.wait()

    outs = [jax.ShapeDtypeStruct(a.shape[1:], a.dtype) for a in layered] + [jax.ShapeDtypeStruct(a.shape, a.dtype) for a in whole]
    return pl.pallas_call(
        body, out_shape=tuple(outs), in_specs=[_ANY] * n, out_specs=(_ANY,) * n,
        scratch_shapes=[pltpu.SemaphoreType.DMA((n,)), pltpu.SemaphoreType.DMA((n,))], name=name)(*layered, *whole)


def _chip_exchange(scatter, bcast, name):
    ns, n = len(scatter), len(scatter) + len(bcast)

    def body(*refs):
        srcs, dsts = refs[:n], refs[n:2 * n]
        send_sems, recv_sems, local_sems = refs[2 * n:]
        x, y, c = _coords()
        me = 2 * x + y
        loc = [pltpu.make_async_copy((s.at[me] if i < ns else s), d.at[me], local_sems.at[i])
               for i, (s, d) in enumerate(zip(srcs, dsts))]
        for cp in loc:
            cp.start()
        rem = []
        for j, (px, py) in enumerate(_other_chips(x, y)):
            for i, (s, d) in enumerate(zip(srcs, dsts)):
                rem.append(pltpu.make_async_remote_copy(
                    src_ref=(s.at[2 * px + py] if i < ns else s), dst_ref=d.at[me], send_sem=send_sems.at[n * j + i],
                    recv_sem=recv_sems.at[n * j + i], device_id=(px, py, c), device_id_type=MESH))
        for cp in rem:
            cp.start()
        for cp in rem:
            cp.wait()
        for cp in loc:
            cp.wait()

    outs = [jax.ShapeDtypeStruct(a.shape, a.dtype) for a in scatter] + [jax.ShapeDtypeStruct((4,) + a.shape, a.dtype) for a in bcast]
    return pl.pallas_call(
        body, out_shape=tuple(outs), in_specs=[_ANY] * n, out_specs=(_ANY,) * n,
        scratch_shapes=[pltpu.SemaphoreType.DMA((3 * n,)), pltpu.SemaphoreType.DMA((3 * n,)), pltpu.SemaphoreType.DMA((n,))],
        name=name)(*scatter, *bcast)


_EW_BLOCK_BYTES = 2 * 1024 * 1024


def _tile2d(R, C):
    if R % 256 == 0 and 256 * C * 4 <= _EW_BLOCK_BYTES:
        return 256, C
    bc = 256 if C % 256 == 0 else C
    for br in range(R, 0, -1):
        if R % br == 0 and (br % 8 == 0 or br == R) and br * bc * 4 <= _EW_BLOCK_BYTES:
            return br, bc
    return R, bc


def _sum_slots(r, name):
    n, R, C = r.shape
    br, bc = _tile2d(R, C)

    def body(r_ref, o_ref):
        acc = r_ref[0].astype(F32)
        for i in range(1, n):
            acc = acc + r_ref[i].astype(F32)
        o_ref[...] = acc

    return pl.pallas_call(body, out_shape=jax.ShapeDtypeStruct((R, C), F32), grid=(R // br, C // bc),
                          in_specs=[pl.BlockSpec((n, br, bc), lambda i, j: (0, i, j))],
                          out_specs=pl.BlockSpec((br, bc), lambda i, j: (i, j)),
                          name=name, compiler_params=_cp(("parallel", "parallel")))(r)


_SMEM = pl.BlockSpec(memory_space=pltpu.SMEM)


def _add2(a, b, out_dtype, name, pick=None):
    R, C = b.shape
    br, bc = _tile2d(R, C)
    blk = pl.BlockSpec((br, bc), lambda i, j: (i, j))
    if pick is None:
        def body(a_ref, b_ref, o_ref):
            o_ref[...] = (a_ref[...].astype(F32) + b_ref[...].astype(F32)).astype(out_dtype)
        in_specs, args = [blk, blk], (a, b)
    else:
        def body(c_ref, a_ref, b_ref, o_ref):
            av = jnp.where(c_ref[0] == 0, a_ref[0], a_ref[1])
            o_ref[...] = (av.astype(F32) + b_ref[...].astype(F32)).astype(out_dtype)
        in_specs = [_SMEM, pl.BlockSpec((2, br, bc), lambda i, j: (0, i, j)), blk]
        args = (pick.reshape(1).astype(jnp.int32), a, b)
    return pl.pallas_call(body, out_shape=jax.ShapeDtypeStruct((R, C), out_dtype), grid=(R // br, C // bc), in_specs=in_specs,
                          out_specs=blk, name=name, compiler_params=_cp(("parallel", "parallel")))(*args)


def _adamw_math(w, g, m, v):
    m = ADAM_B1 * m + (1.0 - ADAM_B1) * g
    v = ADAM_B2 * v + (1.0 - ADAM_B2) * (g * g)
    m_hat = m / (1.0 - ADAM_B1 ** ADAM_STEP)
    v_hat = v / (1.0 - ADAM_B2 ** ADAM_STEP)
    delta = -ADAM_LR * (m_hat / (jnp.sqrt(v_hat) + ADAM_EPS) + ADAM_WD * w)
    return delta, m, v


def _adamw(w, gs, m, v, name, pick=None):
    R, C = w.shape
    br, bc = _tile2d(R, C)
    blk = pl.BlockSpec((br, bc), lambda i, j: (i, j))
    if pick is None:
        g_specs = [pl.BlockSpec((g.shape[0], br, bc), lambda i, j: (0, i, j)) if g.ndim == 3 else blk for g in gs]
        lead = ()
    else:
        nb = (R // 2) // br
        assert nb * br * 2 == R
        g_specs = [pl.BlockSpec((br, bc), lambda i, j: (i % nb, j))] * 2
        lead = (pick.reshape(1).astype(jnp.int32),)

    def body(*refs):
        if pick is not None:
            c_ref, refs = refs[0], refs[1:]
        w_ref = refs[0]
        g_refs = refs[1:1 + len(gs)]
        m_ref, v_ref, g_out, d_out, m_out, v_out = refs[1 + len(gs):]
        if pick is None:
            g = None
            for gr in g_refs:
                parts = [gr[i] for i in range(gr.shape[0])] if len(gr.shape) == 3 else [gr[...]]
                for p in parts:
                    g = p if g is None else g + p
        else:
            g = jnp.where(pl.program_id(0) // nb == c_ref[0], g_refs[0][...], g_refs[1][...])
        d, mn, vn = _adamw_math(w_ref[...], g, m_ref[...], v_ref[...])
        g_out[...] = g
        d_out[...] = d
        m_out[...] = mn
        v_out[...] = vn

    return pl.pallas_call(
        body, out_shape=tuple(jax.ShapeDtypeStruct((R, C), F32) for _ in range(4)), grid=(R // br, C // bc),
        in_specs=[_SMEM] * len(lead) + [blk] + g_specs + [blk, blk], out_specs=(blk,) * 4, name=name,
        compiler_params=_cp(("parallel", "parallel")))(*lead, w, *gs, m, v)


WEIGHTS = ("norm_g", "w_in", "conv_w", "conv_b", "conv_ln_g", "conv_ln_b", "na_q_g", "na_k_g", "na_rpb", "gla_a2_f",
           "gla_ab_f", "gla_a2_b", "gla_ab_b", "gla_o_g", "pool_w", "pool_scale", "w_out")
_REPL = ("norm_g", "conv_b", "conv_ln_g", "conv_ln_b", "na_q_g", "na_k_g", "na_rpb", "gla_ab_f", "gla_ab_b", "gla_o_g",
         "pool_w", "pool_scale")
_SHARD_SMALL = ("conv_w", "gla_a2_f", "gla_a2_b")
_PACK_ROWS = 8 * 128


def _pack(arrs):
    flat = jnp.concatenate([a.reshape(-1) for a in arrs])
    n = -(-flat.shape[0] // _PACK_ROWS) * _PACK_ROWS
    return jnp.pad(flat, (0, n - flat.shape[0])).reshape(-1, 128)


def _unpack(p, shapes):
    flat = p.reshape(-1)
    out, o = [], 0
    for s in shapes:
        n = int(np.prod(s))
        out.append(flat[o:o + n].reshape(s))
        o += n
    return out


def _to_layout_rows(w):
    pad = jnp.zeros((w.shape[0], NZ - N_IN, w.shape[2]), w.dtype)
    return jnp.concatenate([w[:, :5120], w[:, 5152:6176], w[:, 5120:5152], pad], axis=1)


def _from_layout_rows(w):
    return jnp.concatenate([w[:, :5120], w[:, LR_OFF:LR_OFF + 32], w[:, 5120:LR_OFF]], axis=1)


def _reduce_gradients(p_a, p_b, small_g, ci):
    s_a, s_b, s_small = _sibling_exchange((p_a, p_b), (small_g,), "grad_to_sibling")
    flat = lambda a: a.reshape(a.shape[0], -1, a.shape[-1])
    c_a = _add2(flat(p_a), s_a.reshape(-1, s_a.shape[-1]), BF16, "chip_sum_a", pick=ci).reshape(s_a.shape)
    c_b = _add2(flat(p_b), s_b.reshape(-1, s_b.shape[-1]), BF16, "chip_sum_b", pick=ci).reshape(s_b.shape)
    c_small = _add2(small_g, s_small, F32, "chip_sum_small")
    r_a, r_b, r_small = _chip_exchange((c_a, c_b), (c_small,), "grad_to_owner")
    own_a = _sum_slots(r_a, "sum_a")
    own_b = _sum_slots(r_b, "sum_b")
    sib_a, sib_b = _sibling_exchange((), (own_a, own_b), "reduced_to_sibling")
    return (own_a, sib_a), (own_b, sib_b), r_small


def _layer_fwd(l, x, P, S, target=None):
    n = f"l{l}_"
    h = _rmsnorm_fwd(x, P["norm_g"], n + "rms_fwd")
    z = _matmul(h, P["w_in"], dims="nn", out_dtype=F32, tm=1024, tn=1280, tk=D_MODEL, name=n + "mm_z")
    yc = _conv_fwd(z, P["conv_w32"], P["conv_b"], S, n + "conv_fwd")
    pre_a = _ln_silu_fwd(yc, P["conv_ln_g"], P["conv_ln_b"], n + "ln_fwd")
    pre_b = _na_fwd(z, P["na_q_g"], P["na_k_g"], P["na_bias"], S, n + "na_fwd")
    bf, ecf = _gla_decay_fwd(z, P["a2_f"], P["gla_ab_f"], False, n + "gla_decay_f")
    bb, ecb = _gla_decay_fwd(z, P["a2_b"], P["gla_ab_b"], True, n + "gla_decay_b")
    of, af, sf = _gla_fwd(z, bf, ecf, S, False, n + "gla_fwd_f")
    ob, ab, sb = _gla_fwd(z, bb, ecb, S, True, n + "gla_fwd_b")
    pre_c = _gla_norm_fwd(of, ob, P["gla_o_g"], n + "gla_norm_fwd")
    pre_d = _pool_fwd(z, P["pool_w_bf"], P["pool_scale"], S, n + "pool_fwd")
    pres = (pre_a, pre_b, pre_c, pre_d)
    res = _out_proj_fwd(pres, z, P["w_out"], x, target, n + "out_proj")
    y, out = res[0], (res[1] if target is None else res[1:])
    return out, dict(x=x, h=h, z=z, yc=yc, pres=pres, of=of, af=af, sf=sf, ob=ob, ab=ab, sb=sb, y=y, bf=bf, ecf=ecf, bb=bb, ecb=ecb)


def _layer_bwd(l, dout, dout_bf, sv, P, S):
    n = f"l{l}_"
    z = sv["z"]
    T = z.shape[0]
    d_w_out = _matmul(sv["y"], dout_bf, dims="tn", out_dtype=BF16, tm=1024, tn=2048, tk=512, name=n + "mm_dwout")
    dpa, dpb, dpc, dpd, dga, dgb, dgc, dgd = _out_proj_bwd(dout_bf, P["w_out_t"], sv["pres"], z, n + "out_proj_bwd")
    dyc, d_ln_g, d_ln_b = _ln_silu_bwd(sv["yc"], P["conv_ln_g"], P["conv_ln_b"], dpa, n + "ln_bwd")
    dval, dglu, d_cw, d_cb = _conv_bwd(z, P["conv_w32"], dyc, S, n + "conv_bwd")
    dq, dk, dv, dbias, d_qg, d_kg = _na_bwd(z, P["na_q_g"], P["na_k_g"], P["na_bias"], dpb, S, n + "na_bwd")
    d_rpb = _na_rpb_grad(dbias, n + "na_rpb")
    do, d_og = _gla_norm_bwd(sv["of"], sv["ob"], P["gla_o_g"], dpc, n + "gla_norm_bwd")
    part = _gla_bwd(z, sv["bf"], sv["ecf"], sv["af"], sv["sf"], do, None, S, False, n + "gla_bwd_f")
    dcq, dck, dcv, db_b = _gla_bwd(z, sv["bb"], sv["ecb"], sv["ab"], sv["sb"], do, part[:3], S, True, n + "gla_bwd_b")
    dlr, d_a2f, d_abf, d_a2b, d_abb = _gla_decay_bwd(z, P["a2_f"], P["gla_ab_f"], P["a2_b"], P["gla_ab_b"], part[3], db_b,
                                                     n + "gla_decay_bwd")
    dd, d_pw, d_ps = _pool_bwd(z, P["pool_w_bf"], P["pool_scale"], dpd, S, n + "pool_bwd")
    dz = _concat_cols([dval, dglu, dga, dq, dk, dv, dgb, dcq, dck, dcv, dgc, dd, dgd, dlr], NZ, n + "dz_concat")
    dh = _matmul(dz, P["w_in_t"], dims="nn", out_dtype=F32, tm=1024, tn=1024, tk=3200, name=n + "mm_dh")
    d_w_in = _matmul(dz, sv["h"], dims="tn", out_dtype=BF16, tm=1280, tn=1024, tk=1024, name=n + "mm_dwin")
    dx, dx_bf, d_ng = _rmsnorm_bwd(sv["x"], P["norm_g"], dh, dout, n + "rms_bwd")
    grads = dict(norm_g=d_ng[0], w_in=d_w_in, conv_w=d_cw[:CONV_K], conv_b=d_cb[0], conv_ln_g=d_ln_g[0], conv_ln_b=d_ln_b[0],
                 na_q_g=d_qg.reshape(NA_HEADS, NA_DH), na_k_g=d_kg.reshape(NA_HEADS, NA_DH), na_rpb=d_rpb,
                 gla_a2_f=d_a2f[0:GLA_RANK], gla_ab_f=d_abf[0], gla_a2_b=d_a2b[GLA_RANK:2 * GLA_RANK], gla_ab_b=d_abb[0],
                 gla_o_g=d_og.reshape(GLA_HEADS, GLA_DV), pool_w=d_pw, pool_scale=d_ps[0], w_out=d_w_out)
    return dx, dx_bf, grads


def kernel(x, norm_g, w_in, conv_w, conv_b, conv_ln_g, conv_ln_b, na_q_g, na_k_g, na_rpb, gla_a2_f, gla_ab_f, gla_a2_b, gla_ab_b, gla_o_g, pool_w, pool_scale, w_out, loss_target, m_norm_g, m_w_in, m_conv_w, m_conv_b, m_conv_ln_g, m_conv_ln_b, m_na_q_g, m_na_k_g, m_na_rpb, m_gla_a2_f, m_gla_ab_f, m_gla_a2_b, m_gla_ab_b, m_gla_o_g, m_pool_w, m_pool_scale, m_w_out, v_norm_g, v_w_in, v_conv_w, v_conv_b, v_conv_ln_g, v_conv_ln_b, v_na_q_g, v_na_k_g, v_na_rpb, v_gla_a2_f, v_gla_ab_f, v_gla_a2_b, v_gla_ab_b, v_gla_o_g, v_pool_w, v_pool_scale, v_w_out):
    W = dict(norm_g=norm_g, w_in=w_in, conv_w=conv_w, conv_b=conv_b, conv_ln_g=conv_ln_g, conv_ln_b=conv_ln_b, na_q_g=na_q_g,
             na_k_g=na_k_g, na_rpb=na_rpb, gla_a2_f=gla_a2_f, gla_ab_f=gla_ab_f, gla_a2_b=gla_a2_b, gla_ab_b=gla_ab_b,
             gla_o_g=gla_o_g, pool_w=pool_w, pool_scale=pool_scale, w_out=w_out)
    M = dict(norm_g=m_norm_g, w_in=m_w_in, conv_w=m_conv_w, conv_b=m_conv_b, conv_ln_g=m_conv_ln_g, conv_ln_b=m_conv_ln_b,
             na_q_g=m_na_q_g, na_k_g=m_na_k_g, na_rpb=m_na_rpb, gla_a2_f=m_gla_a2_f, gla_ab_f=m_gla_ab_f, gla_a2_b=m_gla_a2_b,
             gla_ab_b=m_gla_ab_b, gla_o_g=m_gla_o_g, pool_w=m_pool_w, pool_scale=m_pool_scale, w_out=m_w_out)
    V = dict(norm_g=v_norm_g, w_in=v_w_in, conv_w=v_conv_w, conv_b=v_conv_b, conv_ln_g=v_conv_ln_g, conv_ln_b=v_conv_ln_b,
             na_q_g=v_na_q_g, na_k_g=v_na_k_g, na_rpb=v_na_rpb, gla_a2_f=v_gla_a2_f, gla_ab_f=v_gla_ab_f, gla_a2_b=v_gla_a2_b,
             gla_ab_b=v_gla_ab_b, gla_o_g=v_gla_o_g, pool_w=v_pool_w, pool_scale=v_pool_scale, w_out=v_w_out)
    E, S, D = x.shape
    T = E * S
    L = DEPTH
    xi, yi, ci = _coords()
    chip = 2 * xi + yi
    cw_sh, a2_sh = conv_w.shape[-1], gla_a2_f.shape[-1]

    small_sh = jnp.concatenate([
        jnp.pad(conv_w, ((0, 0), (0, 1), (0, 0))),
        jnp.pad(gla_a2_f, ((0, 0), (0, 0), (0, 128 - a2_sh))),
        jnp.pad(gla_a2_b, ((0, 0), (0, 0), (0, 128 - a2_sh)))], axis=1)
    w_in_tr, m_w_in_tr, v_w_in_tr = (jnp.transpose(a, (0, 2, 1)) for a in (w_in, m_w_in, v_w_in))
    g_win, g_wout, g_small = _gather_weights((w_in_tr.astype(BF16), w_out.astype(BF16), small_sh), (1, 2, 1), "gather_weights")
    w_in_t_full = _to_layout_rows(jnp.transpose(g_win, (1, 0, 2, 3)).reshape(L, N_IN, D))
    w_in_full = jnp.transpose(w_in_t_full, (0, 2, 1))
    w_out_full = jnp.transpose(g_wout, (1, 0, 2, 3)).reshape(L, D, D)
    conv_w_full = jnp.transpose(g_small[:, :, 0:32, :], (1, 2, 0, 3)).reshape(L, 32, 4 * cw_sh)
    a2f_full = jnp.transpose(g_small[:, :, 32:48, :a2_sh], (1, 2, 0, 3)).reshape(L, GLA_RANK, 4 * a2_sh)
    a2b_full = jnp.transpose(g_small[:, :, 48:64, :a2_sh], (1, 2, 0, 3)).reshape(L, GLA_RANK, 4 * a2_sh)

    params = []
    for l in range(L):
        params.append(dict(
            norm_g=norm_g[l][None], w_in=w_in_full[l], w_out=w_out_full[l], w_in_t=w_in_t_full[l], w_out_t=w_out_full[l].T,
            conv_w32=conv_w_full[l], conv_b=conv_b[l][None],
            conv_ln_g=conv_ln_g[l][None], conv_ln_b=conv_ln_b[l][None], na_q_g=na_q_g[l].reshape(1, GROUP_W),
            na_k_g=na_k_g[l].reshape(1, GROUP_W), na_bias=_na_bias(na_rpb[l], f"l{l}_na_bias"),
            a2_f=jnp.zeros((128, _HK), F32).at[0:GLA_RANK].set(a2f_full[l]),
            a2_b=jnp.zeros((128, _HK), F32).at[GLA_RANK:2 * GLA_RANK].set(a2b_full[l]),
            gla_ab_f=gla_ab_f[l][None], gla_ab_b=gla_ab_b[l][None], gla_o_g=gla_o_g[l].reshape(1, GROUP_W),
            pool_w_bf=pool_w[l].astype(BF16), pool_scale=pool_scale[l][None]))

    act = x.reshape(T, D)
    saved = []
    for l in range(L):
        act, sv = _layer_fwd(l, act, params[l], S, loss_target.reshape(T, D) if l == L - 1 else None)
        saved.append(sv)
    dact, dact_bf, loss_loc = act
    loss = lax.psum(loss_loc[0, 0], ("x", "y", "c"))
    grads = [None] * L
    for l in reversed(range(L)):
        dact, dact_bf, grads[l] = _layer_bwd(l, dact, dact_bf, saved[l], params[l], S)
    grad_x = dact.reshape(E, S, D)
    G = {k: jnp.stack([grads[l][k] for l in range(L)]) for k in WEIGHTS}

    cols_in, cols_out = N_IN // 4, D
    p_win = _from_layout_rows(G["w_in"]).reshape(L, 4, cols_in, D)
    p_wout = G["w_out"].reshape(L, 4, D // 4, D)
    small_names = _REPL + _SHARD_SMALL
    small_g = _pack([G[k] for k in small_names])
    g_in, g_out, r_small = _reduce_gradients(p_win, p_wout, small_g, ci)

    rows_in, rows_out = L * cols_in, L * (D // 4)
    res = {}
    res["w_in"] = [jnp.transpose(a.reshape(L, cols_in, D), (0, 2, 1)) for a in _adamw(
        w_in_tr.reshape(rows_in, D), g_in, m_w_in_tr.reshape(rows_in, D), v_w_in_tr.reshape(rows_in, D), "adamw_w_in", pick=ci)]
    res["w_out"] = [a.reshape(L, D // 4, D) for a in _adamw(
        w_out.reshape(rows_out, cols_out), g_out, m_w_out.reshape(rows_out, cols_out),
        v_w_out.reshape(rows_out, cols_out), "adamw_w_out", pick=ci)]
    zeros_sh = [jnp.zeros(G[k].shape, F32) for k in _SHARD_SMALL]
    pk = lambda dct: _pack([dct[k] for k in _REPL] + zeros_sh)
    small_res = _adamw(pk(W), (r_small,), pk(M), pk(V), "adamw_small")
    shapes = [G[k].shape for k in small_names]
    unp = [_unpack(a, shapes) for a in small_res]
    for i, k in enumerate(_REPL):
        res[k] = [u[i] for u in unp]
    g_sh = []
    for i, k in enumerate(_SHARD_SMALL):
        gfull = unp[0][len(_REPL) + i]
        wdt = W[k].shape[-1]
        g_sh.append(lax.dynamic_slice_in_dim(gfull, chip * wdt, wdt, axis=2))
    g_sh_p = _pack(g_sh)
    sh_res = _adamw(_pack([W[k] for k in _SHARD_SMALL]), (g_sh_p,), _pack([M[k] for k in _SHARD_SMALL]),
                    _pack([V[k] for k in _SHARD_SMALL]), "adamw_shard_small")
    shapes2 = [W[k].shape for k in _SHARD_SMALL]
    unp2 = [_unpack(a, shapes2) for a in sh_res]
    for i, k in enumerate(_SHARD_SMALL):
        res[k] = [u[i] for u in unp2]

    outs = [loss, grad_x]
    for j in range(4):
        outs += [res[k][j] for k in WEIGHTS]
    return tuple(outs)
```

```python
import functools

import numpy as np
import jax
import jax.numpy as jnp
from jax import lax
from jax.experimental import pallas as pl
from jax.experimental.pallas import tpu as pltpu

F32 = jnp.float32
BF16 = jnp.bfloat16
HI = lax.Precision.HIGHEST
MESH = pl.DeviceIdType.MESH

EPS = 1e-6
D_MODEL = 2048
GROUP_W = 512
SEQ = 2048
DEPTH = 2
N_IN = 6176
GRID_W = 64
CONV_K = 31
NA_HEADS = 8
NA_DH = 64
NA_ROWS = 8
NA_COLS = 16
GLA_HEADS = 4
GLA_DK = 64
GLA_DV = 128
GLA_RANK = 16
GLA_TAU = 16.0
CHUNK = 64
POOL_WINDOWS = (2, 4, 8, 16)
ADAM_LR, ADAM_B1, ADAM_B2, ADAM_EPS, ADAM_WD, ADAM_STEP = 0.001, 0.9, 0.999, 1e-08, 0.01, 10

A_VAL, A_GLU, A_GATE = 0, 512, 1024
B_Q, B_K, B_V, B_GATE = 1536, 2048, 2560, 3072
C_Q, C_K, C_V, C_GATE = 3584, 3840, 4096, 4608
D_VAL, D_GATE = 5120, 5632
LR_OFF = 6144
NZ = 6400
NEG = -1e30
VMEM_LIMIT = 56 * 1024 * 1024


def _cp(sem=None):
    return pltpu.CompilerParams(dimension_semantics=sem, vmem_limit_bytes=VMEM_LIMIT)


def _sigmoid(x):
    return 1.0 / (1.0 + jnp.exp(-x))


def _silu(x):
    return x * _sigmoid(x)


def _dsilu(x):
    s = _sigmoid(x)
    return s * (1.0 + x * (1.0 - s))


def _matmul(a, b, *, dims, out_dtype, tm, tn, tk, name, res=None):
    if dims == "nn":
        (M, K), N = a.shape, b.shape[1]
    elif dims == "nt":
        (M, K), N = a.shape, b.shape[0]
    else:
        (K, M), N = a.shape, b.shape[1]
    tm, tn, tk = min(tm, M), min(tn, N), min(tk, K)
    nk = K // tk
    assert M % tm == 0 and N % tn == 0 and K % tk == 0, (M, N, K, tm, tn, tk)
    dn = {"nn": (((1,), (0,)), ((), ())), "nt": (((1,), (1,)), ((), ())), "tn": (((0,), (0,)), ((), ()))}[dims]
    if dims == "tn":
        a_spec = pl.BlockSpec((tk, tm), lambda i, j, k: (k, i))
    else:
        a_spec = pl.BlockSpec((tm, tk), lambda i, j, k: (i, k))
    if dims == "nt":
        b_spec = pl.BlockSpec((tn, tk), lambda i, j, k: (j, k))
    else:
        b_spec = pl.BlockSpec((tk, tn), lambda i, j, k: (k, j))
    o_spec = pl.BlockSpec((tm, tn), lambda i, j, k: (i, j))
    has_res = res is not None

    def body(*refs):
        if has_res:
            a_ref, b_ref, r_ref, o_ref, acc = refs
        else:
            a_ref, b_ref, o_ref, acc = refs
        k = pl.program_id(2)

        @pl.when(k == 0)
        def _():
            acc[...] = jnp.zeros_like(acc)

        acc[...] += lax.dot_general(a_ref[...], b_ref[...], dn, preferred_element_type=F32)

        @pl.when(k == nk - 1)
        def _():
            r = acc[...]
            if has_res:
                r = r + r_ref[...]
            o_ref[...] = r.astype(o_ref.dtype)

    in_specs = [a_spec, b_spec] + ([o_spec] if has_res else [])
    args = (a, b) + ((res,) if has_res else ())
    return pl.pallas_call(
        body, out_shape=jax.ShapeDtypeStruct((M, N), out_dtype), grid=(M // tm, N // tn, nk),
        in_specs=in_specs, out_specs=o_spec, scratch_shapes=[pltpu.VMEM((tm, tn), F32)],
        name=name, compiler_params=_cp(("parallel", "parallel", "arbitrary")))(*args)


def _concat_cols(pieces, width, name):
    T = pieces[0].shape[0]
    tm = min(512, T)
    dt = pieces[0].dtype
    offs = np.cumsum([0] + [p.shape[1] for p in pieces])

    def body(*refs):
        o_ref = refs[-1]
        for p_ref, a, b in zip(refs[:-1], offs[:-1], offs[1:]):
            o_ref[:, a:b] = p_ref[...]
        if offs[-1] < width:
            o_ref[:, offs[-1]:width] = jnp.zeros((tm, width - offs[-1]), dt)

    return pl.pallas_call(
        body, out_shape=jax.ShapeDtypeStruct((T, width), dt), grid=(T // tm,),
        in_specs=[pl.BlockSpec((tm, p.shape[1]), lambda i: (i, 0)) for p in pieces],
        out_specs=pl.BlockSpec((tm, width), lambda i: (i, 0)), name=name, compiler_params=_cp(("parallel",)))(*pieces)


def _rmsnorm_fwd(x, g, name):
    T, D = x.shape
    tm = 256

    def body(x_ref, g_ref, h_ref):
        xv = x_ref[...]
        r = lax.rsqrt(jnp.mean(xv * xv, axis=-1, keepdims=True) + EPS)
        h_ref[...] = (xv * r * g_ref[...]).astype(h_ref.dtype)

    return pl.pallas_call(
        body, out_shape=jax.ShapeDtypeStruct((T, D), BF16), grid=(T // tm,),
        in_specs=[pl.BlockSpec((tm, D), lambda i: (i, 0)), pl.BlockSpec((1, D), lambda i: (0, 0))],
        out_specs=pl.BlockSpec((tm, D), lambda i: (i, 0)), name=name, compiler_params=_cp(("parallel",)))(x, g)


def _rmsnorm_bwd(x, g, dh, dres, name):
    T, D = x.shape
    tm = 256

    def body(x_ref, g_ref, dh_ref, dres_ref, dx_ref, dxb_ref, dg_ref):
        xv = x_ref[...]
        r = lax.rsqrt(jnp.mean(xv * xv, axis=-1, keepdims=True) + EPS)
        xh = xv * r
        dh_v = dh_ref[...]
        dxh = dh_v * g_ref[...]
        dx = r * (dxh - xh * jnp.mean(dxh * xh, axis=-1, keepdims=True)) + dres_ref[...]
        dx_ref[...] = dx
        dxb_ref[...] = dx.astype(BF16)

        @pl.when(pl.program_id(0) == 0)
        def _():
            dg_ref[...] = jnp.zeros_like(dg_ref)

        dg_ref[...] += jnp.sum(dh_v * xh, axis=0, keepdims=True)

    row = pl.BlockSpec((tm, D), lambda i: (i, 0))
    vec = pl.BlockSpec((1, D), lambda i: (0, 0))
    return pl.pallas_call(
        body, out_shape=(jax.ShapeDtypeStruct((T, D), F32), jax.ShapeDtypeStruct((T, D), BF16), jax.ShapeDtypeStruct((1, D), F32)),
        grid=(T // tm,), in_specs=[row, vec, row, row], out_specs=(row, row, vec), name=name,
        compiler_params=_cp(("arbitrary",)))(x, g, dh, dres)


_GATE_COLS = (A_GATE // GROUP_W, B_GATE // GROUP_W, C_GATE // GROUP_W, D_GATE // GROUP_W)
_OP_TM = 256


def _out_proj_fwd(pres, z, w_out, x, target, name):
    T, D = x.shape
    tm = min(_OP_TM, T)
    with_loss = target is not None

    def body(*refs):
        pa, pb, pc, pd, ga, gb, gc, gd, w_ref, x_ref = refs[:10]
        refs = refs[10:]
        if with_loss:
            t_ref, y_ref, d_ref, db_ref, l_ref = refs
        else:
            y_ref, o_ref = refs
        for n_, (p, g) in enumerate(((pa, ga), (pb, gb), (pc, gc), (pd, gd))):
            y_ref[:, n_ * GROUP_W:(n_ + 1) * GROUP_W] = (p[...] * _silu(g[...])).astype(BF16)
        out = jnp.dot(y_ref[...], w_ref[...], preferred_element_type=F32) + x_ref[...]
        if with_loss:
            e = out - t_ref[...]
            d = e * (1.0 / D)
            d_ref[...] = d
            db_ref[...] = d.astype(BF16)

            @pl.when(pl.program_id(0) == 0)
            def _():
                l_ref[...] = jnp.zeros_like(l_ref)

            l_ref[...] += jnp.sum(jnp.sum(e * e, axis=-1, keepdims=True) * (0.5 / D), axis=0, keepdims=True)
        else:
            o_ref[...] = out

    pre_spec = pl.BlockSpec((tm, GROUP_W), lambda i: (i, 0))
    gate_specs = [pl.BlockSpec((tm, GROUP_W), functools.partial(lambda i, c: (i, c), c=c)) for c in _GATE_COLS]
    row = pl.BlockSpec((tm, D), lambda i: (i, 0))
    w_spec = pl.BlockSpec((4 * GROUP_W, D), lambda i: (0, 0))
    in_specs = [pre_spec] * 4 + gate_specs + [w_spec, row]
    args = list(pres) + [z, z, z, z, w_out, x]
    if with_loss:
        in_specs.append(row)
        args.append(target)
        out_shape = (jax.ShapeDtypeStruct((T, D), BF16), jax.ShapeDtypeStruct((T, D), F32), jax.ShapeDtypeStruct((T, D), BF16),
                     jax.ShapeDtypeStruct((1, 1), F32))
        out_specs = (row, row, row, pl.BlockSpec((1, 1), lambda i: (0, 0)))
    else:
        out_shape = (jax.ShapeDtypeStruct((T, D), BF16), jax.ShapeDtypeStruct((T, D), F32))
        out_specs = (row, row)
    return pl.pallas_call(body, out_shape=out_shape, grid=(T // tm,), in_specs=in_specs, out_specs=out_specs, name=name,
                          compiler_params=_cp(("arbitrary",)))(*args)


def _out_proj_bwd(dout_bf, w_out_t, pres, z, name):
    T, D = dout_bf.shape
    tm = min(_OP_TM, T)

    def body(do_ref, w_ref, pa, pb, pc, pd, ga, gb, gc, gd, dpa, dpb, dpc, dpd, dga, dgb, dgc, dgd):
        dy = jnp.dot(do_ref[...], w_ref[...], preferred_element_type=F32)
        for n_, (p, g, dp, dg) in enumerate(((pa, ga, dpa, dga), (pb, gb, dpb, dgb), (pc, gc, dpc, dgc), (pd, gd, dpd, dgd))):
            d = dy[:, n_ * GROUP_W:(n_ + 1) * GROUP_W]
            gv = g[...]
            dp[...] = d * _silu(gv)
            dg[...] = (d * p[...] * _dsilu(gv)).astype(BF16)

    pre_spec = pl.BlockSpec((tm, GROUP_W), lambda i: (i, 0))
    gate_specs = [pl.BlockSpec((tm, GROUP_W), functools.partial(lambda i, c: (i, c), c=c)) for c in _GATE_COLS]
    outs = tuple([jax.ShapeDtypeStruct((T, GROUP_W), F32)] * 4 + [jax.ShapeDtypeStruct((T, GROUP_W), BF16)] * 4)
    return pl.pallas_call(
        body, out_shape=outs, grid=(T // tm,),
        in_specs=[pl.BlockSpec((tm, D), lambda i: (i, 0)), pl.BlockSpec((D, 4 * GROUP_W), lambda i: (0, 0))] + [pre_spec] * 4 + gate_specs,
        out_specs=tuple([pre_spec] * 8), name=name, compiler_params=_cp(("parallel",)))(dout_bf, w_out_t, *pres, z, z, z, z)


_PAD = 16
_RC = 256


def _conv_fwd(z, conv_w32, conv_b, S, name):
    T = z.shape[0]
    E = T // S
    LW = 128

    def body(val_ref, glu_ref, w_ref, b_ref, y_ref, upad):
        upad[0:_PAD, :] = jnp.zeros((_PAD, LW), F32)
        upad[_PAD + S:_PAD + S + _PAD, :] = jnp.zeros((_PAD, LW), F32)
        upad[_PAD:_PAD + S, :] = val_ref[...] * _sigmoid(glu_ref[...])
        for r in range(S // _RC):
            acc = jnp.broadcast_to(b_ref[...], (_RC, LW))
            for k in range(CONV_K):
                st = r * _RC + k + 1
                acc = acc + upad[st:st + _RC, :] * w_ref[k:k + 1, :]
            y_ref[r * _RC:(r + 1) * _RC, :] = acc

    return pl.pallas_call(
        body, out_shape=jax.ShapeDtypeStruct((T, GROUP_W), F32), grid=(E, GROUP_W // LW),
        in_specs=[pl.BlockSpec((S, LW), lambda e, j: (e, A_VAL // LW + j)),
                  pl.BlockSpec((S, LW), lambda e, j: (e, A_GLU // LW + j)),
                  pl.BlockSpec((32, LW), lambda e, j: (0, j)),
                  pl.BlockSpec((1, LW), lambda e, j: (0, j))],
        out_specs=pl.BlockSpec((S, LW), lambda e, j: (e, j)),
        scratch_shapes=[pltpu.VMEM((S + 2 * _PAD, LW), F32)],
        name=name, compiler_params=_cp(("parallel", "parallel")))(z, z, conv_w32, conv_b)


def _conv_bwd(z, conv_w32, dyc, S, name):
    T = z.shape[0]
    E = T // S
    LW = 128

    def body(val_ref, glu_ref, w_ref, dy_ref, dval_ref, dglu_ref, dw_ref, db_ref, upad, dpad):
        e = pl.program_id(1)
        zeros = jnp.zeros((_PAD, LW), F32)
        upad[0:_PAD, :] = zeros
        upad[_PAD + S:_PAD + S + _PAD, :] = zeros
        dpad[0:_PAD, :] = zeros
        dpad[_PAD + S:_PAD + S + _PAD, :] = zeros
        upad[_PAD:_PAD + S, :] = val_ref[...] * _sigmoid(glu_ref[...])
        dpad[_PAD:_PAD + S, :] = dy_ref[...]

        @pl.when(e == 0)
        def _():
            dw_ref[...] = jnp.zeros_like(dw_ref)
            db_ref[...] = jnp.zeros_like(db_ref)

        db_ref[...] += jnp.sum(dy_ref[...], axis=0, keepdims=True)
        for r in range(S // _RC):
            dyr = dy_ref[r * _RC:(r + 1) * _RC, :]
            du = jnp.zeros((_RC, LW), F32)
            for k in range(CONV_K):
                st = r * _RC + k + 1
                dw_ref[k:k + 1, :] += jnp.sum(dyr * upad[st:st + _RC, :], axis=0, keepdims=True)
                sd = r * _RC + (CONV_K - 1 - k) + 1
                du = du + dpad[sd:sd + _RC, :] * w_ref[k:k + 1, :]
            sl = slice(r * _RC, (r + 1) * _RC)
            val = val_ref[sl, :]
            sg = _sigmoid(glu_ref[sl, :])
            dval_ref[sl, :] = (du * sg).astype(BF16)
            dglu_ref[sl, :] = (du * val * sg * (1.0 - sg)).astype(BF16)

    blk = pl.BlockSpec((S, LW), lambda j, e: (e, j))
    return pl.pallas_call(
        body, out_shape=(jax.ShapeDtypeStruct((T, GROUP_W), BF16), jax.ShapeDtypeStruct((T, GROUP_W), BF16),
                         jax.ShapeDtypeStruct((32, GROUP_W), F32), jax.ShapeDtypeStruct((1, GROUP_W), F32)),
        grid=(GROUP_W // LW, E),
        in_specs=[pl.BlockSpec((S, LW), lambda j, e: (e, A_VAL // LW + j)),
                  pl.BlockSpec((S, LW), lambda j, e: (e, A_GLU // LW + j)),
                  pl.BlockSpec((32, LW), lambda j, e: (0, j)), blk],
        out_specs=(blk, blk, pl.BlockSpec((32, LW), lambda j, e: (0, j)), pl.BlockSpec((1, LW), lambda j, e: (0, j))),
        scratch_shapes=[pltpu.VMEM((S + 2 * _PAD, LW), F32), pltpu.VMEM((S + 2 * _PAD, LW), F32)],
        name=name, compiler_params=_cp(("parallel", "arbitrary")))(z, z, conv_w32, dyc)


def _ln_silu_fwd(yc, g, b, name):
    T, C = yc.shape
    tm = 256

    def body(y_ref, g_ref, b_ref, o_ref):
        y = y_ref[...]
        mu = jnp.mean(y, axis=-1, keepdims=True)
        yc_ = y - mu
        r = lax.rsqrt(jnp.mean(yc_ * yc_, axis=-1, keepdims=True) + EPS)
        o_ref[...] = _silu(yc_ * r * g_ref[...] + b_ref[...])

    row = pl.BlockSpec((tm, C), lambda i: (i, 0))
    vec = pl.BlockSpec((1, C), lambda i: (0, 0))
    return pl.pallas_call(body, out_shape=jax.ShapeDtypeStruct((T, C), F32), grid=(T // tm,),
                          in_specs=[row, vec, vec], out_specs=row, name=name, compiler_params=_cp(("parallel",)))(yc, g, b)


def _ln_silu_bwd(yc, g, b, dpre, name):
    T, C = yc.shape
    tm = 256

    def body(y_ref, g_ref, b_ref, dp_ref, dy_ref, dg_ref, db_ref):
        y = y_ref[...]
        mu = jnp.mean(y, axis=-1, keepdims=True)
        yc_ = y - mu
        r = lax.rsqrt(jnp.mean(yc_ * yc_, axis=-1, keepdims=True) + EPS)
        xh = yc_ * r
        gv = g_ref[...]
        dln = dp_ref[...] * _dsilu(xh * gv + b_ref[...])
        dxh = dln * gv
        dy_ref[...] = r * (dxh - jnp.mean(dxh, axis=-1, keepdims=True) - xh * jnp.mean(dxh * xh, axis=-1, keepdims=True))

        @pl.when(pl.program_id(0) == 0)
        def _():
            dg_ref[...] = jnp.zeros_like(dg_ref)
            db_ref[...] = jnp.zeros_like(db_ref)

        dg_ref[...] += jnp.sum(dln * xh, axis=0, keepdims=True)
        db_ref[...] += jnp.sum(dln, axis=0, keepdims=True)

    row = pl.BlockSpec((tm, C), lambda i: (i, 0))
    vec = pl.BlockSpec((1, C), lambda i: (0, 0))
    return pl.pallas_call(
        body, out_shape=(jax.ShapeDtypeStruct((T, C), F32), jax.ShapeDtypeStruct((1, C), F32), jax.ShapeDtypeStruct((1, C), F32)),
        grid=(T // tm,), in_specs=[row, vec, vec, row], out_specs=(row, vec, vec), name=name,
        compiler_params=_cp(("arbitrary",)))(yc, g, b, dpre)


def _pool_counts(S, w, rows0, n):
    t = (lax.broadcasted_iota(jnp.int32, (n, 1), 0) + rows0)
    lo = jnp.maximum(t - w // 2, 0)
    hi = jnp.minimum(t + w // 2, S)
    return (hi - lo).astype(F32)


def _pool_fwd(z, pool_w, pool_scale, S, name):
    T = z.shape[0]
    E = T // S
    CG = 128

    def body(u_ref, w_ref, s_ref, o_ref, upad, dif):
        zeros = jnp.zeros((_PAD, GROUP_W), F32)
        upad[0:_PAD, :] = zeros
        upad[_PAD + S:_PAD + S + _PAD, :] = zeros
        upad[_PAD:_PAD + S, :] = u_ref[...]
        for gi, w in enumerate(POOL_WINDOWS):
            ls = slice(gi * CG, (gi + 1) * CG)
            for r in range(S // _RC):
                acc = jnp.zeros((_RC, CG), F32)
                for j in range(-(w // 2), w // 2):
                    st = _PAD + r * _RC + j
                    acc = acc + upad[st:st + _RC, ls]
                cnt = _pool_counts(S, w, r * _RC, _RC)
                dif[r * _RC:(r + 1) * _RC, :] = (acc / cnt - u_ref[r * _RC:(r + 1) * _RC, ls]).astype(BF16)
            yp = jnp.dot(dif[...], w_ref[gi], preferred_element_type=F32)
            o_ref[:, ls] = yp * s_ref[:, ls]

    return pl.pallas_call(
        body, out_shape=jax.ShapeDtypeStruct((T, GROUP_W), F32), grid=(E,),
        in_specs=[pl.BlockSpec((S, GROUP_W), lambda e: (e, D_VAL // GROUP_W)),
                  pl.BlockSpec((4, CG, CG), lambda e: (0, 0, 0)),
                  pl.BlockSpec((1, GROUP_W), lambda e: (0, 0))],
        out_specs=pl.BlockSpec((S, GROUP_W), lambda e: (e, 0)),
        scratch_shapes=[pltpu.VMEM((S + 2 * _PAD, GROUP_W), F32), pltpu.VMEM((S, CG), BF16)],
        name=name, compiler_params=_cp(("parallel",)))(z, pool_w, pool_scale)


def _pool_bwd(z, pool_w, pool_scale, dpre, S, name):
    T = z.shape[0]
    E = T // S
    CG = 128

    def body(u_ref, w_ref, s_ref, dp_ref, du_ref, dw_ref, ds_ref, upad, dif, qpad):
        zeros = jnp.zeros((_PAD, GROUP_W), F32)
        upad[0:_PAD, :] = zeros
        upad[_PAD + S:_PAD + S + _PAD, :] = zeros
        upad[_PAD:_PAD + S, :] = u_ref[...]
        zc = jnp.zeros((_PAD, CG), F32)
        qpad[0:_PAD, :] = zc
        qpad[_PAD + S:_PAD + S + _PAD, :] = zc

        @pl.when(pl.program_id(0) == 0)
        def _():
            dw_ref[...] = jnp.zeros_like(dw_ref)
            ds_ref[...] = jnp.zeros_like(ds_ref)

        for gi, w in enumerate(POOL_WINDOWS):
            ls = slice(gi * CG, (gi + 1) * CG)
            for r in range(S // _RC):
                acc = jnp.zeros((_RC, CG), F32)
                for j in range(-(w // 2), w // 2):
                    st = _PAD + r * _RC + j
                    acc = acc + upad[st:st + _RC, ls]
                cnt = _pool_counts(S, w, r * _RC, _RC)
                dif[r * _RC:(r + 1) * _RC, :] = (acc / cnt - u_ref[r * _RC:(r + 1) * _RC, ls]).astype(BF16)
            dp = dp_ref[:, ls]
            yp = jnp.dot(dif[...], w_ref[gi], preferred_element_type=F32)
            ds_ref[:, ls] += jnp.sum(dp * yp, axis=0, keepdims=True)
            dys = (dp * s_ref[:, ls]).astype(BF16)
            dw_ref[gi] += lax.dot_general(dif[...], dys, (((0,), (0,)), ((), ())), preferred_element_type=F32)
            dm = lax.dot_general(dys, w_ref[gi], (((1,), (1,)), ((), ())), preferred_element_type=F32)
            for r in range(S // _RC):
                cnt = _pool_counts(S, w, r * _RC, _RC)
                qpad[_PAD + r * _RC:_PAD + (r + 1) * _RC, :] = dm[r * _RC:(r + 1) * _RC, :] / cnt
            for r in range(S // _RC):
                acc = -dm[r * _RC:(r + 1) * _RC, :]
                for j in range(-(w // 2) + 1, w // 2 + 1):
                    st = _PAD + r * _RC + j
                    acc = acc + qpad[st:st + _RC, :]
                du_ref[r * _RC:(r + 1) * _RC, ls] = acc.astype(BF16)

    return pl.pallas_call(
        body, out_shape=(jax.ShapeDtypeStruct((T, GROUP_W), BF16), jax.ShapeDtypeStruct((4, CG, CG), F32),
                         jax.ShapeDtypeStruct((1, GROUP_W), F32)), grid=(E,),
        in_specs=[pl.BlockSpec((S, GROUP_W), lambda e: (e, D_VAL // GROUP_W)),
                  pl.BlockSpec((4, CG, CG), lambda e: (0, 0, 0)),
                  pl.BlockSpec((1, GROUP_W), lambda e: (0, 0)),
                  pl.BlockSpec((S, GROUP_W), lambda e: (e, 0))],
        out_specs=(pl.BlockSpec((S, GROUP_W), lambda e: (e, 0)), pl.BlockSpec((4, CG, CG), lambda e: (0, 0, 0)),
                   pl.BlockSpec((1, GROUP_W), lambda e: (0, 0))),
        scratch_shapes=[pltpu.VMEM((S + 2 * _PAD, GROUP_W), F32), pltpu.VMEM((S, CG), BF16),
                        pltpu.VMEM((S + 2 * _PAD, CG), F32)],
        name=name, compiler_params=_cp(("arbitrary",)))(z, pool_w, pool_scale, dpre)


def _na_tables():
    d = np.arange(NA_ROWS)[:, None]
    kr = np.arange(NA_ROWS)[None, :]
    ro = kr - d + (NA_ROWS - 1)
    qc = np.arange(GRID_W)[:, None]
    kc = np.arange(GRID_W)[None, :]
    cs = np.clip(qc - NA_COLS // 2, 0, GRID_W - NA_COLS)
    valid = (kc >= cs) & (kc < cs + NA_COLS)
    co = np.clip(kc - qc + (NA_COLS - 1), 0, 2 * NA_COLS - 2)
    return ro, co, valid


def _na_onehots():
    ro, co, valid = _na_tables()
    e_np = np.zeros((GRID_W, GRID_W, 128), np.float32)
    qi, ki = np.nonzero(valid)
    e_np[qi, ki, co[qi, ki]] = 1.0
    a_np = np.zeros((16, NA_ROWS * NA_ROWS), np.float32)
    a_np[ro.reshape(-1), np.arange(NA_ROWS * NA_ROWS)] = 1.0
    mask = np.where(valid, 0.0, NEG).astype(np.float32).reshape(1, GRID_W * GRID_W)
    return e_np.reshape(GRID_W * GRID_W, 128), a_np, mask


def _na_bias(rpb, name):
    e_np, _, mask = _na_onehots()
    H = NA_HEADS
    rp = jnp.pad(rpb, ((0, 0), (0, 1), (0, 128 - rpb.shape[2])))

    def bands(r_ref, e_ref, m_ref, o_ref):
        o_ref[0] = lax.dot_general(r_ref[0], e_ref[...], (((1,), (1,)), ((), ())), precision=HI,
                                   preferred_element_type=F32) + m_ref[...]

    t = pl.pallas_call(
        bands, out_shape=jax.ShapeDtypeStruct((H, 16, GRID_W * GRID_W), F32), grid=(H,),
        in_specs=[pl.BlockSpec((1, 16, 128), lambda h: (h, 0, 0)),
                  pl.BlockSpec((GRID_W * GRID_W, 128), lambda h: (0, 0)),
                  pl.BlockSpec((1, GRID_W * GRID_W), lambda h: (0, 0))],
        out_specs=pl.BlockSpec((1, 16, GRID_W * GRID_W), lambda h: (h, 0, 0)),
        name=name + "_bands", compiler_params=_cp(("parallel",)))(rp, jnp.asarray(e_np), jnp.asarray(mask))
    t = t.reshape(H, 16, GRID_W, GRID_W)

    def place(t_ref, o_ref):
        for d in range(NA_ROWS):
            for kr in range(NA_ROWS):
                o_ref[0, d, :, kr * GRID_W:(kr + 1) * GRID_W] = t_ref[0, kr - d + NA_ROWS - 1]

    return pl.pallas_call(
        place, out_shape=jax.ShapeDtypeStruct((H, NA_ROWS, GRID_W, NA_ROWS * GRID_W), F32), grid=(H,),
        in_specs=[pl.BlockSpec((1, 16, GRID_W, GRID_W), lambda h: (h, 0, 0, 0))],
        out_specs=pl.BlockSpec((1, NA_ROWS, GRID_W, NA_ROWS * GRID_W), lambda h: (h, 0, 0, 0)),
        name=name, compiler_params=_cp(("parallel",)))(t)


def _seg_mean_matrix(width, seg):
    i = np.arange(width)
    return jnp.asarray((i[:, None] // seg == i[None, :] // seg).astype(np.float32) / seg, BF16)


def _seg_mean(x, seg_ref):
    hi = x.astype(BF16)
    lo = (x - hi.astype(F32)).astype(BF16)
    return (jnp.dot(hi, seg_ref[...], preferred_element_type=F32) + jnp.dot(lo, seg_ref[...], preferred_element_type=F32))


def _na_fwd(z, qg, kg, bias, S, name):
    T = z.shape[0]
    E = T // S
    rows = S // GRID_W
    WIN = NA_ROWS * GRID_W
    seg = _seg_mean_matrix(128, NA_DH)

    def body(q_ref, k_ref, v_ref, qg_ref, kg_ref, bias_ref, seg_ref, o_ref, qs, ks, vs, s_all, p_all):
        for c in range(S // _RC):
            sl = slice(c * _RC, (c + 1) * _RC)
            q = q_ref[sl, :]
            k = k_ref[sl, :]
            qn = q * lax.rsqrt(_seg_mean(q * q, seg_ref) + EPS) * qg_ref[...]
            kn = k * lax.rsqrt(_seg_mean(k * k, seg_ref) + EPS) * kg_ref[...]
            v = v_ref[sl, :]
            for hh in range(2):
                ls = slice(hh * NA_DH, (hh + 1) * NA_DH)
                qs[hh, sl, :] = qn[:, ls].astype(BF16)
                ks[hh, sl, :] = kn[:, ls].astype(BF16)
                vs[hh, sl, :] = v[:, ls].astype(BF16)
        def where(r):
            rs = jnp.clip(r - NA_ROWS // 2, 0, rows - NA_ROWS)
            return rs, pl.multiple_of(r * GRID_W, GRID_W), pl.multiple_of(rs * GRID_W, GRID_W)

        def scores(r, carry):
            rs, q0, k0 = where(r)
            for hh in range(2):
                s = lax.dot_general(qs[hh, pl.ds(q0, GRID_W), :], ks[hh, pl.ds(k0, WIN), :], (((1,), (1,)), ((), ())),
                                    preferred_element_type=F32) * (NA_DH ** -0.5)
                s_all[hh, pl.ds(q0, GRID_W), :] = s + bias_ref[hh, r - rs]
            return carry
        lax.fori_loop(0, rows, scores, 0, unroll=4)

        def soft(r, carry):
            _, q0, _ = where(r)
            for hh in range(2):
                s = s_all[hh, pl.ds(q0, GRID_W), :]
                p = jnp.exp(s - jnp.max(s, axis=-1, keepdims=True))
                p_all[hh, pl.ds(q0, GRID_W), :] = (p * (1.0 / jnp.sum(p, axis=-1, keepdims=True))).astype(BF16)
            return carry
        lax.fori_loop(0, rows, soft, 0, unroll=2)

        def outp(r, carry):
            _, q0, k0 = where(r)
            outs = [jnp.dot(p_all[hh, pl.ds(q0, GRID_W), :], vs[hh, pl.ds(k0, WIN), :], preferred_element_type=F32)
                    for hh in range(2)]
            o_ref[pl.ds(q0, GRID_W), :] = jnp.concatenate(outs, axis=1)
            return carry
        lax.fori_loop(0, rows, outp, 0, unroll=4)

    LW = 128
    return pl.pallas_call(
        body, out_shape=jax.ShapeDtypeStruct((T, GROUP_W), F32), grid=(E, GROUP_W // LW),
        in_specs=[pl.BlockSpec((S, LW), lambda e, j: (e, B_Q // LW + j)),
                  pl.BlockSpec((S, LW), lambda e, j: (e, B_K // LW + j)),
                  pl.BlockSpec((S, LW), lambda e, j: (e, B_V // LW + j)),
                  pl.BlockSpec((1, LW), lambda e, j: (0, j)),
                  pl.BlockSpec((1, LW), lambda e, j: (0, j)),
                  pl.BlockSpec((2, NA_ROWS, GRID_W, WIN), lambda e, j: (j, 0, 0, 0)),
                  pl.BlockSpec((LW, LW), lambda e, j: (0, 0))],
        out_specs=pl.BlockSpec((S, LW), lambda e, j: (e, j)),
        scratch_shapes=[pltpu.VMEM((2, S, NA_DH), BF16)] * 3 + [pltpu.VMEM((2, S, WIN), F32), pltpu.VMEM((2, S, WIN), BF16)],
        name=name, compiler_params=_cp(("parallel", "parallel")))(z, z, z, qg, kg, bias, seg)


def _na_bwd(z, qg, kg, bias, do, S, name):
    T = z.shape[0]
    E = T // S
    rows = S // GRID_W
    WIN = NA_ROWS * GRID_W
    seg = _seg_mean_matrix(128, NA_DH)
    SC = NA_DH ** -0.5

    def body(q_ref, k_ref, v_ref, qg_ref, kg_ref, bias_ref, seg_ref, do_ref,
             dq_ref, dk_ref, dv_ref, dbias_ref, dqg_ref, dkg_ref, qs, ks, vs, dos, dqn, dkn, dvs, akt, avt,
             s_all, dp_all, p_all, ds_all):
        e = pl.program_id(1)

        @pl.when(e == 0)
        def _():
            dbias_ref[...] = jnp.zeros_like(dbias_ref)
            dqg_ref[...] = jnp.zeros_like(dqg_ref)
            dkg_ref[...] = jnp.zeros_like(dkg_ref)

        for c in range(S // _RC):
            sl = slice(c * _RC, (c + 1) * _RC)
            q = q_ref[sl, :]
            k = k_ref[sl, :]
            qn = q * lax.rsqrt(_seg_mean(q * q, seg_ref) + EPS) * qg_ref[...]
            kn = k * lax.rsqrt(_seg_mean(k * k, seg_ref) + EPS) * kg_ref[...]
            v = v_ref[sl, :]
            dd = do_ref[sl, :]
            for hh in range(2):
                ls = slice(hh * NA_DH, (hh + 1) * NA_DH)
                qs[hh, sl, :] = qn[:, ls].astype(BF16)
                ks[hh, sl, :] = kn[:, ls].astype(BF16)
                vs[hh, sl, :] = v[:, ls].astype(BF16)
                dos[hh, sl, :] = dd[:, ls].astype(BF16)
        akt[...] = jnp.zeros_like(akt)
        avt[...] = jnp.zeros_like(avt)

        def where(r):
            rs = jnp.clip(r - NA_ROWS // 2, 0, rows - NA_ROWS)
            return rs, pl.multiple_of(r * GRID_W, GRID_W), pl.multiple_of(rs * GRID_W, GRID_W)

        for hh in range(2):
            ls = slice(hh * NA_DH, (hh + 1) * NA_DH)

            def products(r, carry, hh=hh):
                rs, q0, k0 = where(r)
                s = lax.dot_general(qs[hh, pl.ds(q0, GRID_W), :], ks[hh, pl.ds(k0, WIN), :], (((1,), (1,)), ((), ())),
                                    preferred_element_type=F32) * SC
                s_all[pl.ds(q0, GRID_W), :] = s + bias_ref[hh, r - rs]
                dp_all[pl.ds(q0, GRID_W), :] = lax.dot_general(dos[hh, pl.ds(q0, GRID_W), :], vs[hh, pl.ds(k0, WIN), :],
                                                               (((1,), (1,)), ((), ())), preferred_element_type=F32)
                return carry
            lax.fori_loop(0, rows, products, 0, unroll=4)

            def soft(r, carry, hh=hh):
                rs, q0, _ = where(r)
                s = s_all[pl.ds(q0, GRID_W), :]
                p = jnp.exp(s - jnp.max(s, axis=-1, keepdims=True))
                p = p * (1.0 / jnp.sum(p, axis=-1, keepdims=True))
                dp = dp_all[pl.ds(q0, GRID_W), :]
                ds = p * (dp - jnp.sum(p * dp, axis=-1, keepdims=True))
                dbias_ref[hh, r - rs] += ds
                p_all[pl.ds(q0, GRID_W), :] = p.astype(BF16)
                ds_all[pl.ds(q0, GRID_W), :] = ds.astype(BF16)
                return carry
            lax.fori_loop(0, rows, soft, 0, unroll=2)

            def grads(r, carry, hh=hh, ls=ls):
                rs, q0, k0 = where(r)
                par = rs % 2
                t0 = (rs + par) // 2
                qr = qs[hh, pl.ds(q0, GRID_W), :]
                dor = dos[hh, pl.ds(q0, GRID_W), :]
                dsb = ds_all[pl.ds(q0, GRID_W), :]
                dqn[pl.ds(q0, GRID_W), ls] = jnp.dot(dsb, ks[hh, pl.ds(k0, WIN), :], preferred_element_type=F32) * SC
                dkt = lax.dot_general(qr, dsb, (((0,), (0,)), ((), ())), preferred_element_type=F32) * SC
                dvt = lax.dot_general(dor, p_all[pl.ds(q0, GRID_W), :], (((0,), (0,)), ((), ())), preferred_element_type=F32)
                akt[hh, par, pl.ds(t0, WIN // 128)] += jnp.stack([dkt[:, 128 * i:128 * (i + 1)] for i in range(WIN // 128)])
                avt[hh, par, pl.ds(t0, WIN // 128)] += jnp.stack([dvt[:, 128 * i:128 * (i + 1)] for i in range(WIN // 128)])
                return carry
            lax.fori_loop(0, rows, grads, 0, unroll=4)

        for hh in range(2):
            ls = slice(hh * NA_DH, (hh + 1) * NA_DH)
            for i in range(S // 128):
                for acc, dst in ((akt, dkn), (avt, dvs)):
                    odd = jnp.concatenate([acc[hh, 1, i][:, NA_DH:], acc[hh, 1, i + 1][:, :NA_DH]], axis=1)
                    dst[128 * i:128 * (i + 1), ls] = (acc[hh, 0, i] + odd).T

        for c in range(S // _RC):
            sl = slice(c * _RC, (c + 1) * _RC)
            for x_ref, g_ref, dn, dx_ref, dg_ref in ((q_ref, qg_ref, dqn, dq_ref, dqg_ref), (k_ref, kg_ref, dkn, dk_ref, dkg_ref)):
                x = x_ref[sl, :]
                r_ = lax.rsqrt(_seg_mean(x * x, seg_ref) + EPS)
                xh = x * r_
                d = dn[sl, :]
                dxh = d * g_ref[...]
                mean = _seg_mean(dxh * xh, seg_ref)
                dx_ref[sl, :] = (r_ * (dxh - xh * mean)).astype(BF16)
                dg_ref[...] += jnp.sum(d * xh, axis=0, keepdims=True)
            dv_ref[sl, :] = dvs[sl, :].astype(BF16)

    LW = 128
    blk = pl.BlockSpec((S, LW), lambda j, e: (e, j))
    vec = pl.BlockSpec((1, LW), lambda j, e: (0, j))
    bsp = pl.BlockSpec((2, NA_ROWS, GRID_W, WIN), lambda j, e: (j, 0, 0, 0))
    return pl.pallas_call(
        body, out_shape=(jax.ShapeDtypeStruct((T, GROUP_W), BF16),) * 3 + (
            jax.ShapeDtypeStruct((NA_HEADS, NA_ROWS, GRID_W, WIN), F32),
            jax.ShapeDtypeStruct((1, GROUP_W), F32), jax.ShapeDtypeStruct((1, GROUP_W), F32)),
        grid=(GROUP_W // LW, E),
        in_specs=[pl.BlockSpec((S, LW), lambda j, e: (e, B_Q // LW + j)),
                  pl.BlockSpec((S, LW), lambda j, e: (e, B_K // LW + j)),
                  pl.BlockSpec((S, LW), lambda j, e: (e, B_V // LW + j)),
                  vec, vec, bsp, pl.BlockSpec((LW, LW), lambda j, e: (0, 0)), blk],
        out_specs=(blk, blk, blk, bsp, vec, vec),
        scratch_shapes=[pltpu.VMEM((2, S, NA_DH), BF16)] * 4 + [pltpu.VMEM((S, LW), F32)] * 3
        + [pltpu.VMEM((2, 2, S // 128 + 1, NA_DH, 128), F32)] * 2
        + [pltpu.VMEM((S, WIN), F32)] * 2 + [pltpu.VMEM((S, WIN), BF16)] * 2,
        name=name, compiler_params=_cp(("parallel", "arbitrary")))(z, z, z, qg, kg, bias, seg, do)


def _na_rpb_grad(dbias, name):
    e_np, _, _ = _na_onehots()
    H = NA_HEADS
    nro = 2 * NA_ROWS - 1

    def fold(x_ref, o_ref):
        for ro in range(nro):
            acc = None
            for d in range(NA_ROWS):
                kr = ro + d - (NA_ROWS - 1)
                if 0 <= kr < NA_ROWS:
                    blk = x_ref[0, d, :, kr * GRID_W:(kr + 1) * GRID_W]
                    acc = blk if acc is None else acc + blk
            o_ref[0, ro] = acc
        o_ref[0, nro] = jnp.zeros((GRID_W, GRID_W), F32)

    t = pl.pallas_call(
        fold, out_shape=jax.ShapeDtypeStruct((H, 16, GRID_W, GRID_W), F32), grid=(H,),
        in_specs=[pl.BlockSpec((1, NA_ROWS, GRID_W, NA_ROWS * GRID_W), lambda h: (h, 0, 0, 0))],
        out_specs=pl.BlockSpec((1, 16, GRID_W, GRID_W), lambda h: (h, 0, 0, 0)),
        name=name + "_fold", compiler_params=_cp(("parallel",)))(dbias)
    t = t.reshape(H, 16, GRID_W * GRID_W)

    def body(x_ref, e_ref, o_ref):
        o_ref[0] = jnp.dot(x_ref[0], e_ref[...], precision=HI, preferred_element_type=F32)

    out = pl.pallas_call(
        body, out_shape=jax.ShapeDtypeStruct((H, 16, 128), F32), grid=(H,),
        in_specs=[pl.BlockSpec((1, 16, GRID_W * GRID_W), lambda h: (h, 0, 0)),
                  pl.BlockSpec((GRID_W * GRID_W, 128), lambda h: (0, 0))],
        out_specs=pl.BlockSpec((1, 16, 128), lambda h: (h, 0, 0)),
        name=name, compiler_params=_cp(("parallel",)))(t, jnp.asarray(e_np))
    return out[:, :nro, :2 * NA_COLS - 1]


_HK = GLA_HEADS * GLA_DK
_HV = GLA_HEADS * GLA_DV


def _gla_consts(reverse):
    i = np.arange(CHUNK)
    tri = (i[:, None] <= i[None, :]) if reverse else (i[:, None] >= i[None, :])
    j = np.arange(_HK)
    oseg = (j[:, None] // GLA_DK == j[None, :] // GLA_DK)
    return (jnp.asarray(tri.astype(np.float32)), jnp.asarray(tri.T.astype(np.float32)), jnp.asarray(oseg.astype(np.float32), BF16))


def _log_decay(lr, a2, ab):
    zg = jnp.dot(lr, a2, precision=HI, preferred_element_type=F32) + ab
    g = (jnp.minimum(zg, 0.0) - jnp.log(1.0 + jnp.exp(-jnp.abs(zg)))) * (1.0 / GLA_TAU)
    return zg, g


def _dotf(a, b, dn):
    return lax.dot_general(a, b, dn, precision=HI, preferred_element_type=F32)


def _dotb(a, b, dn):
    return lax.dot_general(a.astype(BF16), b.astype(BF16), dn, preferred_element_type=F32)


_COLS = 4
_SUB = 16
_NSUB = CHUNK // _SUB


def _gla_cross_blocks(reverse):
    return range(0, _NSUB - 1) if reverse else range(1, _NSUB)


def _gla_cross_terms(s, reverse, b_s, q_s, k_s, oseg_ref):
    r0 = s * _SUB
    ref = r0 + (_SUB - 1 if reverse else 0)
    bref = b_s[ref:ref + 1, :]
    rowj = lax.broadcasted_iota(jnp.int32, (CHUNK, 1), 0)
    seen = (rowj >= r0 + _SUB) if reverse else (rowj < r0)
    ek = jnp.where(seen, jnp.exp(jnp.minimum(bref - b_s[...], 0.0)), 0.0)
    kt = k_s[...] * ek
    eq = jnp.exp(jnp.minimum(b_s[r0:r0 + _SUB, :] - bref, 0.0))
    qt = q_s[r0:r0 + _SUB, :] * eq
    nmat = jnp.concatenate([kt.astype(BF16)] * GLA_HEADS, axis=0) * oseg_ref[...]
    return qt, eq, kt, ek, nmat


_NN = (((1,), (0,)), ((), ()))
_NT = (((1,), (1,)), ((), ()))
_TN = (((0,), (0,)), ((), ()))


_DT = 256


def _gla_block_tri(reverse):
    i = np.arange(_DT)
    same = i[:, None] // CHUNK == i[None, :] // CHUNK
    tri = (i[:, None] <= i[None, :]) if reverse else (i[:, None] >= i[None, :])
    return (tri & same).astype(np.float32)


def _gla_decay_fwd(z, a2, ab, reverse, name):
    T = z.shape[0]
    nc = _DT // CHUNK

    def body(lr_ref, a2_ref, ab_ref, m_ref, b_ref, ec_ref):
        _, g = _log_decay(lr_ref[...], a2_ref[...], ab_ref[...])
        b_ref[...] = _dotf(m_ref[...], g, _NN)
        for c in range(nc):
            ec_ref[c] = jnp.exp(_dotf(g[c * CHUNK:(c + 1) * CHUNK, :], jnp.ones((CHUNK, GLA_DV), F32), _TN))

    return pl.pallas_call(
        body, out_shape=(jax.ShapeDtypeStruct((T, _HK), F32), jax.ShapeDtypeStruct((T // CHUNK, _HK, GLA_DV), F32)),
        grid=(T // _DT,),
        in_specs=[pl.BlockSpec((_DT, 128), lambda i: (i, LR_OFF // 128)),
                  pl.BlockSpec((128, _HK), lambda i: (0, 0)),
                  pl.BlockSpec((1, _HK), lambda i: (0, 0)),
                  pl.BlockSpec((_DT, _DT), lambda i: (0, 0))],
        out_specs=(pl.BlockSpec((_DT, _HK), lambda i: (i, 0)), pl.BlockSpec((nc, _HK, GLA_DV), lambda i: (i, 0, 0))),
        name=name, compiler_params=_cp(("parallel",)))(z, a2, ab, jnp.asarray(_gla_block_tri(reverse)))


def _gla_decay_bwd(z, a2_f, ab_f, a2_b, ab_b, db_f, db_b, name):
    T = z.shape[0]

    def body(lr_ref, a2f_ref, abf_ref, a2b_ref, abb_ref, mf_ref, mb_ref, dbf_ref, dbb_ref,
             dlr_ref, da2f_ref, dabf_ref, da2b_ref, dabb_ref):
        @pl.when(pl.program_id(0) == 0)
        def _():
            for r in (da2f_ref, dabf_ref, da2b_ref, dabb_ref):
                r[...] = jnp.zeros_like(r)

        lr = lr_ref[...]
        dlr = jnp.zeros((_DT, 128), F32)
        for a2_ref, ab_ref, mt_ref, db_ref, da2_ref, dab_ref in ((a2f_ref, abf_ref, mf_ref, dbf_ref, da2f_ref, dabf_ref),
                                                                 (a2b_ref, abb_ref, mb_ref, dbb_ref, da2b_ref, dabb_ref)):
            zg, _ = _log_decay(lr, a2_ref[...], ab_ref[...])
            dg = _dotf(mt_ref[...], db_ref[...], _NN)
            dzg = dg * (1.0 / (1.0 + jnp.exp(zg))) * (1.0 / GLA_TAU)
            dlr = dlr + _dotf(dzg, a2_ref[...], _NT)
            da2_ref[...] += _dotf(lr, dzg, _TN)
            dab_ref[...] += jnp.sum(dzg, axis=0, keepdims=True)
        dlr_ref[...] = dlr.astype(BF16)

    a2s = pl.BlockSpec((128, _HK), lambda i: (0, 0))
    abs_ = pl.BlockSpec((1, _HK), lambda i: (0, 0))
    ms = pl.BlockSpec((_DT, _DT), lambda i: (0, 0))
    row = pl.BlockSpec((_DT, _HK), lambda i: (i, 0))
    return pl.pallas_call(
        body, out_shape=(jax.ShapeDtypeStruct((T, 128), BF16), jax.ShapeDtypeStruct((128, _HK), F32), jax.ShapeDtypeStruct((1, _HK), F32),
                         jax.ShapeDtypeStruct((128, _HK), F32), jax.ShapeDtypeStruct((1, _HK), F32)),
        grid=(T // _DT,),
        in_specs=[pl.BlockSpec((_DT, 128), lambda i: (i, LR_OFF // 128)), a2s, abs_, a2s, abs_, ms, ms, row, row],
        out_specs=(pl.BlockSpec((_DT, 128), lambda i: (i, 0)), a2s, abs_, a2s, abs_),
        name=name, compiler_params=_cp(("arbitrary",)))(
            z, a2_f, ab_f, a2_b, ab_b, jnp.asarray(_gla_block_tri(False).T), jnp.asarray(_gla_block_tri(True).T), db_f, db_b)


def _gla_fwd(z, b_all, ecol, S, reverse, name):
    T = z.shape[0]
    E = T // S
    n = S // CHUNK
    _, _, oseg = _gla_consts(reverse)
    last = 0 if reverse else CHUNK - 1

    def body(q_ref, k_ref, v_ref, b_ref, ec_ref, oseg_ref, o_ref, a_ref, st_ref, st, b_s, q_s, k_s):
        @pl.when(pl.program_id(1) == 0)
        def _():
            st[...] = jnp.zeros_like(st)

        q = q_ref[...] * (GLA_DK ** -0.5)
        k = k_ref[...]
        v = v_ref[...]
        b = b_ref[...]
        bl_row = b_ref[last:last + 1, :]
        e_col = ec_ref[0]
        b_s[...] = b
        q_s[...] = q
        k_s[...] = k
        lane = lax.broadcasted_iota(jnp.int32, (1, _HK), 1) % GLA_DK

        rowi = lax.broadcasted_iota(jnp.int32, (CHUNK, 1), 0)
        blk0 = (rowi // _SUB) * _SUB

        def cols(jj, a):
            ts = []
            for u in range(_COLS):
                jp = jj * _COLS + u
                tiles = []
                for s in range(_NSUB):
                    rs_ = slice(s * _SUB, (s + 1) * _SUB)
                    bj = b_s[pl.ds(s * _SUB + jp, 1), :]
                    kj = k_s[pl.ds(s * _SUB + jp, 1), :]
                    tiles.append(q_s[rs_, :] * jnp.exp(jnp.minimum(b_s[rs_, :] - bj, 0.0)) * kj)
                ts.append(jnp.concatenate(tiles, axis=0).astype(BF16))
            r = jnp.dot(jnp.concatenate(ts, axis=0), oseg_ref[...], preferred_element_type=F32)
            for u in range(_COLS):
                a = jnp.where(lane == blk0 + (jj * _COLS + u), r[u * CHUNK:(u + 1) * CHUNK, :], a)
            return a

        a = lax.fori_loop(0, _SUB // _COLS, cols, jnp.zeros((CHUNK, _HK), F32))
        keep = (rowi <= lane) if reverse else (rowi >= lane)
        a = jnp.where(keep, a, 0.0)
        cross = []
        for s in range(_NSUB):
            if s in _gla_cross_blocks(reverse):
                qt, _, _, _, nmat = _gla_cross_terms(s, reverse, b_s, q_s, k_s, oseg_ref)
                cross.append(lax.dot_general(qt.astype(BF16), nmat, _NT, preferred_element_type=F32))
            else:
                cross.append(jnp.zeros((_SUB, _HK), F32))
        a = a + jnp.concatenate(cross, axis=0)
        a_ref[...] = a
        st_ref[0] = st[...]
        qb = q * jnp.exp(b)
        kd = k * jnp.exp(bl_row - b)
        for h in range(GLA_HEADS):
            ks_ = slice(h * GLA_DK, (h + 1) * GLA_DK)
            vs_ = slice(h * GLA_DV, (h + 1) * GLA_DV)
            s_h = st[ks_, :]
            o_ref[:, vs_] = _dotb(qb[:, ks_], s_h, _NN) + _dotb(a[:, ks_], v[:, vs_], _NN)
            st[ks_, :] = s_h * e_col[ks_, :] + _dotb(kd[:, ks_], v[:, vs_], _TN)

    def rowblk(e, c):
        return e * n + ((n - 1 - c) if reverse else c)

    return pl.pallas_call(
        body, out_shape=(jax.ShapeDtypeStruct((T, _HV), F32), jax.ShapeDtypeStruct((T, _HK), F32),
                         jax.ShapeDtypeStruct((T // CHUNK, _HK, GLA_DV), F32)),
        grid=(E, n),
        in_specs=[pl.BlockSpec((CHUNK, _HK), lambda e, c: (rowblk(e, c), C_Q // _HK)),
                  pl.BlockSpec((CHUNK, _HK), lambda e, c: (rowblk(e, c), C_K // _HK)),
                  pl.BlockSpec((CHUNK, _HV), lambda e, c: (rowblk(e, c), C_V // _HV)),
                  pl.BlockSpec((CHUNK, _HK), lambda e, c: (rowblk(e, c), 0)),
                  pl.BlockSpec((1, _HK, GLA_DV), lambda e, c: (rowblk(e, c), 0, 0)),
                  pl.BlockSpec((_HK, _HK), lambda e, c: (0, 0))],
        out_specs=(pl.BlockSpec((CHUNK, _HV), lambda e, c: (rowblk(e, c), 0)),
                   pl.BlockSpec((CHUNK, _HK), lambda e, c: (rowblk(e, c), 0)),
                   pl.BlockSpec((1, _HK, GLA_DV), lambda e, c: (rowblk(e, c), 0, 0))),
        scratch_shapes=[pltpu.VMEM((_HK, GLA_DV), F32)] + [pltpu.VMEM((CHUNK, _HK), F32)] * 3,
        name=name, compiler_params=_cp(("parallel", "arbitrary")))(z, z, z, b_all, ecol, oseg)


def _gla_bwd(z, b_all, ecol, att, states, do, prev, S, reverse, name):
    T = z.shape[0]
    E = T // S
    n = S // CHUNK
    _, _, oseg = _gla_consts(reverse)
    has_prev = prev is not None
    odt = BF16 if has_prev else F32
    last = 0 if reverse else CHUNK - 1

    def body(*refs):
        (q_ref, k_ref, v_ref, b_ref, ec_ref, oseg_ref, att_ref, st_ref, do_ref) = refs[:9]
        refs = refs[9:]
        if has_prev:
            pq_ref, pk_ref, pv_ref = refs[:3]
            refs = refs[3:]
        (dq_ref, dk_ref, dv_ref, db_ref, dst, b_s, q_s, k_s, da_s, dqb_s, dkd_s, dk3_s, dbn_s, dsp_s) = refs

        @pl.when(pl.program_id(1) == 0)
        def _():
            dst[...] = jnp.zeros_like(dst)

        q = q_ref[...] * (GLA_DK ** -0.5)
        k = k_ref[...]
        v = v_ref[...]
        b = b_ref[...]
        bl_row = b_ref[last:last + 1, :]
        eb = jnp.exp(b)
        ekd = jnp.exp(bl_row - b)
        qb = q * eb
        kd = k * ekd
        b_s[...] = b
        q_s[...] = q
        k_s[...] = k
        att = att_ref[...]
        s_all = st_ref[0]
        dsn = dst[...]
        e_col = ec_ref[0]
        do = do_ref[...]
        lane = lax.broadcasted_iota(jnp.int32, (1, _HK), 1) % GLA_DK
        rowi = lax.broadcasted_iota(jnp.int32, (CHUNK, 1), 0)
        keep = (rowi <= lane) if reverse else (rowi >= lane)
        for h in range(GLA_HEADS):
            ks_ = slice(h * GLA_DK, (h + 1) * GLA_DK)
            vs_ = slice(h * GLA_DV, (h + 1) * GLA_DV)
            do_h = do[:, vs_]
            s_h = s_all[ks_, :]
            dsn_h = dsn[ks_, :]
            dqb_s[:, ks_] = _dotb(do_h, s_h, _NT)
            dsp_s[ks_, :] = _dotb(qb[:, ks_], do_h, _TN) + dsn_h * e_col[ks_, :]
            da_s[:, ks_] = _dotb(do_h, v[:, vs_], _NT)
            dv_h = _dotb(att[:, ks_], do_h, _TN) + _dotb(kd[:, ks_], dsn_h, _NN)
            if has_prev:
                dv_h = dv_h + pv_ref[:, vs_]
            dv_ref[:, vs_] = dv_h.astype(odt)
            dkd_s[:, ks_] = _dotb(v[:, vs_], dsn_h, _NT)
        da_s[...] = jnp.where(keep, da_s[...], 0.0)
        dqb = dqb_s[...]
        dkd = dkd_s[...]
        x = dsn * s_all * e_col
        dbl_row = _dotf(jnp.ones((8, GLA_DV), F32), x, _NT)[0:1, :] + jnp.sum(dkd * kd, axis=0, keepdims=True)

        blk0 = (rowi // _SUB) * _SUB

        def cols(jj, carry):
            dq3, db3 = list(carry[:_NSUB]), list(carry[_NSUB:])
            sel = [jnp.where(lane == blk0 + (jj * _COLS + u), da_s[...], 0.0).astype(BF16) for u in range(_COLS)]
            dcols = jnp.dot(jnp.concatenate(sel, axis=0), oseg_ref[...], preferred_element_type=F32)
            for u in range(_COLS):
                jp = jj * _COLS + u
                for s in range(_NSUB):
                    rs_ = slice(s * _SUB, (s + 1) * _SUB)
                    bj = b_s[pl.ds(s * _SUB + jp, 1), :]
                    kj = k_s[pl.ds(s * _SUB + jp, 1), :]
                    tm_ = dcols[u * CHUNK + s * _SUB:u * CHUNK + (s + 1) * _SUB, :] * jnp.exp(jnp.minimum(b_s[rs_, :] - bj, 0.0))
                    dq3[s] = dq3[s] + tm_ * kj
                    gq = tm_ * q_s[rs_, :]
                    dk3_s[pl.ds(s * _SUB + jp, 1), :] = jnp.sum(gq, axis=0, keepdims=True)
                    w = gq * kj
                    dbn_s[pl.ds(s * _SUB + jp, 1), :] = jnp.sum(w, axis=0, keepdims=True)
                    db3[s] = db3[s] + w
            return tuple(dq3) + tuple(db3)

        zero = jnp.zeros((_SUB, _HK), F32)
        acc = lax.fori_loop(0, _SUB // _COLS, cols, (zero,) * (2 * _NSUB))
        dq3 = jnp.concatenate(acc[:_NSUB], axis=0)
        db3 = jnp.concatenate(acc[_NSUB:], axis=0)
        head = lax.broadcasted_iota(jnp.int32, (1, _HK), 1) // GLA_DK
        dq_x, db_x = [], []
        dk_x = jnp.zeros((CHUNK, _HK), F32)
        db_k = jnp.zeros((CHUNK, _HK), F32)
        for s in range(_NSUB):
            if s not in _gla_cross_blocks(reverse):
                dq_x.append(zero)
                db_x.append(zero)
                continue
            r0 = s * _SUB
            qt, eq, kt, ek, nmat = _gla_cross_terms(s, reverse, b_s, q_s, k_s, oseg_ref)
            seen = (lane >= r0 + _SUB) if reverse else (lane < r0)
            dax = jnp.where(seen, da_s[r0:r0 + _SUB, :], 0.0).astype(BF16)
            dqt = jnp.dot(dax, nmat, preferred_element_type=F32)
            full = lax.dot_general(dax, qt.astype(BF16), _TN, preferred_element_type=F32)
            dkt = full[0:CHUNK, :]
            for h in range(1, GLA_HEADS):
                dkt = jnp.where(head == h, full[h * CHUNK:(h + 1) * CHUNK, :], dkt)
            dq_x.append(dqt * eq)
            db_x.append(dqt * qt)
            dk_x = dk_x + dkt * ek
            db_k = db_k + dkt * kt
        dq = (dqb * eb + dq3 + jnp.concatenate(dq_x, axis=0)) * (GLA_DK ** -0.5)
        dk = dkd * ekd + dk3_s[...] + dk_x
        db = dqb * qb - dkd * kd + db3 - dbn_s[...] + jnp.concatenate(db_x, axis=0) - db_k
        db_ref[...] = jnp.where(rowi == last, db + dbl_row, db)
        if has_prev:
            dq = dq + pq_ref[...]
            dk = dk + pk_ref[...]
        dq_ref[...] = dq.astype(odt)
        dk_ref[...] = dk.astype(odt)
        dst[...] = dsp_s[...]

    def rowblk(e, c):
        return e * n + (c if reverse else (n - 1 - c))

    hk = pl.BlockSpec((CHUNK, _HK), lambda e, c: (rowblk(e, c), 0))
    hv = pl.BlockSpec((CHUNK, _HV), lambda e, c: (rowblk(e, c), 0))
    stb = pl.BlockSpec((1, _HK, GLA_DV), lambda e, c: (rowblk(e, c), 0, 0))
    in_specs = [pl.BlockSpec((CHUNK, _HK), lambda e, c: (rowblk(e, c), C_Q // _HK)),
                pl.BlockSpec((CHUNK, _HK), lambda e, c: (rowblk(e, c), C_K // _HK)),
                pl.BlockSpec((CHUNK, _HV), lambda e, c: (rowblk(e, c), C_V // _HV)),
                hk, stb, pl.BlockSpec((_HK, _HK), lambda e, c: (0, 0)), hk, stb, hv]
    args = [z, z, z, b_all, ecol, oseg, att, states, do]
    if has_prev:
        in_specs += [hk, hk, hv]
        args += list(prev)
    return pl.pallas_call(
        body, out_shape=(jax.ShapeDtypeStruct((T, _HK), odt), jax.ShapeDtypeStruct((T, _HK), odt),
                         jax.ShapeDtypeStruct((T, _HV), odt), jax.ShapeDtypeStruct((T, _HK), F32)),
        grid=(E, n), in_specs=in_specs, out_specs=(hk, hk, hv, hk),
        scratch_shapes=[pltpu.VMEM((_HK, GLA_DV), F32)] + [pltpu.VMEM((CHUNK, _HK), F32)] * 8 + [pltpu.VMEM((_HK, GLA_DV), F32)],
        name=name, compiler_params=_cp(("parallel", "arbitrary")))(*args)


def _gla_norm_fwd(of, ob, og, name):
    T = of.shape[0]
    tm = 256

    def body(f_ref, b_ref, g_ref, o_ref):
        for h in range(GLA_HEADS):
            vs_ = slice(h * GLA_DV, (h + 1) * GLA_DV)
            o = f_ref[:, vs_] + b_ref[:, vs_]
            o_ref[:, vs_] = o * lax.rsqrt(jnp.mean(o * o, axis=-1, keepdims=True) + EPS) * g_ref[:, vs_]

    row = pl.BlockSpec((tm, _HV), lambda i: (i, 0))
    vec = pl.BlockSpec((1, _HV), lambda i: (0, 0))
    return pl.pallas_call(body, out_shape=jax.ShapeDtypeStruct((T, _HV), F32), grid=(T // tm,),
                          in_specs=[row, row, vec], out_specs=row, name=name, compiler_params=_cp(("parallel",)))(of, ob, og)


def _gla_norm_bwd(of, ob, og, dpre, name):
    T = of.shape[0]
    tm = 256

    def body(f_ref, b_ref, g_ref, dp_ref, do_ref, dg_ref):
        @pl.when(pl.program_id(0) == 0)
        def _():
            dg_ref[...] = jnp.zeros_like(dg_ref)

        for h in range(GLA_HEADS):
            vs_ = slice(h * GLA_DV, (h + 1) * GLA_DV)
            o = f_ref[:, vs_] + b_ref[:, vs_]
            r = lax.rsqrt(jnp.mean(o * o, axis=-1, keepdims=True) + EPS)
            xh = o * r
            dp = dp_ref[:, vs_]
            dxh = dp * g_ref[:, vs_]
            do_ref[:, vs_] = r * (dxh - xh * jnp.mean(dxh * xh, axis=-1, keepdims=True))
            dg_ref[:, vs_] += jnp.sum(dp * xh, axis=0, keepdims=True)

    row = pl.BlockSpec((tm, _HV), lambda i: (i, 0))
    vec = pl.BlockSpec((1, _HV), lambda i: (0, 0))
    return pl.pallas_call(
        body, out_shape=(jax.ShapeDtypeStruct((T, _HV), F32), jax.ShapeDtypeStruct((1, _HV), F32)), grid=(T // tm,),
        in_specs=[row, row, vec, row], out_specs=(row, vec), name=name, compiler_params=_cp(("arbitrary",)))(of, ob, og, dpre)


_ANY = pl.BlockSpec(memory_space=pl.ANY)


def _coords():
    return lax.axis_index("x"), lax.axis_index("y"), lax.axis_index("c")


def _other_chips(x, y):
    return ((1 - x, y), (x, 1 - y), (1 - x, 1 - y))


def _gather_weights(arrays, chunks, name):
    n = len(arrays)
    pieces = []
    for k in range(max(chunks)):
        for i, a in enumerate(arrays):
            if k < chunks[i]:
                rc = a.shape[1] // chunks[i]
                pieces.append((i, k * rc, rc))
    m = len(pieces)

    def body(*refs):
        srcs, dsts = refs[:n], refs[n:2 * n]
        send_sems, recv_sems, local_sems = refs[2 * n:]
        x, y, c = _coords()
        me = 2 * x + y
        loc = [pltpu.make_async_copy(s, d.at[me], local_sems.at[i]) for i, (s, d) in enumerate(zip(srcs, dsts))]
        for cp in loc:
            cp.start()
        ici = []
        for p, (i, r0, rc) in enumerate(pieces):
            for j, (px, py) in enumerate(_other_chips(x, y)):
                ici.append(pltpu.make_async_remote_copy(
                    src_ref=srcs[i].at[c, pl.ds(r0, rc)], dst_ref=dsts[i].at[me, c, pl.ds(r0, rc)],
                    send_sem=send_sems.at[3 * p + j], recv_sem=recv_sems.at[3 * p + j],
                    device_id=(px, py, c), device_id_type=MESH))
        for cp in ici:
            cp.start()
        fwd = []
        for p, (i, r0, rc) in enumerate(pieces):
            for j, (px, py) in enumerate(_other_chips(x, y)):
                ici[3 * p + j].wait_recv()
                part = dsts[i].at[2 * px + py, c, pl.ds(r0, rc)]
                cp = pltpu.make_async_remote_copy(
                    src_ref=part, dst_ref=part, send_sem=send_sems.at[3 * m + 3 * p + j], recv_sem=recv_sems.at[3 * m + 3 * p + j],
                    device_id=(x, y, 1 - c), device_id_type=MESH)
                cp.start()
                fwd.append(cp)
        for cp in fwd:
            cp.wait_recv()
        for cp in ici + fwd:
            cp.wait_send()
        for cp in loc:
            cp.wait()

    return pl.pallas_call(
        body, out_shape=tuple(jax.ShapeDtypeStruct((4,) + a.shape, a.dtype) for a in arrays),
        in_specs=[_ANY] * n, out_specs=(_ANY,) * n,
        scratch_shapes=[pltpu.SemaphoreType.DMA((6 * m,)), pltpu.SemaphoreType.DMA((6 * m,)), pltpu.SemaphoreType.DMA((n,))],
        name=name)(*arrays)


def _sibling_exchange(layered, whole, name):
    nl, n = len(layered), len(layered) + len(whole)

    def body(*refs):
        srcs, dsts = refs[:n], refs[n:2 * n]
        send_sems, recv_sems = refs[2 * n:]
        x, y, c = _coords()
        rem = [pltpu.make_async_remote_copy(src_ref=(s.at[1 - c] if i < nl else s), dst_ref=d, send_sem=send_sems.at[i],
                                            recv_sem=recv_sems.at[i], device_id=(x, y, 1 - c), device_id_type=MESH)
               for i, (s, d) in enumerate(zip(srcs, dsts))]
        for cp in rem:
            cp.start()
        for cp in rem:
            cp.wait()

    outs = [jax.ShapeDtypeStruct(a.shape[1:], a.dtype) for a in layered] + [jax.ShapeDtypeStruct(a.shape, a.dtype) for a in whole]
    return pl.pallas_call(
        body, out_shape=tuple(outs), in_specs=[_ANY] * n, out_specs=(_ANY,) * n,
        scratch_shapes=[pltpu.SemaphoreType.DMA((n,)), pltpu.SemaphoreType.DMA((n,))], name=name)(*layered, *whole)


def _chip_exchange(scatter, bcast, name):
    ns, n = len(scatter), len(scatter) + len(bcast)

    def body(*refs):
        srcs, dsts = refs[:n], refs[n:2 * n]
        send_sems, recv_sems, local_sems = refs[2 * n:]
        x, y, c = _coords()
        me = 2 * x + y
        loc = [pltpu.make_async_copy((s.at[me] if i < ns else s), d.at[me], local_sems.at[i])
               for i, (s, d) in enumerate(zip(srcs, dsts))]
        for cp in loc:
            cp.start()
        rem = []
        for j, (px, py) in enumerate(_other_chips(x, y)):
            for i, (s, d) in enumerate(zip(srcs, dsts)):
                rem.append(pltpu.make_async_remote_copy(
                    src_ref=(s.at[2 * px + py] if i < ns else s), dst_ref=d.at[me], send_sem=send_sems.at[n * j + i],
                    recv_sem=recv_sems.at[n * j + i], device_id=(px, py, c), device_id_type=MESH))
        for cp in rem:
            cp.start()
        for cp in rem:
            cp.wait()
        for cp in loc:
            cp.wait()

    outs = [jax.ShapeDtypeStruct(a.shape, a.dtype) for a in scatter] + [jax.ShapeDtypeStruct((4,) + a.shape, a.dtype) for a in bcast]
    return pl.pallas_call(
        body, out_shape=tuple(outs), in_specs=[_ANY] * n, out_specs=(_ANY,) * n,
        scratch_shapes=[pltpu.SemaphoreType.DMA((3 * n,)), pltpu.SemaphoreType.DMA((3 * n,)), pltpu.SemaphoreType.DMA((n,))],
        name=name)(*scatter, *bcast)


_EW_BLOCK_BYTES = 2 * 1024 * 1024


def _tile2d(R, C):
    if R % 256 == 0 and 256 * C * 4 <= _EW_BLOCK_BYTES:
        return 256, C
    bc = 256 if C % 256 == 0 else C
    for br in range(R, 0, -1):
        if R % br == 0 and (br % 8 == 0 or br == R) and br * bc * 4 <= _EW_BLOCK_BYTES:
            return br, bc
    return R, bc


def _sum_slots(r, name):
    n, R, C = r.shape
    br, bc = _tile2d(R, C)

    def body(r_ref, o_ref):
        acc = r_ref[0].astype(F32)
        for i in range(1, n):
            acc = acc + r_ref[i].astype(F32)
        o_ref[...] = acc

    return pl.pallas_call(body, out_shape=jax.ShapeDtypeStruct((R, C), F32), grid=(R // br, C // bc),
                          in_specs=[pl.BlockSpec((n, br, bc), lambda i, j: (0, i, j))],
                          out_specs=pl.BlockSpec((br, bc), lambda i, j: (i, j)),
                          name=name, compiler_params=_cp(("parallel", "parallel")))(r)


_SMEM = pl.BlockSpec(memory_space=pltpu.SMEM)


def _add2(a, b, out_dtype, name, pick=None):
    R, C = b.shape
    br, bc = _tile2d(R, C)
    blk = pl.BlockSpec((br, bc), lambda i, j: (i, j))
    if pick is None:
        def body(a_ref, b_ref, o_ref):
            o_ref[...] = (a_ref[...].astype(F32) + b_ref[...].astype(F32)).astype(out_dtype)
        in_specs, args = [blk, blk], (a, b)
    else:
        def body(c_ref, a_ref, b_ref, o_ref):
            av = jnp.where(c_ref[0] == 0, a_ref[0], a_ref[1])
            o_ref[...] = (av.astype(F32) + b_ref[...].astype(F32)).astype(out_dtype)
        in_specs = [_SMEM, pl.BlockSpec((2, br, bc), lambda i, j: (0, i, j)), blk]
        args = (pick.reshape(1).astype(jnp.int32), a, b)
    return pl.pallas_call(body, out_shape=jax.ShapeDtypeStruct((R, C), out_dtype), grid=(R // br, C // bc), in_specs=in_specs,
                          out_specs=blk, name=name, compiler_params=_cp(("parallel", "parallel")))(*args)


def _adamw_math(w, g, m, v):
    m = ADAM_B1 * m + (1.0 - ADAM_B1) * g
    v = ADAM_B2 * v + (1.0 - ADAM_B2) * (g * g)
    m_hat = m / (1.0 - ADAM_B1 ** ADAM_STEP)
    v_hat = v / (1.0 - ADAM_B2 ** ADAM_STEP)
    delta = -ADAM_LR * (m_hat / (jnp.sqrt(v_hat) + ADAM_EPS) + ADAM_WD * w)
    return delta, m, v


def _adamw(w, gs, m, v, name, pick=None):
    R, C = w.shape
    br, bc = _tile2d(R, C)
    blk = pl.BlockSpec((br, bc), lambda i, j: (i, j))
    if pick is None:
        g_specs = [pl.BlockSpec((g.shape[0], br, bc), lambda i, j: (0, i, j)) if g.ndim == 3 else blk for g in gs]
        lead = ()
    else:
        nb = (R // 2) // br
        assert nb * br * 2 == R
        g_specs = [pl.BlockSpec((br, bc), lambda i, j: (i % nb, j))] * 2
        lead = (pick.reshape(1).astype(jnp.int32),)

    def body(*refs):
        if pick is not None:
            c_ref, refs = refs[0], refs[1:]
        w_ref = refs[0]
        g_refs = refs[1:1 + len(gs)]
        m_ref, v_ref, g_out, d_out, m_out, v_out = refs[1 + len(gs):]
        if pick is None:
            g = None
            for gr in g_refs:
                parts = [gr[i] for i in range(gr.shape[0])] if len(gr.shape) == 3 else [gr[...]]
                for p in parts:
                    g = p if g is None else g + p
        else:
            g = jnp.where(pl.program_id(0) // nb == c_ref[0], g_refs[0][...], g_refs[1][...])
        d, mn, vn = _adamw_math(w_ref[...], g, m_ref[...], v_ref[...])
        g_out[...] = g
        d_out[...] = d
        m_out[...] = mn
        v_out[...] = vn

    return pl.pallas_call(
        body, out_shape=tuple(jax.ShapeDtypeStruct((R, C), F32) for _ in range(4)), grid=(R // br, C // bc),
        in_specs=[_SMEM] * len(lead) + [blk] + g_specs + [blk, blk], out_specs=(blk,) * 4, name=name,
        compiler_params=_cp(("parallel", "parallel")))(*lead, w, *gs, m, v)


WEIGHTS = ("norm_g", "w_in", "conv_w", "conv_b", "conv_ln_g", "conv_ln_b", "na_q_g", "na_k_g", "na_rpb", "gla_a2_f",
           "gla_ab_f", "gla_a2_b", "gla_ab_b", "gla_o_g", "pool_w", "pool_scale", "w_out")
_REPL = ("norm_g", "conv_b", "conv_ln_g", "conv_ln_b", "na_q_g", "na_k_g", "na_rpb", "gla_ab_f", "gla_ab_b", "gla_o_g",
         "pool_w", "pool_scale")
_SHARD_SMALL = ("conv_w", "gla_a2_f", "gla_a2_b")
_PACK_ROWS = 8 * 128


def _pack(arrs):
    flat = jnp.concatenate([a.reshape(-1) for a in arrs])
    n = -(-flat.shape[0] // _PACK_ROWS) * _PACK_ROWS
    return jnp.pad(flat, (0, n - flat.shape[0])).reshape(-1, 128)


def _unpack(p, shapes):
    flat = p.reshape(-1)
    out, o = [], 0
    for s in shapes:
        n = int(np.prod(s))
        out.append(flat[o:o + n].reshape(s))
        o += n
    return out


def _to_layout_rows(w):
    pad = jnp.zeros((w.shape[0], NZ - N_IN, w.shape[2]), w.dtype)
    return jnp.concatenate([w[:, :5120], w[:, 5152:6176], w[:, 5120:5152], pad], axis=1)


def _from_layout_rows(w):
    return jnp.concatenate([w[:, :5120], w[:, LR_OFF:LR_OFF + 32], w[:, 5120:LR_OFF]], axis=1)


def _reduce_gradients(p_a, p_b, small_g, ci):
    s_a, s_b, s_small = _sibling_exchange((p_a, p_b), (small_g,), "grad_to_sibling")
    flat = lambda a: a.reshape(a.shape[0], -1, a.shape[-1])
    c_a = _add2(flat(p_a), s_a.reshape(-1, s_a.shape[-1]), BF16, "chip_sum_a", pick=ci).reshape(s_a.shape)
    c_b = _add2(flat(p_b), s_b.reshape(-1, s_b.shape[-1]), BF16, "chip_sum_b", pick=ci).reshape(s_b.shape)
    c_small = _add2(small_g, s_small, F32, "chip_sum_small")
    r_a, r_b, r_small = _chip_exchange((c_a, c_b), (c_small,), "grad_to_owner")
    own_a = _sum_slots(r_a, "sum_a")
    own_b = _sum_slots(r_b, "sum_b")
    sib_a, sib_b = _sibling_exchange((), (own_a, own_b), "reduced_to_sibling")
    return (own_a, sib_a), (own_b, sib_b), r_small


def _layer_fwd(l, x, P, S, target=None):
    n = f"l{l}_"
    h = _rmsnorm_fwd(x, P["norm_g"], n + "rms_fwd")
    z = _matmul(h, P["w_in"], dims="nn", out_dtype=F32, tm=1024, tn=1280, tk=D_MODEL, name=n + "mm_z")
    yc = _conv_fwd(z, P["conv_w32"], P["conv_b"], S, n + "conv_fwd")
    pre_a = _ln_silu_fwd(yc, P["conv_ln_g"], P["conv_ln_b"], n + "ln_fwd")
    pre_b = _na_fwd(z, P["na_q_g"], P["na_k_g"], P["na_bias"], S, n + "na_fwd")
    bf, ecf = _gla_decay_fwd(z, P["a2_f"], P["gla_ab_f"], False, n + "gla_decay_f")
    bb, ecb = _gla_decay_fwd(z, P["a2_b"], P["gla_ab_b"], True, n + "gla_decay_b")
    of, af, sf = _gla_fwd(z, bf, ecf, S, False, n + "gla_fwd_f")
    ob, ab, sb = _gla_fwd(z, bb, ecb, S, True, n + "gla_fwd_b")
    pre_c = _gla_norm_fwd(of, ob, P["gla_o_g"], n + "gla_norm_fwd")
    pre_d = _pool_fwd(z, P["pool_w_bf"], P["pool_scale"], S, n + "pool_fwd")
    pres = (pre_a, pre_b, pre_c, pre_d)
    res = _out_proj_fwd(pres, z, P["w_out"], x, target, n + "out_proj")
    y, out = res[0], (res[1] if target is None else res[1:])
    return out, dict(x=x, h=h, z=z, yc=yc, pres=pres, of=of, af=af, sf=sf, ob=ob, ab=ab, sb=sb, y=y, bf=bf, ecf=ecf, bb=bb, ecb=ecb)


def _layer_bwd(l, dout, dout_bf, sv, P, S):
    n = f"l{l}_"
    z = sv["z"]
    T = z.shape[0]
    d_w_out = _matmul(sv["y"], dout_bf, dims="tn", out_dtype=BF16, tm=1024, tn=2048, tk=512, name=n + "mm_dwout")
    dpa, dpb, dpc, dpd, dga, dgb, dgc, dgd = _out_proj_bwd(dout_bf, P["w_out_t"], sv["pres"], z, n + "out_proj_bwd")
    dyc, d_ln_g, d_ln_b = _ln_silu_bwd(sv["yc"], P["conv_ln_g"], P["conv_ln_b"], dpa, n + "ln_bwd")
    dval, dglu, d_cw, d_cb = _conv_bwd(z, P["conv_w32"], dyc, S, n + "conv_bwd")
    dq, dk, dv, dbias, d_qg, d_kg = _na_bwd(z, P["na_q_g"], P["na_k_g"], P["na_bias"], dpb, S, n + "na_bwd")
    d_rpb = _na_rpb_grad(dbias, n + "na_rpb")
    do, d_og = _gla_norm_bwd(sv["of"], sv["ob"], P["gla_o_g"], dpc, n + "gla_norm_bwd")
    part = _gla_bwd(z, sv["bf"], sv["ecf"], sv["af"], sv["sf"], do, None, S, False, n + "gla_bwd_f")
    dcq, dck, dcv, db_b = _gla_bwd(z, sv["bb"], sv["ecb"], sv["ab"], sv["sb"], do, part[:3], S, True, n + "gla_bwd_b")
    dlr, d_a2f, d_abf, d_a2b, d_abb = _gla_decay_bwd(z, P["a2_f"], P["gla_ab_f"], P["a2_b"], P["gla_ab_b"], part[3], db_b,
                                                     n + "gla_decay_bwd")
    dd, d_pw, d_ps = _pool_bwd(z, P["pool_w_bf"], P["pool_scale"], dpd, S, n + "pool_bwd")
    dz = _concat_cols([dval, dglu, dga, dq, dk, dv, dgb, dcq, dck, dcv, dgc, dd, dgd, dlr], NZ, n + "dz_concat")
    dh = _matmul(dz, P["w_in_t"], dims="nn", out_dtype=F32, tm=1024, tn=1024, tk=3200, name=n + "mm_dh")
    d_w_in = _matmul(dz, sv["h"], dims="tn", out_dtype=BF16, tm=1280, tn=1024, tk=1024, name=n + "mm_dwin")
    dx, dx_bf, d_ng = _rmsnorm_bwd(sv["x"], P["norm_g"], dh, dout, n + "rms_bwd")
    grads = dict(norm_g=d_ng[0], w_in=d_w_in, conv_w=d_cw[:CONV_K], conv_b=d_cb[0], conv_ln_g=d_ln_g[0], conv_ln_b=d_ln_b[0],
                 na_q_g=d_qg.reshape(NA_HEADS, NA_DH), na_k_g=d_kg.reshape(NA_HEADS, NA_DH), na_rpb=d_rpb,
                 gla_a2_f=d_a2f[0:GLA_RANK], gla_ab_f=d_abf[0], gla_a2_b=d_a2b[GLA_RANK:2 * GLA_RANK], gla_ab_b=d_abb[0],
                 gla_o_g=d_og.reshape(GLA_HEADS, GLA_DV), pool_w=d_pw, pool_scale=d_ps[0], w_out=d_w_out)
    return dx, dx_bf, grads


def kernel(x, norm_g, w_in, conv_w, conv_b, conv_ln_g, conv_ln_b, na_q_g, na_k_g, na_rpb, gla_a2_f, gla_ab_f, gla_a2_b, gla_ab_b, gla_o_g, pool_w, pool_scale, w_out, loss_target, m_norm_g, m_w_in, m_conv_w, m_conv_b, m_conv_ln_g, m_conv_ln_b, m_na_q_g, m_na_k_g, m_na_rpb, m_gla_a2_f, m_gla_ab_f, m_gla_a2_b, m_gla_ab_b, m_gla_o_g, m_pool_w, m_pool_scale, m_w_out, v_norm_g, v_w_in, v_conv_w, v_conv_b, v_conv_ln_g, v_conv_ln_b, v_na_q_g, v_na_k_g, v_na_rpb, v_gla_a2_f, v_gla_ab_f, v_gla_a2_b, v_gla_ab_b, v_gla_o_g, v_pool_w, v_pool_scale, v_w_out):
    W = dict(norm_g=norm_g, w_in=w_in, conv_w=conv_w, conv_b=conv_b, conv_ln_g=conv_ln_g, conv_ln_b=conv_ln_b, na_q_g=na_q_g,
             na_k_g=na_k_g, na_rpb=na_rpb, gla_a2_f=gla_a2_f, gla_ab_f=gla_ab_f, gla_a2_b=gla_a2_b, gla_ab_b=gla_ab_b,
             gla_o_g=gla_o_g, pool_w=pool_w, pool_scale=pool_scale, w_out=w_out)
    M = dict(norm_g=m_norm_g, w_in=m_w_in, conv_w=m_conv_w, conv_b=m_conv_b, conv_ln_g=m_conv_ln_g, conv_ln_b=m_conv_ln_b,
             na_q_g=m_na_q_g, na_k_g=m_na_k_g, na_rpb=m_na_rpb, gla_a2_f=m_gla_a2_f, gla_ab_f=m_gla_ab_f, gla_a2_b=m_gla_a2_b,
             gla_ab_b=m_gla_ab_b, gla_o_g=m_gla_o_g, pool_w=m_pool_w, pool_scale=m_pool_scale, w_out=m_w_out)
    V = dict(norm_g=v_norm_g, w_in=v_w_in, conv_w=v_conv_w, conv_b=v_conv_b, conv_ln_g=v_conv_ln_g, conv_ln_b=v_conv_ln_b,
             na_q_g=v_na_q_g, na_k_g=v_na_k_g, na_rpb=v_na_rpb, gla_a2_f=v_gla_a2_f, gla_ab_f=v_gla_ab_f, gla_a2_b=v_gla_a2_b,
             gla_ab_b=v_gla_ab_b, gla_o_g=v_gla_o_g, pool_w=v_pool_w, pool_scale=v_pool_scale, w_out=v_w_out)
    E, S, D = x.shape
    T = E * S
    L = DEPTH
    xi, yi, ci = _coords()
    chip = 2 * xi + yi
    cw_sh, a2_sh = conv_w.shape[-1], gla_a2_f.shape[-1]

    small_sh = jnp.concatenate([
        jnp.pad(conv_w, ((0, 0), (0, 1), (0, 0))),
        jnp.pad(gla_a2_f, ((0, 0), (0, 0), (0, 128 - a2_sh))),
        jnp.pad(gla_a2_b, ((0, 0), (0, 0), (0, 128 - a2_sh)))], axis=1)
    w_in_tr, m_w_in_tr, v_w_in_tr = (jnp.transpose(a, (0, 2, 1)) for a in (w_in, m_w_in, v_w_in))
    g_win, g_wout, g_small = _gather_weights((w_in_tr.astype(BF16), w_out.astype(BF16), small_sh), (1, 2, 1), "gather_weights")
    w_in_t_full = _to_layout_rows(jnp.transpose(g_win, (1, 0, 2, 3)).reshape(L, N_IN, D))
    w_in_full = jnp.transpose(w_in_t_full, (0, 2, 1))
    w_out_full = jnp.transpose(g_wout, (1, 0, 2, 3)).reshape(L, D, D)
    conv_w_full = jnp.transpose(g_small[:, :, 0:32, :], (1, 2, 0, 3)).reshape(L, 32, 4 * cw_sh)
    a2f_full = jnp.transpose(g_small[:, :, 32:48, :a2_sh], (1, 2, 0, 3)).reshape(L, GLA_RANK, 4 * a2_sh)
    a2b_full = jnp.transpose(g_small[:, :, 48:64, :a2_sh], (1, 2, 0, 3)).reshape(L, GLA_RANK, 4 * a2_sh)

    params = []
    for l in range(L):
        params.append(dict(
            norm_g=norm_g[l][None], w_in=w_in_full[l], w_out=w_out_full[l], w_in_t=w_in_t_full[l], w_out_t=w_out_full[l].T,
            conv_w32=conv_w_full[l], conv_b=conv_b[l][None],
            conv_ln_g=conv_ln_g[l][None], conv_ln_b=conv_ln_b[l][None], na_q_g=na_q_g[l].reshape(1, GROUP_W),
            na_k_g=na_k_g[l].reshape(1, GROUP_W), na_bias=_na_bias(na_rpb[l], f"l{l}_na_bias"),
            a2_f=jnp.zeros((128, _HK), F32).at[0:GLA_RANK].set(a2f_full[l]),
            a2_b=jnp.zeros((128, _HK), F32).at[GLA_RANK:2 * GLA_RANK].set(a2b_full[l]),
            gla_ab_f=gla_ab_f[l][None], gla_ab_b=gla_ab_b[l][None], gla_o_g=gla_o_g[l].reshape(1, GROUP_W),
            pool_w_bf=pool_w[l].astype(BF16), pool_scale=pool_scale[l][None]))

    act = x.reshape(T, D)
    saved = []
    for l in range(L):
        act, sv = _layer_fwd(l, act, params[l], S, loss_target.reshape(T, D) if l == L - 1 else None)
        saved.append(sv)
    dact, dact_bf, loss_loc = act
    loss = lax.psum(loss_loc[0, 0], ("x", "y", "c"))
    grads = [None] * L
    for l in reversed(range(L)):
        dact, dact_bf, grads[l] = _layer_bwd(l, dact, dact_bf, saved[l], params[l], S)
    grad_x = dact.reshape(E, S, D)
    G = {k: jnp.stack([grads[l][k] for l in range(L)]) for k in WEIGHTS}

    cols_in, cols_out = N_IN // 4, D
    p_win = _from_layout_rows(G["w_in"]).reshape(L, 4, cols_in, D)
    p_wout = G["w_out"].reshape(L, 4, D // 4, D)
    small_names = _REPL + _SHARD_SMALL
    small_g = _pack([G[k] for k in small_names])
    g_in, g_out, r_small = _reduce_gradients(p_win, p_wout, small_g, ci)

    rows_in, rows_out = L * cols_in, L * (D // 4)
    res = {}
    res["w_in"] = [jnp.transpose(a.reshape(L, cols_in, D), (0, 2, 1)) for a in _adamw(
        w_in_tr.reshape(rows_in, D), g_in, m_w_in_tr.reshape(rows_in, D), v_w_in_tr.reshape(rows_in, D), "adamw_w_in", pick=ci)]
    res["w_out"] = [a.reshape(L, D // 4, D) for a in _adamw(
        w_out.reshape(rows_out, cols_out), g_out, m_w_out.reshape(rows_out, cols_out),
        v_w_out.reshape(rows_out, cols_out), "adamw_w_out", pick=ci)]
    zeros_sh = [jnp.zeros(G[k].shape, F32) for k in _SHARD_SMALL]
    pk = lambda dct: _pack([dct[k] for k in _REPL] + zeros_sh)
    small_res = _adamw(pk(W), (r_small,), pk(M), pk(V), "adamw_small")
    shapes = [G[k].shape for k in small_names]
    unp = [_unpack(a, shapes) for a in small_res]
    for i, k in enumerate(_REPL):
        res[k] = [u[i] for u in unp]
    g_sh = []
    for i, k in enumerate(_SHARD_SMALL):
        gfull = unp[0][len(_REPL) + i]
        wdt = W[k].shape[-1]
        g_sh.append(lax.dynamic_slice_in_dim(gfull, chip * wdt, wdt, axis=2))
    g_sh_p = _pack(g_sh)
    sh_res = _adamw(_pack([W[k] for k in _SHARD_SMALL]), (g_sh_p,), _pack([M[k] for k in _SHARD_SMALL]),
                    _pack([V[k] for k in _SHARD_SMALL]), "adamw_shard_small")
    shapes2 = [W[k].shape for k in _SHARD_SMALL]
    unp2 = [_unpack(a, shapes2) for a in sh_res]
    for i, k in enumerate(_SHARD_SMALL):
        res[k] = [u[i] for u in unp2]

    outs = [loss, grad_x]
    for j in range(4):
        outs += [res[k][j] for k in WEIGHTS]
    return tuple(outs)
```

```python
import functools

import numpy as np
import jax
import jax.numpy as jnp
from jax import lax
from jax.experimental import pallas as pl
from jax.experimental.pallas import tpu as pltpu

F32 = jnp.float32
BF16 = jnp.bfloat16
HI = lax.Precision.HIGHEST
MESH = pl.DeviceIdType.MESH

EPS = 1e-6
D_MODEL = 2048
GROUP_W = 512
SEQ = 2048
DEPTH = 2
N_IN = 6176
GRID_W = 64
CONV_K = 31
NA_HEADS = 8
NA_DH = 64
NA_ROWS = 8
NA_COLS = 16
GLA_HEADS = 4
GLA_DK = 64
GLA_DV = 128
GLA_RANK = 16
GLA_TAU = 16.0
CHUNK = 64
POOL_WINDOWS = (2, 4, 8, 16)
ADAM_LR, ADAM_B1, ADAM_B2, ADAM_EPS, ADAM_WD, ADAM_STEP = 0.001, 0.9, 0.999, 1e-08, 0.01, 10

A_VAL, A_GLU, A_GATE = 0, 512, 1024
B_Q, B_K, B_V, B_GATE = 1536, 2048, 2560, 3072
C_Q, C_K, C_V, C_GATE = 3584, 3840, 4096, 4608
D_VAL, D_GATE = 5120, 5632
LR_OFF = 6144
NZ = 6400
NEG = -1e30
VMEM_LIMIT = 56 * 1024 * 1024


def _cp(sem=None):
    return pltpu.CompilerParams(dimension_semantics=sem, vmem_limit_bytes=VMEM_LIMIT)


def _sigmoid(x):
    return 1.0 / (1.0 + jnp.exp(-x))


def _silu(x):
    return x * _sigmoid(x)


def _dsilu(x):
    s = _sigmoid(x)
    return s * (1.0 + x * (1.0 - s))


def _matmul(a, b, *, dims, out_dtype, tm, tn, tk, name, res=None):
    if dims == "nn":
        (M, K), N = a.shape, b.shape[1]
    elif dims == "nt":
        (M, K), N = a.shape, b.shape[0]
    else:
        (K, M), N = a.shape, b.shape[1]
    tm, tn, tk = min(tm, M), min(tn, N), min(tk, K)
    nk = K // tk
    assert M % tm == 0 and N % tn == 0 and K % tk == 0, (M, N, K, tm, tn, tk)
    dn = {"nn": (((1,), (0,)), ((), ())), "nt": (((1,), (1,)), ((), ())), "tn": (((0,), (0,)), ((), ()))}[dims]
    if dims == "tn":
        a_spec = pl.BlockSpec((tk, tm), lambda i, j, k: (k, i))
    else:
        a_spec = pl.BlockSpec((tm, tk), lambda i, j, k: (i, k))
    if dims == "nt":
        b_spec = pl.BlockSpec((tn, tk), lambda i, j, k: (j, k))
    else:
        b_spec = pl.BlockSpec((tk, tn), lambda i, j, k: (k, j))
    o_spec = pl.BlockSpec((tm, tn), lambda i, j, k: (i, j))
    has_res = res is not None

    def body(*refs):
        if has_res:
            a_ref, b_ref, r_ref, o_ref, acc = refs
        else:
            a_ref, b_ref, o_ref, acc = refs
        k = pl.program_id(2)

        @pl.when(k == 0)
        def _():
            acc[...] = jnp.zeros_like(acc)

        acc[...] += lax.dot_general(a_ref[...], b_ref[...], dn, preferred_element_type=F32)

        @pl.when(k == nk - 1)
        def _():
            r = acc[...]
            if has_res:
                r = r + r_ref[...]
            o_ref[...] = r.astype(o_ref.dtype)

    in_specs = [a_spec, b_spec] + ([o_spec] if has_res else [])
    args = (a, b) + ((res,) if has_res else ())
    return pl.pallas_call(
        body, out_shape=jax.ShapeDtypeStruct((M, N), out_dtype), grid=(M // tm, N // tn, nk),
        in_specs=in_specs, out_specs=o_spec, scratch_shapes=[pltpu.VMEM((tm, tn), F32)],
        name=name, compiler_params=_cp(("parallel", "parallel", "arbitrary")))(*args)


def _concat_cols(pieces, width, name):
    T = pieces[0].shape[0]
    tm = min(512, T)
    dt = pieces[0].dtype
    offs = np.cumsum([0] + [p.shape[1] for p in pieces])

    def body(*refs):
        o_ref = refs[-1]
        for p_ref, a, b in zip(refs[:-1], offs[:-1], offs[1:]):
            o_ref[:, a:b] = p_ref[...]
        if offs[-1] < width:
            o_ref[:, offs[-1]:width] = jnp.zeros((tm, width - offs[-1]), dt)

    return pl.pallas_call(
        body, out_shape=jax.ShapeDtypeStruct((T, width), dt), grid=(T // tm,),
        in_specs=[pl.BlockSpec((tm, p.shape[1]), lambda i: (i, 0)) for p in pieces],
        out_specs=pl.BlockSpec((tm, width), lambda i: (i, 0)), name=name, compiler_params=_cp(("parallel",)))(*pieces)


def _rmsnorm_fwd(x, g, name):
    T, D = x.shape
    tm = 256

    def body(x_ref, g_ref, h_ref):
        xv = x_ref[...]
        r = lax.rsqrt(jnp.mean(xv * xv, axis=-1, keepdims=True) + EPS)
        h_ref[...] = (xv * r * g_ref[...]).astype(h_ref.dtype)

    return pl.pallas_call(
        body, out_shape=jax.ShapeDtypeStruct((T, D), BF16), grid=(T // tm,),
        in_specs=[pl.BlockSpec((tm, D), lambda i: (i, 0)), pl.BlockSpec((1, D), lambda i: (0, 0))],
        out_specs=pl.BlockSpec((tm, D), lambda i: (i, 0)), name=name, compiler_params=_cp(("parallel",)))(x, g)


def _rmsnorm_bwd(x, g, dh, dres, name):
    T, D = x.shape
    tm = 256

    def body(x_ref, g_ref, dh_ref, dres_ref, dx_ref, dxb_ref, dg_ref):
        xv = x_ref[...]
        r = lax.rsqrt(jnp.mean(xv * xv, axis=-1, keepdims=True) + EPS)
        xh = xv * r
        dh_v = dh_ref[...]
        dxh = dh_v * g_ref[...]
        dx = r * (dxh - xh * jnp.mean(dxh * xh, axis=-1, keepdims=True)) + dres_ref[...]
        dx_ref[...] = dx
        dxb_ref[...] = dx.astype(BF16)

        @pl.when(pl.program_id(0) == 0)
        def _():
            dg_ref[...] = jnp.zeros_like(dg_ref)

        dg_ref[...] += jnp.sum(dh_v * xh, axis=0, keepdims=True)

    row = pl.BlockSpec((tm, D), lambda i: (i, 0))
    vec = pl.BlockSpec((1, D), lambda i: (0, 0))
    return pl.pallas_call(
        body, out_shape=(jax.ShapeDtypeStruct((T, D), F32), jax.ShapeDtypeStruct((T, D), BF16), jax.ShapeDtypeStruct((1, D), F32)),
        grid=(T // tm,), in_specs=[row, vec, row, row], out_specs=(row, row, vec), name=name,
        compiler_params=_cp(("arbitrary",)))(x, g, dh, dres)


_GATE_COLS = (A_GATE // GROUP_W, B_GATE // GROUP_W, C_GATE // GROUP_W, D_GATE // GROUP_W)
_OP_TM = 256


def _out_proj_fwd(pres, z, w_out, x, target, name):
    T, D = x.shape
    tm = min(_OP_TM, T)
    with_loss = target is not None

    def body(*refs):
        pa, pb, pc, pd, ga, gb, gc, gd, w_ref, x_ref = refs[:10]
        refs = refs[10:]
        if with_loss:
            t_ref, y_ref, d_ref, db_ref, l_ref = refs
        else:
            y_ref, o_ref = refs
        for n_, (p, g) in enumerate(((pa, ga), (pb, gb), (pc, gc), (pd, gd))):
            y_ref[:, n_ * GROUP_W:(n_ + 1) * GROUP_W] = (p[...] * _silu(g[...])).astype(BF16)
        out = jnp.dot(y_ref[...], w_ref[...], preferred_element_type=F32) + x_ref[...]
        if with_loss:
            e = out - t_ref[...]
            d = e * (1.0 / D)
            d_ref[...] = d
            db_ref[...] = d.astype(BF16)

            @pl.when(pl.program_id(0) == 0)
            def _():
                l_ref[...] = jnp.zeros_like(l_ref)

            l_ref[...] += jnp.sum(jnp.sum(e * e, axis=-1, keepdims=True) * (0.5 / D), axis=0, keepdims=True)
        else:
            o_ref[...] = out

    pre_spec = pl.BlockSpec((tm, GROUP_W), lambda i: (i, 0))
    gate_specs = [pl.BlockSpec((tm, GROUP_W), functools.partial(lambda i, c: (i, c), c=c)) for c in _GATE_COLS]
    row = pl.BlockSpec((tm, D), lambda i: (i, 0))
    w_spec = pl.BlockSpec((4 * GROUP_W, D), lambda i: (0, 0))
    in_specs = [pre_spec] * 4 + gate_specs + [w_spec, row]
    args = list(pres) + [z, z, z, z, w_out, x]
    if with_loss:
        in_specs.append(row)
        args.append(target)
        out_shape = (jax.ShapeDtypeStruct((T, D), BF16), jax.ShapeDtypeStruct((T, D), F32), jax.ShapeDtypeStruct((T, D), BF16),
                     jax.ShapeDtypeStruct((1, 1), F32))
        out_specs = (row, row, row, pl.BlockSpec((1, 1), lambda i: (0, 0)))
    else:
        out_shape = (jax.ShapeDtypeStruct((T, D), BF16), jax.ShapeDtypeStruct((T, D), F32))
        out_specs = (row, row)
    return pl.pallas_call(body, out_shape=out_shape, grid=(T // tm,), in_specs=in_specs, out_specs=out_specs, name=name,
                          compiler_params=_cp(("arbitrary",)))(*args)


def _out_proj_bwd(dout_bf, w_out_t, pres, z, name):
    T, D = dout_bf.shape
    tm = min(_OP_TM, T)

    def body(do_ref, w_ref, pa, pb, pc, pd, ga, gb, gc, gd, dpa, dpb, dpc, dpd, dga, dgb, dgc, dgd):
        dy = jnp.dot(do_ref[...], w_ref[...], preferred_element_type=F32)
        for n_, (p, g, dp, dg) in enumerate(((pa, ga, dpa, dga), (pb, gb, dpb, dgb), (pc, gc, dpc, dgc), (pd, gd, dpd, dgd))):
            d = dy[:, n_ * GROUP_W:(n_ + 1) * GROUP_W]
            gv = g[...]
            dp[...] = d * _silu(gv)
            dg[...] = (d * p[...] * _dsilu(gv)).astype(BF16)

    pre_spec = pl.BlockSpec((tm, GROUP_W), lambda i: (i, 0))
    gate_specs = [pl.BlockSpec((tm, GROUP_W), functools.partial(lambda i, c: (i, c), c=c)) for c in _GATE_COLS]
    outs = tuple([jax.ShapeDtypeStruct((T, GROUP_W), F32)] * 4 + [jax.ShapeDtypeStruct((T, GROUP_W), BF16)] * 4)
    return pl.pallas_call(
        body, out_shape=outs, grid=(T // tm,),
        in_specs=[pl.BlockSpec((tm, D), lambda i: (i, 0)), pl.BlockSpec((D, 4 * GROUP_W), lambda i: (0, 0))] + [pre_spec] * 4 + gate_specs,
        out_specs=tuple([pre_spec] * 8), name=name, compiler_params=_cp(("parallel",)))(dout_bf, w_out_t, *pres, z, z, z, z)


_PAD = 16
_RC = 256


def _conv_fwd(z, conv_w32, conv_b, S, name):
    T = z.shape[0]
    E = T // S
    LW = 128

    def body(val_ref, glu_ref, w_ref, b_ref, y_ref, upad):
        upad[0:_PAD, :] = jnp.zeros((_PAD, LW), F32)
        upad[_PAD + S:_PAD + S + _PAD, :] = jnp.zeros((_PAD, LW), F32)
        upad[_PAD:_PAD + S, :] = val_ref[...] * _sigmoid(glu_ref[...])
        for r in range(S // _RC):
            acc = jnp.broadcast_to(b_ref[...], (_RC, LW))
            for k in range(CONV_K):
                st = r * _RC + k + 1
                acc = acc + upad[st:st + _RC, :] * w_ref[k:k + 1, :]
            y_ref[r * _RC:(r + 1) * _RC, :] = acc

    return pl.pallas_call(
        body, out_shape=jax.ShapeDtypeStruct((T, GROUP_W), F32), grid=(E, GROUP_W // LW),
        in_specs=[pl.BlockSpec((S, LW), lambda e, j: (e, A_VAL // LW + j)),
                  pl.BlockSpec((S, LW), lambda e, j: (e, A_GLU // LW + j)),
                  pl.BlockSpec((32, LW), lambda e, j: (0, j)),
                  pl.BlockSpec((1, LW), lambda e, j: (0, j))],
        out_specs=pl.BlockSpec((S, LW), lambda e, j: (e, j)),
        scratch_shapes=[pltpu.VMEM((S + 2 * _PAD, LW), F32)],
        name=name, compiler_params=_cp(("parallel", "parallel")))(z, z, conv_w32, conv_b)


def _conv_bwd(z, conv_w32, dyc, S, name):
    T = z.shape[0]
    E = T // S
    LW = 128

    def body(val_ref, glu_ref, w_ref, dy_ref, dval_ref, dglu_ref, dw_ref, db_ref, upad, dpad):
        e = pl.program_id(1)
        zeros = jnp.zeros((_PAD, LW), F32)
        upad[0:_PAD, :] = zeros
        upad[_PAD + S:_PAD + S + _PAD, :] = zeros
        dpad[0:_PAD, :] = zeros
        dpad[_PAD + S:_PAD + S + _PAD, :] = zeros
        upad[_PAD:_PAD + S, :] = val_ref[...] * _sigmoid(glu_ref[...])
        dpad[_PAD:_PAD + S, :] = dy_ref[...]

        @pl.when(e == 0)
        def _():
            dw_ref[...] = jnp.zeros_like(dw_ref)
            db_ref[...] = jnp.zeros_like(db_ref)

        db_ref[...] += jnp.sum(dy_ref[...], axis=0, keepdims=True)
        for r in range(S // _RC):
            dyr = dy_ref[r * _RC:(r + 1) * _RC, :]
            du = jnp.zeros((_RC, LW), F32)
            for k in range(CONV_K):
                st = r * _RC + k + 1
                dw_ref[k:k + 1, :] += jnp.sum(dyr * upad[st:st + _RC, :], axis=0, keepdims=True)
                sd = r * _RC + (CONV_K - 1 - k) + 1
                du = du + dpad[sd:sd + _RC, :] * w_ref[k:k + 1, :]
            sl = slice(r * _RC, (r + 1) * _RC)
            val = val_ref[sl, :]
            sg = _sigmoid(glu_ref[sl, :])
            dval_ref[sl, :] = (du * sg).astype(BF16)
            dglu_ref[sl, :] = (du * val * sg * (1.0 - sg)).astype(BF16)

    blk = pl.BlockSpec((S, LW), lambda j, e: (e, j))
    return pl.pallas_call(
        body, out_shape=(jax.ShapeDtypeStruct((T, GROUP_W), BF16), jax.ShapeDtypeStruct((T, GROUP_W), BF16),
                         jax.ShapeDtypeStruct((32, GROUP_W), F32), jax.ShapeDtypeStruct((1, GROUP_W), F32)),
        grid=(GROUP_W // LW, E),
        in_specs=[pl.BlockSpec((S, LW), lambda j, e: (e, A_VAL // LW + j)),
                  pl.BlockSpec((S, LW), lambda j, e: (e, A_GLU // LW + j)),
                  pl.BlockSpec((32, LW), lambda j, e: (0, j)), blk],
        out_specs=(blk, blk, pl.BlockSpec((32, LW), lambda j, e: (0, j)), pl.BlockSpec((1, LW), lambda j, e: (0, j))),
        scratch_shapes=[pltpu.VMEM((S + 2 * _PAD, LW), F32), pltpu.VMEM((S + 2 * _PAD, LW), F32)],
        name=name, compiler_params=_cp(("parallel", "arbitrary")))(z, z, conv_w32, dyc)


def _ln_silu_fwd(yc, g, b, name):
    T, C = yc.shape
    tm = 256

    def body(y_ref, g_ref, b_ref, o_ref):
        y = y_ref[...]
        mu = jnp.mean(y, axis=-1, keepdims=True)
        yc_ = y - mu
        r = lax.rsqrt(jnp.mean(yc_ * yc_, axis=-1, keepdims=True) + EPS)
        o_ref[...] = _silu(yc_ * r * g_ref[...] + b_ref[...])

    row = pl.BlockSpec((tm, C), lambda i: (i, 0))
    vec = pl.BlockSpec((1, C), lambda i: (0, 0))
    return pl.pallas_call(body, out_shape=jax.ShapeDtypeStruct((T, C), F32), grid=(T // tm,),
                          in_specs=[row, vec, vec], out_specs=row, name=name, compiler_params=_cp(("parallel",)))(yc, g, b)


def _ln_silu_bwd(yc, g, b, dpre, name):
    T, C = yc.shape
    tm = 256

    def body(y_ref, g_ref, b_ref, dp_ref, dy_ref, dg_ref, db_ref):
        y = y_ref[...]
        mu = jnp.mean(y, axis=-1, keepdims=True)
        yc_ = y - mu
        r = lax.rsqrt(jnp.mean(yc_ * yc_, axis=-1, keepdims=True) + EPS)
        xh = yc_ * r
        gv = g_ref[...]
        dln = dp_ref[...] * _dsilu(xh * gv + b_ref[...])
        dxh = dln * gv
        dy_ref[...] = r * (dxh - jnp.mean(dxh, axis=-1, keepdims=True) - xh * jnp.mean(dxh * xh, axis=-1, keepdims=True))

        @pl.when(pl.program_id(0) == 0)
        def _():
            dg_ref[...] = jnp.zeros_like(dg_ref)
            db_ref[...] = jnp.zeros_like(db_ref)

        dg_ref[...] += jnp.sum(dln * xh, axis=0, keepdims=True)
        db_ref[...] += jnp.sum(dln, axis=0, keepdims=True)

    row = pl.BlockSpec((tm, C), lambda i: (i, 0))
    vec = pl.BlockSpec((1, C), lambda i: (0, 0))
    return pl.pallas_call(
        body, out_shape=(jax.ShapeDtypeStruct((T, C), F32), jax.ShapeDtypeStruct((1, C), F32), jax.ShapeDtypeStruct((1, C), F32)),
        grid=(T // tm,), in_specs=[row, vec, vec, row], out_specs=(row, vec, vec), name=name,
        compiler_params=_cp(("arbitrary",)))(yc, g, b, dpre)


def _pool_counts(S, w, rows0, n):
    t = (lax.broadcasted_iota(jnp.int32, (n, 1), 0) + rows0)
    lo = jnp.maximum(t - w // 2, 0)
    hi = jnp.minimum(t + w // 2, S)
    return (hi - lo).astype(F32)


def _pool_fwd(z, pool_w, pool_scale, S, name):
    T = z.shape[0]
    E = T // S
    CG = 128

    def body(u_ref, w_ref, s_ref, o_ref, upad, dif):
        zeros = jnp.zeros((_PAD, GROUP_W), F32)
        upad[0:_PAD, :] = zeros
        upad[_PAD + S:_PAD + S + _PAD, :] = zeros
        upad[_PAD:_PAD + S, :] = u_ref[...]
        for gi, w in enumerate(POOL_WINDOWS):
            ls = slice(gi * CG, (gi + 1) * CG)
            for r in range(S // _RC):
                acc = jnp.zeros((_RC, CG), F32)
                for j in range(-(w // 2), w // 2):
                    st = _PAD + r * _RC + j
                    acc = acc + upad[st:st + _RC, ls]
                cnt = _pool_counts(S, w, r * _RC, _RC)
                dif[r * _RC:(r + 1) * _RC, :] = (acc / cnt - u_ref[r * _RC:(r + 1) * _RC, ls]).astype(BF16)
            yp = jnp.dot(dif[...], w_ref[gi], preferred_element_type=F32)
            o_ref[:, ls] = yp * s_ref[:, ls]

    return pl.pallas_call(
        body, out_shape=jax.ShapeDtypeStruct((T, GROUP_W), F32), grid=(E,),
        in_specs=[pl.BlockSpec((S, GROUP_W), lambda e: (e, D_VAL // GROUP_W)),
                  pl.BlockSpec((4, CG, CG), lambda e: (0, 0, 0)),
                  pl.BlockSpec((1, GROUP_W), lambda e: (0, 0))],
        out_specs=pl.BlockSpec((S, GROUP_W), lambda e: (e, 0)),
        scratch_shapes=[pltpu.VMEM((S + 2 * _PAD, GROUP_W), F32), pltpu.VMEM((S, CG), BF16)],
        name=name, compiler_params=_cp(("parallel",)))(z, pool_w, pool_scale)


def _pool_bwd(z, pool_w, pool_scale, dpre, S, name):
    T = z.shape[0]
    E = T // S
    CG = 128

    def body(u_ref, w_ref, s_ref, dp_ref, du_ref, dw_ref, ds_ref, upad, dif, qpad):
        zeros = jnp.zeros((_PAD, GROUP_W), F32)
        upad[0:_PAD, :] = zeros
        upad[_PAD + S:_PAD + S + _PAD, :] = zeros
        upad[_PAD:_PAD + S, :] = u_ref[...]
        zc = jnp.zeros((_PAD, CG), F32)
        qpad[0:_PAD, :] = zc
        qpad[_PAD + S:_PAD + S + _PAD, :] = zc

        @pl.when(pl.program_id(0) == 0)
        def _():
            dw_ref[...] = jnp.zeros_like(dw_ref)
            ds_ref[...] = jnp.zeros_like(ds_ref)

        for gi, w in enumerate(POOL_WINDOWS):
            ls = slice(gi * CG, (gi + 1) * CG)
            for r in range(S // _RC):
                acc = jnp.zeros((_RC, CG), F32)
                for j in range(-(w // 2), w // 2):
                    st = _PAD + r * _RC + j
                    acc = acc + upad[st:st + _RC, ls]
                cnt = _pool_counts(S, w, r * _RC, _RC)
                dif[r * _RC:(r + 1) * _RC, :] = (acc / cnt - u_ref[r * _RC:(r + 1) * _RC, ls]).astype(BF16)
            dp = dp_ref[:, ls]
            yp = jnp.dot(dif[...], w_ref[gi], preferred_element_type=F32)
            ds_ref[:, ls] += jnp.sum(dp * yp, axis=0, keepdims=True)
            dys = (dp * s_ref[:, ls]).astype(BF16)
            dw_ref[gi] += lax.dot_general(dif[...], dys, (((0,), (0,)), ((), ())), preferred_element_type=F32)
            dm = lax.dot_general(dys, w_ref[gi], (((1,), (1,)), ((), ())), preferred_element_type=F32)
            for r in range(S // _RC):
                cnt = _pool_counts(S, w, r * _RC, _RC)
                qpad[_PAD + r * _RC:_PAD + (r + 1) * _RC, :] = dm[r * _RC:(r + 1) * _RC, :] / cnt
            for r in range(S // _RC):
                acc = -dm[r * _RC:(r + 1) * _RC, :]
                for j in range(-(w // 2) + 1, w // 2 + 1):
                    st = _PAD + r * _RC + j
                    acc = acc + qpad[st:st + _RC, :]
                du_ref[r * _RC:(r + 1) * _RC, ls] = acc.astype(BF16)

    return pl.pallas_call(
        body, out_shape=(jax.ShapeDtypeStruct((T, GROUP_W), BF16), jax.ShapeDtypeStruct((4, CG, CG), F32),
                         jax.ShapeDtypeStruct((1, GROUP_W), F32)), grid=(E,),
        in_specs=[pl.BlockSpec((S, GROUP_W), lambda e: (e, D_VAL // GROUP_W)),
                  pl.BlockSpec((4, CG, CG), lambda e: (0, 0, 0)),
                  pl.BlockSpec((1, GROUP_W), lambda e: (0, 0)),
                  pl.BlockSpec((S, GROUP_W), lambda e: (e, 0))],
        out_specs=(pl.BlockSpec((S, GROUP_W), lambda e: (e, 0)), pl.BlockSpec((4, CG, CG), lambda e: (0, 0, 0)),
                   pl.BlockSpec((1, GROUP_W), lambda e: (0, 0))),
        scratch_shapes=[pltpu.VMEM((S + 2 * _PAD, GROUP_W), F32), pltpu.VMEM((S, CG), BF16),
                        pltpu.VMEM((S + 2 * _PAD, CG), F32)],
        name=name, compiler_params=_cp(("arbitrary",)))(z, pool_w, pool_scale, dpre)


def _na_tables():
    d = np.arange(NA_ROWS)[:, None]
    kr = np.arange(NA_ROWS)[None, :]
    ro = kr - d + (NA_ROWS - 1)
    qc = np.arange(GRID_W)[:, None]
    kc = np.arange(GRID_W)[None, :]
    cs = np.clip(qc - NA_COLS // 2, 0, GRID_W - NA_COLS)
    valid = (kc >= cs) & (kc < cs + NA_COLS)
    co = np.clip(kc - qc + (NA_COLS - 1), 0, 2 * NA_COLS - 2)
    return ro, co, valid


def _na_onehots():
    ro, co, valid = _na_tables()
    e_np = np.zeros((GRID_W, GRID_W, 128), np.float32)
    qi, ki = np.nonzero(valid)
    e_np[qi, ki, co[qi, ki]] = 1.0
    a_np = np.zeros((16, NA_ROWS * NA_ROWS), np.float32)
    a_np[ro.reshape(-1), np.arange(NA_ROWS * NA_ROWS)] = 1.0
    mask = np.where(valid, 0.0, NEG).astype(np.float32).reshape(1, GRID_W * GRID_W)
    return e_np.reshape(GRID_W * GRID_W, 128), a_np, mask


def _na_bias(rpb, name):
    e_np, _, mask = _na_onehots()
    H = NA_HEADS
    rp = jnp.pad(rpb, ((0, 0), (0, 1), (0, 128 - rpb.shape[2])))

    def bands(r_ref, e_ref, m_ref, o_ref):
        o_ref[0] = lax.dot_general(r_ref[0], e_ref[...], (((1,), (1,)), ((), ())), precision=HI,
                                   preferred_element_type=F32) + m_ref[...]

    t = pl.pallas_call(
        bands, out_shape=jax.ShapeDtypeStruct((H, 16, GRID_W * GRID_W), F32), grid=(H,),
        in_specs=[pl.BlockSpec((1, 16, 128), lambda h: (h, 0, 0)),
                  pl.BlockSpec((GRID_W * GRID_W, 128), lambda h: (0, 0)),
                  pl.BlockSpec((1, GRID_W * GRID_W), lambda h: (0, 0))],
        out_specs=pl.BlockSpec((1, 16, GRID_W * GRID_W), lambda h: (h, 0, 0)),
        name=name + "_bands", compiler_params=_cp(("parallel",)))(rp, jnp.asarray(e_np), jnp.asarray(mask))
    t = t.reshape(H, 16, GRID_W, GRID_W)

    def place(t_ref, o_ref):
        for d in range(NA_ROWS):
            for kr in range(NA_ROWS):
                o_ref[0, d, :, kr * GRID_W:(kr + 1) * GRID_W] = t_ref[0, kr - d + NA_ROWS - 1]

    return pl.pallas_call(
        place, out_shape=jax.ShapeDtypeStruct((H, NA_ROWS, GRID_W, NA_ROWS * GRID_W), F32), grid=(H,),
        in_specs=[pl.BlockSpec((1, 16, GRID_W, GRID_W), lambda h: (h, 0, 0, 0))],
        out_specs=pl.BlockSpec((1, NA_ROWS, GRID_W, NA_ROWS * GRID_W), lambda h: (h, 0, 0, 0)),
        name=name, compiler_params=_cp(("parallel",)))(t)


def _seg_mean_matrix(width, seg):
    i = np.arange(width)
    return jnp.asarray((i[:, None] // seg == i[None, :] // seg).astype(np.float32) / seg, BF16)


def _seg_mean(x, seg_ref):
    hi = x.astype(BF16)
    lo = (x - hi.astype(F32)).astype(BF16)
    return (jnp.dot(hi, seg_ref[...], preferred_element_type=F32) + jnp.dot(lo, seg_ref[...], preferred_element_type=F32))


def _na_fwd(z, qg, kg, bias, S, name):
    T = z.shape[0]
    E = T // S
    rows = S // GRID_W
    WIN = NA_ROWS * GRID_W
    seg = _seg_mean_matrix(128, NA_DH)

    def body(q_ref, k_ref, v_ref, qg_ref, kg_ref, bias_ref, seg_ref, o_ref, qs, ks, vs, s_all, p_all):
        for c in range(S // _RC):
            sl = slice(c * _RC, (c + 1) * _RC)
            q = q_ref[sl, :]
            k = k_ref[sl, :]
            qn = q * lax.rsqrt(_seg_mean(q * q, seg_ref) + EPS) * qg_ref[...]
            kn = k * lax.rsqrt(_seg_mean(k * k, seg_ref) + EPS) * kg_ref[...]
            v = v_ref[sl, :]
            for hh in range(2):
                ls = slice(hh * NA_DH, (hh + 1) * NA_DH)
                qs[hh, sl, :] = qn[:, ls].astype(BF16)
                ks[hh, sl, :] = kn[:, ls].astype(BF16)
                vs[hh, sl, :] = v[:, ls].astype(BF16)
        def where(r):
            rs = jnp.clip(r - NA_ROWS // 2, 0, rows - NA_ROWS)
            return rs, pl.multiple_of(r * GRID_W, GRID_W), pl.multiple_of(rs * GRID_W, GRID_W)

        def scores(r, carry):
            rs, q0, k0 = where(r)
            for hh in range(2):
                s = lax.dot_general(qs[hh, pl.ds(q0, GRID_W), :], ks[hh, pl.ds(k0, WIN), :], (((1,), (1,)), ((), ())),
                                    preferred_element_type=F32) * (NA_DH ** -0.5)
                s_all[hh, pl.ds(q0, GRID_W), :] = s + bias_ref[hh, r - rs]
            return carry
        lax.fori_loop(0, rows, scores, 0, unroll=4)

        def soft(r, carry):
            _, q0, _ = where(r)
            for hh in range(2):
                s = s_all[hh, pl.ds(q0, GRID_W), :]
                p = jnp.exp(s - jnp.max(s, axis=-1, keepdims=True))
                p_all[hh, pl.ds(q0, GRID_W), :] = (p * (1.0 / jnp.sum(p, axis=-1, keepdims=True))).astype(BF16)
            return carry
        lax.fori_loop(0, rows, soft, 0, unroll=2)

        def outp(r, carry):
            _, q0, k0 = where(r)
            outs = [jnp.dot(p_all[hh, pl.ds(q0, GRID_W), :], vs[hh, pl.ds(k0, WIN), :], preferred_element_type=F32)
                    for hh in range(2)]
            o_ref[pl.ds(q0, GRID_W), :] = jnp.concatenate(outs, axis=1)
            return carry
        lax.fori_loop(0, rows, outp, 0, unroll=4)

    LW = 128
    return pl.pallas_call(
        body, out_shape=jax.ShapeDtypeStruct((T, GROUP_W), F32), grid=(E, GROUP_W // LW),
        in_specs=[pl.BlockSpec((S, LW), lambda e, j: (e, B_Q // LW + j)),
                  pl.BlockSpec((S, LW), lambda e, j: (e, B_K // LW + j)),
                  pl.BlockSpec((S, LW), lambda e, j: (e, B_V // LW + j)),
                  pl.BlockSpec((1, LW), lambda e, j: (0, j)),
                  pl.BlockSpec((1, LW), lambda e, j: (0, j)),
                  pl.BlockSpec((2, NA_ROWS, GRID_W, WIN), lambda e, j: (j, 0, 0, 0)),
                  pl.BlockSpec((LW, LW), lambda e, j: (0, 0))],
        out_specs=pl.BlockSpec((S, LW), lambda e, j: (e, j)),
        scratch_shapes=[pltpu.VMEM((2, S, NA_DH), BF16)] * 3 + [pltpu.VMEM((2, S, WIN), F32), pltpu.VMEM((2, S, WIN), BF16)],
        name=name, compiler_params=_cp(("parallel", "parallel")))(z, z, z, qg, kg, bias, seg)


def _na_bwd(z, qg, kg, bias, do, S, name):
    T = z.shape[0]
    E = T // S
    rows = S // GRID_W
    WIN = NA_ROWS * GRID_W
    seg = _seg_mean_matrix(128, NA_DH)
    SC = NA_DH ** -0.5

    def body(q_ref, k_ref, v_ref, qg_ref, kg_ref, bias_ref, seg_ref, do_ref,
             dq_ref, dk_ref, dv_ref, dbias_ref, dqg_ref, dkg_ref, qs, ks, vs, dos, dqn, dkn, dvs, akt, avt,
             s_all, dp_all, p_all, ds_all):
        e = pl.program_id(1)

        @pl.when(e == 0)
        def _():
            dbias_ref[...] = jnp.zeros_like(dbias_ref)
            dqg_ref[...] = jnp.zeros_like(dqg_ref)
            dkg_ref[...] = jnp.zeros_like(dkg_ref)

        for c in range(S // _RC):
            sl = slice(c * _RC, (c + 1) * _RC)
            q = q_ref[sl, :]
            k = k_ref[sl, :]
            qn = q * lax.rsqrt(_seg_mean(q * q, seg_ref) + EPS) * qg_ref[...]
            kn = k * lax.rsqrt(_seg_mean(k * k, seg_ref) + EPS) * kg_ref[...]
            v = v_ref[sl, :]
            dd = do_ref[sl, :]
            for hh in range(2):
                ls = slice(hh * NA_DH, (hh + 1) * NA_DH)
                qs[hh, sl, :] = qn[:, ls].astype(BF16)
                ks[hh, sl, :] = kn[:, ls].astype(BF16)
                vs[hh, sl, :] = v[:, ls].astype(BF16)
                dos[hh, sl, :] = dd[:, ls].astype(BF16)
        akt[...] = jnp.zeros_like(akt)
        avt[...] = jnp.zeros_like(avt)

        def where(r):
            rs = jnp.clip(r - NA_ROWS // 2, 0, rows - NA_ROWS)
            return rs, pl.multiple_of(r * GRID_W, GRID_W), pl.multiple_of(rs * GRID_W, GRID_W)

        for hh in range(2):
            ls = slice(hh * NA_DH, (hh + 1) * NA_DH)

            def products(r, carry, hh=hh):
                rs, q0, k0 = where(r)
                s = lax.dot_general(qs[hh, pl.ds(q0, GRID_W), :], ks[hh, pl.ds(k0, WIN), :], (((1,), (1,)), ((), ())),
                                    preferred_element_type=F32) * SC
                s_all[pl.ds(q0, GRID_W), :] = s + bias_ref[hh, r - rs]
                dp_all[pl.ds(q0, GRID_W), :] = lax.dot_general(dos[hh, pl.ds(q0, GRID_W), :], vs[hh, pl.ds(k0, WIN), :],
                                                               (((1,), (1,)), ((), ())), preferred_element_type=F32)
                return carry
            lax.fori_loop(0, rows, products, 0, unroll=4)

            def soft(r, carry, hh=hh):
                rs, q0, _ = where(r)
                s = s_all[pl.ds(q0, GRID_W), :]
                p = jnp.exp(s - jnp.max(s, axis=-1, keepdims=True))
                p = p * (1.0 / jnp.sum(p, axis=-1, keepdims=True))
                dp = dp_all[pl.ds(q0, GRID_W), :]
                ds = p * (dp - jnp.sum(p * dp, axis=-1, keepdims=True))
                dbias_ref[hh, r - rs] += ds
                p_all[pl.ds(q0, GRID_W), :] = p.astype(BF16)
                ds_all[pl.ds(q0, GRID_W), :] = ds.astype(BF16)
                return carry
            lax.fori_loop(0, rows, soft, 0, unroll=2)

            def grads(r, carry, hh=hh, ls=ls):
                rs, q0, k0 = where(r)
                par = rs % 2
                t0 = (rs + par) // 2
                qr = qs[hh, pl.ds(q0, GRID_W), :]
                dor = dos[hh, pl.ds(q0, GRID_W), :]
                dsb = ds_all[pl.ds(q0, GRID_W), :]
                dqn[pl.ds(q0, GRID_W), ls] = jnp.dot(dsb, ks[hh, pl.ds(k0, WIN), :], preferred_element_type=F32) * SC
                dkt = lax.dot_general(qr, dsb, (((0,), (0,)), ((), ())), preferred_element_type=F32) * SC
                dvt = lax.dot_general(dor, p_all[pl.ds(q0, GRID_W), :], (((0,), (0,)), ((), ())), preferred_element_type=F32)
                akt[hh, par, pl.ds(t0, WIN // 128)] += jnp.stack([dkt[:, 128 * i:128 * (i + 1)] for i in range(WIN // 128)])
                avt[hh, par, pl.ds(t0, WIN // 128)] += jnp.stack([dvt[:, 128 * i:128 * (i + 1)] for i in range(WIN // 128)])
                return carry
            lax.fori_loop(0, rows, grads, 0, unroll=4)

        for hh in range(2):
            ls = slice(hh * NA_DH, (hh + 1) * NA_DH)
            for i in range(S // 128):
                for acc, dst in ((akt, dkn), (avt, dvs)):
                    odd = jnp.concatenate([acc[hh, 1, i][:, NA_DH:], acc[hh, 1, i + 1][:, :NA_DH]], axis=1)
                    dst[128 * i:128 * (i + 1), ls] = (acc[hh, 0, i] + odd).T

        for c in range(S // _RC):
            sl = slice(c * _RC, (c + 1) * _RC)
            for x_ref, g_ref, dn, dx_ref, dg_ref in ((q_ref, qg_ref, dqn, dq_ref, dqg_ref), (k_ref, kg_ref, dkn, dk_ref, dkg_ref)):
                x = x_ref[sl, :]
                r_ = lax.rsqrt(_seg_mean(x * x, seg_ref) + EPS)
                xh = x * r_
                d = dn[sl, :]
                dxh = d * g_ref[...]
                mean = _seg_mean(dxh * xh, seg_ref)
                dx_ref[sl, :] = (r_ * (dxh - xh * mean)).astype(BF16)
                dg_ref[...] += jnp.sum(d * xh, axis=0, keepdims=True)
            dv_ref[sl, :] = dvs[sl, :].astype(BF16)

    LW = 128
    blk = pl.BlockSpec((S, LW), lambda j, e: (e, j))
    vec = pl.BlockSpec((1, LW), lambda j, e: (0, j))
    bsp = pl.BlockSpec((2, NA_ROWS, GRID_W, WIN), lambda j, e: (j, 0, 0, 0))
    return pl.pallas_call(
        body, out_shape=(jax.ShapeDtypeStruct((T, GROUP_W), BF16),) * 3 + (
            jax.ShapeDtypeStruct((NA_HEADS, NA_ROWS, GRID_W, WIN), F32),
            jax.ShapeDtypeStruct((1, GROUP_W), F32), jax.ShapeDtypeStruct((1, GROUP_W), F32)),
        grid=(GROUP_W // LW, E),
        in_specs=[pl.BlockSpec((S, LW), lambda j, e: (e, B_Q // LW + j)),
                  pl.BlockSpec((S, LW), lambda j, e: (e, B_K // LW + j)),
                  pl.BlockSpec((S, LW), lambda j, e: (e, B_V // LW + j)),
                  vec, vec, bsp, pl.BlockSpec((LW, LW), lambda j, e: (0, 0)), blk],
        out_specs=(blk, blk, blk, bsp, vec, vec),
        scratch_shapes=[pltpu.VMEM((2, S, NA_DH), BF16)] * 4 + [pltpu.VMEM((S, LW), F32)] * 3
        + [pltpu.VMEM((2, 2, S // 128 + 1, NA_DH, 128), F32)] * 2
        + [pltpu.VMEM((S, WIN), F32)] * 2 + [pltpu.VMEM((S, WIN), BF16)] * 2,
        name=name, compiler_params=_cp(("parallel", "arbitrary")))(z, z, z, qg, kg, bias, seg, do)


def _na_rpb_grad(dbias, name):
    e_np, _, _ = _na_onehots()
    H = NA_HEADS
    nro = 2 * NA_ROWS - 1

    def fold(x_ref, o_ref):
        for ro in range(nro):
            acc = None
            for d in range(NA_ROWS):
                kr = ro + d - (NA_ROWS - 1)
                if 0 <= kr < NA_ROWS:
                    blk = x_ref[0, d, :, kr * GRID_W:(kr + 1) * GRID_W]
                    acc = blk if acc is None else acc + blk
            o_ref[0, ro] = acc
        o_ref[0, nro] = jnp.zeros((GRID_W, GRID_W), F32)

    t = pl.pallas_call(
        fold, out_shape=jax.ShapeDtypeStruct((H, 16, GRID_W, GRID_W), F32), grid=(H,),
        in_specs=[pl.BlockSpec((1, NA_ROWS, GRID_W, NA_ROWS * GRID_W), lambda h: (h, 0, 0, 0))],
        out_specs=pl.BlockSpec((1, 16, GRID_W, GRID_W), lambda h: (h, 0, 0, 0)),
        name=name + "_fold", compiler_params=_cp(("parallel",)))(dbias)
    t = t.reshape(H, 16, GRID_W * GRID_W)

    def body(x_ref, e_ref, o_ref):
        o_ref[0] = jnp.dot(x_ref[0], e_ref[...], precision=HI, preferred_element_type=F32)

    out = pl.pallas_call(
        body, out_shape=jax.ShapeDtypeStruct((H, 16, 128), F32), grid=(H,),
        in_specs=[pl.BlockSpec((1, 16, GRID_W * GRID_W), lambda h: (h, 0, 0)),
                  pl.BlockSpec((GRID_W * GRID_W, 128), lambda h: (0, 0))],
        out_specs=pl.BlockSpec((1, 16, 128), lambda h: (h, 0, 0)),
        name=name, compiler_params=_cp(("parallel",)))(t, jnp.asarray(e_np))
    return out[:, :nro, :2 * NA_COLS - 1]


_HK = GLA_HEADS * GLA_DK
_HV = GLA_HEADS * GLA_DV


def _gla_consts(reverse):
    i = np.arange(CHUNK)
    tri = (i[:, None] <= i[None, :]) if reverse else (i[:, None] >= i[None, :])
    j = np.arange(_HK)
    oseg = (j[:, None] // GLA_DK == j[None, :] // GLA_DK)
    return (jnp.asarray(tri.astype(np.float32)), jnp.asarray(tri.T.astype(np.float32)), jnp.asarray(oseg.astype(np.float32), BF16))


def _log_decay(lr, a2, ab):
    zg = jnp.dot(lr, a2, precision=HI, preferred_element_type=F32) + ab
    g = (jnp.minimum(zg, 0.0) - jnp.log(1.0 + jnp.exp(-jnp.abs(zg)))) * (1.0 / GLA_TAU)
    return zg, g


def _dotf(a, b, dn):
    return lax.dot_general(a, b, dn, precision=HI, preferred_element_type=F32)


def _dotb(a, b, dn):
    return lax.dot_general(a.astype(BF16), b.astype(BF16), dn, preferred_element_type=F32)


_COLS = 4
_SUB = 16
_NSUB = CHUNK // _SUB


def _gla_cross_blocks(reverse):
    return range(0, _NSUB - 1) if reverse else range(1, _NSUB)


def _gla_cross_terms(s, reverse, b_s, q_s, k_s, oseg_ref):
    r0 = s * _SUB
    ref = r0 + (_SUB - 1 if reverse else 0)
    bref = b_s[ref:ref + 1, :]
    rowj = lax.broadcasted_iota(jnp.int32, (CHUNK, 1), 0)
    seen = (rowj >= r0 + _SUB) if reverse else (rowj < r0)
    ek = jnp.where(seen, jnp.exp(jnp.minimum(bref - b_s[...], 0.0)), 0.0)
    kt = k_s[...] * ek
    eq = jnp.exp(jnp.minimum(b_s[r0:r0 + _SUB, :] - bref, 0.0))
    qt = q_s[r0:r0 + _SUB, :] * eq
    nmat = jnp.concatenate([kt.astype(BF16)] * GLA_HEADS, axis=0) * oseg_ref[...]
    return qt, eq, kt, ek, nmat


_NN = (((1,), (0,)), ((), ()))
_NT = (((1,), (1,)), ((), ()))
_TN = (((0,), (0,)), ((), ()))


_DT = 256


def _gla_block_tri(reverse):
    i = np.arange(_DT)
    same = i[:, None] // CHUNK == i[None, :] // CHUNK
    tri = (i[:, None] <= i[None, :]) if reverse else (i[:, None] >= i[None, :])
    return (tri & same).astype(np.float32)


def _gla_decay_fwd(z, a2, ab, reverse, name):
    T = z.shape[0]
    nc = _DT // CHUNK

    def body(lr_ref, a2_ref, ab_ref, m_ref, b_ref, ec_ref):
        _, g = _log_decay(lr_ref[...], a2_ref[...], ab_ref[...])
        b_ref[...] = _dotf(m_ref[...], g, _NN)
        for c in range(nc):
            ec_ref[c] = jnp.exp(_dotf(g[c * CHUNK:(c + 1) * CHUNK, :], jnp.ones((CHUNK, GLA_DV), F32), _TN))

    return pl.pallas_call(
        body, out_shape=(jax.ShapeDtypeStruct((T, _HK), F32), jax.ShapeDtypeStruct((T // CHUNK, _HK, GLA_DV), F32)),
        grid=(T // _DT,),
        in_specs=[pl.BlockSpec((_DT, 128), lambda i: (i, LR_OFF // 128)),
                  pl.BlockSpec((128, _HK), lambda i: (0, 0)),
                  pl.BlockSpec((1, _HK), lambda i: (0, 0)),
                  pl.BlockSpec((_DT, _DT), lambda i: (0, 0))],
        out_specs=(pl.BlockSpec((_DT, _HK), lambda i: (i, 0)), pl.BlockSpec((nc, _HK, GLA_DV), lambda i: (i, 0, 0))),
        name=name, compiler_params=_cp(("parallel",)))(z, a2, ab, jnp.asarray(_gla_block_tri(reverse)))


def _gla_decay_bwd(z, a2_f, ab_f, a2_b, ab_b, db_f, db_b, name):
    T = z.shape[0]

    def body(lr_ref, a2f_ref, abf_ref, a2b_ref, abb_ref, mf_ref, mb_ref, dbf_ref, dbb_ref,
             dlr_ref, da2f_ref, dabf_ref, da2b_ref, dabb_ref):
        @pl.when(pl.program_id(0) == 0)
        def _():
            for r in (da2f_ref, dabf_ref, da2b_ref, dabb_ref):
                r[...] = jnp.zeros_like(r)

        lr = lr_ref[...]
        dlr = jnp.zeros((_DT, 128), F32)
        for a2_ref, ab_ref, mt_ref, db_ref, da2_ref, dab_ref in ((a2f_ref, abf_ref, mf_ref, dbf_ref, da2f_ref, dabf_ref),
                                                                 (a2b_ref, abb_ref, mb_ref, dbb_ref, da2b_ref, dabb_ref)):
            zg, _ = _log_decay(lr, a2_ref[...], ab_ref[...])
            dg = _dotf(mt_ref[...], db_ref[...], _NN)
            dzg = dg * (1.0 / (1.0 + jnp.exp(zg))) * (1.0 / GLA_TAU)
            dlr = dlr + _dotf(dzg, a2_ref[...], _NT)
            da2_ref[...] += _dotf(lr, dzg, _TN)
            dab_ref[...] += jnp.sum(dzg, axis=0, keepdims=True)
        dlr_ref[...] = dlr.astype(BF16)

    a2s = pl.BlockSpec((128, _HK), lambda i: (0, 0))
    abs_ = pl.BlockSpec((1, _HK), lambda i: (0, 0))
    ms = pl.BlockSpec((_DT, _DT), lambda i: (0, 0))
    row = pl.BlockSpec((_DT, _HK), lambda i: (i, 0))
    return pl.pallas_call(
        body, out_shape=(jax.ShapeDtypeStruct((T, 128), BF16), jax.ShapeDtypeStruct((128, _HK), F32), jax.ShapeDtypeStruct((1, _HK), F32),
                         jax.ShapeDtypeStruct((128, _HK), F32), jax.ShapeDtypeStruct((1, _HK), F32)),
        grid=(T // _DT,),
        in_specs=[pl.BlockSpec((_DT, 128), lambda i: (i, LR_OFF // 128)), a2s, abs_, a2s, abs_, ms, ms, row, row],
        out_specs=(pl.BlockSpec((_DT, 128), lambda i: (i, 0)), a2s, abs_, a2s, abs_),
        name=name, compiler_params=_cp(("arbitrary",)))(
            z, a2_f, ab_f, a2_b, ab_b, jnp.asarray(_gla_block_tri(False).T), jnp.asarray(_gla_block_tri(True).T), db_f, db_b)


def _gla_fwd(z, b_all, ecol, S, reverse, name):
    T = z.shape[0]
    E = T // S
    n = S // CHUNK
    _, _, oseg = _gla_consts(reverse)
    last = 0 if reverse else CHUNK - 1

    def body(q_ref, k_ref, v_ref, b_ref, ec_ref, oseg_ref, o_ref, a_ref, st_ref, st, b_s, q_s, k_s):
        @pl.when(pl.program_id(1) == 0)
        def _():
            st[...] = jnp.zeros_like(st)

        q = q_ref[...] * (GLA_DK ** -0.5)
        k = k_ref[...]
        v = v_ref[...]
        b = b_ref[...]
        bl_row = b_ref[last:last + 1, :]
        e_col = ec_ref[0]
        b_s[...] = b
        q_s[...] = q
        k_s[...] = k
        lane = lax.broadcasted_iota(jnp.int32, (1, _HK), 1) % GLA_DK

        rowi = lax.broadcasted_iota(jnp.int32, (CHUNK, 1), 0)
        blk0 = (rowi // _SUB) * _SUB

        def cols(jj, a):
            ts = []
            for u in range(_COLS):
                jp = jj * _COLS + u
                tiles = []
                for s in range(_NSUB):
                    rs_ = slice(s * _SUB, (s + 1) * _SUB)
                    bj = b_s[pl.ds(s * _SUB + jp, 1), :]
                    kj = k_s[pl.ds(s * _SUB + jp, 1), :]
                    tiles.append(q_s[rs_, :] * jnp.exp(jnp.minimum(b_s[rs_, :] - bj, 0.0)) * kj)
                ts.append(jnp.concatenate(tiles, axis=0).astype(BF16))
            r = jnp.dot(jnp.concatenate(ts, axis=0), oseg_ref[...], preferred_element_type=F32)
            for u in range(_COLS):
                a = jnp.where(lane == blk0 + (jj * _COLS + u), r[u * CHUNK:(u + 1) * CHUNK, :], a)
            return a

        a = lax.fori_loop(0, _SUB // _COLS, cols, jnp.zeros((CHUNK, _HK), F32))
        keep = (rowi <= lane) if reverse else (rowi >= lane)
        a = jnp.where(keep, a, 0.0)
        cross = []
        for s in range(_NSUB):
            if s in _gla_cross_blocks(reverse):
                qt, _, _, _, nmat = _gla_cross_terms(s, reverse, b_s, q_s, k_s, oseg_ref)
                cross.append(lax.dot_general(qt.astype(BF16), nmat, _NT, preferred_element_type=F32))
            else:
                cross.append(jnp.zeros((_SUB, _HK), F32))
        a = a + jnp.concatenate(cross, axis=0)
        a_ref[...] = a
        st_ref[0] = st[...]
        qb = q * jnp.exp(b)
        kd = k * jnp.exp(bl_row - b)
        for h in range(GLA_HEADS):
            ks_ = slice(h * GLA_DK, (h + 1) * GLA_DK)
            vs_ = slice(h * GLA_DV, (h + 1) * GLA_DV)
            s_h = st[ks_, :]
            o_ref[:, vs_] = _dotb(qb[:, ks_], s_h, _NN) + _dotb(a[:, ks_], v[:, vs_], _NN)
            st[ks_, :] = s_h * e_col[ks_, :] + _dotb(kd[:, ks_], v[:, vs_], _TN)

    def rowblk(e, c):
        return e * n + ((n - 1 - c) if reverse else c)

    return pl.pallas_call(
        body, out_shape=(jax.ShapeDtypeStruct((T, _HV), F32), jax.ShapeDtypeStruct((T, _HK), F32),
                         jax.ShapeDtypeStruct((T // CHUNK, _HK, GLA_DV), F32)),
        grid=(E, n),
        in_specs=[pl.BlockSpec((CHUNK, _HK), lambda e, c: (rowblk(e, c), C_Q // _HK)),
                  pl.BlockSpec((CHUNK, _HK), lambda e, c: (rowblk(e, c), C_K // _HK)),
                  pl.BlockSpec((CHUNK, _HV), lambda e, c: (rowblk(e, c), C_V // _HV)),
                  pl.BlockSpec((CHUNK, _HK), lambda e, c: (rowblk(e, c), 0)),
                  pl.BlockSpec((1, _HK, GLA_DV), lambda e, c: (rowblk(e, c), 0, 0)),
                  pl.BlockSpec((_HK, _HK), lambda e, c: (0, 0))],
        out_specs=(pl.BlockSpec((CHUNK, _HV), lambda e, c: (rowblk(e, c), 0)),
                   pl.BlockSpec((CHUNK, _HK), lambda e, c: (rowblk(e, c), 0)),
                   pl.BlockSpec((1, _HK, GLA_DV), lambda e, c: (rowblk(e, c), 0, 0))),
        scratch_shapes=[pltpu.VMEM((_HK, GLA_DV), F32)] + [pltpu.VMEM((CHUNK, _HK), F32)] * 3,
        name=name, compiler_params=_cp(("parallel", "arbitrary")))(z, z, z, b_all, ecol, oseg)


def _gla_fwd_both(z, b_f, ec_f, b_b, ec_b, S, name):
    T = z.shape[0]
    E = T // S
    n = S // CHUNK
    _, _, oseg = _gla_consts(False)

    def body(*refs):
        oseg_ref = refs[10]
        dirs = []
        for di, reverse in enumerate((False, True)):
            q_ref, k_ref, v_ref, b_ref, ec_ref = refs[5 * di:5 * di + 5]
            o_ref, a_ref, st_ref = refs[11 + 3 * di:14 + 3 * di]
            st, b_s, q_s, k_s = refs[17 + 4 * di:21 + 4 * di]
            dirs.append((reverse, q_ref, k_ref, v_ref, b_ref, ec_ref, o_ref, a_ref, st_ref, st, b_s, q_s, k_s))

        @pl.when(pl.program_id(1) == 0)
        def _():
            for d in dirs:
                d[9][...] = jnp.zeros_like(d[9])

        lane = lax.broadcasted_iota(jnp.int32, (1, _HK), 1) % GLA_DK
        rowi = lax.broadcasted_iota(jnp.int32, (CHUNK, 1), 0)
        blk0 = (rowi // _SUB) * _SUB
        for (_, q_ref, k_ref, _, b_ref, _, _, _, _, _, b_s, q_s, k_s) in dirs:
            b_s[...] = b_ref[...]
            q_s[...] = q_ref[...] * (GLA_DK ** -0.5)
            k_s[...] = k_ref[...]

        def cols(d, jj, a):
            b_s, q_s, k_s = d[10], d[11], d[12]
            ts = []
            for u in range(_COLS):
                jp = jj * _COLS + u
                tiles = []
                for s in range(_NSUB):
                    rs_ = slice(s * _SUB, (s + 1) * _SUB)
                    bj = b_s[pl.ds(s * _SUB + jp, 1), :]
                    kj = k_s[pl.ds(s * _SUB + jp, 1), :]
                    tiles.append(q_s[rs_, :] * jnp.exp(jnp.minimum(b_s[rs_, :] - bj, 0.0)) * kj)
                ts.append(jnp.concatenate(tiles, axis=0).astype(BF16))
            r = jnp.dot(jnp.concatenate(ts, axis=0), oseg_ref[...], preferred_element_type=F32)
            for u in range(_COLS):
                a = jnp.where(lane == blk0 + (jj * _COLS + u), r[u * CHUNK:(u + 1) * CHUNK, :], a)
            return a

        zero = jnp.zeros((CHUNK, _HK), F32)
        acc = lax.fori_loop(0, _SUB // _COLS, lambda jj, c: tuple(cols(d, jj, a) for d, a in zip(dirs, c)), (zero, zero))

        for (reverse, _, _, v_ref, b_ref, ec_ref, o_ref, a_ref, st_ref, st, b_s, q_s, k_s), a in zip(dirs, acc):
            last = 0 if reverse else CHUNK - 1
            keep = (rowi <= lane) if reverse else (rowi >= lane)
            a = jnp.where(keep, a, 0.0)
            cross = []
            for s in range(_NSUB):
                if s in _gla_cross_blocks(reverse):
                    qt, _, _, _, nmat = _gla_cross_terms(s, reverse, b_s, q_s, k_s, oseg_ref)
                    cross.append(lax.dot_general(qt.astype(BF16), nmat, _NT, preferred_element_type=F32))
                else:
                    cross.append(jnp.zeros((_SUB, _HK), F32))
            a = a + jnp.concatenate(cross, axis=0)
            a_ref[...] = a
            st_ref[0] = st[...]
            b = b_s[...]
            v = v_ref[...]
            e_col = ec_ref[0]
            qb = q_s[...] * jnp.exp(b)
            kd = k_s[...] * jnp.exp(b_ref[last:last + 1, :] - b)
            for h in range(GLA_HEADS):
                ks_ = slice(h * GLA_DK, (h + 1) * GLA_DK)
                vs_ = slice(h * GLA_DV, (h + 1) * GLA_DV)
                s_h = st[ks_, :]
                o_ref[:, vs_] = _dotb(qb[:, ks_], s_h, _NN) + _dotb(a[:, ks_], v[:, vs_], _NN)
                st[ks_, :] = s_h * e_col[ks_, :] + _dotb(kd[:, ks_], v[:, vs_], _TN)

    def specs(reverse):
        rb = (lambda e, c: e * n + (n - 1 - c)) if reverse else (lambda e, c: e * n + c)
        ins = [pl.BlockSpec((CHUNK, _HK), lambda e, c: (rb(e, c), C_Q // _HK)),
               pl.BlockSpec((CHUNK, _HK), lambda e, c: (rb(e, c), C_K // _HK)),
               pl.BlockSpec((CHUNK, _HV), lambda e, c: (rb(e, c), C_V // _HV)),
               pl.BlockSpec((CHUNK, _HK), lambda e, c: (rb(e, c), 0)),
               pl.BlockSpec((1, _HK, GLA_DV), lambda e, c: (rb(e, c), 0, 0))]
        outs = [pl.BlockSpec((CHUNK, _HV), lambda e, c: (rb(e, c), 0)),
                pl.BlockSpec((CHUNK, _HK), lambda e, c: (rb(e, c), 0)),
                pl.BlockSpec((1, _HK, GLA_DV), lambda e, c: (rb(e, c), 0, 0))]
        return ins, outs

    in_f, out_f = specs(False)
    in_b, out_b = specs(True)
    shapes = (jax.ShapeDtypeStruct((T, _HV), F32), jax.ShapeDtypeStruct((T, _HK), F32),
              jax.ShapeDtypeStruct((T // CHUNK, _HK, GLA_DV), F32))
    res = pl.pallas_call(
        body, out_shape=shapes + shapes, grid=(E, n),
        in_specs=in_f + in_b + [pl.BlockSpec((_HK, _HK), lambda e, c: (0, 0))], out_specs=tuple(out_f + out_b),
        scratch_shapes=([pltpu.VMEM((_HK, GLA_DV), F32)] + [pltpu.VMEM((CHUNK, _HK), F32)] * 3) * 2,
        name=name, compiler_params=_cp(("parallel", "arbitrary")))(z, z, z, b_f, ec_f, z, z, z, b_b, ec_b, oseg)
    return res[:3], res[3:]


def _gla_bwd(z, b_all, ecol, att, states, do, prev, S, reverse, name):
    T = z.shape[0]
    E = T // S
    n = S // CHUNK
    _, _, oseg = _gla_consts(reverse)
    has_prev = prev is not None
    odt = BF16 if has_prev else F32
    last = 0 if reverse else CHUNK - 1

    def body(*refs):
        (q_ref, k_ref, v_ref, b_ref, ec_ref, oseg_ref, att_ref, st_ref, do_ref) = refs[:9]
        refs = refs[9:]
        if has_prev:
            pq_ref, pk_ref, pv_ref = refs[:3]
            refs = refs[3:]
        (dq_ref, dk_ref, dv_ref, db_ref, dst, b_s, q_s, k_s, da_s, dqb_s, dkd_s, dk3_s, dbn_s, dsp_s) = refs

        @pl.when(pl.program_id(1) == 0)
        def _():
            dst[...] = jnp.zeros_like(dst)

        q = q_ref[...] * (GLA_DK ** -0.5)
        k = k_ref[...]
        v = v_ref[...]
        b = b_ref[...]
        bl_row = b_ref[last:last + 1, :]
        eb = jnp.exp(b)
        ekd = jnp.exp(bl_row - b)
        qb = q * eb
        kd = k * ekd
        b_s[...] = b
        q_s[...] = q
        k_s[...] = k
        att = att_ref[...]
        s_all = st_ref[0]
        dsn = dst[...]
        e_col = ec_ref[0]
        do = do_ref[...]
        lane = lax.broadcasted_iota(jnp.int32, (1, _HK), 1) % GLA_DK
        rowi = lax.broadcasted_iota(jnp.int32, (CHUNK, 1), 0)
        keep = (rowi <= lane) if reverse else (rowi >= lane)
        for h in range(GLA_HEADS):
            ks_ = slice(h * GLA_DK, (h + 1) * GLA_DK)
            vs_ = slice(h * GLA_DV, (h + 1) * GLA_DV)
            do_h = do[:, vs_]
            s_h = s_all[ks_, :]
            dsn_h = dsn[ks_, :]
            dqb_s[:, ks_] = _dotb(do_h, s_h, _NT)
            dsp_s[ks_, :] = _dotb(qb[:, ks_], do_h, _TN) + dsn_h * e_col[ks_, :]
            da_s[:, ks_] = _dotb(do_h, v[:, vs_], _NT)
            dv_h = _dotb(att[:, ks_], do_h, _TN) + _dotb(kd[:, ks_], dsn_h, _NN)
            if has_prev:
                dv_h = dv_h + pv_ref[:, vs_]
            dv_ref[:, vs_] = dv_h.astype(odt)
            dkd_s[:, ks_] = _dotb(v[:, vs_], dsn_h, _NT)
        da_s[...] = jnp.where(keep, da_s[...], 0.0)
        dqb = dqb_s[...]
        dkd = dkd_s[...]
        x = dsn * s_all * e_col
        dbl_row = _dotf(jnp.ones((8, GLA_DV), F32), x, _NT)[0:1, :] + jnp.sum(dkd * kd, axis=0, keepdims=True)

        blk0 = (rowi // _SUB) * _SUB

        def cols(jj, carry):
            dq3, db3 = list(carry[:_NSUB]), list(carry[_NSUB:])
            sel = [jnp.where(lane == blk0 + (jj * _COLS + u), da_s[...], 0.0).astype(BF16) for u in range(_COLS)]
            dcols = jnp.dot(jnp.concatenate(sel, axis=0), oseg_ref[...], preferred_element_type=F32)
            for u in range(_COLS):
                jp = jj * _COLS + u
                for s in range(_NSUB):
                    rs_ = slice(s * _SUB, (s + 1) * _SUB)
                    bj = b_s[pl.ds(s * _SUB + jp, 1), :]
                    kj = k_s[pl.ds(s * _SUB + jp, 1), :]
                    tm_ = dcols[u * CHUNK + s * _SUB:u * CHUNK + (s + 1) * _SUB, :] * jnp.exp(jnp.minimum(b_s[rs_, :] - bj, 0.0))
                    dq3[s] = dq3[s] + tm_ * kj
                    gq = tm_ * q_s[rs_, :]
                    dk3_s[pl.ds(s * _SUB + jp, 1), :] = jnp.sum(gq, axis=0, keepdims=True)
                    w = gq * kj
                    dbn_s[pl.ds(s * _SUB + jp, 1), :] = jnp.sum(w, axis=0, keepdims=True)
                    db3[s] = db3[s] + w
            return tuple(dq3) + tuple(db3)

        zero = jnp.zeros((_SUB, _HK), F32)
        acc = lax.fori_loop(0, _SUB // _COLS, cols, (zero,) * (2 * _NSUB))
        dq3 = jnp.concatenate(acc[:_NSUB], axis=0)
        db3 = jnp.concatenate(acc[_NSUB:], axis=0)
        head = lax.broadcasted_iota(jnp.int32, (1, _HK), 1) // GLA_DK
        dq_x, db_x = [], []
        dk_x = jnp.zeros((CHUNK, _HK), F32)
        db_k = jnp.zeros((CHUNK, _HK), F32)
        for s in range(_NSUB):
            if s not in _gla_cross_blocks(reverse):
                dq_x.append(zero)
                db_x.append(zero)
                continue
            r0 = s * _SUB
            qt, eq, kt, ek, nmat = _gla_cross_terms(s, reverse, b_s, q_s, k_s, oseg_ref)
            seen = (lane >= r0 + _SUB) if reverse else (lane < r0)
            dax = jnp.where(seen, da_s[r0:r0 + _SUB, :], 0.0).astype(BF16)
            dqt = jnp.dot(dax, nmat, preferred_element_type=F32)
            full = lax.dot_general(dax, qt.astype(BF16), _TN, preferred_element_type=F32)
            dkt = full[0:CHUNK, :]
            for h in range(1, GLA_HEADS):
                dkt = jnp.where(head == h, full[h * CHUNK:(h + 1) * CHUNK, :], dkt)
            dq_x.append(dqt * eq)
            db_x.append(dqt * qt)
            dk_x = dk_x + dkt * ek
            db_k = db_k + dkt * kt
        dq = (dqb * eb + dq3 + jnp.concatenate(dq_x, axis=0)) * (GLA_DK ** -0.5)
        dk = dkd * ekd + dk3_s[...] + dk_x
        db = dqb * qb - dkd * kd + db3 - dbn_s[...] + jnp.concatenate(db_x, axis=0) - db_k
        db_ref[...] = jnp.where(rowi == last, db + dbl_row, db)
        if has_prev:
            dq = dq + pq_ref[...]
            dk = dk + pk_ref[...]
        dq_ref[...] = dq.astype(odt)
        dk_ref[...] = dk.astype(odt)
        dst[...] = dsp_s[...]

    def rowblk(e, c):
        return e * n + (c if reverse else (n - 1 - c))

    hk = pl.BlockSpec((CHUNK, _HK), lambda e, c: (rowblk(e, c), 0))
    hv = pl.BlockSpec((CHUNK, _HV), lambda e, c: (rowblk(e, c), 0))
    stb = pl.BlockSpec((1, _HK, GLA_DV), lambda e, c: (rowblk(e, c), 0, 0))
    in_specs = [pl.BlockSpec((CHUNK, _HK), lambda e, c: (rowblk(e, c), C_Q // _HK)),
                pl.BlockSpec((CHUNK, _HK), lambda e, c: (rowblk(e, c), C_K // _HK)),
                pl.BlockSpec((CHUNK, _HV), lambda e, c: (rowblk(e, c), C_V // _HV)),
                hk, stb, pl.BlockSpec((_HK, _HK), lambda e, c: (0, 0)), hk, stb, hv]
    args = [z, z, z, b_all, ecol, oseg, att, states, do]
    if has_prev:
        in_specs += [hk, hk, hv]
        args += list(prev)
    return pl.pallas_call(
        body, out_shape=(jax.ShapeDtypeStruct((T, _HK), odt), jax.ShapeDtypeStruct((T, _HK), odt),
                         jax.ShapeDtypeStruct((T, _HV), odt), jax.ShapeDtypeStruct((T, _HK), F32)),
        grid=(E, n), in_specs=in_specs, out_specs=(hk, hk, hv, hk),
        scratch_shapes=[pltpu.VMEM((_HK, GLA_DV), F32)] + [pltpu.VMEM((CHUNK, _HK), F32)] * 8 + [pltpu.VMEM((_HK, GLA_DV), F32)],
        name=name, compiler_params=_cp(("parallel", "arbitrary")))(*args)


def _gla_bwd_both(z, fwd_saved, rev_saved, do, S, name):
    T = z.shape[0]
    E = T // S
    n = S // CHUNK
    _, _, oseg = _gla_consts(False)
    NI, NO, NS = 8, 4, 10

    def body(*refs):
        oseg_ref = refs[2 * NI]
        dirs = []
        for di, reverse in enumerate((False, True)):
            ins = refs[NI * di:NI * (di + 1)]
            outs = refs[2 * NI + 1 + NO * di:2 * NI + 1 + NO * (di + 1)]
            scr = refs[2 * NI + 1 + 2 * NO + NS * di:2 * NI + 1 + 2 * NO + NS * (di + 1)]
            dirs.append((reverse, ins, outs, scr))

        @pl.when(pl.program_id(1) == 0)
        def _():
            for d in dirs:
                d[3][0][...] = jnp.zeros_like(d[3][0])

        lane = lax.broadcasted_iota(jnp.int32, (1, _HK), 1) % GLA_DK
        head = lax.broadcasted_iota(jnp.int32, (1, _HK), 1) // GLA_DK
        rowi = lax.broadcasted_iota(jnp.int32, (CHUNK, 1), 0)
        blk0 = (rowi // _SUB) * _SUB

        def factors(reverse, b_ref, b_s, q_s, k_s):
            last = 0 if reverse else CHUNK - 1
            b = b_s[...]
            eb = jnp.exp(b)
            ekd = jnp.exp(b_ref[last:last + 1, :] - b)
            return eb, ekd, q_s[...] * eb, k_s[...] * ekd

        dbl_rows = []
        for reverse, (q_ref, k_ref, v_ref, b_ref, ec_ref, att_ref, st_ref, do_ref), (dq_ref, dk_ref, dv_ref, db_ref), \
                (dst, b_s, q_s, k_s, da_s, dqb_s, dkd_s, dk3_s, dbn_s, dsp_s) in dirs:
            b_s[...] = b_ref[...]
            q_s[...] = q_ref[...] * (GLA_DK ** -0.5)
            k_s[...] = k_ref[...]
            _, _, qb, kd = factors(reverse, b_ref, b_s, q_s, k_s)
            v = v_ref[...]
            att = att_ref[...]
            s_all = st_ref[0]
            dsn = dst[...]
            e_col = ec_ref[0]
            do = do_ref[...]
            keep = (rowi <= lane) if reverse else (rowi >= lane)
            for h in range(GLA_HEADS):
                ks_ = slice(h * GLA_DK, (h + 1) * GLA_DK)
                vs_ = slice(h * GLA_DV, (h + 1) * GLA_DV)
                do_h = do[:, vs_]
                s_h = s_all[ks_, :]
                dsn_h = dsn[ks_, :]
                dqb_s[:, ks_] = _dotb(do_h, s_h, _NT)
                dsp_s[ks_, :] = _dotb(qb[:, ks_], do_h, _TN) + dsn_h * e_col[ks_, :]
                da_s[:, ks_] = _dotb(do_h, v[:, vs_], _NT)
                dv_ref[:, vs_] = _dotb(att[:, ks_], do_h, _TN) + _dotb(kd[:, ks_], dsn_h, _NN)
                dkd_s[:, ks_] = _dotb(v[:, vs_], dsn_h, _NT)
            da_s[...] = jnp.where(keep, da_s[...], 0.0)
            x = dsn * s_all * e_col
            dbl_rows.append(_dotf(jnp.ones((8, GLA_DV), F32), x, _NT)[0:1, :] + jnp.sum(dkd_s[...] * kd, axis=0, keepdims=True))

        def cols(d, jj, carry):
            _, b_s, q_s, k_s, da_s, _, _, dk3_s, dbn_s, _ = d[3]
            dq3, db3 = list(carry[:_NSUB]), list(carry[_NSUB:])
            sel = [jnp.where(lane == blk0 + (jj * _COLS + u), da_s[...], 0.0).astype(BF16) for u in range(_COLS)]
            dcols = jnp.dot(jnp.concatenate(sel, axis=0), oseg_ref[...], preferred_element_type=F32)
            for u in range(_COLS):
                jp = jj * _COLS + u
                for s in range(_NSUB):
                    rs_ = slice(s * _SUB, (s + 1) * _SUB)
                    bj = b_s[pl.ds(s * _SUB + jp, 1), :]
                    kj = k_s[pl.ds(s * _SUB + jp, 1), :]
                    tm_ = dcols[u * CHUNK + s * _SUB:u * CHUNK + (s + 1) * _SUB, :] * jnp.exp(jnp.minimum(b_s[rs_, :] - bj, 0.0))
                    dq3[s] = dq3[s] + tm_ * kj
                    gq = tm_ * q_s[rs_, :]
                    dk3_s[pl.ds(s * _SUB + jp, 1), :] = jnp.sum(gq, axis=0, keepdims=True)
                    w = gq * kj
                    dbn_s[pl.ds(s * _SUB + jp, 1), :] = jnp.sum(w, axis=0, keepdims=True)
                    db3[s] = db3[s] + w
            return tuple(dq3) + tuple(db3)

        zero = jnp.zeros((_SUB, _HK), F32)
        init = (zero,) * (2 * _NSUB)
        accs = lax.fori_loop(0, _SUB // _COLS, lambda jj, c: tuple(cols(d, jj, a) for d, a in zip(dirs, c)), (init, init))

        for (reverse, ins, (dq_ref, dk_ref, dv_ref, db_ref), (dst, b_s, q_s, k_s, da_s, dqb_s, dkd_s, dk3_s, dbn_s, dsp_s)), \
                acc, dbl_row in zip(dirs, accs, dbl_rows):
            last = 0 if reverse else CHUNK - 1
            eb, ekd, qb, kd = factors(reverse, ins[3], b_s, q_s, k_s)
            dqb = dqb_s[...]
            dkd = dkd_s[...]
            dq3 = jnp.concatenate(acc[:_NSUB], axis=0)
            db3 = jnp.concatenate(acc[_NSUB:], axis=0)
            dq_x, db_x = [], []
            dk_x = jnp.zeros((CHUNK, _HK), F32)
            db_k = jnp.zeros((CHUNK, _HK), F32)
            for s in range(_NSUB):
                if s not in _gla_cross_blocks(reverse):
                    dq_x.append(zero)
                    db_x.append(zero)
                    continue
                r0 = s * _SUB
                qt, eq, kt, ek, nmat = _gla_cross_terms(s, reverse, b_s, q_s, k_s, oseg_ref)
                seen = (lane >= r0 + _SUB) if reverse else (lane < r0)
                dax = jnp.where(seen, da_s[r0:r0 + _SUB, :], 0.0).astype(BF16)
                dqt = jnp.dot(dax, nmat, preferred_element_type=F32)
                full = lax.dot_general(dax, qt.astype(BF16), _TN, preferred_element_type=F32)
                dkt = full[0:CHUNK, :]
                for h in range(1, GLA_HEADS):
                    dkt = jnp.where(head == h, full[h * CHUNK:(h + 1) * CHUNK, :], dkt)
                dq_x.append(dqt * eq)
                db_x.append(dqt * qt)
                dk_x = dk_x + dkt * ek
                db_k = db_k + dkt * kt
            dq_ref[...] = (dqb * eb + dq3 + jnp.concatenate(dq_x, axis=0)) * (GLA_DK ** -0.5)
            dk_ref[...] = dkd * ekd + dk3_s[...] + dk_x
            db = dqb * qb - dkd * kd + db3 - dbn_s[...] + jnp.concatenate(db_x, axis=0) - db_k
            db_ref[...] = jnp.where(rowi == last, db + dbl_row, db)
            dst[...] = dsp_s[...]

    def specs(reverse):
        rb = (lambda e, c: e * n + c) if reverse else (lambda e, c: e * n + (n - 1 - c))
        hk = pl.BlockSpec((CHUNK, _HK), lambda e, c: (rb(e, c), 0))
        hv = pl.BlockSpec((CHUNK, _HV), lambda e, c: (rb(e, c), 0))
        stb = pl.BlockSpec((1, _HK, GLA_DV), lambda e, c: (rb(e, c), 0, 0))
        ins = [pl.BlockSpec((CHUNK, _HK), lambda e, c: (rb(e, c), C_Q // _HK)),
               pl.BlockSpec((CHUNK, _HK), lambda e, c: (rb(e, c), C_K // _HK)),
               pl.BlockSpec((CHUNK, _HV), lambda e, c: (rb(e, c), C_V // _HV)),
               hk, stb, hk, stb, hv]
        return ins, [hk, hk, hv, hk]

    in_f, out_f = specs(False)
    in_b, out_b = specs(True)
    shapes = (jax.ShapeDtypeStruct((T, _HK), F32), jax.ShapeDtypeStruct((T, _HK), F32),
              jax.ShapeDtypeStruct((T, _HV), F32), jax.ShapeDtypeStruct((T, _HK), F32))
    scratch = [pltpu.VMEM((_HK, GLA_DV), F32)] + [pltpu.VMEM((CHUNK, _HK), F32)] * 8 + [pltpu.VMEM((_HK, GLA_DV), F32)]
    res = pl.pallas_call(
        body, out_shape=shapes + shapes, grid=(E, n),
        in_specs=in_f + in_b + [pl.BlockSpec((_HK, _HK), lambda e, c: (0, 0))], out_specs=tuple(out_f + out_b),
        scratch_shapes=scratch * 2, name=name, compiler_params=_cp(("parallel", "arbitrary")))(
            z, z, z, *fwd_saved, do, z, z, z, *rev_saved, do, oseg)
    return res[:4], res[4:]


def _gla_norm_fwd(of, ob, og, name):
    T = of.shape[0]
    tm = 256

    def body(f_ref, b_ref, g_ref, o_ref):
        for h in range(GLA_HEADS):
            vs_ = slice(h * GLA_DV, (h + 1) * GLA_DV)
            o = f_ref[:, vs_] + b_ref[:, vs_]
            o_ref[:, vs_] = o * lax.rsqrt(jnp.mean(o * o, axis=-1, keepdims=True) + EPS) * g_ref[:, vs_]

    row = pl.BlockSpec((tm, _HV), lambda i: (i, 0))
    vec = pl.BlockSpec((1, _HV), lambda i: (0, 0))
    return pl.pallas_call(body, out_shape=jax.ShapeDtypeStruct((T, _HV), F32), grid=(T // tm,),
                          in_specs=[row, row, vec], out_specs=row, name=name, compiler_params=_cp(("parallel",)))(of, ob, og)


def _gla_norm_bwd(of, ob, og, dpre, name):
    T = of.shape[0]
    tm = 256

    def body(f_ref, b_ref, g_ref, dp_ref, do_ref, dg_ref):
        @pl.when(pl.program_id(0) == 0)
        def _():
            dg_ref[...] = jnp.zeros_like(dg_ref)

        for h in range(GLA_HEADS):
            vs_ = slice(h * GLA_DV, (h + 1) * GLA_DV)
            o = f_ref[:, vs_] + b_ref[:, vs_]
            r = lax.rsqrt(jnp.mean(o * o, axis=-1, keepdims=True) + EPS)
            xh = o * r
            dp = dp_ref[:, vs_]
            dxh = dp * g_ref[:, vs_]
            do_ref[:, vs_] = r * (dxh - xh * jnp.mean(dxh * xh, axis=-1, keepdims=True))
            dg_ref[:, vs_] += jnp.sum(dp * xh, axis=0, keepdims=True)

    row = pl.BlockSpec((tm, _HV), lambda i: (i, 0))
    vec = pl.BlockSpec((1, _HV), lambda i: (0, 0))
    return pl.pallas_call(
        body, out_shape=(jax.ShapeDtypeStruct((T, _HV), F32), jax.ShapeDtypeStruct((1, _HV), F32)), grid=(T // tm,),
        in_specs=[row, row, vec, row], out_specs=(row, vec), name=name, compiler_params=_cp(("arbitrary",)))(of, ob, og, dpre)


_ANY = pl.BlockSpec(memory_space=pl.ANY)


def _coords():
    return lax.axis_index("x"), lax.axis_index("y"), lax.axis_index("c")


def _other_chips(x, y):
    return ((1 - x, y), (x, 1 - y), (1 - x, 1 - y))


def _gather_weights(arrays, chunks, name):
    n = len(arrays)
    pieces = []
    for k in range(max(chunks)):
        for i, a in enumerate(arrays):
            if k < chunks[i]:
                rc = a.shape[1] // chunks[i]
                pieces.append((i, k * rc, rc))
    m = len(pieces)

    def body(*refs):
        srcs, dsts = refs[:n], refs[n:2 * n]
        send_sems, recv_sems, local_sems = refs[2 * n:]
        x, y, c = _coords()
        me = 2 * x + y
        loc = [pltpu.make_async_copy(s, d.at[me], local_sems.at[i]) for i, (s, d) in enumerate(zip(srcs, dsts))]
        for cp in loc:
            cp.start()
        ici = []
        for p, (i, r0, rc) in enumerate(pieces):
            for j, (px, py) in enumerate(_other_chips(x, y)):
                ici.append(pltpu.make_async_remote_copy(
                    src_ref=srcs[i].at[c, pl.ds(r0, rc)], dst_ref=dsts[i].at[me, c, pl.ds(r0, rc)],
                    send_sem=send_sems.at[3 * p + j], recv_sem=recv_sems.at[3 * p + j],
                    device_id=(px, py, c), device_id_type=MESH))
        for cp in ici:
            cp.start()
        fwd = []
        for p, (i, r0, rc) in enumerate(pieces):
            for j, (px, py) in enumerate(_other_chips(x, y)):
                ici[3 * p + j].wait_recv()
                part = dsts[i].at[2 * px + py, c, pl.ds(r0, rc)]
                cp = pltpu.make_async_remote_copy(
                    src_ref=part, dst_ref=part, send_sem=send_sems.at[3 * m + 3 * p + j], recv_sem=recv_sems.at[3 * m + 3 * p + j],
                    device_id=(x, y, 1 - c), device_id_type=MESH)
                cp.start()
                fwd.append(cp)
        for cp in fwd:
            cp.wait_recv()
        for cp in ici + fwd:
            cp.wait_send()
        for cp in loc:
            cp.wait()

    return pl.pallas_call(
        body, out_shape=tuple(jax.ShapeDtypeStruct((4,) + a.shape, a.dtype) for a in arrays),
        in_specs=[_ANY] * n, out_specs=(_ANY,) * n,
        scratch_shapes=[pltpu.SemaphoreType.DMA((6 * m,)), pltpu.SemaphoreType.DMA((6 * m,)), pltpu.SemaphoreType.DMA((n,))],
        name=name)(*arrays)


def _sibling_exchange(layered, whole, name):
    nl, n = len(layered), len(layered) + len(whole)

    def body(*refs):
        srcs, dsts = refs[:n], refs[n:2 * n]
        send_sems, recv_sems = refs[2 * n:]
        x, y, c = _coords()
        rem = [pltpu.make_async_remote_copy(src_ref=(s.at[1 - c] if i < nl else s), dst_ref=d, send_sem=send_sems.at[i],
                                            recv_sem=recv_sems.at[i], device_id=(x, y, 1 - c), device_id_type=MESH)
               for i, (s, d) in enumerate(zip(srcs, dsts))]
        for cp in rem:
            cp.start()
        for cp in rem:
            cp.wait()

    outs = [jax.ShapeDtypeStruct(a.shape[1:], a.dtype) for a in layered] + [jax.ShapeDtypeStruct(a.shape, a.dtype) for a in whole]
    return pl.pallas_call(
        body, out_shape=tuple(outs), in_specs=[_ANY] * n, out_specs=(_ANY,) * n,
        scratch_shapes=[pltpu.SemaphoreType.DMA((n,)), pltpu.SemaphoreType.DMA((n,))], name=name)(*layered, *whole)


def _chip_exchange(scatter, bcast, name):
    ns, n = len(scatter), len(scatter) + len(bcast)

    def body(*refs):
        srcs, dsts = refs[:n], refs[n:2 * n]
        send_sems, recv_sems, local_sems = refs[2 * n:]
        x, y, c = _coords()
        me = 2 * x + y
        loc = [pltpu.make_async_copy((s.at[me] if i < ns else s), d.at[me], local_sems.at[i])
               for i, (s, d) in enumerate(zip(srcs, dsts))]
        for cp in loc:
            cp.start()
        rem = []
        for j, (px, py) in enumerate(_other_chips(x, y)):
            for i, (s, d) in enumerate(zip(srcs, dsts)):
                rem.append(pltpu.make_async_remote_copy(
                    src_ref=(s.at[2 * px + py] if i < ns else s), dst_ref=d.at[me], send_sem=send_sems.at[n * j + i],
                    recv_sem=recv_sems.at[n * j + i], device_id=(px, py, c), device_id_type=MESH))
        for cp in rem:
            cp.start()
        for cp in rem:
            cp.wait()
        for cp in loc:
            cp.wait()

    outs = [jax.ShapeDtypeStruct(a.shape, a.dtype) for a in scatter] + [jax.ShapeDtypeStruct((4,) + a.shape, a.dtype) for a in bcast]
    return pl.pallas_call(
        body, out_shape=tuple(outs), in_specs=[_ANY] * n, out_specs=(_ANY,) * n,
        scratch_shapes=[pltpu.SemaphoreType.DMA((3 * n,)), pltpu.SemaphoreType.DMA((3 * n,)), pltpu.SemaphoreType.DMA((n,))],
        name=name)(*scatter, *bcast)


_EW_BLOCK_BYTES = 2 * 1024 * 1024


def _tile2d(R, C):
    if R % 256 == 0 and 256 * C * 4 <= _EW_BLOCK_BYTES:
        return 256, C
    bc = 256 if C % 256 == 0 else C
    for br in range(R, 0, -1):
        if R % br == 0 and (br % 8 == 0 or br == R) and br * bc * 4 <= _EW_BLOCK_BYTES:
            return br, bc
    return R, bc


def _sum_slots(r, name):
    n, R, C = r.shape
    br, bc = _tile2d(R, C)

    def body(r_ref, o_ref):
        acc = r_ref[0].astype(F32)
        for i in range(1, n):
            acc = acc + r_ref[i].astype(F32)
        o_ref[...] = acc

    return pl.pallas_call(body, out_shape=jax.ShapeDtypeStruct((R, C), F32), grid=(R // br, C // bc),
                          in_specs=[pl.BlockSpec((n, br, bc), lambda i, j: (0, i, j))],
                          out_specs=pl.BlockSpec((br, bc), lambda i, j: (i, j)),
                          name=name, compiler_params=_cp(("parallel", "parallel")))(r)


_SMEM = pl.BlockSpec(memory_space=pltpu.SMEM)


def _add2(a, b, out_dtype, name, pick=None):
    R, C = b.shape
    br, bc = _tile2d(R, C)
    blk = pl.BlockSpec((br, bc), lambda i, j: (i, j))
    if pick is None:
        def body(a_ref, b_ref, o_ref):
            o_ref[...] = (a_ref[...].astype(F32) + b_ref[...].astype(F32)).astype(out_dtype)
        in_specs, args = [blk, blk], (a, b)
    else:
        def body(c_ref, a_ref, b_ref, o_ref):
            av = jnp.where(c_ref[0] == 0, a_ref[0], a_ref[1])
            o_ref[...] = (av.astype(F32) + b_ref[...].astype(F32)).astype(out_dtype)
        in_specs = [_SMEM, pl.BlockSpec((2, br, bc), lambda i, j: (0, i, j)), blk]
        args = (pick.reshape(1).astype(jnp.int32), a, b)
    return pl.pallas_call(body, out_shape=jax.ShapeDtypeStruct((R, C), out_dtype), grid=(R // br, C // bc), in_specs=in_specs,
                          out_specs=blk, name=name, compiler_params=_cp(("parallel", "parallel")))(*args)


def _adamw_math(w, g, m, v):
    m = ADAM_B1 * m + (1.0 - ADAM_B1) * g
    v = ADAM_B2 * v + (1.0 - ADAM_B2) * (g * g)
    m_hat = m / (1.0 - ADAM_B1 ** ADAM_STEP)
    v_hat = v / (1.0 - ADAM_B2 ** ADAM_STEP)
    delta = -ADAM_LR * (m_hat / (jnp.sqrt(v_hat) + ADAM_EPS) + ADAM_WD * w)
    return delta, m, v


def _adamw(w, gs, m, v, name, pick=None):
    R, C = w.shape
    br, bc = _tile2d(R, C)
    blk = pl.BlockSpec((br, bc), lambda i, j: (i, j))
    if pick is None:
        g_specs = [pl.BlockSpec((g.shape[0], br, bc), lambda i, j: (0, i, j)) if g.ndim == 3 else blk for g in gs]
        lead = ()
    else:
        nb = (R // 2) // br
        assert nb * br * 2 == R
        g_specs = [pl.BlockSpec((br, bc), lambda i, j: (i % nb, j))] * 2
        lead = (pick.reshape(1).astype(jnp.int32),)

    def body(*refs):
        if pick is not None:
            c_ref, refs = refs[0], refs[1:]
        w_ref = refs[0]
        g_refs = refs[1:1 + len(gs)]
        m_ref, v_ref, g_out, d_out, m_out, v_out = refs[1 + len(gs):]
        if pick is None:
            g = None
            for gr in g_refs:
                parts = [gr[i] for i in range(gr.shape[0])] if len(gr.shape) == 3 else [gr[...]]
                for p in parts:
                    g = p if g is None else g + p
        else:
            g = jnp.where(pl.program_id(0) // nb == c_ref[0], g_refs[0][...], g_refs[1][...])
        d, mn, vn = _adamw_math(w_ref[...], g, m_ref[...], v_ref[...])
        g_out[...] = g
        d_out[...] = d
        m_out[...] = mn
        v_out[...] = vn

    return pl.pallas_call(
        body, out_shape=tuple(jax.ShapeDtypeStruct((R, C), F32) for _ in range(4)), grid=(R // br, C // bc),
        in_specs=[_SMEM] * len(lead) + [blk] + g_specs + [blk, blk], out_specs=(blk,) * 4, name=name,
        compiler_params=_cp(("parallel", "parallel")))(*lead, w, *gs, m, v)


WEIGHTS = ("norm_g", "w_in", "conv_w", "conv_b", "conv_ln_g", "conv_ln_b", "na_q_g", "na_k_g", "na_rpb", "gla_a2_f",
           "gla_ab_f", "gla_a2_b", "gla_ab_b", "gla_o_g", "pool_w", "pool_scale", "w_out")
_REPL = ("norm_g", "conv_b", "conv_ln_g", "conv_ln_b", "na_q_g", "na_k_g", "na_rpb", "gla_ab_f", "gla_ab_b", "gla_o_g",
         "pool_w", "pool_scale")
_SHARD_SMALL = ("conv_w", "gla_a2_f", "gla_a2_b")
_PACK_ROWS = 8 * 128


def _pack(arrs):
    flat = jnp.concatenate([a.reshape(-1) for a in arrs])
    n = -(-flat.shape[0] // _PACK_ROWS) * _PACK_ROWS
    return jnp.pad(flat, (0, n - flat.shape[0])).reshape(-1, 128)


def _unpack(p, shapes):
    flat = p.reshape(-1)
    out, o = [], 0
    for s in shapes:
        n = int(np.prod(s))
        out.append(flat[o:o + n].reshape(s))
        o += n
    return out


def _to_layout_rows(w):
    pad = jnp.zeros((w.shape[0], NZ - N_IN, w.shape[2]), w.dtype)
    return jnp.concatenate([w[:, :5120], w[:, 5152:6176], w[:, 5120:5152], pad], axis=1)


def _from_layout_rows(w):
    return jnp.concatenate([w[:, :5120], w[:, LR_OFF:LR_OFF + 32], w[:, 5120:LR_OFF]], axis=1)


def _reduce_gradients(p_a, p_b, small_g, ci):
    s_a, s_b, s_small = _sibling_exchange((p_a, p_b), (small_g,), "grad_to_sibling")
    flat = lambda a: a.reshape(a.shape[0], -1, a.shape[-1])
    c_a = _add2(flat(p_a), s_a.reshape(-1, s_a.shape[-1]), BF16, "chip_sum_a", pick=ci).reshape(s_a.shape)
    c_b = _add2(flat(p_b), s_b.reshape(-1, s_b.shape[-1]), BF16, "chip_sum_b", pick=ci).reshape(s_b.shape)
    c_small = _add2(small_g, s_small, F32, "chip_sum_small")
    r_a, r_b, r_small = _chip_exchange((c_a, c_b), (c_small,), "grad_to_owner")
    own_a = _sum_slots(r_a, "sum_a")
    own_b = _sum_slots(r_b, "sum_b")
    sib_a, sib_b = _sibling_exchange((), (own_a, own_b), "reduced_to_sibling")
    return (own_a, sib_a), (own_b, sib_b), r_small


def _layer_fwd(l, x, P, S, target=None):
    n = f"l{l}_"
    h = _rmsnorm_fwd(x, P["norm_g"], n + "rms_fwd")
    z = _matmul(h, P["w_in"], dims="nn", out_dtype=F32, tm=1024, tn=1280, tk=D_MODEL, name=n + "mm_z")
    yc = _conv_fwd(z, P["conv_w32"], P["conv_b"], S, n + "conv_fwd")
    pre_a = _ln_silu_fwd(yc, P["conv_ln_g"], P["conv_ln_b"], n + "ln_fwd")
    pre_b = _na_fwd(z, P["na_q_g"], P["na_k_g"], P["na_bias"], S, n + "na_fwd")
    bf, ecf = _gla_decay_fwd(z, P["a2_f"], P["gla_ab_f"], False, n + "gla_decay_f")
    bb, ecb = _gla_decay_fwd(z, P["a2_b"], P["gla_ab_b"], True, n + "gla_decay_b")
    (of, af, sf), (ob, ab, sb) = _gla_fwd_both(z, bf, ecf, bb, ecb, S, n + "gla_fwd")
    pre_c = _gla_norm_fwd(of, ob, P["gla_o_g"], n + "gla_norm_fwd")
    pre_d = _pool_fwd(z, P["pool_w_bf"], P["pool_scale"], S, n + "pool_fwd")
    pres = (pre_a, pre_b, pre_c, pre_d)
    res = _out_proj_fwd(pres, z, P["w_out"], x, target, n + "out_proj")
    y, out = res[0], (res[1] if target is None else res[1:])
    return out, dict(x=x, h=h, z=z, yc=yc, pres=pres, of=of, af=af, sf=sf, ob=ob, ab=ab, sb=sb, y=y, bf=bf, ecf=ecf, bb=bb, ecb=ecb)


def _layer_bwd(l, dout, dout_bf, sv, P, S):
    n = f"l{l}_"
    z = sv["z"]
    T = z.shape[0]
    d_w_out = _matmul(sv["y"], dout_bf, dims="tn", out_dtype=BF16, tm=1024, tn=2048, tk=512, name=n + "mm_dwout")
    dpa, dpb, dpc, dpd, dga, dgb, dgc, dgd = _out_proj_bwd(dout_bf, P["w_out_t"], sv["pres"], z, n + "out_proj_bwd")
    dyc, d_ln_g, d_ln_b = _ln_silu_bwd(sv["yc"], P["conv_ln_g"], P["conv_ln_b"], dpa, n + "ln_bwd")
    dval, dglu, d_cw, d_cb = _conv_bwd(z, P["conv_w32"], dyc, S, n + "conv_bwd")
    dq, dk, dv, dbias, d_qg, d_kg = _na_bwd(z, P["na_q_g"], P["na_k_g"], P["na_bias"], dpb, S, n + "na_bwd")
    d_rpb = _na_rpb_grad(dbias, n + "na_rpb")
    do, d_og = _gla_norm_bwd(sv["of"], sv["ob"], P["gla_o_g"], dpc, n + "gla_norm_bwd")
    gf, gb_ = _gla_bwd_both(z, (sv["bf"], sv["ecf"], sv["af"], sv["sf"]), (sv["bb"], sv["ecb"], sv["ab"], sv["sb"]), do, S,
                            n + "gla_bwd")
    dcq, dck, dcv = (_add2(a, b, BF16, n + "gla_d" + t) for a, b, t in zip(gf[:3], gb_[:3], "qkv"))
    dlr, d_a2f, d_abf, d_a2b, d_abb = _gla_decay_bwd(z, P["a2_f"], P["gla_ab_f"], P["a2_b"], P["gla_ab_b"], gf[3], gb_[3],
                                                     n + "gla_decay_bwd")
    dd, d_pw, d_ps = _pool_bwd(z, P["pool_w_bf"], P["pool_scale"], dpd, S, n + "pool_bwd")
    dz = _concat_cols([dval, dglu, dga, dq, dk, dv, dgb, dcq, dck, dcv, dgc, dd, dgd, dlr], NZ, n + "dz_concat")
    dh = _matmul(dz, P["w_in_t"], dims="nn", out_dtype=F32, tm=1024, tn=1024, tk=3200, name=n + "mm_dh")
    d_w_in = _matmul(dz, sv["h"], dims="tn", out_dtype=BF16, tm=1280, tn=1024, tk=1024, name=n + "mm_dwin")
    dx, dx_bf, d_ng = _rmsnorm_bwd(sv["x"], P["norm_g"], dh, dout, n + "rms_bwd")
    grads = dict(norm_g=d_ng[0], w_in=d_w_in, conv_w=d_cw[:CONV_K], conv_b=d_cb[0], conv_ln_g=d_ln_g[0], conv_ln_b=d_ln_b[0],
                 na_q_g=d_qg.reshape(NA_HEADS, NA_DH), na_k_g=d_kg.reshape(NA_HEADS, NA_DH), na_rpb=d_rpb,
                 gla_a2_f=d_a2f[0:GLA_RANK], gla_ab_f=d_abf[0], gla_a2_b=d_a2b[GLA_RANK:2 * GLA_RANK], gla_ab_b=d_abb[0],
                 gla_o_g=d_og.reshape(GLA_HEADS, GLA_DV), pool_w=d_pw, pool_scale=d_ps[0], w_out=d_w_out)
    return dx, dx_bf, grads


def kernel(x, norm_g, w_in, conv_w, conv_b, conv_ln_g, conv_ln_b, na_q_g, na_k_g, na_rpb, gla_a2_f, gla_ab_f, gla_a2_b, gla_ab_b, gla_o_g, pool_w, pool_scale, w_out, loss_target, m_norm_g, m_w_in, m_conv_w, m_conv_b, m_conv_ln_g, m_conv_ln_b, m_na_q_g, m_na_k_g, m_na_rpb, m_gla_a2_f, m_gla_ab_f, m_gla_a2_b, m_gla_ab_b, m_gla_o_g, m_pool_w, m_pool_scale, m_w_out, v_norm_g, v_w_in, v_conv_w, v_conv_b, v_conv_ln_g, v_conv_ln_b, v_na_q_g, v_na_k_g, v_na_rpb, v_gla_a2_f, v_gla_ab_f, v_gla_a2_b, v_gla_ab_b, v_gla_o_g, v_pool_w, v_pool_scale, v_w_out):
    W = dict(norm_g=norm_g, w_in=w_in, conv_w=conv_w, conv_b=conv_b, conv_ln_g=conv_ln_g, conv_ln_b=conv_ln_b, na_q_g=na_q_g,
             na_k_g=na_k_g, na_rpb=na_rpb, gla_a2_f=gla_a2_f, gla_ab_f=gla_ab_f, gla_a2_b=gla_a2_b, gla_ab_b=gla_ab_b,
             gla_o_g=gla_o_g, pool_w=pool_w, pool_scale=pool_scale, w_out=w_out)
    M = dict(norm_g=m_norm_g, w_in=m_w_in, conv_w=m_conv_w, conv_b=m_conv_b, conv_ln_g=m_conv_ln_g, conv_ln_b=m_conv_ln_b,
             na_q_g=m_na_q_g, na_k_g=m_na_k_g, na_rpb=m_na_rpb, gla_a2_f=m_gla_a2_f, gla_ab_f=m_gla_ab_f, gla_a2_b=m_gla_a2_b,
             gla_ab_b=m_gla_ab_b, gla_o_g=m_gla_o_g, pool_w=m_pool_w, pool_scale=m_pool_scale, w_out=m_w_out)
    V = dict(norm_g=v_norm_g, w_in=v_w_in, conv_w=v_conv_w, conv_b=v_conv_b, conv_ln_g=v_conv_ln_g, conv_ln_b=v_conv_ln_b,
             na_q_g=v_na_q_g, na_k_g=v_na_k_g, na_rpb=v_na_rpb, gla_a2_f=v_gla_a2_f, gla_ab_f=v_gla_ab_f, gla_a2_b=v_gla_a2_b,
             gla_ab_b=v_gla_ab_b, gla_o_g=v_gla_o_g, pool_w=v_pool_w, pool_scale=v_pool_scale, w_out=v_w_out)
    E, S, D = x.shape
    T = E * S
    L = DEPTH
    xi, yi, ci = _coords()
    chip = 2 * xi + yi
    cw_sh, a2_sh = conv_w.shape[-1], gla_a2_f.shape[-1]

    small_sh = jnp.concatenate([
        jnp.pad(conv_w, ((0, 0), (0, 1), (0, 0))),
        jnp.pad(gla_a2_f, ((0, 0), (0, 0), (0, 128 - a2_sh))),
        jnp.pad(gla_a2_b, ((0, 0), (0, 0), (0, 128 - a2_sh)))], axis=1)
    w_in_tr, m_w_in_tr, v_w_in_tr = (jnp.transpose(a, (0, 2, 1)) for a in (w_in, m_w_in, v_w_in))
    g_win, g_wout, g_small = _gather_weights((w_in_tr.astype(BF16), w_out.astype(BF16), small_sh), (1, 2, 1), "gather_weights")
    w_in_t_full = _to_layout_rows(jnp.transpose(g_win, (1, 0, 2, 3)).reshape(L, N_IN, D))
    w_in_full = jnp.transpose(w_in_t_full, (0, 2, 1))
    w_out_full = jnp.transpose(g_wout, (1, 0, 2, 3)).reshape(L, D, D)
    conv_w_full = jnp.transpose(g_small[:, :, 0:32, :], (1, 2, 0, 3)).reshape(L, 32, 4 * cw_sh)
    a2f_full = jnp.transpose(g_small[:, :, 32:48, :a2_sh], (1, 2, 0, 3)).reshape(L, GLA_RANK, 4 * a2_sh)
    a2b_full = jnp.transpose(g_small[:, :, 48:64, :a2_sh], (1, 2, 0, 3)).reshape(L, GLA_RANK, 4 * a2_sh)

    params = []
    for l in range(L):
        params.append(dict(
            norm_g=norm_g[l][None], w_in=w_in_full[l], w_out=w_out_full[l], w_in_t=w_in_t_full[l], w_out_t=w_out_full[l].T,
            conv_w32=conv_w_full[l], conv_b=conv_b[l][None],
            conv_ln_g=conv_ln_g[l][None], conv_ln_b=conv_ln_b[l][None], na_q_g=na_q_g[l].reshape(1, GROUP_W),
            na_k_g=na_k_g[l].reshape(1, GROUP_W), na_bias=_na_bias(na_rpb[l], f"l{l}_na_bias"),
            a2_f=jnp.zeros((128, _HK), F32).at[0:GLA_RANK].set(a2f_full[l]),
            a2_b=jnp.zeros((128, _HK), F32).at[GLA_RANK:2 * GLA_RANK].set(a2b_full[l]),
            gla_ab_f=gla_ab_f[l][None], gla_ab_b=gla_ab_b[l][None], gla_o_g=gla_o_g[l].reshape(1, GROUP_W),
            pool_w_bf=pool_w[l].astype(BF16), pool_scale=pool_scale[l][None]))

    act = x.reshape(T, D)
    saved = []
    for l in range(L):
        act, sv = _layer_fwd(l, act, params[l], S, loss_target.reshape(T, D) if l == L - 1 else None)
        saved.append(sv)
    dact, dact_bf, loss_loc = act
    loss = lax.psum(loss_loc[0, 0], ("x", "y", "c"))
    grads = [None] * L
    for l in reversed(range(L)):
        dact, dact_bf, grads[l] = _layer_bwd(l, dact, dact_bf, saved[l], params[l], S)
    grad_x = dact.reshape(E, S, D)
    G = {k: jnp.stack([grads[l][k] for l in range(L)]) for k in WEIGHTS}

    cols_in, cols_out = N_IN // 4, D
    p_win = _from_layout_rows(G["w_in"]).reshape(L, 4, cols_in, D)
    p_wout = G["w_out"].reshape(L, 4, D // 4, D)
    small_names = _REPL + _SHARD_SMALL
    small_g = _pack([G[k] for k in small_names])
    g_in, g_out, r_small = _reduce_gradients(p_win, p_wout, small_g, ci)

    rows_in, rows_out = L * cols_in, L * (D // 4)
    res = {}
    res["w_in"] = [jnp.transpose(a.reshape(L, cols_in, D), (0, 2, 1)) for a in _adamw(
        w_in_tr.reshape(rows_in, D), g_in, m_w_in_tr.reshape(rows_in, D), v_w_in_tr.reshape(rows_in, D), "adamw_w_in", pick=ci)]
    res["w_out"] = [a.reshape(L, D // 4, D) for a in _adamw(
        w_out.reshape(rows_out, cols_out), g_out, m_w_out.reshape(rows_out, cols_out),
        v_w_out.reshape(rows_out, cols_out), "adamw_w_out", pick=ci)]
    zeros_sh = [jnp.zeros(G[k].shape, F32) for k in _SHARD_SMALL]
    pk = lambda dct: _pack([dct[k] for k in _REPL] + zeros_sh)
    small_res = _adamw(pk(W), (r_small,), pk(M), pk(V), "adamw_small")
    shapes = [G[k].shape for k in small_names]
    unp = [_unpack(a, shapes) for a in small_res]
    for i, k in enumerate(_REPL):
        res[k] = [u[i] for u in unp]
    g_sh = []
    for i, k in enumerate(_SHARD_SMALL):
        gfull = unp[0][len(_REPL) + i]
        wdt = W[k].shape[-1]
        g_sh.append(lax.dynamic_slice_in_dim(gfull, chip * wdt, wdt, axis=2))
    g_sh_p = _pack(g_sh)
    sh_res = _adamw(_pack([W[k] for k in _SHARD_SMALL]), (g_sh_p,), _pack([M[k] for k in _SHARD_SMALL]),
                    _pack([V[k] for k in _SHARD_SMALL]), "adamw_shard_small")
    shapes2 = [W[k].shape for k in _SHARD_SMALL]
    unp2 = [_unpack(a, shapes2) for a in sh_res]
    for i, k in enumerate(_SHARD_SMALL):
        res[k] = [u[i] for u in unp2]

    outs = [loss, grad_x]
    for j in range(4):
        outs += [res[k][j] for k in WEIGHTS]
    return tuple(outs)
```

```python
import functools

import numpy as np
import jax
import jax.numpy as jnp
from jax import lax
from jax.experimental import pallas as pl
from jax.experimental.pallas import tpu as pltpu

F32 = jnp.float32
BF16 = jnp.bfloat16
HI = lax.Precision.HIGHEST
MESH = pl.DeviceIdType.MESH

EPS = 1e-6
D_MODEL = 2048
GROUP_W = 512
SEQ = 2048
DEPTH = 2
N_IN = 6176
GRID_W = 64
CONV_K = 31
NA_HEADS = 8
NA_DH = 64
NA_ROWS = 8
NA_COLS = 16
GLA_HEADS = 4
GLA_DK = 64
GLA_DV = 128
GLA_RANK = 16
GLA_TAU = 16.0
CHUNK = 64
POOL_WINDOWS = (2, 4, 8, 16)
ADAM_LR, ADAM_B1, ADAM_B2, ADAM_EPS, ADAM_WD, ADAM_STEP = 0.001, 0.9, 0.999, 1e-08, 0.01, 10

A_VAL, A_GLU, A_GATE = 0, 512, 1024
B_Q, B_K, B_V, B_GATE = 1536, 2048, 2560, 3072
C_Q, C_K, C_V, C_GATE = 3584, 3840, 4096, 4608
D_VAL, D_GATE = 5120, 5632
LR_OFF = 6144
NZ = 6400
NEG = -1e30
VMEM_LIMIT = 56 * 1024 * 1024


def _cp(sem=None):
    return pltpu.CompilerParams(dimension_semantics=sem, vmem_limit_bytes=VMEM_LIMIT)


def _sigmoid(x):
    return 1.0 / (1.0 + jnp.exp(-x))


def _silu(x):
    return x * _sigmoid(x)


def _dsilu(x):
    s = _sigmoid(x)
    return s * (1.0 + x * (1.0 - s))


def _matmul(a, b, *, dims, out_dtype, tm, tn, tk, name, res=None):
    if dims == "nn":
        (M, K), N = a.shape, b.shape[1]
    elif dims == "nt":
        (M, K), N = a.shape, b.shape[0]
    else:
        (K, M), N = a.shape, b.shape[1]
    tm, tn, tk = min(tm, M), min(tn, N), min(tk, K)
    nk = K // tk
    assert M % tm == 0 and N % tn == 0 and K % tk == 0, (M, N, K, tm, tn, tk)
    dn = {"nn": (((1,), (0,)), ((), ())), "nt": (((1,), (1,)), ((), ())), "tn": (((0,), (0,)), ((), ()))}[dims]
    if dims == "tn":
        a_spec = pl.BlockSpec((tk, tm), lambda i, j, k: (k, i))
    else:
        a_spec = pl.BlockSpec((tm, tk), lambda i, j, k: (i, k))
    if dims == "nt":
        b_spec = pl.BlockSpec((tn, tk), lambda i, j, k: (j, k))
    else:
        b_spec = pl.BlockSpec((tk, tn), lambda i, j, k: (k, j))
    o_spec = pl.BlockSpec((tm, tn), lambda i, j, k: (i, j))
    has_res = res is not None

    def body(*refs):
        if has_res:
            a_ref, b_ref, r_ref, o_ref, acc = refs
        else:
            a_ref, b_ref, o_ref, acc = refs
        k = pl.program_id(2)

        @pl.when(k == 0)
        def _():
            acc[...] = jnp.zeros_like(acc)

        acc[...] += lax.dot_general(a_ref[...], b_ref[...], dn, preferred_element_type=F32)

        @pl.when(k == nk - 1)
        def _():
            r = acc[...]
            if has_res:
                r = r + r_ref[...]
            o_ref[...] = r.astype(o_ref.dtype)

    in_specs = [a_spec, b_spec] + ([o_spec] if has_res else [])
    args = (a, b) + ((res,) if has_res else ())
    return pl.pallas_call(
        body, out_shape=jax.ShapeDtypeStruct((M, N), out_dtype), grid=(M // tm, N // tn, nk),
        in_specs=in_specs, out_specs=o_spec, scratch_shapes=[pltpu.VMEM((tm, tn), F32)],
        name=name, compiler_params=_cp(("parallel", "parallel", "arbitrary")))(*args)


def _concat_cols(pieces, width, name):
    pairs = [p if isinstance(p, tuple) else (p,) for p in pieces]
    T = pairs[0][0].shape[0]
    tm = min(512, T)
    dt = BF16
    offs = np.cumsum([0] + [p[0].shape[1] for p in pairs])
    flat = [a for p in pairs for a in p]

    def body(*refs):
        o_ref = refs[-1]
        k = 0
        for p, a, b in zip(pairs, offs[:-1], offs[1:]):
            val = refs[k][...] if len(p) == 1 else refs[k][...] + refs[k + 1][...]
            o_ref[:, a:b] = val.astype(dt)
            k += len(p)
        if offs[-1] < width:
            o_ref[:, offs[-1]:width] = jnp.zeros((tm, width - offs[-1]), dt)

    return pl.pallas_call(
        body, out_shape=jax.ShapeDtypeStruct((T, width), dt), grid=(T // tm,),
        in_specs=[pl.BlockSpec((tm, a.shape[1]), lambda i: (i, 0)) for a in flat],
        out_specs=pl.BlockSpec((tm, width), lambda i: (i, 0)), name=name, compiler_params=_cp(("parallel",)))(*flat)


def _rmsnorm_fwd(x, g, name):
    T, D = x.shape
    tm = 256

    def body(x_ref, g_ref, h_ref):
        xv = x_ref[...]
        r = lax.rsqrt(jnp.mean(xv * xv, axis=-1, keepdims=True) + EPS)
        h_ref[...] = (xv * r * g_ref[...]).astype(h_ref.dtype)

    return pl.pallas_call(
        body, out_shape=jax.ShapeDtypeStruct((T, D), BF16), grid=(T // tm,),
        in_specs=[pl.BlockSpec((tm, D), lambda i: (i, 0)), pl.BlockSpec((1, D), lambda i: (0, 0))],
        out_specs=pl.BlockSpec((tm, D), lambda i: (i, 0)), name=name, compiler_params=_cp(("parallel",)))(x, g)


def _rmsnorm_bwd(x, g, dh, dres, name):
    T, D = x.shape
    tm = 256

    def body(x_ref, g_ref, dh_ref, dres_ref, dx_ref, dxb_ref, dg_ref):
        xv = x_ref[...]
        r = lax.rsqrt(jnp.mean(xv * xv, axis=-1, keepdims=True) + EPS)
        xh = xv * r
        dh_v = dh_ref[...]
        dxh = dh_v * g_ref[...]
        dx = r * (dxh - xh * jnp.mean(dxh * xh, axis=-1, keepdims=True)) + dres_ref[...]
        dx_ref[...] = dx
        dxb_ref[...] = dx.astype(BF16)

        @pl.when(pl.program_id(0) == 0)
        def _():
            dg_ref[...] = jnp.zeros_like(dg_ref)

        dg_ref[...] += jnp.sum(dh_v * xh, axis=0, keepdims=True)

    row = pl.BlockSpec((tm, D), lambda i: (i, 0))
    vec = pl.BlockSpec((1, D), lambda i: (0, 0))
    return pl.pallas_call(
        body, out_shape=(jax.ShapeDtypeStruct((T, D), F32), jax.ShapeDtypeStruct((T, D), BF16), jax.ShapeDtypeStruct((1, D), F32)),
        grid=(T // tm,), in_specs=[row, vec, row, row], out_specs=(row, row, vec), name=name,
        compiler_params=_cp(("arbitrary",)))(x, g, dh, dres)


_GATE_COLS = (A_GATE // GROUP_W, B_GATE // GROUP_W, C_GATE // GROUP_W, D_GATE // GROUP_W)
_OP_TM = 256


def _out_proj_fwd(pres, z, w_out, x, target, name):
    T, D = x.shape
    tm = min(_OP_TM, T)
    with_loss = target is not None

    def body(*refs):
        pa, pb, pc, pd, ga, gb, gc, gd, w_ref, x_ref = refs[:10]
        refs = refs[10:]
        if with_loss:
            t_ref, y_ref, d_ref, db_ref, l_ref = refs
        else:
            y_ref, o_ref = refs
        for n_, (p, g) in enumerate(((pa, ga), (pb, gb), (pc, gc), (pd, gd))):
            y_ref[:, n_ * GROUP_W:(n_ + 1) * GROUP_W] = (p[...] * _silu(g[...])).astype(BF16)
        out = jnp.dot(y_ref[...], w_ref[...], preferred_element_type=F32) + x_ref[...]
        if with_loss:
            e = out - t_ref[...]
            d = e * (1.0 / D)
            d_ref[...] = d
            db_ref[...] = d.astype(BF16)

            @pl.when(pl.program_id(0) == 0)
            def _():
                l_ref[...] = jnp.zeros_like(l_ref)

            l_ref[...] += jnp.sum(jnp.sum(e * e, axis=-1, keepdims=True) * (0.5 / D), axis=0, keepdims=True)
        else:
            o_ref[...] = out

    pre_spec = pl.BlockSpec((tm, GROUP_W), lambda i: (i, 0))
    gate_specs = [pl.BlockSpec((tm, GROUP_W), functools.partial(lambda i, c: (i, c), c=c)) for c in _GATE_COLS]
    row = pl.BlockSpec((tm, D), lambda i: (i, 0))
    w_spec = pl.BlockSpec((4 * GROUP_W, D), lambda i: (0, 0))
    in_specs = [pre_spec] * 4 + gate_specs + [w_spec, row]
    args = list(pres) + [z, z, z, z, w_out, x]
    if with_loss:
        in_specs.append(row)
        args.append(target)
        out_shape = (jax.ShapeDtypeStruct((T, D), BF16), jax.ShapeDtypeStruct((T, D), F32), jax.ShapeDtypeStruct((T, D), BF16),
                     jax.ShapeDtypeStruct((1, 1), F32))
        out_specs = (row, row, row, pl.BlockSpec((1, 1), lambda i: (0, 0)))
    else:
        out_shape = (jax.ShapeDtypeStruct((T, D), BF16), jax.ShapeDtypeStruct((T, D), F32))
        out_specs = (row, row)
    return pl.pallas_call(body, out_shape=out_shape, grid=(T // tm,), in_specs=in_specs, out_specs=out_specs, name=name,
                          compiler_params=_cp(("arbitrary",)))(*args)


def _out_proj_bwd(dout_bf, w_out_t, pres, z, name):
    T, D = dout_bf.shape
    tm = min(_OP_TM, T)

    def body(do_ref, w_ref, pa, pb, pc, pd, ga, gb, gc, gd, dpa, dpb, dpc, dpd, dga, dgb, dgc, dgd):
        dy = jnp.dot(do_ref[...], w_ref[...], preferred_element_type=F32)
        for n_, (p, g, dp, dg) in enumerate(((pa, ga, dpa, dga), (pb, gb, dpb, dgb), (pc, gc, dpc, dgc), (pd, gd, dpd, dgd))):
            d = dy[:, n_ * GROUP_W:(n_ + 1) * GROUP_W]
            gv = g[...]
            dp[...] = d * _silu(gv)
            dg[...] = (d * p[...] * _dsilu(gv)).astype(BF16)

    pre_spec = pl.BlockSpec((tm, GROUP_W), lambda i: (i, 0))
    gate_specs = [pl.BlockSpec((tm, GROUP_W), functools.partial(lambda i, c: (i, c), c=c)) for c in _GATE_COLS]
    outs = tuple([jax.ShapeDtypeStruct((T, GROUP_W), F32)] * 4 + [jax.ShapeDtypeStruct((T, GROUP_W), BF16)] * 4)
    return pl.pallas_call(
        body, out_shape=outs, grid=(T // tm,),
        in_specs=[pl.BlockSpec((tm, D), lambda i: (i, 0)), pl.BlockSpec((D, 4 * GROUP_W), lambda i: (0, 0))] + [pre_spec] * 4 + gate_specs,
        out_specs=tuple([pre_spec] * 8), name=name, compiler_params=_cp(("parallel",)))(dout_bf, w_out_t, *pres, z, z, z, z)


_PAD = 16
_RC = 256


def _conv_fwd(z, conv_w32, conv_b, S, name):
    T = z.shape[0]
    E = T // S
    LW = 128

    def body(val_ref, glu_ref, w_ref, b_ref, y_ref, upad):
        upad[0:_PAD, :] = jnp.zeros((_PAD, LW), F32)
        upad[_PAD + S:_PAD + S + _PAD, :] = jnp.zeros((_PAD, LW), F32)
        upad[_PAD:_PAD + S, :] = val_ref[...] * _sigmoid(glu_ref[...])
        for r in range(S // _RC):
            acc = jnp.broadcast_to(b_ref[...], (_RC, LW))
            for k in range(CONV_K):
                st = r * _RC + k + 1
                acc = acc + upad[st:st + _RC, :] * w_ref[k:k + 1, :]
            y_ref[r * _RC:(r + 1) * _RC, :] = acc

    return pl.pallas_call(
        body, out_shape=jax.ShapeDtypeStruct((T, GROUP_W), F32), grid=(E, GROUP_W // LW),
        in_specs=[pl.BlockSpec((S, LW), lambda e, j: (e, A_VAL // LW + j)),
                  pl.BlockSpec((S, LW), lambda e, j: (e, A_GLU // LW + j)),
                  pl.BlockSpec((32, LW), lambda e, j: (0, j)),
                  pl.BlockSpec((1, LW), lambda e, j: (0, j))],
        out_specs=pl.BlockSpec((S, LW), lambda e, j: (e, j)),
        scratch_shapes=[pltpu.VMEM((S + 2 * _PAD, LW), F32)],
        name=name, compiler_params=_cp(("parallel", "parallel")))(z, z, conv_w32, conv_b)


def _conv_bwd(z, conv_w32, dyc, S, name):
    T = z.shape[0]
    E = T // S
    LW = 128

    def body(val_ref, glu_ref, w_ref, dy_ref, dval_ref, dglu_ref, dw_ref, db_ref, upad, dpad):
        e = pl.program_id(1)
        zeros = jnp.zeros((_PAD, LW), F32)
        upad[0:_PAD, :] = zeros
        upad[_PAD + S:_PAD + S + _PAD, :] = zeros
        dpad[0:_PAD, :] = zeros
        dpad[_PAD + S:_PAD + S + _PAD, :] = zeros
        upad[_PAD:_PAD + S, :] = val_ref[...] * _sigmoid(glu_ref[...])
        dpad[_PAD:_PAD + S, :] = dy_ref[...]

        @pl.when(e == 0)
        def _():
            dw_ref[...] = jnp.zeros_like(dw_ref)
            db_ref[...] = jnp.zeros_like(db_ref)

        db_ref[...] += jnp.sum(dy_ref[...], axis=0, keepdims=True)
        for r in range(S // _RC):
            dyr = dy_ref[r * _RC:(r + 1) * _RC, :]
            du = jnp.zeros((_RC, LW), F32)
            for k in range(CONV_K):
                st = r * _RC + k + 1
                dw_ref[k:k + 1, :] += jnp.sum(dyr * upad[st:st + _RC, :], axis=0, keepdims=True)
                sd = r * _RC + (CONV_K - 1 - k) + 1
                du = du + dpad[sd:sd + _RC, :] * w_ref[k:k + 1, :]
            sl = slice(r * _RC, (r + 1) * _RC)
            val = val_ref[sl, :]
            sg = _sigmoid(glu_ref[sl, :])
            dval_ref[sl, :] = (du * sg).astype(BF16)
            dglu_ref[sl, :] = (du * val * sg * (1.0 - sg)).astype(BF16)

    blk = pl.BlockSpec((S, LW), lambda j, e: (e, j))
    return pl.pallas_call(
        body, out_shape=(jax.ShapeDtypeStruct((T, GROUP_W), BF16), jax.ShapeDtypeStruct((T, GROUP_W), BF16),
                         jax.ShapeDtypeStruct((32, GROUP_W), F32), jax.ShapeDtypeStruct((1, GROUP_W), F32)),
        grid=(GROUP_W // LW, E),
        in_specs=[pl.BlockSpec((S, LW), lambda j, e: (e, A_VAL // LW + j)),
                  pl.BlockSpec((S, LW), lambda j, e: (e, A_GLU // LW + j)),
                  pl.BlockSpec((32, LW), lambda j, e: (0, j)), blk],
        out_specs=(blk, blk, pl.BlockSpec((32, LW), lambda j, e: (0, j)), pl.BlockSpec((1, LW), lambda j, e: (0, j))),
        scratch_shapes=[pltpu.VMEM((S + 2 * _PAD, LW), F32), pltpu.VMEM((S + 2 * _PAD, LW), F32)],
        name=name, compiler_params=_cp(("parallel", "arbitrary")))(z, z, conv_w32, dyc)


def _ln_silu_fwd(yc, g, b, name):
    T, C = yc.shape
    tm = 256

    def body(y_ref, g_ref, b_ref, o_ref):
        y = y_ref[...]
        mu = jnp.mean(y, axis=-1, keepdims=True)
        yc_ = y - mu
        r = lax.rsqrt(jnp.mean(yc_ * yc_, axis=-1, keepdims=True) + EPS)
        o_ref[...] = _silu(yc_ * r * g_ref[...] + b_ref[...])

    row = pl.BlockSpec((tm, C), lambda i: (i, 0))
    vec = pl.BlockSpec((1, C), lambda i: (0, 0))
    return pl.pallas_call(body, out_shape=jax.ShapeDtypeStruct((T, C), F32), grid=(T // tm,),
                          in_specs=[row, vec, vec], out_specs=row, name=name, compiler_params=_cp(("parallel",)))(yc, g, b)


def _ln_silu_bwd(yc, g, b, dpre, name):
    T, C = yc.shape
    tm = 256

    def body(y_ref, g_ref, b_ref, dp_ref, dy_ref, dg_ref, db_ref):
        y = y_ref[...]
        mu = jnp.mean(y, axis=-1, keepdims=True)
        yc_ = y - mu
        r = lax.rsqrt(jnp.mean(yc_ * yc_, axis=-1, keepdims=True) + EPS)
        xh = yc_ * r
        gv = g_ref[...]
        dln = dp_ref[...] * _dsilu(xh * gv + b_ref[...])
        dxh = dln * gv
        dy_ref[...] = r * (dxh - jnp.mean(dxh, axis=-1, keepdims=True) - xh * jnp.mean(dxh * xh, axis=-1, keepdims=True))

        @pl.when(pl.program_id(0) == 0)
        def _():
            dg_ref[...] = jnp.zeros_like(dg_ref)
            db_ref[...] = jnp.zeros_like(db_ref)

        dg_ref[...] += jnp.sum(dln * xh, axis=0, keepdims=True)
        db_ref[...] += jnp.sum(dln, axis=0, keepdims=True)

    row = pl.BlockSpec((tm, C), lambda i: (i, 0))
    vec = pl.BlockSpec((1, C), lambda i: (0, 0))
    return pl.pallas_call(
        body, out_shape=(jax.ShapeDtypeStruct((T, C), F32), jax.ShapeDtypeStruct((1, C), F32), jax.ShapeDtypeStruct((1, C), F32)),
        grid=(T // tm,), in_specs=[row, vec, vec, row], out_specs=(row, vec, vec), name=name,
        compiler_params=_cp(("arbitrary",)))(yc, g, b, dpre)


def _pool_counts(S, w, rows0, n):
    t = (lax.broadcasted_iota(jnp.int32, (n, 1), 0) + rows0)
    lo = jnp.maximum(t - w // 2, 0)
    hi = jnp.minimum(t + w // 2, S)
    return (hi - lo).astype(F32)


def _pool_fwd(z, pool_w, pool_scale, S, name):
    T = z.shape[0]
    E = T // S
    CG = 128

    def body(u_ref, w_ref, s_ref, o_ref, upad, dif):
        zeros = jnp.zeros((_PAD, GROUP_W), F32)
        upad[0:_PAD, :] = zeros
        upad[_PAD + S:_PAD + S + _PAD, :] = zeros
        upad[_PAD:_PAD + S, :] = u_ref[...]
        for gi, w in enumerate(POOL_WINDOWS):
            ls = slice(gi * CG, (gi + 1) * CG)
            for r in range(S // _RC):
                acc = jnp.zeros((_RC, CG), F32)
                for j in range(-(w // 2), w // 2):
                    st = _PAD + r * _RC + j
                    acc = acc + upad[st:st + _RC, ls]
                cnt = _pool_counts(S, w, r * _RC, _RC)
                dif[r * _RC:(r + 1) * _RC, :] = (acc / cnt - u_ref[r * _RC:(r + 1) * _RC, ls]).astype(BF16)
            yp = jnp.dot(dif[...], w_ref[gi], preferred_element_type=F32)
            o_ref[:, ls] = yp * s_ref[:, ls]

    return pl.pallas_call(
        body, out_shape=jax.ShapeDtypeStruct((T, GROUP_W), F32), grid=(E,),
        in_specs=[pl.BlockSpec((S, GROUP_W), lambda e: (e, D_VAL // GROUP_W)),
                  pl.BlockSpec((4, CG, CG), lambda e: (0, 0, 0)),
                  pl.BlockSpec((1, GROUP_W), lambda e: (0, 0))],
        out_specs=pl.BlockSpec((S, GROUP_W), lambda e: (e, 0)),
        scratch_shapes=[pltpu.VMEM((S + 2 * _PAD, GROUP_W), F32), pltpu.VMEM((S, CG), BF16)],
        name=name, compiler_params=_cp(("parallel",)))(z, pool_w, pool_scale)


def _pool_bwd(z, pool_w, pool_scale, dpre, S, name):
    T = z.shape[0]
    E = T // S
    CG = 128

    def body(u_ref, w_ref, s_ref, dp_ref, du_ref, dw_ref, ds_ref, upad, dif, qpad):
        zeros = jnp.zeros((_PAD, GROUP_W), F32)
        upad[0:_PAD, :] = zeros
        upad[_PAD + S:_PAD + S + _PAD, :] = zeros
        upad[_PAD:_PAD + S, :] = u_ref[...]
        zc = jnp.zeros((_PAD, CG), F32)
        qpad[0:_PAD, :] = zc
        qpad[_PAD + S:_PAD + S + _PAD, :] = zc

        @pl.when(pl.program_id(0) == 0)
        def _():
            dw_ref[...] = jnp.zeros_like(dw_ref)
            ds_ref[...] = jnp.zeros_like(ds_ref)

        for gi, w in enumerate(POOL_WINDOWS):
            ls = slice(gi * CG, (gi + 1) * CG)
            for r in range(S // _RC):
                acc = jnp.zeros((_RC, CG), F32)
                for j in range(-(w // 2), w // 2):
                    st = _PAD + r * _RC + j
                    acc = acc + upad[st:st + _RC, ls]
                cnt = _pool_counts(S, w, r * _RC, _RC)
                dif[r * _RC:(r + 1) * _RC, :] = (acc / cnt - u_ref[r * _RC:(r + 1) * _RC, ls]).astype(BF16)
            dp = dp_ref[:, ls]
            yp = jnp.dot(dif[...], w_ref[gi], preferred_element_type=F32)
            ds_ref[:, ls] += jnp.sum(dp * yp, axis=0, keepdims=True)
            dys = (dp * s_ref[:, ls]).astype(BF16)
            dw_ref[gi] += lax.dot_general(dif[...], dys, (((0,), (0,)), ((), ())), preferred_element_type=F32)
            dm = lax.dot_general(dys, w_ref[gi], (((1,), (1,)), ((), ())), preferred_element_type=F32)
            for r in range(S // _RC):
                cnt = _pool_counts(S, w, r * _RC, _RC)
                qpad[_PAD + r * _RC:_PAD + (r + 1) * _RC, :] = dm[r * _RC:(r + 1) * _RC, :] / cnt
            for r in range(S // _RC):
                acc = -dm[r * _RC:(r + 1) * _RC, :]
                for j in range(-(w // 2) + 1, w // 2 + 1):
                    st = _PAD + r * _RC + j
                    acc = acc + qpad[st:st + _RC, :]
                du_ref[r * _RC:(r + 1) * _RC, ls] = acc.astype(BF16)

    return pl.pallas_call(
        body, out_shape=(jax.ShapeDtypeStruct((T, GROUP_W), BF16), jax.ShapeDtypeStruct((4, CG, CG), F32),
                         jax.ShapeDtypeStruct((1, GROUP_W), F32)), grid=(E,),
        in_specs=[pl.BlockSpec((S, GROUP_W), lambda e: (e, D_VAL // GROUP_W)),
                  pl.BlockSpec((4, CG, CG), lambda e: (0, 0, 0)),
                  pl.BlockSpec((1, GROUP_W), lambda e: (0, 0)),
                  pl.BlockSpec((S, GROUP_W), lambda e: (e, 0))],
        out_specs=(pl.BlockSpec((S, GROUP_W), lambda e: (e, 0)), pl.BlockSpec((4, CG, CG), lambda e: (0, 0, 0)),
                   pl.BlockSpec((1, GROUP_W), lambda e: (0, 0))),
        scratch_shapes=[pltpu.VMEM((S + 2 * _PAD, GROUP_W), F32), pltpu.VMEM((S, CG), BF16),
                        pltpu.VMEM((S + 2 * _PAD, CG), F32)],
        name=name, compiler_params=_cp(("arbitrary",)))(z, pool_w, pool_scale, dpre)


def _na_tables():
    d = np.arange(NA_ROWS)[:, None]
    kr = np.arange(NA_ROWS)[None, :]
    ro = kr - d + (NA_ROWS - 1)
    qc = np.arange(GRID_W)[:, None]
    kc = np.arange(GRID_W)[None, :]
    cs = np.clip(qc - NA_COLS // 2, 0, GRID_W - NA_COLS)
    valid = (kc >= cs) & (kc < cs + NA_COLS)
    co = np.clip(kc - qc + (NA_COLS - 1), 0, 2 * NA_COLS - 2)
    return ro, co, valid


def _na_onehots():
    ro, co, valid = _na_tables()
    e_np = np.zeros((GRID_W, GRID_W, 128), np.float32)
    qi, ki = np.nonzero(valid)
    e_np[qi, ki, co[qi, ki]] = 1.0
    a_np = np.zeros((16, NA_ROWS * NA_ROWS), np.float32)
    a_np[ro.reshape(-1), np.arange(NA_ROWS * NA_ROWS)] = 1.0
    mask = np.where(valid, 0.0, NEG).astype(np.float32).reshape(1, GRID_W * GRID_W)
    return e_np.reshape(GRID_W * GRID_W, 128), a_np, mask


def _na_bias(rpb, name):
    e_np, _, mask = _na_onehots()
    H = NA_HEADS
    rp = jnp.pad(rpb, ((0, 0), (0, 1), (0, 128 - rpb.shape[2])))

    def bands(r_ref, e_ref, m_ref, o_ref):
        o_ref[0] = lax.dot_general(r_ref[0], e_ref[...], (((1,), (1,)), ((), ())), precision=HI,
                                   preferred_element_type=F32) + m_ref[...]

    t = pl.pallas_call(
        bands, out_shape=jax.ShapeDtypeStruct((H, 16, GRID_W * GRID_W), F32), grid=(H,),
        in_specs=[pl.BlockSpec((1, 16, 128), lambda h: (h, 0, 0)),
                  pl.BlockSpec((GRID_W * GRID_W, 128), lambda h: (0, 0)),
                  pl.BlockSpec((1, GRID_W * GRID_W), lambda h: (0, 0))],
        out_specs=pl.BlockSpec((1, 16, GRID_W * GRID_W), lambda h: (h, 0, 0)),
        name=name + "_bands", compiler_params=_cp(("parallel",)))(rp, jnp.asarray(e_np), jnp.asarray(mask))
    t = t.reshape(H, 16, GRID_W, GRID_W)

    def place(t_ref, o_ref):
        for d in range(NA_ROWS):
            for kr in range(NA_ROWS):
                o_ref[0, d, :, kr * GRID_W:(kr + 1) * GRID_W] = t_ref[0, kr - d + NA_ROWS - 1]

    return pl.pallas_call(
        place, out_shape=jax.ShapeDtypeStruct((H, NA_ROWS, GRID_W, NA_ROWS * GRID_W), F32), grid=(H,),
        in_specs=[pl.BlockSpec((1, 16, GRID_W, GRID_W), lambda h: (h, 0, 0, 0))],
        out_specs=pl.BlockSpec((1, NA_ROWS, GRID_W, NA_ROWS * GRID_W), lambda h: (h, 0, 0, 0)),
        name=name, compiler_params=_cp(("parallel",)))(t)


def _seg_mean_matrix(width, seg):
    i = np.arange(width)
    return jnp.asarray((i[:, None] // seg == i[None, :] // seg).astype(np.float32) / seg, BF16)


def _seg_mean(x, seg_ref):
    hi = x.astype(BF16)
    lo = (x - hi.astype(F32)).astype(BF16)
    return (jnp.dot(hi, seg_ref[...], preferred_element_type=F32) + jnp.dot(lo, seg_ref[...], preferred_element_type=F32))


def _na_fwd(z, qg, kg, bias, S, name):
    T = z.shape[0]
    E = T // S
    rows = S // GRID_W
    WIN = NA_ROWS * GRID_W
    seg = _seg_mean_matrix(128, NA_DH)

    def body(q_ref, k_ref, v_ref, qg_ref, kg_ref, bias_ref, seg_ref, o_ref, qs, ks, vs, s_all, p_all):
        for c in range(S // _RC):
            sl = slice(c * _RC, (c + 1) * _RC)
            q = q_ref[sl, :]
            k = k_ref[sl, :]
            qn = q * lax.rsqrt(_seg_mean(q * q, seg_ref) + EPS) * qg_ref[...]
            kn = k * lax.rsqrt(_seg_mean(k * k, seg_ref) + EPS) * kg_ref[...]
            v = v_ref[sl, :]
            for hh in range(2):
                ls = slice(hh * NA_DH, (hh + 1) * NA_DH)
                qs[hh, sl, :] = qn[:, ls].astype(BF16)
                ks[hh, sl, :] = kn[:, ls].astype(BF16)
                vs[hh, sl, :] = v[:, ls].astype(BF16)
        def where(r):
            rs = jnp.clip(r - NA_ROWS // 2, 0, rows - NA_ROWS)
            return rs, pl.multiple_of(r * GRID_W, GRID_W), pl.multiple_of(rs * GRID_W, GRID_W)

        def scores(r, carry):
            rs, q0, k0 = where(r)
            for hh in range(2):
                s = lax.dot_general(qs[hh, pl.ds(q0, GRID_W), :], ks[hh, pl.ds(k0, WIN), :], (((1,), (1,)), ((), ())),
                                    preferred_element_type=F32) * (NA_DH ** -0.5)
                s_all[hh, pl.ds(q0, GRID_W), :] = s + bias_ref[hh, r - rs]
            return carry
        lax.fori_loop(0, rows, scores, 0, unroll=4)

        def soft(r, carry):
            _, q0, _ = where(r)
            for hh in range(2):
                s = s_all[hh, pl.ds(q0, GRID_W), :]
                p = jnp.exp(s - jnp.max(s, axis=-1, keepdims=True))
                p_all[hh, pl.ds(q0, GRID_W), :] = (p * (1.0 / jnp.sum(p, axis=-1, keepdims=True))).astype(BF16)
            return carry
        lax.fori_loop(0, rows, soft, 0, unroll=2)

        def outp(r, carry):
            _, q0, k0 = where(r)
            outs = [jnp.dot(p_all[hh, pl.ds(q0, GRID_W), :], vs[hh, pl.ds(k0, WIN), :], preferred_element_type=F32)
                    for hh in range(2)]
            o_ref[pl.ds(q0, GRID_W), :] = jnp.concatenate(outs, axis=1)
            return carry
        lax.fori_loop(0, rows, outp, 0, unroll=4)

    LW = 128
    return pl.pallas_call(
        body, out_shape=jax.ShapeDtypeStruct((T, GROUP_W), F32), grid=(E, GROUP_W // LW),
        in_specs=[pl.BlockSpec((S, LW), lambda e, j: (e, B_Q // LW + j)),
                  pl.BlockSpec((S, LW), lambda e, j: (e, B_K // LW + j)),
                  pl.BlockSpec((S, LW), lambda e, j: (e, B_V // LW + j)),
                  pl.BlockSpec((1, LW), lambda e, j: (0, j)),
                  pl.BlockSpec((1, LW), lambda e, j: (0, j)),
                  pl.BlockSpec((2, NA_ROWS, GRID_W, WIN), lambda e, j: (j, 0, 0, 0)),
                  pl.BlockSpec((LW, LW), lambda e, j: (0, 0))],
        out_specs=pl.BlockSpec((S, LW), lambda e, j: (e, j)),
        scratch_shapes=[pltpu.VMEM((2, S, NA_DH), BF16)] * 3 + [pltpu.VMEM((2, S, WIN), F32), pltpu.VMEM((2, S, WIN), BF16)],
        name=name, compiler_params=_cp(("parallel", "parallel")))(z, z, z, qg, kg, bias, seg)


def _na_bwd(z, qg, kg, bias, do, S, name):
    T = z.shape[0]
    E = T // S
    rows = S // GRID_W
    WIN = NA_ROWS * GRID_W
    seg = _seg_mean_matrix(128, NA_DH)
    SC = NA_DH ** -0.5

    def body(q_ref, k_ref, v_ref, qg_ref, kg_ref, bias_ref, seg_ref, do_ref,
             dq_ref, dk_ref, dv_ref, dbias_ref, dqg_ref, dkg_ref, qs, ks, vs, dos, dqn, dkn, dvs, akt, avt,
             s_all, dp_all, p_all, ds_all):
        e = pl.program_id(1)

        @pl.when(e == 0)
        def _():
            dbias_ref[...] = jnp.zeros_like(dbias_ref)
            dqg_ref[...] = jnp.zeros_like(dqg_ref)
            dkg_ref[...] = jnp.zeros_like(dkg_ref)

        for c in range(S // _RC):
            sl = slice(c * _RC, (c + 1) * _RC)
            q = q_ref[sl, :]
            k = k_ref[sl, :]
            qn = q * lax.rsqrt(_seg_mean(q * q, seg_ref) + EPS) * qg_ref[...]
            kn = k * lax.rsqrt(_seg_mean(k * k, seg_ref) + EPS) * kg_ref[...]
            v = v_ref[sl, :]
            dd = do_ref[sl, :]
            for hh in range(2):
                ls = slice(hh * NA_DH, (hh + 1) * NA_DH)
                qs[hh, sl, :] = qn[:, ls].astype(BF16)
                ks[hh, sl, :] = kn[:, ls].astype(BF16)
                vs[hh, sl, :] = v[:, ls].astype(BF16)
                dos[hh, sl, :] = dd[:, ls].astype(BF16)
        akt[...] = jnp.zeros_like(akt)
        avt[...] = jnp.zeros_like(avt)

        def where(r):
            rs = jnp.clip(r - NA_ROWS // 2, 0, rows - NA_ROWS)
            return rs, pl.multiple_of(r * GRID_W, GRID_W), pl.multiple_of(rs * GRID_W, GRID_W)

        for hh in range(2):
            ls = slice(hh * NA_DH, (hh + 1) * NA_DH)

            def products(r, carry, hh=hh):
                rs, q0, k0 = where(r)
                s = lax.dot_general(qs[hh, pl.ds(q0, GRID_W), :], ks[hh, pl.ds(k0, WIN), :], (((1,), (1,)), ((), ())),
                                    preferred_element_type=F32) * SC
                s_all[pl.ds(q0, GRID_W), :] = s + bias_ref[hh, r - rs]
                dp_all[pl.ds(q0, GRID_W), :] = lax.dot_general(dos[hh, pl.ds(q0, GRID_W), :], vs[hh, pl.ds(k0, WIN), :],
                                                               (((1,), (1,)), ((), ())), preferred_element_type=F32)
                return carry
            lax.fori_loop(0, rows, products, 0, unroll=4)

            def soft(r, carry, hh=hh):
                rs, q0, _ = where(r)
                s = s_all[pl.ds(q0, GRID_W), :]
                p = jnp.exp(s - jnp.max(s, axis=-1, keepdims=True))
                p = p * (1.0 / jnp.sum(p, axis=-1, keepdims=True))
                dp = dp_all[pl.ds(q0, GRID_W), :]
                ds = p * (dp - jnp.sum(p * dp, axis=-1, keepdims=True))
                dbias_ref[hh, r - rs] += ds
                p_all[pl.ds(q0, GRID_W), :] = p.astype(BF16)
                ds_all[pl.ds(q0, GRID_W), :] = ds.astype(BF16)
                return carry
            lax.fori_loop(0, rows, soft, 0, unroll=2)

            def grads(r, carry, hh=hh, ls=ls):
                rs, q0, k0 = where(r)
                par = rs % 2
                t0 = (rs + par) // 2
                qr = qs[hh, pl.ds(q0, GRID_W), :]
                dor = dos[hh, pl.ds(q0, GRID_W), :]
                dsb = ds_all[pl.ds(q0, GRID_W), :]
                dqn[pl.ds(q0, GRID_W), ls] = jnp.dot(dsb, ks[hh, pl.ds(k0, WIN), :], preferred_element_type=F32) * SC
                dkt = lax.dot_general(qr, dsb, (((0,), (0,)), ((), ())), preferred_element_type=F32) * SC
                dvt = lax.dot_general(dor, p_all[pl.ds(q0, GRID_W), :], (((0,), (0,)), ((), ())), preferred_element_type=F32)
                akt[hh, par, pl.ds(t0, WIN // 128)] += jnp.stack([dkt[:, 128 * i:128 * (i + 1)] for i in range(WIN // 128)])
                avt[hh, par, pl.ds(t0, WIN // 128)] += jnp.stack([dvt[:, 128 * i:128 * (i + 1)] for i in range(WIN // 128)])
                return carry
            lax.fori_loop(0, rows, grads, 0, unroll=4)

        for hh in range(2):
            ls = slice(hh * NA_DH, (hh + 1) * NA_DH)
            for i in range(S // 128):
                for acc, dst in ((akt, dkn), (avt, dvs)):
                    odd = jnp.concatenate([acc[hh, 1, i][:, NA_DH:], acc[hh, 1, i + 1][:, :NA_DH]], axis=1)
                    dst[128 * i:128 * (i + 1), ls] = (acc[hh, 0, i] + odd).T

        for c in range(S // _RC):
            sl = slice(c * _RC, (c + 1) * _RC)
            for x_ref, g_ref, dn, dx_ref, dg_ref in ((q_ref, qg_ref, dqn, dq_ref, dqg_ref), (k_ref, kg_ref, dkn, dk_ref, dkg_ref)):
                x = x_ref[sl, :]
                r_ = lax.rsqrt(_seg_mean(x * x, seg_ref) + EPS)
                xh = x * r_
                d = dn[sl, :]
                dxh = d * g_ref[...]
                mean = _seg_mean(dxh * xh, seg_ref)
                dx_ref[sl, :] = (r_ * (dxh - xh * mean)).astype(BF16)
                dg_ref[...] += jnp.sum(d * xh, axis=0, keepdims=True)
            dv_ref[sl, :] = dvs[sl, :].astype(BF16)

    LW = 128
    blk = pl.BlockSpec((S, LW), lambda j, e: (e, j))
    vec = pl.BlockSpec((1, LW), lambda j, e: (0, j))
    bsp = pl.BlockSpec((2, NA_ROWS, GRID_W, WIN), lambda j, e: (j, 0, 0, 0))
    return pl.pallas_call(
        body, out_shape=(jax.ShapeDtypeStruct((T, GROUP_W), BF16),) * 3 + (
            jax.ShapeDtypeStruct((NA_HEADS, NA_ROWS, GRID_W, WIN), F32),
            jax.ShapeDtypeStruct((1, GROUP_W), F32), jax.ShapeDtypeStruct((1, GROUP_W), F32)),
        grid=(GROUP_W // LW, E),
        in_specs=[pl.BlockSpec((S, LW), lambda j, e: (e, B_Q // LW + j)),
                  pl.BlockSpec((S, LW), lambda j, e: (e, B_K // LW + j)),
                  pl.BlockSpec((S, LW), lambda j, e: (e, B_V // LW + j)),
                  vec, vec, bsp, pl.BlockSpec((LW, LW), lambda j, e: (0, 0)), blk],
        out_specs=(blk, blk, blk, bsp, vec, vec),
        scratch_shapes=[pltpu.VMEM((2, S, NA_DH), BF16)] * 4 + [pltpu.VMEM((S, LW), F32)] * 3
        + [pltpu.VMEM((2, 2, S // 128 + 1, NA_DH, 128), F32)] * 2
        + [pltpu.VMEM((S, WIN), F32)] * 2 + [pltpu.VMEM((S, WIN), BF16)] * 2,
        name=name, compiler_params=_cp(("parallel", "arbitrary")))(z, z, z, qg, kg, bias, seg, do)


def _na_rpb_grad(dbias, name):
    e_np, _, _ = _na_onehots()
    H = NA_HEADS
    nro = 2 * NA_ROWS - 1

    def fold(x_ref, o_ref):
        for ro in range(nro):
            acc = None
            for d in range(NA_ROWS):
                kr = ro + d - (NA_ROWS - 1)
                if 0 <= kr < NA_ROWS:
                    blk = x_ref[0, d, :, kr * GRID_W:(kr + 1) * GRID_W]
                    acc = blk if acc is None else acc + blk
            o_ref[0, ro] = acc
        o_ref[0, nro] = jnp.zeros((GRID_W, GRID_W), F32)

    t = pl.pallas_call(
        fold, out_shape=jax.ShapeDtypeStruct((H, 16, GRID_W, GRID_W), F32), grid=(H,),
        in_specs=[pl.BlockSpec((1, NA_ROWS, GRID_W, NA_ROWS * GRID_W), lambda h: (h, 0, 0, 0))],
        out_specs=pl.BlockSpec((1, 16, GRID_W, GRID_W), lambda h: (h, 0, 0, 0)),
        name=name + "_fold", compiler_params=_cp(("parallel",)))(dbias)
    t = t.reshape(H, 16, GRID_W * GRID_W)

    def body(x_ref, e_ref, o_ref):
        o_ref[0] = jnp.dot(x_ref[0], e_ref[...], precision=HI, preferred_element_type=F32)

    out = pl.pallas_call(
        body, out_shape=jax.ShapeDtypeStruct((H, 16, 128), F32), grid=(H,),
        in_specs=[pl.BlockSpec((1, 16, GRID_W * GRID_W), lambda h: (h, 0, 0)),
                  pl.BlockSpec((GRID_W * GRID_W, 128), lambda h: (0, 0))],
        out_specs=pl.BlockSpec((1, 16, 128), lambda h: (h, 0, 0)),
        name=name, compiler_params=_cp(("parallel",)))(t, jnp.asarray(e_np))
    return out[:, :nro, :2 * NA_COLS - 1]


_HK = GLA_HEADS * GLA_DK
_HV = GLA_HEADS * GLA_DV


def _gla_consts(reverse):
    i = np.arange(CHUNK)
    tri = (i[:, None] <= i[None, :]) if reverse else (i[:, None] >= i[None, :])
    j = np.arange(_HK)
    oseg = (j[:, None] // GLA_DK == j[None, :] // GLA_DK)
    return (jnp.asarray(tri.astype(np.float32)), jnp.asarray(tri.T.astype(np.float32)), jnp.asarray(oseg.astype(np.float32), BF16))


def _log_decay(lr, a2, ab):
    zg = jnp.dot(lr, a2, precision=HI, preferred_element_type=F32) + ab
    g = (jnp.minimum(zg, 0.0) - jnp.log(1.0 + jnp.exp(-jnp.abs(zg)))) * (1.0 / GLA_TAU)
    return zg, g


def _dotf(a, b, dn):
    return lax.dot_general(a, b, dn, precision=HI, preferred_element_type=F32)


def _dotb(a, b, dn):
    return lax.dot_general(a.astype(BF16), b.astype(BF16), dn, preferred_element_type=F32)


_COLS = 4
_SUB = 16
_NSUB = CHUNK // _SUB


def _gla_cross_blocks(reverse):
    return range(0, _NSUB - 1) if reverse else range(1, _NSUB)


def _gla_cross_terms(s, reverse, b_s, q_s, k_s, oseg_ref):
    r0 = s * _SUB
    ref = r0 + (_SUB - 1 if reverse else 0)
    bref = b_s[ref:ref + 1, :]
    rowj = lax.broadcasted_iota(jnp.int32, (CHUNK, 1), 0)
    seen = (rowj >= r0 + _SUB) if reverse else (rowj < r0)
    ek = jnp.where(seen, jnp.exp(jnp.minimum(bref - b_s[...], 0.0)), 0.0)
    kt = k_s[...] * ek
    eq = jnp.exp(jnp.minimum(b_s[r0:r0 + _SUB, :] - bref, 0.0))
    qt = q_s[r0:r0 + _SUB, :] * eq
    nmat = jnp.concatenate([kt.astype(BF16)] * GLA_HEADS, axis=0) * oseg_ref[...]
    return qt, eq, kt, ek, nmat


_NN = (((1,), (0,)), ((), ()))
_NT = (((1,), (1,)), ((), ()))
_TN = (((0,), (0,)), ((), ()))


_DT = 256


def _gla_block_tri(reverse):
    i = np.arange(_DT)
    same = i[:, None] // CHUNK == i[None, :] // CHUNK
    tri = (i[:, None] <= i[None, :]) if reverse else (i[:, None] >= i[None, :])
    return (tri & same).astype(np.float32)


def _gla_decay_fwd(z, a2, ab, reverse, name):
    T = z.shape[0]
    nc = _DT // CHUNK

    def body(lr_ref, a2_ref, ab_ref, m_ref, b_ref, ec_ref):
        _, g = _log_decay(lr_ref[...], a2_ref[...], ab_ref[...])
        b_ref[...] = _dotf(m_ref[...], g, _NN)
        for c in range(nc):
            ec_ref[c] = jnp.exp(_dotf(g[c * CHUNK:(c + 1) * CHUNK, :], jnp.ones((CHUNK, GLA_DV), F32), _TN))

    return pl.pallas_call(
        body, out_shape=(jax.ShapeDtypeStruct((T, _HK), F32), jax.ShapeDtypeStruct((T // CHUNK, _HK, GLA_DV), F32)),
        grid=(T // _DT,),
        in_specs=[pl.BlockSpec((_DT, 128), lambda i: (i, LR_OFF // 128)),
                  pl.BlockSpec((128, _HK), lambda i: (0, 0)),
                  pl.BlockSpec((1, _HK), lambda i: (0, 0)),
                  pl.BlockSpec((_DT, _DT), lambda i: (0, 0))],
        out_specs=(pl.BlockSpec((_DT, _HK), lambda i: (i, 0)), pl.BlockSpec((nc, _HK, GLA_DV), lambda i: (i, 0, 0))),
        name=name, compiler_params=_cp(("parallel",)))(z, a2, ab, jnp.asarray(_gla_block_tri(reverse)))


def _gla_decay_bwd(z, a2_f, ab_f, a2_b, ab_b, db_f, db_b, name):
    T = z.shape[0]

    def body(lr_ref, a2f_ref, abf_ref, a2b_ref, abb_ref, mf_ref, mb_ref, dbf_ref, dbb_ref,
             dlr_ref, da2f_ref, dabf_ref, da2b_ref, dabb_ref):
        @pl.when(pl.program_id(0) == 0)
        def _():
            for r in (da2f_ref, dabf_ref, da2b_ref, dabb_ref):
                r[...] = jnp.zeros_like(r)

        lr = lr_ref[...]
        dlr = jnp.zeros((_DT, 128), F32)
        for a2_ref, ab_ref, mt_ref, db_ref, da2_ref, dab_ref in ((a2f_ref, abf_ref, mf_ref, dbf_ref, da2f_ref, dabf_ref),
                                                                 (a2b_ref, abb_ref, mb_ref, dbb_ref, da2b_ref, dabb_ref)):
            zg, _ = _log_decay(lr, a2_ref[...], ab_ref[...])
            dg = _dotf(mt_ref[...], db_ref[...], _NN)
            dzg = dg * (1.0 / (1.0 + jnp.exp(zg))) * (1.0 / GLA_TAU)
            dlr = dlr + _dotf(dzg, a2_ref[...], _NT)
            da2_ref[...] += _dotf(lr, dzg, _TN)
            dab_ref[...] += jnp.sum(dzg, axis=0, keepdims=True)
        dlr_ref[...] = dlr.astype(BF16)

    a2s = pl.BlockSpec((128, _HK), lambda i: (0, 0))
    abs_ = pl.BlockSpec((1, _HK), lambda i: (0, 0))
    ms = pl.BlockSpec((_DT, _DT), lambda i: (0, 0))
    row = pl.BlockSpec((_DT, _HK), lambda i: (i, 0))
    return pl.pallas_call(
        body, out_shape=(jax.ShapeDtypeStruct((T, 128), BF16), jax.ShapeDtypeStruct((128, _HK), F32), jax.ShapeDtypeStruct((1, _HK), F32),
                         jax.ShapeDtypeStruct((128, _HK), F32), jax.ShapeDtypeStruct((1, _HK), F32)),
        grid=(T // _DT,),
        in_specs=[pl.BlockSpec((_DT, 128), lambda i: (i, LR_OFF // 128)), a2s, abs_, a2s, abs_, ms, ms, row, row],
        out_specs=(pl.BlockSpec((_DT, 128), lambda i: (i, 0)), a2s, abs_, a2s, abs_),
        name=name, compiler_params=_cp(("arbitrary",)))(
            z, a2_f, ab_f, a2_b, ab_b, jnp.asarray(_gla_block_tri(False).T), jnp.asarray(_gla_block_tri(True).T), db_f, db_b)


def _gla_fwd(z, b_all, ecol, S, reverse, name):
    T = z.shape[0]
    E = T // S
    n = S // CHUNK
    _, _, oseg = _gla_consts(reverse)
    last = 0 if reverse else CHUNK - 1

    def body(q_ref, k_ref, v_ref, b_ref, ec_ref, oseg_ref, o_ref, a_ref, st_ref, st, b_s, q_s, k_s):
        @pl.when(pl.program_id(1) == 0)
        def _():
            st[...] = jnp.zeros_like(st)

        q = q_ref[...] * (GLA_DK ** -0.5)
        k = k_ref[...]
        v = v_ref[...]
        b = b_ref[...]
        bl_row = b_ref[last:last + 1, :]
        e_col = ec_ref[0]
        b_s[...] = b
        q_s[...] = q
        k_s[...] = k
        lane = lax.broadcasted_iota(jnp.int32, (1, _HK), 1) % GLA_DK

        rowi = lax.broadcasted_iota(jnp.int32, (CHUNK, 1), 0)
        blk0 = (rowi // _SUB) * _SUB

        def cols(jj, a):
            ts = []
            for u in range(_COLS):
                jp = jj * _COLS + u
                tiles = []
                for s in range(_NSUB):
                    rs_ = slice(s * _SUB, (s + 1) * _SUB)
                    bj = b_s[pl.ds(s * _SUB + jp, 1), :]
                    kj = k_s[pl.ds(s * _SUB + jp, 1), :]
                    tiles.append(q_s[rs_, :] * jnp.exp(jnp.minimum(b_s[rs_, :] - bj, 0.0)) * kj)
                ts.append(jnp.concatenate(tiles, axis=0).astype(BF16))
            r = jnp.dot(jnp.concatenate(ts, axis=0), oseg_ref[...], preferred_element_type=F32)
            for u in range(_COLS):
                a = jnp.where(lane == blk0 + (jj * _COLS + u), r[u * CHUNK:(u + 1) * CHUNK, :], a)
            return a

        a = lax.fori_loop(0, _SUB // _COLS, cols, jnp.zeros((CHUNK, _HK), F32))
        keep = (rowi <= lane) if reverse else (rowi >= lane)
        a = jnp.where(keep, a, 0.0)
        cross = []
        for s in range(_NSUB):
            if s in _gla_cross_blocks(reverse):
                qt, _, _, _, nmat = _gla_cross_terms(s, reverse, b_s, q_s, k_s, oseg_ref)
                cross.append(lax.dot_general(qt.astype(BF16), nmat, _NT, preferred_element_type=F32))
            else:
                cross.append(jnp.zeros((_SUB, _HK), F32))
        a = a + jnp.concatenate(cross, axis=0)
        a_ref[...] = a
        st_ref[0] = st[...]
        qb = q * jnp.exp(b)
        kd = k * jnp.exp(bl_row - b)
        for h in range(GLA_HEADS):
            ks_ = slice(h * GLA_DK, (h + 1) * GLA_DK)
            vs_ = slice(h * GLA_DV, (h + 1) * GLA_DV)
            s_h = st[ks_, :]
            o_ref[:, vs_] = _dotb(qb[:, ks_], s_h, _NN) + _dotb(a[:, ks_], v[:, vs_], _NN)
            st[ks_, :] = s_h * e_col[ks_, :] + _dotb(kd[:, ks_], v[:, vs_], _TN)

    def rowblk(e, c):
        return e * n + ((n - 1 - c) if reverse else c)

    return pl.pallas_call(
        body, out_shape=(jax.ShapeDtypeStruct((T, _HV), F32), jax.ShapeDtypeStruct((T, _HK), F32),
                         jax.ShapeDtypeStruct((T // CHUNK, _HK, GLA_DV), F32)),
        grid=(E, n),
        in_specs=[pl.BlockSpec((CHUNK, _HK), lambda e, c: (rowblk(e, c), C_Q // _HK)),
                  pl.BlockSpec((CHUNK, _HK), lambda e, c: (rowblk(e, c), C_K // _HK)),
                  pl.BlockSpec((CHUNK, _HV), lambda e, c: (rowblk(e, c), C_V // _HV)),
                  pl.BlockSpec((CHUNK, _HK), lambda e, c: (rowblk(e, c), 0)),
                  pl.BlockSpec((1, _HK, GLA_DV), lambda e, c: (rowblk(e, c), 0, 0)),
                  pl.BlockSpec((_HK, _HK), lambda e, c: (0, 0))],
        out_specs=(pl.BlockSpec((CHUNK, _HV), lambda e, c: (rowblk(e, c), 0)),
                   pl.BlockSpec((CHUNK, _HK), lambda e, c: (rowblk(e, c), 0)),
                   pl.BlockSpec((1, _HK, GLA_DV), lambda e, c: (rowblk(e, c), 0, 0))),
        scratch_shapes=[pltpu.VMEM((_HK, GLA_DV), F32)] + [pltpu.VMEM((CHUNK, _HK), F32)] * 3,
        name=name, compiler_params=_cp(("parallel", "arbitrary")))(z, z, z, b_all, ecol, oseg)


def _gla_fwd_both(z, b_f, ec_f, b_b, ec_b, S, name):
    T = z.shape[0]
    E = T // S
    n = S // CHUNK
    _, _, oseg = _gla_consts(False)

    def body(*refs):
        oseg_ref = refs[10]
        dirs = []
        for di, reverse in enumerate((False, True)):
            q_ref, k_ref, v_ref, b_ref, ec_ref = refs[5 * di:5 * di + 5]
            o_ref, a_ref, st_ref = refs[11 + 3 * di:14 + 3 * di]
            st, b_s, q_s, k_s = refs[17 + 4 * di:21 + 4 * di]
            dirs.append((reverse, q_ref, k_ref, v_ref, b_ref, ec_ref, o_ref, a_ref, st_ref, st, b_s, q_s, k_s))

        @pl.when(pl.program_id(1) == 0)
        def _():
            for d in dirs:
                d[9][...] = jnp.zeros_like(d[9])

        lane = lax.broadcasted_iota(jnp.int32, (1, _HK), 1) % GLA_DK
        rowi = lax.broadcasted_iota(jnp.int32, (CHUNK, 1), 0)
        blk0 = (rowi // _SUB) * _SUB
        for (_, q_ref, k_ref, _, b_ref, _, _, _, _, _, b_s, q_s, k_s) in dirs:
            b_s[...] = b_ref[...]
            q_s[...] = q_ref[...] * (GLA_DK ** -0.5)
            k_s[...] = k_ref[...]

        def cols(d, jj, a):
            b_s, q_s, k_s = d[10], d[11], d[12]
            ts = []
            for u in range(_COLS):
                jp = jj * _COLS + u
                tiles = []
                for s in range(_NSUB):
                    rs_ = slice(s * _SUB, (s + 1) * _SUB)
                    bj = b_s[pl.ds(s * _SUB + jp, 1), :]
                    kj = k_s[pl.ds(s * _SUB + jp, 1), :]
                    tiles.append(q_s[rs_, :] * jnp.exp(jnp.minimum(b_s[rs_, :] - bj, 0.0)) * kj)
                ts.append(jnp.concatenate(tiles, axis=0).astype(BF16))
            r = jnp.dot(jnp.concatenate(ts, axis=0), oseg_ref[...], preferred_element_type=F32)
            for u in range(_COLS):
                a = jnp.where(lane == blk0 + (jj * _COLS + u), r[u * CHUNK:(u + 1) * CHUNK, :], a)
            return a

        zero = jnp.zeros((CHUNK, _HK), F32)
        acc = lax.fori_loop(0, _SUB // _COLS, lambda jj, c: tuple(cols(d, jj, a) for d, a in zip(dirs, c)), (zero, zero))

        for (reverse, _, _, v_ref, b_ref, ec_ref, o_ref, a_ref, st_ref, st, b_s, q_s, k_s), a in zip(dirs, acc):
            last = 0 if reverse else CHUNK - 1
            keep = (rowi <= lane) if reverse else (rowi >= lane)
            a = jnp.where(keep, a, 0.0)
            cross = []
            for s in range(_NSUB):
                if s in _gla_cross_blocks(reverse):
                    qt, _, _, _, nmat = _gla_cross_terms(s, reverse, b_s, q_s, k_s, oseg_ref)
                    cross.append(lax.dot_general(qt.astype(BF16), nmat, _NT, preferred_element_type=F32))
                else:
                    cross.append(jnp.zeros((_SUB, _HK), F32))
            a = a + jnp.concatenate(cross, axis=0)
            a_ref[...] = a
            st_ref[0] = st[...]
            b = b_s[...]
            v = v_ref[...]
            e_col = ec_ref[0]
            qb = q_s[...] * jnp.exp(b)
            kd = k_s[...] * jnp.exp(b_ref[last:last + 1, :] - b)
            for h in range(GLA_HEADS):
                ks_ = slice(h * GLA_DK, (h + 1) * GLA_DK)
                vs_ = slice(h * GLA_DV, (h + 1) * GLA_DV)
                s_h = st[ks_, :]
                o_ref[:, vs_] = _dotb(qb[:, ks_], s_h, _NN) + _dotb(a[:, ks_], v[:, vs_], _NN)
                st[ks_, :] = s_h * e_col[ks_, :] + _dotb(kd[:, ks_], v[:, vs_], _TN)

    def specs(reverse):
        rb = (lambda e, c: e * n + (n - 1 - c)) if reverse else (lambda e, c: e * n + c)
        ins = [pl.BlockSpec((CHUNK, _HK), lambda e, c: (rb(e, c), C_Q // _HK)),
               pl.BlockSpec((CHUNK, _HK), lambda e, c: (rb(e, c), C_K // _HK)),
               pl.BlockSpec((CHUNK, _HV), lambda e, c: (rb(e, c), C_V // _HV)),
               pl.BlockSpec((CHUNK, _HK), lambda e, c: (rb(e, c), 0)),
               pl.BlockSpec((1, _HK, GLA_DV), lambda e, c: (rb(e, c), 0, 0))]
        outs = [pl.BlockSpec((CHUNK, _HV), lambda e, c: (rb(e, c), 0)),
                pl.BlockSpec((CHUNK, _HK), lambda e, c: (rb(e, c), 0)),
                pl.BlockSpec((1, _HK, GLA_DV), lambda e, c: (rb(e, c), 0, 0))]
        return ins, outs

    in_f, out_f = specs(False)
    in_b, out_b = specs(True)
    shapes = (jax.ShapeDtypeStruct((T, _HV), F32), jax.ShapeDtypeStruct((T, _HK), F32),
              jax.ShapeDtypeStruct((T // CHUNK, _HK, GLA_DV), F32))
    res = pl.pallas_call(
        body, out_shape=shapes + shapes, grid=(E, n),
        in_specs=in_f + in_b + [pl.BlockSpec((_HK, _HK), lambda e, c: (0, 0))], out_specs=tuple(out_f + out_b),
        scratch_shapes=([pltpu.VMEM((_HK, GLA_DV), F32)] + [pltpu.VMEM((CHUNK, _HK), F32)] * 3) * 2,
        name=name, compiler_params=_cp(("parallel", "arbitrary")))(z, z, z, b_f, ec_f, z, z, z, b_b, ec_b, oseg)
    return res[:3], res[3:]


def _gla_bwd(z, b_all, ecol, att, states, do, prev, S, reverse, name):
    T = z.shape[0]
    E = T // S
    n = S // CHUNK
    _, _, oseg = _gla_consts(reverse)
    has_prev = prev is not None
    odt = BF16 if has_prev else F32
    last = 0 if reverse else CHUNK - 1

    def body(*refs):
        (q_ref, k_ref, v_ref, b_ref, ec_ref, oseg_ref, att_ref, st_ref, do_ref) = refs[:9]
        refs = refs[9:]
        if has_prev:
            pq_ref, pk_ref, pv_ref = refs[:3]
            refs = refs[3:]
        (dq_ref, dk_ref, dv_ref, db_ref, dst, b_s, q_s, k_s, da_s, dqb_s, dkd_s, dk3_s, dbn_s, dsp_s) = refs

        @pl.when(pl.program_id(1) == 0)
        def _():
            dst[...] = jnp.zeros_like(dst)

        q = q_ref[...] * (GLA_DK ** -0.5)
        k = k_ref[...]
        v = v_ref[...]
        b = b_ref[...]
        bl_row = b_ref[last:last + 1, :]
        eb = jnp.exp(b)
        ekd = jnp.exp(bl_row - b)
        qb = q * eb
        kd = k * ekd
        b_s[...] = b
        q_s[...] = q
        k_s[...] = k
        att = att_ref[...]
        s_all = st_ref[0]
        dsn = dst[...]
        e_col = ec_ref[0]
        do = do_ref[...]
        lane = lax.broadcasted_iota(jnp.int32, (1, _HK), 1) % GLA_DK
        rowi = lax.broadcasted_iota(jnp.int32, (CHUNK, 1), 0)
        keep = (rowi <= lane) if reverse else (rowi >= lane)
        for h in range(GLA_HEADS):
            ks_ = slice(h * GLA_DK, (h + 1) * GLA_DK)
            vs_ = slice(h * GLA_DV, (h + 1) * GLA_DV)
            do_h = do[:, vs_]
            s_h = s_all[ks_, :]
            dsn_h = dsn[ks_, :]
            dqb_s[:, ks_] = _dotb(do_h, s_h, _NT)
            dsp_s[ks_, :] = _dotb(qb[:, ks_], do_h, _TN) + dsn_h * e_col[ks_, :]
            da_s[:, ks_] = _dotb(do_h, v[:, vs_], _NT)
            dv_h = _dotb(att[:, ks_], do_h, _TN) + _dotb(kd[:, ks_], dsn_h, _NN)
            if has_prev:
                dv_h = dv_h + pv_ref[:, vs_]
            dv_ref[:, vs_] = dv_h.astype(odt)
            dkd_s[:, ks_] = _dotb(v[:, vs_], dsn_h, _NT)
        da_s[...] = jnp.where(keep, da_s[...], 0.0)
        dqb = dqb_s[...]
        dkd = dkd_s[...]
        x = dsn * s_all * e_col
        dbl_row = _dotf(jnp.ones((8, GLA_DV), F32), x, _NT)[0:1, :] + jnp.sum(dkd * kd, axis=0, keepdims=True)

        blk0 = (rowi // _SUB) * _SUB

        def cols(jj, carry):
            dq3, db3 = list(carry[:_NSUB]), list(carry[_NSUB:])
            sel = [jnp.where(lane == blk0 + (jj * _COLS + u), da_s[...], 0.0).astype(BF16) for u in range(_COLS)]
            dcols = jnp.dot(jnp.concatenate(sel, axis=0), oseg_ref[...], preferred_element_type=F32)
            for u in range(_COLS):
                jp = jj * _COLS + u
                for s in range(_NSUB):
                    rs_ = slice(s * _SUB, (s + 1) * _SUB)
                    bj = b_s[pl.ds(s * _SUB + jp, 1), :]
                    kj = k_s[pl.ds(s * _SUB + jp, 1), :]
                    tm_ = dcols[u * CHUNK + s * _SUB:u * CHUNK + (s + 1) * _SUB, :] * jnp.exp(jnp.minimum(b_s[rs_, :] - bj, 0.0))
                    dq3[s] = dq3[s] + tm_ * kj
                    gq = tm_ * q_s[rs_, :]
                    dk3_s[pl.ds(s * _SUB + jp, 1), :] = jnp.sum(gq, axis=0, keepdims=True)
                    w = gq * kj
                    dbn_s[pl.ds(s * _SUB + jp, 1), :] = jnp.sum(w, axis=0, keepdims=True)
                    db3[s] = db3[s] + w
            return tuple(dq3) + tuple(db3)

        zero = jnp.zeros((_SUB, _HK), F32)
        acc = lax.fori_loop(0, _SUB // _COLS, cols, (zero,) * (2 * _NSUB))
        dq3 = jnp.concatenate(acc[:_NSUB], axis=0)
        db3 = jnp.concatenate(acc[_NSUB:], axis=0)
        head = lax.broadcasted_iota(jnp.int32, (1, _HK), 1) // GLA_DK
        dq_x, db_x = [], []
        dk_x = jnp.zeros((CHUNK, _HK), F32)
        db_k = jnp.zeros((CHUNK, _HK), F32)
        for s in range(_NSUB):
            if s not in _gla_cross_blocks(reverse):
                dq_x.append(zero)
                db_x.append(zero)
                continue
            r0 = s * _SUB
            qt, eq, kt, ek, nmat = _gla_cross_terms(s, reverse, b_s, q_s, k_s, oseg_ref)
            seen = (lane >= r0 + _SUB) if reverse else (lane < r0)
            dax = jnp.where(seen, da_s[r0:r0 + _SUB, :], 0.0).astype(BF16)
            dqt = jnp.dot(dax, nmat, preferred_element_type=F32)
            full = lax.dot_general(dax, qt.astype(BF16), _TN, preferred_element_type=F32)
            dkt = full[0:CHUNK, :]
            for h in range(1, GLA_HEADS):
                dkt = jnp.where(head == h, full[h * CHUNK:(h + 1) * CHUNK, :], dkt)
            dq_x.append(dqt * eq)
            db_x.append(dqt * qt)
            dk_x = dk_x + dkt * ek
            db_k = db_k + dkt * kt
        dq = (dqb * eb + dq3 + jnp.concatenate(dq_x, axis=0)) * (GLA_DK ** -0.5)
        dk = dkd * ekd + dk3_s[...] + dk_x
        db = dqb * qb - dkd * kd + db3 - dbn_s[...] + jnp.concatenate(db_x, axis=0) - db_k
        db_ref[...] = jnp.where(rowi == last, db + dbl_row, db)
        if has_prev:
            dq = dq + pq_ref[...]
            dk = dk + pk_ref[...]
        dq_ref[...] = dq.astype(odt)
        dk_ref[...] = dk.astype(odt)
        dst[...] = dsp_s[...]

    def rowblk(e, c):
        return e * n + (c if reverse else (n - 1 - c))

    hk = pl.BlockSpec((CHUNK, _HK), lambda e, c: (rowblk(e, c), 0))
    hv = pl.BlockSpec((CHUNK, _HV), lambda e, c: (rowblk(e, c), 0))
    stb = pl.BlockSpec((1, _HK, GLA_DV), lambda e, c: (rowblk(e, c), 0, 0))
    in_specs = [pl.BlockSpec((CHUNK, _HK), lambda e, c: (rowblk(e, c), C_Q // _HK)),
                pl.BlockSpec((CHUNK, _HK), lambda e, c: (rowblk(e, c), C_K // _HK)),
                pl.BlockSpec((CHUNK, _HV), lambda e, c: (rowblk(e, c), C_V // _HV)),
                hk, stb, pl.BlockSpec((_HK, _HK), lambda e, c: (0, 0)), hk, stb, hv]
    args = [z, z, z, b_all, ecol, oseg, att, states, do]
    if has_prev:
        in_specs += [hk, hk, hv]
        args += list(prev)
    return pl.pallas_call(
        body, out_shape=(jax.ShapeDtypeStruct((T, _HK), odt), jax.ShapeDtypeStruct((T, _HK), odt),
                         jax.ShapeDtypeStruct((T, _HV), odt), jax.ShapeDtypeStruct((T, _HK), F32)),
        grid=(E, n), in_specs=in_specs, out_specs=(hk, hk, hv, hk),
        scratch_shapes=[pltpu.VMEM((_HK, GLA_DV), F32)] + [pltpu.VMEM((CHUNK, _HK), F32)] * 8 + [pltpu.VMEM((_HK, GLA_DV), F32)],
        name=name, compiler_params=_cp(("parallel", "arbitrary")))(*args)


def _gla_bwd_both(z, fwd_saved, rev_saved, do, S, name):
    T = z.shape[0]
    E = T // S
    n = S // CHUNK
    _, _, oseg = _gla_consts(False)
    NI, NO, NS = 8, 4, 10

    def body(*refs):
        oseg_ref = refs[2 * NI]
        dirs = []
        for di, reverse in enumerate((False, True)):
            ins = refs[NI * di:NI * (di + 1)]
            outs = refs[2 * NI + 1 + NO * di:2 * NI + 1 + NO * (di + 1)]
            scr = refs[2 * NI + 1 + 2 * NO + NS * di:2 * NI + 1 + 2 * NO + NS * (di + 1)]
            dirs.append((reverse, ins, outs, scr))

        @pl.when(pl.program_id(1) == 0)
        def _():
            for d in dirs:
                d[3][0][...] = jnp.zeros_like(d[3][0])

        lane = lax.broadcasted_iota(jnp.int32, (1, _HK), 1) % GLA_DK
        head = lax.broadcasted_iota(jnp.int32, (1, _HK), 1) // GLA_DK
        rowi = lax.broadcasted_iota(jnp.int32, (CHUNK, 1), 0)
        blk0 = (rowi // _SUB) * _SUB

        def factors(reverse, b_ref, b_s, q_s, k_s):
            last = 0 if reverse else CHUNK - 1
            b = b_s[...]
            eb = jnp.exp(b)
            ekd = jnp.exp(b_ref[last:last + 1, :] - b)
            return eb, ekd, q_s[...] * eb, k_s[...] * ekd

        dbl_rows = []
        for reverse, (q_ref, k_ref, v_ref, b_ref, ec_ref, att_ref, st_ref, do_ref), (dq_ref, dk_ref, dv_ref, db_ref), \
                (dst, b_s, q_s, k_s, da_s, dqb_s, dkd_s, dk3_s, dbn_s, dsp_s) in dirs:
            b_s[...] = b_ref[...]
            q_s[...] = q_ref[...] * (GLA_DK ** -0.5)
            k_s[...] = k_ref[...]
            _, _, qb, kd = factors(reverse, b_ref, b_s, q_s, k_s)
            v = v_ref[...]
            att = att_ref[...]
            s_all = st_ref[0]
            dsn = dst[...]
            e_col = ec_ref[0]
            do = do_ref[...]
            keep = (rowi <= lane) if reverse else (rowi >= lane)
            for h in range(GLA_HEADS):
                ks_ = slice(h * GLA_DK, (h + 1) * GLA_DK)
                vs_ = slice(h * GLA_DV, (h + 1) * GLA_DV)
                do_h = do[:, vs_]
                s_h = s_all[ks_, :]
                dsn_h = dsn[ks_, :]
                dqb_s[:, ks_] = _dotb(do_h, s_h, _NT)
                dsp_s[ks_, :] = _dotb(qb[:, ks_], do_h, _TN) + dsn_h * e_col[ks_, :]
                da_s[:, ks_] = _dotb(do_h, v[:, vs_], _NT)
                dv_ref[:, vs_] = _dotb(att[:, ks_], do_h, _TN) + _dotb(kd[:, ks_], dsn_h, _NN)
                dkd_s[:, ks_] = _dotb(v[:, vs_], dsn_h, _NT)
            da_s[...] = jnp.where(keep, da_s[...], 0.0)
            x = dsn * s_all * e_col
            dbl_rows.append(_dotf(jnp.ones((8, GLA_DV), F32), x, _NT)[0:1, :] + jnp.sum(dkd_s[...] * kd, axis=0, keepdims=True))

        def cols(d, jj, carry):
            _, b_s, q_s, k_s, da_s, _, _, dk3_s, dbn_s, _ = d[3]
            dq3, db3 = list(carry[:_NSUB]), list(carry[_NSUB:])
            sel = [jnp.where(lane == blk0 + (jj * _COLS + u), da_s[...], 0.0).astype(BF16) for u in range(_COLS)]
            dcols = jnp.dot(jnp.concatenate(sel, axis=0), oseg_ref[...], preferred_element_type=F32)
            for u in range(_COLS):
                jp = jj * _COLS + u
                for s in range(_NSUB):
                    rs_ = slice(s * _SUB, (s + 1) * _SUB)
                    bj = b_s[pl.ds(s * _SUB + jp, 1), :]
                    kj = k_s[pl.ds(s * _SUB + jp, 1), :]
                    tm_ = dcols[u * CHUNK + s * _SUB:u * CHUNK + (s + 1) * _SUB, :] * jnp.exp(jnp.minimum(b_s[rs_, :] - bj, 0.0))
                    dq3[s] = dq3[s] + tm_ * kj
                    gq = tm_ * q_s[rs_, :]
                    dk3_s[pl.ds(s * _SUB + jp, 1), :] = jnp.sum(gq, axis=0, keepdims=True)
                    w = gq * kj
                    dbn_s[pl.ds(s * _SUB + jp, 1), :] = jnp.sum(w, axis=0, keepdims=True)
                    db3[s] = db3[s] + w
            return tuple(dq3) + tuple(db3)

        zero = jnp.zeros((_SUB, _HK), F32)
        init = (zero,) * (2 * _NSUB)
        accs = lax.fori_loop(0, _SUB // _COLS, lambda jj, c: tuple(cols(d, jj, a) for d, a in zip(dirs, c)), (init, init))

        for (reverse, ins, (dq_ref, dk_ref, dv_ref, db_ref), (dst, b_s, q_s, k_s, da_s, dqb_s, dkd_s, dk3_s, dbn_s, dsp_s)), \
                acc, dbl_row in zip(dirs, accs, dbl_rows):
            last = 0 if reverse else CHUNK - 1
            eb, ekd, qb, kd = factors(reverse, ins[3], b_s, q_s, k_s)
            dqb = dqb_s[...]
            dkd = dkd_s[...]
            dq3 = jnp.concatenate(acc[:_NSUB], axis=0)
            db3 = jnp.concatenate(acc[_NSUB:], axis=0)
            dq_x, db_x = [], []
            dk_x = jnp.zeros((CHUNK, _HK), F32)
            db_k = jnp.zeros((CHUNK, _HK), F32)
            for s in range(_NSUB):
                if s not in _gla_cross_blocks(reverse):
                    dq_x.append(zero)
                    db_x.append(zero)
                    continue
                r0 = s * _SUB
                qt, eq, kt, ek, nmat = _gla_cross_terms(s, reverse, b_s, q_s, k_s, oseg_ref)
                seen = (lane >= r0 + _SUB) if reverse else (lane < r0)
                dax = jnp.where(seen, da_s[r0:r0 + _SUB, :], 0.0).astype(BF16)
                dqt = jnp.dot(dax, nmat, preferred_element_type=F32)
                full = lax.dot_general(dax, qt.astype(BF16), _TN, preferred_element_type=F32)
                dkt = full[0:CHUNK, :]
                for h in range(1, GLA_HEADS):
                    dkt = jnp.where(head == h, full[h * CHUNK:(h + 1) * CHUNK, :], dkt)
                dq_x.append(dqt * eq)
                db_x.append(dqt * qt)
                dk_x = dk_x + dkt * ek
                db_k = db_k + dkt * kt
            dq_ref[...] = (dqb * eb + dq3 + jnp.concatenate(dq_x, axis=0)) * (GLA_DK ** -0.5)
            dk_ref[...] = dkd * ekd + dk3_s[...] + dk_x
            db = dqb * qb - dkd * kd + db3 - dbn_s[...] + jnp.concatenate(db_x, axis=0) - db_k
            db_ref[...] = jnp.where(rowi == last, db + dbl_row, db)
            dst[...] = dsp_s[...]

    def specs(reverse):
        rb = (lambda e, c: e * n + c) if reverse else (lambda e, c: e * n + (n - 1 - c))
        hk = pl.BlockSpec((CHUNK, _HK), lambda e, c: (rb(e, c), 0))
        hv = pl.BlockSpec((CHUNK, _HV), lambda e, c: (rb(e, c), 0))
        stb = pl.BlockSpec((1, _HK, GLA_DV), lambda e, c: (rb(e, c), 0, 0))
        ins = [pl.BlockSpec((CHUNK, _HK), lambda e, c: (rb(e, c), C_Q // _HK)),
               pl.BlockSpec((CHUNK, _HK), lambda e, c: (rb(e, c), C_K // _HK)),
               pl.BlockSpec((CHUNK, _HV), lambda e, c: (rb(e, c), C_V // _HV)),
               hk, stb, hk, stb, hv]
        return ins, [hk, hk, hv, hk]

    in_f, out_f = specs(False)
    in_b, out_b = specs(True)
    shapes = (jax.ShapeDtypeStruct((T, _HK), F32), jax.ShapeDtypeStruct((T, _HK), F32),
              jax.ShapeDtypeStruct((T, _HV), F32), jax.ShapeDtypeStruct((T, _HK), F32))
    scratch = [pltpu.VMEM((_HK, GLA_DV), F32)] + [pltpu.VMEM((CHUNK, _HK), F32)] * 8 + [pltpu.VMEM((_HK, GLA_DV), F32)]
    res = pl.pallas_call(
        body, out_shape=shapes + shapes, grid=(E, n),
        in_specs=in_f + in_b + [pl.BlockSpec((_HK, _HK), lambda e, c: (0, 0))], out_specs=tuple(out_f + out_b),
        scratch_shapes=scratch * 2, name=name, compiler_params=_cp(("parallel", "arbitrary")))(
            z, z, z, *fwd_saved, do, z, z, z, *rev_saved, do, oseg)
    return res[:4], res[4:]


def _gla_norm_fwd(of, ob, og, name):
    T = of.shape[0]
    tm = 256

    def body(f_ref, b_ref, g_ref, o_ref):
        for h in range(GLA_HEADS):
            vs_ = slice(h * GLA_DV, (h + 1) * GLA_DV)
            o = f_ref[:, vs_] + b_ref[:, vs_]
            o_ref[:, vs_] = o * lax.rsqrt(jnp.mean(o * o, axis=-1, keepdims=True) + EPS) * g_ref[:, vs_]

    row = pl.BlockSpec((tm, _HV), lambda i: (i, 0))
    vec = pl.BlockSpec((1, _HV), lambda i: (0, 0))
    return pl.pallas_call(body, out_shape=jax.ShapeDtypeStruct((T, _HV), F32), grid=(T // tm,),
                          in_specs=[row, row, vec], out_specs=row, name=name, compiler_params=_cp(("parallel",)))(of, ob, og)


def _gla_norm_bwd(of, ob, og, dpre, name):
    T = of.shape[0]
    tm = 256

    def body(f_ref, b_ref, g_ref, dp_ref, do_ref, dg_ref):
        @pl.when(pl.program_id(0) == 0)
        def _():
            dg_ref[...] = jnp.zeros_like(dg_ref)

        for h in range(GLA_HEADS):
            vs_ = slice(h * GLA_DV, (h + 1) * GLA_DV)
            o = f_ref[:, vs_] + b_ref[:, vs_]
            r = lax.rsqrt(jnp.mean(o * o, axis=-1, keepdims=True) + EPS)
            xh = o * r
            dp = dp_ref[:, vs_]
            dxh = dp * g_ref[:, vs_]
            do_ref[:, vs_] = r * (dxh - xh * jnp.mean(dxh * xh, axis=-1, keepdims=True))
            dg_ref[:, vs_] += jnp.sum(dp * xh, axis=0, keepdims=True)

    row = pl.BlockSpec((tm, _HV), lambda i: (i, 0))
    vec = pl.BlockSpec((1, _HV), lambda i: (0, 0))
    return pl.pallas_call(
        body, out_shape=(jax.ShapeDtypeStruct((T, _HV), F32), jax.ShapeDtypeStruct((1, _HV), F32)), grid=(T // tm,),
        in_specs=[row, row, vec, row], out_specs=(row, vec), name=name, compiler_params=_cp(("arbitrary",)))(of, ob, og, dpre)


_ANY = pl.BlockSpec(memory_space=pl.ANY)


def _coords():
    return lax.axis_index("x"), lax.axis_index("y"), lax.axis_index("c")


def _other_chips(x, y):
    return ((1 - x, y), (x, 1 - y), (1 - x, 1 - y))


def _gather_weights(arrays, chunks, name):
    n = len(arrays)
    pieces = []
    for k in range(max(chunks)):
        for i, a in enumerate(arrays):
            if k < chunks[i]:
                rc = a.shape[1] // chunks[i]
                pieces.append((i, k * rc, rc))
    m = len(pieces)

    def body(*refs):
        srcs, dsts = refs[:n], refs[n:2 * n]
        send_sems, recv_sems, local_sems = refs[2 * n:]
        x, y, c = _coords()
        me = 2 * x + y
        loc = [pltpu.make_async_copy(s, d.at[me], local_sems.at[i]) for i, (s, d) in enumerate(zip(srcs, dsts))]
        for cp in loc:
            cp.start()
        ici = []
        for p, (i, r0, rc) in enumerate(pieces):
            for j, (px, py) in enumerate(_other_chips(x, y)):
                ici.append(pltpu.make_async_remote_copy(
                    src_ref=srcs[i].at[c, pl.ds(r0, rc)], dst_ref=dsts[i].at[me, c, pl.ds(r0, rc)],
                    send_sem=send_sems.at[3 * p + j], recv_sem=recv_sems.at[3 * p + j],
                    device_id=(px, py, c), device_id_type=MESH))
        for cp in ici:
            cp.start()
        fwd = []
        for p, (i, r0, rc) in enumerate(pieces):
            for j, (px, py) in enumerate(_other_chips(x, y)):
                ici[3 * p + j].wait_recv()
                part = dsts[i].at[2 * px + py, c, pl.ds(r0, rc)]
                cp = pltpu.make_async_remote_copy(
                    src_ref=part, dst_ref=part, send_sem=send_sems.at[3 * m + 3 * p + j], recv_sem=recv_sems.at[3 * m + 3 * p + j],
                    device_id=(x, y, 1 - c), device_id_type=MESH)
                cp.start()
                fwd.append(cp)
        for cp in fwd:
            cp.wait_recv()
        for cp in ici + fwd:
            cp.wait_send()
        for cp in loc:
            cp.wait()

    return pl.pallas_call(
        body, out_shape=tuple(jax.ShapeDtypeStruct((4,) + a.shape, a.dtype) for a in arrays),
        in_specs=[_ANY] * n, out_specs=(_ANY,) * n,
        scratch_shapes=[pltpu.SemaphoreType.DMA((6 * m,)), pltpu.SemaphoreType.DMA((6 * m,)), pltpu.SemaphoreType.DMA((n,))],
        name=name)(*arrays)


def _sibling_exchange(layered, whole, name):
    nl, n = len(layered), len(layered) + len(whole)

    def body(*refs):
        srcs, dsts = refs[:n], refs[n:2 * n]
        send_sems, recv_sems = refs[2 * n:]
        x, y, c = _coords()
        rem = [pltpu.make_async_remote_copy(src_ref=(s.at[1 - c] if i < nl else s), dst_ref=d, send_sem=send_sems.at[i],
                                            recv_sem=recv_sems.at[i], device_id=(x, y, 1 - c), device_id_type=MESH)
               for i, (s, d) in enumerate(zip(srcs, dsts))]
        for cp in rem:
            cp.start()
        for cp in rem:
            cp.wait()

    outs = [jax.ShapeDtypeStruct(a.shape[1:], a.dtype) for a in layered] + [jax.ShapeDtypeStruct(a.shape, a.dtype) for a in whole]
    return pl.pallas_call(
        body, out_shape=tuple(outs), in_specs=[_ANY] * n, out_specs=(_ANY,) * n,
        scratch_shapes=[pltpu.SemaphoreType.DMA((n,)), pltpu.SemaphoreType.DMA((n,))], name=name)(*layered, *whole)


def _chip_exchange(scatter, bcast, name):
    ns, n = len(scatter), len(scatter) + len(bcast)

    def body(*refs):
        srcs, dsts = refs[:n], refs[n:2 * n]
        send_sems, recv_sems, local_sems = refs[2 * n:]
        x, y, c = _coords()
        me = 2 * x + y
        loc = [pltpu.make_async_copy((s.at[me] if i < ns else s), d.at[me], local_sems.at[i])
               for i, (s, d) in enumerate(zip(srcs, dsts))]
        for cp in loc:
            cp.start()
        rem = []
        for j, (px, py) in enumerate(_other_chips(x, y)):
            for i, (s, d) in enumerate(zip(srcs, dsts)):
                rem.append(pltpu.make_async_remote_copy(
                    src_ref=(s.at[2 * px + py] if i < ns else s), dst_ref=d.at[me], send_sem=send_sems.at[n * j + i],
                    recv_sem=recv_sems.at[n * j + i], device_id=(px, py, c), device_id_type=MESH))
        for cp in rem:
            cp.start()
        for cp in rem:
            cp.wait()
        for cp in loc:
            cp.wait()

    outs = [jax.ShapeDtypeStruct(a.shape, a.dtype) for a in scatter] + [jax.ShapeDtypeStruct((4,) + a.shape, a.dtype) for a in bcast]
    return pl.pallas_call(
        body, out_shape=tuple(outs), in_specs=[_ANY] * n, out_specs=(_ANY,) * n,
        scratch_shapes=[pltpu.SemaphoreType.DMA((3 * n,)), pltpu.SemaphoreType.DMA((3 * n,)), pltpu.SemaphoreType.DMA((n,))],
        name=name)(*scatter, *bcast)


_EW_BLOCK_BYTES = 2 * 1024 * 1024


def _tile2d(R, C):
    if R % 256 == 0 and 256 * C * 4 <= _EW_BLOCK_BYTES:
        return 256, C
    bc = 256 if C % 256 == 0 else C
    for br in range(R, 0, -1):
        if R % br == 0 and (br % 8 == 0 or br == R) and br * bc * 4 <= _EW_BLOCK_BYTES:
            return br, bc
    return R, bc


def _sum_slots(r, name):
    n, R, C = r.shape
    br, bc = _tile2d(R, C)

    def body(r_ref, o_ref):
        acc = r_ref[0].astype(F32)
        for i in range(1, n):
            acc = acc + r_ref[i].astype(F32)
        o_ref[...] = acc

    return pl.pallas_call(body, out_shape=jax.ShapeDtypeStruct((R, C), F32), grid=(R // br, C // bc),
                          in_specs=[pl.BlockSpec((n, br, bc), lambda i, j: (0, i, j))],
                          out_specs=pl.BlockSpec((br, bc), lambda i, j: (i, j)),
                          name=name, compiler_params=_cp(("parallel", "parallel")))(r)


_SMEM = pl.BlockSpec(memory_space=pltpu.SMEM)


def _add2(a, b, out_dtype, name, pick=None):
    R, C = b.shape
    br, bc = _tile2d(R, C)
    blk = pl.BlockSpec((br, bc), lambda i, j: (i, j))
    if pick is None:
        def body(a_ref, b_ref, o_ref):
            o_ref[...] = (a_ref[...].astype(F32) + b_ref[...].astype(F32)).astype(out_dtype)
        in_specs, args = [blk, blk], (a, b)
    else:
        def body(c_ref, a_ref, b_ref, o_ref):
            av = jnp.where(c_ref[0] == 0, a_ref[0], a_ref[1])
            o_ref[...] = (av.astype(F32) + b_ref[...].astype(F32)).astype(out_dtype)
        in_specs = [_SMEM, pl.BlockSpec((2, br, bc), lambda i, j: (0, i, j)), blk]
        args = (pick.reshape(1).astype(jnp.int32), a, b)
    return pl.pallas_call(body, out_shape=jax.ShapeDtypeStruct((R, C), out_dtype), grid=(R // br, C // bc), in_specs=in_specs,
                          out_specs=blk, name=name, compiler_params=_cp(("parallel", "parallel")))(*args)


def _adamw_math(w, g, m, v):
    m = ADAM_B1 * m + (1.0 - ADAM_B1) * g
    v = ADAM_B2 * v + (1.0 - ADAM_B2) * (g * g)
    m_hat = m / (1.0 - ADAM_B1 ** ADAM_STEP)
    v_hat = v / (1.0 - ADAM_B2 ** ADAM_STEP)
    delta = -ADAM_LR * (m_hat / (jnp.sqrt(v_hat) + ADAM_EPS) + ADAM_WD * w)
    return delta, m, v


def _adamw(w, gs, m, v, name, pick=None):
    R, C = w.shape
    br, bc = _tile2d(R, C)
    blk = pl.BlockSpec((br, bc), lambda i, j: (i, j))
    if pick is None:
        g_specs = [pl.BlockSpec((g.shape[0], br, bc), lambda i, j: (0, i, j)) if g.ndim == 3 else blk for g in gs]
        lead = ()
    else:
        nb = (R // 2) // br
        assert nb * br * 2 == R
        g_specs = [pl.BlockSpec((br, bc), lambda i, j: (i % nb, j))] * 2
        lead = (pick.reshape(1).astype(jnp.int32),)

    def body(*refs):
        if pick is not None:
            c_ref, refs = refs[0], refs[1:]
        w_ref = refs[0]
        g_refs = refs[1:1 + len(gs)]
        m_ref, v_ref, g_out, d_out, m_out, v_out = refs[1 + len(gs):]
        if pick is None:
            g = None
            for gr in g_refs:
                parts = [gr[i] for i in range(gr.shape[0])] if len(gr.shape) == 3 else [gr[...]]
                for p in parts:
                    g = p if g is None else g + p
        else:
            g = jnp.where(pl.program_id(0) // nb == c_ref[0], g_refs[0][...], g_refs[1][...])
        d, mn, vn = _adamw_math(w_ref[...], g, m_ref[...], v_ref[...])
        g_out[...] = g
        d_out[...] = d
        m_out[...] = mn
        v_out[...] = vn

    return pl.pallas_call(
        body, out_shape=tuple(jax.ShapeDtypeStruct((R, C), F32) for _ in range(4)), grid=(R // br, C // bc),
        in_specs=[_SMEM] * len(lead) + [blk] + g_specs + [blk, blk], out_specs=(blk,) * 4, name=name,
        compiler_params=_cp(("parallel", "parallel")))(*lead, w, *gs, m, v)


WEIGHTS = ("norm_g", "w_in", "conv_w", "conv_b", "conv_ln_g", "conv_ln_b", "na_q_g", "na_k_g", "na_rpb", "gla_a2_f",
           "gla_ab_f", "gla_a2_b", "gla_ab_b", "gla_o_g", "pool_w", "pool_scale", "w_out")
_REPL = ("norm_g", "conv_b", "conv_ln_g", "conv_ln_b", "na_q_g", "na_k_g", "na_rpb", "gla_ab_f", "gla_ab_b", "gla_o_g",
         "pool_w", "pool_scale")
_SHARD_SMALL = ("conv_w", "gla_a2_f", "gla_a2_b")
_PACK_ROWS = 8 * 128


def _pack(arrs):
    flat = jnp.concatenate([a.reshape(-1) for a in arrs])
    n = -(-flat.shape[0] // _PACK_ROWS) * _PACK_ROWS
    return jnp.pad(flat, (0, n - flat.shape[0])).reshape(-1, 128)


def _unpack(p, shapes):
    flat = p.reshape(-1)
    out, o = [], 0
    for s in shapes:
        n = int(np.prod(s))
        out.append(flat[o:o + n].reshape(s))
        o += n
    return out


def _to_layout_rows(w):
    pad = jnp.zeros((w.shape[0], NZ - N_IN, w.shape[2]), w.dtype)
    return jnp.concatenate([w[:, :5120], w[:, 5152:6176], w[:, 5120:5152], pad], axis=1)


def _from_layout_rows(w):
    return jnp.concatenate([w[:, :5120], w[:, LR_OFF:LR_OFF + 32], w[:, 5120:LR_OFF]], axis=1)


def _reduce_gradients(p_a, p_b, small_g, ci):
    s_a, s_b, s_small = _sibling_exchange((p_a, p_b), (small_g,), "grad_to_sibling")
    flat = lambda a: a.reshape(a.shape[0], -1, a.shape[-1])
    c_a = _add2(flat(p_a), s_a.reshape(-1, s_a.shape[-1]), BF16, "chip_sum_a", pick=ci).reshape(s_a.shape)
    c_b = _add2(flat(p_b), s_b.reshape(-1, s_b.shape[-1]), BF16, "chip_sum_b", pick=ci).reshape(s_b.shape)
    c_small = _add2(small_g, s_small, F32, "chip_sum_small")
    r_a, r_b, r_small = _chip_exchange((c_a, c_b), (c_small,), "grad_to_owner")
    own_a = _sum_slots(r_a, "sum_a")
    own_b = _sum_slots(r_b, "sum_b")
    sib_a, sib_b = _sibling_exchange((), (own_a, own_b), "reduced_to_sibling")
    return (own_a, sib_a), (own_b, sib_b), r_small


def _layer_fwd(l, x, P, S, target=None):
    n = f"l{l}_"
    h = _rmsnorm_fwd(x, P["norm_g"], n + "rms_fwd")
    z = _matmul(h, P["w_in"], dims="nn", out_dtype=F32, tm=1024, tn=1280, tk=D_MODEL, name=n + "mm_z")
    yc = _conv_fwd(z, P["conv_w32"], P["conv_b"], S, n + "conv_fwd")
    pre_a = _ln_silu_fwd(yc, P["conv_ln_g"], P["conv_ln_b"], n + "ln_fwd")
    pre_b = _na_fwd(z, P["na_q_g"], P["na_k_g"], P["na_bias"], S, n + "na_fwd")
    bf, ecf = _gla_decay_fwd(z, P["a2_f"], P["gla_ab_f"], False, n + "gla_decay_f")
    bb, ecb = _gla_decay_fwd(z, P["a2_b"], P["gla_ab_b"], True, n + "gla_decay_b")
    (of, af, sf), (ob, ab, sb) = _gla_fwd_both(z, bf, ecf, bb, ecb, S, n + "gla_fwd")
    pre_c = _gla_norm_fwd(of, ob, P["gla_o_g"], n + "gla_norm_fwd")
    pre_d = _pool_fwd(z, P["pool_w_bf"], P["pool_scale"], S, n + "pool_fwd")
    pres = (pre_a, pre_b, pre_c, pre_d)
    res = _out_proj_fwd(pres, z, P["w_out"], x, target, n + "out_proj")
    y, out = res[0], (res[1] if target is None else res[1:])
    return out, dict(x=x, h=h, z=z, yc=yc, pres=pres, of=of, af=af, sf=sf, ob=ob, ab=ab, sb=sb, y=y, bf=bf, ecf=ecf, bb=bb, ecb=ecb)


def _layer_bwd(l, dout, dout_bf, sv, P, S):
    n = f"l{l}_"
    z = sv["z"]
    T = z.shape[0]
    d_w_out = _matmul(sv["y"], dout_bf, dims="tn", out_dtype=BF16, tm=1024, tn=2048, tk=512, name=n + "mm_dwout")
    dpa, dpb, dpc, dpd, dga, dgb, dgc, dgd = _out_proj_bwd(dout_bf, P["w_out_t"], sv["pres"], z, n + "out_proj_bwd")
    dyc, d_ln_g, d_ln_b = _ln_silu_bwd(sv["yc"], P["conv_ln_g"], P["conv_ln_b"], dpa, n + "ln_bwd")
    dval, dglu, d_cw, d_cb = _conv_bwd(z, P["conv_w32"], dyc, S, n + "conv_bwd")
    dq, dk, dv, dbias, d_qg, d_kg = _na_bwd(z, P["na_q_g"], P["na_k_g"], P["na_bias"], dpb, S, n + "na_bwd")
    d_rpb = _na_rpb_grad(dbias, n + "na_rpb")
    do, d_og = _gla_norm_bwd(sv["of"], sv["ob"], P["gla_o_g"], dpc, n + "gla_norm_bwd")
    gf, gb_ = _gla_bwd_both(z, (sv["bf"], sv["ecf"], sv["af"], sv["sf"]), (sv["bb"], sv["ecb"], sv["ab"], sv["sb"]), do, S,
                            n + "gla_bwd")
    dcq, dck, dcv = zip(gf[:3], gb_[:3])
    dlr, d_a2f, d_abf, d_a2b, d_abb = _gla_decay_bwd(z, P["a2_f"], P["gla_ab_f"], P["a2_b"], P["gla_ab_b"], gf[3], gb_[3],
                                                     n + "gla_decay_bwd")
    dd, d_pw, d_ps = _pool_bwd(z, P["pool_w_bf"], P["pool_scale"], dpd, S, n + "pool_bwd")
    dz = _concat_cols([dval, dglu, dga, dq, dk, dv, dgb, dcq, dck, dcv, dgc, dd, dgd, dlr], NZ, n + "dz_concat")
    dh = _matmul(dz, P["w_in_t"], dims="nn", out_dtype=F32, tm=1024, tn=1024, tk=3200, name=n + "mm_dh")
    d_w_in = _matmul(dz, sv["h"], dims="tn", out_dtype=BF16, tm=1280, tn=1024, tk=1024, name=n + "mm_dwin")
    dx, dx_bf, d_ng = _rmsnorm_bwd(sv["x"], P["norm_g"], dh, dout, n + "rms_bwd")
    grads = dict(norm_g=d_ng[0], w_in=d_w_in, conv_w=d_cw[:CONV_K], conv_b=d_cb[0], conv_ln_g=d_ln_g[0], conv_ln_b=d_ln_b[0],
                 na_q_g=d_qg.reshape(NA_HEADS, NA_DH), na_k_g=d_kg.reshape(NA_HEADS, NA_DH), na_rpb=d_rpb,
                 gla_a2_f=d_a2f[0:GLA_RANK], gla_ab_f=d_abf[0], gla_a2_b=d_a2b[GLA_RANK:2 * GLA_RANK], gla_ab_b=d_abb[0],
                 gla_o_g=d_og.reshape(GLA_HEADS, GLA_DV), pool_w=d_pw, pool_scale=d_ps[0], w_out=d_w_out)
    return dx, dx_bf, grads


def kernel(x, norm_g, w_in, conv_w, conv_b, conv_ln_g, conv_ln_b, na_q_g, na_k_g, na_rpb, gla_a2_f, gla_ab_f, gla_a2_b, gla_ab_b, gla_o_g, pool_w, pool_scale, w_out, loss_target, m_norm_g, m_w_in, m_conv_w, m_conv_b, m_conv_ln_g, m_conv_ln_b, m_na_q_g, m_na_k_g, m_na_rpb, m_gla_a2_f, m_gla_ab_f, m_gla_a2_b, m_gla_ab_b, m_gla_o_g, m_pool_w, m_pool_scale, m_w_out, v_norm_g, v_w_in, v_conv_w, v_conv_b, v_conv_ln_g, v_conv_ln_b, v_na_q_g, v_na_k_g, v_na_rpb, v_gla_a2_f, v_gla_ab_f, v_gla_a2_b, v_gla_ab_b, v_gla_o_g, v_pool_w, v_pool_scale, v_w_out):
    W = dict(norm_g=norm_g, w_in=w_in, conv_w=conv_w, conv_b=conv_b, conv_ln_g=conv_ln_g, conv_ln_b=conv_ln_b, na_q_g=na_q_g,
             na_k_g=na_k_g, na_rpb=na_rpb, gla_a2_f=gla_a2_f, gla_ab_f=gla_ab_f, gla_a2_b=gla_a2_b, gla_ab_b=gla_ab_b,
             gla_o_g=gla_o_g, pool_w=pool_w, pool_scale=pool_scale, w_out=w_out)
    M = dict(norm_g=m_norm_g, w_in=m_w_in, conv_w=m_conv_w, conv_b=m_conv_b, conv_ln_g=m_conv_ln_g, conv_ln_b=m_conv_ln_b,
             na_q_g=m_na_q_g, na_k_g=m_na_k_g, na_rpb=m_na_rpb, gla_a2_f=m_gla_a2_f, gla_ab_f=m_gla_ab_f, gla_a2_b=m_gla_a2_b,
             gla_ab_b=m_gla_ab_b, gla_o_g=m_gla_o_g, pool_w=m_pool_w, pool_scale=m_pool_scale, w_out=m_w_out)
    V = dict(norm_g=v_norm_g, w_in=v_w_in, conv_w=v_conv_w, conv_b=v_conv_b, conv_ln_g=v_conv_ln_g, conv_ln_b=v_conv_ln_b,
             na_q_g=v_na_q_g, na_k_g=v_na_k_g, na_rpb=v_na_rpb, gla_a2_f=v_gla_a2_f, gla_ab_f=v_gla_ab_f, gla_a2_b=v_gla_a2_b,
             gla_ab_b=v_gla_ab_b, gla_o_g=v_gla_o_g, pool_w=v_pool_w, pool_scale=v_pool_scale, w_out=v_w_out)
    E, S, D = x.shape
    T = E * S
    L = DEPTH
    xi, yi, ci = _coords()
    chip = 2 * xi + yi
    cw_sh, a2_sh = conv_w.shape[-1], gla_a2_f.shape[-1]

    small_sh = jnp.concatenate([
        jnp.pad(conv_w, ((0, 0), (0, 1), (0, 0))),
        jnp.pad(gla_a2_f, ((0, 0), (0, 0), (0, 128 - a2_sh))),
        jnp.pad(gla_a2_b, ((0, 0), (0, 0), (0, 128 - a2_sh)))], axis=1)
    w_in_tr, m_w_in_tr, v_w_in_tr = (jnp.transpose(a, (0, 2, 1)) for a in (w_in, m_w_in, v_w_in))
    g_win, g_wout, g_small = _gather_weights((w_in_tr.astype(BF16), w_out.astype(BF16), small_sh), (1, 2, 1), "gather_weights")
    w_in_t_full = _to_layout_rows(jnp.transpose(g_win, (1, 0, 2, 3)).reshape(L, N_IN, D))
    w_in_full = jnp.transpose(w_in_t_full, (0, 2, 1))
    w_out_full = jnp.transpose(g_wout, (1, 0, 2, 3)).reshape(L, D, D)
    conv_w_full = jnp.transpose(g_small[:, :, 0:32, :], (1, 2, 0, 3)).reshape(L, 32, 4 * cw_sh)
    a2f_full = jnp.transpose(g_small[:, :, 32:48, :a2_sh], (1, 2, 0, 3)).reshape(L, GLA_RANK, 4 * a2_sh)
    a2b_full = jnp.transpose(g_small[:, :, 48:64, :a2_sh], (1, 2, 0, 3)).reshape(L, GLA_RANK, 4 * a2_sh)

    params = []
    for l in range(L):
        params.append(dict(
            norm_g=norm_g[l][None], w_in=w_in_full[l], w_out=w_out_full[l], w_in_t=w_in_t_full[l], w_out_t=w_out_full[l].T,
            conv_w32=conv_w_full[l], conv_b=conv_b[l][None],
            conv_ln_g=conv_ln_g[l][None], conv_ln_b=conv_ln_b[l][None], na_q_g=na_q_g[l].reshape(1, GROUP_W),
            na_k_g=na_k_g[l].reshape(1, GROUP_W), na_bias=_na_bias(na_rpb[l], f"l{l}_na_bias"),
            a2_f=jnp.zeros((128, _HK), F32).at[0:GLA_RANK].set(a2f_full[l]),
            a2_b=jnp.zeros((128, _HK), F32).at[GLA_RANK:2 * GLA_RANK].set(a2b_full[l]),
            gla_ab_f=gla_ab_f[l][None], gla_ab_b=gla_ab_b[l][None], gla_o_g=gla_o_g[l].reshape(1, GROUP_W),
            pool_w_bf=pool_w[l].astype(BF16), pool_scale=pool_scale[l][None]))

    act = x.reshape(T, D)
    saved = []
    for l in range(L):
        act, sv = _layer_fwd(l, act, params[l], S, loss_target.reshape(T, D) if l == L - 1 else None)
        saved.append(sv)
    dact, dact_bf, loss_loc = act
    loss = lax.psum(loss_loc[0, 0], ("x", "y", "c"))
    grads = [None] * L
    for l in reversed(range(L)):
        dact, dact_bf, grads[l] = _layer_bwd(l, dact, dact_bf, saved[l], params[l], S)
    grad_x = dact.reshape(E, S, D)
    G = {k: jnp.stack([grads[l][k] for l in range(L)]) for k in WEIGHTS}

    cols_in, cols_out = N_IN // 4, D
    p_win = _from_layout_rows(G["w_in"]).reshape(L, 4, cols_in, D)
    p_wout = G["w_out"].reshape(L, 4, D // 4, D)
    small_names = _REPL + _SHARD_SMALL
    small_g = _pack([G[k] for k in small_names])
    g_in, g_out, r_small = _reduce_gradients(p_win, p_wout, small_g, ci)

    rows_in, rows_out = L * cols_in, L * (D // 4)
    res = {}
    res["w_in"] = [jnp.transpose(a.reshape(L, cols_in, D), (0, 2, 1)) for a in _adamw(
        w_in_tr.reshape(rows_in, D), g_in, m_w_in_tr.reshape(rows_in, D), v_w_in_tr.reshape(rows_in, D), "adamw_w_in", pick=ci)]
    res["w_out"] = [a.reshape(L, D // 4, D) for a in _adamw(
        w_out.reshape(rows_out, cols_out), g_out, m_w_out.reshape(rows_out, cols_out),
        v_w_out.reshape(rows_out, cols_out), "adamw_w_out", pick=ci)]
    zeros_sh = [jnp.zeros(G[k].shape, F32) for k in _SHARD_SMALL]
    pk = lambda dct: _pack([dct[k] for k in _REPL] + zeros_sh)
    small_res = _adamw(pk(W), (r_small,), pk(M), pk(V), "adamw_small")
    shapes = [G[k].shape for k in small_names]
    unp = [_unpack(a, shapes) for a in small_res]
    for i, k in enumerate(_REPL):
        res[k] = [u[i] for u in unp]
    g_sh = []
    for i, k in enumerate(_SHARD_SMALL):
        gfull = unp[0][len(_REPL) + i]
        wdt = W[k].shape[-1]
        g_sh.append(lax.dynamic_slice_in_dim(gfull, chip * wdt, wdt, axis=2))
    g_sh_p = _pack(g_sh)
    sh_res = _adamw(_pack([W[k] for k in _SHARD_SMALL]), (g_sh_p,), _pack([M[k] for k in _SHARD_SMALL]),
                    _pack([V[k] for k in _SHARD_SMALL]), "adamw_shard_small")
    shapes2 = [W[k].shape for k in _SHARD_SMALL]
    unp2 = [_unpack(a, shapes2) for a in sh_res]
    for i, k in enumerate(_SHARD_SMALL):
        res[k] = [u[i] for u in unp2]

    outs = [loss, grad_x]
    for j in range(4):
        outs += [res[k][j] for k in WEIGHTS]
    return tuple(outs)
```

```python
import functools

import numpy as np
import jax
import jax.numpy as jnp
from jax import lax
from jax.experimental import pallas as pl
from jax.experimental.pallas import tpu as pltpu

F32 = jnp.float32
BF16 = jnp.bfloat16
HI = lax.Precision.HIGHEST
MESH = pl.DeviceIdType.MESH

EPS = 1e-6
D_MODEL = 2048
GROUP_W = 512
SEQ = 2048
DEPTH = 2
N_IN = 6176
GRID_W = 64
CONV_K = 31
NA_HEADS = 8
NA_DH = 64
NA_ROWS = 8
NA_COLS = 16
GLA_HEADS = 4
GLA_DK = 64
GLA_DV = 128
GLA_RANK = 16
GLA_TAU = 16.0
CHUNK = 64
POOL_WINDOWS = (2, 4, 8, 16)
ADAM_LR, ADAM_B1, ADAM_B2, ADAM_EPS, ADAM_WD, ADAM_STEP = 0.001, 0.9, 0.999, 1e-08, 0.01, 10

A_VAL, A_GLU, A_GATE = 0, 512, 1024
B_Q, B_K, B_V, B_GATE = 1536, 2048, 2560, 3072
C_Q, C_K, C_V, C_GATE = 3584, 3840, 4096, 4608
D_VAL, D_GATE = 5120, 5632
LR_OFF = 6144
NZ = 6400
NEG = -1e30
VMEM_LIMIT = 56 * 1024 * 1024


def _cp(sem=None):
    return pltpu.CompilerParams(dimension_semantics=sem, vmem_limit_bytes=VMEM_LIMIT)


def _sigmoid(x):
    return 1.0 / (1.0 + jnp.exp(-x))


def _silu(x):
    return x * _sigmoid(x)


def _dsilu(x):
    s = _sigmoid(x)
    return s * (1.0 + x * (1.0 - s))


def _matmul(a, b, *, dims, out_dtype, tm, tn, tk, name, res=None):
    if dims == "nn":
        (M, K), N = a.shape, b.shape[1]
    elif dims == "nt":
        (M, K), N = a.shape, b.shape[0]
    else:
        (K, M), N = a.shape, b.shape[1]
    tm, tn, tk = min(tm, M), min(tn, N), min(tk, K)
    nk = K // tk
    assert M % tm == 0 and N % tn == 0 and K % tk == 0, (M, N, K, tm, tn, tk)
    dn = {"nn": (((1,), (0,)), ((), ())), "nt": (((1,), (1,)), ((), ())), "tn": (((0,), (0,)), ((), ()))}[dims]
    if dims == "tn":
        a_spec = pl.BlockSpec((tk, tm), lambda i, j, k: (k, i))
    else:
        a_spec = pl.BlockSpec((tm, tk), lambda i, j, k: (i, k))
    if dims == "nt":
        b_spec = pl.BlockSpec((tn, tk), lambda i, j, k: (j, k))
    else:
        b_spec = pl.BlockSpec((tk, tn), lambda i, j, k: (k, j))
    o_spec = pl.BlockSpec((tm, tn), lambda i, j, k: (i, j))
    has_res = res is not None

    def body(*refs):
        if has_res:
            a_ref, b_ref, r_ref, o_ref, acc = refs
        else:
            a_ref, b_ref, o_ref, acc = refs
        k = pl.program_id(2)

        @pl.when(k == 0)
        def _():
            acc[...] = jnp.zeros_like(acc)

        acc[...] += lax.dot_general(a_ref[...], b_ref[...], dn, preferred_element_type=F32)

        @pl.when(k == nk - 1)
        def _():
            r = acc[...]
            if has_res:
                r = r + r_ref[...]
            o_ref[...] = r.astype(o_ref.dtype)

    in_specs = [a_spec, b_spec] + ([o_spec] if has_res else [])
    args = (a, b) + ((res,) if has_res else ())
    return pl.pallas_call(
        body, out_shape=jax.ShapeDtypeStruct((M, N), out_dtype), grid=(M // tm, N // tn, nk),
        in_specs=in_specs, out_specs=o_spec, scratch_shapes=[pltpu.VMEM((tm, tn), F32)],
        name=name, compiler_params=_cp(("parallel", "parallel", "arbitrary")))(*args)


def _concat_cols(pieces, width, name):
    pairs = [p if isinstance(p, tuple) else (p,) for p in pieces]
    T = pairs[0][0].shape[0]
    tm = min(512, T)
    dt = BF16
    offs = np.cumsum([0] + [p[0].shape[1] for p in pairs])
    flat = [a for p in pairs for a in p]

    def body(*refs):
        o_ref = refs[-1]
        k = 0
        for p, a, b in zip(pairs, offs[:-1], offs[1:]):
            val = refs[k][...] if len(p) == 1 else refs[k][...] + refs[k + 1][...]
            o_ref[:, a:b] = val.astype(dt)
            k += len(p)
        if offs[-1] < width:
            o_ref[:, offs[-1]:width] = jnp.zeros((tm, width - offs[-1]), dt)

    return pl.pallas_call(
        body, out_shape=jax.ShapeDtypeStruct((T, width), dt), grid=(T // tm,),
        in_specs=[pl.BlockSpec((tm, a.shape[1]), lambda i: (i, 0)) for a in flat],
        out_specs=pl.BlockSpec((tm, width), lambda i: (i, 0)), name=name, compiler_params=_cp(("parallel",)))(*flat)


def _rmsnorm_fwd(x, g, name):
    T, D = x.shape
    tm = 256

    def body(x_ref, g_ref, h_ref):
        xv = x_ref[...]
        r = lax.rsqrt(jnp.mean(xv * xv, axis=-1, keepdims=True) + EPS)
        h_ref[...] = (xv * r * g_ref[...]).astype(h_ref.dtype)

    return pl.pallas_call(
        body, out_shape=jax.ShapeDtypeStruct((T, D), BF16), grid=(T // tm,),
        in_specs=[pl.BlockSpec((tm, D), lambda i: (i, 0)), pl.BlockSpec((1, D), lambda i: (0, 0))],
        out_specs=pl.BlockSpec((tm, D), lambda i: (i, 0)), name=name, compiler_params=_cp(("parallel",)))(x, g)


def _rmsnorm_bwd(x, g, dh, dres, name):
    T, D = x.shape
    tm = 256

    def body(x_ref, g_ref, dh_ref, dres_ref, dx_ref, dxb_ref, dg_ref):
        xv = x_ref[...]
        r = lax.rsqrt(jnp.mean(xv * xv, axis=-1, keepdims=True) + EPS)
        xh = xv * r
        dh_v = dh_ref[...]
        dxh = dh_v * g_ref[...]
        dx = r * (dxh - xh * jnp.mean(dxh * xh, axis=-1, keepdims=True)) + dres_ref[...]
        dx_ref[...] = dx
        dxb_ref[...] = dx.astype(BF16)

        @pl.when(pl.program_id(0) == 0)
        def _():
            dg_ref[...] = jnp.zeros_like(dg_ref)

        dg_ref[...] += jnp.sum(dh_v * xh, axis=0, keepdims=True)

    row = pl.BlockSpec((tm, D), lambda i: (i, 0))
    vec = pl.BlockSpec((1, D), lambda i: (0, 0))
    return pl.pallas_call(
        body, out_shape=(jax.ShapeDtypeStruct((T, D), F32), jax.ShapeDtypeStruct((T, D), BF16), jax.ShapeDtypeStruct((1, D), F32)),
        grid=(T // tm,), in_specs=[row, vec, row, row], out_specs=(row, row, vec), name=name,
        compiler_params=_cp(("arbitrary",)))(x, g, dh, dres)


_GATE_COLS = (A_GATE // GROUP_W, B_GATE // GROUP_W, C_GATE // GROUP_W, D_GATE // GROUP_W)
_OP_TM = 256


def _out_proj_fwd(pres, z, w_out, x, target, name):
    T, D = x.shape
    tm = min(_OP_TM, T)
    with_loss = target is not None

    def body(*refs):
        pa, pb, pc, pd, ga, gb, gc, gd, w_ref, x_ref = refs[:10]
        refs = refs[10:]
        if with_loss:
            t_ref, y_ref, d_ref, db_ref, l_ref = refs
        else:
            y_ref, o_ref = refs
        for n_, (p, g) in enumerate(((pa, ga), (pb, gb), (pc, gc), (pd, gd))):
            y_ref[:, n_ * GROUP_W:(n_ + 1) * GROUP_W] = (p[...] * _silu(g[...])).astype(BF16)
        out = jnp.dot(y_ref[...], w_ref[...], preferred_element_type=F32) + x_ref[...]
        if with_loss:
            e = out - t_ref[...]
            d = e * (1.0 / D)
            d_ref[...] = d
            db_ref[...] = d.astype(BF16)

            @pl.when(pl.program_id(0) == 0)
            def _():
                l_ref[...] = jnp.zeros_like(l_ref)

            l_ref[...] += jnp.sum(jnp.sum(e * e, axis=-1, keepdims=True) * (0.5 / D), axis=0, keepdims=True)
        else:
            o_ref[...] = out

    pre_spec = pl.BlockSpec((tm, GROUP_W), lambda i: (i, 0))
    gate_specs = [pl.BlockSpec((tm, GROUP_W), functools.partial(lambda i, c: (i, c), c=c)) for c in _GATE_COLS]
    row = pl.BlockSpec((tm, D), lambda i: (i, 0))
    w_spec = pl.BlockSpec((4 * GROUP_W, D), lambda i: (0, 0))
    in_specs = [pre_spec] * 4 + gate_specs + [w_spec, row]
    args = list(pres) + [z, z, z, z, w_out, x]
    if with_loss:
        in_specs.append(row)
        args.append(target)
        out_shape = (jax.ShapeDtypeStruct((T, D), BF16), jax.ShapeDtypeStruct((T, D), F32), jax.ShapeDtypeStruct((T, D), BF16),
                     jax.ShapeDtypeStruct((1, 1), F32))
        out_specs = (row, row, row, pl.BlockSpec((1, 1), lambda i: (0, 0)))
    else:
        out_shape = (jax.ShapeDtypeStruct((T, D), BF16), jax.ShapeDtypeStruct((T, D), F32))
        out_specs = (row, row)
    return pl.pallas_call(body, out_shape=out_shape, grid=(T // tm,), in_specs=in_specs, out_specs=out_specs, name=name,
                          compiler_params=_cp(("arbitrary",)))(*args)


def _out_proj_bwd(dout_bf, w_out_t, pres, z, name):
    T, D = dout_bf.shape
    tm = min(_OP_TM, T)

    def body(do_ref, w_ref, pa, pb, pc, pd, ga, gb, gc, gd, dpa, dpb, dpc, dpd, dga, dgb, dgc, dgd):
        dy = jnp.dot(do_ref[...], w_ref[...], preferred_element_type=F32)
        for n_, (p, g, dp, dg) in enumerate(((pa, ga, dpa, dga), (pb, gb, dpb, dgb), (pc, gc, dpc, dgc), (pd, gd, dpd, dgd))):
            d = dy[:, n_ * GROUP_W:(n_ + 1) * GROUP_W]
            gv = g[...]
            dp[...] = d * _silu(gv)
            dg[...] = (d * p[...] * _dsilu(gv)).astype(BF16)

    pre_spec = pl.BlockSpec((tm, GROUP_W), lambda i: (i, 0))
    gate_specs = [pl.BlockSpec((tm, GROUP_W), functools.partial(lambda i, c: (i, c), c=c)) for c in _GATE_COLS]
    outs = tuple([jax.ShapeDtypeStruct((T, GROUP_W), F32)] * 4 + [jax.ShapeDtypeStruct((T, GROUP_W), BF16)] * 4)
    return pl.pallas_call(
        body, out_shape=outs, grid=(T // tm,),
        in_specs=[pl.BlockSpec((tm, D), lambda i: (i, 0)), pl.BlockSpec((D, 4 * GROUP_W), lambda i: (0, 0))] + [pre_spec] * 4 + gate_specs,
        out_specs=tuple([pre_spec] * 8), name=name, compiler_params=_cp(("parallel",)))(dout_bf, w_out_t, *pres, z, z, z, z)


_PAD = 16
_RC = 256


def _conv_fwd(z, conv_w32, conv_b, S, name):
    T = z.shape[0]
    E = T // S
    LW = 128

    def body(val_ref, glu_ref, w_ref, b_ref, y_ref, upad):
        upad[0:_PAD, :] = jnp.zeros((_PAD, LW), F32)
        upad[_PAD + S:_PAD + S + _PAD, :] = jnp.zeros((_PAD, LW), F32)
        upad[_PAD:_PAD + S, :] = val_ref[...] * _sigmoid(glu_ref[...])
        for r in range(S // _RC):
            acc = jnp.broadcast_to(b_ref[...], (_RC, LW))
            for k in range(CONV_K):
                st = r * _RC + k + 1
                acc = acc + upad[st:st + _RC, :] * w_ref[k:k + 1, :]
            y_ref[r * _RC:(r + 1) * _RC, :] = acc

    return pl.pallas_call(
        body, out_shape=jax.ShapeDtypeStruct((T, GROUP_W), F32), grid=(E, GROUP_W // LW),
        in_specs=[pl.BlockSpec((S, LW), lambda e, j: (e, A_VAL // LW + j)),
                  pl.BlockSpec((S, LW), lambda e, j: (e, A_GLU // LW + j)),
                  pl.BlockSpec((32, LW), lambda e, j: (0, j)),
                  pl.BlockSpec((1, LW), lambda e, j: (0, j))],
        out_specs=pl.BlockSpec((S, LW), lambda e, j: (e, j)),
        scratch_shapes=[pltpu.VMEM((S + 2 * _PAD, LW), F32)],
        name=name, compiler_params=_cp(("parallel", "parallel")))(z, z, conv_w32, conv_b)


def _conv_bwd(z, conv_w32, dyc, S, name):
    T = z.shape[0]
    E = T // S
    LW = 128

    def body(val_ref, glu_ref, w_ref, dy_ref, dval_ref, dglu_ref, dw_ref, db_ref, upad, dpad):
        e = pl.program_id(1)
        zeros = jnp.zeros((_PAD, LW), F32)
        upad[0:_PAD, :] = zeros
        upad[_PAD + S:_PAD + S + _PAD, :] = zeros
        dpad[0:_PAD, :] = zeros
        dpad[_PAD + S:_PAD + S + _PAD, :] = zeros
        upad[_PAD:_PAD + S, :] = val_ref[...] * _sigmoid(glu_ref[...])
        dpad[_PAD:_PAD + S, :] = dy_ref[...]

        @pl.when(e == 0)
        def _():
            dw_ref[...] = jnp.zeros_like(dw_ref)
            db_ref[...] = jnp.zeros_like(db_ref)

        db_ref[...] += jnp.sum(dy_ref[...], axis=0, keepdims=True)
        for r in range(S // _RC):
            dyr = dy_ref[r * _RC:(r + 1) * _RC, :]
            du = jnp.zeros((_RC, LW), F32)
            for k in range(CONV_K):
                st = r * _RC + k + 1
                dw_ref[k:k + 1, :] += jnp.sum(dyr * upad[st:st + _RC, :], axis=0, keepdims=True)
                sd = r * _RC + (CONV_K - 1 - k) + 1
                du = du + dpad[sd:sd + _RC, :] * w_ref[k:k + 1, :]
            sl = slice(r * _RC, (r + 1) * _RC)
            val = val_ref[sl, :]
            sg = _sigmoid(glu_ref[sl, :])
            dval_ref[sl, :] = (du * sg).astype(BF16)
            dglu_ref[sl, :] = (du * val * sg * (1.0 - sg)).astype(BF16)

    blk = pl.BlockSpec((S, LW), lambda j, e: (e, j))
    return pl.pallas_call(
        body, out_shape=(jax.ShapeDtypeStruct((T, GROUP_W), BF16), jax.ShapeDtypeStruct((T, GROUP_W), BF16),
                         jax.ShapeDtypeStruct((32, GROUP_W), F32), jax.ShapeDtypeStruct((1, GROUP_W), F32)),
        grid=(GROUP_W // LW, E),
        in_specs=[pl.BlockSpec((S, LW), lambda j, e: (e, A_VAL // LW + j)),
                  pl.BlockSpec((S, LW), lambda j, e: (e, A_GLU // LW + j)),
                  pl.BlockSpec((32, LW), lambda j, e: (0, j)), blk],
        out_specs=(blk, blk, pl.BlockSpec((32, LW), lambda j, e: (0, j)), pl.BlockSpec((1, LW), lambda j, e: (0, j))),
        scratch_shapes=[pltpu.VMEM((S + 2 * _PAD, LW), F32), pltpu.VMEM((S + 2 * _PAD, LW), F32)],
        name=name, compiler_params=_cp(("parallel", "arbitrary")))(z, z, conv_w32, dyc)


def _ln_silu_fwd(yc, g, b, name):
    T, C = yc.shape
    tm = 256

    def body(y_ref, g_ref, b_ref, o_ref):
        y = y_ref[...]
        mu = jnp.mean(y, axis=-1, keepdims=True)
        yc_ = y - mu
        r = lax.rsqrt(jnp.mean(yc_ * yc_, axis=-1, keepdims=True) + EPS)
        o_ref[...] = _silu(yc_ * r * g_ref[...] + b_ref[...])

    row = pl.BlockSpec((tm, C), lambda i: (i, 0))
    vec = pl.BlockSpec((1, C), lambda i: (0, 0))
    return pl.pallas_call(body, out_shape=jax.ShapeDtypeStruct((T, C), F32), grid=(T // tm,),
                          in_specs=[row, vec, vec], out_specs=row, name=name, compiler_params=_cp(("parallel",)))(yc, g, b)


def _ln_silu_bwd(yc, g, b, dpre, name):
    T, C = yc.shape
    tm = 256

    def body(y_ref, g_ref, b_ref, dp_ref, dy_ref, dg_ref, db_ref):
        y = y_ref[...]
        mu = jnp.mean(y, axis=-1, keepdims=True)
        yc_ = y - mu
        r = lax.rsqrt(jnp.mean(yc_ * yc_, axis=-1, keepdims=True) + EPS)
        xh = yc_ * r
        gv = g_ref[...]
        dln = dp_ref[...] * _dsilu(xh * gv + b_ref[...])
        dxh = dln * gv
        dy_ref[...] = r * (dxh - jnp.mean(dxh, axis=-1, keepdims=True) - xh * jnp.mean(dxh * xh, axis=-1, keepdims=True))

        @pl.when(pl.program_id(0) == 0)
        def _():
            dg_ref[...] = jnp.zeros_like(dg_ref)
            db_ref[...] = jnp.zeros_like(db_ref)

        dg_ref[...] += jnp.sum(dln * xh, axis=0, keepdims=True)
        db_ref[...] += jnp.sum(dln, axis=0, keepdims=True)

    row = pl.BlockSpec((tm, C), lambda i: (i, 0))
    vec = pl.BlockSpec((1, C), lambda i: (0, 0))
    return pl.pallas_call(
        body, out_shape=(jax.ShapeDtypeStruct((T, C), F32), jax.ShapeDtypeStruct((1, C), F32), jax.ShapeDtypeStruct((1, C), F32)),
        grid=(T // tm,), in_specs=[row, vec, vec, row], out_specs=(row, vec, vec), name=name,
        compiler_params=_cp(("arbitrary",)))(yc, g, b, dpre)


def _pool_counts(S, w, rows0, n):
    t = (lax.broadcasted_iota(jnp.int32, (n, 1), 0) + rows0)
    lo = jnp.maximum(t - w // 2, 0)
    hi = jnp.minimum(t + w // 2, S)
    return (hi - lo).astype(F32)


def _pool_fwd(z, pool_w, pool_scale, S, name):
    T = z.shape[0]
    E = T // S
    CG = 128

    def body(u_ref, w_ref, s_ref, o_ref, upad, dif):
        zeros = jnp.zeros((_PAD, GROUP_W), F32)
        upad[0:_PAD, :] = zeros
        upad[_PAD + S:_PAD + S + _PAD, :] = zeros
        upad[_PAD:_PAD + S, :] = u_ref[...]
        for gi, w in enumerate(POOL_WINDOWS):
            ls = slice(gi * CG, (gi + 1) * CG)
            for r in range(S // _RC):
                acc = jnp.zeros((_RC, CG), F32)
                for j in range(-(w // 2), w // 2):
                    st = _PAD + r * _RC + j
                    acc = acc + upad[st:st + _RC, ls]
                cnt = _pool_counts(S, w, r * _RC, _RC)
                dif[r * _RC:(r + 1) * _RC, :] = (acc / cnt - u_ref[r * _RC:(r + 1) * _RC, ls]).astype(BF16)
            yp = jnp.dot(dif[...], w_ref[gi], preferred_element_type=F32)
            o_ref[:, ls] = yp * s_ref[:, ls]

    return pl.pallas_call(
        body, out_shape=jax.ShapeDtypeStruct((T, GROUP_W), F32), grid=(E,),
        in_specs=[pl.BlockSpec((S, GROUP_W), lambda e: (e, D_VAL // GROUP_W)),
                  pl.BlockSpec((4, CG, CG), lambda e: (0, 0, 0)),
                  pl.BlockSpec((1, GROUP_W), lambda e: (0, 0))],
        out_specs=pl.BlockSpec((S, GROUP_W), lambda e: (e, 0)),
        scratch_shapes=[pltpu.VMEM((S + 2 * _PAD, GROUP_W), F32), pltpu.VMEM((S, CG), BF16)],
        name=name, compiler_params=_cp(("parallel",)))(z, pool_w, pool_scale)


def _pool_bwd(z, pool_w, pool_scale, dpre, S, name):
    T = z.shape[0]
    E = T // S
    CG = 128

    def body(u_ref, w_ref, s_ref, dp_ref, du_ref, dw_ref, ds_ref, upad, dif, qpad):
        zeros = jnp.zeros((_PAD, GROUP_W), F32)
        upad[0:_PAD, :] = zeros
        upad[_PAD + S:_PAD + S + _PAD, :] = zeros
        upad[_PAD:_PAD + S, :] = u_ref[...]
        zc = jnp.zeros((_PAD, CG), F32)
        qpad[0:_PAD, :] = zc
        qpad[_PAD + S:_PAD + S + _PAD, :] = zc

        @pl.when(pl.program_id(0) == 0)
        def _():
            dw_ref[...] = jnp.zeros_like(dw_ref)
            ds_ref[...] = jnp.zeros_like(ds_ref)

        for gi, w in enumerate(POOL_WINDOWS):
            ls = slice(gi * CG, (gi + 1) * CG)
            for r in range(S // _RC):
                acc = jnp.zeros((_RC, CG), F32)
                for j in range(-(w // 2), w // 2):
                    st = _PAD + r * _RC + j
                    acc = acc + upad[st:st + _RC, ls]
                cnt = _pool_counts(S, w, r * _RC, _RC)
                dif[r * _RC:(r + 1) * _RC, :] = (acc / cnt - u_ref[r * _RC:(r + 1) * _RC, ls]).astype(BF16)
            dp = dp_ref[:, ls]
            yp = jnp.dot(dif[...], w_ref[gi], preferred_element_type=F32)
            ds_ref[:, ls] += jnp.sum(dp * yp, axis=0, keepdims=True)
            dys = (dp * s_ref[:, ls]).astype(BF16)
            dw_ref[gi] += lax.dot_general(dif[...], dys, (((0,), (0,)), ((), ())), preferred_element_type=F32)
            dm = lax.dot_general(dys, w_ref[gi], (((1,), (1,)), ((), ())), preferred_element_type=F32)
            for r in range(S // _RC):
                cnt = _pool_counts(S, w, r * _RC, _RC)
                qpad[_PAD + r * _RC:_PAD + (r + 1) * _RC, :] = dm[r * _RC:(r + 1) * _RC, :] / cnt
            for r in range(S // _RC):
                acc = -dm[r * _RC:(r + 1) * _RC, :]
                for j in range(-(w // 2) + 1, w // 2 + 1):
                    st = _PAD + r * _RC + j
                    acc = acc + qpad[st:st + _RC, :]
                du_ref[r * _RC:(r + 1) * _RC, ls] = acc.astype(BF16)

    return pl.pallas_call(
        body, out_shape=(jax.ShapeDtypeStruct((T, GROUP_W), BF16), jax.ShapeDtypeStruct((4, CG, CG), F32),
                         jax.ShapeDtypeStruct((1, GROUP_W), F32)), grid=(E,),
        in_specs=[pl.BlockSpec((S, GROUP_W), lambda e: (e, D_VAL // GROUP_W)),
                  pl.BlockSpec((4, CG, CG), lambda e: (0, 0, 0)),
                  pl.BlockSpec((1, GROUP_W), lambda e: (0, 0)),
                  pl.BlockSpec((S, GROUP_W), lambda e: (e, 0))],
        out_specs=(pl.BlockSpec((S, GROUP_W), lambda e: (e, 0)), pl.BlockSpec((4, CG, CG), lambda e: (0, 0, 0)),
                   pl.BlockSpec((1, GROUP_W), lambda e: (0, 0))),
        scratch_shapes=[pltpu.VMEM((S + 2 * _PAD, GROUP_W), F32), pltpu.VMEM((S, CG), BF16),
                        pltpu.VMEM((S + 2 * _PAD, CG), F32)],
        name=name, compiler_params=_cp(("arbitrary",)))(z, pool_w, pool_scale, dpre)


def _na_tables():
    d = np.arange(NA_ROWS)[:, None]
    kr = np.arange(NA_ROWS)[None, :]
    ro = kr - d + (NA_ROWS - 1)
    qc = np.arange(GRID_W)[:, None]
    kc = np.arange(GRID_W)[None, :]
    cs = np.clip(qc - NA_COLS // 2, 0, GRID_W - NA_COLS)
    valid = (kc >= cs) & (kc < cs + NA_COLS)
    co = np.clip(kc - qc + (NA_COLS - 1), 0, 2 * NA_COLS - 2)
    return ro, co, valid


def _na_onehots():
    ro, co, valid = _na_tables()
    e_np = np.zeros((GRID_W, GRID_W, 128), np.float32)
    qi, ki = np.nonzero(valid)
    e_np[qi, ki, co[qi, ki]] = 1.0
    a_np = np.zeros((16, NA_ROWS * NA_ROWS), np.float32)
    a_np[ro.reshape(-1), np.arange(NA_ROWS * NA_ROWS)] = 1.0
    mask = np.where(valid, 0.0, NEG).astype(np.float32).reshape(1, GRID_W * GRID_W)
    return e_np.reshape(GRID_W * GRID_W, 128), a_np, mask


def _na_bias(rpb, name):
    e_np, _, mask = _na_onehots()
    H = NA_HEADS
    rp = jnp.pad(rpb, ((0, 0), (0, 1), (0, 128 - rpb.shape[2])))

    def bands(r_ref, e_ref, m_ref, o_ref):
        o_ref[0] = lax.dot_general(r_ref[0], e_ref[...], (((1,), (1,)), ((), ())), precision=HI,
                                   preferred_element_type=F32) + m_ref[...]

    t = pl.pallas_call(
        bands, out_shape=jax.ShapeDtypeStruct((H, 16, GRID_W * GRID_W), F32), grid=(H,),
        in_specs=[pl.BlockSpec((1, 16, 128), lambda h: (h, 0, 0)),
                  pl.BlockSpec((GRID_W * GRID_W, 128), lambda h: (0, 0)),
                  pl.BlockSpec((1, GRID_W * GRID_W), lambda h: (0, 0))],
        out_specs=pl.BlockSpec((1, 16, GRID_W * GRID_W), lambda h: (h, 0, 0)),
        name=name + "_bands", compiler_params=_cp(("parallel",)))(rp, jnp.asarray(e_np), jnp.asarray(mask))
    t = t.reshape(H, 16, GRID_W, GRID_W)

    def place(t_ref, o_ref):
        for d in range(NA_ROWS):
            for kr in range(NA_ROWS):
                o_ref[0, d, :, kr * GRID_W:(kr + 1) * GRID_W] = t_ref[0, kr - d + NA_ROWS - 1]

    return pl.pallas_call(
        place, out_shape=jax.ShapeDtypeStruct((H, NA_ROWS, GRID_W, NA_ROWS * GRID_W), F32), grid=(H,),
        in_specs=[pl.BlockSpec((1, 16, GRID_W, GRID_W), lambda h: (h, 0, 0, 0))],
        out_specs=pl.BlockSpec((1, NA_ROWS, GRID_W, NA_ROWS * GRID_W), lambda h: (h, 0, 0, 0)),
        name=name, compiler_params=_cp(("parallel",)))(t)


def _seg_mean_matrix(width, seg):
    i = np.arange(width)
    return jnp.asarray((i[:, None] // seg == i[None, :] // seg).astype(np.float32) / seg, BF16)


def _seg_mean(x, seg_ref):
    hi = x.astype(BF16)
    lo = (x - hi.astype(F32)).astype(BF16)
    return (jnp.dot(hi, seg_ref[...], preferred_element_type=F32) + jnp.dot(lo, seg_ref[...], preferred_element_type=F32))


def _na_fwd(z, qg, kg, bias, S, name):
    T = z.shape[0]
    E = T // S
    rows = S // GRID_W
    WIN = NA_ROWS * GRID_W
    seg = _seg_mean_matrix(128, NA_DH)

    def body(q_ref, k_ref, v_ref, qg_ref, kg_ref, bias_ref, seg_ref, o_ref, qs, ks, vs, s_all, p_all):
        for c in range(S // _RC):
            sl = slice(c * _RC, (c + 1) * _RC)
            q = q_ref[sl, :]
            k = k_ref[sl, :]
            qn = q * lax.rsqrt(_seg_mean(q * q, seg_ref) + EPS) * qg_ref[...]
            kn = k * lax.rsqrt(_seg_mean(k * k, seg_ref) + EPS) * kg_ref[...]
            v = v_ref[sl, :]
            for hh in range(2):
                ls = slice(hh * NA_DH, (hh + 1) * NA_DH)
                qs[hh, sl, :] = qn[:, ls].astype(BF16)
                ks[hh, sl, :] = kn[:, ls].astype(BF16)
                vs[hh, sl, :] = v[:, ls].astype(BF16)
        def where(r):
            rs = jnp.clip(r - NA_ROWS // 2, 0, rows - NA_ROWS)
            return rs, pl.multiple_of(r * GRID_W, GRID_W), pl.multiple_of(rs * GRID_W, GRID_W)

        def scores(r, carry):
            rs, q0, k0 = where(r)
            for hh in range(2):
                s = lax.dot_general(qs[hh, pl.ds(q0, GRID_W), :], ks[hh, pl.ds(k0, WIN), :], (((1,), (1,)), ((), ())),
                                    preferred_element_type=F32) * (NA_DH ** -0.5)
                s_all[hh, pl.ds(q0, GRID_W), :] = s + bias_ref[hh, r - rs]
            return carry
        lax.fori_loop(0, rows, scores, 0, unroll=4)

        def soft(r, carry):
            _, q0, _ = where(r)
            for hh in range(2):
                s = s_all[hh, pl.ds(q0, GRID_W), :]
                p = jnp.exp(s - jnp.max(s, axis=-1, keepdims=True))
                p_all[hh, pl.ds(q0, GRID_W), :] = (p * (1.0 / jnp.sum(p, axis=-1, keepdims=True))).astype(BF16)
            return carry
        lax.fori_loop(0, rows, soft, 0, unroll=2)

        def outp(r, carry):
            _, q0, k0 = where(r)
            outs = [jnp.dot(p_all[hh, pl.ds(q0, GRID_W), :], vs[hh, pl.ds(k0, WIN), :], preferred_element_type=F32)
                    for hh in range(2)]
            o_ref[pl.ds(q0, GRID_W), :] = jnp.concatenate(outs, axis=1)
            return carry
        lax.fori_loop(0, rows, outp, 0, unroll=4)

    LW = 128
    return pl.pallas_call(
        body, out_shape=jax.ShapeDtypeStruct((T, GROUP_W), F32), grid=(E, GROUP_W // LW),
        in_specs=[pl.BlockSpec((S, LW), lambda e, j: (e, B_Q // LW + j)),
                  pl.BlockSpec((S, LW), lambda e, j: (e, B_K // LW + j)),
                  pl.BlockSpec((S, LW), lambda e, j: (e, B_V // LW + j)),
                  pl.BlockSpec((1, LW), lambda e, j: (0, j)),
                  pl.BlockSpec((1, LW), lambda e, j: (0, j)),
                  pl.BlockSpec((2, NA_ROWS, GRID_W, WIN), lambda e, j: (j, 0, 0, 0)),
                  pl.BlockSpec((LW, LW), lambda e, j: (0, 0))],
        out_specs=pl.BlockSpec((S, LW), lambda e, j: (e, j)),
        scratch_shapes=[pltpu.VMEM((2, S, NA_DH), BF16)] * 3 + [pltpu.VMEM((2, S, WIN), F32), pltpu.VMEM((2, S, WIN), BF16)],
        name=name, compiler_params=_cp(("parallel", "parallel")))(z, z, z, qg, kg, bias, seg)


def _na_bwd(z, qg, kg, bias, do, S, name):
    T = z.shape[0]
    E = T // S
    rows = S // GRID_W
    WIN = NA_ROWS * GRID_W
    seg = _seg_mean_matrix(128, NA_DH)
    SC = NA_DH ** -0.5

    def body(q_ref, k_ref, v_ref, qg_ref, kg_ref, bias_ref, seg_ref, do_ref,
             dq_ref, dk_ref, dv_ref, dbias_ref, dqg_ref, dkg_ref, qs, ks, vs, dos, dqn, dkn, dvs, akt, avt,
             s_all, dp_all, p_all, ds_all):
        e = pl.program_id(1)

        @pl.when(e == 0)
        def _():
            dbias_ref[...] = jnp.zeros_like(dbias_ref)
            dqg_ref[...] = jnp.zeros_like(dqg_ref)
            dkg_ref[...] = jnp.zeros_like(dkg_ref)

        for c in range(S // _RC):
            sl = slice(c * _RC, (c + 1) * _RC)
            q = q_ref[sl, :]
            k = k_ref[sl, :]
            qn = q * lax.rsqrt(_seg_mean(q * q, seg_ref) + EPS) * qg_ref[...]
            kn = k * lax.rsqrt(_seg_mean(k * k, seg_ref) + EPS) * kg_ref[...]
            v = v_ref[sl, :]
            dd = do_ref[sl, :]
            for hh in range(2):
                ls = slice(hh * NA_DH, (hh + 1) * NA_DH)
                qs[hh, sl, :] = qn[:, ls].astype(BF16)
                ks[hh, sl, :] = kn[:, ls].astype(BF16)
                vs[hh, sl, :] = v[:, ls].astype(BF16)
                dos[hh, sl, :] = dd[:, ls].astype(BF16)
        akt[...] = jnp.zeros_like(akt)
        avt[...] = jnp.zeros_like(avt)

        def where(r):
            rs = jnp.clip(r - NA_ROWS // 2, 0, rows - NA_ROWS)
            return rs, pl.multiple_of(r * GRID_W, GRID_W), pl.multiple_of(rs * GRID_W, GRID_W)

        for hh in range(2):
            ls = slice(hh * NA_DH, (hh + 1) * NA_DH)

            def products(r, carry, hh=hh):
                rs, q0, k0 = where(r)
                s = lax.dot_general(qs[hh, pl.ds(q0, GRID_W), :], ks[hh, pl.ds(k0, WIN), :], (((1,), (1,)), ((), ())),
                                    preferred_element_type=F32) * SC
                s_all[pl.ds(q0, GRID_W), :] = s + bias_ref[hh, r - rs]
                dp_all[pl.ds(q0, GRID_W), :] = lax.dot_general(dos[hh, pl.ds(q0, GRID_W), :], vs[hh, pl.ds(k0, WIN), :],
                                                               (((1,), (1,)), ((), ())), preferred_element_type=F32)
                return carry
            lax.fori_loop(0, rows, products, 0, unroll=4)

            def soft(r, carry, hh=hh):
                rs, q0, _ = where(r)
                s = s_all[pl.ds(q0, GRID_W), :]
                p = jnp.exp(s - jnp.max(s, axis=-1, keepdims=True))
                p = p * (1.0 / jnp.sum(p, axis=-1, keepdims=True))
                dp = dp_all[pl.ds(q0, GRID_W), :]
                ds = p * (dp - jnp.sum(p * dp, axis=-1, keepdims=True))
                dbias_ref[hh, r - rs] += ds
                p_all[pl.ds(q0, GRID_W), :] = p.astype(BF16)
                ds_all[pl.ds(q0, GRID_W), :] = ds.astype(BF16)
                return carry
            lax.fori_loop(0, rows, soft, 0, unroll=2)

            def grads(r, carry, hh=hh, ls=ls):
                rs, q0, k0 = where(r)
                par = rs % 2
                t0 = (rs + par) // 2
                qr = qs[hh, pl.ds(q0, GRID_W), :]
                dor = dos[hh, pl.ds(q0, GRID_W), :]
                dsb = ds_all[pl.ds(q0, GRID_W), :]
                dqn[pl.ds(q0, GRID_W), ls] = jnp.dot(dsb, ks[hh, pl.ds(k0, WIN), :], preferred_element_type=F32) * SC
                dkt = lax.dot_general(qr, dsb, (((0,), (0,)), ((), ())), preferred_element_type=F32) * SC
                dvt = lax.dot_general(dor, p_all[pl.ds(q0, GRID_W), :], (((0,), (0,)), ((), ())), preferred_element_type=F32)
                akt[hh, par, pl.ds(t0, WIN // 128)] += jnp.stack([dkt[:, 128 * i:128 * (i + 1)] for i in range(WIN // 128)])
                avt[hh, par, pl.ds(t0, WIN // 128)] += jnp.stack([dvt[:, 128 * i:128 * (i + 1)] for i in range(WIN // 128)])
                return carry
            lax.fori_loop(0, rows, grads, 0, unroll=4)

        for hh in range(2):
            ls = slice(hh * NA_DH, (hh + 1) * NA_DH)
            for i in range(S // 128):
                for acc, dst in ((akt, dkn), (avt, dvs)):
                    odd = jnp.concatenate([acc[hh, 1, i][:, NA_DH:], acc[hh, 1, i + 1][:, :NA_DH]], axis=1)
                    dst[128 * i:128 * (i + 1), ls] = (acc[hh, 0, i] + odd).T

        for c in range(S // _RC):
            sl = slice(c * _RC, (c + 1) * _RC)
            for x_ref, g_ref, dn, dx_ref, dg_ref in ((q_ref, qg_ref, dqn, dq_ref, dqg_ref), (k_ref, kg_ref, dkn, dk_ref, dkg_ref)):
                x = x_ref[sl, :]
                r_ = lax.rsqrt(_seg_mean(x * x, seg_ref) + EPS)
                xh = x * r_
                d = dn[sl, :]
                dxh = d * g_ref[...]
                mean = _seg_mean(dxh * xh, seg_ref)
                dx_ref[sl, :] = (r_ * (dxh - xh * mean)).astype(BF16)
                dg_ref[...] += jnp.sum(d * xh, axis=0, keepdims=True)
            dv_ref[sl, :] = dvs[sl, :].astype(BF16)

    LW = 128
    blk = pl.BlockSpec((S, LW), lambda j, e: (e, j))
    vec = pl.BlockSpec((1, LW), lambda j, e: (0, j))
    bsp = pl.BlockSpec((2, NA_ROWS, GRID_W, WIN), lambda j, e: (j, 0, 0, 0))
    return pl.pallas_call(
        body, out_shape=(jax.ShapeDtypeStruct((T, GROUP_W), BF16),) * 3 + (
            jax.ShapeDtypeStruct((NA_HEADS, NA_ROWS, GRID_W, WIN), F32),
            jax.ShapeDtypeStruct((1, GROUP_W), F32), jax.ShapeDtypeStruct((1, GROUP_W), F32)),
        grid=(GROUP_W // LW, E),
        in_specs=[pl.BlockSpec((S, LW), lambda j, e: (e, B_Q // LW + j)),
                  pl.BlockSpec((S, LW), lambda j, e: (e, B_K // LW + j)),
                  pl.BlockSpec((S, LW), lambda j, e: (e, B_V // LW + j)),
                  vec, vec, bsp, pl.BlockSpec((LW, LW), lambda j, e: (0, 0)), blk],
        out_specs=(blk, blk, blk, bsp, vec, vec),
        scratch_shapes=[pltpu.VMEM((2, S, NA_DH), BF16)] * 4 + [pltpu.VMEM((S, LW), F32)] * 3
        + [pltpu.VMEM((2, 2, S // 128 + 1, NA_DH, 128), F32)] * 2
        + [pltpu.VMEM((S, WIN), F32)] * 2 + [pltpu.VMEM((S, WIN), BF16)] * 2,
        name=name, compiler_params=_cp(("parallel", "arbitrary")))(z, z, z, qg, kg, bias, seg, do)


def _na_rpb_grad(dbias, name):
    e_np, _, _ = _na_onehots()
    H = NA_HEADS
    nro = 2 * NA_ROWS - 1

    def fold(x_ref, o_ref):
        for ro in range(nro):
            acc = None
            for d in range(NA_ROWS):
                kr = ro + d - (NA_ROWS - 1)
                if 0 <= kr < NA_ROWS:
                    blk = x_ref[0, d, :, kr * GRID_W:(kr + 1) * GRID_W]
                    acc = blk if acc is None else acc + blk
            o_ref[0, ro] = acc
        o_ref[0, nro] = jnp.zeros((GRID_W, GRID_W), F32)

    t = pl.pallas_call(
        fold, out_shape=jax.ShapeDtypeStruct((H, 16, GRID_W, GRID_W), F32), grid=(H,),
        in_specs=[pl.BlockSpec((1, NA_ROWS, GRID_W, NA_ROWS * GRID_W), lambda h: (h, 0, 0, 0))],
        out_specs=pl.BlockSpec((1, 16, GRID_W, GRID_W), lambda h: (h, 0, 0, 0)),
        name=name + "_fold", compiler_params=_cp(("parallel",)))(dbias)
    t = t.reshape(H, 16, GRID_W * GRID_W)

    def body(x_ref, e_ref, o_ref):
        o_ref[0] = jnp.dot(x_ref[0], e_ref[...], precision=HI, preferred_element_type=F32)

    out = pl.pallas_call(
        body, out_shape=jax.ShapeDtypeStruct((H, 16, 128), F32), grid=(H,),
        in_specs=[pl.BlockSpec((1, 16, GRID_W * GRID_W), lambda h: (h, 0, 0)),
                  pl.BlockSpec((GRID_W * GRID_W, 128), lambda h: (0, 0))],
        out_specs=pl.BlockSpec((1, 16, 128), lambda h: (h, 0, 0)),
        name=name, compiler_params=_cp(("parallel",)))(t, jnp.asarray(e_np))
    return out[:, :nro, :2 * NA_COLS - 1]


_HK = GLA_HEADS * GLA_DK
_HV = GLA_HEADS * GLA_DV


def _gla_consts(reverse):
    i = np.arange(CHUNK)
    tri = (i[:, None] <= i[None, :]) if reverse else (i[:, None] >= i[None, :])
    j = np.arange(_HK)
    oseg = (j[:, None] // GLA_DK == j[None, :] // GLA_DK)
    return (jnp.asarray(tri.astype(np.float32)), jnp.asarray(tri.T.astype(np.float32)), jnp.asarray(oseg.astype(np.float32), BF16))


def _log_decay(lr, a2, ab):
    zg = jnp.dot(lr, a2, precision=HI, preferred_element_type=F32) + ab
    g = (jnp.minimum(zg, 0.0) - jnp.log(1.0 + jnp.exp(-jnp.abs(zg)))) * (1.0 / GLA_TAU)
    return zg, g


def _dotf(a, b, dn):
    return lax.dot_general(a, b, dn, precision=HI, preferred_element_type=F32)


def _dotb(a, b, dn):
    return lax.dot_general(a.astype(BF16), b.astype(BF16), dn, preferred_element_type=F32)


_COLS = 4
_SUB = 16
_NSUB = CHUNK // _SUB


def _gla_cross_blocks(reverse):
    return range(0, _NSUB - 1) if reverse else range(1, _NSUB)


def _gla_cross_terms(s, reverse, b_s, q_s, k_s, oseg_ref):
    r0 = s * _SUB
    ref = r0 + (_SUB - 1 if reverse else 0)
    bref = b_s[ref:ref + 1, :]
    rowj = lax.broadcasted_iota(jnp.int32, (CHUNK, 1), 0)
    seen = (rowj >= r0 + _SUB) if reverse else (rowj < r0)
    ek = jnp.where(seen, jnp.exp(jnp.minimum(bref - b_s[...], 0.0)), 0.0)
    kt = k_s[...] * ek
    eq = jnp.exp(jnp.minimum(b_s[r0:r0 + _SUB, :] - bref, 0.0))
    qt = q_s[r0:r0 + _SUB, :] * eq
    nmat = jnp.concatenate([kt.astype(BF16)] * GLA_HEADS, axis=0) * oseg_ref[...]
    return qt, eq, kt, ek, nmat


_NN = (((1,), (0,)), ((), ()))
_NT = (((1,), (1,)), ((), ()))
_TN = (((0,), (0,)), ((), ()))


_DT = 256


def _gla_block_tri(reverse):
    i = np.arange(_DT)
    same = i[:, None] // CHUNK == i[None, :] // CHUNK
    tri = (i[:, None] <= i[None, :]) if reverse else (i[:, None] >= i[None, :])
    return (tri & same).astype(np.float32)


def _gla_decay_fwd(z, a2, ab, reverse, name):
    T = z.shape[0]
    nc = _DT // CHUNK

    def body(lr_ref, a2_ref, ab_ref, m_ref, b_ref, ec_ref):
        _, g = _log_decay(lr_ref[...], a2_ref[...], ab_ref[...])
        b_ref[...] = _dotf(m_ref[...], g, _NN)
        for c in range(nc):
            ec_ref[c] = jnp.exp(_dotf(g[c * CHUNK:(c + 1) * CHUNK, :], jnp.ones((CHUNK, GLA_DV), F32), _TN))

    return pl.pallas_call(
        body, out_shape=(jax.ShapeDtypeStruct((T, _HK), F32), jax.ShapeDtypeStruct((T // CHUNK, _HK, GLA_DV), F32)),
        grid=(T // _DT,),
        in_specs=[pl.BlockSpec((_DT, 128), lambda i: (i, LR_OFF // 128)),
                  pl.BlockSpec((128, _HK), lambda i: (0, 0)),
                  pl.BlockSpec((1, _HK), lambda i: (0, 0)),
                  pl.BlockSpec((_DT, _DT), lambda i: (0, 0))],
        out_specs=(pl.BlockSpec((_DT, _HK), lambda i: (i, 0)), pl.BlockSpec((nc, _HK, GLA_DV), lambda i: (i, 0, 0))),
        name=name, compiler_params=_cp(("parallel",)))(z, a2, ab, jnp.asarray(_gla_block_tri(reverse)))


def _gla_decay_bwd(z, a2_f, ab_f, a2_b, ab_b, db_f, db_b, name):
    T = z.shape[0]

    def body(lr_ref, a2f_ref, abf_ref, a2b_ref, abb_ref, mf_ref, mb_ref, dbf_ref, dbb_ref,
             dlr_ref, da2f_ref, dabf_ref, da2b_ref, dabb_ref):
        @pl.when(pl.program_id(0) == 0)
        def _():
            for r in (da2f_ref, dabf_ref, da2b_ref, dabb_ref):
                r[...] = jnp.zeros_like(r)

        lr = lr_ref[...]
        dlr = jnp.zeros((_DT, 128), F32)
        for a2_ref, ab_ref, mt_ref, db_ref, da2_ref, dab_ref in ((a2f_ref, abf_ref, mf_ref, dbf_ref, da2f_ref, dabf_ref),
                                                                 (a2b_ref, abb_ref, mb_ref, dbb_ref, da2b_ref, dabb_ref)):
            zg, _ = _log_decay(lr, a2_ref[...], ab_ref[...])
            dg = _dotf(mt_ref[...], db_ref[...], _NN)
            dzg = dg * (1.0 / (1.0 + jnp.exp(zg))) * (1.0 / GLA_TAU)
            dlr = dlr + _dotf(dzg, a2_ref[...], _NT)
            da2_ref[...] += _dotf(lr, dzg, _TN)
            dab_ref[...] += jnp.sum(dzg, axis=0, keepdims=True)
        dlr_ref[...] = dlr.astype(BF16)

    a2s = pl.BlockSpec((128, _HK), lambda i: (0, 0))
    abs_ = pl.BlockSpec((1, _HK), lambda i: (0, 0))
    ms = pl.BlockSpec((_DT, _DT), lambda i: (0, 0))
    row = pl.BlockSpec((_DT, _HK), lambda i: (i, 0))
    return pl.pallas_call(
        body, out_shape=(jax.ShapeDtypeStruct((T, 128), BF16), jax.ShapeDtypeStruct((128, _HK), F32), jax.ShapeDtypeStruct((1, _HK), F32),
                         jax.ShapeDtypeStruct((128, _HK), F32), jax.ShapeDtypeStruct((1, _HK), F32)),
        grid=(T // _DT,),
        in_specs=[pl.BlockSpec((_DT, 128), lambda i: (i, LR_OFF // 128)), a2s, abs_, a2s, abs_, ms, ms, row, row],
        out_specs=(pl.BlockSpec((_DT, 128), lambda i: (i, 0)), a2s, abs_, a2s, abs_),
        name=name, compiler_params=_cp(("arbitrary",)))(
            z, a2_f, ab_f, a2_b, ab_b, jnp.asarray(_gla_block_tri(False).T), jnp.asarray(_gla_block_tri(True).T), db_f, db_b)


def _gla_fwd(z, b_all, ecol, S, reverse, name):
    T = z.shape[0]
    E = T // S
    n = S // CHUNK
    _, _, oseg = _gla_consts(reverse)
    last = 0 if reverse else CHUNK - 1

    def body(q_ref, k_ref, v_ref, b_ref, ec_ref, oseg_ref, o_ref, a_ref, st_ref, st, b_s, q_s, k_s):
        @pl.when(pl.program_id(1) == 0)
        def _():
            st[...] = jnp.zeros_like(st)

        q = q_ref[...] * (GLA_DK ** -0.5)
        k = k_ref[...]
        v = v_ref[...]
        b = b_ref[...]
        bl_row = b_ref[last:last + 1, :]
        e_col = ec_ref[0]
        b_s[...] = b
        q_s[...] = q
        k_s[...] = k
        lane = lax.broadcasted_iota(jnp.int32, (1, _HK), 1) % GLA_DK

        rowi = lax.broadcasted_iota(jnp.int32, (CHUNK, 1), 0)
        blk0 = (rowi // _SUB) * _SUB

        def cols(jj, a):
            ts = []
            for u in range(_COLS):
                jp = jj * _COLS + u
                tiles = []
                for s in range(_NSUB):
                    rs_ = slice(s * _SUB, (s + 1) * _SUB)
                    bj = b_s[pl.ds(s * _SUB + jp, 1), :]
                    kj = k_s[pl.ds(s * _SUB + jp, 1), :]
                    tiles.append(q_s[rs_, :] * jnp.exp(jnp.minimum(b_s[rs_, :] - bj, 0.0)) * kj)
                ts.append(jnp.concatenate(tiles, axis=0).astype(BF16))
            r = jnp.dot(jnp.concatenate(ts, axis=0), oseg_ref[...], preferred_element_type=F32)
            for u in range(_COLS):
                a = jnp.where(lane == blk0 + (jj * _COLS + u), r[u * CHUNK:(u + 1) * CHUNK, :], a)
            return a

        a = lax.fori_loop(0, _SUB // _COLS, cols, jnp.zeros((CHUNK, _HK), F32))
        keep = (rowi <= lane) if reverse else (rowi >= lane)
        a = jnp.where(keep, a, 0.0)
        cross = []
        for s in range(_NSUB):
            if s in _gla_cross_blocks(reverse):
                qt, _, _, _, nmat = _gla_cross_terms(s, reverse, b_s, q_s, k_s, oseg_ref)
                cross.append(lax.dot_general(qt.astype(BF16), nmat, _NT, preferred_element_type=F32))
            else:
                cross.append(jnp.zeros((_SUB, _HK), F32))
        a = a + jnp.concatenate(cross, axis=0)
        a_ref[...] = a
        st_ref[0] = st[...]
        qb = q * jnp.exp(b)
        kd = k * jnp.exp(bl_row - b)
        for h in range(GLA_HEADS):
            ks_ = slice(h * GLA_DK, (h + 1) * GLA_DK)
            vs_ = slice(h * GLA_DV, (h + 1) * GLA_DV)
            s_h = st[ks_, :]
            o_ref[:, vs_] = _dotb(qb[:, ks_], s_h, _NN) + _dotb(a[:, ks_], v[:, vs_], _NN)
            st[ks_, :] = s_h * e_col[ks_, :] + _dotb(kd[:, ks_], v[:, vs_], _TN)

    def rowblk(e, c):
        return e * n + ((n - 1 - c) if reverse else c)

    return pl.pallas_call(
        body, out_shape=(jax.ShapeDtypeStruct((T, _HV), F32), jax.ShapeDtypeStruct((T, _HK), F32),
                         jax.ShapeDtypeStruct((T // CHUNK, _HK, GLA_DV), F32)),
        grid=(E, n),
        in_specs=[pl.BlockSpec((CHUNK, _HK), lambda e, c: (rowblk(e, c), C_Q // _HK)),
                  pl.BlockSpec((CHUNK, _HK), lambda e, c: (rowblk(e, c), C_K // _HK)),
                  pl.BlockSpec((CHUNK, _HV), lambda e, c: (rowblk(e, c), C_V // _HV)),
                  pl.BlockSpec((CHUNK, _HK), lambda e, c: (rowblk(e, c), 0)),
                  pl.BlockSpec((1, _HK, GLA_DV), lambda e, c: (rowblk(e, c), 0, 0)),
                  pl.BlockSpec((_HK, _HK), lambda e, c: (0, 0))],
        out_specs=(pl.BlockSpec((CHUNK, _HV), lambda e, c: (rowblk(e, c), 0)),
                   pl.BlockSpec((CHUNK, _HK), lambda e, c: (rowblk(e, c), 0)),
                   pl.BlockSpec((1, _HK, GLA_DV), lambda e, c: (rowblk(e, c), 0, 0))),
        scratch_shapes=[pltpu.VMEM((_HK, GLA_DV), F32)] + [pltpu.VMEM((CHUNK, _HK), F32)] * 3,
        name=name, compiler_params=_cp(("parallel", "arbitrary")))(z, z, z, b_all, ecol, oseg)


def _gla_fwd_both(z, b_f, ec_f, b_b, ec_b, S, name):
    T = z.shape[0]
    E = T // S
    n = S // CHUNK
    _, _, oseg = _gla_consts(False)

    def body(*refs):
        oseg_ref = refs[10]
        dirs = []
        for di, reverse in enumerate((False, True)):
            q_ref, k_ref, v_ref, b_ref, ec_ref = refs[5 * di:5 * di + 5]
            o_ref, a_ref, st_ref = refs[11 + 3 * di:14 + 3 * di]
            st, b_s, q_s, k_s = refs[17 + 4 * di:21 + 4 * di]
            dirs.append((reverse, q_ref, k_ref, v_ref, b_ref, ec_ref, o_ref, a_ref, st_ref, st, b_s, q_s, k_s))

        @pl.when(pl.program_id(1) == 0)
        def _():
            for d in dirs:
                d[9][...] = jnp.zeros_like(d[9])

        lane = lax.broadcasted_iota(jnp.int32, (1, _HK), 1) % GLA_DK
        rowi = lax.broadcasted_iota(jnp.int32, (CHUNK, 1), 0)
        blk0 = (rowi // _SUB) * _SUB
        for (_, q_ref, k_ref, _, b_ref, _, _, _, _, _, b_s, q_s, k_s) in dirs:
            b_s[...] = b_ref[...]
            q_s[...] = q_ref[...] * (GLA_DK ** -0.5)
            k_s[...] = k_ref[...]

        def cols(d, jj, a):
            b_s, q_s, k_s = d[10], d[11], d[12]
            ts = []
            for u in range(_COLS):
                jp = jj * _COLS + u
                tiles = []
                for s in range(_NSUB):
                    rs_ = slice(s * _SUB, (s + 1) * _SUB)
                    bj = b_s[pl.ds(s * _SUB + jp, 1), :]
                    kj = k_s[pl.ds(s * _SUB + jp, 1), :]
                    tiles.append(q_s[rs_, :] * jnp.exp(jnp.minimum(b_s[rs_, :] - bj, 0.0)) * kj)
                ts.append(jnp.concatenate(tiles, axis=0).astype(BF16))
            r = jnp.dot(jnp.concatenate(ts, axis=0), oseg_ref[...], preferred_element_type=F32)
            for u in range(_COLS):
                a = jnp.where(lane == blk0 + (jj * _COLS + u), r[u * CHUNK:(u + 1) * CHUNK, :], a)
            return a

        zero = jnp.zeros((CHUNK, _HK), F32)
        acc = lax.fori_loop(0, _SUB // _COLS, lambda jj, c: tuple(cols(d, jj, a) for d, a in zip(dirs, c)), (zero, zero),
                            unroll=True)

        for (reverse, _, _, v_ref, b_ref, ec_ref, o_ref, a_ref, st_ref, st, b_s, q_s, k_s), a in zip(dirs, acc):
            last = 0 if reverse else CHUNK - 1
            keep = (rowi <= lane) if reverse else (rowi >= lane)
            a = jnp.where(keep, a, 0.0)
            cross = []
            for s in range(_NSUB):
                if s in _gla_cross_blocks(reverse):
                    qt, _, _, _, nmat = _gla_cross_terms(s, reverse, b_s, q_s, k_s, oseg_ref)
                    cross.append(lax.dot_general(qt.astype(BF16), nmat, _NT, preferred_element_type=F32))
                else:
                    cross.append(jnp.zeros((_SUB, _HK), F32))
            a = a + jnp.concatenate(cross, axis=0)
            a_ref[...] = a
            st_ref[0] = st[...]
            b = b_s[...]
            v = v_ref[...]
            e_col = ec_ref[0]
            qb = q_s[...] * jnp.exp(b)
            kd = k_s[...] * jnp.exp(b_ref[last:last + 1, :] - b)
            for h in range(GLA_HEADS):
                ks_ = slice(h * GLA_DK, (h + 1) * GLA_DK)
                vs_ = slice(h * GLA_DV, (h + 1) * GLA_DV)
                s_h = st[ks_, :]
                o_ref[:, vs_] = _dotb(qb[:, ks_], s_h, _NN) + _dotb(a[:, ks_], v[:, vs_], _NN)
                st[ks_, :] = s_h * e_col[ks_, :] + _dotb(kd[:, ks_], v[:, vs_], _TN)

    def specs(reverse):
        rb = (lambda e, c: e * n + (n - 1 - c)) if reverse else (lambda e, c: e * n + c)
        ins = [pl.BlockSpec((CHUNK, _HK), lambda e, c: (rb(e, c), C_Q // _HK)),
               pl.BlockSpec((CHUNK, _HK), lambda e, c: (rb(e, c), C_K // _HK)),
               pl.BlockSpec((CHUNK, _HV), lambda e, c: (rb(e, c), C_V // _HV)),
               pl.BlockSpec((CHUNK, _HK), lambda e, c: (rb(e, c), 0)),
               pl.BlockSpec((1, _HK, GLA_DV), lambda e, c: (rb(e, c), 0, 0))]
        outs = [pl.BlockSpec((CHUNK, _HV), lambda e, c: (rb(e, c), 0)),
                pl.BlockSpec((CHUNK, _HK), lambda e, c: (rb(e, c), 0)),
                pl.BlockSpec((1, _HK, GLA_DV), lambda e, c: (rb(e, c), 0, 0))]
        return ins, outs

    in_f, out_f = specs(False)
    in_b, out_b = specs(True)
    shapes = (jax.ShapeDtypeStruct((T, _HV), F32), jax.ShapeDtypeStruct((T, _HK), F32),
              jax.ShapeDtypeStruct((T // CHUNK, _HK, GLA_DV), F32))
    res = pl.pallas_call(
        body, out_shape=shapes + shapes, grid=(E, n),
        in_specs=in_f + in_b + [pl.BlockSpec((_HK, _HK), lambda e, c: (0, 0))], out_specs=tuple(out_f + out_b),
        scratch_shapes=([pltpu.VMEM((_HK, GLA_DV), F32)] + [pltpu.VMEM((CHUNK, _HK), F32)] * 3) * 2,
        name=name, compiler_params=_cp(("parallel", "arbitrary")))(z, z, z, b_f, ec_f, z, z, z, b_b, ec_b, oseg)
    return res[:3], res[3:]


def _gla_bwd(z, b_all, ecol, att, states, do, prev, S, reverse, name):
    T = z.shape[0]
    E = T // S
    n = S // CHUNK
    _, _, oseg = _gla_consts(reverse)
    has_prev = prev is not None
    odt = BF16 if has_prev else F32
    last = 0 if reverse else CHUNK - 1

    def body(*refs):
        (q_ref, k_ref, v_ref, b_ref, ec_ref, oseg_ref, att_ref, st_ref, do_ref) = refs[:9]
        refs = refs[9:]
        if has_prev:
            pq_ref, pk_ref, pv_ref = refs[:3]
            refs = refs[3:]
        (dq_ref, dk_ref, dv_ref, db_ref, dst, b_s, q_s, k_s, da_s, dqb_s, dkd_s, dk3_s, dbn_s, dsp_s) = refs

        @pl.when(pl.program_id(1) == 0)
        def _():
            dst[...] = jnp.zeros_like(dst)

        q = q_ref[...] * (GLA_DK ** -0.5)
        k = k_ref[...]
        v = v_ref[...]
        b = b_ref[...]
        bl_row = b_ref[last:last + 1, :]
        eb = jnp.exp(b)
        ekd = jnp.exp(bl_row - b)
        qb = q * eb
        kd = k * ekd
        b_s[...] = b
        q_s[...] = q
        k_s[...] = k
        att = att_ref[...]
        s_all = st_ref[0]
        dsn = dst[...]
        e_col = ec_ref[0]
        do = do_ref[...]
        lane = lax.broadcasted_iota(jnp.int32, (1, _HK), 1) % GLA_DK
        rowi = lax.broadcasted_iota(jnp.int32, (CHUNK, 1), 0)
        keep = (rowi <= lane) if reverse else (rowi >= lane)
        for h in range(GLA_HEADS):
            ks_ = slice(h * GLA_DK, (h + 1) * GLA_DK)
            vs_ = slice(h * GLA_DV, (h + 1) * GLA_DV)
            do_h = do[:, vs_]
            s_h = s_all[ks_, :]
            dsn_h = dsn[ks_, :]
            dqb_s[:, ks_] = _dotb(do_h, s_h, _NT)
            dsp_s[ks_, :] = _dotb(qb[:, ks_], do_h, _TN) + dsn_h * e_col[ks_, :]
            da_s[:, ks_] = _dotb(do_h, v[:, vs_], _NT)
            dv_h = _dotb(att[:, ks_], do_h, _TN) + _dotb(kd[:, ks_], dsn_h, _NN)
            if has_prev:
                dv_h = dv_h + pv_ref[:, vs_]
            dv_ref[:, vs_] = dv_h.astype(odt)
            dkd_s[:, ks_] = _dotb(v[:, vs_], dsn_h, _NT)
        da_s[...] = jnp.where(keep, da_s[...], 0.0)
        dqb = dqb_s[...]
        dkd = dkd_s[...]
        x = dsn * s_all * e_col
        dbl_row = _dotf(jnp.ones((8, GLA_DV), F32), x, _NT)[0:1, :] + jnp.sum(dkd * kd, axis=0, keepdims=True)

        blk0 = (rowi // _SUB) * _SUB

        def cols(jj, carry):
            dq3, db3 = list(carry[:_NSUB]), list(carry[_NSUB:])
            sel = [jnp.where(lane == blk0 + (jj * _COLS + u), da_s[...], 0.0).astype(BF16) for u in range(_COLS)]
            dcols = jnp.dot(jnp.concatenate(sel, axis=0), oseg_ref[...], preferred_element_type=F32)
            for u in range(_COLS):
                jp = jj * _COLS + u
                for s in range(_NSUB):
                    rs_ = slice(s * _SUB, (s + 1) * _SUB)
                    bj = b_s[pl.ds(s * _SUB + jp, 1), :]
                    kj = k_s[pl.ds(s * _SUB + jp, 1), :]
                    tm_ = dcols[u * CHUNK + s * _SUB:u * CHUNK + (s + 1) * _SUB, :] * jnp.exp(jnp.minimum(b_s[rs_, :] - bj, 0.0))
                    dq3[s] = dq3[s] + tm_ * kj
                    gq = tm_ * q_s[rs_, :]
                    dk3_s[pl.ds(s * _SUB + jp, 1), :] = jnp.sum(gq, axis=0, keepdims=True)
                    w = gq * kj
                    dbn_s[pl.ds(s * _SUB + jp, 1), :] = jnp.sum(w, axis=0, keepdims=True)
                    db3[s] = db3[s] + w
            return tuple(dq3) + tuple(db3)

        zero = jnp.zeros((_SUB, _HK), F32)
        acc = lax.fori_loop(0, _SUB // _COLS, cols, (zero,) * (2 * _NSUB))
        dq3 = jnp.concatenate(acc[:_NSUB], axis=0)
        db3 = jnp.concatenate(acc[_NSUB:], axis=0)
        head = lax.broadcasted_iota(jnp.int32, (1, _HK), 1) // GLA_DK
        dq_x, db_x = [], []
        dk_x = jnp.zeros((CHUNK, _HK), F32)
        db_k = jnp.zeros((CHUNK, _HK), F32)
        for s in range(_NSUB):
            if s not in _gla_cross_blocks(reverse):
                dq_x.append(zero)
                db_x.append(zero)
                continue
            r0 = s * _SUB
            qt, eq, kt, ek, nmat = _gla_cross_terms(s, reverse, b_s, q_s, k_s, oseg_ref)
            seen = (lane >= r0 + _SUB) if reverse else (lane < r0)
            dax = jnp.where(seen, da_s[r0:r0 + _SUB, :], 0.0).astype(BF16)
            dqt = jnp.dot(dax, nmat, preferred_element_type=F32)
            full = lax.dot_general(dax, qt.astype(BF16), _TN, preferred_element_type=F32)
            dkt = full[0:CHUNK, :]
            for h in range(1, GLA_HEADS):
                dkt = jnp.where(head == h, full[h * CHUNK:(h + 1) * CHUNK, :], dkt)
            dq_x.append(dqt * eq)
            db_x.append(dqt * qt)
            dk_x = dk_x + dkt * ek
            db_k = db_k + dkt * kt
        dq = (dqb * eb + dq3 + jnp.concatenate(dq_x, axis=0)) * (GLA_DK ** -0.5)
        dk = dkd * ekd + dk3_s[...] + dk_x
        db = dqb * qb - dkd * kd + db3 - dbn_s[...] + jnp.concatenate(db_x, axis=0) - db_k
        db_ref[...] = jnp.where(rowi == last, db + dbl_row, db)
        if has_prev:
            dq = dq + pq_ref[...]
            dk = dk + pk_ref[...]
        dq_ref[...] = dq.astype(odt)
        dk_ref[...] = dk.astype(odt)
        dst[...] = dsp_s[...]

    def rowblk(e, c):
        return e * n + (c if reverse else (n - 1 - c))

    hk = pl.BlockSpec((CHUNK, _HK), lambda e, c: (rowblk(e, c), 0))
    hv = pl.BlockSpec((CHUNK, _HV), lambda e, c: (rowblk(e, c), 0))
    stb = pl.BlockSpec((1, _HK, GLA_DV), lambda e, c: (rowblk(e, c), 0, 0))
    in_specs = [pl.BlockSpec((CHUNK, _HK), lambda e, c: (rowblk(e, c), C_Q // _HK)),
                pl.BlockSpec((CHUNK, _HK), lambda e, c: (rowblk(e, c), C_K // _HK)),
                pl.BlockSpec((CHUNK, _HV), lambda e, c: (rowblk(e, c), C_V // _HV)),
                hk, stb, pl.BlockSpec((_HK, _HK), lambda e, c: (0, 0)), hk, stb, hv]
    args = [z, z, z, b_all, ecol, oseg, att, states, do]
    if has_prev:
        in_specs += [hk, hk, hv]
        args += list(prev)
    return pl.pallas_call(
        body, out_shape=(jax.ShapeDtypeStruct((T, _HK), odt), jax.ShapeDtypeStruct((T, _HK), odt),
                         jax.ShapeDtypeStruct((T, _HV), odt), jax.ShapeDtypeStruct((T, _HK), F32)),
        grid=(E, n), in_specs=in_specs, out_specs=(hk, hk, hv, hk),
        scratch_shapes=[pltpu.VMEM((_HK, GLA_DV), F32)] + [pltpu.VMEM((CHUNK, _HK), F32)] * 8 + [pltpu.VMEM((_HK, GLA_DV), F32)],
        name=name, compiler_params=_cp(("parallel", "arbitrary")))(*args)


def _gla_bwd_both(z, fwd_saved, rev_saved, do, S, name):
    T = z.shape[0]
    E = T // S
    n = S // CHUNK
    _, _, oseg = _gla_consts(False)
    NI, NO, NS = 8, 4, 10

    def body(*refs):
        oseg_ref = refs[2 * NI]
        dirs = []
        for di, reverse in enumerate((False, True)):
            ins = refs[NI * di:NI * (di + 1)]
            outs = refs[2 * NI + 1 + NO * di:2 * NI + 1 + NO * (di + 1)]
            scr = refs[2 * NI + 1 + 2 * NO + NS * di:2 * NI + 1 + 2 * NO + NS * (di + 1)]
            dirs.append((reverse, ins, outs, scr))

        @pl.when(pl.program_id(1) == 0)
        def _():
            for d in dirs:
                d[3][0][...] = jnp.zeros_like(d[3][0])

        lane = lax.broadcasted_iota(jnp.int32, (1, _HK), 1) % GLA_DK
        head = lax.broadcasted_iota(jnp.int32, (1, _HK), 1) // GLA_DK
        rowi = lax.broadcasted_iota(jnp.int32, (CHUNK, 1), 0)
        blk0 = (rowi // _SUB) * _SUB

        def factors(reverse, b_ref, b_s, q_s, k_s):
            last = 0 if reverse else CHUNK - 1
            b = b_s[...]
            eb = jnp.exp(b)
            ekd = jnp.exp(b_ref[last:last + 1, :] - b)
            return eb, ekd, q_s[...] * eb, k_s[...] * ekd

        dbl_rows = []
        for reverse, (q_ref, k_ref, v_ref, b_ref, ec_ref, att_ref, st_ref, do_ref), (dq_ref, dk_ref, dv_ref, db_ref), \
                (dst, b_s, q_s, k_s, da_s, dqb_s, dkd_s, dk3_s, dbn_s, dsp_s) in dirs:
            b_s[...] = b_ref[...]
            q_s[...] = q_ref[...] * (GLA_DK ** -0.5)
            k_s[...] = k_ref[...]
            _, _, qb, kd = factors(reverse, b_ref, b_s, q_s, k_s)
            v = v_ref[...]
            att = att_ref[...]
            s_all = st_ref[0]
            dsn = dst[...]
            e_col = ec_ref[0]
            do = do_ref[...]
            keep = (rowi <= lane) if reverse else (rowi >= lane)
            for h in range(GLA_HEADS):
                ks_ = slice(h * GLA_DK, (h + 1) * GLA_DK)
                vs_ = slice(h * GLA_DV, (h + 1) * GLA_DV)
                do_h = do[:, vs_]
                s_h = s_all[ks_, :]
                dsn_h = dsn[ks_, :]
                dqb_s[:, ks_] = _dotb(do_h, s_h, _NT)
                dsp_s[ks_, :] = _dotb(qb[:, ks_], do_h, _TN) + dsn_h * e_col[ks_, :]
                da_s[:, ks_] = _dotb(do_h, v[:, vs_], _NT)
                dv_ref[:, vs_] = _dotb(att[:, ks_], do_h, _TN) + _dotb(kd[:, ks_], dsn_h, _NN)
                dkd_s[:, ks_] = _dotb(v[:, vs_], dsn_h, _NT)
            da_s[...] = jnp.where(keep, da_s[...], 0.0)
            x = dsn * s_all * e_col
            dbl_rows.append(_dotf(jnp.ones((8, GLA_DV), F32), x, _NT)[0:1, :] + jnp.sum(dkd_s[...] * kd, axis=0, keepdims=True))

        def cols(d, jj, carry):
            _, b_s, q_s, k_s, da_s, _, _, dk3_s, dbn_s, _ = d[3]
            dq3, db3 = list(carry[:_NSUB]), list(carry[_NSUB:])
            sel = [jnp.where(lane == blk0 + (jj * _COLS + u), da_s[...], 0.0).astype(BF16) for u in range(_COLS)]
            dcols = jnp.dot(jnp.concatenate(sel, axis=0), oseg_ref[...], preferred_element_type=F32)
            for u in range(_COLS):
                jp = jj * _COLS + u
                for s in range(_NSUB):
                    rs_ = slice(s * _SUB, (s + 1) * _SUB)
                    bj = b_s[pl.ds(s * _SUB + jp, 1), :]
                    kj = k_s[pl.ds(s * _SUB + jp, 1), :]
                    tm_ = dcols[u * CHUNK + s * _SUB:u * CHUNK + (s + 1) * _SUB, :] * jnp.exp(jnp.minimum(b_s[rs_, :] - bj, 0.0))
                    dq3[s] = dq3[s] + tm_ * kj
                    gq = tm_ * q_s[rs_, :]
                    dk3_s[pl.ds(s * _SUB + jp, 1), :] = jnp.sum(gq, axis=0, keepdims=True)
                    w = gq * kj
                    dbn_s[pl.ds(s * _SUB + jp, 1), :] = jnp.sum(w, axis=0, keepdims=True)
                    db3[s] = db3[s] + w
            return tuple(dq3) + tuple(db3)

        zero = jnp.zeros((_SUB, _HK), F32)
        init = (zero,) * (2 * _NSUB)
        accs = lax.fori_loop(0, _SUB // _COLS, lambda jj, c: tuple(cols(d, jj, a) for d, a in zip(dirs, c)), (init, init),
                             unroll=True)

        for (reverse, ins, (dq_ref, dk_ref, dv_ref, db_ref), (dst, b_s, q_s, k_s, da_s, dqb_s, dkd_s, dk3_s, dbn_s, dsp_s)), \
                acc, dbl_row in zip(dirs, accs, dbl_rows):
            last = 0 if reverse else CHUNK - 1
            eb, ekd, qb, kd = factors(reverse, ins[3], b_s, q_s, k_s)
            dqb = dqb_s[...]
            dkd = dkd_s[...]
            dq3 = jnp.concatenate(acc[:_NSUB], axis=0)
            db3 = jnp.concatenate(acc[_NSUB:], axis=0)
            dq_x, db_x = [], []
            dk_x = jnp.zeros((CHUNK, _HK), F32)
            db_k = jnp.zeros((CHUNK, _HK), F32)
            for s in range(_NSUB):
                if s not in _gla_cross_blocks(reverse):
                    dq_x.append(zero)
                    db_x.append(zero)
                    continue
                r0 = s * _SUB
                qt, eq, kt, ek, nmat = _gla_cross_terms(s, reverse, b_s, q_s, k_s, oseg_ref)
                seen = (lane >= r0 + _SUB) if reverse else (lane < r0)
                dax = jnp.where(seen, da_s[r0:r0 + _SUB, :], 0.0).astype(BF16)
                dqt = jnp.dot(dax, nmat, preferred_element_type=F32)
                full = lax.dot_general(dax, qt.astype(BF16), _TN, preferred_element_type=F32)
                dkt = full[0:CHUNK, :]
                for h in range(1, GLA_HEADS):
                    dkt = jnp.where(head == h, full[h * CHUNK:(h + 1) * CHUNK, :], dkt)
                dq_x.append(dqt * eq)
                db_x.append(dqt * qt)
                dk_x = dk_x + dkt * ek
                db_k = db_k + dkt * kt
            dq_ref[...] = (dqb * eb + dq3 + jnp.concatenate(dq_x, axis=0)) * (GLA_DK ** -0.5)
            dk_ref[...] = dkd * ekd + dk3_s[...] + dk_x
            db = dqb * qb - dkd * kd + db3 - dbn_s[...] + jnp.concatenate(db_x, axis=0) - db_k
            db_ref[...] = jnp.where(rowi == last, db + dbl_row, db)
            dst[...] = dsp_s[...]

    def specs(reverse):
        rb = (lambda e, c: e * n + c) if reverse else (lambda e, c: e * n + (n - 1 - c))
        hk = pl.BlockSpec((CHUNK, _HK), lambda e, c: (rb(e, c), 0))
        hv = pl.BlockSpec((CHUNK, _HV), lambda e, c: (rb(e, c), 0))
        stb = pl.BlockSpec((1, _HK, GLA_DV), lambda e, c: (rb(e, c), 0, 0))
        ins = [pl.BlockSpec((CHUNK, _HK), lambda e, c: (rb(e, c), C_Q // _HK)),
               pl.BlockSpec((CHUNK, _HK), lambda e, c: (rb(e, c), C_K // _HK)),
               pl.BlockSpec((CHUNK, _HV), lambda e, c: (rb(e, c), C_V // _HV)),
               hk, stb, hk, stb, hv]
        return ins, [hk, hk, hv, hk]

    in_f, out_f = specs(False)
    in_b, out_b = specs(True)
    shapes = (jax.ShapeDtypeStruct((T, _HK), F32), jax.ShapeDtypeStruct((T, _HK), F32),
              jax.ShapeDtypeStruct((T, _HV), F32), jax.ShapeDtypeStruct((T, _HK), F32))
    scratch = [pltpu.VMEM((_HK, GLA_DV), F32)] + [pltpu.VMEM((CHUNK, _HK), F32)] * 8 + [pltpu.VMEM((_HK, GLA_DV), F32)]
    res = pl.pallas_call(
        body, out_shape=shapes + shapes, grid=(E, n),
        in_specs=in_f + in_b + [pl.BlockSpec((_HK, _HK), lambda e, c: (0, 0))], out_specs=tuple(out_f + out_b),
        scratch_shapes=scratch * 2, name=name, compiler_params=_cp(("parallel", "arbitrary")))(
            z, z, z, *fwd_saved, do, z, z, z, *rev_saved, do, oseg)
    return res[:4], res[4:]


def _gla_norm_fwd(of, ob, og, name):
    T = of.shape[0]
    tm = 256

    def body(f_ref, b_ref, g_ref, o_ref):
        for h in range(GLA_HEADS):
            vs_ = slice(h * GLA_DV, (h + 1) * GLA_DV)
            o = f_ref[:, vs_] + b_ref[:, vs_]
            o_ref[:, vs_] = o * lax.rsqrt(jnp.mean(o * o, axis=-1, keepdims=True) + EPS) * g_ref[:, vs_]

    row = pl.BlockSpec((tm, _HV), lambda i: (i, 0))
    vec = pl.BlockSpec((1, _HV), lambda i: (0, 0))
    return pl.pallas_call(body, out_shape=jax.ShapeDtypeStruct((T, _HV), F32), grid=(T // tm,),
                          in_specs=[row, row, vec], out_specs=row, name=name, compiler_params=_cp(("parallel",)))(of, ob, og)


def _gla_norm_bwd(of, ob, og, dpre, name):
    T = of.shape[0]
    tm = 256

    def body(f_ref, b_ref, g_ref, dp_ref, do_ref, dg_ref):
        @pl.when(pl.program_id(0) == 0)
        def _():
            dg_ref[...] = jnp.zeros_like(dg_ref)

        for h in range(GLA_HEADS):
            vs_ = slice(h * GLA_DV, (h + 1) * GLA_DV)
            o = f_ref[:, vs_] + b_ref[:, vs_]
            r = lax.rsqrt(jnp.mean(o * o, axis=-1, keepdims=True) + EPS)
            xh = o * r
            dp = dp_ref[:, vs_]
            dxh = dp * g_ref[:, vs_]
            do_ref[:, vs_] = r * (dxh - xh * jnp.mean(dxh * xh, axis=-1, keepdims=True))
            dg_ref[:, vs_] += jnp.sum(dp * xh, axis=0, keepdims=True)

    row = pl.BlockSpec((tm, _HV), lambda i: (i, 0))
    vec = pl.BlockSpec((1, _HV), lambda i: (0, 0))
    return pl.pallas_call(
        body, out_shape=(jax.ShapeDtypeStruct((T, _HV), F32), jax.ShapeDtypeStruct((1, _HV), F32)), grid=(T // tm,),
        in_specs=[row, row, vec, row], out_specs=(row, vec), name=name, compiler_params=_cp(("arbitrary",)))(of, ob, og, dpre)


_ANY = pl.BlockSpec(memory_space=pl.ANY)


def _coords():
    return lax.axis_index("x"), lax.axis_index("y"), lax.axis_index("c")


def _other_chips(x, y):
    return ((1 - x, y), (x, 1 - y), (1 - x, 1 - y))


def _gather_weights(arrays, chunks, name):
    n = len(arrays)
    pieces = []
    for k in range(max(chunks)):
        for i, a in enumerate(arrays):
            if k < chunks[i]:
                rc = a.shape[1] // chunks[i]
                pieces.append((i, k * rc, rc))
    m = len(pieces)

    def body(*refs):
        srcs, dsts = refs[:n], refs[n:2 * n]
        send_sems, recv_sems, local_sems = refs[2 * n:]
        x, y, c = _coords()
        me = 2 * x + y
        loc = [pltpu.make_async_copy(s, d.at[me], local_sems.at[i]) for i, (s, d) in enumerate(zip(srcs, dsts))]
        for cp in loc:
            cp.start()
        ici = []
        for p, (i, r0, rc) in enumerate(pieces):
            for j, (px, py) in enumerate(_other_chips(x, y)):
                ici.append(pltpu.make_async_remote_copy(
                    src_ref=srcs[i].at[c, pl.ds(r0, rc)], dst_ref=dsts[i].at[me, c, pl.ds(r0, rc)],
                    send_sem=send_sems.at[3 * p + j], recv_sem=recv_sems.at[3 * p + j],
                    device_id=(px, py, c), device_id_type=MESH))
        for cp in ici:
            cp.start()
        fwd = []
        for p, (i, r0, rc) in enumerate(pieces):
            for j, (px, py) in enumerate(_other_chips(x, y)):
                ici[3 * p + j].wait_recv()
                part = dsts[i].at[2 * px + py, c, pl.ds(r0, rc)]
                cp = pltpu.make_async_remote_copy(
                    src_ref=part, dst_ref=part, send_sem=send_sems.at[3 * m + 3 * p + j], recv_sem=recv_sems.at[3 * m + 3 * p + j],
                    device_id=(x, y, 1 - c), device_id_type=MESH)
                cp.start()
                fwd.append(cp)
        for cp in fwd:
            cp.wait_recv()
        for cp in ici + fwd:
            cp.wait_send()
        for cp in loc:
            cp.wait()

    return pl.pallas_call(
        body, out_shape=tuple(jax.ShapeDtypeStruct((4,) + a.shape, a.dtype) for a in arrays),
        in_specs=[_ANY] * n, out_specs=(_ANY,) * n,
        scratch_shapes=[pltpu.SemaphoreType.DMA((6 * m,)), pltpu.SemaphoreType.DMA((6 * m,)), pltpu.SemaphoreType.DMA((n,))],
        name=name)(*arrays)


def _sibling_exchange(layered, whole, name):
    nl, n = len(layered), len(layered) + len(whole)

    def body(*refs):
        srcs, dsts = refs[:n], refs[n:2 * n]
        send_sems, recv_sems = refs[2 * n:]
        x, y, c = _coords()
        rem = [pltpu.make_async_remote_copy(src_ref=(s.at[1 - c] if i < nl else s), dst_ref=d, send_sem=send_sems.at[i],
                                            recv_sem=recv_sems.at[i], device_id=(x, y, 1 - c), device_id_type=MESH)
               for i, (s, d) in enumerate(zip(srcs, dsts))]
        for cp in rem:
            cp.start()
        for cp in rem:
            cp.wait()

    outs = [jax.ShapeDtypeStruct(a.shape[1:], a.dtype) for a in layered] + [jax.ShapeDtypeStruct(a.shape, a.dtype) for a in whole]
    return pl.pallas_call(
        body, out_shape=tuple(outs), in_specs=[_ANY] * n, out_specs=(_ANY,) * n,
        scratch_shapes=[pltpu.SemaphoreType.DMA((n,)), pltpu.SemaphoreType.DMA((n,))], name=name)(*layered, *whole)


def _chip_exchange(scatter, bcast, name):
    ns, n = len(scatter), len(scatter) + len(bcast)

    def body(*refs):
        srcs, dsts = refs[:n], refs[n:2 * n]
        send_sems, recv_sems, local_sems = refs[2 * n:]
        x, y, c = _coords()
        me = 2 * x + y
        loc = [pltpu.make_async_copy((s.at[me] if i < ns else s), d.at[me], local_sems.at[i])
               for i, (s, d) in enumerate(zip(srcs, dsts))]
        for cp in loc:
            cp.start()
        rem = []
        for j, (px, py) in enumerate(_other_chips(x, y)):
            for i, (s, d) in enumerate(zip(srcs, dsts)):
                rem.append(pltpu.make_async_remote_copy(
                    src_ref=(s.at[2 * px + py] if i < ns else s), dst_ref=d.at[me], send_sem=send_sems.at[n * j + i],
                    recv_sem=recv_sems.at[n * j + i], device_id=(px, py, c), device_id_type=MESH))
        for cp in rem:
            cp.start()
        for cp in rem:
            cp.wait()
        for cp in loc:
            cp.wait()

    outs = [jax.ShapeDtypeStruct(a.shape, a.dtype) for a in scatter] + [jax.ShapeDtypeStruct((4,) + a.shape, a.dtype) for a in bcast]
    return pl.pallas_call(
        body, out_shape=tuple(outs), in_specs=[_ANY] * n, out_specs=(_ANY,) * n,
        scratch_shapes=[pltpu.SemaphoreType.DMA((3 * n,)), pltpu.SemaphoreType.DMA((3 * n,)), pltpu.SemaphoreType.DMA((n,))],
        name=name)(*scatter, *bcast)


_EW_BLOCK_BYTES = 2 * 1024 * 1024


def _tile2d(R, C):
    if R % 256 == 0 and 256 * C * 4 <= _EW_BLOCK_BYTES:
        return 256, C
    bc = 256 if C % 256 == 0 else C
    for br in range(R, 0, -1):
        if R % br == 0 and (br % 8 == 0 or br == R) and br * bc * 4 <= _EW_BLOCK_BYTES:
            return br, bc
    return R, bc


def _sum_slots(r, name):
    n, R, C = r.shape
    br, bc = _tile2d(R, C)

    def body(r_ref, o_ref):
        acc = r_ref[0].astype(F32)
        for i in range(1, n):
            acc = acc + r_ref[i].astype(F32)
        o_ref[...] = acc

    return pl.pallas_call(body, out_shape=jax.ShapeDtypeStruct((R, C), F32), grid=(R // br, C // bc),
                          in_specs=[pl.BlockSpec((n, br, bc), lambda i, j: (0, i, j))],
                          out_specs=pl.BlockSpec((br, bc), lambda i, j: (i, j)),
                          name=name, compiler_params=_cp(("parallel", "parallel")))(r)


_SMEM = pl.BlockSpec(memory_space=pltpu.SMEM)


def _add2(a, b, out_dtype, name, pick=None):
    R, C = b.shape
    br, bc = _tile2d(R, C)
    blk = pl.BlockSpec((br, bc), lambda i, j: (i, j))
    if pick is None:
        def body(a_ref, b_ref, o_ref):
            o_ref[...] = (a_ref[...].astype(F32) + b_ref[...].astype(F32)).astype(out_dtype)
        in_specs, args = [blk, blk], (a, b)
    else:
        def body(c_ref, a_ref, b_ref, o_ref):
            av = jnp.where(c_ref[0] == 0, a_ref[0], a_ref[1])
            o_ref[...] = (av.astype(F32) + b_ref[...].astype(F32)).astype(out_dtype)
        in_specs = [_SMEM, pl.BlockSpec((2, br, bc), lambda i, j: (0, i, j)), blk]
        args = (pick.reshape(1).astype(jnp.int32), a, b)
    return pl.pallas_call(body, out_shape=jax.ShapeDtypeStruct((R, C), out_dtype), grid=(R // br, C // bc), in_specs=in_specs,
                          out_specs=blk, name=name, compiler_params=_cp(("parallel", "parallel")))(*args)


def _adamw_math(w, g, m, v):
    m = ADAM_B1 * m + (1.0 - ADAM_B1) * g
    v = ADAM_B2 * v + (1.0 - ADAM_B2) * (g * g)
    m_hat = m / (1.0 - ADAM_B1 ** ADAM_STEP)
    v_hat = v / (1.0 - ADAM_B2 ** ADAM_STEP)
    delta = -ADAM_LR * (m_hat / (jnp.sqrt(v_hat) + ADAM_EPS) + ADAM_WD * w)
    return delta, m, v


def _adamw(w, gs, m, v, name, pick=None):
    R, C = w.shape
    br, bc = _tile2d(R, C)
    blk = pl.BlockSpec((br, bc), lambda i, j: (i, j))
    if pick is None:
        g_specs = [pl.BlockSpec((g.shape[0], br, bc), lambda i, j: (0, i, j)) if g.ndim == 3 else blk for g in gs]
        lead = ()
    else:
        nb = (R // 2) // br
        assert nb * br * 2 == R
        g_specs = [pl.BlockSpec((br, bc), lambda i, j: (i % nb, j))] * 2
        lead = (pick.reshape(1).astype(jnp.int32),)

    def body(*refs):
        if pick is not None:
            c_ref, refs = refs[0], refs[1:]
        w_ref = refs[0]
        g_refs = refs[1:1 + len(gs)]
        m_ref, v_ref, g_out, d_out, m_out, v_out = refs[1 + len(gs):]
        if pick is None:
            g = None
            for gr in g_refs:
                parts = [gr[i] for i in range(gr.shape[0])] if len(gr.shape) == 3 else [gr[...]]
                for p in parts:
                    g = p if g is None else g + p
        else:
            g = jnp.where(pl.program_id(0) // nb == c_ref[0], g_refs[0][...], g_refs[1][...])
        d, mn, vn = _adamw_math(w_ref[...], g, m_ref[...], v_ref[...])
        g_out[...] = g
        d_out[...] = d
        m_out[...] = mn
        v_out[...] = vn

    return pl.pallas_call(
        body, out_shape=tuple(jax.ShapeDtypeStruct((R, C), F32) for _ in range(4)), grid=(R // br, C // bc),
        in_specs=[_SMEM] * len(lead) + [blk] + g_specs + [blk, blk], out_specs=(blk,) * 4, name=name,
        compiler_params=_cp(("parallel", "parallel")))(*lead, w, *gs, m, v)


WEIGHTS = ("norm_g", "w_in", "conv_w", "conv_b", "conv_ln_g", "conv_ln_b", "na_q_g", "na_k_g", "na_rpb", "gla_a2_f",
           "gla_ab_f", "gla_a2_b", "gla_ab_b", "gla_o_g", "pool_w", "pool_scale", "w_out")
_REPL = ("norm_g", "conv_b", "conv_ln_g", "conv_ln_b", "na_q_g", "na_k_g", "na_rpb", "gla_ab_f", "gla_ab_b", "gla_o_g",
         "pool_w", "pool_scale")
_SHARD_SMALL = ("conv_w", "gla_a2_f", "gla_a2_b")
_PACK_ROWS = 8 * 128


def _pack(arrs):
    flat = jnp.concatenate([a.reshape(-1) for a in arrs])
    n = -(-flat.shape[0] // _PACK_ROWS) * _PACK_ROWS
    return jnp.pad(flat, (0, n - flat.shape[0])).reshape(-1, 128)


def _unpack(p, shapes):
    flat = p.reshape(-1)
    out, o = [], 0
    for s in shapes:
        n = int(np.prod(s))
        out.append(flat[o:o + n].reshape(s))
        o += n
    return out


def _to_layout_rows(w):
    pad = jnp.zeros((w.shape[0], NZ - N_IN, w.shape[2]), w.dtype)
    return jnp.concatenate([w[:, :5120], w[:, 5152:6176], w[:, 5120:5152], pad], axis=1)


def _from_layout_rows(w):
    return jnp.concatenate([w[:, :5120], w[:, LR_OFF:LR_OFF + 32], w[:, 5120:LR_OFF]], axis=1)


def _reduce_gradients(p_a, p_b, small_g, ci):
    s_a, s_b, s_small = _sibling_exchange((p_a, p_b), (small_g,), "grad_to_sibling")
    flat = lambda a: a.reshape(a.shape[0], -1, a.shape[-1])
    c_a = _add2(flat(p_a), s_a.reshape(-1, s_a.shape[-1]), BF16, "chip_sum_a", pick=ci).reshape(s_a.shape)
    c_b = _add2(flat(p_b), s_b.reshape(-1, s_b.shape[-1]), BF16, "chip_sum_b", pick=ci).reshape(s_b.shape)
    c_small = _add2(small_g, s_small, F32, "chip_sum_small")
    r_a, r_b, r_small = _chip_exchange((c_a, c_b), (c_small,), "grad_to_owner")
    own_a = _sum_slots(r_a, "sum_a")
    own_b = _sum_slots(r_b, "sum_b")
    sib_a, sib_b = _sibling_exchange((), (own_a, own_b), "reduced_to_sibling")
    return (own_a, sib_a), (own_b, sib_b), r_small


def _layer_fwd(l, x, P, S, target=None):
    n = f"l{l}_"
    h = _rmsnorm_fwd(x, P["norm_g"], n + "rms_fwd")
    z = _matmul(h, P["w_in"], dims="nn", out_dtype=F32, tm=1024, tn=1280, tk=D_MODEL, name=n + "mm_z")
    yc = _conv_fwd(z, P["conv_w32"], P["conv_b"], S, n + "conv_fwd")
    pre_a = _ln_silu_fwd(yc, P["conv_ln_g"], P["conv_ln_b"], n + "ln_fwd")
    pre_b = _na_fwd(z, P["na_q_g"], P["na_k_g"], P["na_bias"], S, n + "na_fwd")
    bf, ecf = _gla_decay_fwd(z, P["a2_f"], P["gla_ab_f"], False, n + "gla_decay_f")
    bb, ecb = _gla_decay_fwd(z, P["a2_b"], P["gla_ab_b"], True, n + "gla_decay_b")
    (of, af, sf), (ob, ab, sb) = _gla_fwd_both(z, bf, ecf, bb, ecb, S, n + "gla_fwd")
    pre_c = _gla_norm_fwd(of, ob, P["gla_o_g"], n + "gla_norm_fwd")
    pre_d = _pool_fwd(z, P["pool_w_bf"], P["pool_scale"], S, n + "pool_fwd")
    pres = (pre_a, pre_b, pre_c, pre_d)
    res = _out_proj_fwd(pres, z, P["w_out"], x, target, n + "out_proj")
    y, out = res[0], (res[1] if target is None else res[1:])
    return out, dict(x=x, h=h, z=z, yc=yc, pres=pres, of=of, af=af, sf=sf, ob=ob, ab=ab, sb=sb, y=y, bf=bf, ecf=ecf, bb=bb, ecb=ecb)


def _layer_bwd(l, dout, dout_bf, sv, P, S):
    n = f"l{l}_"
    z = sv["z"]
    T = z.shape[0]
    d_w_out = _matmul(sv["y"], dout_bf, dims="tn", out_dtype=BF16, tm=1024, tn=2048, tk=512, name=n + "mm_dwout")
    dpa, dpb, dpc, dpd, dga, dgb, dgc, dgd = _out_proj_bwd(dout_bf, P["w_out_t"], sv["pres"], z, n + "out_proj_bwd")
    dyc, d_ln_g, d_ln_b = _ln_silu_bwd(sv["yc"], P["conv_ln_g"], P["conv_ln_b"], dpa, n + "ln_bwd")
    dval, dglu, d_cw, d_cb = _conv_bwd(z, P["conv_w32"], dyc, S, n + "conv_bwd")
    dq, dk, dv, dbias, d_qg, d_kg = _na_bwd(z, P["na_q_g"], P["na_k_g"], P["na_bias"], dpb, S, n + "na_bwd")
    d_rpb = _na_rpb_grad(dbias, n + "na_rpb")
    do, d_og = _gla_norm_bwd(sv["of"], sv["ob"], P["gla_o_g"], dpc, n + "gla_norm_bwd")
    gf, gb_ = _gla_bwd_both(z, (sv["bf"], sv["ecf"], sv["af"], sv["sf"]), (sv["bb"], sv["ecb"], sv["ab"], sv["sb"]), do, S,
                            n + "gla_bwd")
    dcq, dck, dcv = zip(gf[:3], gb_[:3])
    dlr, d_a2f, d_abf, d_a2b, d_abb = _gla_decay_bwd(z, P["a2_f"], P["gla_ab_f"], P["a2_b"], P["gla_ab_b"], gf[3], gb_[3],
                                                     n + "gla_decay_bwd")
    dd, d_pw, d_ps = _pool_bwd(z, P["pool_w_bf"], P["pool_scale"], dpd, S, n + "pool_bwd")
    dz = _concat_cols([dval, dglu, dga, dq, dk, dv, dgb, dcq, dck, dcv, dgc, dd, dgd, dlr], NZ, n + "dz_concat")
    dh = _matmul(dz, P["w_in_t"], dims="nn", out_dtype=F32, tm=1024, tn=1024, tk=3200, name=n + "mm_dh")
    d_w_in = _matmul(dz, sv["h"], dims="tn", out_dtype=BF16, tm=1280, tn=1024, tk=1024, name=n + "mm_dwin")
    dx, dx_bf, d_ng = _rmsnorm_bwd(sv["x"], P["norm_g"], dh, dout, n + "rms_bwd")
    grads = dict(norm_g=d_ng[0], w_in=d_w_in, conv_w=d_cw[:CONV_K], conv_b=d_cb[0], conv_ln_g=d_ln_g[0], conv_ln_b=d_ln_b[0],
                 na_q_g=d_qg.reshape(NA_HEADS, NA_DH), na_k_g=d_kg.reshape(NA_HEADS, NA_DH), na_rpb=d_rpb,
                 gla_a2_f=d_a2f[0:GLA_RANK], gla_ab_f=d_abf[0], gla_a2_b=d_a2b[GLA_RANK:2 * GLA_RANK], gla_ab_b=d_abb[0],
                 gla_o_g=d_og.reshape(GLA_HEADS, GLA_DV), pool_w=d_pw, pool_scale=d_ps[0], w_out=d_w_out)
    return dx, dx_bf, grads


def kernel(x, norm_g, w_in, conv_w, conv_b, conv_ln_g, conv_ln_b, na_q_g, na_k_g, na_rpb, gla_a2_f, gla_ab_f, gla_a2_b, gla_ab_b, gla_o_g, pool_w, pool_scale, w_out, loss_target, m_norm_g, m_w_in, m_conv_w, m_conv_b, m_conv_ln_g, m_conv_ln_b, m_na_q_g, m_na_k_g, m_na_rpb, m_gla_a2_f, m_gla_ab_f, m_gla_a2_b, m_gla_ab_b, m_gla_o_g, m_pool_w, m_pool_scale, m_w_out, v_norm_g, v_w_in, v_conv_w, v_conv_b, v_conv_ln_g, v_conv_ln_b, v_na_q_g, v_na_k_g, v_na_rpb, v_gla_a2_f, v_gla_ab_f, v_gla_a2_b, v_gla_ab_b, v_gla_o_g, v_pool_w, v_pool_scale, v_w_out):
    W = dict(norm_g=norm_g, w_in=w_in, conv_w=conv_w, conv_b=conv_b, conv_ln_g=conv_ln_g, conv_ln_b=conv_ln_b, na_q_g=na_q_g,
             na_k_g=na_k_g, na_rpb=na_rpb, gla_a2_f=gla_a2_f, gla_ab_f=gla_ab_f, gla_a2_b=gla_a2_b, gla_ab_b=gla_ab_b,
             gla_o_g=gla_o_g, pool_w=pool_w, pool_scale=pool_scale, w_out=w_out)
    M = dict(norm_g=m_norm_g, w_in=m_w_in, conv_w=m_conv_w, conv_b=m_conv_b, conv_ln_g=m_conv_ln_g, conv_ln_b=m_conv_ln_b,
             na_q_g=m_na_q_g, na_k_g=m_na_k_g, na_rpb=m_na_rpb, gla_a2_f=m_gla_a2_f, gla_ab_f=m_gla_ab_f, gla_a2_b=m_gla_a2_b,
             gla_ab_b=m_gla_ab_b, gla_o_g=m_gla_o_g, pool_w=m_pool_w, pool_scale=m_pool_scale, w_out=m_w_out)
    V = dict(norm_g=v_norm_g, w_in=v_w_in, conv_w=v_conv_w, conv_b=v_conv_b, conv_ln_g=v_conv_ln_g, conv_ln_b=v_conv_ln_b,
             na_q_g=v_na_q_g, na_k_g=v_na_k_g, na_rpb=v_na_rpb, gla_a2_f=v_gla_a2_f, gla_ab_f=v_gla_ab_f, gla_a2_b=v_gla_a2_b,
             gla_ab_b=v_gla_ab_b, gla_o_g=v_gla_o_g, pool_w=v_pool_w, pool_scale=v_pool_scale, w_out=v_w_out)
    E, S, D = x.shape
    T = E * S
    L = DEPTH
    xi, yi, ci = _coords()
    chip = 2 * xi + yi
    cw_sh, a2_sh = conv_w.shape[-1], gla_a2_f.shape[-1]

    small_sh = jnp.concatenate([
        jnp.pad(conv_w, ((0, 0), (0, 1), (0, 0))),
        jnp.pad(gla_a2_f, ((0, 0), (0, 0), (0, 128 - a2_sh))),
        jnp.pad(gla_a2_b, ((0, 0), (0, 0), (0, 128 - a2_sh)))], axis=1)
    w_in_tr, m_w_in_tr, v_w_in_tr = (jnp.transpose(a, (0, 2, 1)) for a in (w_in, m_w_in, v_w_in))
    g_win, g_wout, g_small = _gather_weights((w_in_tr.astype(BF16), w_out.astype(BF16), small_sh), (1, 2, 1), "gather_weights")
    w_in_t_full = _to_layout_rows(jnp.transpose(g_win, (1, 0, 2, 3)).reshape(L, N_IN, D))
    w_in_full = jnp.transpose(w_in_t_full, (0, 2, 1))
    w_out_full = jnp.transpose(g_wout, (1, 0, 2, 3)).reshape(L, D, D)
    conv_w_full = jnp.transpose(g_small[:, :, 0:32, :], (1, 2, 0, 3)).reshape(L, 32, 4 * cw_sh)
    a2f_full = jnp.transpose(g_small[:, :, 32:48, :a2_sh], (1, 2, 0, 3)).reshape(L, GLA_RANK, 4 * a2_sh)
    a2b_full = jnp.transpose(g_small[:, :, 48:64, :a2_sh], (1, 2, 0, 3)).reshape(L, GLA_RANK, 4 * a2_sh)

    params = []
    for l in range(L):
        params.append(dict(
            norm_g=norm_g[l][None], w_in=w_in_full[l], w_out=w_out_full[l], w_in_t=w_in_t_full[l], w_out_t=w_out_full[l].T,
            conv_w32=conv_w_full[l], conv_b=conv_b[l][None],
            conv_ln_g=conv_ln_g[l][None], conv_ln_b=conv_ln_b[l][None], na_q_g=na_q_g[l].reshape(1, GROUP_W),
            na_k_g=na_k_g[l].reshape(1, GROUP_W), na_bias=_na_bias(na_rpb[l], f"l{l}_na_bias"),
            a2_f=jnp.zeros((128, _HK), F32).at[0:GLA_RANK].set(a2f_full[l]),
            a2_b=jnp.zeros((128, _HK), F32).at[GLA_RANK:2 * GLA_RANK].set(a2b_full[l]),
            gla_ab_f=gla_ab_f[l][None], gla_ab_b=gla_ab_b[l][None], gla_o_g=gla_o_g[l].reshape(1, GROUP_W),
            pool_w_bf=pool_w[l].astype(BF16), pool_scale=pool_scale[l][None]))

    act = x.reshape(T, D)
    saved = []
    for l in range(L):
        act, sv = _layer_fwd(l, act, params[l], S, loss_target.reshape(T, D) if l == L - 1 else None)
        saved.append(sv)
    dact, dact_bf, loss_loc = act
    loss = lax.psum(loss_loc[0, 0], ("x", "y", "c"))
    grads = [None] * L
    for l in reversed(range(L)):
        dact, dact_bf, grads[l] = _layer_bwd(l, dact, dact_bf, saved[l], params[l], S)
    grad_x = dact.reshape(E, S, D)
    G = {k: jnp.stack([grads[l][k] for l in range(L)]) for k in WEIGHTS}

    cols_in, cols_out = N_IN // 4, D
    p_win = _from_layout_rows(G["w_in"]).reshape(L, 4, cols_in, D)
    p_wout = G["w_out"].reshape(L, 4, D // 4, D)
    small_names = _REPL + _SHARD_SMALL
    small_g = _pack([G[k] for k in small_names])
    g_in, g_out, r_small = _reduce_gradients(p_win, p_wout, small_g, ci)

    rows_in, rows_out = L * cols_in, L * (D // 4)
    res = {}
    res["w_in"] = [jnp.transpose(a.reshape(L, cols_in, D), (0, 2, 1)) for a in _adamw(
        w_in_tr.reshape(rows_in, D), g_in, m_w_in_tr.reshape(rows_in, D), v_w_in_tr.reshape(rows_in, D), "adamw_w_in", pick=ci)]
    res["w_out"] = [a.reshape(L, D // 4, D) for a in _adamw(
        w_out.reshape(rows_out, cols_out), g_out, m_w_out.reshape(rows_out, cols_out),
        v_w_out.reshape(rows_out, cols_out), "adamw_w_out", pick=ci)]
    zeros_sh = [jnp.zeros(G[k].shape, F32) for k in _SHARD_SMALL]
    pk = lambda dct: _pack([dct[k] for k in _REPL] + zeros_sh)
    small_res = _adamw(pk(W), (r_small,), pk(M), pk(V), "adamw_small")
    shapes = [G[k].shape for k in small_names]
    unp = [_unpack(a, shapes) for a in small_res]
    for i, k in enumerate(_REPL):
        res[k] = [u[i] for u in unp]
    g_sh = []
    for i, k in enumerate(_SHARD_SMALL):
        gfull = unp[0][len(_REPL) + i]
        wdt = W[k].shape[-1]
        g_sh.append(lax.dynamic_slice_in_dim(gfull, chip * wdt, wdt, axis=2))
    g_sh_p = _pack(g_sh)
    sh_res = _adamw(_pack([W[k] for k in _SHARD_SMALL]), (g_sh_p,), _pack([M[k] for k in _SHARD_SMALL]),
                    _pack([V[k] for k in _SHARD_SMALL]), "adamw_shard_small")
    shapes2 = [W[k].shape for k in _SHARD_SMALL]
    unp2 = [_unpack(a, shapes2) for a in sh_res]
    for i, k in enumerate(_SHARD_SMALL):
        res[k] = [u[i] for u in unp2]

    outs = [loss, grad_x]
    for j in range(4):
        outs += [res[k][j] for k in WEIGHTS]
    return tuple(outs)
```

```python
import functools

import numpy as np
import jax
import jax.numpy as jnp
from jax import lax
from jax.experimental import pallas as pl
from jax.experimental.pallas import tpu as pltpu

F32 = jnp.float32
BF16 = jnp.bfloat16
HI = lax.Precision.HIGHEST
MESH = pl.DeviceIdType.MESH

EPS = 1e-6
D_MODEL = 2048
GROUP_W = 512
SEQ = 2048
DEPTH = 2
N_IN = 6176
GRID_W = 64
CONV_K = 31
NA_HEADS = 8
NA_DH = 64
NA_ROWS = 8
NA_COLS = 16
GLA_HEADS = 4
GLA_DK = 64
GLA_DV = 128
GLA_RANK = 16
GLA_TAU = 16.0
CHUNK = 64
POOL_WINDOWS = (2, 4, 8, 16)
ADAM_LR, ADAM_B1, ADAM_B2, ADAM_EPS, ADAM_WD, ADAM_STEP = 0.001, 0.9, 0.999, 1e-08, 0.01, 10

A_VAL, A_GLU, A_GATE = 0, 512, 1024
B_Q, B_K, B_V, B_GATE = 1536, 2048, 2560, 3072
C_Q, C_K, C_V, C_GATE = 3584, 3840, 4096, 4608
D_VAL, D_GATE = 5120, 5632
LR_OFF = 6144
NZ = 6400
NEG = -1e30
VMEM_LIMIT = 56 * 1024 * 1024


def _cp(sem=None):
    return pltpu.CompilerParams(dimension_semantics=sem, vmem_limit_bytes=VMEM_LIMIT)


def _sigmoid(x):
    return 1.0 / (1.0 + jnp.exp(-x))


def _silu(x):
    return x * _sigmoid(x)


def _dsilu(x):
    s = _sigmoid(x)
    return s * (1.0 + x * (1.0 - s))


def _matmul(a, b, *, dims, out_dtype, tm, tn, tk, name, res=None):
    if dims == "nn":
        (M, K), N = a.shape, b.shape[1]
    elif dims == "nt":
        (M, K), N = a.shape, b.shape[0]
    else:
        (K, M), N = a.shape, b.shape[1]
    tm, tn, tk = min(tm, M), min(tn, N), min(tk, K)
    nk = K // tk
    assert M % tm == 0 and N % tn == 0 and K % tk == 0, (M, N, K, tm, tn, tk)
    dn = {"nn": (((1,), (0,)), ((), ())), "nt": (((1,), (1,)), ((), ())), "tn": (((0,), (0,)), ((), ()))}[dims]
    if dims == "tn":
        a_spec = pl.BlockSpec((tk, tm), lambda i, j, k: (k, i))
    else:
        a_spec = pl.BlockSpec((tm, tk), lambda i, j, k: (i, k))
    if dims == "nt":
        b_spec = pl.BlockSpec((tn, tk), lambda i, j, k: (j, k))
    else:
        b_spec = pl.BlockSpec((tk, tn), lambda i, j, k: (k, j))
    o_spec = pl.BlockSpec((tm, tn), lambda i, j, k: (i, j))
    has_res = res is not None

    def body(*refs):
        if has_res:
            a_ref, b_ref, r_ref, o_ref, acc = refs
        else:
            a_ref, b_ref, o_ref, acc = refs
        k = pl.program_id(2)

        @pl.when(k == 0)
        def _():
            acc[...] = jnp.zeros_like(acc)

        acc[...] += lax.dot_general(a_ref[...], b_ref[...], dn, preferred_element_type=F32)

        @pl.when(k == nk - 1)
        def _():
            r = acc[...]
            if has_res:
                r = r + r_ref[...]
            o_ref[...] = r.astype(o_ref.dtype)

    in_specs = [a_spec, b_spec] + ([o_spec] if has_res else [])
    args = (a, b) + ((res,) if has_res else ())
    return pl.pallas_call(
        body, out_shape=jax.ShapeDtypeStruct((M, N), out_dtype), grid=(M // tm, N // tn, nk),
        in_specs=in_specs, out_specs=o_spec, scratch_shapes=[pltpu.VMEM((tm, tn), F32)],
        name=name, compiler_params=_cp(("parallel", "parallel", "arbitrary")))(*args)


def _concat_cols(pieces, width, name):
    pairs = [p if isinstance(p, tuple) else (p,) for p in pieces]
    T = pairs[0][0].shape[0]
    tm = min(512, T)
    dt = BF16
    offs = np.cumsum([0] + [p[0].shape[1] for p in pairs])
    flat = [a for p in pairs for a in p]

    def body(*refs):
        o_ref = refs[-1]
        k = 0
        for p, a, b in zip(pairs, offs[:-1], offs[1:]):
            val = refs[k][...] if len(p) == 1 else refs[k][...] + refs[k + 1][...]
            o_ref[:, a:b] = val.astype(dt)
            k += len(p)
        if offs[-1] < width:
            o_ref[:, offs[-1]:width] = jnp.zeros((tm, width - offs[-1]), dt)

    return pl.pallas_call(
        body, out_shape=jax.ShapeDtypeStruct((T, width), dt), grid=(T // tm,),
        in_specs=[pl.BlockSpec((tm, a.shape[1]), lambda i: (i, 0)) for a in flat],
        out_specs=pl.BlockSpec((tm, width), lambda i: (i, 0)), name=name, compiler_params=_cp(("parallel",)))(*flat)


def _rmsnorm_fwd(x, g, name):
    T, D = x.shape
    tm = 256

    def body(x_ref, g_ref, h_ref):
        xv = x_ref[...]
        r = lax.rsqrt(jnp.mean(xv * xv, axis=-1, keepdims=True) + EPS)
        h_ref[...] = (xv * r * g_ref[...]).astype(h_ref.dtype)

    return pl.pallas_call(
        body, out_shape=jax.ShapeDtypeStruct((T, D), BF16), grid=(T // tm,),
        in_specs=[pl.BlockSpec((tm, D), lambda i: (i, 0)), pl.BlockSpec((1, D), lambda i: (0, 0))],
        out_specs=pl.BlockSpec((tm, D), lambda i: (i, 0)), name=name, compiler_params=_cp(("parallel",)))(x, g)


def _rmsnorm_bwd(x, g, dh, dres, name):
    T, D = x.shape
    tm = 256

    def body(x_ref, g_ref, dh_ref, dres_ref, dx_ref, dxb_ref, dg_ref):
        xv = x_ref[...]
        r = lax.rsqrt(jnp.mean(xv * xv, axis=-1, keepdims=True) + EPS)
        xh = xv * r
        dh_v = dh_ref[...]
        dxh = dh_v * g_ref[...]
        dx = r * (dxh - xh * jnp.mean(dxh * xh, axis=-1, keepdims=True)) + dres_ref[...]
        dx_ref[...] = dx
        dxb_ref[...] = dx.astype(BF16)

        @pl.when(pl.program_id(0) == 0)
        def _():
            dg_ref[...] = jnp.zeros_like(dg_ref)

        dg_ref[...] += jnp.sum(dh_v * xh, axis=0, keepdims=True)

    row = pl.BlockSpec((tm, D), lambda i: (i, 0))
    vec = pl.BlockSpec((1, D), lambda i: (0, 0))
    return pl.pallas_call(
        body, out_shape=(jax.ShapeDtypeStruct((T, D), F32), jax.ShapeDtypeStruct((T, D), BF16), jax.ShapeDtypeStruct((1, D), F32)),
        grid=(T // tm,), in_specs=[row, vec, row, row], out_specs=(row, row, vec), name=name,
        compiler_params=_cp(("arbitrary",)))(x, g, dh, dres)


_GATE_COLS = (A_GATE // GROUP_W, B_GATE // GROUP_W, C_GATE // GROUP_W, D_GATE // GROUP_W)
_OP_TM = 256


def _out_proj_fwd(pres, z, w_out, x, target, name):
    T, D = x.shape
    tm = min(_OP_TM, T)
    with_loss = target is not None

    def body(*refs):
        pa, pb, pc, pd, ga, gb, gc, gd, w_ref, x_ref = refs[:10]
        refs = refs[10:]
        if with_loss:
            t_ref, y_ref, d_ref, db_ref, l_ref = refs
        else:
            y_ref, o_ref = refs
        for n_, (p, g) in enumerate(((pa, ga), (pb, gb), (pc, gc), (pd, gd))):
            y_ref[:, n_ * GROUP_W:(n_ + 1) * GROUP_W] = (p[...] * _silu(g[...])).astype(BF16)
        out = jnp.dot(y_ref[...], w_ref[...], preferred_element_type=F32) + x_ref[...]
        if with_loss:
            e = out - t_ref[...]
            d = e * (1.0 / D)
            d_ref[...] = d
            db_ref[...] = d.astype(BF16)

            @pl.when(pl.program_id(0) == 0)
            def _():
                l_ref[...] = jnp.zeros_like(l_ref)

            l_ref[...] += jnp.sum(jnp.sum(e * e, axis=-1, keepdims=True) * (0.5 / D), axis=0, keepdims=True)
        else:
            o_ref[...] = out

    pre_spec = pl.BlockSpec((tm, GROUP_W), lambda i: (i, 0))
    gate_specs = [pl.BlockSpec((tm, GROUP_W), functools.partial(lambda i, c: (i, c), c=c)) for c in _GATE_COLS]
    row = pl.BlockSpec((tm, D), lambda i: (i, 0))
    w_spec = pl.BlockSpec((4 * GROUP_W, D), lambda i: (0, 0))
    in_specs = [pre_spec] * 4 + gate_specs + [w_spec, row]
    args = list(pres) + [z, z, z, z, w_out, x]
    if with_loss:
        in_specs.append(row)
        args.append(target)
        out_shape = (jax.ShapeDtypeStruct((T, D), BF16), jax.ShapeDtypeStruct((T, D), F32), jax.ShapeDtypeStruct((T, D), BF16),
                     jax.ShapeDtypeStruct((1, 1), F32))
        out_specs = (row, row, row, pl.BlockSpec((1, 1), lambda i: (0, 0)))
    else:
        out_shape = (jax.ShapeDtypeStruct((T, D), BF16), jax.ShapeDtypeStruct((T, D), F32))
        out_specs = (row, row)
    return pl.pallas_call(body, out_shape=out_shape, grid=(T // tm,), in_specs=in_specs, out_specs=out_specs, name=name,
                          compiler_params=_cp(("arbitrary",)))(*args)


def _out_proj_bwd(dout_bf, w_out_t, pres, z, name):
    T, D = dout_bf.shape
    tm = min(_OP_TM, T)

    def body(do_ref, w_ref, pa, pb, pc, pd, ga, gb, gc, gd, dpa, dpb, dpc, dpd, dga, dgb, dgc, dgd):
        dy = jnp.dot(do_ref[...], w_ref[...], preferred_element_type=F32)
        for n_, (p, g, dp, dg) in enumerate(((pa, ga, dpa, dga), (pb, gb, dpb, dgb), (pc, gc, dpc, dgc), (pd, gd, dpd, dgd))):
            d = dy[:, n_ * GROUP_W:(n_ + 1) * GROUP_W]
            gv = g[...]
            dp[...] = d * _silu(gv)
            dg[...] = (d * p[...] * _dsilu(gv)).astype(BF16)

    pre_spec = pl.BlockSpec((tm, GROUP_W), lambda i: (i, 0))
    gate_specs = [pl.BlockSpec((tm, GROUP_W), functools.partial(lambda i, c: (i, c), c=c)) for c in _GATE_COLS]
    outs = tuple([jax.ShapeDtypeStruct((T, GROUP_W), F32)] * 4 + [jax.ShapeDtypeStruct((T, GROUP_W), BF16)] * 4)
    return pl.pallas_call(
        body, out_shape=outs, grid=(T // tm,),
        in_specs=[pl.BlockSpec((tm, D), lambda i: (i, 0)), pl.BlockSpec((D, 4 * GROUP_W), lambda i: (0, 0))] + [pre_spec] * 4 + gate_specs,
        out_specs=tuple([pre_spec] * 8), name=name, compiler_params=_cp(("parallel",)))(dout_bf, w_out_t, *pres, z, z, z, z)


_PAD = 16
_RC = 256


def _conv_fwd(z, conv_w32, conv_b, S, name):
    T = z.shape[0]
    E = T // S
    LW = 128

    def body(val_ref, glu_ref, w_ref, b_ref, y_ref, upad):
        upad[0:_PAD, :] = jnp.zeros((_PAD, LW), F32)
        upad[_PAD + S:_PAD + S + _PAD, :] = jnp.zeros((_PAD, LW), F32)
        upad[_PAD:_PAD + S, :] = val_ref[...] * _sigmoid(glu_ref[...])
        for r in range(S // _RC):
            acc = jnp.broadcast_to(b_ref[...], (_RC, LW))
            for k in range(CONV_K):
                st = r * _RC + k + 1
                acc = acc + upad[st:st + _RC, :] * w_ref[k:k + 1, :]
            y_ref[r * _RC:(r + 1) * _RC, :] = acc

    return pl.pallas_call(
        body, out_shape=jax.ShapeDtypeStruct((T, GROUP_W), F32), grid=(E, GROUP_W // LW),
        in_specs=[pl.BlockSpec((S, LW), lambda e, j: (e, A_VAL // LW + j)),
                  pl.BlockSpec((S, LW), lambda e, j: (e, A_GLU // LW + j)),
                  pl.BlockSpec((32, LW), lambda e, j: (0, j)),
                  pl.BlockSpec((1, LW), lambda e, j: (0, j))],
        out_specs=pl.BlockSpec((S, LW), lambda e, j: (e, j)),
        scratch_shapes=[pltpu.VMEM((S + 2 * _PAD, LW), F32)],
        name=name, compiler_params=_cp(("parallel", "parallel")))(z, z, conv_w32, conv_b)


def _conv_bwd(z, conv_w32, dyc, S, name):
    T = z.shape[0]
    E = T // S
    LW = 128

    def body(val_ref, glu_ref, w_ref, dy_ref, dval_ref, dglu_ref, dw_ref, db_ref, upad, dpad):
        e = pl.program_id(1)
        zeros = jnp.zeros((_PAD, LW), F32)
        upad[0:_PAD, :] = zeros
        upad[_PAD + S:_PAD + S + _PAD, :] = zeros
        dpad[0:_PAD, :] = zeros
        dpad[_PAD + S:_PAD + S + _PAD, :] = zeros
        upad[_PAD:_PAD + S, :] = val_ref[...] * _sigmoid(glu_ref[...])
        dpad[_PAD:_PAD + S, :] = dy_ref[...]

        @pl.when(e == 0)
        def _():
            dw_ref[...] = jnp.zeros_like(dw_ref)
            db_ref[...] = jnp.zeros_like(db_ref)

        db_ref[...] += jnp.sum(dy_ref[...], axis=0, keepdims=True)
        for r in range(S // _RC):
            dyr = dy_ref[r * _RC:(r + 1) * _RC, :]
            du = jnp.zeros((_RC, LW), F32)
            for k in range(CONV_K):
                st = r * _RC + k + 1
                dw_ref[k:k + 1, :] += jnp.sum(dyr * upad[st:st + _RC, :], axis=0, keepdims=True)
                sd = r * _RC + (CONV_K - 1 - k) + 1
                du = du + dpad[sd:sd + _RC, :] * w_ref[k:k + 1, :]
            sl = slice(r * _RC, (r + 1) * _RC)
            val = val_ref[sl, :]
            sg = _sigmoid(glu_ref[sl, :])
            dval_ref[sl, :] = (du * sg).astype(BF16)
            dglu_ref[sl, :] = (du * val * sg * (1.0 - sg)).astype(BF16)

    blk = pl.BlockSpec((S, LW), lambda j, e: (e, j))
    return pl.pallas_call(
        body, out_shape=(jax.ShapeDtypeStruct((T, GROUP_W), BF16), jax.ShapeDtypeStruct((T, GROUP_W), BF16),
                         jax.ShapeDtypeStruct((32, GROUP_W), F32), jax.ShapeDtypeStruct((1, GROUP_W), F32)),
        grid=(GROUP_W // LW, E),
        in_specs=[pl.BlockSpec((S, LW), lambda j, e: (e, A_VAL // LW + j)),
                  pl.BlockSpec((S, LW), lambda j, e: (e, A_GLU // LW + j)),
                  pl.BlockSpec((32, LW), lambda j, e: (0, j)), blk],
        out_specs=(blk, blk, pl.BlockSpec((32, LW), lambda j, e: (0, j)), pl.BlockSpec((1, LW), lambda j, e: (0, j))),
        scratch_shapes=[pltpu.VMEM((S + 2 * _PAD, LW), F32), pltpu.VMEM((S + 2 * _PAD, LW), F32)],
        name=name, compiler_params=_cp(("parallel", "arbitrary")))(z, z, conv_w32, dyc)


def _ln_silu_fwd(yc, g, b, name):
    T, C = yc.shape
    tm = 256

    def body(y_ref, g_ref, b_ref, o_ref):
        y = y_ref[...]
        mu = jnp.mean(y, axis=-1, keepdims=True)
        yc_ = y - mu
        r = lax.rsqrt(jnp.mean(yc_ * yc_, axis=-1, keepdims=True) + EPS)
        o_ref[...] = _silu(yc_ * r * g_ref[...] + b_ref[...])

    row = pl.BlockSpec((tm, C), lambda i: (i, 0))
    vec = pl.BlockSpec((1, C), lambda i: (0, 0))
    return pl.pallas_call(body, out_shape=jax.ShapeDtypeStruct((T, C), F32), grid=(T // tm,),
                          in_specs=[row, vec, vec], out_specs=row, name=name, compiler_params=_cp(("parallel",)))(yc, g, b)


def _ln_silu_bwd(yc, g, b, dpre, name):
    T, C = yc.shape
    tm = 256

    def body(y_ref, g_ref, b_ref, dp_ref, dy_ref, dg_ref, db_ref):
        y = y_ref[...]
        mu = jnp.mean(y, axis=-1, keepdims=True)
        yc_ = y - mu
        r = lax.rsqrt(jnp.mean(yc_ * yc_, axis=-1, keepdims=True) + EPS)
        xh = yc_ * r
        gv = g_ref[...]
        dln = dp_ref[...] * _dsilu(xh * gv + b_ref[...])
        dxh = dln * gv
        dy_ref[...] = r * (dxh - jnp.mean(dxh, axis=-1, keepdims=True) - xh * jnp.mean(dxh * xh, axis=-1, keepdims=True))

        @pl.when(pl.program_id(0) == 0)
        def _():
            dg_ref[...] = jnp.zeros_like(dg_ref)
            db_ref[...] = jnp.zeros_like(db_ref)

        dg_ref[...] += jnp.sum(dln * xh, axis=0, keepdims=True)
        db_ref[...] += jnp.sum(dln, axis=0, keepdims=True)

    row = pl.BlockSpec((tm, C), lambda i: (i, 0))
    vec = pl.BlockSpec((1, C), lambda i: (0, 0))
    return pl.pallas_call(
        body, out_shape=(jax.ShapeDtypeStruct((T, C), F32), jax.ShapeDtypeStruct((1, C), F32), jax.ShapeDtypeStruct((1, C), F32)),
        grid=(T // tm,), in_specs=[row, vec, vec, row], out_specs=(row, vec, vec), name=name,
        compiler_params=_cp(("arbitrary",)))(yc, g, b, dpre)


def _pool_counts(S, w, rows0, n):
    t = (lax.broadcasted_iota(jnp.int32, (n, 1), 0) + rows0)
    lo = jnp.maximum(t - w // 2, 0)
    hi = jnp.minimum(t + w // 2, S)
    return (hi - lo).astype(F32)


def _pool_fwd(z, pool_w, pool_scale, S, name):
    T = z.shape[0]
    E = T // S
    CG = 128

    def body(u_ref, w_ref, s_ref, o_ref, upad, dif):
        zeros = jnp.zeros((_PAD, GROUP_W), F32)
        upad[0:_PAD, :] = zeros
        upad[_PAD + S:_PAD + S + _PAD, :] = zeros
        upad[_PAD:_PAD + S, :] = u_ref[...]
        for gi, w in enumerate(POOL_WINDOWS):
            ls = slice(gi * CG, (gi + 1) * CG)
            for r in range(S // _RC):
                acc = jnp.zeros((_RC, CG), F32)
                for j in range(-(w // 2), w // 2):
                    st = _PAD + r * _RC + j
                    acc = acc + upad[st:st + _RC, ls]
                cnt = _pool_counts(S, w, r * _RC, _RC)
                dif[r * _RC:(r + 1) * _RC, :] = (acc / cnt - u_ref[r * _RC:(r + 1) * _RC, ls]).astype(BF16)
            yp = jnp.dot(dif[...], w_ref[gi], preferred_element_type=F32)
            o_ref[:, ls] = yp * s_ref[:, ls]

    return pl.pallas_call(
        body, out_shape=jax.ShapeDtypeStruct((T, GROUP_W), F32), grid=(E,),
        in_specs=[pl.BlockSpec((S, GROUP_W), lambda e: (e, D_VAL // GROUP_W)),
                  pl.BlockSpec((4, CG, CG), lambda e: (0, 0, 0)),
                  pl.BlockSpec((1, GROUP_W), lambda e: (0, 0))],
        out_specs=pl.BlockSpec((S, GROUP_W), lambda e: (e, 0)),
        scratch_shapes=[pltpu.VMEM((S + 2 * _PAD, GROUP_W), F32), pltpu.VMEM((S, CG), BF16)],
        name=name, compiler_params=_cp(("parallel",)))(z, pool_w, pool_scale)


def _pool_bwd(z, pool_w, pool_scale, dpre, S, name):
    T = z.shape[0]
    E = T // S
    CG = 128

    def body(u_ref, w_ref, s_ref, dp_ref, du_ref, dw_ref, ds_ref, upad, dif, qpad):
        zeros = jnp.zeros((_PAD, GROUP_W), F32)
        upad[0:_PAD, :] = zeros
        upad[_PAD + S:_PAD + S + _PAD, :] = zeros
        upad[_PAD:_PAD + S, :] = u_ref[...]
        zc = jnp.zeros((_PAD, CG), F32)
        qpad[0:_PAD, :] = zc
        qpad[_PAD + S:_PAD + S + _PAD, :] = zc

        @pl.when(pl.program_id(0) == 0)
        def _():
            dw_ref[...] = jnp.zeros_like(dw_ref)
            ds_ref[...] = jnp.zeros_like(ds_ref)

        for gi, w in enumerate(POOL_WINDOWS):
            ls = slice(gi * CG, (gi + 1) * CG)
            for r in range(S // _RC):
                acc = jnp.zeros((_RC, CG), F32)
                for j in range(-(w // 2), w // 2):
                    st = _PAD + r * _RC + j
                    acc = acc + upad[st:st + _RC, ls]
                cnt = _pool_counts(S, w, r * _RC, _RC)
                dif[r * _RC:(r + 1) * _RC, :] = (acc / cnt - u_ref[r * _RC:(r + 1) * _RC, ls]).astype(BF16)
            dp = dp_ref[:, ls]
            yp = jnp.dot(dif[...], w_ref[gi], preferred_element_type=F32)
            ds_ref[:, ls] += jnp.sum(dp * yp, axis=0, keepdims=True)
            dys = (dp * s_ref[:, ls]).astype(BF16)
            dw_ref[gi] += lax.dot_general(dif[...], dys, (((0,), (0,)), ((), ())), preferred_element_type=F32)
            dm = lax.dot_general(dys, w_ref[gi], (((1,), (1,)), ((), ())), preferred_element_type=F32)
            for r in range(S // _RC):
                cnt = _pool_counts(S, w, r * _RC, _RC)
                qpad[_PAD + r * _RC:_PAD + (r + 1) * _RC, :] = dm[r * _RC:(r + 1) * _RC, :] / cnt
            for r in range(S // _RC):
                acc = -dm[r * _RC:(r + 1) * _RC, :]
                for j in range(-(w // 2) + 1, w // 2 + 1):
                    st = _PAD + r * _RC + j
                    acc = acc + qpad[st:st + _RC, :]
                du_ref[r * _RC:(r + 1) * _RC, ls] = acc.astype(BF16)

    return pl.pallas_call(
        body, out_shape=(jax.ShapeDtypeStruct((T, GROUP_W), BF16), jax.ShapeDtypeStruct((4, CG, CG), F32),
                         jax.ShapeDtypeStruct((1, GROUP_W), F32)), grid=(E,),
        in_specs=[pl.BlockSpec((S, GROUP_W), lambda e: (e, D_VAL // GROUP_W)),
                  pl.BlockSpec((4, CG, CG), lambda e: (0, 0, 0)),
                  pl.BlockSpec((1, GROUP_W), lambda e: (0, 0)),
                  pl.BlockSpec((S, GROUP_W), lambda e: (e, 0))],
        out_specs=(pl.BlockSpec((S, GROUP_W), lambda e: (e, 0)), pl.BlockSpec((4, CG, CG), lambda e: (0, 0, 0)),
                   pl.BlockSpec((1, GROUP_W), lambda e: (0, 0))),
        scratch_shapes=[pltpu.VMEM((S + 2 * _PAD, GROUP_W), F32), pltpu.VMEM((S, CG), BF16),
                        pltpu.VMEM((S + 2 * _PAD, CG), F32)],
        name=name, compiler_params=_cp(("arbitrary",)))(z, pool_w, pool_scale, dpre)


def _na_tables():
    d = np.arange(NA_ROWS)[:, None]
    kr = np.arange(NA_ROWS)[None, :]
    ro = kr - d + (NA_ROWS - 1)
    qc = np.arange(GRID_W)[:, None]
    kc = np.arange(GRID_W)[None, :]
    cs = np.clip(qc - NA_COLS // 2, 0, GRID_W - NA_COLS)
    valid = (kc >= cs) & (kc < cs + NA_COLS)
    co = np.clip(kc - qc + (NA_COLS - 1), 0, 2 * NA_COLS - 2)
    return ro, co, valid


def _na_onehots():
    ro, co, valid = _na_tables()
    e_np = np.zeros((GRID_W, GRID_W, 128), np.float32)
    qi, ki = np.nonzero(valid)
    e_np[qi, ki, co[qi, ki]] = 1.0
    a_np = np.zeros((16, NA_ROWS * NA_ROWS), np.float32)
    a_np[ro.reshape(-1), np.arange(NA_ROWS * NA_ROWS)] = 1.0
    mask = np.where(valid, 0.0, NEG).astype(np.float32).reshape(1, GRID_W * GRID_W)
    return e_np.reshape(GRID_W * GRID_W, 128), a_np, mask


def _na_bias(rpb, name):
    e_np, _, mask = _na_onehots()
    H = NA_HEADS
    rp = jnp.pad(rpb, ((0, 0), (0, 1), (0, 128 - rpb.shape[2])))

    def bands(r_ref, e_ref, m_ref, o_ref):
        o_ref[0] = lax.dot_general(r_ref[0], e_ref[...], (((1,), (1,)), ((), ())), precision=HI,
                                   preferred_element_type=F32) + m_ref[...]

    t = pl.pallas_call(
        bands, out_shape=jax.ShapeDtypeStruct((H, 16, GRID_W * GRID_W), F32), grid=(H,),
        in_specs=[pl.BlockSpec((1, 16, 128), lambda h: (h, 0, 0)),
                  pl.BlockSpec((GRID_W * GRID_W, 128), lambda h: (0, 0)),
                  pl.BlockSpec((1, GRID_W * GRID_W), lambda h: (0, 0))],
        out_specs=pl.BlockSpec((1, 16, GRID_W * GRID_W), lambda h: (h, 0, 0)),
        name=name + "_bands", compiler_params=_cp(("parallel",)))(rp, jnp.asarray(e_np), jnp.asarray(mask))
    t = t.reshape(H, 16, GRID_W, GRID_W)

    def place(t_ref, o_ref):
        for d in range(NA_ROWS):
            for kr in range(NA_ROWS):
                o_ref[0, d, :, kr * GRID_W:(kr + 1) * GRID_W] = t_ref[0, kr - d + NA_ROWS - 1]

    return pl.pallas_call(
        place, out_shape=jax.ShapeDtypeStruct((H, NA_ROWS, GRID_W, NA_ROWS * GRID_W), F32), grid=(H,),
        in_specs=[pl.BlockSpec((1, 16, GRID_W, GRID_W), lambda h: (h, 0, 0, 0))],
        out_specs=pl.BlockSpec((1, NA_ROWS, GRID_W, NA_ROWS * GRID_W), lambda h: (h, 0, 0, 0)),
        name=name, compiler_params=_cp(("parallel",)))(t)


def _seg_mean_matrix(width, seg):
    i = np.arange(width)
    return jnp.asarray((i[:, None] // seg == i[None, :] // seg).astype(np.float32) / seg, BF16)


def _seg_mean(x, seg_ref):
    hi = x.astype(BF16)
    lo = (x - hi.astype(F32)).astype(BF16)
    return (jnp.dot(hi, seg_ref[...], preferred_element_type=F32) + jnp.dot(lo, seg_ref[...], preferred_element_type=F32))


def _na_fwd(z, qg, kg, bias, S, name):
    T = z.shape[0]
    E = T // S
    rows = S // GRID_W
    WIN = NA_ROWS * GRID_W
    seg = _seg_mean_matrix(128, NA_DH)

    def body(q_ref, k_ref, v_ref, qg_ref, kg_ref, bias_ref, seg_ref, o_ref, qs, ks, vs, s_all, p_all):
        for c in range(S // _RC):
            sl = slice(c * _RC, (c + 1) * _RC)
            q = q_ref[sl, :]
            k = k_ref[sl, :]
            qn = q * lax.rsqrt(_seg_mean(q * q, seg_ref) + EPS) * qg_ref[...]
            kn = k * lax.rsqrt(_seg_mean(k * k, seg_ref) + EPS) * kg_ref[...]
            v = v_ref[sl, :]
            for hh in range(2):
                ls = slice(hh * NA_DH, (hh + 1) * NA_DH)
                qs[hh, sl, :] = qn[:, ls].astype(BF16)
                ks[hh, sl, :] = kn[:, ls].astype(BF16)
                vs[hh, sl, :] = v[:, ls].astype(BF16)
        def where(r):
            rs = jnp.clip(r - NA_ROWS // 2, 0, rows - NA_ROWS)
            return rs, pl.multiple_of(r * GRID_W, GRID_W), pl.multiple_of(rs * GRID_W, GRID_W)

        def scores(r, carry):
            rs, q0, k0 = where(r)
            for hh in range(2):
                s = lax.dot_general(qs[hh, pl.ds(q0, GRID_W), :], ks[hh, pl.ds(k0, WIN), :], (((1,), (1,)), ((), ())),
                                    preferred_element_type=F32) * (NA_DH ** -0.5)
                s_all[hh, pl.ds(q0, GRID_W), :] = s + bias_ref[hh, r - rs]
            return carry
        lax.fori_loop(0, rows, scores, 0, unroll=8)

        def soft(r, carry):
            _, q0, _ = where(r)
            for hh in range(2):
                s = s_all[hh, pl.ds(q0, GRID_W), :]
                p = jnp.exp(s - jnp.max(s, axis=-1, keepdims=True))
                p_all[hh, pl.ds(q0, GRID_W), :] = (p * (1.0 / jnp.sum(p, axis=-1, keepdims=True))).astype(BF16)
            return carry
        lax.fori_loop(0, rows, soft, 0, unroll=2)

        def outp(r, carry):
            _, q0, k0 = where(r)
            outs = [jnp.dot(p_all[hh, pl.ds(q0, GRID_W), :], vs[hh, pl.ds(k0, WIN), :], preferred_element_type=F32)
                    for hh in range(2)]
            o_ref[pl.ds(q0, GRID_W), :] = jnp.concatenate(outs, axis=1)
            return carry
        lax.fori_loop(0, rows, outp, 0, unroll=8)

    LW = 128
    return pl.pallas_call(
        body, out_shape=jax.ShapeDtypeStruct((T, GROUP_W), F32), grid=(E, GROUP_W // LW),
        in_specs=[pl.BlockSpec((S, LW), lambda e, j: (e, B_Q // LW + j)),
                  pl.BlockSpec((S, LW), lambda e, j: (e, B_K // LW + j)),
                  pl.BlockSpec((S, LW), lambda e, j: (e, B_V // LW + j)),
                  pl.BlockSpec((1, LW), lambda e, j: (0, j)),
                  pl.BlockSpec((1, LW), lambda e, j: (0, j)),
                  pl.BlockSpec((2, NA_ROWS, GRID_W, WIN), lambda e, j: (j, 0, 0, 0)),
                  pl.BlockSpec((LW, LW), lambda e, j: (0, 0))],
        out_specs=pl.BlockSpec((S, LW), lambda e, j: (e, j)),
        scratch_shapes=[pltpu.VMEM((2, S, NA_DH), BF16)] * 3 + [pltpu.VMEM((2, S, WIN), F32), pltpu.VMEM((2, S, WIN), BF16)],
        name=name, compiler_params=_cp(("parallel", "parallel")))(z, z, z, qg, kg, bias, seg)


def _na_bwd(z, qg, kg, bias, do, S, name):
    T = z.shape[0]
    E = T // S
    rows = S // GRID_W
    WIN = NA_ROWS * GRID_W
    seg = _seg_mean_matrix(128, NA_DH)
    SC = NA_DH ** -0.5

    def body(q_ref, k_ref, v_ref, qg_ref, kg_ref, bias_ref, seg_ref, do_ref,
             dq_ref, dk_ref, dv_ref, dbias_ref, dqg_ref, dkg_ref, qs, ks, vs, dos, dqn, dkn, dvs, akt, avt,
             s_all, dp_all, p_all, ds_all):
        e = pl.program_id(1)

        @pl.when(e == 0)
        def _():
            dbias_ref[...] = jnp.zeros_like(dbias_ref)
            dqg_ref[...] = jnp.zeros_like(dqg_ref)
            dkg_ref[...] = jnp.zeros_like(dkg_ref)

        for c in range(S // _RC):
            sl = slice(c * _RC, (c + 1) * _RC)
            q = q_ref[sl, :]
            k = k_ref[sl, :]
            qn = q * lax.rsqrt(_seg_mean(q * q, seg_ref) + EPS) * qg_ref[...]
            kn = k * lax.rsqrt(_seg_mean(k * k, seg_ref) + EPS) * kg_ref[...]
            v = v_ref[sl, :]
            dd = do_ref[sl, :]
            for hh in range(2):
                ls = slice(hh * NA_DH, (hh + 1) * NA_DH)
                qs[hh, sl, :] = qn[:, ls].astype(BF16)
                ks[hh, sl, :] = kn[:, ls].astype(BF16)
                vs[hh, sl, :] = v[:, ls].astype(BF16)
                dos[hh, sl, :] = dd[:, ls].astype(BF16)
        akt[...] = jnp.zeros_like(akt)
        avt[...] = jnp.zeros_like(avt)

        def where(r):
            rs = jnp.clip(r - NA_ROWS // 2, 0, rows - NA_ROWS)
            return rs, pl.multiple_of(r * GRID_W, GRID_W), pl.multiple_of(rs * GRID_W, GRID_W)

        for hh in range(2):
            ls = slice(hh * NA_DH, (hh + 1) * NA_DH)

            def products(r, carry, hh=hh):
                rs, q0, k0 = where(r)
                s = lax.dot_general(qs[hh, pl.ds(q0, GRID_W), :], ks[hh, pl.ds(k0, WIN), :], (((1,), (1,)), ((), ())),
                                    preferred_element_type=F32) * SC
                s_all[pl.ds(q0, GRID_W), :] = s + bias_ref[hh, r - rs]
                dp_all[pl.ds(q0, GRID_W), :] = lax.dot_general(dos[hh, pl.ds(q0, GRID_W), :], vs[hh, pl.ds(k0, WIN), :],
                                                               (((1,), (1,)), ((), ())), preferred_element_type=F32)
                return carry
            lax.fori_loop(0, rows, products, 0, unroll=8)

            def soft(r, carry, hh=hh):
                rs, q0, _ = where(r)
                s = s_all[pl.ds(q0, GRID_W), :]
                p = jnp.exp(s - jnp.max(s, axis=-1, keepdims=True))
                p = p * (1.0 / jnp.sum(p, axis=-1, keepdims=True))
                dp = dp_all[pl.ds(q0, GRID_W), :]
                ds = p * (dp - jnp.sum(p * dp, axis=-1, keepdims=True))
                dbias_ref[hh, r - rs] += ds
                p_all[pl.ds(q0, GRID_W), :] = p.astype(BF16)
                ds_all[pl.ds(q0, GRID_W), :] = ds.astype(BF16)
                return carry
            lax.fori_loop(0, rows, soft, 0, unroll=2)

            def grads(r, carry, hh=hh, ls=ls):
                rs, q0, k0 = where(r)
                par = rs % 2
                t0 = (rs + par) // 2
                qr = qs[hh, pl.ds(q0, GRID_W), :]
                dor = dos[hh, pl.ds(q0, GRID_W), :]
                dsb = ds_all[pl.ds(q0, GRID_W), :]
                dqn[pl.ds(q0, GRID_W), ls] = jnp.dot(dsb, ks[hh, pl.ds(k0, WIN), :], preferred_element_type=F32) * SC
                dkt = lax.dot_general(qr, dsb, (((0,), (0,)), ((), ())), preferred_element_type=F32) * SC
                dvt = lax.dot_general(dor, p_all[pl.ds(q0, GRID_W), :], (((0,), (0,)), ((), ())), preferred_element_type=F32)
                akt[hh, par, pl.ds(t0, WIN // 128)] += jnp.stack([dkt[:, 128 * i:128 * (i + 1)] for i in range(WIN // 128)])
                avt[hh, par, pl.ds(t0, WIN // 128)] += jnp.stack([dvt[:, 128 * i:128 * (i + 1)] for i in range(WIN // 128)])
                return carry
            lax.fori_loop(0, rows, grads, 0, unroll=8)

        for hh in range(2):
            ls = slice(hh * NA_DH, (hh + 1) * NA_DH)
            for i in range(S // 128):
                for acc, dst in ((akt, dkn), (avt, dvs)):
                    odd = jnp.concatenate([acc[hh, 1, i][:, NA_DH:], acc[hh, 1, i + 1][:, :NA_DH]], axis=1)
                    dst[128 * i:128 * (i + 1), ls] = (acc[hh, 0, i] + odd).T

        for c in range(S // _RC):
            sl = slice(c * _RC, (c + 1) * _RC)
            for x_ref, g_ref, dn, dx_ref, dg_ref in ((q_ref, qg_ref, dqn, dq_ref, dqg_ref), (k_ref, kg_ref, dkn, dk_ref, dkg_ref)):
                x = x_ref[sl, :]
                r_ = lax.rsqrt(_seg_mean(x * x, seg_ref) + EPS)
                xh = x * r_
                d = dn[sl, :]
                dxh = d * g_ref[...]
                mean = _seg_mean(dxh * xh, seg_ref)
                dx_ref[sl, :] = (r_ * (dxh - xh * mean)).astype(BF16)
                dg_ref[...] += jnp.sum(d * xh, axis=0, keepdims=True)
            dv_ref[sl, :] = dvs[sl, :].astype(BF16)

    LW = 128
    blk = pl.BlockSpec((S, LW), lambda j, e: (e, j))
    vec = pl.BlockSpec((1, LW), lambda j, e: (0, j))
    bsp = pl.BlockSpec((2, NA_ROWS, GRID_W, WIN), lambda j, e: (j, 0, 0, 0))
    return pl.pallas_call(
        body, out_shape=(jax.ShapeDtypeStruct((T, GROUP_W), BF16),) * 3 + (
            jax.ShapeDtypeStruct((NA_HEADS, NA_ROWS, GRID_W, WIN), F32),
            jax.ShapeDtypeStruct((1, GROUP_W), F32), jax.ShapeDtypeStruct((1, GROUP_W), F32)),
        grid=(GROUP_W // LW, E),
        in_specs=[pl.BlockSpec((S, LW), lambda j, e: (e, B_Q // LW + j)),
                  pl.BlockSpec((S, LW), lambda j, e: (e, B_K // LW + j)),
                  pl.BlockSpec((S, LW), lambda j, e: (e, B_V // LW + j)),
                  vec, vec, bsp, pl.BlockSpec((LW, LW), lambda j, e: (0, 0)), blk],
        out_specs=(blk, blk, blk, bsp, vec, vec),
        scratch_shapes=[pltpu.VMEM((2, S, NA_DH), BF16)] * 4 + [pltpu.VMEM((S, LW), F32)] * 3
        + [pltpu.VMEM((2, 2, S // 128 + 1, NA_DH, 128), F32)] * 2
        + [pltpu.VMEM((S, WIN), F32)] * 2 + [pltpu.VMEM((S, WIN), BF16)] * 2,
        name=name, compiler_params=_cp(("parallel", "arbitrary")))(z, z, z, qg, kg, bias, seg, do)


def _na_rpb_grad(dbias, name):
    e_np, _, _ = _na_onehots()
    H = NA_HEADS
    nro = 2 * NA_ROWS - 1

    def fold(x_ref, o_ref):
        for ro in range(nro):
            acc = None
            for d in range(NA_ROWS):
                kr = ro + d - (NA_ROWS - 1)
                if 0 <= kr < NA_ROWS:
                    blk = x_ref[0, d, :, kr * GRID_W:(kr + 1) * GRID_W]
                    acc = blk if acc is None else acc + blk
            o_ref[0, ro] = acc
        o_ref[0, nro] = jnp.zeros((GRID_W, GRID_W), F32)

    t = pl.pallas_call(
        fold, out_shape=jax.ShapeDtypeStruct((H, 16, GRID_W, GRID_W), F32), grid=(H,),
        in_specs=[pl.BlockSpec((1, NA_ROWS, GRID_W, NA_ROWS * GRID_W), lambda h: (h, 0, 0, 0))],
        out_specs=pl.BlockSpec((1, 16, GRID_W, GRID_W), lambda h: (h, 0, 0, 0)),
        name=name + "_fold", compiler_params=_cp(("parallel",)))(dbias)
    t = t.reshape(H, 16, GRID_W * GRID_W)

    def body(x_ref, e_ref, o_ref):
        o_ref[0] = jnp.dot(x_ref[0], e_ref[...], precision=HI, preferred_element_type=F32)

    out = pl.pallas_call(
        body, out_shape=jax.ShapeDtypeStruct((H, 16, 128), F32), grid=(H,),
        in_specs=[pl.BlockSpec((1, 16, GRID_W * GRID_W), lambda h: (h, 0, 0)),
                  pl.BlockSpec((GRID_W * GRID_W, 128), lambda h: (0, 0))],
        out_specs=pl.BlockSpec((1, 16, 128), lambda h: (h, 0, 0)),
        name=name, compiler_params=_cp(("parallel",)))(t, jnp.asarray(e_np))
    return out[:, :nro, :2 * NA_COLS - 1]


_HK = GLA_HEADS * GLA_DK
_HV = GLA_HEADS * GLA_DV


def _gla_consts(reverse):
    i = np.arange(CHUNK)
    tri = (i[:, None] <= i[None, :]) if reverse else (i[:, None] >= i[None, :])
    j = np.arange(_HK)
    oseg = (j[:, None] // GLA_DK == j[None, :] // GLA_DK)
    return (jnp.asarray(tri.astype(np.float32)), jnp.asarray(tri.T.astype(np.float32)), jnp.asarray(oseg.astype(np.float32), BF16))


def _log_decay(lr, a2, ab):
    zg = jnp.dot(lr, a2, precision=HI, preferred_element_type=F32) + ab
    g = (jnp.minimum(zg, 0.0) - jnp.log(1.0 + jnp.exp(-jnp.abs(zg)))) * (1.0 / GLA_TAU)
    return zg, g


def _dotf(a, b, dn):
    return lax.dot_general(a, b, dn, precision=HI, preferred_element_type=F32)


def _dotb(a, b, dn):
    return lax.dot_general(a.astype(BF16), b.astype(BF16), dn, preferred_element_type=F32)


_COLS = 4
_SUB = 16
_NSUB = CHUNK // _SUB


def _gla_cross_blocks(reverse):
    return range(0, _NSUB - 1) if reverse else range(1, _NSUB)


def _gla_cross_terms(s, reverse, b_s, q_s, k_s, oseg_ref):
    r0 = s * _SUB
    ref = r0 + (_SUB - 1 if reverse else 0)
    bref = b_s[ref:ref + 1, :]
    rowj = lax.broadcasted_iota(jnp.int32, (CHUNK, 1), 0)
    seen = (rowj >= r0 + _SUB) if reverse else (rowj < r0)
    ek = jnp.where(seen, jnp.exp(jnp.minimum(bref - b_s[...], 0.0)), 0.0)
    kt = k_s[...] * ek
    eq = jnp.exp(jnp.minimum(b_s[r0:r0 + _SUB, :] - bref, 0.0))
    qt = q_s[r0:r0 + _SUB, :] * eq
    nmat = jnp.concatenate([kt.astype(BF16)] * GLA_HEADS, axis=0) * oseg_ref[...]
    return qt, eq, kt, ek, nmat


_NN = (((1,), (0,)), ((), ()))
_NT = (((1,), (1,)), ((), ()))
_TN = (((0,), (0,)), ((), ()))


_DT = 256


def _gla_block_tri(reverse):
    i = np.arange(_DT)
    same = i[:, None] // CHUNK == i[None, :] // CHUNK
    tri = (i[:, None] <= i[None, :]) if reverse else (i[:, None] >= i[None, :])
    return (tri & same).astype(np.float32)


def _gla_decay_fwd(z, a2, ab, reverse, name):
    T = z.shape[0]
    nc = _DT // CHUNK

    def body(lr_ref, a2_ref, ab_ref, m_ref, b_ref, ec_ref):
        _, g = _log_decay(lr_ref[...], a2_ref[...], ab_ref[...])
        b_ref[...] = _dotf(m_ref[...], g, _NN)
        for c in range(nc):
            ec_ref[c] = jnp.exp(_dotf(g[c * CHUNK:(c + 1) * CHUNK, :], jnp.ones((CHUNK, GLA_DV), F32), _TN))

    return pl.pallas_call(
        body, out_shape=(jax.ShapeDtypeStruct((T, _HK), F32), jax.ShapeDtypeStruct((T // CHUNK, _HK, GLA_DV), F32)),
        grid=(T // _DT,),
        in_specs=[pl.BlockSpec((_DT, 128), lambda i: (i, LR_OFF // 128)),
                  pl.BlockSpec((128, _HK), lambda i: (0, 0)),
                  pl.BlockSpec((1, _HK), lambda i: (0, 0)),
                  pl.BlockSpec((_DT, _DT), lambda i: (0, 0))],
        out_specs=(pl.BlockSpec((_DT, _HK), lambda i: (i, 0)), pl.BlockSpec((nc, _HK, GLA_DV), lambda i: (i, 0, 0))),
        name=name, compiler_params=_cp(("parallel",)))(z, a2, ab, jnp.asarray(_gla_block_tri(reverse)))


def _gla_decay_bwd(z, a2_f, ab_f, a2_b, ab_b, db_f, db_b, name):
    T = z.shape[0]

    def body(lr_ref, a2f_ref, abf_ref, a2b_ref, abb_ref, mf_ref, mb_ref, dbf_ref, dbb_ref,
             dlr_ref, da2f_ref, dabf_ref, da2b_ref, dabb_ref):
        @pl.when(pl.program_id(0) == 0)
        def _():
            for r in (da2f_ref, dabf_ref, da2b_ref, dabb_ref):
                r[...] = jnp.zeros_like(r)

        lr = lr_ref[...]
        dlr = jnp.zeros((_DT, 128), F32)
        for a2_ref, ab_ref, mt_ref, db_ref, da2_ref, dab_ref in ((a2f_ref, abf_ref, mf_ref, dbf_ref, da2f_ref, dabf_ref),
                                                                 (a2b_ref, abb_ref, mb_ref, dbb_ref, da2b_ref, dabb_ref)):
            zg, _ = _log_decay(lr, a2_ref[...], ab_ref[...])
            dg = _dotf(mt_ref[...], db_ref[...], _NN)
            dzg = dg * (1.0 / (1.0 + jnp.exp(zg))) * (1.0 / GLA_TAU)
            dlr = dlr + _dotf(dzg, a2_ref[...], _NT)
            da2_ref[...] += _dotf(lr, dzg, _TN)
            dab_ref[...] += jnp.sum(dzg, axis=0, keepdims=True)
        dlr_ref[...] = dlr.astype(BF16)

    a2s = pl.BlockSpec((128, _HK), lambda i: (0, 0))
    abs_ = pl.BlockSpec((1, _HK), lambda i: (0, 0))
    ms = pl.BlockSpec((_DT, _DT), lambda i: (0, 0))
    row = pl.BlockSpec((_DT, _HK), lambda i: (i, 0))
    return pl.pallas_call(
        body, out_shape=(jax.ShapeDtypeStruct((T, 128), BF16), jax.ShapeDtypeStruct((128, _HK), F32), jax.ShapeDtypeStruct((1, _HK), F32),
                         jax.ShapeDtypeStruct((128, _HK), F32), jax.ShapeDtypeStruct((1, _HK), F32)),
        grid=(T // _DT,),
        in_specs=[pl.BlockSpec((_DT, 128), lambda i: (i, LR_OFF // 128)), a2s, abs_, a2s, abs_, ms, ms, row, row],
        out_specs=(pl.BlockSpec((_DT, 128), lambda i: (i, 0)), a2s, abs_, a2s, abs_),
        name=name, compiler_params=_cp(("arbitrary",)))(
            z, a2_f, ab_f, a2_b, ab_b, jnp.asarray(_gla_block_tri(False).T), jnp.asarray(_gla_block_tri(True).T), db_f, db_b)


def _gla_fwd(z, b_all, ecol, S, reverse, name):
    T = z.shape[0]
    E = T // S
    n = S // CHUNK
    _, _, oseg = _gla_consts(reverse)
    last = 0 if reverse else CHUNK - 1

    def body(q_ref, k_ref, v_ref, b_ref, ec_ref, oseg_ref, o_ref, a_ref, st_ref, st, b_s, q_s, k_s):
        @pl.when(pl.program_id(1) == 0)
        def _():
            st[...] = jnp.zeros_like(st)

        q = q_ref[...] * (GLA_DK ** -0.5)
        k = k_ref[...]
        v = v_ref[...]
        b = b_ref[...]
        bl_row = b_ref[last:last + 1, :]
        e_col = ec_ref[0]
        b_s[...] = b
        q_s[...] = q
        k_s[...] = k
        lane = lax.broadcasted_iota(jnp.int32, (1, _HK), 1) % GLA_DK

        rowi = lax.broadcasted_iota(jnp.int32, (CHUNK, 1), 0)
        blk0 = (rowi // _SUB) * _SUB

        def cols(jj, a):
            ts = []
            for u in range(_COLS):
                jp = jj * _COLS + u
                tiles = []
                for s in range(_NSUB):
                    rs_ = slice(s * _SUB, (s + 1) * _SUB)
                    bj = b_s[pl.ds(s * _SUB + jp, 1), :]
                    kj = k_s[pl.ds(s * _SUB + jp, 1), :]
                    tiles.append(q_s[rs_, :] * jnp.exp(jnp.minimum(b_s[rs_, :] - bj, 0.0)) * kj)
                ts.append(jnp.concatenate(tiles, axis=0).astype(BF16))
            r = jnp.dot(jnp.concatenate(ts, axis=0), oseg_ref[...], preferred_element_type=F32)
            for u in range(_COLS):
                a = jnp.where(lane == blk0 + (jj * _COLS + u), r[u * CHUNK:(u + 1) * CHUNK, :], a)
            return a

        a = lax.fori_loop(0, _SUB // _COLS, cols, jnp.zeros((CHUNK, _HK), F32))
        keep = (rowi <= lane) if reverse else (rowi >= lane)
        a = jnp.where(keep, a, 0.0)
        cross = []
        for s in range(_NSUB):
            if s in _gla_cross_blocks(reverse):
                qt, _, _, _, nmat = _gla_cross_terms(s, reverse, b_s, q_s, k_s, oseg_ref)
                cross.append(lax.dot_general(qt.astype(BF16), nmat, _NT, preferred_element_type=F32))
            else:
                cross.append(jnp.zeros((_SUB, _HK), F32))
        a = a + jnp.concatenate(cross, axis=0)
        a_ref[...] = a
        st_ref[0] = st[...]
        qb = q * jnp.exp(b)
        kd = k * jnp.exp(bl_row - b)
        for h in range(GLA_HEADS):
            ks_ = slice(h * GLA_DK, (h + 1) * GLA_DK)
            vs_ = slice(h * GLA_DV, (h + 1) * GLA_DV)
            s_h = st[ks_, :]
            o_ref[:, vs_] = _dotb(qb[:, ks_], s_h, _NN) + _dotb(a[:, ks_], v[:, vs_], _NN)
            st[ks_, :] = s_h * e_col[ks_, :] + _dotb(kd[:, ks_], v[:, vs_], _TN)

    def rowblk(e, c):
        return e * n + ((n - 1 - c) if reverse else c)

    return pl.pallas_call(
        body, out_shape=(jax.ShapeDtypeStruct((T, _HV), F32), jax.ShapeDtypeStruct((T, _HK), F32),
                         jax.ShapeDtypeStruct((T // CHUNK, _HK, GLA_DV), F32)),
        grid=(E, n),
        in_specs=[pl.BlockSpec((CHUNK, _HK), lambda e, c: (rowblk(e, c), C_Q // _HK)),
                  pl.BlockSpec((CHUNK, _HK), lambda e, c: (rowblk(e, c), C_K // _HK)),
                  pl.BlockSpec((CHUNK, _HV), lambda e, c: (rowblk(e, c), C_V // _HV)),
                  pl.BlockSpec((CHUNK, _HK), lambda e, c: (rowblk(e, c), 0)),
                  pl.BlockSpec((1, _HK, GLA_DV), lambda e, c: (rowblk(e, c), 0, 0)),
                  pl.BlockSpec((_HK, _HK), lambda e, c: (0, 0))],
        out_specs=(pl.BlockSpec((CHUNK, _HV), lambda e, c: (rowblk(e, c), 0)),
                   pl.BlockSpec((CHUNK, _HK), lambda e, c: (rowblk(e, c), 0)),
                   pl.BlockSpec((1, _HK, GLA_DV), lambda e, c: (rowblk(e, c), 0, 0))),
        scratch_shapes=[pltpu.VMEM((_HK, GLA_DV), F32)] + [pltpu.VMEM((CHUNK, _HK), F32)] * 3,
        name=name, compiler_params=_cp(("parallel", "arbitrary")))(z, z, z, b_all, ecol, oseg)


def _gla_fwd_both(z, b_f, ec_f, b_b, ec_b, S, name):
    T = z.shape[0]
    E = T // S
    n = S // CHUNK
    _, _, oseg = _gla_consts(False)

    def body(*refs):
        oseg_ref = refs[10]
        dirs = []
        for di, reverse in enumerate((False, True)):
            q_ref, k_ref, v_ref, b_ref, ec_ref = refs[5 * di:5 * di + 5]
            o_ref, a_ref, st_ref = refs[11 + 3 * di:14 + 3 * di]
            st, b_s, q_s, k_s = refs[17 + 4 * di:21 + 4 * di]
            dirs.append((reverse, q_ref, k_ref, v_ref, b_ref, ec_ref, o_ref, a_ref, st_ref, st, b_s, q_s, k_s))

        @pl.when(pl.program_id(1) == 0)
        def _():
            for d in dirs:
                d[9][...] = jnp.zeros_like(d[9])

        lane = lax.broadcasted_iota(jnp.int32, (1, _HK), 1) % GLA_DK
        rowi = lax.broadcasted_iota(jnp.int32, (CHUNK, 1), 0)
        blk0 = (rowi // _SUB) * _SUB
        for (_, q_ref, k_ref, _, b_ref, _, _, _, _, _, b_s, q_s, k_s) in dirs:
            b_s[...] = b_ref[...]
            q_s[...] = q_ref[...] * (GLA_DK ** -0.5)
            k_s[...] = k_ref[...]

        def cols(d, jj, a):
            b_s, q_s, k_s = d[10], d[11], d[12]
            ts = []
            for u in range(_COLS):
                jp = jj * _COLS + u
                tiles = []
                for s in range(_NSUB):
                    rs_ = slice(s * _SUB, (s + 1) * _SUB)
                    bj = b_s[pl.ds(s * _SUB + jp, 1), :]
                    kj = k_s[pl.ds(s * _SUB + jp, 1), :]
                    tiles.append(q_s[rs_, :] * jnp.exp(jnp.minimum(b_s[rs_, :] - bj, 0.0)) * kj)
                ts.append(jnp.concatenate(tiles, axis=0).astype(BF16))
            r = jnp.dot(jnp.concatenate(ts, axis=0), oseg_ref[...], preferred_element_type=F32)
            for u in range(_COLS):
                a = jnp.where(lane == blk0 + (jj * _COLS + u), r[u * CHUNK:(u + 1) * CHUNK, :], a)
            return a

        zero = jnp.zeros((CHUNK, _HK), F32)
        acc = lax.fori_loop(0, _SUB // _COLS, lambda jj, c: tuple(cols(d, jj, a) for d, a in zip(dirs, c)), (zero, zero),
                            unroll=True)

        for (reverse, _, _, v_ref, b_ref, ec_ref, o_ref, a_ref, st_ref, st, b_s, q_s, k_s), a in zip(dirs, acc):
            last = 0 if reverse else CHUNK - 1
            keep = (rowi <= lane) if reverse else (rowi >= lane)
            a = jnp.where(keep, a, 0.0)
            cross = []
            for s in range(_NSUB):
                if s in _gla_cross_blocks(reverse):
                    qt, _, _, _, nmat = _gla_cross_terms(s, reverse, b_s, q_s, k_s, oseg_ref)
                    cross.append(lax.dot_general(qt.astype(BF16), nmat, _NT, preferred_element_type=F32))
                else:
                    cross.append(jnp.zeros((_SUB, _HK), F32))
            a = a + jnp.concatenate(cross, axis=0)
            a_ref[...] = a
            st_ref[0] = st[...]
            b = b_s[...]
            v = v_ref[...]
            e_col = ec_ref[0]
            qb = q_s[...] * jnp.exp(b)
            kd = k_s[...] * jnp.exp(b_ref[last:last + 1, :] - b)
            for h in range(GLA_HEADS):
                ks_ = slice(h * GLA_DK, (h + 1) * GLA_DK)
                vs_ = slice(h * GLA_DV, (h + 1) * GLA_DV)
                s_h = st[ks_, :]
                o_ref[:, vs_] = _dotb(qb[:, ks_], s_h, _NN) + _dotb(a[:, ks_], v[:, vs_], _NN)
                st[ks_, :] = s_h * e_col[ks_, :] + _dotb(kd[:, ks_], v[:, vs_], _TN)

    def specs(reverse):
        rb = (lambda e, c: e * n + (n - 1 - c)) if reverse else (lambda e, c: e * n + c)
        ins = [pl.BlockSpec((CHUNK, _HK), lambda e, c: (rb(e, c), C_Q // _HK)),
               pl.BlockSpec((CHUNK, _HK), lambda e, c: (rb(e, c), C_K // _HK)),
               pl.BlockSpec((CHUNK, _HV), lambda e, c: (rb(e, c), C_V // _HV)),
               pl.BlockSpec((CHUNK, _HK), lambda e, c: (rb(e, c), 0)),
               pl.BlockSpec((1, _HK, GLA_DV), lambda e, c: (rb(e, c), 0, 0))]
        outs = [pl.BlockSpec((CHUNK, _HV), lambda e, c: (rb(e, c), 0)),
                pl.BlockSpec((CHUNK, _HK), lambda e, c: (rb(e, c), 0)),
                pl.BlockSpec((1, _HK, GLA_DV), lambda e, c: (rb(e, c), 0, 0))]
        return ins, outs

    in_f, out_f = specs(False)
    in_b, out_b = specs(True)
    shapes = (jax.ShapeDtypeStruct((T, _HV), F32), jax.ShapeDtypeStruct((T, _HK), F32),
              jax.ShapeDtypeStruct((T // CHUNK, _HK, GLA_DV), F32))
    res = pl.pallas_call(
        body, out_shape=shapes + shapes, grid=(E, n),
        in_specs=in_f + in_b + [pl.BlockSpec((_HK, _HK), lambda e, c: (0, 0))], out_specs=tuple(out_f + out_b),
        scratch_shapes=([pltpu.VMEM((_HK, GLA_DV), F32)] + [pltpu.VMEM((CHUNK, _HK), F32)] * 3) * 2,
        name=name, compiler_params=_cp(("parallel", "arbitrary")))(z, z, z, b_f, ec_f, z, z, z, b_b, ec_b, oseg)
    return res[:3], res[3:]


def _gla_bwd(z, b_all, ecol, att, states, do, prev, S, reverse, name):
    T = z.shape[0]
    E = T // S
    n = S // CHUNK
    _, _, oseg = _gla_consts(reverse)
    has_prev = prev is not None
    odt = BF16 if has_prev else F32
    last = 0 if reverse else CHUNK - 1

    def body(*refs):
        (q_ref, k_ref, v_ref, b_ref, ec_ref, oseg_ref, att_ref, st_ref, do_ref) = refs[:9]
        refs = refs[9:]
        if has_prev:
            pq_ref, pk_ref, pv_ref = refs[:3]
            refs = refs[3:]
        (dq_ref, dk_ref, dv_ref, db_ref, dst, b_s, q_s, k_s, da_s, dqb_s, dkd_s, dk3_s, dbn_s, dsp_s) = refs

        @pl.when(pl.program_id(1) == 0)
        def _():
            dst[...] = jnp.zeros_like(dst)

        q = q_ref[...] * (GLA_DK ** -0.5)
        k = k_ref[...]
        v = v_ref[...]
        b = b_ref[...]
        bl_row = b_ref[last:last + 1, :]
        eb = jnp.exp(b)
        ekd = jnp.exp(bl_row - b)
        qb = q * eb
        kd = k * ekd
        b_s[...] = b
        q_s[...] = q
        k_s[...] = k
        att = att_ref[...]
        s_all = st_ref[0]
        dsn = dst[...]
        e_col = ec_ref[0]
        do = do_ref[...]
        lane = lax.broadcasted_iota(jnp.int32, (1, _HK), 1) % GLA_DK
        rowi = lax.broadcasted_iota(jnp.int32, (CHUNK, 1), 0)
        keep = (rowi <= lane) if reverse else (rowi >= lane)
        for h in range(GLA_HEADS):
            ks_ = slice(h * GLA_DK, (h + 1) * GLA_DK)
            vs_ = slice(h * GLA_DV, (h + 1) * GLA_DV)
            do_h = do[:, vs_]
            s_h = s_all[ks_, :]
            dsn_h = dsn[ks_, :]
            dqb_s[:, ks_] = _dotb(do_h, s_h, _NT)
            dsp_s[ks_, :] = _dotb(qb[:, ks_], do_h, _TN) + dsn_h * e_col[ks_, :]
            da_s[:, ks_] = _dotb(do_h, v[:, vs_], _NT)
            dv_h = _dotb(att[:, ks_], do_h, _TN) + _dotb(kd[:, ks_], dsn_h, _NN)
            if has_prev:
                dv_h = dv_h + pv_ref[:, vs_]
            dv_ref[:, vs_] = dv_h.astype(odt)
            dkd_s[:, ks_] = _dotb(v[:, vs_], dsn_h, _NT)
        da_s[...] = jnp.where(keep, da_s[...], 0.0)
        dqb = dqb_s[...]
        dkd = dkd_s[...]
        x = dsn * s_all * e_col
        dbl_row = _dotf(jnp.ones((8, GLA_DV), F32), x, _NT)[0:1, :] + jnp.sum(dkd * kd, axis=0, keepdims=True)

        blk0 = (rowi // _SUB) * _SUB

        def cols(jj, carry):
            dq3, db3 = list(carry[:_NSUB]), list(carry[_NSUB:])
            sel = [jnp.where(lane == blk0 + (jj * _COLS + u), da_s[...], 0.0).astype(BF16) for u in range(_COLS)]
            dcols = jnp.dot(jnp.concatenate(sel, axis=0), oseg_ref[...], preferred_element_type=F32)
            for u in range(_COLS):
                jp = jj * _COLS + u
                for s in range(_NSUB):
                    rs_ = slice(s * _SUB, (s + 1) * _SUB)
                    bj = b_s[pl.ds(s * _SUB + jp, 1), :]
                    kj = k_s[pl.ds(s * _SUB + jp, 1), :]
                    tm_ = dcols[u * CHUNK + s * _SUB:u * CHUNK + (s + 1) * _SUB, :] * jnp.exp(jnp.minimum(b_s[rs_, :] - bj, 0.0))
                    dq3[s] = dq3[s] + tm_ * kj
                    gq = tm_ * q_s[rs_, :]
                    dk3_s[pl.ds(s * _SUB + jp, 1), :] = jnp.sum(gq, axis=0, keepdims=True)
                    w = gq * kj
                    dbn_s[pl.ds(s * _SUB + jp, 1), :] = jnp.sum(w, axis=0, keepdims=True)
                    db3[s] = db3[s] + w
            return tuple(dq3) + tuple(db3)

        zero = jnp.zeros((_SUB, _HK), F32)
        acc = lax.fori_loop(0, _SUB // _COLS, cols, (zero,) * (2 * _NSUB))
        dq3 = jnp.concatenate(acc[:_NSUB], axis=0)
        db3 = jnp.concatenate(acc[_NSUB:], axis=0)
        head = lax.broadcasted_iota(jnp.int32, (1, _HK), 1) // GLA_DK
        dq_x, db_x = [], []
        dk_x = jnp.zeros((CHUNK, _HK), F32)
        db_k = jnp.zeros((CHUNK, _HK), F32)
        for s in range(_NSUB):
            if s not in _gla_cross_blocks(reverse):
                dq_x.append(zero)
                db_x.append(zero)
                continue
            r0 = s * _SUB
            qt, eq, kt, ek, nmat = _gla_cross_terms(s, reverse, b_s, q_s, k_s, oseg_ref)
            seen = (lane >= r0 + _SUB) if reverse else (lane < r0)
            dax = jnp.where(seen, da_s[r0:r0 + _SUB, :], 0.0).astype(BF16)
            dqt = jnp.dot(dax, nmat, preferred_element_type=F32)
            full = lax.dot_general(dax, qt.astype(BF16), _TN, preferred_element_type=F32)
            dkt = full[0:CHUNK, :]
            for h in range(1, GLA_HEADS):
                dkt = jnp.where(head == h, full[h * CHUNK:(h + 1) * CHUNK, :], dkt)
            dq_x.append(dqt * eq)
            db_x.append(dqt * qt)
            dk_x = dk_x + dkt * ek
            db_k = db_k + dkt * kt
        dq = (dqb * eb + dq3 + jnp.concatenate(dq_x, axis=0)) * (GLA_DK ** -0.5)
        dk = dkd * ekd + dk3_s[...] + dk_x
        db = dqb * qb - dkd * kd + db3 - dbn_s[...] + jnp.concatenate(db_x, axis=0) - db_k
        db_ref[...] = jnp.where(rowi == last, db + dbl_row, db)
        if has_prev:
            dq = dq + pq_ref[...]
            dk = dk + pk_ref[...]
        dq_ref[...] = dq.astype(odt)
        dk_ref[...] = dk.astype(odt)
        dst[...] = dsp_s[...]

    def rowblk(e, c):
        return e * n + (c if reverse else (n - 1 - c))

    hk = pl.BlockSpec((CHUNK, _HK), lambda e, c: (rowblk(e, c), 0))
    hv = pl.BlockSpec((CHUNK, _HV), lambda e, c: (rowblk(e, c), 0))
    stb = pl.BlockSpec((1, _HK, GLA_DV), lambda e, c: (rowblk(e, c), 0, 0))
    in_specs = [pl.BlockSpec((CHUNK, _HK), lambda e, c: (rowblk(e, c), C_Q // _HK)),
                pl.BlockSpec((CHUNK, _HK), lambda e, c: (rowblk(e, c), C_K // _HK)),
                pl.BlockSpec((CHUNK, _HV), lambda e, c: (rowblk(e, c), C_V // _HV)),
                hk, stb, pl.BlockSpec((_HK, _HK), lambda e, c: (0, 0)), hk, stb, hv]
    args = [z, z, z, b_all, ecol, oseg, att, states, do]
    if has_prev:
        in_specs += [hk, hk, hv]
        args += list(prev)
    return pl.pallas_call(
        body, out_shape=(jax.ShapeDtypeStruct((T, _HK), odt), jax.ShapeDtypeStruct((T, _HK), odt),
                         jax.ShapeDtypeStruct((T, _HV), odt), jax.ShapeDtypeStruct((T, _HK), F32)),
        grid=(E, n), in_specs=in_specs, out_specs=(hk, hk, hv, hk),
        scratch_shapes=[pltpu.VMEM((_HK, GLA_DV), F32)] + [pltpu.VMEM((CHUNK, _HK), F32)] * 8 + [pltpu.VMEM((_HK, GLA_DV), F32)],
        name=name, compiler_params=_cp(("parallel", "arbitrary")))(*args)


def _gla_bwd_both(z, fwd_saved, rev_saved, do, S, name):
    T = z.shape[0]
    E = T // S
    n = S // CHUNK
    _, _, oseg = _gla_consts(False)
    NI, NO, NS = 8, 4, 10

    def body(*refs):
        oseg_ref = refs[2 * NI]
        dirs = []
        for di, reverse in enumerate((False, True)):
            ins = refs[NI * di:NI * (di + 1)]
            outs = refs[2 * NI + 1 + NO * di:2 * NI + 1 + NO * (di + 1)]
            scr = refs[2 * NI + 1 + 2 * NO + NS * di:2 * NI + 1 + 2 * NO + NS * (di + 1)]
            dirs.append((reverse, ins, outs, scr))

        @pl.when(pl.program_id(1) == 0)
        def _():
            for d in dirs:
                d[3][0][...] = jnp.zeros_like(d[3][0])

        lane = lax.broadcasted_iota(jnp.int32, (1, _HK), 1) % GLA_DK
        head = lax.broadcasted_iota(jnp.int32, (1, _HK), 1) // GLA_DK
        rowi = lax.broadcasted_iota(jnp.int32, (CHUNK, 1), 0)
        blk0 = (rowi // _SUB) * _SUB

        def factors(reverse, b_ref, b_s, q_s, k_s):
            last = 0 if reverse else CHUNK - 1
            b = b_s[...]
            eb = jnp.exp(b)
            ekd = jnp.exp(b_ref[last:last + 1, :] - b)
            return eb, ekd, q_s[...] * eb, k_s[...] * ekd

        dbl_rows = []
        for reverse, (q_ref, k_ref, v_ref, b_ref, ec_ref, att_ref, st_ref, do_ref), (dq_ref, dk_ref, dv_ref, db_ref), \
                (dst, b_s, q_s, k_s, da_s, dqb_s, dkd_s, dk3_s, dbn_s, dsp_s) in dirs:
            b_s[...] = b_ref[...]
            q_s[...] = q_ref[...] * (GLA_DK ** -0.5)
            k_s[...] = k_ref[...]
            _, _, qb, kd = factors(reverse, b_ref, b_s, q_s, k_s)
            v = v_ref[...]
            att = att_ref[...]
            s_all = st_ref[0]
            dsn = dst[...]
            e_col = ec_ref[0]
            do = do_ref[...]
            keep = (rowi <= lane) if reverse else (rowi >= lane)
            for h in range(GLA_HEADS):
                ks_ = slice(h * GLA_DK, (h + 1) * GLA_DK)
                vs_ = slice(h * GLA_DV, (h + 1) * GLA_DV)
                do_h = do[:, vs_]
                s_h = s_all[ks_, :]
                dsn_h = dsn[ks_, :]
                dqb_s[:, ks_] = _dotb(do_h, s_h, _NT)
                dsp_s[ks_, :] = _dotb(qb[:, ks_], do_h, _TN) + dsn_h * e_col[ks_, :]
                da_s[:, ks_] = _dotb(do_h, v[:, vs_], _NT)
                dv_ref[:, vs_] = _dotb(att[:, ks_], do_h, _TN) + _dotb(kd[:, ks_], dsn_h, _NN)
                dkd_s[:, ks_] = _dotb(v[:, vs_], dsn_h, _NT)
            da_s[...] = jnp.where(keep, da_s[...], 0.0)
            x = dsn * s_all * e_col
            dbl_rows.append(_dotf(jnp.ones((8, GLA_DV), F32), x, _NT)[0:1, :] + jnp.sum(dkd_s[...] * kd, axis=0, keepdims=True))

        def cols(d, jj, carry):
            _, b_s, q_s, k_s, da_s, _, _, dk3_s, dbn_s, _ = d[3]
            dq3, db3 = list(carry[:_NSUB]), list(carry[_NSUB:])
            sel = [jnp.where(lane == blk0 + (jj * _COLS + u), da_s[...], 0.0).astype(BF16) for u in range(_COLS)]
            dcols = jnp.dot(jnp.concatenate(sel, axis=0), oseg_ref[...], preferred_element_type=F32)
            for u in range(_COLS):
                jp = jj * _COLS + u
                for s in range(_NSUB):
                    rs_ = slice(s * _SUB, (s + 1) * _SUB)
                    bj = b_s[pl.ds(s * _SUB + jp, 1), :]
                    kj = k_s[pl.ds(s * _SUB + jp, 1), :]
                    tm_ = dcols[u * CHUNK + s * _SUB:u * CHUNK + (s + 1) * _SUB, :] * jnp.exp(jnp.minimum(b_s[rs_, :] - bj, 0.0))
                    dq3[s] = dq3[s] + tm_ * kj
                    gq = tm_ * q_s[rs_, :]
                    dk3_s[pl.ds(s * _SUB + jp, 1), :] = jnp.sum(gq, axis=0, keepdims=True)
                    w = gq * kj
                    dbn_s[pl.ds(s * _SUB + jp, 1), :] = jnp.sum(w, axis=0, keepdims=True)
                    db3[s] = db3[s] + w
            return tuple(dq3) + tuple(db3)

        zero = jnp.zeros((_SUB, _HK), F32)
        init = (zero,) * (2 * _NSUB)
        accs = lax.fori_loop(0, _SUB // _COLS, lambda jj, c: tuple(cols(d, jj, a) for d, a in zip(dirs, c)), (init, init),
                             unroll=True)

        for (reverse, ins, (dq_ref, dk_ref, dv_ref, db_ref), (dst, b_s, q_s, k_s, da_s, dqb_s, dkd_s, dk3_s, dbn_s, dsp_s)), \
                acc, dbl_row in zip(dirs, accs, dbl_rows):
            last = 0 if reverse else CHUNK - 1
            eb, ekd, qb, kd = factors(reverse, ins[3], b_s, q_s, k_s)
            dqb = dqb_s[...]
            dkd = dkd_s[...]
            dq3 = jnp.concatenate(acc[:_NSUB], axis=0)
            db3 = jnp.concatenate(acc[_NSUB:], axis=0)
            dq_x, db_x = [], []
            dk_x = jnp.zeros((CHUNK, _HK), F32)
            db_k = jnp.zeros((CHUNK, _HK), F32)
            for s in range(_NSUB):
                if s not in _gla_cross_blocks(reverse):
                    dq_x.append(zero)
                    db_x.append(zero)
                    continue
                r0 = s * _SUB
                qt, eq, kt, ek, nmat = _gla_cross_terms(s, reverse, b_s, q_s, k_s, oseg_ref)
                seen = (lane >= r0 + _SUB) if reverse else (lane < r0)
                dax = jnp.where(seen, da_s[r0:r0 + _SUB, :], 0.0).astype(BF16)
                dqt = jnp.dot(dax, nmat, preferred_element_type=F32)
                full = lax.dot_general(dax, qt.astype(BF16), _TN, preferred_element_type=F32)
                dkt = full[0:CHUNK, :]
                for h in range(1, GLA_HEADS):
                    dkt = jnp.where(head == h, full[h * CHUNK:(h + 1) * CHUNK, :], dkt)
                dq_x.append(dqt * eq)
                db_x.append(dqt * qt)
                dk_x = dk_x + dkt * ek
                db_k = db_k + dkt * kt
            dq_ref[...] = (dqb * eb + dq3 + jnp.concatenate(dq_x, axis=0)) * (GLA_DK ** -0.5)
            dk_ref[...] = dkd * ekd + dk3_s[...] + dk_x
            db = dqb * qb - dkd * kd + db3 - dbn_s[...] + jnp.concatenate(db_x, axis=0) - db_k
            db_ref[...] = jnp.where(rowi == last, db + dbl_row, db)
            dst[...] = dsp_s[...]

    def specs(reverse):
        rb = (lambda e, c: e * n + c) if reverse else (lambda e, c: e * n + (n - 1 - c))
        hk = pl.BlockSpec((CHUNK, _HK), lambda e, c: (rb(e, c), 0))
        hv = pl.BlockSpec((CHUNK, _HV), lambda e, c: (rb(e, c), 0))
        stb = pl.BlockSpec((1, _HK, GLA_DV), lambda e, c: (rb(e, c), 0, 0))
        ins = [pl.BlockSpec((CHUNK, _HK), lambda e, c: (rb(e, c), C_Q // _HK)),
               pl.BlockSpec((CHUNK, _HK), lambda e, c: (rb(e, c), C_K // _HK)),
               pl.BlockSpec((CHUNK, _HV), lambda e, c: (rb(e, c), C_V // _HV)),
               hk, stb, hk, stb, hv]
        return ins, [hk, hk, hv, hk]

    in_f, out_f = specs(False)
    in_b, out_b = specs(True)
    shapes = (jax.ShapeDtypeStruct((T, _HK), F32), jax.ShapeDtypeStruct((T, _HK), F32),
              jax.ShapeDtypeStruct((T, _HV), F32), jax.ShapeDtypeStruct((T, _HK), F32))
    scratch = [pltpu.VMEM((_HK, GLA_DV), F32)] + [pltpu.VMEM((CHUNK, _HK), F32)] * 8 + [pltpu.VMEM((_HK, GLA_DV), F32)]
    res = pl.pallas_call(
        body, out_shape=shapes + shapes, grid=(E, n),
        in_specs=in_f + in_b + [pl.BlockSpec((_HK, _HK), lambda e, c: (0, 0))], out_specs=tuple(out_f + out_b),
        scratch_shapes=scratch * 2, name=name, compiler_params=_cp(("parallel", "arbitrary")))(
            z, z, z, *fwd_saved, do, z, z, z, *rev_saved, do, oseg)
    return res[:4], res[4:]


def _gla_norm_fwd(of, ob, og, name):
    T = of.shape[0]
    tm = 256

    def body(f_ref, b_ref, g_ref, o_ref):
        for h in range(GLA_HEADS):
            vs_ = slice(h * GLA_DV, (h + 1) * GLA_DV)
            o = f_ref[:, vs_] + b_ref[:, vs_]
            o_ref[:, vs_] = o * lax.rsqrt(jnp.mean(o * o, axis=-1, keepdims=True) + EPS) * g_ref[:, vs_]

    row = pl.BlockSpec((tm, _HV), lambda i: (i, 0))
    vec = pl.BlockSpec((1, _HV), lambda i: (0, 0))
    return pl.pallas_call(body, out_shape=jax.ShapeDtypeStruct((T, _HV), F32), grid=(T // tm,),
                          in_specs=[row, row, vec], out_specs=row, name=name, compiler_params=_cp(("parallel",)))(of, ob, og)


def _gla_norm_bwd(of, ob, og, dpre, name):
    T = of.shape[0]
    tm = 256

    def body(f_ref, b_ref, g_ref, dp_ref, do_ref, dg_ref):
        @pl.when(pl.program_id(0) == 0)
        def _():
            dg_ref[...] = jnp.zeros_like(dg_ref)

        for h in range(GLA_HEADS):
            vs_ = slice(h * GLA_DV, (h + 1) * GLA_DV)
            o = f_ref[:, vs_] + b_ref[:, vs_]
            r = lax.rsqrt(jnp.mean(o * o, axis=-1, keepdims=True) + EPS)
            xh = o * r
            dp = dp_ref[:, vs_]
            dxh = dp * g_ref[:, vs_]
            do_ref[:, vs_] = r * (dxh - xh * jnp.mean(dxh * xh, axis=-1, keepdims=True))
            dg_ref[:, vs_] += jnp.sum(dp * xh, axis=0, keepdims=True)

    row = pl.BlockSpec((tm, _HV), lambda i: (i, 0))
    vec = pl.BlockSpec((1, _HV), lambda i: (0, 0))
    return pl.pallas_call(
        body, out_shape=(jax.ShapeDtypeStruct((T, _HV), F32), jax.ShapeDtypeStruct((1, _HV), F32)), grid=(T // tm,),
        in_specs=[row, row, vec, row], out_specs=(row, vec), name=name, compiler_params=_cp(("arbitrary",)))(of, ob, og, dpre)


_ANY = pl.BlockSpec(memory_space=pl.ANY)


def _coords():
    return lax.axis_index("x"), lax.axis_index("y"), lax.axis_index("c")


def _other_chips(x, y):
    return ((1 - x, y), (x, 1 - y), (1 - x, 1 - y))


def _gather_weights(arrays, chunks, name):
    n = len(arrays)
    pieces = []
    for k in range(max(chunks)):
        for i, a in enumerate(arrays):
            if k < chunks[i]:
                rc = a.shape[1] // chunks[i]
                pieces.append((i, k * rc, rc))
    m = len(pieces)

    def body(*refs):
        srcs, dsts = refs[:n], refs[n:2 * n]
        send_sems, recv_sems, local_sems = refs[2 * n:]
        x, y, c = _coords()
        me = 2 * x + y
        loc = [pltpu.make_async_copy(s, d.at[me], local_sems.at[i]) for i, (s, d) in enumerate(zip(srcs, dsts))]
        for cp in loc:
            cp.start()
        ici = []
        for p, (i, r0, rc) in enumerate(pieces):
            for j, (px, py) in enumerate(_other_chips(x, y)):
                ici.append(pltpu.make_async_remote_copy(
                    src_ref=srcs[i].at[c, pl.ds(r0, rc)], dst_ref=dsts[i].at[me, c, pl.ds(r0, rc)],
                    send_sem=send_sems.at[3 * p + j], recv_sem=recv_sems.at[3 * p + j],
                    device_id=(px, py, c), device_id_type=MESH))
        for cp in ici:
            cp.start()
        fwd = []
        for p, (i, r0, rc) in enumerate(pieces):
            for j, (px, py) in enumerate(_other_chips(x, y)):
                ici[3 * p + j].wait_recv()
                part = dsts[i].at[2 * px + py, c, pl.ds(r0, rc)]
                cp = pltpu.make_async_remote_copy(
                    src_ref=part, dst_ref=part, send_sem=send_sems.at[3 * m + 3 * p + j], recv_sem=recv_sems.at[3 * m + 3 * p + j],
                    device_id=(x, y, 1 - c), device_id_type=MESH)
                cp.start()
                fwd.append(cp)
        for cp in fwd:
            cp.wait_recv()
        for cp in ici + fwd:
            cp.wait_send()
        for cp in loc:
            cp.wait()

    return pl.pallas_call(
        body, out_shape=tuple(jax.ShapeDtypeStruct((4,) + a.shape, a.dtype) for a in arrays),
        in_specs=[_ANY] * n, out_specs=(_ANY,) * n,
        scratch_shapes=[pltpu.SemaphoreType.DMA((6 * m,)), pltpu.SemaphoreType.DMA((6 * m,)), pltpu.SemaphoreType.DMA((n,))],
        name=name)(*arrays)


def _sibling_exchange(layered, whole, name):
    nl, n = len(layered), len(layered) + len(whole)

    def body(*refs):
        srcs, dsts = refs[:n], refs[n:2 * n]
        send_sems, recv_sems = refs[2 * n:]
        x, y, c = _coords()
        rem = [pltpu.make_async_remote_copy(src_ref=(s.at[1 - c] if i < nl else s), dst_ref=d, send_sem=send_sems.at[i],
                                            recv_sem=recv_sems.at[i], device_id=(x, y, 1 - c), device_id_type=MESH)
               for i, (s, d) in enumerate(zip(srcs, dsts))]
        for cp in rem:
            cp.start()
        for cp in rem:
            cp.wait()

    outs = [jax.ShapeDtypeStruct(a.shape[1:], a.dtype) for a in layered] + [jax.ShapeDtypeStruct(a.shape, a.dtype) for a in whole]
    return pl.pallas_call(
        body, out_shape=tuple(outs), in_specs=[_ANY] * n, out_specs=(_ANY,) * n,
        scratch_shapes=[pltpu.SemaphoreType.DMA((n,)), pltpu.SemaphoreType.DMA((n,))], name=name)(*layered, *whole)


def _chip_exchange(scatter, bcast, name):
    ns, n = len(scatter), len(scatter) + len(bcast)

    def body(*refs):
        srcs, dsts = refs[:n], refs[n:2 * n]
        send_sems, recv_sems, local_sems = refs[2 * n:]
        x, y, c = _coords()
        me = 2 * x + y
        loc = [pltpu.make_async_copy((s.at[me] if i < ns else s), d.at[me], local_sems.at[i])
               for i, (s, d) in enumerate(zip(srcs, dsts))]
        for cp in loc:
            cp.start()
        rem = []
        for j, (px, py) in enumerate(_other_chips(x, y)):
            for i, (s, d) in enumerate(zip(srcs, dsts)):
                rem.append(pltpu.make_async_remote_copy(
                    src_ref=(s.at[2 * px + py] if i < ns else s), dst_ref=d.at[me], send_sem=send_sems.at[n * j + i],
                    recv_sem=recv_sems.at[n * j + i], device_id=(px, py, c), device_id_type=MESH))
        for cp in rem:
            cp.start()
        for cp in rem:
            cp.wait()
        for cp in loc:
            cp.wait()

    outs = [jax.ShapeDtypeStruct(a.shape, a.dtype) for a in scatter] + [jax.ShapeDtypeStruct((4,) + a.shape, a.dtype) for a in bcast]
    return pl.pallas_call(
        body, out_shape=tuple(outs), in_specs=[_ANY] * n, out_specs=(_ANY,) * n,
        scratch_shapes=[pltpu.SemaphoreType.DMA((3 * n,)), pltpu.SemaphoreType.DMA((3 * n,)), pltpu.SemaphoreType.DMA((n,))],
        name=name)(*scatter, *bcast)


_EW_BLOCK_BYTES = 2 * 1024 * 1024


def _tile2d(R, C):
    if R % 256 == 0 and 256 * C * 4 <= _EW_BLOCK_BYTES:
        return 256, C
    bc = 256 if C % 256 == 0 else C
    for br in range(R, 0, -1):
        if R % br == 0 and (br % 8 == 0 or br == R) and br * bc * 4 <= _EW_BLOCK_BYTES:
            return br, bc
    return R, bc


def _sum_slots(r, name):
    n, R, C = r.shape
    br, bc = _tile2d(R, C)

    def body(r_ref, o_ref):
        acc = r_ref[0].astype(F32)
        for i in range(1, n):
            acc = acc + r_ref[i].astype(F32)
        o_ref[...] = acc

    return pl.pallas_call(body, out_shape=jax.ShapeDtypeStruct((R, C), F32), grid=(R // br, C // bc),
                          in_specs=[pl.BlockSpec((n, br, bc), lambda i, j: (0, i, j))],
                          out_specs=pl.BlockSpec((br, bc), lambda i, j: (i, j)),
                          name=name, compiler_params=_cp(("parallel", "parallel")))(r)


_SMEM = pl.BlockSpec(memory_space=pltpu.SMEM)


def _add2(a, b, out_dtype, name, pick=None):
    R, C = b.shape
    br, bc = _tile2d(R, C)
    blk = pl.BlockSpec((br, bc), lambda i, j: (i, j))
    if pick is None:
        def body(a_ref, b_ref, o_ref):
            o_ref[...] = (a_ref[...].astype(F32) + b_ref[...].astype(F32)).astype(out_dtype)
        in_specs, args = [blk, blk], (a, b)
    else:
        def body(c_ref, a_ref, b_ref, o_ref):
            av = jnp.where(c_ref[0] == 0, a_ref[0], a_ref[1])
            o_ref[...] = (av.astype(F32) + b_ref[...].astype(F32)).astype(out_dtype)
        in_specs = [_SMEM, pl.BlockSpec((2, br, bc), lambda i, j: (0, i, j)), blk]
        args = (pick.reshape(1).astype(jnp.int32), a, b)
    return pl.pallas_call(body, out_shape=jax.ShapeDtypeStruct((R, C), out_dtype), grid=(R // br, C // bc), in_specs=in_specs,
                          out_specs=blk, name=name, compiler_params=_cp(("parallel", "parallel")))(*args)


def _adamw_math(w, g, m, v):
    m = ADAM_B1 * m + (1.0 - ADAM_B1) * g
    v = ADAM_B2 * v + (1.0 - ADAM_B2) * (g * g)
    m_hat = m / (1.0 - ADAM_B1 ** ADAM_STEP)
    v_hat = v / (1.0 - ADAM_B2 ** ADAM_STEP)
    delta = -ADAM_LR * (m_hat / (jnp.sqrt(v_hat) + ADAM_EPS) + ADAM_WD * w)
    return delta, m, v


def _adamw(w, gs, m, v, name, pick=None):
    R, C = w.shape
    br, bc = _tile2d(R, C)
    blk = pl.BlockSpec((br, bc), lambda i, j: (i, j))
    if pick is None:
        g_specs = [pl.BlockSpec((g.shape[0], br, bc), lambda i, j: (0, i, j)) if g.ndim == 3 else blk for g in gs]
        lead = ()
    else:
        nb = (R // 2) // br
        assert nb * br * 2 == R
        g_specs = [pl.BlockSpec((br, bc), lambda i, j: (i % nb, j))] * 2
        lead = (pick.reshape(1).astype(jnp.int32),)

    def body(*refs):
        if pick is not None:
            c_ref, refs = refs[0], refs[1:]
        w_ref = refs[0]
        g_refs = refs[1:1 + len(gs)]
        m_ref, v_ref, g_out, d_out, m_out, v_out = refs[1 + len(gs):]
        if pick is None:
            g = None
            for gr in g_refs:
                parts = [gr[i] for i in range(gr.shape[0])] if len(gr.shape) == 3 else [gr[...]]
                for p in parts:
                    g = p if g is None else g + p
        else:
            g = jnp.where(pl.program_id(0) // nb == c_ref[0], g_refs[0][...], g_refs[1][...])
        d, mn, vn = _adamw_math(w_ref[...], g, m_ref[...], v_ref[...])
        g_out[...] = g
        d_out[...] = d
        m_out[...] = mn
        v_out[...] = vn

    return pl.pallas_call(
        body, out_shape=tuple(jax.ShapeDtypeStruct((R, C), F32) for _ in range(4)), grid=(R // br, C // bc),
        in_specs=[_SMEM] * len(lead) + [blk] + g_specs + [blk, blk], out_specs=(blk,) * 4, name=name,
        compiler_params=_cp(("parallel", "parallel")))(*lead, w, *gs, m, v)


WEIGHTS = ("norm_g", "w_in", "conv_w", "conv_b", "conv_ln_g", "conv_ln_b", "na_q_g", "na_k_g", "na_rpb", "gla_a2_f",
           "gla_ab_f", "gla_a2_b", "gla_ab_b", "gla_o_g", "pool_w", "pool_scale", "w_out")
_REPL = ("norm_g", "conv_b", "conv_ln_g", "conv_ln_b", "na_q_g", "na_k_g", "na_rpb", "gla_ab_f", "gla_ab_b", "gla_o_g",
         "pool_w", "pool_scale")
_SHARD_SMALL = ("conv_w", "gla_a2_f", "gla_a2_b")
_PACK_ROWS = 8 * 128


def _pack(arrs):
    flat = jnp.concatenate([a.reshape(-1) for a in arrs])
    n = -(-flat.shape[0] // _PACK_ROWS) * _PACK_ROWS
    return jnp.pad(flat, (0, n - flat.shape[0])).reshape(-1, 128)


def _unpack(p, shapes):
    flat = p.reshape(-1)
    out, o = [], 0
    for s in shapes:
        n = int(np.prod(s))
        out.append(flat[o:o + n].reshape(s))
        o += n
    return out


def _to_layout_rows(w):
    pad = jnp.zeros((w.shape[0], NZ - N_IN, w.shape[2]), w.dtype)
    return jnp.concatenate([w[:, :5120], w[:, 5152:6176], w[:, 5120:5152], pad], axis=1)


def _from_layout_rows(w):
    return jnp.concatenate([w[:, :5120], w[:, LR_OFF:LR_OFF + 32], w[:, 5120:LR_OFF]], axis=1)


def _reduce_gradients(p_a, p_b, small_g, ci):
    s_a, s_b, s_small = _sibling_exchange((p_a, p_b), (small_g,), "grad_to_sibling")
    flat = lambda a: a.reshape(a.shape[0], -1, a.shape[-1])
    c_a = _add2(flat(p_a), s_a.reshape(-1, s_a.shape[-1]), BF16, "chip_sum_a", pick=ci).reshape(s_a.shape)
    c_b = _add2(flat(p_b), s_b.reshape(-1, s_b.shape[-1]), BF16, "chip_sum_b", pick=ci).reshape(s_b.shape)
    c_small = _add2(small_g, s_small, F32, "chip_sum_small")
    r_a, r_b, r_small = _chip_exchange((c_a, c_b), (c_small,), "grad_to_owner")
    own_a = _sum_slots(r_a, "sum_a")
    own_b = _sum_slots(r_b, "sum_b")
    sib_a, sib_b = _sibling_exchange((), (own_a, own_b), "reduced_to_sibling")
    return (own_a, sib_a), (own_b, sib_b), r_small


def _layer_fwd(l, x, P, S, target=None):
    n = f"l{l}_"
    h = _rmsnorm_fwd(x, P["norm_g"], n + "rms_fwd")
    z = _matmul(h, P["w_in"], dims="nn", out_dtype=F32, tm=1024, tn=1280, tk=D_MODEL, name=n + "mm_z")
    yc = _conv_fwd(z, P["conv_w32"], P["conv_b"], S, n + "conv_fwd")
    pre_a = _ln_silu_fwd(yc, P["conv_ln_g"], P["conv_ln_b"], n + "ln_fwd")
    pre_b = _na_fwd(z, P["na_q_g"], P["na_k_g"], P["na_bias"], S, n + "na_fwd")
    bf, ecf = _gla_decay_fwd(z, P["a2_f"], P["gla_ab_f"], False, n + "gla_decay_f")
    bb, ecb = _gla_decay_fwd(z, P["a2_b"], P["gla_ab_b"], True, n + "gla_decay_b")
    (of, af, sf), (ob, ab, sb) = _gla_fwd_both(z, bf, ecf, bb, ecb, S, n + "gla_fwd")
    pre_c = _gla_norm_fwd(of, ob, P["gla_o_g"], n + "gla_norm_fwd")
    pre_d = _pool_fwd(z, P["pool_w_bf"], P["pool_scale"], S, n + "pool_fwd")
    pres = (pre_a, pre_b, pre_c, pre_d)
    res = _out_proj_fwd(pres, z, P["w_out"], x, target, n + "out_proj")
    y, out = res[0], (res[1] if target is None else res[1:])
    return out, dict(x=x, h=h, z=z, yc=yc, pres=pres, of=of, af=af, sf=sf, ob=ob, ab=ab, sb=sb, y=y, bf=bf, ecf=ecf, bb=bb, ecb=ecb)


def _layer_bwd(l, dout, dout_bf, sv, P, S):
    n = f"l{l}_"
    z = sv["z"]
    T = z.shape[0]
    d_w_out = _matmul(sv["y"], dout_bf, dims="tn", out_dtype=BF16, tm=1024, tn=2048, tk=512, name=n + "mm_dwout")
    dpa, dpb, dpc, dpd, dga, dgb, dgc, dgd = _out_proj_bwd(dout_bf, P["w_out_t"], sv["pres"], z, n + "out_proj_bwd")
    dyc, d_ln_g, d_ln_b = _ln_silu_bwd(sv["yc"], P["conv_ln_g"], P["conv_ln_b"], dpa, n + "ln_bwd")
    dval, dglu, d_cw, d_cb = _conv_bwd(z, P["conv_w32"], dyc, S, n + "conv_bwd")
    dq, dk, dv, dbias, d_qg, d_kg = _na_bwd(z, P["na_q_g"], P["na_k_g"], P["na_bias"], dpb, S, n + "na_bwd")
    d_rpb = _na_rpb_grad(dbias, n + "na_rpb")
    do, d_og = _gla_norm_bwd(sv["of"], sv["ob"], P["gla_o_g"], dpc, n + "gla_norm_bwd")
    gf, gb_ = _gla_bwd_both(z, (sv["bf"], sv["ecf"], sv["af"], sv["sf"]), (sv["bb"], sv["ecb"], sv["ab"], sv["sb"]), do, S,
                            n + "gla_bwd")
    dcq, dck, dcv = zip(gf[:3], gb_[:3])
    dlr, d_a2f, d_abf, d_a2b, d_abb = _gla_decay_bwd(z, P["a2_f"], P["gla_ab_f"], P["a2_b"], P["gla_ab_b"], gf[3], gb_[3],
                                                     n + "gla_decay_bwd")
    dd, d_pw, d_ps = _pool_bwd(z, P["pool_w_bf"], P["pool_scale"], dpd, S, n + "pool_bwd")
    dz = _concat_cols([dval, dglu, dga, dq, dk, dv, dgb, dcq, dck, dcv, dgc, dd, dgd, dlr], NZ, n + "dz_concat")
    dh = _matmul(dz, P["w_in_t"], dims="nn", out_dtype=F32, tm=1024, tn=1024, tk=3200, name=n + "mm_dh")
    d_w_in = _matmul(dz, sv["h"], dims="tn", out_dtype=BF16, tm=1280, tn=1024, tk=1024, name=n + "mm_dwin")
    dx, dx_bf, d_ng = _rmsnorm_bwd(sv["x"], P["norm_g"], dh, dout, n + "rms_bwd")
    grads = dict(norm_g=d_ng[0], w_in=d_w_in, conv_w=d_cw[:CONV_K], conv_b=d_cb[0], conv_ln_g=d_ln_g[0], conv_ln_b=d_ln_b[0],
                 na_q_g=d_qg.reshape(NA_HEADS, NA_DH), na_k_g=d_kg.reshape(NA_HEADS, NA_DH), na_rpb=d_rpb,
                 gla_a2_f=d_a2f[0:GLA_RANK], gla_ab_f=d_abf[0], gla_a2_b=d_a2b[GLA_RANK:2 * GLA_RANK], gla_ab_b=d_abb[0],
                 gla_o_g=d_og.reshape(GLA_HEADS, GLA_DV), pool_w=d_pw, pool_scale=d_ps[0], w_out=d_w_out)
    return dx, dx_bf, grads


def kernel(x, norm_g, w_in, conv_w, conv_b, conv_ln_g, conv_ln_b, na_q_g, na_k_g, na_rpb, gla_a2_f, gla_ab_f, gla_a2_b, gla_ab_b, gla_o_g, pool_w, pool_scale, w_out, loss_target, m_norm_g, m_w_in, m_conv_w, m_conv_b, m_conv_ln_g, m_conv_ln_b, m_na_q_g, m_na_k_g, m_na_rpb, m_gla_a2_f, m_gla_ab_f, m_gla_a2_b, m_gla_ab_b, m_gla_o_g, m_pool_w, m_pool_scale, m_w_out, v_norm_g, v_w_in, v_conv_w, v_conv_b, v_conv_ln_g, v_conv_ln_b, v_na_q_g, v_na_k_g, v_na_rpb, v_gla_a2_f, v_gla_ab_f, v_gla_a2_b, v_gla_ab_b, v_gla_o_g, v_pool_w, v_pool_scale, v_w_out):
    W = dict(norm_g=norm_g, w_in=w_in, conv_w=conv_w, conv_b=conv_b, conv_ln_g=conv_ln_g, conv_ln_b=conv_ln_b, na_q_g=na_q_g,
             na_k_g=na_k_g, na_rpb=na_rpb, gla_a2_f=gla_a2_f, gla_ab_f=gla_ab_f, gla_a2_b=gla_a2_b, gla_ab_b=gla_ab_b,
             gla_o_g=gla_o_g, pool_w=pool_w, pool_scale=pool_scale, w_out=w_out)
    M = dict(norm_g=m_norm_g, w_in=m_w_in, conv_w=m_conv_w, conv_b=m_conv_b, conv_ln_g=m_conv_ln_g, conv_ln_b=m_conv_ln_b,
             na_q_g=m_na_q_g, na_k_g=m_na_k_g, na_rpb=m_na_rpb, gla_a2_f=m_gla_a2_f, gla_ab_f=m_gla_ab_f, gla_a2_b=m_gla_a2_b,
             gla_ab_b=m_gla_ab_b, gla_o_g=m_gla_o_g, pool_w=m_pool_w, pool_scale=m_pool_scale, w_out=m_w_out)
    V = dict(norm_g=v_norm_g, w_in=v_w_in, conv_w=v_conv_w, conv_b=v_conv_b, conv_ln_g=v_conv_ln_g, conv_ln_b=v_conv_ln_b,
             na_q_g=v_na_q_g, na_k_g=v_na_k_g, na_rpb=v_na_rpb, gla_a2_f=v_gla_a2_f, gla_ab_f=v_gla_ab_f, gla_a2_b=v_gla_a2_b,
             gla_ab_b=v_gla_ab_b, gla_o_g=v_gla_o_g, pool_w=v_pool_w, pool_scale=v_pool_scale, w_out=v_w_out)
    E, S, D = x.shape
    T = E * S
    L = DEPTH
    xi, yi, ci = _coords()
    chip = 2 * xi + yi
    cw_sh, a2_sh = conv_w.shape[-1], gla_a2_f.shape[-1]

    small_sh = jnp.concatenate([
        jnp.pad(conv_w, ((0, 0), (0, 1), (0, 0))),
        jnp.pad(gla_a2_f, ((0, 0), (0, 0), (0, 128 - a2_sh))),
        jnp.pad(gla_a2_b, ((0, 0), (0, 0), (0, 128 - a2_sh)))], axis=1)
    w_in_tr, m_w_in_tr, v_w_in_tr = (jnp.transpose(a, (0, 2, 1)) for a in (w_in, m_w_in, v_w_in))
    g_win, g_wout, g_small = _gather_weights((w_in_tr.astype(BF16), w_out.astype(BF16), small_sh), (1, 2, 1), "gather_weights")
    w_in_t_full = _to_layout_rows(jnp.transpose(g_win, (1, 0, 2, 3)).reshape(L, N_IN, D))
    w_in_full = jnp.transpose(w_in_t_full, (0, 2, 1))
    w_out_full = jnp.transpose(g_wout, (1, 0, 2, 3)).reshape(L, D, D)
    conv_w_full = jnp.transpose(g_small[:, :, 0:32, :], (1, 2, 0, 3)).reshape(L, 32, 4 * cw_sh)
    a2f_full = jnp.transpose(g_small[:, :, 32:48, :a2_sh], (1, 2, 0, 3)).reshape(L, GLA_RANK, 4 * a2_sh)
    a2b_full = jnp.transpose(g_small[:, :, 48:64, :a2_sh], (1, 2, 0, 3)).reshape(L, GLA_RANK, 4 * a2_sh)

    params = []
    for l in range(L):
        params.append(dict(
            norm_g=norm_g[l][None], w_in=w_in_full[l], w_out=w_out_full[l], w_in_t=w_in_t_full[l], w_out_t=w_out_full[l].T,
            conv_w32=conv_w_full[l], conv_b=conv_b[l][None],
            conv_ln_g=conv_ln_g[l][None], conv_ln_b=conv_ln_b[l][None], na_q_g=na_q_g[l].reshape(1, GROUP_W),
            na_k_g=na_k_g[l].reshape(1, GROUP_W), na_bias=_na_bias(na_rpb[l], f"l{l}_na_bias"),
            a2_f=jnp.zeros((128, _HK), F32).at[0:GLA_RANK].set(a2f_full[l]),
            a2_b=jnp.zeros((128, _HK), F32).at[GLA_RANK:2 * GLA_RANK].set(a2b_full[l]),
            gla_ab_f=gla_ab_f[l][None], gla_ab_b=gla_ab_b[l][None], gla_o_g=gla_o_g[l].reshape(1, GROUP_W),
            pool_w_bf=pool_w[l].astype(BF16), pool_scale=pool_scale[l][None]))

    act = x.reshape(T, D)
    saved = []
    for l in range(L):
        act, sv = _layer_fwd(l, act, params[l], S, loss_target.reshape(T, D) if l == L - 1 else None)
        saved.append(sv)
    dact, dact_bf, loss_loc = act
    loss = lax.psum(loss_loc[0, 0], ("x", "y", "c"))
    grads = [None] * L
    for l in reversed(range(L)):
        dact, dact_bf, grads[l] = _layer_bwd(l, dact, dact_bf, saved[l], params[l], S)
    grad_x = dact.reshape(E, S, D)
    G = {k: jnp.stack([grads[l][k] for l in range(L)]) for k in WEIGHTS}

    cols_in, cols_out = N_IN // 4, D
    p_win = _from_layout_rows(G["w_in"]).reshape(L, 4, cols_in, D)
    p_wout = G["w_out"].reshape(L, 4, D // 4, D)
    small_names = _REPL + _SHARD_SMALL
    small_g = _pack([G[k] for k in small_names])
    g_in, g_out, r_small = _reduce_gradients(p_win, p_wout, small_g, ci)

    rows_in, rows_out = L * cols_in, L * (D // 4)
    res = {}
    res["w_in"] = [jnp.transpose(a.reshape(L, cols_in, D), (0, 2, 1)) for a in _adamw(
        w_in_tr.reshape(rows_in, D), g_in, m_w_in_tr.reshape(rows_in, D), v_w_in_tr.reshape(rows_in, D), "adamw_w_in", pick=ci)]
    res["w_out"] = [a.reshape(L, D // 4, D) for a in _adamw(
        w_out.reshape(rows_out, cols_out), g_out, m_w_out.reshape(rows_out, cols_out),
        v_w_out.reshape(rows_out, cols_out), "adamw_w_out", pick=ci)]
    zeros_sh = [jnp.zeros(G[k].shape, F32) for k in _SHARD_SMALL]
    pk = lambda dct: _pack([dct[k] for k in _REPL] + zeros_sh)
    small_res = _adamw(pk(W), (r_small,), pk(M), pk(V), "adamw_small")
    shapes = [G[k].shape for k in small_names]
    unp = [_unpack(a, shapes) for a in small_res]
    for i, k in enumerate(_REPL):
        res[k] = [u[i] for u in unp]
    g_sh = []
    for i, k in enumerate(_SHARD_SMALL):
        gfull = unp[0][len(_REPL) + i]
        wdt = W[k].shape[-1]
        g_sh.append(lax.dynamic_slice_in_dim(gfull, chip * wdt, wdt, axis=2))
    g_sh_p = _pack(g_sh)
    sh_res = _adamw(_pack([W[k] for k in _SHARD_SMALL]), (g_sh_p,), _pack([M[k] for k in _SHARD_SMALL]),
                    _pack([V[k] for k in _SHARD_SMALL]), "adamw_shard_small")
    shapes2 = [W[k].shape for k in _SHARD_SMALL]
    unp2 = [_unpack(a, shapes2) for a in sh_res]
    for i, k in enumerate(_SHARD_SMALL):
        res[k] = [u[i] for u in unp2]

    outs = [loss, grad_x]
    for j in range(4):
        outs += [res[k][j] for k in WEIGHTS]
    return tuple(outs)
```

```python
import functools

import numpy as np
import jax
import jax.numpy as jnp
from jax import lax
from jax.experimental import pallas as pl
from jax.experimental.pallas import tpu as pltpu

F32 = jnp.float32
BF16 = jnp.bfloat16
HI = lax.Precision.HIGHEST
MESH = pl.DeviceIdType.MESH

EPS = 1e-6
D_MODEL = 2048
GROUP_W = 512
SEQ = 2048
DEPTH = 2
N_IN = 6176
GRID_W = 64
CONV_K = 31
NA_HEADS = 8
NA_DH = 64
NA_ROWS = 8
NA_COLS = 16
GLA_HEADS = 4
GLA_DK = 64
GLA_DV = 128
GLA_RANK = 16
GLA_TAU = 16.0
CHUNK = 64
POOL_WINDOWS = (2, 4, 8, 16)
ADAM_LR, ADAM_B1, ADAM_B2, ADAM_EPS, ADAM_WD, ADAM_STEP = 0.001, 0.9, 0.999, 1e-08, 0.01, 10

A_VAL, A_GLU, A_GATE = 0, 512, 1024
B_Q, B_K, B_V, B_GATE = 1536, 2048, 2560, 3072
C_Q, C_K, C_V, C_GATE = 3584, 3840, 4096, 4608
D_VAL, D_GATE = 5120, 5632
LR_OFF = 6144
NZ = 6400
NEG = -1e30
VMEM_LIMIT = 56 * 1024 * 1024


def _cp(sem=None):
    return pltpu.CompilerParams(dimension_semantics=sem, vmem_limit_bytes=VMEM_LIMIT)


def _sigmoid(x):
    return 1.0 / (1.0 + jnp.exp(-x))


def _silu(x):
    return x * _sigmoid(x)


def _dsilu(x):
    s = _sigmoid(x)
    return s * (1.0 + x * (1.0 - s))


def _matmul(a, b, *, dims, out_dtype, tm, tn, tk, name, res=None):
    if dims == "nn":
        (M, K), N = a.shape, b.shape[1]
    elif dims == "nt":
        (M, K), N = a.shape, b.shape[0]
    else:
        (K, M), N = a.shape, b.shape[1]
    tm, tn, tk = min(tm, M), min(tn, N), min(tk, K)
    nk = K // tk
    assert M % tm == 0 and N % tn == 0 and K % tk == 0, (M, N, K, tm, tn, tk)
    dn = {"nn": (((1,), (0,)), ((), ())), "nt": (((1,), (1,)), ((), ())), "tn": (((0,), (0,)), ((), ()))}[dims]
    if dims == "tn":
        a_spec = pl.BlockSpec((tk, tm), lambda i, j, k: (k, i))
    else:
        a_spec = pl.BlockSpec((tm, tk), lambda i, j, k: (i, k))
    if dims == "nt":
        b_spec = pl.BlockSpec((tn, tk), lambda i, j, k: (j, k))
    else:
        b_spec = pl.BlockSpec((tk, tn), lambda i, j, k: (k, j))
    o_spec = pl.BlockSpec((tm, tn), lambda i, j, k: (i, j))
    has_res = res is not None

    def body(*refs):
        if has_res:
            a_ref, b_ref, r_ref, o_ref, acc = refs
        else:
            a_ref, b_ref, o_ref, acc = refs
        k = pl.program_id(2)

        @pl.when(k == 0)
        def _():
            acc[...] = jnp.zeros_like(acc)

        acc[...] += lax.dot_general(a_ref[...], b_ref[...], dn, preferred_element_type=F32)

        @pl.when(k == nk - 1)
        def _():
            r = acc[...]
            if has_res:
                r = r + r_ref[...]
            o_ref[...] = r.astype(o_ref.dtype)

    in_specs = [a_spec, b_spec] + ([o_spec] if has_res else [])
    args = (a, b) + ((res,) if has_res else ())
    return pl.pallas_call(
        body, out_shape=jax.ShapeDtypeStruct((M, N), out_dtype), grid=(M // tm, N // tn, nk),
        in_specs=in_specs, out_specs=o_spec, scratch_shapes=[pltpu.VMEM((tm, tn), F32)],
        name=name, compiler_params=_cp(("parallel", "parallel", "arbitrary")))(*args)


def _concat_cols(pieces, width, name):
    pairs = [p if isinstance(p, tuple) else (p,) for p in pieces]
    T = pairs[0][0].shape[0]
    tm = min(512, T)
    dt = BF16
    offs = np.cumsum([0] + [p[0].shape[1] for p in pairs])
    flat = [a for p in pairs for a in p]

    def body(*refs):
        o_ref = refs[-1]
        k = 0
        for p, a, b in zip(pairs, offs[:-1], offs[1:]):
            val = refs[k][...] if len(p) == 1 else refs[k][...] + refs[k + 1][...]
            o_ref[:, a:b] = val.astype(dt)
            k += len(p)
        if offs[-1] < width:
            o_ref[:, offs[-1]:width] = jnp.zeros((tm, width - offs[-1]), dt)

    return pl.pallas_call(
        body, out_shape=jax.ShapeDtypeStruct((T, width), dt), grid=(T // tm,),
        in_specs=[pl.BlockSpec((tm, a.shape[1]), lambda i: (i, 0)) for a in flat],
        out_specs=pl.BlockSpec((tm, width), lambda i: (i, 0)), name=name, compiler_params=_cp(("parallel",)))(*flat)


def _rmsnorm_fwd(x, g, name):
    T, D = x.shape
    tm = 256

    def body(x_ref, g_ref, h_ref):
        xv = x_ref[...]
        r = lax.rsqrt(jnp.mean(xv * xv, axis=-1, keepdims=True) + EPS)
        h_ref[...] = (xv * r * g_ref[...]).astype(h_ref.dtype)

    return pl.pallas_call(
        body, out_shape=jax.ShapeDtypeStruct((T, D), BF16), grid=(T // tm,),
        in_specs=[pl.BlockSpec((tm, D), lambda i: (i, 0)), pl.BlockSpec((1, D), lambda i: (0, 0))],
        out_specs=pl.BlockSpec((tm, D), lambda i: (i, 0)), name=name, compiler_params=_cp(("parallel",)))(x, g)


def _rmsnorm_bwd(x, g, dh, dres, name):
    T, D = x.shape
    tm = 256

    def body(x_ref, g_ref, dh_ref, dres_ref, dx_ref, dxb_ref, dg_ref):
        xv = x_ref[...]
        r = lax.rsqrt(jnp.mean(xv * xv, axis=-1, keepdims=True) + EPS)
        xh = xv * r
        dh_v = dh_ref[...]
        dxh = dh_v * g_ref[...]
        dx = r * (dxh - xh * jnp.mean(dxh * xh, axis=-1, keepdims=True)) + dres_ref[...]
        dx_ref[...] = dx
        dxb_ref[...] = dx.astype(BF16)

        @pl.when(pl.program_id(0) == 0)
        def _():
            dg_ref[...] = jnp.zeros_like(dg_ref)

        dg_ref[...] += jnp.sum(dh_v * xh, axis=0, keepdims=True)

    row = pl.BlockSpec((tm, D), lambda i: (i, 0))
    vec = pl.BlockSpec((1, D), lambda i: (0, 0))
    return pl.pallas_call(
        body, out_shape=(jax.ShapeDtypeStruct((T, D), F32), jax.ShapeDtypeStruct((T, D), BF16), jax.ShapeDtypeStruct((1, D), F32)),
        grid=(T // tm,), in_specs=[row, vec, row, row], out_specs=(row, row, vec), name=name,
        compiler_params=_cp(("arbitrary",)))(x, g, dh, dres)


_GATE_COLS = (A_GATE // GROUP_W, B_GATE // GROUP_W, C_GATE // GROUP_W, D_GATE // GROUP_W)
_OP_TM = 256


def _out_proj_fwd(pres, z, w_out, x, target, name):
    T, D = x.shape
    tm = min(_OP_TM, T)
    with_loss = target is not None

    def body(*refs):
        pa, pb, pc, pd, ga, gb, gc, gd, w_ref, x_ref = refs[:10]
        refs = refs[10:]
        if with_loss:
            t_ref, y_ref, d_ref, db_ref, l_ref = refs
        else:
            y_ref, o_ref = refs
        for n_, (p, g) in enumerate(((pa, ga), (pb, gb), (pc, gc), (pd, gd))):
            y_ref[:, n_ * GROUP_W:(n_ + 1) * GROUP_W] = (p[...] * _silu(g[...])).astype(BF16)
        out = jnp.dot(y_ref[...], w_ref[...], preferred_element_type=F32) + x_ref[...]
        if with_loss:
            e = out - t_ref[...]
            d = e * (1.0 / D)
            d_ref[...] = d
            db_ref[...] = d.astype(BF16)

            @pl.when(pl.program_id(0) == 0)
            def _():
                l_ref[...] = jnp.zeros_like(l_ref)

            l_ref[...] += jnp.sum(jnp.sum(e * e, axis=-1, keepdims=True) * (0.5 / D), axis=0, keepdims=True)
        else:
            o_ref[...] = out

    pre_spec = pl.BlockSpec((tm, GROUP_W), lambda i: (i, 0))
    gate_specs = [pl.BlockSpec((tm, GROUP_W), functools.partial(lambda i, c: (i, c), c=c)) for c in _GATE_COLS]
    row = pl.BlockSpec((tm, D), lambda i: (i, 0))
    w_spec = pl.BlockSpec((4 * GROUP_W, D), lambda i: (0, 0))
    in_specs = [pre_spec] * 4 + gate_specs + [w_spec, row]
    args = list(pres) + [z, z, z, z, w_out, x]
    if with_loss:
        in_specs.append(row)
        args.append(target)
        out_shape = (jax.ShapeDtypeStruct((T, D), BF16), jax.ShapeDtypeStruct((T, D), F32), jax.ShapeDtypeStruct((T, D), BF16),
                     jax.ShapeDtypeStruct((1, 1), F32))
        out_specs = (row, row, row, pl.BlockSpec((1, 1), lambda i: (0, 0)))
    else:
        out_shape = (jax.ShapeDtypeStruct((T, D), BF16), jax.ShapeDtypeStruct((T, D), F32))
        out_specs = (row, row)
    return pl.pallas_call(body, out_shape=out_shape, grid=(T // tm,), in_specs=in_specs, out_specs=out_specs, name=name,
                          compiler_params=_cp(("arbitrary",)))(*args)


def _out_proj_bwd(dout_bf, w_out_t, pres, z, name):
    T, D = dout_bf.shape
    tm = min(_OP_TM, T)

    def body(do_ref, w_ref, pa, pb, pc, pd, ga, gb, gc, gd, dpa, dpb, dpc, dpd, dga, dgb, dgc, dgd):
        dy = jnp.dot(do_ref[...], w_ref[...], preferred_element_type=F32)
        for n_, (p, g, dp, dg) in enumerate(((pa, ga, dpa, dga), (pb, gb, dpb, dgb), (pc, gc, dpc, dgc), (pd, gd, dpd, dgd))):
            d = dy[:, n_ * GROUP_W:(n_ + 1) * GROUP_W]
            gv = g[...]
            dp[...] = d * _silu(gv)
            dg[...] = (d * p[...] * _dsilu(gv)).astype(BF16)

    pre_spec = pl.BlockSpec((tm, GROUP_W), lambda i: (i, 0))
    gate_specs = [pl.BlockSpec((tm, GROUP_W), functools.partial(lambda i, c: (i, c), c=c)) for c in _GATE_COLS]
    outs = tuple([jax.ShapeDtypeStruct((T, GROUP_W), F32)] * 4 + [jax.ShapeDtypeStruct((T, GROUP_W), BF16)] * 4)
    return pl.pallas_call(
        body, out_shape=outs, grid=(T // tm,),
        in_specs=[pl.BlockSpec((tm, D), lambda i: (i, 0)), pl.BlockSpec((D, 4 * GROUP_W), lambda i: (0, 0))] + [pre_spec] * 4 + gate_specs,
        out_specs=tuple([pre_spec] * 8), name=name, compiler_params=_cp(("parallel",)))(dout_bf, w_out_t, *pres, z, z, z, z)


_PAD = 16
_RC = 256


def _conv_fwd(z, conv_w32, conv_b, S, name):
    T = z.shape[0]
    E = T // S
    LW = 128

    def body(val_ref, glu_ref, w_ref, b_ref, y_ref, upad):
        upad[0:_PAD, :] = jnp.zeros((_PAD, LW), F32)
        upad[_PAD + S:_PAD + S + _PAD, :] = jnp.zeros((_PAD, LW), F32)
        upad[_PAD:_PAD + S, :] = val_ref[...] * _sigmoid(glu_ref[...])
        for r in range(S // _RC):
            acc = jnp.broadcast_to(b_ref[...], (_RC, LW))
            for k in range(CONV_K):
                st = r * _RC + k + 1
                acc = acc + upad[st:st + _RC, :] * w_ref[k:k + 1, :]
            y_ref[r * _RC:(r + 1) * _RC, :] = acc

    return pl.pallas_call(
        body, out_shape=jax.ShapeDtypeStruct((T, GROUP_W), F32), grid=(E, GROUP_W // LW),
        in_specs=[pl.BlockSpec((S, LW), lambda e, j: (e, A_VAL // LW + j)),
                  pl.BlockSpec((S, LW), lambda e, j: (e, A_GLU // LW + j)),
                  pl.BlockSpec((32, LW), lambda e, j: (0, j)),
                  pl.BlockSpec((1, LW), lambda e, j: (0, j))],
        out_specs=pl.BlockSpec((S, LW), lambda e, j: (e, j)),
        scratch_shapes=[pltpu.VMEM((S + 2 * _PAD, LW), F32)],
        name=name, compiler_params=_cp(("parallel", "parallel")))(z, z, conv_w32, conv_b)


def _conv_bwd(z, conv_w32, dyc, S, name):
    T = z.shape[0]
    E = T // S
    LW = 128

    def body(val_ref, glu_ref, w_ref, dy_ref, dval_ref, dglu_ref, dw_ref, db_ref, upad, dpad):
        e = pl.program_id(1)
        zeros = jnp.zeros((_PAD, LW), F32)
        upad[0:_PAD, :] = zeros
        upad[_PAD + S:_PAD + S + _PAD, :] = zeros
        dpad[0:_PAD, :] = zeros
        dpad[_PAD + S:_PAD + S + _PAD, :] = zeros
        upad[_PAD:_PAD + S, :] = val_ref[...] * _sigmoid(glu_ref[...])
        dpad[_PAD:_PAD + S, :] = dy_ref[...]

        @pl.when(e == 0)
        def _():
            dw_ref[...] = jnp.zeros_like(dw_ref)
            db_ref[...] = jnp.zeros_like(db_ref)

        db_ref[...] += jnp.sum(dy_ref[...], axis=0, keepdims=True)
        for r in range(S // _RC):
            dyr = dy_ref[r * _RC:(r + 1) * _RC, :]
            du = jnp.zeros((_RC, LW), F32)
            for k in range(CONV_K):
                st = r * _RC + k + 1
                dw_ref[k:k + 1, :] += jnp.sum(dyr * upad[st:st + _RC, :], axis=0, keepdims=True)
                sd = r * _RC + (CONV_K - 1 - k) + 1
                du = du + dpad[sd:sd + _RC, :] * w_ref[k:k + 1, :]
            sl = slice(r * _RC, (r + 1) * _RC)
            val = val_ref[sl, :]
            sg = _sigmoid(glu_ref[sl, :])
            dval_ref[sl, :] = (du * sg).astype(BF16)
            dglu_ref[sl, :] = (du * val * sg * (1.0 - sg)).astype(BF16)

    blk = pl.BlockSpec((S, LW), lambda j, e: (e, j))
    return pl.pallas_call(
        body, out_shape=(jax.ShapeDtypeStruct((T, GROUP_W), BF16), jax.ShapeDtypeStruct((T, GROUP_W), BF16),
                         jax.ShapeDtypeStruct((32, GROUP_W), F32), jax.ShapeDtypeStruct((1, GROUP_W), F32)),
        grid=(GROUP_W // LW, E),
        in_specs=[pl.BlockSpec((S, LW), lambda j, e: (e, A_VAL // LW + j)),
                  pl.BlockSpec((S, LW), lambda j, e: (e, A_GLU // LW + j)),
                  pl.BlockSpec((32, LW), lambda j, e: (0, j)), blk],
        out_specs=(blk, blk, pl.BlockSpec((32, LW), lambda j, e: (0, j)), pl.BlockSpec((1, LW), lambda j, e: (0, j))),
        scratch_shapes=[pltpu.VMEM((S + 2 * _PAD, LW), F32), pltpu.VMEM((S + 2 * _PAD, LW), F32)],
        name=name, compiler_params=_cp(("parallel", "arbitrary")))(z, z, conv_w32, dyc)


def _ln_silu_fwd(yc, g, b, name):
    T, C = yc.shape
    tm = 256

    def body(y_ref, g_ref, b_ref, o_ref):
        y = y_ref[...]
        mu = jnp.mean(y, axis=-1, keepdims=True)
        yc_ = y - mu
        r = lax.rsqrt(jnp.mean(yc_ * yc_, axis=-1, keepdims=True) + EPS)
        o_ref[...] = _silu(yc_ * r * g_ref[...] + b_ref[...])

    row = pl.BlockSpec((tm, C), lambda i: (i, 0))
    vec = pl.BlockSpec((1, C), lambda i: (0, 0))
    return pl.pallas_call(body, out_shape=jax.ShapeDtypeStruct((T, C), F32), grid=(T // tm,),
                          in_specs=[row, vec, vec], out_specs=row, name=name, compiler_params=_cp(("parallel",)))(yc, g, b)


def _ln_silu_bwd(yc, g, b, dpre, name):
    T, C = yc.shape
    tm = 256

    def body(y_ref, g_ref, b_ref, dp_ref, dy_ref, dg_ref, db_ref):
        y = y_ref[...]
        mu = jnp.mean(y, axis=-1, keepdims=True)
        yc_ = y - mu
        r = lax.rsqrt(jnp.mean(yc_ * yc_, axis=-1, keepdims=True) + EPS)
        xh = yc_ * r
        gv = g_ref[...]
        dln = dp_ref[...] * _dsilu(xh * gv + b_ref[...])
        dxh = dln * gv
        dy_ref[...] = r * (dxh - jnp.mean(dxh, axis=-1, keepdims=True) - xh * jnp.mean(dxh * xh, axis=-1, keepdims=True))

        @pl.when(pl.program_id(0) == 0)
        def _():
            dg_ref[...] = jnp.zeros_like(dg_ref)
            db_ref[...] = jnp.zeros_like(db_ref)

        dg_ref[...] += jnp.sum(dln * xh, axis=0, keepdims=True)
        db_ref[...] += jnp.sum(dln, axis=0, keepdims=True)

    row = pl.BlockSpec((tm, C), lambda i: (i, 0))
    vec = pl.BlockSpec((1, C), lambda i: (0, 0))
    return pl.pallas_call(
        body, out_shape=(jax.ShapeDtypeStruct((T, C), F32), jax.ShapeDtypeStruct((1, C), F32), jax.ShapeDtypeStruct((1, C), F32)),
        grid=(T // tm,), in_specs=[row, vec, vec, row], out_specs=(row, vec, vec), name=name,
        compiler_params=_cp(("arbitrary",)))(yc, g, b, dpre)


def _pool_counts(S, w, rows0, n):
    t = (lax.broadcasted_iota(jnp.int32, (n, 1), 0) + rows0)
    lo = jnp.maximum(t - w // 2, 0)
    hi = jnp.minimum(t + w // 2, S)
    return (hi - lo).astype(F32)


def _pool_fwd(z, pool_w, pool_scale, S, name):
    T = z.shape[0]
    E = T // S
    CG = 128

    def body(u_ref, w_ref, s_ref, o_ref, upad, dif):
        zeros = jnp.zeros((_PAD, GROUP_W), F32)
        upad[0:_PAD, :] = zeros
        upad[_PAD + S:_PAD + S + _PAD, :] = zeros
        upad[_PAD:_PAD + S, :] = u_ref[...]
        for gi, w in enumerate(POOL_WINDOWS):
            ls = slice(gi * CG, (gi + 1) * CG)
            for r in range(S // _RC):
                acc = jnp.zeros((_RC, CG), F32)
                for j in range(-(w // 2), w // 2):
                    st = _PAD + r * _RC + j
                    acc = acc + upad[st:st + _RC, ls]
                cnt = _pool_counts(S, w, r * _RC, _RC)
                dif[r * _RC:(r + 1) * _RC, :] = (acc / cnt - u_ref[r * _RC:(r + 1) * _RC, ls]).astype(BF16)
            yp = jnp.dot(dif[...], w_ref[gi], preferred_element_type=F32)
            o_ref[:, ls] = yp * s_ref[:, ls]

    return pl.pallas_call(
        body, out_shape=jax.ShapeDtypeStruct((T, GROUP_W), F32), grid=(E,),
        in_specs=[pl.BlockSpec((S, GROUP_W), lambda e: (e, D_VAL // GROUP_W)),
                  pl.BlockSpec((4, CG, CG), lambda e: (0, 0, 0)),
                  pl.BlockSpec((1, GROUP_W), lambda e: (0, 0))],
        out_specs=pl.BlockSpec((S, GROUP_W), lambda e: (e, 0)),
        scratch_shapes=[pltpu.VMEM((S + 2 * _PAD, GROUP_W), F32), pltpu.VMEM((S, CG), BF16)],
        name=name, compiler_params=_cp(("parallel",)))(z, pool_w, pool_scale)


def _pool_bwd(z, pool_w, pool_scale, dpre, S, name):
    T = z.shape[0]
    E = T // S
    CG = 128

    def body(u_ref, w_ref, s_ref, dp_ref, du_ref, dw_ref, ds_ref, upad, dif, qpad):
        zeros = jnp.zeros((_PAD, GROUP_W), F32)
        upad[0:_PAD, :] = zeros
        upad[_PAD + S:_PAD + S + _PAD, :] = zeros
        upad[_PAD:_PAD + S, :] = u_ref[...]
        zc = jnp.zeros((_PAD, CG), F32)
        qpad[0:_PAD, :] = zc
        qpad[_PAD + S:_PAD + S + _PAD, :] = zc

        @pl.when(pl.program_id(0) == 0)
        def _():
            dw_ref[...] = jnp.zeros_like(dw_ref)
            ds_ref[...] = jnp.zeros_like(ds_ref)

        for gi, w in enumerate(POOL_WINDOWS):
            ls = slice(gi * CG, (gi + 1) * CG)
            for r in range(S // _RC):
                acc = jnp.zeros((_RC, CG), F32)
                for j in range(-(w // 2), w // 2):
                    st = _PAD + r * _RC + j
                    acc = acc + upad[st:st + _RC, ls]
                cnt = _pool_counts(S, w, r * _RC, _RC)
                dif[r * _RC:(r + 1) * _RC, :] = (acc / cnt - u_ref[r * _RC:(r + 1) * _RC, ls]).astype(BF16)
            dp = dp_ref[:, ls]
            yp = jnp.dot(dif[...], w_ref[gi], preferred_element_type=F32)
            ds_ref[:, ls] += jnp.sum(dp * yp, axis=0, keepdims=True)
            dys = (dp * s_ref[:, ls]).astype(BF16)
            dw_ref[gi] += lax.dot_general(dif[...], dys, (((0,), (0,)), ((), ())), preferred_element_type=F32)
            dm = lax.dot_general(dys, w_ref[gi], (((1,), (1,)), ((), ())), preferred_element_type=F32)
            for r in range(S // _RC):
                cnt = _pool_counts(S, w, r * _RC, _RC)
                qpad[_PAD + r * _RC:_PAD + (r + 1) * _RC, :] = dm[r * _RC:(r + 1) * _RC, :] / cnt
            for r in range(S // _RC):
                acc = -dm[r * _RC:(r + 1) * _RC, :]
                for j in range(-(w // 2) + 1, w // 2 + 1):
                    st = _PAD + r * _RC + j
                    acc = acc + qpad[st:st + _RC, :]
                du_ref[r * _RC:(r + 1) * _RC, ls] = acc.astype(BF16)

    return pl.pallas_call(
        body, out_shape=(jax.ShapeDtypeStruct((T, GROUP_W), BF16), jax.ShapeDtypeStruct((4, CG, CG), F32),
                         jax.ShapeDtypeStruct((1, GROUP_W), F32)), grid=(E,),
        in_specs=[pl.BlockSpec((S, GROUP_W), lambda e: (e, D_VAL // GROUP_W)),
                  pl.BlockSpec((4, CG, CG), lambda e: (0, 0, 0)),
                  pl.BlockSpec((1, GROUP_W), lambda e: (0, 0)),
                  pl.BlockSpec((S, GROUP_W), lambda e: (e, 0))],
        out_specs=(pl.BlockSpec((S, GROUP_W), lambda e: (e, 0)), pl.BlockSpec((4, CG, CG), lambda e: (0, 0, 0)),
                   pl.BlockSpec((1, GROUP_W), lambda e: (0, 0))),
        scratch_shapes=[pltpu.VMEM((S + 2 * _PAD, GROUP_W), F32), pltpu.VMEM((S, CG), BF16),
                        pltpu.VMEM((S + 2 * _PAD, CG), F32)],
        name=name, compiler_params=_cp(("arbitrary",)))(z, pool_w, pool_scale, dpre)


def _na_tables():
    d = np.arange(NA_ROWS)[:, None]
    kr = np.arange(NA_ROWS)[None, :]
    ro = kr - d + (NA_ROWS - 1)
    qc = np.arange(GRID_W)[:, None]
    kc = np.arange(GRID_W)[None, :]
    cs = np.clip(qc - NA_COLS // 2, 0, GRID_W - NA_COLS)
    valid = (kc >= cs) & (kc < cs + NA_COLS)
    co = np.clip(kc - qc + (NA_COLS - 1), 0, 2 * NA_COLS - 2)
    return ro, co, valid


def _na_onehots():
    ro, co, valid = _na_tables()
    e_np = np.zeros((GRID_W, GRID_W, 128), np.float32)
    qi, ki = np.nonzero(valid)
    e_np[qi, ki, co[qi, ki]] = 1.0
    a_np = np.zeros((16, NA_ROWS * NA_ROWS), np.float32)
    a_np[ro.reshape(-1), np.arange(NA_ROWS * NA_ROWS)] = 1.0
    mask = np.where(valid, 0.0, NEG).astype(np.float32).reshape(1, GRID_W * GRID_W)
    return e_np.reshape(GRID_W * GRID_W, 128), a_np, mask


def _na_bias(rpb, name):
    e_np, _, mask = _na_onehots()
    H = NA_HEADS
    rp = jnp.pad(rpb, ((0, 0), (0, 1), (0, 128 - rpb.shape[2])))

    def bands(r_ref, e_ref, m_ref, o_ref):
        o_ref[0] = lax.dot_general(r_ref[0], e_ref[...], (((1,), (1,)), ((), ())), precision=HI,
                                   preferred_element_type=F32) + m_ref[...]

    t = pl.pallas_call(
        bands, out_shape=jax.ShapeDtypeStruct((H, 16, GRID_W * GRID_W), F32), grid=(H,),
        in_specs=[pl.BlockSpec((1, 16, 128), lambda h: (h, 0, 0)),
                  pl.BlockSpec((GRID_W * GRID_W, 128), lambda h: (0, 0)),
                  pl.BlockSpec((1, GRID_W * GRID_W), lambda h: (0, 0))],
        out_specs=pl.BlockSpec((1, 16, GRID_W * GRID_W), lambda h: (h, 0, 0)),
        name=name + "_bands", compiler_params=_cp(("parallel",)))(rp, jnp.asarray(e_np), jnp.asarray(mask))
    t = t.reshape(H, 16, GRID_W, GRID_W)

    def place(t_ref, o_ref):
        for d in range(NA_ROWS):
            for kr in range(NA_ROWS):
                o_ref[0, d, :, kr * GRID_W:(kr + 1) * GRID_W] = t_ref[0, kr - d + NA_ROWS - 1]

    return pl.pallas_call(
        place, out_shape=jax.ShapeDtypeStruct((H, NA_ROWS, GRID_W, NA_ROWS * GRID_W), F32), grid=(H,),
        in_specs=[pl.BlockSpec((1, 16, GRID_W, GRID_W), lambda h: (h, 0, 0, 0))],
        out_specs=pl.BlockSpec((1, NA_ROWS, GRID_W, NA_ROWS * GRID_W), lambda h: (h, 0, 0, 0)),
        name=name, compiler_params=_cp(("parallel",)))(t)


def _seg_mean_matrix(width, seg):
    i = np.arange(width)
    return jnp.asarray((i[:, None] // seg == i[None, :] // seg).astype(np.float32) / seg, BF16)


def _seg_mean(x, seg_ref):
    hi = x.astype(BF16)
    lo = (x - hi.astype(F32)).astype(BF16)
    return (jnp.dot(hi, seg_ref[...], preferred_element_type=F32) + jnp.dot(lo, seg_ref[...], preferred_element_type=F32))


def _na_fwd(z, qg, kg, bias, S, name):
    T = z.shape[0]
    E = T // S
    rows = S // GRID_W
    WIN = NA_ROWS * GRID_W
    seg = _seg_mean_matrix(128, NA_DH)

    def body(q_ref, k_ref, v_ref, qg_ref, kg_ref, bias_ref, seg_ref, o_ref, qs, ks, vs, s_all, p_all):
        for c in range(S // _RC):
            sl = slice(c * _RC, (c + 1) * _RC)
            q = q_ref[sl, :]
            k = k_ref[sl, :]
            qn = q * lax.rsqrt(_seg_mean(q * q, seg_ref) + EPS) * qg_ref[...]
            kn = k * lax.rsqrt(_seg_mean(k * k, seg_ref) + EPS) * kg_ref[...]
            v = v_ref[sl, :]
            for hh in range(2):
                ls = slice(hh * NA_DH, (hh + 1) * NA_DH)
                qs[hh, sl, :] = qn[:, ls].astype(BF16)
                ks[hh, sl, :] = kn[:, ls].astype(BF16)
                vs[hh, sl, :] = v[:, ls].astype(BF16)
        def where(r):
            rs = jnp.clip(r - NA_ROWS // 2, 0, rows - NA_ROWS)
            return rs, pl.multiple_of(r * GRID_W, GRID_W), pl.multiple_of(rs * GRID_W, GRID_W)

        def scores(r, carry):
            rs, q0, k0 = where(r)
            for hh in range(2):
                s = lax.dot_general(qs[hh, pl.ds(q0, GRID_W), :], ks[hh, pl.ds(k0, WIN), :], (((1,), (1,)), ((), ())),
                                    preferred_element_type=F32) * (NA_DH ** -0.5)
                s_all[hh, pl.ds(q0, GRID_W), :] = s + bias_ref[hh, r - rs]
            return carry
        lax.fori_loop(0, rows, scores, 0, unroll=8)

        def soft(r, carry):
            _, q0, _ = where(r)
            for hh in range(2):
                s = s_all[hh, pl.ds(q0, GRID_W), :]
                p = jnp.exp(s - jnp.max(s, axis=-1, keepdims=True))
                p_all[hh, pl.ds(q0, GRID_W), :] = (p * (1.0 / jnp.sum(p, axis=-1, keepdims=True))).astype(BF16)
            return carry
        lax.fori_loop(0, rows, soft, 0, unroll=4)

        def outp(r, carry):
            _, q0, k0 = where(r)
            outs = [jnp.dot(p_all[hh, pl.ds(q0, GRID_W), :], vs[hh, pl.ds(k0, WIN), :], preferred_element_type=F32)
                    for hh in range(2)]
            o_ref[pl.ds(q0, GRID_W), :] = jnp.concatenate(outs, axis=1)
            return carry
        lax.fori_loop(0, rows, outp, 0, unroll=8)

    LW = 128
    return pl.pallas_call(
        body, out_shape=jax.ShapeDtypeStruct((T, GROUP_W), F32), grid=(E, GROUP_W // LW),
        in_specs=[pl.BlockSpec((S, LW), lambda e, j: (e, B_Q // LW + j)),
                  pl.BlockSpec((S, LW), lambda e, j: (e, B_K // LW + j)),
                  pl.BlockSpec((S, LW), lambda e, j: (e, B_V // LW + j)),
                  pl.BlockSpec((1, LW), lambda e, j: (0, j)),
                  pl.BlockSpec((1, LW), lambda e, j: (0, j)),
                  pl.BlockSpec((2, NA_ROWS, GRID_W, WIN), lambda e, j: (j, 0, 0, 0)),
                  pl.BlockSpec((LW, LW), lambda e, j: (0, 0))],
        out_specs=pl.BlockSpec((S, LW), lambda e, j: (e, j)),
        scratch_shapes=[pltpu.VMEM((2, S, NA_DH), BF16)] * 3 + [pltpu.VMEM((2, S, WIN), F32), pltpu.VMEM((2, S, WIN), BF16)],
        name=name, compiler_params=_cp(("parallel", "parallel")))(z, z, z, qg, kg, bias, seg)


def _na_bwd(z, qg, kg, bias, do, S, name):
    T = z.shape[0]
    E = T // S
    rows = S // GRID_W
    WIN = NA_ROWS * GRID_W
    seg = _seg_mean_matrix(128, NA_DH)
    SC = NA_DH ** -0.5

    def body(q_ref, k_ref, v_ref, qg_ref, kg_ref, bias_ref, seg_ref, do_ref,
             dq_ref, dk_ref, dv_ref, dbias_ref, dqg_ref, dkg_ref, qs, ks, vs, dos, dqn, dkn, dvs, akt, avt,
             s_all, dp_all, p_all, ds_all):
        e = pl.program_id(1)

        @pl.when(e == 0)
        def _():
            dbias_ref[...] = jnp.zeros_like(dbias_ref)
            dqg_ref[...] = jnp.zeros_like(dqg_ref)
            dkg_ref[...] = jnp.zeros_like(dkg_ref)

        for c in range(S // _RC):
            sl = slice(c * _RC, (c + 1) * _RC)
            q = q_ref[sl, :]
            k = k_ref[sl, :]
            qn = q * lax.rsqrt(_seg_mean(q * q, seg_ref) + EPS) * qg_ref[...]
            kn = k * lax.rsqrt(_seg_mean(k * k, seg_ref) + EPS) * kg_ref[...]
            v = v_ref[sl, :]
            dd = do_ref[sl, :]
            for hh in range(2):
                ls = slice(hh * NA_DH, (hh + 1) * NA_DH)
                qs[hh, sl, :] = qn[:, ls].astype(BF16)
                ks[hh, sl, :] = kn[:, ls].astype(BF16)
                vs[hh, sl, :] = v[:, ls].astype(BF16)
                dos[hh, sl, :] = dd[:, ls].astype(BF16)
        akt[...] = jnp.zeros_like(akt)
        avt[...] = jnp.zeros_like(avt)

        def where(r):
            rs = jnp.clip(r - NA_ROWS // 2, 0, rows - NA_ROWS)
            return rs, pl.multiple_of(r * GRID_W, GRID_W), pl.multiple_of(rs * GRID_W, GRID_W)

        for hh in range(2):
            ls = slice(hh * NA_DH, (hh + 1) * NA_DH)

            def products(r, carry, hh=hh):
                rs, q0, k0 = where(r)
                s = lax.dot_general(qs[hh, pl.ds(q0, GRID_W), :], ks[hh, pl.ds(k0, WIN), :], (((1,), (1,)), ((), ())),
                                    preferred_element_type=F32) * SC
                s_all[pl.ds(q0, GRID_W), :] = s + bias_ref[hh, r - rs]
                dp_all[pl.ds(q0, GRID_W), :] = lax.dot_general(dos[hh, pl.ds(q0, GRID_W), :], vs[hh, pl.ds(k0, WIN), :],
                                                               (((1,), (1,)), ((), ())), preferred_element_type=F32)
                return carry
            lax.fori_loop(0, rows, products, 0, unroll=8)

            def soft(r, carry, hh=hh):
                rs, q0, _ = where(r)
                s = s_all[pl.ds(q0, GRID_W), :]
                p = jnp.exp(s - jnp.max(s, axis=-1, keepdims=True))
                p = p * (1.0 / jnp.sum(p, axis=-1, keepdims=True))
                dp = dp_all[pl.ds(q0, GRID_W), :]
                ds = p * (dp - jnp.sum(p * dp, axis=-1, keepdims=True))
                dbias_ref[hh, r - rs] += ds
                p_all[pl.ds(q0, GRID_W), :] = p.astype(BF16)
                ds_all[pl.ds(q0, GRID_W), :] = ds.astype(BF16)
                return carry
            lax.fori_loop(0, rows, soft, 0, unroll=4)

            def grads(r, carry, hh=hh, ls=ls):
                rs, q0, k0 = where(r)
                par = rs % 2
                t0 = (rs + par) // 2
                qr = qs[hh, pl.ds(q0, GRID_W), :]
                dor = dos[hh, pl.ds(q0, GRID_W), :]
                dsb = ds_all[pl.ds(q0, GRID_W), :]
                dqn[pl.ds(q0, GRID_W), ls] = jnp.dot(dsb, ks[hh, pl.ds(k0, WIN), :], preferred_element_type=F32) * SC
                dkt = lax.dot_general(qr, dsb, (((0,), (0,)), ((), ())), preferred_element_type=F32) * SC
                dvt = lax.dot_general(dor, p_all[pl.ds(q0, GRID_W), :], (((0,), (0,)), ((), ())), preferred_element_type=F32)
                akt[hh, par, pl.ds(t0, WIN // 128)] += jnp.stack([dkt[:, 128 * i:128 * (i + 1)] for i in range(WIN // 128)])
                avt[hh, par, pl.ds(t0, WIN // 128)] += jnp.stack([dvt[:, 128 * i:128 * (i + 1)] for i in range(WIN // 128)])
                return carry
            lax.fori_loop(0, rows, grads, 0, unroll=8)

        for hh in range(2):
            ls = slice(hh * NA_DH, (hh + 1) * NA_DH)
            for i in range(S // 128):
                for acc, dst in ((akt, dkn), (avt, dvs)):
                    odd = jnp.concatenate([acc[hh, 1, i][:, NA_DH:], acc[hh, 1, i + 1][:, :NA_DH]], axis=1)
                    dst[128 * i:128 * (i + 1), ls] = (acc[hh, 0, i] + odd).T

        for c in range(S // _RC):
            sl = slice(c * _RC, (c + 1) * _RC)
            for x_ref, g_ref, dn, dx_ref, dg_ref in ((q_ref, qg_ref, dqn, dq_ref, dqg_ref), (k_ref, kg_ref, dkn, dk_ref, dkg_ref)):
                x = x_ref[sl, :]
                r_ = lax.rsqrt(_seg_mean(x * x, seg_ref) + EPS)
                xh = x * r_
                d = dn[sl, :]
                dxh = d * g_ref[...]
                mean = _seg_mean(dxh * xh, seg_ref)
                dx_ref[sl, :] = (r_ * (dxh - xh * mean)).astype(BF16)
                dg_ref[...] += jnp.sum(d * xh, axis=0, keepdims=True)
            dv_ref[sl, :] = dvs[sl, :].astype(BF16)

    LW = 128
    blk = pl.BlockSpec((S, LW), lambda j, e: (e, j))
    vec = pl.BlockSpec((1, LW), lambda j, e: (0, j))
    bsp = pl.BlockSpec((2, NA_ROWS, GRID_W, WIN), lambda j, e: (j, 0, 0, 0))
    return pl.pallas_call(
        body, out_shape=(jax.ShapeDtypeStruct((T, GROUP_W), BF16),) * 3 + (
            jax.ShapeDtypeStruct((NA_HEADS, NA_ROWS, GRID_W, WIN), F32),
            jax.ShapeDtypeStruct((1, GROUP_W), F32), jax.ShapeDtypeStruct((1, GROUP_W), F32)),
        grid=(GROUP_W // LW, E),
        in_specs=[pl.BlockSpec((S, LW), lambda j, e: (e, B_Q // LW + j)),
                  pl.BlockSpec((S, LW), lambda j, e: (e, B_K // LW + j)),
                  pl.BlockSpec((S, LW), lambda j, e: (e, B_V // LW + j)),
                  vec, vec, bsp, pl.BlockSpec((LW, LW), lambda j, e: (0, 0)), blk],
        out_specs=(blk, blk, blk, bsp, vec, vec),
        scratch_shapes=[pltpu.VMEM((2, S, NA_DH), BF16)] * 4 + [pltpu.VMEM((S, LW), F32)] * 3
        + [pltpu.VMEM((2, 2, S // 128 + 1, NA_DH, 128), F32)] * 2
        + [pltpu.VMEM((S, WIN), F32)] * 2 + [pltpu.VMEM((S, WIN), BF16)] * 2,
        name=name, compiler_params=_cp(("parallel", "arbitrary")))(z, z, z, qg, kg, bias, seg, do)


def _na_rpb_grad(dbias, name):
    e_np, _, _ = _na_onehots()
    H = NA_HEADS
    nro = 2 * NA_ROWS - 1

    def fold(x_ref, o_ref):
        for ro in range(nro):
            acc = None
            for d in range(NA_ROWS):
                kr = ro + d - (NA_ROWS - 1)
                if 0 <= kr < NA_ROWS:
                    blk = x_ref[0, d, :, kr * GRID_W:(kr + 1) * GRID_W]
                    acc = blk if acc is None else acc + blk
            o_ref[0, ro] = acc
        o_ref[0, nro] = jnp.zeros((GRID_W, GRID_W), F32)

    t = pl.pallas_call(
        fold, out_shape=jax.ShapeDtypeStruct((H, 16, GRID_W, GRID_W), F32), grid=(H,),
        in_specs=[pl.BlockSpec((1, NA_ROWS, GRID_W, NA_ROWS * GRID_W), lambda h: (h, 0, 0, 0))],
        out_specs=pl.BlockSpec((1, 16, GRID_W, GRID_W), lambda h: (h, 0, 0, 0)),
        name=name + "_fold", compiler_params=_cp(("parallel",)))(dbias)
    t = t.reshape(H, 16, GRID_W * GRID_W)

    def body(x_ref, e_ref, o_ref):
        o_ref[0] = jnp.dot(x_ref[0], e_ref[...], precision=HI, preferred_element_type=F32)

    out = pl.pallas_call(
        body, out_shape=jax.ShapeDtypeStruct((H, 16, 128), F32), grid=(H,),
        in_specs=[pl.BlockSpec((1, 16, GRID_W * GRID_W), lambda h: (h, 0, 0)),
                  pl.BlockSpec((GRID_W * GRID_W, 128), lambda h: (0, 0))],
        out_specs=pl.BlockSpec((1, 16, 128), lambda h: (h, 0, 0)),
        name=name, compiler_params=_cp(("parallel",)))(t, jnp.asarray(e_np))
    return out[:, :nro, :2 * NA_COLS - 1]


_HK = GLA_HEADS * GLA_DK
_HV = GLA_HEADS * GLA_DV


def _gla_consts(reverse):
    i = np.arange(CHUNK)
    tri = (i[:, None] <= i[None, :]) if reverse else (i[:, None] >= i[None, :])
    j = np.arange(_HK)
    oseg = (j[:, None] // GLA_DK == j[None, :] // GLA_DK)
    return (jnp.asarray(tri.astype(np.float32)), jnp.asarray(tri.T.astype(np.float32)), jnp.asarray(oseg.astype(np.float32), BF16))


def _log_decay(lr, a2, ab):
    zg = jnp.dot(lr, a2, precision=HI, preferred_element_type=F32) + ab
    g = (jnp.minimum(zg, 0.0) - jnp.log(1.0 + jnp.exp(-jnp.abs(zg)))) * (1.0 / GLA_TAU)
    return zg, g


def _dotf(a, b, dn):
    return lax.dot_general(a, b, dn, precision=HI, preferred_element_type=F32)


def _dotb(a, b, dn):
    return lax.dot_general(a.astype(BF16), b.astype(BF16), dn, preferred_element_type=F32)


_COLS = 4
_SUB = 16
_NSUB = CHUNK // _SUB


def _gla_cross_blocks(reverse):
    return range(0, _NSUB - 1) if reverse else range(1, _NSUB)


def _gla_cross_terms(s, reverse, b_s, q_s, k_s, oseg_ref):
    r0 = s * _SUB
    ref = r0 + (_SUB - 1 if reverse else 0)
    bref = b_s[ref:ref + 1, :]
    rowj = lax.broadcasted_iota(jnp.int32, (CHUNK, 1), 0)
    seen = (rowj >= r0 + _SUB) if reverse else (rowj < r0)
    ek = jnp.where(seen, jnp.exp(jnp.minimum(bref - b_s[...], 0.0)), 0.0)
    kt = k_s[...] * ek
    eq = jnp.exp(jnp.minimum(b_s[r0:r0 + _SUB, :] - bref, 0.0))
    qt = q_s[r0:r0 + _SUB, :] * eq
    nmat = jnp.concatenate([kt.astype(BF16)] * GLA_HEADS, axis=0) * oseg_ref[...]
    return qt, eq, kt, ek, nmat


_NN = (((1,), (0,)), ((), ()))
_NT = (((1,), (1,)), ((), ()))
_TN = (((0,), (0,)), ((), ()))


_DT = 256


def _gla_block_tri(reverse):
    i = np.arange(_DT)
    same = i[:, None] // CHUNK == i[None, :] // CHUNK
    tri = (i[:, None] <= i[None, :]) if reverse else (i[:, None] >= i[None, :])
    return (tri & same).astype(np.float32)


def _gla_decay_fwd(z, a2, ab, reverse, name):
    T = z.shape[0]
    nc = _DT // CHUNK

    def body(lr_ref, a2_ref, ab_ref, m_ref, b_ref, ec_ref):
        _, g = _log_decay(lr_ref[...], a2_ref[...], ab_ref[...])
        b_ref[...] = _dotf(m_ref[...], g, _NN)
        for c in range(nc):
            ec_ref[c] = jnp.exp(_dotf(g[c * CHUNK:(c + 1) * CHUNK, :], jnp.ones((CHUNK, GLA_DV), F32), _TN))

    return pl.pallas_call(
        body, out_shape=(jax.ShapeDtypeStruct((T, _HK), F32), jax.ShapeDtypeStruct((T // CHUNK, _HK, GLA_DV), F32)),
        grid=(T // _DT,),
        in_specs=[pl.BlockSpec((_DT, 128), lambda i: (i, LR_OFF // 128)),
                  pl.BlockSpec((128, _HK), lambda i: (0, 0)),
                  pl.BlockSpec((1, _HK), lambda i: (0, 0)),
                  pl.BlockSpec((_DT, _DT), lambda i: (0, 0))],
        out_specs=(pl.BlockSpec((_DT, _HK), lambda i: (i, 0)), pl.BlockSpec((nc, _HK, GLA_DV), lambda i: (i, 0, 0))),
        name=name, compiler_params=_cp(("parallel",)))(z, a2, ab, jnp.asarray(_gla_block_tri(reverse)))


def _gla_decay_bwd(z, a2_f, ab_f, a2_b, ab_b, db_f, db_b, name):
    T = z.shape[0]

    def body(lr_ref, a2f_ref, abf_ref, a2b_ref, abb_ref, mf_ref, mb_ref, dbf_ref, dbb_ref,
             dlr_ref, da2f_ref, dabf_ref, da2b_ref, dabb_ref):
        @pl.when(pl.program_id(0) == 0)
        def _():
            for r in (da2f_ref, dabf_ref, da2b_ref, dabb_ref):
                r[...] = jnp.zeros_like(r)

        lr = lr_ref[...]
        dlr = jnp.zeros((_DT, 128), F32)
        for a2_ref, ab_ref, mt_ref, db_ref, da2_ref, dab_ref in ((a2f_ref, abf_ref, mf_ref, dbf_ref, da2f_ref, dabf_ref),
                                                                 (a2b_ref, abb_ref, mb_ref, dbb_ref, da2b_ref, dabb_ref)):
            zg, _ = _log_decay(lr, a2_ref[...], ab_ref[...])
            dg = _dotf(mt_ref[...], db_ref[...], _NN)
            dzg = dg * (1.0 / (1.0 + jnp.exp(zg))) * (1.0 / GLA_TAU)
            dlr = dlr + _dotf(dzg, a2_ref[...], _NT)
            da2_ref[...] += _dotf(lr, dzg, _TN)
            dab_ref[...] += jnp.sum(dzg, axis=0, keepdims=True)
        dlr_ref[...] = dlr.astype(BF16)

    a2s = pl.BlockSpec((128, _HK), lambda i: (0, 0))
    abs_ = pl.BlockSpec((1, _HK), lambda i: (0, 0))
    ms = pl.BlockSpec((_DT, _DT), lambda i: (0, 0))
    row = pl.BlockSpec((_DT, _HK), lambda i: (i, 0))
    return pl.pallas_call(
        body, out_shape=(jax.ShapeDtypeStruct((T, 128), BF16), jax.ShapeDtypeStruct((128, _HK), F32), jax.ShapeDtypeStruct((1, _HK), F32),
                         jax.ShapeDtypeStruct((128, _HK), F32), jax.ShapeDtypeStruct((1, _HK), F32)),
        grid=(T // _DT,),
        in_specs=[pl.BlockSpec((_DT, 128), lambda i: (i, LR_OFF // 128)), a2s, abs_, a2s, abs_, ms, ms, row, row],
        out_specs=(pl.BlockSpec((_DT, 128), lambda i: (i, 0)), a2s, abs_, a2s, abs_),
        name=name, compiler_params=_cp(("arbitrary",)))(
            z, a2_f, ab_f, a2_b, ab_b, jnp.asarray(_gla_block_tri(False).T), jnp.asarray(_gla_block_tri(True).T), db_f, db_b)


def _gla_fwd(z, b_all, ecol, S, reverse, name):
    T = z.shape[0]
    E = T // S
    n = S // CHUNK
    _, _, oseg = _gla_consts(reverse)
    last = 0 if reverse else CHUNK - 1

    def body(q_ref, k_ref, v_ref, b_ref, ec_ref, oseg_ref, o_ref, a_ref, st_ref, st, b_s, q_s, k_s):
        @pl.when(pl.program_id(1) == 0)
        def _():
            st[...] = jnp.zeros_like(st)

        q = q_ref[...] * (GLA_DK ** -0.5)
        k = k_ref[...]
        v = v_ref[...]
        b = b_ref[...]
        bl_row = b_ref[last:last + 1, :]
        e_col = ec_ref[0]
        b_s[...] = b
        q_s[...] = q
        k_s[...] = k
        lane = lax.broadcasted_iota(jnp.int32, (1, _HK), 1) % GLA_DK

        rowi = lax.broadcasted_iota(jnp.int32, (CHUNK, 1), 0)
        blk0 = (rowi // _SUB) * _SUB

        def cols(jj, a):
            ts = []
            for u in range(_COLS):
                jp = jj * _COLS + u
                tiles = []
                for s in range(_NSUB):
                    rs_ = slice(s * _SUB, (s + 1) * _SUB)
                    bj = b_s[pl.ds(s * _SUB + jp, 1), :]
                    kj = k_s[pl.ds(s * _SUB + jp, 1), :]
                    tiles.append(q_s[rs_, :] * jnp.exp(jnp.minimum(b_s[rs_, :] - bj, 0.0)) * kj)
                ts.append(jnp.concatenate(tiles, axis=0).astype(BF16))
            r = jnp.dot(jnp.concatenate(ts, axis=0), oseg_ref[...], preferred_element_type=F32)
            for u in range(_COLS):
                a = jnp.where(lane == blk0 + (jj * _COLS + u), r[u * CHUNK:(u + 1) * CHUNK, :], a)
            return a

        a = lax.fori_loop(0, _SUB // _COLS, cols, jnp.zeros((CHUNK, _HK), F32))
        keep = (rowi <= lane) if reverse else (rowi >= lane)
        a = jnp.where(keep, a, 0.0)
        cross = []
        for s in range(_NSUB):
            if s in _gla_cross_blocks(reverse):
                qt, _, _, _, nmat = _gla_cross_terms(s, reverse, b_s, q_s, k_s, oseg_ref)
                cross.append(lax.dot_general(qt.astype(BF16), nmat, _NT, preferred_element_type=F32))
            else:
                cross.append(jnp.zeros((_SUB, _HK), F32))
        a = a + jnp.concatenate(cross, axis=0)
        a_ref[...] = a
        st_ref[0] = st[...]
        qb = q * jnp.exp(b)
        kd = k * jnp.exp(bl_row - b)
        for h in range(GLA_HEADS):
            ks_ = slice(h * GLA_DK, (h + 1) * GLA_DK)
            vs_ = slice(h * GLA_DV, (h + 1) * GLA_DV)
            s_h = st[ks_, :]
            o_ref[:, vs_] = _dotb(qb[:, ks_], s_h, _NN) + _dotb(a[:, ks_], v[:, vs_], _NN)
            st[ks_, :] = s_h * e_col[ks_, :] + _dotb(kd[:, ks_], v[:, vs_], _TN)

    def rowblk(e, c):
        return e * n + ((n - 1 - c) if reverse else c)

    return pl.pallas_call(
        body, out_shape=(jax.ShapeDtypeStruct((T, _HV), F32), jax.ShapeDtypeStruct((T, _HK), F32),
                         jax.ShapeDtypeStruct((T // CHUNK, _HK, GLA_DV), F32)),
        grid=(E, n),
        in_specs=[pl.BlockSpec((CHUNK, _HK), lambda e, c: (rowblk(e, c), C_Q // _HK)),
                  pl.BlockSpec((CHUNK, _HK), lambda e, c: (rowblk(e, c), C_K // _HK)),
                  pl.BlockSpec((CHUNK, _HV), lambda e, c: (rowblk(e, c), C_V // _HV)),
                  pl.BlockSpec((CHUNK, _HK), lambda e, c: (rowblk(e, c), 0)),
                  pl.BlockSpec((1, _HK, GLA_DV), lambda e, c: (rowblk(e, c), 0, 0)),
                  pl.BlockSpec((_HK, _HK), lambda e, c: (0, 0))],
        out_specs=(pl.BlockSpec((CHUNK, _HV), lambda e, c: (rowblk(e, c), 0)),
                   pl.BlockSpec((CHUNK, _HK), lambda e, c: (rowblk(e, c), 0)),
                   pl.BlockSpec((1, _HK, GLA_DV), lambda e, c: (rowblk(e, c), 0, 0))),
        scratch_shapes=[pltpu.VMEM((_HK, GLA_DV), F32)] + [pltpu.VMEM((CHUNK, _HK), F32)] * 3,
        name=name, compiler_params=_cp(("parallel", "arbitrary")))(z, z, z, b_all, ecol, oseg)


def _gla_fwd_both(z, b_f, ec_f, b_b, ec_b, S, name):
    T = z.shape[0]
    E = T // S
    n = S // CHUNK
    _, _, oseg = _gla_consts(False)

    def body(*refs):
        oseg_ref = refs[10]
        dirs = []
        for di, reverse in enumerate((False, True)):
            q_ref, k_ref, v_ref, b_ref, ec_ref = refs[5 * di:5 * di + 5]
            o_ref, a_ref, st_ref = refs[11 + 3 * di:14 + 3 * di]
            st, b_s, q_s, k_s = refs[17 + 4 * di:21 + 4 * di]
            dirs.append((reverse, q_ref, k_ref, v_ref, b_ref, ec_ref, o_ref, a_ref, st_ref, st, b_s, q_s, k_s))

        @pl.when(pl.program_id(1) == 0)
        def _():
            for d in dirs:
                d[9][...] = jnp.zeros_like(d[9])

        lane = lax.broadcasted_iota(jnp.int32, (1, _HK), 1) % GLA_DK
        rowi = lax.broadcasted_iota(jnp.int32, (CHUNK, 1), 0)
        blk0 = (rowi // _SUB) * _SUB
        for (_, q_ref, k_ref, _, b_ref, _, _, _, _, _, b_s, q_s, k_s) in dirs:
            b_s[...] = b_ref[...]
            q_s[...] = q_ref[...] * (GLA_DK ** -0.5)
            k_s[...] = k_ref[...]

        def cols(d, jj, a):
            b_s, q_s, k_s = d[10], d[11], d[12]
            ts = []
            for u in range(_COLS):
                jp = jj * _COLS + u
                tiles = []
                for s in range(_NSUB):
                    rs_ = slice(s * _SUB, (s + 1) * _SUB)
                    bj = b_s[pl.ds(s * _SUB + jp, 1), :]
                    kj = k_s[pl.ds(s * _SUB + jp, 1), :]
                    tiles.append(q_s[rs_, :] * jnp.exp(jnp.minimum(b_s[rs_, :] - bj, 0.0)) * kj)
                ts.append(jnp.concatenate(tiles, axis=0).astype(BF16))
            r = jnp.dot(jnp.concatenate(ts, axis=0), oseg_ref[...], preferred_element_type=F32)
            for u in range(_COLS):
                a = jnp.where(lane == blk0 + (jj * _COLS + u), r[u * CHUNK:(u + 1) * CHUNK, :], a)
            return a

        zero = jnp.zeros((CHUNK, _HK), F32)
        acc = lax.fori_loop(0, _SUB // _COLS, lambda jj, c: tuple(cols(d, jj, a) for d, a in zip(dirs, c)), (zero, zero),
                            unroll=True)

        for (reverse, _, _, v_ref, b_ref, ec_ref, o_ref, a_ref, st_ref, st, b_s, q_s, k_s), a in zip(dirs, acc):
            last = 0 if reverse else CHUNK - 1
            keep = (rowi <= lane) if reverse else (rowi >= lane)
            a = jnp.where(keep, a, 0.0)
            cross = []
            for s in range(_NSUB):
                if s in _gla_cross_blocks(reverse):
                    qt, _, _, _, nmat = _gla_cross_terms(s, reverse, b_s, q_s, k_s, oseg_ref)
                    cross.append(lax.dot_general(qt.astype(BF16), nmat, _NT, preferred_element_type=F32))
                else:
                    cross.append(jnp.zeros((_SUB, _HK), F32))
            a = a + jnp.concatenate(cross, axis=0)
            a_ref[...] = a
            st_ref[0] = st[...]
            b = b_s[...]
            v = v_ref[...]
            e_col = ec_ref[0]
            qb = q_s[...] * jnp.exp(b)
            kd = k_s[...] * jnp.exp(b_ref[last:last + 1, :] - b)
            for h in range(GLA_HEADS):
                ks_ = slice(h * GLA_DK, (h + 1) * GLA_DK)
                vs_ = slice(h * GLA_DV, (h + 1) * GLA_DV)
                s_h = st[ks_, :]
                o_ref[:, vs_] = _dotb(qb[:, ks_], s_h, _NN) + _dotb(a[:, ks_], v[:, vs_], _NN)
                st[ks_, :] = s_h * e_col[ks_, :] + _dotb(kd[:, ks_], v[:, vs_], _TN)

    def specs(reverse):
        rb = (lambda e, c: e * n + (n - 1 - c)) if reverse else (lambda e, c: e * n + c)
        ins = [pl.BlockSpec((CHUNK, _HK), lambda e, c: (rb(e, c), C_Q // _HK)),
               pl.BlockSpec((CHUNK, _HK), lambda e, c: (rb(e, c), C_K // _HK)),
               pl.BlockSpec((CHUNK, _HV), lambda e, c: (rb(e, c), C_V // _HV)),
               pl.BlockSpec((CHUNK, _HK), lambda e, c: (rb(e, c), 0)),
               pl.BlockSpec((1, _HK, GLA_DV), lambda e, c: (rb(e, c), 0, 0))]
        outs = [pl.BlockSpec((CHUNK, _HV), lambda e, c: (rb(e, c), 0)),
                pl.BlockSpec((CHUNK, _HK), lambda e, c: (rb(e, c), 0)),
                pl.BlockSpec((1, _HK, GLA_DV), lambda e, c: (rb(e, c), 0, 0))]
        return ins, outs

    in_f, out_f = specs(False)
    in_b, out_b = specs(True)
    shapes = (jax.ShapeDtypeStruct((T, _HV), F32), jax.ShapeDtypeStruct((T, _HK), F32),
              jax.ShapeDtypeStruct((T // CHUNK, _HK, GLA_DV), F32))
    res = pl.pallas_call(
        body, out_shape=shapes + shapes, grid=(E, n),
        in_specs=in_f + in_b + [pl.BlockSpec((_HK, _HK), lambda e, c: (0, 0))], out_specs=tuple(out_f + out_b),
        scratch_shapes=([pltpu.VMEM((_HK, GLA_DV), F32)] + [pltpu.VMEM((CHUNK, _HK), F32)] * 3) * 2,
        name=name, compiler_params=_cp(("parallel", "arbitrary")))(z, z, z, b_f, ec_f, z, z, z, b_b, ec_b, oseg)
    return res[:3], res[3:]


def _gla_bwd(z, b_all, ecol, att, states, do, prev, S, reverse, name):
    T = z.shape[0]
    E = T // S
    n = S // CHUNK
    _, _, oseg = _gla_consts(reverse)
    has_prev = prev is not None
    odt = BF16 if has_prev else F32
    last = 0 if reverse else CHUNK - 1

    def body(*refs):
        (q_ref, k_ref, v_ref, b_ref, ec_ref, oseg_ref, att_ref, st_ref, do_ref) = refs[:9]
        refs = refs[9:]
        if has_prev:
            pq_ref, pk_ref, pv_ref = refs[:3]
            refs = refs[3:]
        (dq_ref, dk_ref, dv_ref, db_ref, dst, b_s, q_s, k_s, da_s, dqb_s, dkd_s, dk3_s, dbn_s, dsp_s) = refs

        @pl.when(pl.program_id(1) == 0)
        def _():
            dst[...] = jnp.zeros_like(dst)

        q = q_ref[...] * (GLA_DK ** -0.5)
        k = k_ref[...]
        v = v_ref[...]
        b = b_ref[...]
        bl_row = b_ref[last:last + 1, :]
        eb = jnp.exp(b)
        ekd = jnp.exp(bl_row - b)
        qb = q * eb
        kd = k * ekd
        b_s[...] = b
        q_s[...] = q
        k_s[...] = k
        att = att_ref[...]
        s_all = st_ref[0]
        dsn = dst[...]
        e_col = ec_ref[0]
        do = do_ref[...]
        lane = lax.broadcasted_iota(jnp.int32, (1, _HK), 1) % GLA_DK
        rowi = lax.broadcasted_iota(jnp.int32, (CHUNK, 1), 0)
        keep = (rowi <= lane) if reverse else (rowi >= lane)
        for h in range(GLA_HEADS):
            ks_ = slice(h * GLA_DK, (h + 1) * GLA_DK)
            vs_ = slice(h * GLA_DV, (h + 1) * GLA_DV)
            do_h = do[:, vs_]
            s_h = s_all[ks_, :]
            dsn_h = dsn[ks_, :]
            dqb_s[:, ks_] = _dotb(do_h, s_h, _NT)
            dsp_s[ks_, :] = _dotb(qb[:, ks_], do_h, _TN) + dsn_h * e_col[ks_, :]
            da_s[:, ks_] = _dotb(do_h, v[:, vs_], _NT)
            dv_h = _dotb(att[:, ks_], do_h, _TN) + _dotb(kd[:, ks_], dsn_h, _NN)
            if has_prev:
                dv_h = dv_h + pv_ref[:, vs_]
            dv_ref[:, vs_] = dv_h.astype(odt)
            dkd_s[:, ks_] = _dotb(v[:, vs_], dsn_h, _NT)
        da_s[...] = jnp.where(keep, da_s[...], 0.0)
        dqb = dqb_s[...]
        dkd = dkd_s[...]
        x = dsn * s_all * e_col
        dbl_row = _dotf(jnp.ones((8, GLA_DV), F32), x, _NT)[0:1, :] + jnp.sum(dkd * kd, axis=0, keepdims=True)

        blk0 = (rowi // _SUB) * _SUB

        def cols(jj, carry):
            dq3, db3 = list(carry[:_NSUB]), list(carry[_NSUB:])
            sel = [jnp.where(lane == blk0 + (jj * _COLS + u), da_s[...], 0.0).astype(BF16) for u in range(_COLS)]
            dcols = jnp.dot(jnp.concatenate(sel, axis=0), oseg_ref[...], preferred_element_type=F32)
            for u in range(_COLS):
                jp = jj * _COLS + u
                for s in range(_NSUB):
                    rs_ = slice(s * _SUB, (s + 1) * _SUB)
                    bj = b_s[pl.ds(s * _SUB + jp, 1), :]
                    kj = k_s[pl.ds(s * _SUB + jp, 1), :]
                    tm_ = dcols[u * CHUNK + s * _SUB:u * CHUNK + (s + 1) * _SUB, :] * jnp.exp(jnp.minimum(b_s[rs_, :] - bj, 0.0))
                    dq3[s] = dq3[s] + tm_ * kj
                    gq = tm_ * q_s[rs_, :]
                    dk3_s[pl.ds(s * _SUB + jp, 1), :] = jnp.sum(gq, axis=0, keepdims=True)
                    w = gq * kj
                    dbn_s[pl.ds(s * _SUB + jp, 1), :] = jnp.sum(w, axis=0, keepdims=True)
                    db3[s] = db3[s] + w
            return tuple(dq3) + tuple(db3)

        zero = jnp.zeros((_SUB, _HK), F32)
        acc = lax.fori_loop(0, _SUB // _COLS, cols, (zero,) * (2 * _NSUB))
        dq3 = jnp.concatenate(acc[:_NSUB], axis=0)
        db3 = jnp.concatenate(acc[_NSUB:], axis=0)
        head = lax.broadcasted_iota(jnp.int32, (1, _HK), 1) // GLA_DK
        dq_x, db_x = [], []
        dk_x = jnp.zeros((CHUNK, _HK), F32)
        db_k = jnp.zeros((CHUNK, _HK), F32)
        for s in range(_NSUB):
            if s not in _gla_cross_blocks(reverse):
                dq_x.append(zero)
                db_x.append(zero)
                continue
            r0 = s * _SUB
            qt, eq, kt, ek, nmat = _gla_cross_terms(s, reverse, b_s, q_s, k_s, oseg_ref)
            seen = (lane >= r0 + _SUB) if reverse else (lane < r0)
            dax = jnp.where(seen, da_s[r0:r0 + _SUB, :], 0.0).astype(BF16)
            dqt = jnp.dot(dax, nmat, preferred_element_type=F32)
            full = lax.dot_general(dax, qt.astype(BF16), _TN, preferred_element_type=F32)
            dkt = full[0:CHUNK, :]
            for h in range(1, GLA_HEADS):
                dkt = jnp.where(head == h, full[h * CHUNK:(h + 1) * CHUNK, :], dkt)
            dq_x.append(dqt * eq)
            db_x.append(dqt * qt)
            dk_x = dk_x + dkt * ek
            db_k = db_k + dkt * kt
        dq = (dqb * eb + dq3 + jnp.concatenate(dq_x, axis=0)) * (GLA_DK ** -0.5)
        dk = dkd * ekd + dk3_s[...] + dk_x
        db = dqb * qb - dkd * kd + db3 - dbn_s[...] + jnp.concatenate(db_x, axis=0) - db_k
        db_ref[...] = jnp.where(rowi == last, db + dbl_row, db)
        if has_prev:
            dq = dq + pq_ref[...]
            dk = dk + pk_ref[...]
        dq_ref[...] = dq.astype(odt)
        dk_ref[...] = dk.astype(odt)
        dst[...] = dsp_s[...]

    def rowblk(e, c):
        return e * n + (c if reverse else (n - 1 - c))

    hk = pl.BlockSpec((CHUNK, _HK), lambda e, c: (rowblk(e, c), 0))
    hv = pl.BlockSpec((CHUNK, _HV), lambda e, c: (rowblk(e, c), 0))
    stb = pl.BlockSpec((1, _HK, GLA_DV), lambda e, c: (rowblk(e, c), 0, 0))
    in_specs = [pl.BlockSpec((CHUNK, _HK), lambda e, c: (rowblk(e, c), C_Q // _HK)),
                pl.BlockSpec((CHUNK, _HK), lambda e, c: (rowblk(e, c), C_K // _HK)),
                pl.BlockSpec((CHUNK, _HV), lambda e, c: (rowblk(e, c), C_V // _HV)),
                hk, stb, pl.BlockSpec((_HK, _HK), lambda e, c: (0, 0)), hk, stb, hv]
    args = [z, z, z, b_all, ecol, oseg, att, states, do]
    if has_prev:
        in_specs += [hk, hk, hv]
        args += list(prev)
    return pl.pallas_call(
        body, out_shape=(jax.ShapeDtypeStruct((T, _HK), odt), jax.ShapeDtypeStruct((T, _HK), odt),
                         jax.ShapeDtypeStruct((T, _HV), odt), jax.ShapeDtypeStruct((T, _HK), F32)),
        grid=(E, n), in_specs=in_specs, out_specs=(hk, hk, hv, hk),
        scratch_shapes=[pltpu.VMEM((_HK, GLA_DV), F32)] + [pltpu.VMEM((CHUNK, _HK), F32)] * 8 + [pltpu.VMEM((_HK, GLA_DV), F32)],
        name=name, compiler_params=_cp(("parallel", "arbitrary")))(*args)


def _gla_bwd_both(z, fwd_saved, rev_saved, do, S, name):
    T = z.shape[0]
    E = T // S
    n = S // CHUNK
    _, _, oseg = _gla_consts(False)
    NI, NO, NS = 8, 4, 10

    def body(*refs):
        oseg_ref = refs[2 * NI]
        dirs = []
        for di, reverse in enumerate((False, True)):
            ins = refs[NI * di:NI * (di + 1)]
            outs = refs[2 * NI + 1 + NO * di:2 * NI + 1 + NO * (di + 1)]
            scr = refs[2 * NI + 1 + 2 * NO + NS * di:2 * NI + 1 + 2 * NO + NS * (di + 1)]
            dirs.append((reverse, ins, outs, scr))

        @pl.when(pl.program_id(1) == 0)
        def _():
            for d in dirs:
                d[3][0][...] = jnp.zeros_like(d[3][0])

        lane = lax.broadcasted_iota(jnp.int32, (1, _HK), 1) % GLA_DK
        head = lax.broadcasted_iota(jnp.int32, (1, _HK), 1) // GLA_DK
        rowi = lax.broadcasted_iota(jnp.int32, (CHUNK, 1), 0)
        blk0 = (rowi // _SUB) * _SUB

        def factors(reverse, b_ref, b_s, q_s, k_s):
            last = 0 if reverse else CHUNK - 1
            b = b_s[...]
            eb = jnp.exp(b)
            ekd = jnp.exp(b_ref[last:last + 1, :] - b)
            return eb, ekd, q_s[...] * eb, k_s[...] * ekd

        dbl_rows = []
        for reverse, (q_ref, k_ref, v_ref, b_ref, ec_ref, att_ref, st_ref, do_ref), (dq_ref, dk_ref, dv_ref, db_ref), \
                (dst, b_s, q_s, k_s, da_s, dqb_s, dkd_s, dk3_s, dbn_s, dsp_s) in dirs:
            b_s[...] = b_ref[...]
            q_s[...] = q_ref[...] * (GLA_DK ** -0.5)
            k_s[...] = k_ref[...]
            _, _, qb, kd = factors(reverse, b_ref, b_s, q_s, k_s)
            v = v_ref[...]
            att = att_ref[...]
            s_all = st_ref[0]
            dsn = dst[...]
            e_col = ec_ref[0]
            do = do_ref[...]
            keep = (rowi <= lane) if reverse else (rowi >= lane)
            for h in range(GLA_HEADS):
                ks_ = slice(h * GLA_DK, (h + 1) * GLA_DK)
                vs_ = slice(h * GLA_DV, (h + 1) * GLA_DV)
                do_h = do[:, vs_]
                s_h = s_all[ks_, :]
                dsn_h = dsn[ks_, :]
                dqb_s[:, ks_] = _dotb(do_h, s_h, _NT)
                dsp_s[ks_, :] = _dotb(qb[:, ks_], do_h, _TN) + dsn_h * e_col[ks_, :]
                da_s[:, ks_] = _dotb(do_h, v[:, vs_], _NT)
                dv_ref[:, vs_] = _dotb(att[:, ks_], do_h, _TN) + _dotb(kd[:, ks_], dsn_h, _NN)
                dkd_s[:, ks_] = _dotb(v[:, vs_], dsn_h, _NT)
            da_s[...] = jnp.where(keep, da_s[...], 0.0)
            x = dsn * s_all * e_col
            dbl_rows.append(_dotf(jnp.ones((8, GLA_DV), F32), x, _NT)[0:1, :] + jnp.sum(dkd_s[...] * kd, axis=0, keepdims=True))

        def cols(d, jj, carry):
            _, b_s, q_s, k_s, da_s, _, _, dk3_s, dbn_s, _ = d[3]
            dq3, db3 = list(carry[:_NSUB]), list(carry[_NSUB:])
            sel = [jnp.where(lane == blk0 + (jj * _COLS + u), da_s[...], 0.0).astype(BF16) for u in range(_COLS)]
            dcols = jnp.dot(jnp.concatenate(sel, axis=0), oseg_ref[...], preferred_element_type=F32)
            for u in range(_COLS):
                jp = jj * _COLS + u
                for s in range(_NSUB):
                    rs_ = slice(s * _SUB, (s + 1) * _SUB)
                    bj = b_s[pl.ds(s * _SUB + jp, 1), :]
                    kj = k_s[pl.ds(s * _SUB + jp, 1), :]
                    tm_ = dcols[u * CHUNK + s * _SUB:u * CHUNK + (s + 1) * _SUB, :] * jnp.exp(jnp.minimum(b_s[rs_, :] - bj, 0.0))
                    dq3[s] = dq3[s] + tm_ * kj
                    gq = tm_ * q_s[rs_, :]
                    dk3_s[pl.ds(s * _SUB + jp, 1), :] = jnp.sum(gq, axis=0, keepdims=True)
                    w = gq * kj
                    dbn_s[pl.ds(s * _SUB + jp, 1), :] = jnp.sum(w, axis=0, keepdims=True)
                    db3[s] = db3[s] + w
            return tuple(dq3) + tuple(db3)

        zero = jnp.zeros((_SUB, _HK), F32)
        init = (zero,) * (2 * _NSUB)
        accs = lax.fori_loop(0, _SUB // _COLS, lambda jj, c: tuple(cols(d, jj, a) for d, a in zip(dirs, c)), (init, init),
                             unroll=True)

        for (reverse, ins, (dq_ref, dk_ref, dv_ref, db_ref), (dst, b_s, q_s, k_s, da_s, dqb_s, dkd_s, dk3_s, dbn_s, dsp_s)), \
                acc, dbl_row in zip(dirs, accs, dbl_rows):
            last = 0 if reverse else CHUNK - 1
            eb, ekd, qb, kd = factors(reverse, ins[3], b_s, q_s, k_s)
            dqb = dqb_s[...]
            dkd = dkd_s[...]
            dq3 = jnp.concatenate(acc[:_NSUB], axis=0)
            db3 = jnp.concatenate(acc[_NSUB:], axis=0)
            dq_x, db_x = [], []
            dk_x = jnp.zeros((CHUNK, _HK), F32)
            db_k = jnp.zeros((CHUNK, _HK), F32)
            for s in range(_NSUB):
                if s not in _gla_cross_blocks(reverse):
                    dq_x.append(zero)
                    db_x.append(zero)
                    continue
                r0 = s * _SUB
                qt, eq, kt, ek, nmat = _gla_cross_terms(s, reverse, b_s, q_s, k_s, oseg_ref)
                seen = (lane >= r0 + _SUB) if reverse else (lane < r0)
                dax = jnp.where(seen, da_s[r0:r0 + _SUB, :], 0.0).astype(BF16)
                dqt = jnp.dot(dax, nmat, preferred_element_type=F32)
                full = lax.dot_general(dax, qt.astype(BF16), _TN, preferred_element_type=F32)
                dkt = full[0:CHUNK, :]
                for h in range(1, GLA_HEADS):
                    dkt = jnp.where(head == h, full[h * CHUNK:(h + 1) * CHUNK, :], dkt)
                dq_x.append(dqt * eq)
                db_x.append(dqt * qt)
                dk_x = dk_x + dkt * ek
                db_k = db_k + dkt * kt
            dq_ref[...] = (dqb * eb + dq3 + jnp.concatenate(dq_x, axis=0)) * (GLA_DK ** -0.5)
            dk_ref[...] = dkd * ekd + dk3_s[...] + dk_x
            db = dqb * qb - dkd * kd + db3 - dbn_s[...] + jnp.concatenate(db_x, axis=0) - db_k
            db_ref[...] = jnp.where(rowi == last, db + dbl_row, db)
            dst[...] = dsp_s[...]

    def specs(reverse):
        rb = (lambda e, c: e * n + c) if reverse else (lambda e, c: e * n + (n - 1 - c))
        hk = pl.BlockSpec((CHUNK, _HK), lambda e, c: (rb(e, c), 0))
        hv = pl.BlockSpec((CHUNK, _HV), lambda e, c: (rb(e, c), 0))
        stb = pl.BlockSpec((1, _HK, GLA_DV), lambda e, c: (rb(e, c), 0, 0))
        ins = [pl.BlockSpec((CHUNK, _HK), lambda e, c: (rb(e, c), C_Q // _HK)),
               pl.BlockSpec((CHUNK, _HK), lambda e, c: (rb(e, c), C_K // _HK)),
               pl.BlockSpec((CHUNK, _HV), lambda e, c: (rb(e, c), C_V // _HV)),
               hk, stb, hk, stb, hv]
        return ins, [hk, hk, hv, hk]

    in_f, out_f = specs(False)
    in_b, out_b = specs(True)
    shapes = (jax.ShapeDtypeStruct((T, _HK), F32), jax.ShapeDtypeStruct((T, _HK), F32),
              jax.ShapeDtypeStruct((T, _HV), F32), jax.ShapeDtypeStruct((T, _HK), F32))
    scratch = [pltpu.VMEM((_HK, GLA_DV), F32)] + [pltpu.VMEM((CHUNK, _HK), F32)] * 8 + [pltpu.VMEM((_HK, GLA_DV), F32)]
    res = pl.pallas_call(
        body, out_shape=shapes + shapes, grid=(E, n),
        in_specs=in_f + in_b + [pl.BlockSpec((_HK, _HK), lambda e, c: (0, 0))], out_specs=tuple(out_f + out_b),
        scratch_shapes=scratch * 2, name=name, compiler_params=_cp(("parallel", "arbitrary")))(
            z, z, z, *fwd_saved, do, z, z, z, *rev_saved, do, oseg)
    return res[:4], res[4:]


def _gla_norm_fwd(of, ob, og, name):
    T = of.shape[0]
    tm = 256

    def body(f_ref, b_ref, g_ref, o_ref):
        for h in range(GLA_HEADS):
            vs_ = slice(h * GLA_DV, (h + 1) * GLA_DV)
            o = f_ref[:, vs_] + b_ref[:, vs_]
            o_ref[:, vs_] = o * lax.rsqrt(jnp.mean(o * o, axis=-1, keepdims=True) + EPS) * g_ref[:, vs_]

    row = pl.BlockSpec((tm, _HV), lambda i: (i, 0))
    vec = pl.BlockSpec((1, _HV), lambda i: (0, 0))
    return pl.pallas_call(body, out_shape=jax.ShapeDtypeStruct((T, _HV), F32), grid=(T // tm,),
                          in_specs=[row, row, vec], out_specs=row, name=name, compiler_params=_cp(("parallel",)))(of, ob, og)


def _gla_norm_bwd(of, ob, og, dpre, name):
    T = of.shape[0]
    tm = 256

    def body(f_ref, b_ref, g_ref, dp_ref, do_ref, dg_ref):
        @pl.when(pl.program_id(0) == 0)
        def _():
            dg_ref[...] = jnp.zeros_like(dg_ref)

        for h in range(GLA_HEADS):
            vs_ = slice(h * GLA_DV, (h + 1) * GLA_DV)
            o = f_ref[:, vs_] + b_ref[:, vs_]
            r = lax.rsqrt(jnp.mean(o * o, axis=-1, keepdims=True) + EPS)
            xh = o * r
            dp = dp_ref[:, vs_]
            dxh = dp * g_ref[:, vs_]
            do_ref[:, vs_] = r * (dxh - xh * jnp.mean(dxh * xh, axis=-1, keepdims=True))
            dg_ref[:, vs_] += jnp.sum(dp * xh, axis=0, keepdims=True)

    row = pl.BlockSpec((tm, _HV), lambda i: (i, 0))
    vec = pl.BlockSpec((1, _HV), lambda i: (0, 0))
    return pl.pallas_call(
        body, out_shape=(jax.ShapeDtypeStruct((T, _HV), F32), jax.ShapeDtypeStruct((1, _HV), F32)), grid=(T // tm,),
        in_specs=[row, row, vec, row], out_specs=(row, vec), name=name, compiler_params=_cp(("arbitrary",)))(of, ob, og, dpre)


_ANY = pl.BlockSpec(memory_space=pl.ANY)


def _coords():
    return lax.axis_index("x"), lax.axis_index("y"), lax.axis_index("c")


def _other_chips(x, y):
    return ((1 - x, y), (x, 1 - y), (1 - x, 1 - y))


def _gather_weights(arrays, chunks, name):
    n = len(arrays)
    pieces = []
    for k in range(max(chunks)):
        for i, a in enumerate(arrays):
            if k < chunks[i]:
                rc = a.shape[1] // chunks[i]
                pieces.append((i, k * rc, rc))
    m = len(pieces)

    def body(*refs):
        srcs, dsts = refs[:n], refs[n:2 * n]
        send_sems, recv_sems, local_sems = refs[2 * n:]
        x, y, c = _coords()
        me = 2 * x + y
        loc = [pltpu.make_async_copy(s, d.at[me], local_sems.at[i]) for i, (s, d) in enumerate(zip(srcs, dsts))]
        for cp in loc:
            cp.start()
        ici = []
        for p, (i, r0, rc) in enumerate(pieces):
            for j, (px, py) in enumerate(_other_chips(x, y)):
                ici.append(pltpu.make_async_remote_copy(
                    src_ref=srcs[i].at[c, pl.ds(r0, rc)], dst_ref=dsts[i].at[me, c, pl.ds(r0, rc)],
                    send_sem=send_sems.at[3 * p + j], recv_sem=recv_sems.at[3 * p + j],
                    device_id=(px, py, c), device_id_type=MESH))
        for cp in ici:
            cp.start()
        fwd = []
        for p, (i, r0, rc) in enumerate(pieces):
            for j, (px, py) in enumerate(_other_chips(x, y)):
                ici[3 * p + j].wait_recv()
                part = dsts[i].at[2 * px + py, c, pl.ds(r0, rc)]
                cp = pltpu.make_async_remote_copy(
                    src_ref=part, dst_ref=part, send_sem=send_sems.at[3 * m + 3 * p + j], recv_sem=recv_sems.at[3 * m + 3 * p + j],
                    device_id=(x, y, 1 - c), device_id_type=MESH)
                cp.start()
                fwd.append(cp)
        for cp in fwd:
            cp.wait_recv()
        for cp in ici + fwd:
            cp.wait_send()
        for cp in loc:
            cp.wait()

    return pl.pallas_call(
        body, out_shape=tuple(jax.ShapeDtypeStruct((4,) + a.shape, a.dtype) for a in arrays),
        in_specs=[_ANY] * n, out_specs=(_ANY,) * n,
        scratch_shapes=[pltpu.SemaphoreType.DMA((6 * m,)), pltpu.SemaphoreType.DMA((6 * m,)), pltpu.SemaphoreType.DMA((n,))],
        name=name)(*arrays)


def _sibling_exchange(layered, whole, name):
    nl, n = len(layered), len(layered) + len(whole)

    def body(*refs):
        srcs, dsts = refs[:n], refs[n:2 * n]
        send_sems, recv_sems = refs[2 * n:]
        x, y, c = _coords()
        rem = [pltpu.make_async_remote_copy(src_ref=(s.at[1 - c] if i < nl else s), dst_ref=d, send_sem=send_sems.at[i],
                                            recv_sem=recv_sems.at[i], device_id=(x, y, 1 - c), device_id_type=MESH)
               for i, (s, d) in enumerate(zip(srcs, dsts))]
        for cp in rem:
            cp.start()
        for cp in rem:
            cp.wait()

    outs = [jax.ShapeDtypeStruct(a.shape[1:], a.dtype) for a in layered] + [jax.ShapeDtypeStruct(a.shape, a.dtype) for a in whole]
    return pl.pallas_call(
        body, out_shape=tuple(outs), in_specs=[_ANY] * n, out_specs=(_ANY,) * n,
        scratch_shapes=[pltpu.SemaphoreType.DMA((n,)), pltpu.SemaphoreType.DMA((n,))], name=name)(*layered, *whole)


def _chip_exchange(scatter, bcast, name):
    ns, n = len(scatter), len(scatter) + len(bcast)

    def body(*refs):
        srcs, dsts = refs[:n], refs[n:2 * n]
        send_sems, recv_sems, local_sems = refs[2 * n:]
        x, y, c = _coords()
        me = 2 * x + y
        loc = [pltpu.make_async_copy((s.at[me] if i < ns else s), d.at[me], local_sems.at[i])
               for i, (s, d) in enumerate(zip(srcs, dsts))]
        for cp in loc:
            cp.start()
        rem = []
        for j, (px, py) in enumerate(_other_chips(x, y)):
            for i, (s, d) in enumerate(zip(srcs, dsts)):
                rem.append(pltpu.make_async_remote_copy(
                    src_ref=(s.at[2 * px + py] if i < ns else s), dst_ref=d.at[me], send_sem=send_sems.at[n * j + i],
                    recv_sem=recv_sems.at[n * j + i], device_id=(px, py, c), device_id_type=MESH))
        for cp in rem:
            cp.start()
        for cp in rem:
            cp.wait()
        for cp in loc:
            cp.wait()

    outs = [jax.ShapeDtypeStruct(a.shape, a.dtype) for a in scatter] + [jax.ShapeDtypeStruct((4,) + a.shape, a.dtype) for a in bcast]
    return pl.pallas_call(
        body, out_shape=tuple(outs), in_specs=[_ANY] * n, out_specs=(_ANY,) * n,
        scratch_shapes=[pltpu.SemaphoreType.DMA((3 * n,)), pltpu.SemaphoreType.DMA((3 * n,)), pltpu.SemaphoreType.DMA((n,))],
        name=name)(*scatter, *bcast)


_EW_BLOCK_BYTES = 2 * 1024 * 1024


def _tile2d(R, C):
    if R % 256 == 0 and 256 * C * 4 <= _EW_BLOCK_BYTES:
        return 256, C
    bc = 256 if C % 256 == 0 else C
    for br in range(R, 0, -1):
        if R % br == 0 and (br % 8 == 0 or br == R) and br * bc * 4 <= _EW_BLOCK_BYTES:
            return br, bc
    return R, bc


def _sum_slots(r, name):
    n, R, C = r.shape
    br, bc = _tile2d(R, C)

    def body(r_ref, o_ref):
        acc = r_ref[0].astype(F32)
        for i in range(1, n):
            acc = acc + r_ref[i].astype(F32)
        o_ref[...] = acc

    return pl.pallas_call(body, out_shape=jax.ShapeDtypeStruct((R, C), F32), grid=(R // br, C // bc),
                          in_specs=[pl.BlockSpec((n, br, bc), lambda i, j: (0, i, j))],
                          out_specs=pl.BlockSpec((br, bc), lambda i, j: (i, j)),
                          name=name, compiler_params=_cp(("parallel", "parallel")))(r)


_SMEM = pl.BlockSpec(memory_space=pltpu.SMEM)


def _add2(a, b, out_dtype, name, pick=None):
    R, C = b.shape
    br, bc = _tile2d(R, C)
    blk = pl.BlockSpec((br, bc), lambda i, j: (i, j))
    if pick is None:
        def body(a_ref, b_ref, o_ref):
            o_ref[...] = (a_ref[...].astype(F32) + b_ref[...].astype(F32)).astype(out_dtype)
        in_specs, args = [blk, blk], (a, b)
    else:
        def body(c_ref, a_ref, b_ref, o_ref):
            av = jnp.where(c_ref[0] == 0, a_ref[0], a_ref[1])
            o_ref[...] = (av.astype(F32) + b_ref[...].astype(F32)).astype(out_dtype)
        in_specs = [_SMEM, pl.BlockSpec((2, br, bc), lambda i, j: (0, i, j)), blk]
        args = (pick.reshape(1).astype(jnp.int32), a, b)
    return pl.pallas_call(body, out_shape=jax.ShapeDtypeStruct((R, C), out_dtype), grid=(R // br, C // bc), in_specs=in_specs,
                          out_specs=blk, name=name, compiler_params=_cp(("parallel", "parallel")))(*args)


def _adamw_math(w, g, m, v):
    m = ADAM_B1 * m + (1.0 - ADAM_B1) * g
    v = ADAM_B2 * v + (1.0 - ADAM_B2) * (g * g)
    m_hat = m / (1.0 - ADAM_B1 ** ADAM_STEP)
    v_hat = v / (1.0 - ADAM_B2 ** ADAM_STEP)
    delta = -ADAM_LR * (m_hat / (jnp.sqrt(v_hat) + ADAM_EPS) + ADAM_WD * w)
    return delta, m, v


def _adamw(w, gs, m, v, name, pick=None):
    R, C = w.shape
    br, bc = _tile2d(R, C)
    blk = pl.BlockSpec((br, bc), lambda i, j: (i, j))
    if pick is None:
        g_specs = [pl.BlockSpec((g.shape[0], br, bc), lambda i, j: (0, i, j)) if g.ndim == 3 else blk for g in gs]
        lead = ()
    else:
        nb = (R // 2) // br
        assert nb * br * 2 == R
        g_specs = [pl.BlockSpec((br, bc), lambda i, j: (i % nb, j))] * 2
        lead = (pick.reshape(1).astype(jnp.int32),)

    def body(*refs):
        if pick is not None:
            c_ref, refs = refs[0], refs[1:]
        w_ref = refs[0]
        g_refs = refs[1:1 + len(gs)]
        m_ref, v_ref, g_out, d_out, m_out, v_out = refs[1 + len(gs):]
        if pick is None:
            g = None
            for gr in g_refs:
                parts = [gr[i] for i in range(gr.shape[0])] if len(gr.shape) == 3 else [gr[...]]
                for p in parts:
                    g = p if g is None else g + p
        else:
            g = jnp.where(pl.program_id(0) // nb == c_ref[0], g_refs[0][...], g_refs[1][...])
        d, mn, vn = _adamw_math(w_ref[...], g, m_ref[...], v_ref[...])
        g_out[...] = g
        d_out[...] = d
        m_out[...] = mn
        v_out[...] = vn

    return pl.pallas_call(
        body, out_shape=tuple(jax.ShapeDtypeStruct((R, C), F32) for _ in range(4)), grid=(R // br, C // bc),
        in_specs=[_SMEM] * len(lead) + [blk] + g_specs + [blk, blk], out_specs=(blk,) * 4, name=name,
        compiler_params=_cp(("parallel", "parallel")))(*lead, w, *gs, m, v)


WEIGHTS = ("norm_g", "w_in", "conv_w", "conv_b", "conv_ln_g", "conv_ln_b", "na_q_g", "na_k_g", "na_rpb", "gla_a2_f",
           "gla_ab_f", "gla_a2_b", "gla_ab_b", "gla_o_g", "pool_w", "pool_scale", "w_out")
_REPL = ("norm_g", "conv_b", "conv_ln_g", "conv_ln_b", "na_q_g", "na_k_g", "na_rpb", "gla_ab_f", "gla_ab_b", "gla_o_g",
         "pool_w", "pool_scale")
_SHARD_SMALL = ("conv_w", "gla_a2_f", "gla_a2_b")
_PACK_ROWS = 8 * 128


def _pack(arrs):
    flat = jnp.concatenate([a.reshape(-1) for a in arrs])
    n = -(-flat.shape[0] // _PACK_ROWS) * _PACK_ROWS
    return jnp.pad(flat, (0, n - flat.shape[0])).reshape(-1, 128)


def _unpack(p, shapes):
    flat = p.reshape(-1)
    out, o = [], 0
    for s in shapes:
        n = int(np.prod(s))
        out.append(flat[o:o + n].reshape(s))
        o += n
    return out


def _to_layout_rows(w):
    pad = jnp.zeros((w.shape[0], NZ - N_IN, w.shape[2]), w.dtype)
    return jnp.concatenate([w[:, :5120], w[:, 5152:6176], w[:, 5120:5152], pad], axis=1)


def _from_layout_rows(w):
    return jnp.concatenate([w[:, :5120], w[:, LR_OFF:LR_OFF + 32], w[:, 5120:LR_OFF]], axis=1)


def _reduce_gradients(p_a, p_b, small_g, ci):
    s_a, s_b, s_small = _sibling_exchange((p_a, p_b), (small_g,), "grad_to_sibling")
    flat = lambda a: a.reshape(a.shape[0], -1, a.shape[-1])
    c_a = _add2(flat(p_a), s_a.reshape(-1, s_a.shape[-1]), BF16, "chip_sum_a", pick=ci).reshape(s_a.shape)
    c_b = _add2(flat(p_b), s_b.reshape(-1, s_b.shape[-1]), BF16, "chip_sum_b", pick=ci).reshape(s_b.shape)
    c_small = _add2(small_g, s_small, F32, "chip_sum_small")
    r_a, r_b, r_small = _chip_exchange((c_a, c_b), (c_small,), "grad_to_owner")
    own_a = _sum_slots(r_a, "sum_a")
    own_b = _sum_slots(r_b, "sum_b")
    sib_a, sib_b = _sibling_exchange((), (own_a, own_b), "reduced_to_sibling")
    return (own_a, sib_a), (own_b, sib_b), r_small


def _layer_fwd(l, x, P, S, target=None):
    n = f"l{l}_"
    h = _rmsnorm_fwd(x, P["norm_g"], n + "rms_fwd")
    z = _matmul(h, P["w_in"], dims="nn", out_dtype=F32, tm=1024, tn=1280, tk=D_MODEL, name=n + "mm_z")
    yc = _conv_fwd(z, P["conv_w32"], P["conv_b"], S, n + "conv_fwd")
    pre_a = _ln_silu_fwd(yc, P["conv_ln_g"], P["conv_ln_b"], n + "ln_fwd")
    pre_b = _na_fwd(z, P["na_q_g"], P["na_k_g"], P["na_bias"], S, n + "na_fwd")
    bf, ecf = _gla_decay_fwd(z, P["a2_f"], P["gla_ab_f"], False, n + "gla_decay_f")
    bb, ecb = _gla_decay_fwd(z, P["a2_b"], P["gla_ab_b"], True, n + "gla_decay_b")
    (of, af, sf), (ob, ab, sb) = _gla_fwd_both(z, bf, ecf, bb, ecb, S, n + "gla_fwd")
    pre_c = _gla_norm_fwd(of, ob, P["gla_o_g"], n + "gla_norm_fwd")
    pre_d = _pool_fwd(z, P["pool_w_bf"], P["pool_scale"], S, n + "pool_fwd")
    pres = (pre_a, pre_b, pre_c, pre_d)
    res = _out_proj_fwd(pres, z, P["w_out"], x, target, n + "out_proj")
    y, out = res[0], (res[1] if target is None else res[1:])
    return out, dict(x=x, h=h, z=z, yc=yc, pres=pres, of=of, af=af, sf=sf, ob=ob, ab=ab, sb=sb, y=y, bf=bf, ecf=ecf, bb=bb, ecb=ecb)


def _layer_bwd(l, dout, dout_bf, sv, P, S):
    n = f"l{l}_"
    z = sv["z"]
    T = z.shape[0]
    d_w_out = _matmul(sv["y"], dout_bf, dims="tn", out_dtype=BF16, tm=1024, tn=2048, tk=512, name=n + "mm_dwout")
    dpa, dpb, dpc, dpd, dga, dgb, dgc, dgd = _out_proj_bwd(dout_bf, P["w_out_t"], sv["pres"], z, n + "out_proj_bwd")
    dyc, d_ln_g, d_ln_b = _ln_silu_bwd(sv["yc"], P["conv_ln_g"], P["conv_ln_b"], dpa, n + "ln_bwd")
    dval, dglu, d_cw, d_cb = _conv_bwd(z, P["conv_w32"], dyc, S, n + "conv_bwd")
    dq, dk, dv, dbias, d_qg, d_kg = _na_bwd(z, P["na_q_g"], P["na_k_g"], P["na_bias"], dpb, S, n + "na_bwd")
    d_rpb = _na_rpb_grad(dbias, n + "na_rpb")
    do, d_og = _gla_norm_bwd(sv["of"], sv["ob"], P["gla_o_g"], dpc, n + "gla_norm_bwd")
    gf, gb_ = _gla_bwd_both(z, (sv["bf"], sv["ecf"], sv["af"], sv["sf"]), (sv["bb"], sv["ecb"], sv["ab"], sv["sb"]), do, S,
                            n + "gla_bwd")
    dcq, dck, dcv = zip(gf[:3], gb_[:3])
    dlr, d_a2f, d_abf, d_a2b, d_abb = _gla_decay_bwd(z, P["a2_f"], P["gla_ab_f"], P["a2_b"], P["gla_ab_b"], gf[3], gb_[3],
                                                     n + "gla_decay_bwd")
    dd, d_pw, d_ps = _pool_bwd(z, P["pool_w_bf"], P["pool_scale"], dpd, S, n + "pool_bwd")
    dz = _concat_cols([dval, dglu, dga, dq, dk, dv, dgb, dcq, dck, dcv, dgc, dd, dgd, dlr], NZ, n + "dz_concat")
    dh = _matmul(dz, P["w_in_t"], dims="nn", out_dtype=F32, tm=1024, tn=1024, tk=3200, name=n + "mm_dh")
    d_w_in = _matmul(dz, sv["h"], dims="tn", out_dtype=BF16, tm=1280, tn=1024, tk=1024, name=n + "mm_dwin")
    dx, dx_bf, d_ng = _rmsnorm_bwd(sv["x"], P["norm_g"], dh, dout, n + "rms_bwd")
    grads = dict(norm_g=d_ng[0], w_in=d_w_in, conv_w=d_cw[:CONV_K], conv_b=d_cb[0], conv_ln_g=d_ln_g[0], conv_ln_b=d_ln_b[0],
                 na_q_g=d_qg.reshape(NA_HEADS, NA_DH), na_k_g=d_kg.reshape(NA_HEADS, NA_DH), na_rpb=d_rpb,
                 gla_a2_f=d_a2f[0:GLA_RANK], gla_ab_f=d_abf[0], gla_a2_b=d_a2b[GLA_RANK:2 * GLA_RANK], gla_ab_b=d_abb[0],
                 gla_o_g=d_og.reshape(GLA_HEADS, GLA_DV), pool_w=d_pw, pool_scale=d_ps[0], w_out=d_w_out)
    return dx, dx_bf, grads


def kernel(x, norm_g, w_in, conv_w, conv_b, conv_ln_g, conv_ln_b, na_q_g, na_k_g, na_rpb, gla_a2_f, gla_ab_f, gla_a2_b, gla_ab_b, gla_o_g, pool_w, pool_scale, w_out, loss_target, m_norm_g, m_w_in, m_conv_w, m_conv_b, m_conv_ln_g, m_conv_ln_b, m_na_q_g, m_na_k_g, m_na_rpb, m_gla_a2_f, m_gla_ab_f, m_gla_a2_b, m_gla_ab_b, m_gla_o_g, m_pool_w, m_pool_scale, m_w_out, v_norm_g, v_w_in, v_conv_w, v_conv_b, v_conv_ln_g, v_conv_ln_b, v_na_q_g, v_na_k_g, v_na_rpb, v_gla_a2_f, v_gla_ab_f, v_gla_a2_b, v_gla_ab_b, v_gla_o_g, v_pool_w, v_pool_scale, v_w_out):
    W = dict(norm_g=norm_g, w_in=w_in, conv_w=conv_w, conv_b=conv_b, conv_ln_g=conv_ln_g, conv_ln_b=conv_ln_b, na_q_g=na_q_g,
             na_k_g=na_k_g, na_rpb=na_rpb, gla_a2_f=gla_a2_f, gla_ab_f=gla_ab_f, gla_a2_b=gla_a2_b, gla_ab_b=gla_ab_b,
             gla_o_g=gla_o_g, pool_w=pool_w, pool_scale=pool_scale, w_out=w_out)
    M = dict(norm_g=m_norm_g, w_in=m_w_in, conv_w=m_conv_w, conv_b=m_conv_b, conv_ln_g=m_conv_ln_g, conv_ln_b=m_conv_ln_b,
             na_q_g=m_na_q_g, na_k_g=m_na_k_g, na_rpb=m_na_rpb, gla_a2_f=m_gla_a2_f, gla_ab_f=m_gla_ab_f, gla_a2_b=m_gla_a2_b,
             gla_ab_b=m_gla_ab_b, gla_o_g=m_gla_o_g, pool_w=m_pool_w, pool_scale=m_pool_scale, w_out=m_w_out)
    V = dict(norm_g=v_norm_g, w_in=v_w_in, conv_w=v_conv_w, conv_b=v_conv_b, conv_ln_g=v_conv_ln_g, conv_ln_b=v_conv_ln_b,
             na_q_g=v_na_q_g, na_k_g=v_na_k_g, na_rpb=v_na_rpb, gla_a2_f=v_gla_a2_f, gla_ab_f=v_gla_ab_f, gla_a2_b=v_gla_a2_b,
             gla_ab_b=v_gla_ab_b, gla_o_g=v_gla_o_g, pool_w=v_pool_w, pool_scale=v_pool_scale, w_out=v_w_out)
    E, S, D = x.shape
    T = E * S
    L = DEPTH
    xi, yi, ci = _coords()
    chip = 2 * xi + yi
    cw_sh, a2_sh = conv_w.shape[-1], gla_a2_f.shape[-1]

    small_sh = jnp.concatenate([
        jnp.pad(conv_w, ((0, 0), (0, 1), (0, 0))),
        jnp.pad(gla_a2_f, ((0, 0), (0, 0), (0, 128 - a2_sh))),
        jnp.pad(gla_a2_b, ((0, 0), (0, 0), (0, 128 - a2_sh)))], axis=1)
    w_in_tr, m_w_in_tr, v_w_in_tr = (jnp.transpose(a, (0, 2, 1)) for a in (w_in, m_w_in, v_w_in))
    g_win, g_wout, g_small = _gather_weights((w_in_tr.astype(BF16), w_out.astype(BF16), small_sh), (1, 2, 1), "gather_weights")
    w_in_t_full = _to_layout_rows(jnp.transpose(g_win, (1, 0, 2, 3)).reshape(L, N_IN, D))
    w_in_full = jnp.transpose(w_in_t_full, (0, 2, 1))
    w_out_full = jnp.transpose(g_wout, (1, 0, 2, 3)).reshape(L, D, D)
    conv_w_full = jnp.transpose(g_small[:, :, 0:32, :], (1, 2, 0, 3)).reshape(L, 32, 4 * cw_sh)
    a2f_full = jnp.transpose(g_small[:, :, 32:48, :a2_sh], (1, 2, 0, 3)).reshape(L, GLA_RANK, 4 * a2_sh)
    a2b_full = jnp.transpose(g_small[:, :, 48:64, :a2_sh], (1, 2, 0, 3)).reshape(L, GLA_RANK, 4 * a2_sh)

    params = []
    for l in range(L):
        params.append(dict(
            norm_g=norm_g[l][None], w_in=w_in_full[l], w_out=w_out_full[l], w_in_t=w_in_t_full[l], w_out_t=w_out_full[l].T,
            conv_w32=conv_w_full[l], conv_b=conv_b[l][None],
            conv_ln_g=conv_ln_g[l][None], conv_ln_b=conv_ln_b[l][None], na_q_g=na_q_g[l].reshape(1, GROUP_W),
            na_k_g=na_k_g[l].reshape(1, GROUP_W), na_bias=_na_bias(na_rpb[l], f"l{l}_na_bias"),
            a2_f=jnp.zeros((128, _HK), F32).at[0:GLA_RANK].set(a2f_full[l]),
            a2_b=jnp.zeros((128, _HK), F32).at[GLA_RANK:2 * GLA_RANK].set(a2b_full[l]),
            gla_ab_f=gla_ab_f[l][None], gla_ab_b=gla_ab_b[l][None], gla_o_g=gla_o_g[l].reshape(1, GROUP_W),
            pool_w_bf=pool_w[l].astype(BF16), pool_scale=pool_scale[l][None]))

    act = x.reshape(T, D)
    saved = []
    for l in range(L):
        act, sv = _layer_fwd(l, act, params[l], S, loss_target.reshape(T, D) if l == L - 1 else None)
        saved.append(sv)
    dact, dact_bf, loss_loc = act
    loss = lax.psum(loss_loc[0, 0], ("x", "y", "c"))
    grads = [None] * L
    for l in reversed(range(L)):
        dact, dact_bf, grads[l] = _layer_bwd(l, dact, dact_bf, saved[l], params[l], S)
    grad_x = dact.reshape(E, S, D)
    G = {k: jnp.stack([grads[l][k] for l in range(L)]) for k in WEIGHTS}

    cols_in, cols_out = N_IN // 4, D
    p_win = _from_layout_rows(G["w_in"]).reshape(L, 4, cols_in, D)
    p_wout = G["w_out"].reshape(L, 4, D // 4, D)
    small_names = _REPL + _SHARD_SMALL
    small_g = _pack([G[k] for k in small_names])
    g_in, g_out, r_small = _reduce_gradients(p_win, p_wout, small_g, ci)

    rows_in, rows_out = L * cols_in, L * (D // 4)
    res = {}
    res["w_in"] = [jnp.transpose(a.reshape(L, cols_in, D), (0, 2, 1)) for a in _adamw(
        w_in_tr.reshape(rows_in, D), g_in, m_w_in_tr.reshape(rows_in, D), v_w_in_tr.reshape(rows_in, D), "adamw_w_in", pick=ci)]
    res["w_out"] = [a.reshape(L, D // 4, D) for a in _adamw(
        w_out.reshape(rows_out, cols_out), g_out, m_w_out.reshape(rows_out, cols_out),
        v_w_out.reshape(rows_out, cols_out), "adamw_w_out", pick=ci)]
    zeros_sh = [jnp.zeros(G[k].shape, F32) for k in _SHARD_SMALL]
    pk = lambda dct: _pack([dct[k] for k in _REPL] + zeros_sh)
    small_res = _adamw(pk(W), (r_small,), pk(M), pk(V), "adamw_small")
    shapes = [G[k].shape for k in small_names]
    unp = [_unpack(a, shapes) for a in small_res]
    for i, k in enumerate(_REPL):
        res[k] = [u[i] for u in unp]
    g_sh = []
    for i, k in enumerate(_SHARD_SMALL):
        gfull = unp[0][len(_REPL) + i]
        wdt = W[k].shape[-1]
        g_sh.append(lax.dynamic_slice_in_dim(gfull, chip * wdt, wdt, axis=2))
    g_sh_p = _pack(g_sh)
    sh_res = _adamw(_pack([W[k] for k in _SHARD_SMALL]), (g_sh_p,), _pack([M[k] for k in _SHARD_SMALL]),
                    _pack([V[k] for k in _SHARD_SMALL]), "adamw_shard_small")
    shapes2 = [W[k].shape for k in _SHARD_SMALL]
    unp2 = [_unpack(a, shapes2) for a in sh_res]
    for i, k in enumerate(_SHARD_SMALL):
        res[k] = [u[i] for u in unp2]

    outs = [loss, grad_x]
    for j in range(4):
        outs += [res[k][j] for k in WEIGHTS]
    return tuple(outs)
```

```python
import functools

import numpy as np
import jax
import jax.numpy as jnp
from jax import lax
from jax.experimental import pallas as pl
from jax.experimental.pallas import tpu as pltpu

F32 = jnp.float32
BF16 = jnp.bfloat16
HI = lax.Precision.HIGHEST
MESH = pl.DeviceIdType.MESH

EPS = 1e-6
D_MODEL = 2048
GROUP_W = 512
SEQ = 2048
DEPTH = 2
N_IN = 6176
GRID_W = 64
CONV_K = 31
NA_HEADS = 8
NA_DH = 64
NA_ROWS = 8
NA_COLS = 16
GLA_HEADS = 4
GLA_DK = 64
GLA_DV = 128
GLA_RANK = 16
GLA_TAU = 16.0
CHUNK = 64
POOL_WINDOWS = (2, 4, 8, 16)
ADAM_LR, ADAM_B1, ADAM_B2, ADAM_EPS, ADAM_WD, ADAM_STEP = 0.001, 0.9, 0.999, 1e-08, 0.01, 10

A_VAL, A_GLU, A_GATE = 0, 512, 1024
B_Q, B_K, B_V, B_GATE = 1536, 2048, 2560, 3072
C_Q, C_K, C_V, C_GATE = 3584, 3840, 4096, 4608
D_VAL, D_GATE = 5120, 5632
LR_OFF = 6144
NZ = 6400
NEG = -1e30
VMEM_LIMIT = 56 * 1024 * 1024


def _cp(sem=None):
    return pltpu.CompilerParams(dimension_semantics=sem, vmem_limit_bytes=VMEM_LIMIT)


def _sigmoid(x):
    return 1.0 / (1.0 + jnp.exp(-x))


def _silu(x):
    return x * _sigmoid(x)


def _dsilu(x):
    s = _sigmoid(x)
    return s * (1.0 + x * (1.0 - s))


def _matmul(a, b, *, dims, out_dtype, tm, tn, tk, name, res=None):
    if dims == "nn":
        (M, K), N = a.shape, b.shape[1]
    elif dims == "nt":
        (M, K), N = a.shape, b.shape[0]
    else:
        (K, M), N = a.shape, b.shape[1]
    tm, tn, tk = min(tm, M), min(tn, N), min(tk, K)
    nk = K // tk
    assert M % tm == 0 and N % tn == 0 and K % tk == 0, (M, N, K, tm, tn, tk)
    dn = {"nn": (((1,), (0,)), ((), ())), "nt": (((1,), (1,)), ((), ())), "tn": (((0,), (0,)), ((), ()))}[dims]
    if dims == "tn":
        a_spec = pl.BlockSpec((tk, tm), lambda i, j, k: (k, i))
    else:
        a_spec = pl.BlockSpec((tm, tk), lambda i, j, k: (i, k))
    if dims == "nt":
        b_spec = pl.BlockSpec((tn, tk), lambda i, j, k: (j, k))
    else:
        b_spec = pl.BlockSpec((tk, tn), lambda i, j, k: (k, j))
    o_spec = pl.BlockSpec((tm, tn), lambda i, j, k: (i, j))
    has_res = res is not None

    def body(*refs):
        if has_res:
            a_ref, b_ref, r_ref, o_ref, acc = refs
        else:
            a_ref, b_ref, o_ref, acc = refs
        k = pl.program_id(2)

        @pl.when(k == 0)
        def _():
            acc[...] = jnp.zeros_like(acc)

        acc[...] += lax.dot_general(a_ref[...], b_ref[...], dn, preferred_element_type=F32)

        @pl.when(k == nk - 1)
        def _():
            r = acc[...]
            if has_res:
                r = r + r_ref[...]
            o_ref[...] = r.astype(o_ref.dtype)

    in_specs = [a_spec, b_spec] + ([o_spec] if has_res else [])
    args = (a, b) + ((res,) if has_res else ())
    return pl.pallas_call(
        body, out_shape=jax.ShapeDtypeStruct((M, N), out_dtype), grid=(M // tm, N // tn, nk),
        in_specs=in_specs, out_specs=o_spec, scratch_shapes=[pltpu.VMEM((tm, tn), F32)],
        name=name, compiler_params=_cp(("parallel", "parallel", "arbitrary")))(*args)


def _concat_cols(pieces, width, name):
    pairs = [p if isinstance(p, tuple) else (p,) for p in pieces]
    T = pairs[0][0].shape[0]
    tm = min(512, T)
    dt = BF16
    offs = np.cumsum([0] + [p[0].shape[1] for p in pairs])
    flat = [a for p in pairs for a in p]

    def body(*refs):
        o_ref = refs[-1]
        k = 0
        for p, a, b in zip(pairs, offs[:-1], offs[1:]):
            val = refs[k][...] if len(p) == 1 else refs[k][...] + refs[k + 1][...]
            o_ref[:, a:b] = val.astype(dt)
            k += len(p)
        if offs[-1] < width:
            o_ref[:, offs[-1]:width] = jnp.zeros((tm, width - offs[-1]), dt)

    return pl.pallas_call(
        body, out_shape=jax.ShapeDtypeStruct((T, width), dt), grid=(T // tm,),
        in_specs=[pl.BlockSpec((tm, a.shape[1]), lambda i: (i, 0)) for a in flat],
        out_specs=pl.BlockSpec((tm, width), lambda i: (i, 0)), name=name, compiler_params=_cp(("parallel",)))(*flat)


def _rmsnorm_fwd(x, g, name):
    T, D = x.shape
    tm = 256

    def body(x_ref, g_ref, h_ref):
        xv = x_ref[...]
        r = lax.rsqrt(jnp.mean(xv * xv, axis=-1, keepdims=True) + EPS)
        h_ref[...] = (xv * r * g_ref[...]).astype(h_ref.dtype)

    return pl.pallas_call(
        body, out_shape=jax.ShapeDtypeStruct((T, D), BF16), grid=(T // tm,),
        in_specs=[pl.BlockSpec((tm, D), lambda i: (i, 0)), pl.BlockSpec((1, D), lambda i: (0, 0))],
        out_specs=pl.BlockSpec((tm, D), lambda i: (i, 0)), name=name, compiler_params=_cp(("parallel",)))(x, g)


def _rmsnorm_bwd(x, g, dh, dres, name):
    T, D = x.shape
    tm = 256

    def body(x_ref, g_ref, dh_ref, dres_ref, dx_ref, dxb_ref, dg_ref):
        xv = x_ref[...]
        r = lax.rsqrt(jnp.mean(xv * xv, axis=-1, keepdims=True) + EPS)
        xh = xv * r
        dh_v = dh_ref[...]
        dxh = dh_v * g_ref[...]
        dx = r * (dxh - xh * jnp.mean(dxh * xh, axis=-1, keepdims=True)) + dres_ref[...]
        dx_ref[...] = dx
        dxb_ref[...] = dx.astype(BF16)

        @pl.when(pl.program_id(0) == 0)
        def _():
            dg_ref[...] = jnp.zeros_like(dg_ref)

        dg_ref[...] += jnp.sum(dh_v * xh, axis=0, keepdims=True)

    row = pl.BlockSpec((tm, D), lambda i: (i, 0))
    vec = pl.BlockSpec((1, D), lambda i: (0, 0))
    return pl.pallas_call(
        body, out_shape=(jax.ShapeDtypeStruct((T, D), F32), jax.ShapeDtypeStruct((T, D), BF16), jax.ShapeDtypeStruct((1, D), F32)),
        grid=(T // tm,), in_specs=[row, vec, row, row], out_specs=(row, row, vec), name=name,
        compiler_params=_cp(("arbitrary",)))(x, g, dh, dres)


_GATE_COLS = (A_GATE // GROUP_W, B_GATE // GROUP_W, C_GATE // GROUP_W, D_GATE // GROUP_W)
_OP_TM = 256


def _out_proj_fwd(pres, z, w_out, x, target, name):
    T, D = x.shape
    tm = min(_OP_TM, T)
    with_loss = target is not None

    def body(*refs):
        pa, pb, pc, pd, ga, gb, gc, gd, w_ref, x_ref = refs[:10]
        refs = refs[10:]
        if with_loss:
            t_ref, y_ref, d_ref, db_ref, l_ref = refs
        else:
            y_ref, o_ref = refs
        for n_, (p, g) in enumerate(((pa, ga), (pb, gb), (pc, gc), (pd, gd))):
            y_ref[:, n_ * GROUP_W:(n_ + 1) * GROUP_W] = (p[...] * _silu(g[...])).astype(BF16)
        out = jnp.dot(y_ref[...], w_ref[...], preferred_element_type=F32) + x_ref[...]
        if with_loss:
            e = out - t_ref[...]
            d = e * (1.0 / D)
            d_ref[...] = d
            db_ref[...] = d.astype(BF16)

            @pl.when(pl.program_id(0) == 0)
            def _():
                l_ref[...] = jnp.zeros_like(l_ref)

            l_ref[...] += jnp.sum(jnp.sum(e * e, axis=-1, keepdims=True) * (0.5 / D), axis=0, keepdims=True)
        else:
            o_ref[...] = out

    pre_spec = pl.BlockSpec((tm, GROUP_W), lambda i: (i, 0))
    gate_specs = [pl.BlockSpec((tm, GROUP_W), functools.partial(lambda i, c: (i, c), c=c)) for c in _GATE_COLS]
    row = pl.BlockSpec((tm, D), lambda i: (i, 0))
    w_spec = pl.BlockSpec((4 * GROUP_W, D), lambda i: (0, 0))
    in_specs = [pre_spec] * 4 + gate_specs + [w_spec, row]
    args = list(pres) + [z, z, z, z, w_out, x]
    if with_loss:
        in_specs.append(row)
        args.append(target)
        out_shape = (jax.ShapeDtypeStruct((T, D), BF16), jax.ShapeDtypeStruct((T, D), F32), jax.ShapeDtypeStruct((T, D), BF16),
                     jax.ShapeDtypeStruct((1, 1), F32))
        out_specs = (row, row, row, pl.BlockSpec((1, 1), lambda i: (0, 0)))
    else:
        out_shape = (jax.ShapeDtypeStruct((T, D), BF16), jax.ShapeDtypeStruct((T, D), F32))
        out_specs = (row, row)
    return pl.pallas_call(body, out_shape=out_shape, grid=(T // tm,), in_specs=in_specs, out_specs=out_specs, name=name,
                          compiler_params=_cp(("arbitrary",)))(*args)


def _out_proj_bwd(dout_bf, w_out_t, pres, z, name):
    T, D = dout_bf.shape
    tm = min(_OP_TM, T)

    def body(do_ref, w_ref, pa, pb, pc, pd, ga, gb, gc, gd, dpa, dpb, dpc, dpd, dga, dgb, dgc, dgd):
        dy = jnp.dot(do_ref[...], w_ref[...], preferred_element_type=F32)
        for n_, (p, g, dp, dg) in enumerate(((pa, ga, dpa, dga), (pb, gb, dpb, dgb), (pc, gc, dpc, dgc), (pd, gd, dpd, dgd))):
            d = dy[:, n_ * GROUP_W:(n_ + 1) * GROUP_W]
            gv = g[...]
            dp[...] = d * _silu(gv)
            dg[...] = (d * p[...] * _dsilu(gv)).astype(BF16)

    pre_spec = pl.BlockSpec((tm, GROUP_W), lambda i: (i, 0))
    gate_specs = [pl.BlockSpec((tm, GROUP_W), functools.partial(lambda i, c: (i, c), c=c)) for c in _GATE_COLS]
    outs = tuple([jax.ShapeDtypeStruct((T, GROUP_W), F32)] * 4 + [jax.ShapeDtypeStruct((T, GROUP_W), BF16)] * 4)
    return pl.pallas_call(
        body, out_shape=outs, grid=(T // tm,),
        in_specs=[pl.BlockSpec((tm, D), lambda i: (i, 0)), pl.BlockSpec((D, 4 * GROUP_W), lambda i: (0, 0))] + [pre_spec] * 4 + gate_specs,
        out_specs=tuple([pre_spec] * 8), name=name, compiler_params=_cp(("parallel",)))(dout_bf, w_out_t, *pres, z, z, z, z)


_PAD = 16
_RC = 256


def _conv_fwd(z, conv_w32, conv_b, S, name):
    T = z.shape[0]
    E = T // S
    LW = 128

    def body(val_ref, glu_ref, w_ref, b_ref, y_ref, upad):
        upad[0:_PAD, :] = jnp.zeros((_PAD, LW), F32)
        upad[_PAD + S:_PAD + S + _PAD, :] = jnp.zeros((_PAD, LW), F32)
        upad[_PAD:_PAD + S, :] = val_ref[...] * _sigmoid(glu_ref[...])
        for r in range(S // _RC):
            acc = jnp.broadcast_to(b_ref[...], (_RC, LW))
            for k in range(CONV_K):
                st = r * _RC + k + 1
                acc = acc + upad[st:st + _RC, :] * w_ref[k:k + 1, :]
            y_ref[r * _RC:(r + 1) * _RC, :] = acc

    return pl.pallas_call(
        body, out_shape=jax.ShapeDtypeStruct((T, GROUP_W), F32), grid=(E, GROUP_W // LW),
        in_specs=[pl.BlockSpec((S, LW), lambda e, j: (e, A_VAL // LW + j)),
                  pl.BlockSpec((S, LW), lambda e, j: (e, A_GLU // LW + j)),
                  pl.BlockSpec((32, LW), lambda e, j: (0, j)),
                  pl.BlockSpec((1, LW), lambda e, j: (0, j))],
        out_specs=pl.BlockSpec((S, LW), lambda e, j: (e, j)),
        scratch_shapes=[pltpu.VMEM((S + 2 * _PAD, LW), F32)],
        name=name, compiler_params=_cp(("parallel", "parallel")))(z, z, conv_w32, conv_b)


def _conv_bwd(z, conv_w32, dyc, S, name):
    T = z.shape[0]
    E = T // S
    LW = 128

    def body(val_ref, glu_ref, w_ref, dy_ref, dval_ref, dglu_ref, dw_ref, db_ref, upad, dpad):
        e = pl.program_id(1)
        zeros = jnp.zeros((_PAD, LW), F32)
        upad[0:_PAD, :] = zeros
        upad[_PAD + S:_PAD + S + _PAD, :] = zeros
        dpad[0:_PAD, :] = zeros
        dpad[_PAD + S:_PAD + S + _PAD, :] = zeros
        upad[_PAD:_PAD + S, :] = val_ref[...] * _sigmoid(glu_ref[...])
        dpad[_PAD:_PAD + S, :] = dy_ref[...]

        @pl.when(e == 0)
        def _():
            dw_ref[...] = jnp.zeros_like(dw_ref)
            db_ref[...] = jnp.zeros_like(db_ref)

        db_ref[...] += jnp.sum(dy_ref[...], axis=0, keepdims=True)
        for r in range(S // _RC):
            dyr = dy_ref[r * _RC:(r + 1) * _RC, :]
            du = jnp.zeros((_RC, LW), F32)
            for k in range(CONV_K):
                st = r * _RC + k + 1
                dw_ref[k:k + 1, :] += jnp.sum(dyr * upad[st:st + _RC, :], axis=0, keepdims=True)
                sd = r * _RC + (CONV_K - 1 - k) + 1
                du = du + dpad[sd:sd + _RC, :] * w_ref[k:k + 1, :]
            sl = slice(r * _RC, (r + 1) * _RC)
            val = val_ref[sl, :]
            sg = _sigmoid(glu_ref[sl, :])
            dval_ref[sl, :] = (du * sg).astype(BF16)
            dglu_ref[sl, :] = (du * val * sg * (1.0 - sg)).astype(BF16)

    blk = pl.BlockSpec((S, LW), lambda j, e: (e, j))
    return pl.pallas_call(
        body, out_shape=(jax.ShapeDtypeStruct((T, GROUP_W), BF16), jax.ShapeDtypeStruct((T, GROUP_W), BF16),
                         jax.ShapeDtypeStruct((32, GROUP_W), F32), jax.ShapeDtypeStruct((1, GROUP_W), F32)),
        grid=(GROUP_W // LW, E),
        in_specs=[pl.BlockSpec((S, LW), lambda j, e: (e, A_VAL // LW + j)),
                  pl.BlockSpec((S, LW), lambda j, e: (e, A_GLU // LW + j)),
                  pl.BlockSpec((32, LW), lambda j, e: (0, j)), blk],
        out_specs=(blk, blk, pl.BlockSpec((32, LW), lambda j, e: (0, j)), pl.BlockSpec((1, LW), lambda j, e: (0, j))),
        scratch_shapes=[pltpu.VMEM((S + 2 * _PAD, LW), F32), pltpu.VMEM((S + 2 * _PAD, LW), F32)],
        name=name, compiler_params=_cp(("parallel", "arbitrary")))(z, z, conv_w32, dyc)


def _ln_silu_fwd(yc, g, b, name):
    T, C = yc.shape
    tm = 256

    def body(y_ref, g_ref, b_ref, o_ref):
        y = y_ref[...]
        mu = jnp.mean(y, axis=-1, keepdims=True)
        yc_ = y - mu
        r = lax.rsqrt(jnp.mean(yc_ * yc_, axis=-1, keepdims=True) + EPS)
        o_ref[...] = _silu(yc_ * r * g_ref[...] + b_ref[...])

    row = pl.BlockSpec((tm, C), lambda i: (i, 0))
    vec = pl.BlockSpec((1, C), lambda i: (0, 0))
    return pl.pallas_call(body, out_shape=jax.ShapeDtypeStruct((T, C), F32), grid=(T // tm,),
                          in_specs=[row, vec, vec], out_specs=row, name=name, compiler_params=_cp(("parallel",)))(yc, g, b)


def _ln_silu_bwd(yc, g, b, dpre, name):
    T, C = yc.shape
    tm = 256

    def body(y_ref, g_ref, b_ref, dp_ref, dy_ref, dg_ref, db_ref):
        y = y_ref[...]
        mu = jnp.mean(y, axis=-1, keepdims=True)
        yc_ = y - mu
        r = lax.rsqrt(jnp.mean(yc_ * yc_, axis=-1, keepdims=True) + EPS)
        xh = yc_ * r
        gv = g_ref[...]
        dln = dp_ref[...] * _dsilu(xh * gv + b_ref[...])
        dxh = dln * gv
        dy_ref[...] = r * (dxh - jnp.mean(dxh, axis=-1, keepdims=True) - xh * jnp.mean(dxh * xh, axis=-1, keepdims=True))

        @pl.when(pl.program_id(0) == 0)
        def _():
            dg_ref[...] = jnp.zeros_like(dg_ref)
            db_ref[...] = jnp.zeros_like(db_ref)

        dg_ref[...] += jnp.sum(dln * xh, axis=0, keepdims=True)
        db_ref[...] += jnp.sum(dln, axis=0, keepdims=True)

    row = pl.BlockSpec((tm, C), lambda i: (i, 0))
    vec = pl.BlockSpec((1, C), lambda i: (0, 0))
    return pl.pallas_call(
        body, out_shape=(jax.ShapeDtypeStruct((T, C), F32), jax.ShapeDtypeStruct((1, C), F32), jax.ShapeDtypeStruct((1, C), F32)),
        grid=(T // tm,), in_specs=[row, vec, vec, row], out_specs=(row, vec, vec), name=name,
        compiler_params=_cp(("arbitrary",)))(yc, g, b, dpre)


def _pool_counts(S, w, rows0, n):
    t = (lax.broadcasted_iota(jnp.int32, (n, 1), 0) + rows0)
    lo = jnp.maximum(t - w // 2, 0)
    hi = jnp.minimum(t + w // 2, S)
    return (hi - lo).astype(F32)


def _pool_fwd(z, pool_w, pool_scale, S, name):
    T = z.shape[0]
    E = T // S
    CG = 128

    def body(u_ref, w_ref, s_ref, o_ref, upad, dif):
        zeros = jnp.zeros((_PAD, GROUP_W), F32)
        upad[0:_PAD, :] = zeros
        upad[_PAD + S:_PAD + S + _PAD, :] = zeros
        upad[_PAD:_PAD + S, :] = u_ref[...]
        for gi, w in enumerate(POOL_WINDOWS):
            ls = slice(gi * CG, (gi + 1) * CG)
            for r in range(S // _RC):
                acc = jnp.zeros((_RC, CG), F32)
                for j in range(-(w // 2), w // 2):
                    st = _PAD + r * _RC + j
                    acc = acc + upad[st:st + _RC, ls]
                cnt = _pool_counts(S, w, r * _RC, _RC)
                dif[r * _RC:(r + 1) * _RC, :] = (acc / cnt - u_ref[r * _RC:(r + 1) * _RC, ls]).astype(BF16)
            yp = jnp.dot(dif[...], w_ref[gi], preferred_element_type=F32)
            o_ref[:, ls] = yp * s_ref[:, ls]

    return pl.pallas_call(
        body, out_shape=jax.ShapeDtypeStruct((T, GROUP_W), F32), grid=(E,),
        in_specs=[pl.BlockSpec((S, GROUP_W), lambda e: (e, D_VAL // GROUP_W)),
                  pl.BlockSpec((4, CG, CG), lambda e: (0, 0, 0)),
                  pl.BlockSpec((1, GROUP_W), lambda e: (0, 0))],
        out_specs=pl.BlockSpec((S, GROUP_W), lambda e: (e, 0)),
        scratch_shapes=[pltpu.VMEM((S + 2 * _PAD, GROUP_W), F32), pltpu.VMEM((S, CG), BF16)],
        name=name, compiler_params=_cp(("parallel",)))(z, pool_w, pool_scale)


def _pool_bwd(z, pool_w, pool_scale, dpre, S, name):
    T = z.shape[0]
    E = T // S
    CG = 128

    def body(u_ref, w_ref, s_ref, dp_ref, du_ref, dw_ref, ds_ref, upad, dif, qpad):
        zeros = jnp.zeros((_PAD, GROUP_W), F32)
        upad[0:_PAD, :] = zeros
        upad[_PAD + S:_PAD + S + _PAD, :] = zeros
        upad[_PAD:_PAD + S, :] = u_ref[...]
        zc = jnp.zeros((_PAD, CG), F32)
        qpad[0:_PAD, :] = zc
        qpad[_PAD + S:_PAD + S + _PAD, :] = zc

        @pl.when(pl.program_id(0) == 0)
        def _():
            dw_ref[...] = jnp.zeros_like(dw_ref)
            ds_ref[...] = jnp.zeros_like(ds_ref)

        for gi, w in enumerate(POOL_WINDOWS):
            ls = slice(gi * CG, (gi + 1) * CG)
            for r in range(S // _RC):
                acc = jnp.zeros((_RC, CG), F32)
                for j in range(-(w // 2), w // 2):
                    st = _PAD + r * _RC + j
                    acc = acc + upad[st:st + _RC, ls]
                cnt = _pool_counts(S, w, r * _RC, _RC)
                dif[r * _RC:(r + 1) * _RC, :] = (acc / cnt - u_ref[r * _RC:(r + 1) * _RC, ls]).astype(BF16)
            dp = dp_ref[:, ls]
            yp = jnp.dot(dif[...], w_ref[gi], preferred_element_type=F32)
            ds_ref[:, ls] += jnp.sum(dp * yp, axis=0, keepdims=True)
            dys = (dp * s_ref[:, ls]).astype(BF16)
            dw_ref[gi] += lax.dot_general(dif[...], dys, (((0,), (0,)), ((), ())), preferred_element_type=F32)
            dm = lax.dot_general(dys, w_ref[gi], (((1,), (1,)), ((), ())), preferred_element_type=F32)
            for r in range(S // _RC):
                cnt = _pool_counts(S, w, r * _RC, _RC)
                qpad[_PAD + r * _RC:_PAD + (r + 1) * _RC, :] = dm[r * _RC:(r + 1) * _RC, :] / cnt
            for r in range(S // _RC):
                acc = -dm[r * _RC:(r + 1) * _RC, :]
                for j in range(-(w // 2) + 1, w // 2 + 1):
                    st = _PAD + r * _RC + j
                    acc = acc + qpad[st:st + _RC, :]
                du_ref[r * _RC:(r + 1) * _RC, ls] = acc.astype(BF16)

    return pl.pallas_call(
        body, out_shape=(jax.ShapeDtypeStruct((T, GROUP_W), BF16), jax.ShapeDtypeStruct((4, CG, CG), F32),
                         jax.ShapeDtypeStruct((1, GROUP_W), F32)), grid=(E,),
        in_specs=[pl.BlockSpec((S, GROUP_W), lambda e: (e, D_VAL // GROUP_W)),
                  pl.BlockSpec((4, CG, CG), lambda e: (0, 0, 0)),
                  pl.BlockSpec((1, GROUP_W), lambda e: (0, 0)),
                  pl.BlockSpec((S, GROUP_W), lambda e: (e, 0))],
        out_specs=(pl.BlockSpec((S, GROUP_W), lambda e: (e, 0)), pl.BlockSpec((4, CG, CG), lambda e: (0, 0, 0)),
                   pl.BlockSpec((1, GROUP_W), lambda e: (0, 0))),
        scratch_shapes=[pltpu.VMEM((S + 2 * _PAD, GROUP_W), F32), pltpu.VMEM((S, CG), BF16),
                        pltpu.VMEM((S + 2 * _PAD, CG), F32)],
        name=name, compiler_params=_cp(("arbitrary",)))(z, pool_w, pool_scale, dpre)


def _na_tables():
    d = np.arange(NA_ROWS)[:, None]
    kr = np.arange(NA_ROWS)[None, :]
    ro = kr - d + (NA_ROWS - 1)
    qc = np.arange(GRID_W)[:, None]
    kc = np.arange(GRID_W)[None, :]
    cs = np.clip(qc - NA_COLS // 2, 0, GRID_W - NA_COLS)
    valid = (kc >= cs) & (kc < cs + NA_COLS)
    co = np.clip(kc - qc + (NA_COLS - 1), 0, 2 * NA_COLS - 2)
    return ro, co, valid


def _na_onehots():
    ro, co, valid = _na_tables()
    e_np = np.zeros((GRID_W, GRID_W, 128), np.float32)
    qi, ki = np.nonzero(valid)
    e_np[qi, ki, co[qi, ki]] = 1.0
    a_np = np.zeros((16, NA_ROWS * NA_ROWS), np.float32)
    a_np[ro.reshape(-1), np.arange(NA_ROWS * NA_ROWS)] = 1.0
    mask = np.where(valid, 0.0, NEG).astype(np.float32).reshape(1, GRID_W * GRID_W)
    return e_np.reshape(GRID_W * GRID_W, 128), a_np, mask


def _na_bias(rpb, name):
    e_np, _, mask = _na_onehots()
    H = NA_HEADS
    rp = jnp.pad(rpb, ((0, 0), (0, 1), (0, 128 - rpb.shape[2])))

    def bands(r_ref, e_ref, m_ref, o_ref):
        o_ref[0] = lax.dot_general(r_ref[0], e_ref[...], (((1,), (1,)), ((), ())), precision=HI,
                                   preferred_element_type=F32) + m_ref[...]

    t = pl.pallas_call(
        bands, out_shape=jax.ShapeDtypeStruct((H, 16, GRID_W * GRID_W), F32), grid=(H,),
        in_specs=[pl.BlockSpec((1, 16, 128), lambda h: (h, 0, 0)),
                  pl.BlockSpec((GRID_W * GRID_W, 128), lambda h: (0, 0)),
                  pl.BlockSpec((1, GRID_W * GRID_W), lambda h: (0, 0))],
        out_specs=pl.BlockSpec((1, 16, GRID_W * GRID_W), lambda h: (h, 0, 0)),
        name=name + "_bands", compiler_params=_cp(("parallel",)))(rp, jnp.asarray(e_np), jnp.asarray(mask))
    t = t.reshape(H, 16, GRID_W, GRID_W)

    def place(t_ref, o_ref):
        for d in range(NA_ROWS):
            for kr in range(NA_ROWS):
                o_ref[0, d, :, kr * GRID_W:(kr + 1) * GRID_W] = t_ref[0, kr - d + NA_ROWS - 1]

    return pl.pallas_call(
        place, out_shape=jax.ShapeDtypeStruct((H, NA_ROWS, GRID_W, NA_ROWS * GRID_W), F32), grid=(H,),
        in_specs=[pl.BlockSpec((1, 16, GRID_W, GRID_W), lambda h: (h, 0, 0, 0))],
        out_specs=pl.BlockSpec((1, NA_ROWS, GRID_W, NA_ROWS * GRID_W), lambda h: (h, 0, 0, 0)),
        name=name, compiler_params=_cp(("parallel",)))(t)


def _seg_mean_matrix(width, seg):
    i = np.arange(width)
    return jnp.asarray((i[:, None] // seg == i[None, :] // seg).astype(np.float32) / seg, BF16)


def _seg_mean(x, seg_ref):
    hi = x.astype(BF16)
    lo = (x - hi.astype(F32)).astype(BF16)
    return (jnp.dot(hi, seg_ref[...], preferred_element_type=F32) + jnp.dot(lo, seg_ref[...], preferred_element_type=F32))


def _na_fwd(z, qg, kg, bias, S, name):
    T = z.shape[0]
    E = T // S
    rows = S // GRID_W
    WIN = NA_ROWS * GRID_W
    seg = _seg_mean_matrix(128, NA_DH)

    def body(q_ref, k_ref, v_ref, qg_ref, kg_ref, bias_ref, seg_ref, o_ref, qs, ks, vs, s_all, p_all):
        for c in range(S // _RC):
            sl = slice(c * _RC, (c + 1) * _RC)
            q = q_ref[sl, :]
            k = k_ref[sl, :]
            qn = q * lax.rsqrt(_seg_mean(q * q, seg_ref) + EPS) * qg_ref[...]
            kn = k * lax.rsqrt(_seg_mean(k * k, seg_ref) + EPS) * kg_ref[...]
            v = v_ref[sl, :]
            for hh in range(2):
                ls = slice(hh * NA_DH, (hh + 1) * NA_DH)
                qs[hh, sl, :] = qn[:, ls].astype(BF16)
                ks[hh, sl, :] = kn[:, ls].astype(BF16)
                vs[hh, sl, :] = v[:, ls].astype(BF16)
        def where(r):
            rs = jnp.clip(r - NA_ROWS // 2, 0, rows - NA_ROWS)
            return rs, pl.multiple_of(r * GRID_W, GRID_W), pl.multiple_of(rs * GRID_W, GRID_W)

        def scores(r, carry):
            rs, q0, k0 = where(r)
            for hh in range(2):
                s = lax.dot_general(qs[hh, pl.ds(q0, GRID_W), :], ks[hh, pl.ds(k0, WIN), :], (((1,), (1,)), ((), ())),
                                    preferred_element_type=F32) * (NA_DH ** -0.5)
                s_all[hh, pl.ds(q0, GRID_W), :] = s + bias_ref[hh, r - rs]
            return carry
        lax.fori_loop(0, rows, scores, 0, unroll=8)

        def soft(r, carry):
            _, q0, _ = where(r)
            for hh in range(2):
                s = s_all[hh, pl.ds(q0, GRID_W), :]
                p = jnp.exp(s - jnp.max(s, axis=-1, keepdims=True))
                p_all[hh, pl.ds(q0, GRID_W), :] = (p * (1.0 / jnp.sum(p, axis=-1, keepdims=True))).astype(BF16)
            return carry
        lax.fori_loop(0, rows, soft, 0, unroll=4)

        def outp(r, carry):
            _, q0, k0 = where(r)
            outs = [jnp.dot(p_all[hh, pl.ds(q0, GRID_W), :], vs[hh, pl.ds(k0, WIN), :], preferred_element_type=F32)
                    for hh in range(2)]
            o_ref[pl.ds(q0, GRID_W), :] = jnp.concatenate(outs, axis=1)
            return carry
        lax.fori_loop(0, rows, outp, 0, unroll=8)

    LW = 128
    return pl.pallas_call(
        body, out_shape=jax.ShapeDtypeStruct((T, GROUP_W), F32), grid=(E, GROUP_W // LW),
        in_specs=[pl.BlockSpec((S, LW), lambda e, j: (e, B_Q // LW + j)),
                  pl.BlockSpec((S, LW), lambda e, j: (e, B_K // LW + j)),
                  pl.BlockSpec((S, LW), lambda e, j: (e, B_V // LW + j)),
                  pl.BlockSpec((1, LW), lambda e, j: (0, j)),
                  pl.BlockSpec((1, LW), lambda e, j: (0, j)),
                  pl.BlockSpec((2, NA_ROWS, GRID_W, WIN), lambda e, j: (j, 0, 0, 0)),
                  pl.BlockSpec((LW, LW), lambda e, j: (0, 0))],
        out_specs=pl.BlockSpec((S, LW), lambda e, j: (e, j)),
        scratch_shapes=[pltpu.VMEM((2, S, NA_DH), BF16)] * 3 + [pltpu.VMEM((2, S, WIN), F32), pltpu.VMEM((2, S, WIN), BF16)],
        name=name, compiler_params=_cp(("parallel", "parallel")))(z, z, z, qg, kg, bias, seg)


def _na_bwd(z, qg, kg, bias, do, S, name):
    T = z.shape[0]
    E = T // S
    rows = S // GRID_W
    WIN = NA_ROWS * GRID_W
    seg = _seg_mean_matrix(128, NA_DH)
    SC = NA_DH ** -0.5

    def body(q_ref, k_ref, v_ref, qg_ref, kg_ref, bias_ref, seg_ref, do_ref,
             dq_ref, dk_ref, dv_ref, dbias_ref, dqg_ref, dkg_ref, qs, ks, vs, dos, dqn, dkn, dvs, akt, avt,
             s_all, dp_all, p_all, ds_all):
        e = pl.program_id(1)

        @pl.when(e == 0)
        def _():
            dbias_ref[...] = jnp.zeros_like(dbias_ref)
            dqg_ref[...] = jnp.zeros_like(dqg_ref)
            dkg_ref[...] = jnp.zeros_like(dkg_ref)

        for c in range(S // _RC):
            sl = slice(c * _RC, (c + 1) * _RC)
            q = q_ref[sl, :]
            k = k_ref[sl, :]
            qn = q * lax.rsqrt(_seg_mean(q * q, seg_ref) + EPS) * qg_ref[...]
            kn = k * lax.rsqrt(_seg_mean(k * k, seg_ref) + EPS) * kg_ref[...]
            v = v_ref[sl, :]
            dd = do_ref[sl, :]
            for hh in range(2):
                ls = slice(hh * NA_DH, (hh + 1) * NA_DH)
                qs[hh, sl, :] = qn[:, ls].astype(BF16)
                ks[hh, sl, :] = kn[:, ls].astype(BF16)
                vs[hh, sl, :] = v[:, ls].astype(BF16)
                dos[hh, sl, :] = dd[:, ls].astype(BF16)
        akt[...] = jnp.zeros_like(akt)
        avt[...] = jnp.zeros_like(avt)

        def where(r):
            rs = jnp.clip(r - NA_ROWS // 2, 0, rows - NA_ROWS)
            return rs, pl.multiple_of(r * GRID_W, GRID_W), pl.multiple_of(rs * GRID_W, GRID_W)

        for hh in range(2):
            ls = slice(hh * NA_DH, (hh + 1) * NA_DH)

            def products(r, carry, hh=hh):
                rs, q0, k0 = where(r)
                s = lax.dot_general(qs[hh, pl.ds(q0, GRID_W), :], ks[hh, pl.ds(k0, WIN), :], (((1,), (1,)), ((), ())),
                                    preferred_element_type=F32) * SC
                s_all[pl.ds(q0, GRID_W), :] = s + bias_ref[hh, r - rs]
                dp_all[pl.ds(q0, GRID_W), :] = lax.dot_general(dos[hh, pl.ds(q0, GRID_W), :], vs[hh, pl.ds(k0, WIN), :],
                                                               (((1,), (1,)), ((), ())), preferred_element_type=F32)
                return carry
            lax.fori_loop(0, rows, products, 0, unroll=8)

            def soft(r, carry, hh=hh):
                rs, q0, _ = where(r)
                s = s_all[pl.ds(q0, GRID_W), :]
                p = jnp.exp(s - jnp.max(s, axis=-1, keepdims=True))
                p = p * (1.0 / jnp.sum(p, axis=-1, keepdims=True))
                dp = dp_all[pl.ds(q0, GRID_W), :]
                ds = p * (dp - jnp.sum(p * dp, axis=-1, keepdims=True))
                dbias_ref[hh, r - rs] += ds
                p_all[pl.ds(q0, GRID_W), :] = p.astype(BF16)
                ds_all[pl.ds(q0, GRID_W), :] = ds.astype(BF16)
                return carry
            lax.fori_loop(0, rows, soft, 0, unroll=4)

            def grads(r, carry, hh=hh, ls=ls):
                rs, q0, k0 = where(r)
                par = rs % 2
                t0 = (rs + par) // 2
                qr = qs[hh, pl.ds(q0, GRID_W), :]
                dor = dos[hh, pl.ds(q0, GRID_W), :]
                dsb = ds_all[pl.ds(q0, GRID_W), :]
                dqn[pl.ds(q0, GRID_W), ls] = jnp.dot(dsb, ks[hh, pl.ds(k0, WIN), :], preferred_element_type=F32) * SC
                dkt = lax.dot_general(qr, dsb, (((0,), (0,)), ((), ())), preferred_element_type=F32) * SC
                dvt = lax.dot_general(dor, p_all[pl.ds(q0, GRID_W), :], (((0,), (0,)), ((), ())), preferred_element_type=F32)
                akt[hh, par, pl.ds(t0, WIN // 128)] += jnp.stack([dkt[:, 128 * i:128 * (i + 1)] for i in range(WIN // 128)])
                avt[hh, par, pl.ds(t0, WIN // 128)] += jnp.stack([dvt[:, 128 * i:128 * (i + 1)] for i in range(WIN // 128)])
                return carry
            lax.fori_loop(0, rows, grads, 0, unroll=8)

        for hh in range(2):
            ls = slice(hh * NA_DH, (hh + 1) * NA_DH)
            for i in range(S // 128):
                for acc, dst in ((akt, dkn), (avt, dvs)):
                    odd = jnp.concatenate([acc[hh, 1, i][:, NA_DH:], acc[hh, 1, i + 1][:, :NA_DH]], axis=1)
                    dst[128 * i:128 * (i + 1), ls] = (acc[hh, 0, i] + odd).T

        for c in range(S // _RC):
            sl = slice(c * _RC, (c + 1) * _RC)
            for x_ref, g_ref, dn, dx_ref, dg_ref in ((q_ref, qg_ref, dqn, dq_ref, dqg_ref), (k_ref, kg_ref, dkn, dk_ref, dkg_ref)):
                x = x_ref[sl, :]
                r_ = lax.rsqrt(_seg_mean(x * x, seg_ref) + EPS)
                xh = x * r_
                d = dn[sl, :]
                dxh = d * g_ref[...]
                mean = _seg_mean(dxh * xh, seg_ref)
                dx_ref[sl, :] = (r_ * (dxh - xh * mean)).astype(BF16)
                dg_ref[...] += jnp.sum(d * xh, axis=0, keepdims=True)
            dv_ref[sl, :] = dvs[sl, :].astype(BF16)

    LW = 128
    blk = pl.BlockSpec((S, LW), lambda j, e: (e, j))
    vec = pl.BlockSpec((1, LW), lambda j, e: (0, j))
    bsp = pl.BlockSpec((2, NA_ROWS, GRID_W, WIN), lambda j, e: (j, 0, 0, 0))
    return pl.pallas_call(
        body, out_shape=(jax.ShapeDtypeStruct((T, GROUP_W), BF16),) * 3 + (
            jax.ShapeDtypeStruct((NA_HEADS, NA_ROWS, GRID_W, WIN), F32),
            jax.ShapeDtypeStruct((1, GROUP_W), F32), jax.ShapeDtypeStruct((1, GROUP_W), F32)),
        grid=(GROUP_W // LW, E),
        in_specs=[pl.BlockSpec((S, LW), lambda j, e: (e, B_Q // LW + j)),
                  pl.BlockSpec((S, LW), lambda j, e: (e, B_K // LW + j)),
                  pl.BlockSpec((S, LW), lambda j, e: (e, B_V // LW + j)),
                  vec, vec, bsp, pl.BlockSpec((LW, LW), lambda j, e: (0, 0)), blk],
        out_specs=(blk, blk, blk, bsp, vec, vec),
        scratch_shapes=[pltpu.VMEM((2, S, NA_DH), BF16)] * 4 + [pltpu.VMEM((S, LW), F32)] * 3
        + [pltpu.VMEM((2, 2, S // 128 + 1, NA_DH, 128), F32)] * 2
        + [pltpu.VMEM((S, WIN), F32)] * 2 + [pltpu.VMEM((S, WIN), BF16)] * 2,
        name=name, compiler_params=_cp(("parallel", "arbitrary")))(z, z, z, qg, kg, bias, seg, do)


def _na_rpb_grad(dbias, name):
    e_np, _, _ = _na_onehots()
    H = NA_HEADS
    nro = 2 * NA_ROWS - 1

    def fold(x_ref, o_ref):
        for ro in range(nro):
            acc = None
            for d in range(NA_ROWS):
                kr = ro + d - (NA_ROWS - 1)
                if 0 <= kr < NA_ROWS:
                    blk = x_ref[0, d, :, kr * GRID_W:(kr + 1) * GRID_W]
                    acc = blk if acc is None else acc + blk
            o_ref[0, ro] = acc
        o_ref[0, nro] = jnp.zeros((GRID_W, GRID_W), F32)

    t = pl.pallas_call(
        fold, out_shape=jax.ShapeDtypeStruct((H, 16, GRID_W, GRID_W), F32), grid=(H,),
        in_specs=[pl.BlockSpec((1, NA_ROWS, GRID_W, NA_ROWS * GRID_W), lambda h: (h, 0, 0, 0))],
        out_specs=pl.BlockSpec((1, 16, GRID_W, GRID_W), lambda h: (h, 0, 0, 0)),
        name=name + "_fold", compiler_params=_cp(("parallel",)))(dbias)
    t = t.reshape(H, 16, GRID_W * GRID_W)

    def body(x_ref, e_ref, o_ref):
        o_ref[0] = jnp.dot(x_ref[0], e_ref[...], precision=HI, preferred_element_type=F32)

    out = pl.pallas_call(
        body, out_shape=jax.ShapeDtypeStruct((H, 16, 128), F32), grid=(H,),
        in_specs=[pl.BlockSpec((1, 16, GRID_W * GRID_W), lambda h: (h, 0, 0)),
                  pl.BlockSpec((GRID_W * GRID_W, 128), lambda h: (0, 0))],
        out_specs=pl.BlockSpec((1, 16, 128), lambda h: (h, 0, 0)),
        name=name, compiler_params=_cp(("parallel",)))(t, jnp.asarray(e_np))
    return out[:, :nro, :2 * NA_COLS - 1]


_HK = GLA_HEADS * GLA_DK
_HV = GLA_HEADS * GLA_DV


def _gla_consts(reverse):
    i = np.arange(CHUNK)
    tri = (i[:, None] <= i[None, :]) if reverse else (i[:, None] >= i[None, :])
    j = np.arange(_HK)
    oseg = (j[:, None] // GLA_DK == j[None, :] // GLA_DK)
    return (jnp.asarray(tri.astype(np.float32)), jnp.asarray(tri.T.astype(np.float32)), jnp.asarray(oseg.astype(np.float32), BF16))


def _log_decay(lr, a2, ab):
    zg = jnp.dot(lr, a2, precision=HI, preferred_element_type=F32) + ab
    g = (jnp.minimum(zg, 0.0) - jnp.log(1.0 + jnp.exp(-jnp.abs(zg)))) * (1.0 / GLA_TAU)
    return zg, g


def _dotf(a, b, dn):
    return lax.dot_general(a, b, dn, precision=HI, preferred_element_type=F32)


def _dotb(a, b, dn):
    return lax.dot_general(a.astype(BF16), b.astype(BF16), dn, preferred_element_type=F32)


_COLS = 4
_SUB = 16
_NSUB = CHUNK // _SUB


def _gla_cross_blocks(reverse):
    return range(0, _NSUB - 1) if reverse else range(1, _NSUB)


def _gla_cross_terms(s, reverse, b_s, q_s, k_s, oseg_ref):
    r0 = s * _SUB
    ref = r0 + (_SUB - 1 if reverse else 0)
    bref = b_s[ref:ref + 1, :]
    rowj = lax.broadcasted_iota(jnp.int32, (CHUNK, 1), 0)
    seen = (rowj >= r0 + _SUB) if reverse else (rowj < r0)
    ek = jnp.where(seen, jnp.exp(jnp.minimum(bref - b_s[...], 0.0)), 0.0)
    kt = k_s[...] * ek
    eq = jnp.exp(jnp.minimum(b_s[r0:r0 + _SUB, :] - bref, 0.0))
    qt = q_s[r0:r0 + _SUB, :] * eq
    nmat = jnp.concatenate([kt.astype(BF16)] * GLA_HEADS, axis=0) * oseg_ref[...]
    return qt, eq, kt, ek, nmat


_NN = (((1,), (0,)), ((), ()))
_NT = (((1,), (1,)), ((), ()))
_TN = (((0,), (0,)), ((), ()))


_DT = 256


def _gla_block_tri(reverse):
    i = np.arange(_DT)
    same = i[:, None] // CHUNK == i[None, :] // CHUNK
    tri = (i[:, None] <= i[None, :]) if reverse else (i[:, None] >= i[None, :])
    return (tri & same).astype(np.float32)


def _gla_decay_fwd(z, a2, ab, reverse, name):
    T = z.shape[0]
    nc = _DT // CHUNK

    def body(lr_ref, a2_ref, ab_ref, m_ref, b_ref, ec_ref):
        _, g = _log_decay(lr_ref[...], a2_ref[...], ab_ref[...])
        b_ref[...] = _dotf(m_ref[...], g, _NN)
        for c in range(nc):
            ec_ref[c] = jnp.exp(_dotf(g[c * CHUNK:(c + 1) * CHUNK, :], jnp.ones((CHUNK, GLA_DV), F32), _TN))

    return pl.pallas_call(
        body, out_shape=(jax.ShapeDtypeStruct((T, _HK), F32), jax.ShapeDtypeStruct((T // CHUNK, _HK, GLA_DV), F32)),
        grid=(T // _DT,),
        in_specs=[pl.BlockSpec((_DT, 128), lambda i: (i, LR_OFF // 128)),
                  pl.BlockSpec((128, _HK), lambda i: (0, 0)),
                  pl.BlockSpec((1, _HK), lambda i: (0, 0)),
                  pl.BlockSpec((_DT, _DT), lambda i: (0, 0))],
        out_specs=(pl.BlockSpec((_DT, _HK), lambda i: (i, 0)), pl.BlockSpec((nc, _HK, GLA_DV), lambda i: (i, 0, 0))),
        name=name, compiler_params=_cp(("parallel",)))(z, a2, ab, jnp.asarray(_gla_block_tri(reverse)))


def _gla_decay_fwd_both(z, a2_f, ab_f, a2_b, ab_b, name):
    T = z.shape[0]
    nc = _DT // CHUNK

    def body(lr_ref, a2f_ref, abf_ref, a2b_ref, abb_ref, mf_ref, mb_ref, bf_ref, ecf_ref, bb_ref, ecb_ref):
        lr = lr_ref[...]
        ones = jnp.ones((CHUNK, GLA_DV), F32)
        for a2_ref, ab_ref, m_ref, b_ref, ec_ref in ((a2f_ref, abf_ref, mf_ref, bf_ref, ecf_ref),
                                                     (a2b_ref, abb_ref, mb_ref, bb_ref, ecb_ref)):
            _, g = _log_decay(lr, a2_ref[...], ab_ref[...])
            b_ref[...] = _dotf(m_ref[...], g, _NN)
            for c in range(nc):
                ec_ref[c] = jnp.exp(_dotf(g[c * CHUNK:(c + 1) * CHUNK, :], ones, _TN))

    a2s = pl.BlockSpec((128, _HK), lambda i: (0, 0))
    abs_ = pl.BlockSpec((1, _HK), lambda i: (0, 0))
    ms = pl.BlockSpec((_DT, _DT), lambda i: (0, 0))
    row = pl.BlockSpec((_DT, _HK), lambda i: (i, 0))
    ecs = pl.BlockSpec((nc, _HK, GLA_DV), lambda i: (i, 0, 0))
    shapes = (jax.ShapeDtypeStruct((T, _HK), F32), jax.ShapeDtypeStruct((T // CHUNK, _HK, GLA_DV), F32))
    return pl.pallas_call(
        body, out_shape=shapes + shapes, grid=(T // _DT,),
        in_specs=[pl.BlockSpec((_DT, 128), lambda i: (i, LR_OFF // 128)), a2s, abs_, a2s, abs_, ms, ms],
        out_specs=(row, ecs, row, ecs), name=name, compiler_params=_cp(("parallel",)))(
            z, a2_f, ab_f, a2_b, ab_b, jnp.asarray(_gla_block_tri(False)), jnp.asarray(_gla_block_tri(True)))


def _gla_decay_bwd(z, a2_f, ab_f, a2_b, ab_b, db_f, db_b, name):
    T = z.shape[0]

    def body(lr_ref, a2f_ref, abf_ref, a2b_ref, abb_ref, mf_ref, mb_ref, dbf_ref, dbb_ref,
             dlr_ref, da2f_ref, dabf_ref, da2b_ref, dabb_ref):
        @pl.when(pl.program_id(0) == 0)
        def _():
            for r in (da2f_ref, dabf_ref, da2b_ref, dabb_ref):
                r[...] = jnp.zeros_like(r)

        lr = lr_ref[...]
        dlr = jnp.zeros((_DT, 128), F32)
        for a2_ref, ab_ref, mt_ref, db_ref, da2_ref, dab_ref in ((a2f_ref, abf_ref, mf_ref, dbf_ref, da2f_ref, dabf_ref),
                                                                 (a2b_ref, abb_ref, mb_ref, dbb_ref, da2b_ref, dabb_ref)):
            zg, _ = _log_decay(lr, a2_ref[...], ab_ref[...])
            dg = _dotf(mt_ref[...], db_ref[...], _NN)
            dzg = dg * (1.0 / (1.0 + jnp.exp(zg))) * (1.0 / GLA_TAU)
            dlr = dlr + _dotf(dzg, a2_ref[...], _NT)
            da2_ref[...] += _dotf(lr, dzg, _TN)
            dab_ref[...] += jnp.sum(dzg, axis=0, keepdims=True)
        dlr_ref[...] = dlr.astype(BF16)

    a2s = pl.BlockSpec((128, _HK), lambda i: (0, 0))
    abs_ = pl.BlockSpec((1, _HK), lambda i: (0, 0))
    ms = pl.BlockSpec((_DT, _DT), lambda i: (0, 0))
    row = pl.BlockSpec((_DT, _HK), lambda i: (i, 0))
    return pl.pallas_call(
        body, out_shape=(jax.ShapeDtypeStruct((T, 128), BF16), jax.ShapeDtypeStruct((128, _HK), F32), jax.ShapeDtypeStruct((1, _HK), F32),
                         jax.ShapeDtypeStruct((128, _HK), F32), jax.ShapeDtypeStruct((1, _HK), F32)),
        grid=(T // _DT,),
        in_specs=[pl.BlockSpec((_DT, 128), lambda i: (i, LR_OFF // 128)), a2s, abs_, a2s, abs_, ms, ms, row, row],
        out_specs=(pl.BlockSpec((_DT, 128), lambda i: (i, 0)), a2s, abs_, a2s, abs_),
        name=name, compiler_params=_cp(("arbitrary",)))(
            z, a2_f, ab_f, a2_b, ab_b, jnp.asarray(_gla_block_tri(False).T), jnp.asarray(_gla_block_tri(True).T), db_f, db_b)


def _gla_fwd(z, b_all, ecol, S, reverse, name):
    T = z.shape[0]
    E = T // S
    n = S // CHUNK
    _, _, oseg = _gla_consts(reverse)
    last = 0 if reverse else CHUNK - 1

    def body(q_ref, k_ref, v_ref, b_ref, ec_ref, oseg_ref, o_ref, a_ref, st_ref, st, b_s, q_s, k_s):
        @pl.when(pl.program_id(1) == 0)
        def _():
            st[...] = jnp.zeros_like(st)

        q = q_ref[...] * (GLA_DK ** -0.5)
        k = k_ref[...]
        v = v_ref[...]
        b = b_ref[...]
        bl_row = b_ref[last:last + 1, :]
        e_col = ec_ref[0]
        b_s[...] = b
        q_s[...] = q
        k_s[...] = k
        lane = lax.broadcasted_iota(jnp.int32, (1, _HK), 1) % GLA_DK

        rowi = lax.broadcasted_iota(jnp.int32, (CHUNK, 1), 0)
        blk0 = (rowi // _SUB) * _SUB

        def cols(jj, a):
            ts = []
            for u in range(_COLS):
                jp = jj * _COLS + u
                tiles = []
                for s in range(_NSUB):
                    rs_ = slice(s * _SUB, (s + 1) * _SUB)
                    bj = b_s[pl.ds(s * _SUB + jp, 1), :]
                    kj = k_s[pl.ds(s * _SUB + jp, 1), :]
                    tiles.append(q_s[rs_, :] * jnp.exp(jnp.minimum(b_s[rs_, :] - bj, 0.0)) * kj)
                ts.append(jnp.concatenate(tiles, axis=0).astype(BF16))
            r = jnp.dot(jnp.concatenate(ts, axis=0), oseg_ref[...], preferred_element_type=F32)
            for u in range(_COLS):
                a = jnp.where(lane == blk0 + (jj * _COLS + u), r[u * CHUNK:(u + 1) * CHUNK, :], a)
            return a

        a = lax.fori_loop(0, _SUB // _COLS, cols, jnp.zeros((CHUNK, _HK), F32))
        keep = (rowi <= lane) if reverse else (rowi >= lane)
        a = jnp.where(keep, a, 0.0)
        cross = []
        for s in range(_NSUB):
            if s in _gla_cross_blocks(reverse):
                qt, _, _, _, nmat = _gla_cross_terms(s, reverse, b_s, q_s, k_s, oseg_ref)
                cross.append(lax.dot_general(qt.astype(BF16), nmat, _NT, preferred_element_type=F32))
            else:
                cross.append(jnp.zeros((_SUB, _HK), F32))
        a = a + jnp.concatenate(cross, axis=0)
        a_ref[...] = a
        st_ref[0] = st[...]
        qb = q * jnp.exp(b)
        kd = k * jnp.exp(bl_row - b)
        for h in range(GLA_HEADS):
            ks_ = slice(h * GLA_DK, (h + 1) * GLA_DK)
            vs_ = slice(h * GLA_DV, (h + 1) * GLA_DV)
            s_h = st[ks_, :]
            o_ref[:, vs_] = _dotb(qb[:, ks_], s_h, _NN) + _dotb(a[:, ks_], v[:, vs_], _NN)
            st[ks_, :] = s_h * e_col[ks_, :] + _dotb(kd[:, ks_], v[:, vs_], _TN)

    def rowblk(e, c):
        return e * n + ((n - 1 - c) if reverse else c)

    return pl.pallas_call(
        body, out_shape=(jax.ShapeDtypeStruct((T, _HV), F32), jax.ShapeDtypeStruct((T, _HK), F32),
                         jax.ShapeDtypeStruct((T // CHUNK, _HK, GLA_DV), F32)),
        grid=(E, n),
        in_specs=[pl.BlockSpec((CHUNK, _HK), lambda e, c: (rowblk(e, c), C_Q // _HK)),
                  pl.BlockSpec((CHUNK, _HK), lambda e, c: (rowblk(e, c), C_K // _HK)),
                  pl.BlockSpec((CHUNK, _HV), lambda e, c: (rowblk(e, c), C_V // _HV)),
                  pl.BlockSpec((CHUNK, _HK), lambda e, c: (rowblk(e, c), 0)),
                  pl.BlockSpec((1, _HK, GLA_DV), lambda e, c: (rowblk(e, c), 0, 0)),
                  pl.BlockSpec((_HK, _HK), lambda e, c: (0, 0))],
        out_specs=(pl.BlockSpec((CHUNK, _HV), lambda e, c: (rowblk(e, c), 0)),
                   pl.BlockSpec((CHUNK, _HK), lambda e, c: (rowblk(e, c), 0)),
                   pl.BlockSpec((1, _HK, GLA_DV), lambda e, c: (rowblk(e, c), 0, 0))),
        scratch_shapes=[pltpu.VMEM((_HK, GLA_DV), F32)] + [pltpu.VMEM((CHUNK, _HK), F32)] * 3,
        name=name, compiler_params=_cp(("parallel", "arbitrary")))(z, z, z, b_all, ecol, oseg)


def _gla_fwd_both(z, b_f, ec_f, b_b, ec_b, S, name):
    T = z.shape[0]
    E = T // S
    n = S // CHUNK
    _, _, oseg = _gla_consts(False)

    def body(*refs):
        oseg_ref = refs[10]
        dirs = []
        for di, reverse in enumerate((False, True)):
            q_ref, k_ref, v_ref, b_ref, ec_ref = refs[5 * di:5 * di + 5]
            o_ref, a_ref, st_ref = refs[11 + 3 * di:14 + 3 * di]
            st, b_s, q_s, k_s = refs[17 + 4 * di:21 + 4 * di]
            dirs.append((reverse, q_ref, k_ref, v_ref, b_ref, ec_ref, o_ref, a_ref, st_ref, st, b_s, q_s, k_s))

        @pl.when(pl.program_id(1) == 0)
        def _():
            for d in dirs:
                d[9][...] = jnp.zeros_like(d[9])

        lane = lax.broadcasted_iota(jnp.int32, (1, _HK), 1) % GLA_DK
        rowi = lax.broadcasted_iota(jnp.int32, (CHUNK, 1), 0)
        blk0 = (rowi // _SUB) * _SUB
        for (_, q_ref, k_ref, _, b_ref, _, _, _, _, _, b_s, q_s, k_s) in dirs:
            b_s[...] = b_ref[...]
            q_s[...] = q_ref[...] * (GLA_DK ** -0.5)
            k_s[...] = k_ref[...]

        def cols(d, jj, a):
            b_s, q_s, k_s = d[10], d[11], d[12]
            ts = []
            for u in range(_COLS):
                jp = jj * _COLS + u
                tiles = []
                for s in range(_NSUB):
                    rs_ = slice(s * _SUB, (s + 1) * _SUB)
                    bj = b_s[pl.ds(s * _SUB + jp, 1), :]
                    kj = k_s[pl.ds(s * _SUB + jp, 1), :]
                    tiles.append(q_s[rs_, :] * jnp.exp(jnp.minimum(b_s[rs_, :] - bj, 0.0)) * kj)
                ts.append(jnp.concatenate(tiles, axis=0).astype(BF16))
            r = jnp.dot(jnp.concatenate(ts, axis=0), oseg_ref[...], preferred_element_type=F32)
            for u in range(_COLS):
                a = jnp.where(lane == blk0 + (jj * _COLS + u), r[u * CHUNK:(u + 1) * CHUNK, :], a)
            return a

        zero = jnp.zeros((CHUNK, _HK), F32)
        acc = lax.fori_loop(0, _SUB // _COLS, lambda jj, c: tuple(cols(d, jj, a) for d, a in zip(dirs, c)), (zero, zero),
                            unroll=True)

        for (reverse, _, _, v_ref, b_ref, ec_ref, o_ref, a_ref, st_ref, st, b_s, q_s, k_s), a in zip(dirs, acc):
            last = 0 if reverse else CHUNK - 1
            keep = (rowi <= lane) if reverse else (rowi >= lane)
            a = jnp.where(keep, a, 0.0)
            cross = []
            for s in range(_NSUB):
                if s in _gla_cross_blocks(reverse):
                    qt, _, _, _, nmat = _gla_cross_terms(s, reverse, b_s, q_s, k_s, oseg_ref)
                    cross.append(lax.dot_general(qt.astype(BF16), nmat, _NT, preferred_element_type=F32))
                else:
                    cross.append(jnp.zeros((_SUB, _HK), F32))
            a = a + jnp.concatenate(cross, axis=0)
            a_ref[...] = a
            st_ref[0] = st[...]
            b = b_s[...]
            v = v_ref[...]
            e_col = ec_ref[0]
            qb = q_s[...] * jnp.exp(b)
            kd = k_s[...] * jnp.exp(b_ref[last:last + 1, :] - b)
            for h in range(GLA_HEADS):
                ks_ = slice(h * GLA_DK, (h + 1) * GLA_DK)
                vs_ = slice(h * GLA_DV, (h + 1) * GLA_DV)
                s_h = st[ks_, :]
                o_ref[:, vs_] = _dotb(qb[:, ks_], s_h, _NN) + _dotb(a[:, ks_], v[:, vs_], _NN)
                st[ks_, :] = s_h * e_col[ks_, :] + _dotb(kd[:, ks_], v[:, vs_], _TN)

    def specs(reverse):
        rb = (lambda e, c: e * n + (n - 1 - c)) if reverse else (lambda e, c: e * n + c)
        ins = [pl.BlockSpec((CHUNK, _HK), lambda e, c: (rb(e, c), C_Q // _HK)),
               pl.BlockSpec((CHUNK, _HK), lambda e, c: (rb(e, c), C_K // _HK)),
               pl.BlockSpec((CHUNK, _HV), lambda e, c: (rb(e, c), C_V // _HV)),
               pl.BlockSpec((CHUNK, _HK), lambda e, c: (rb(e, c), 0)),
               pl.BlockSpec((1, _HK, GLA_DV), lambda e, c: (rb(e, c), 0, 0))]
        outs = [pl.BlockSpec((CHUNK, _HV), lambda e, c: (rb(e, c), 0)),
                pl.BlockSpec((CHUNK, _HK), lambda e, c: (rb(e, c), 0)),
                pl.BlockSpec((1, _HK, GLA_DV), lambda e, c: (rb(e, c), 0, 0))]
        return ins, outs

    in_f, out_f = specs(False)
    in_b, out_b = specs(True)
    shapes = (jax.ShapeDtypeStruct((T, _HV), F32), jax.ShapeDtypeStruct((T, _HK), F32),
              jax.ShapeDtypeStruct((T // CHUNK, _HK, GLA_DV), F32))
    res = pl.pallas_call(
        body, out_shape=shapes + shapes, grid=(E, n),
        in_specs=in_f + in_b + [pl.BlockSpec((_HK, _HK), lambda e, c: (0, 0))], out_specs=tuple(out_f + out_b),
        scratch_shapes=([pltpu.VMEM((_HK, GLA_DV), F32)] + [pltpu.VMEM((CHUNK, _HK), F32)] * 3) * 2,
        name=name, compiler_params=_cp(("parallel", "arbitrary")))(z, z, z, b_f, ec_f, z, z, z, b_b, ec_b, oseg)
    return res[:3], res[3:]


def _gla_bwd(z, b_all, ecol, att, states, do, prev, S, reverse, name):
    T = z.shape[0]
    E = T // S
    n = S // CHUNK
    _, _, oseg = _gla_consts(reverse)
    has_prev = prev is not None
    odt = BF16 if has_prev else F32
    last = 0 if reverse else CHUNK - 1

    def body(*refs):
        (q_ref, k_ref, v_ref, b_ref, ec_ref, oseg_ref, att_ref, st_ref, do_ref) = refs[:9]
        refs = refs[9:]
        if has_prev:
            pq_ref, pk_ref, pv_ref = refs[:3]
            refs = refs[3:]
        (dq_ref, dk_ref, dv_ref, db_ref, dst, b_s, q_s, k_s, da_s, dqb_s, dkd_s, dk3_s, dbn_s, dsp_s) = refs

        @pl.when(pl.program_id(1) == 0)
        def _():
            dst[...] = jnp.zeros_like(dst)

        q = q_ref[...] * (GLA_DK ** -0.5)
        k = k_ref[...]
        v = v_ref[...]
        b = b_ref[...]
        bl_row = b_ref[last:last + 1, :]
        eb = jnp.exp(b)
        ekd = jnp.exp(bl_row - b)
        qb = q * eb
        kd = k * ekd
        b_s[...] = b
        q_s[...] = q
        k_s[...] = k
        att = att_ref[...]
        s_all = st_ref[0]
        dsn = dst[...]
        e_col = ec_ref[0]
        do = do_ref[...]
        lane = lax.broadcasted_iota(jnp.int32, (1, _HK), 1) % GLA_DK
        rowi = lax.broadcasted_iota(jnp.int32, (CHUNK, 1), 0)
        keep = (rowi <= lane) if reverse else (rowi >= lane)
        for h in range(GLA_HEADS):
            ks_ = slice(h * GLA_DK, (h + 1) * GLA_DK)
            vs_ = slice(h * GLA_DV, (h + 1) * GLA_DV)
            do_h = do[:, vs_]
            s_h = s_all[ks_, :]
            dsn_h = dsn[ks_, :]
            dqb_s[:, ks_] = _dotb(do_h, s_h, _NT)
            dsp_s[ks_, :] = _dotb(qb[:, ks_], do_h, _TN) + dsn_h * e_col[ks_, :]
            da_s[:, ks_] = _dotb(do_h, v[:, vs_], _NT)
            dv_h = _dotb(att[:, ks_], do_h, _TN) + _dotb(kd[:, ks_], dsn_h, _NN)
            if has_prev:
                dv_h = dv_h + pv_ref[:, vs_]
            dv_ref[:, vs_] = dv_h.astype(odt)
            dkd_s[:, ks_] = _dotb(v[:, vs_], dsn_h, _NT)
        da_s[...] = jnp.where(keep, da_s[...], 0.0)
        dqb = dqb_s[...]
        dkd = dkd_s[...]
        x = dsn * s_all * e_col
        dbl_row = _dotf(jnp.ones((8, GLA_DV), F32), x, _NT)[0:1, :] + jnp.sum(dkd * kd, axis=0, keepdims=True)

        blk0 = (rowi // _SUB) * _SUB

        def cols(jj, carry):
            dq3, db3 = list(carry[:_NSUB]), list(carry[_NSUB:])
            sel = [jnp.where(lane == blk0 + (jj * _COLS + u), da_s[...], 0.0).astype(BF16) for u in range(_COLS)]
            dcols = jnp.dot(jnp.concatenate(sel, axis=0), oseg_ref[...], preferred_element_type=F32)
            for u in range(_COLS):
                jp = jj * _COLS + u
                for s in range(_NSUB):
                    rs_ = slice(s * _SUB, (s + 1) * _SUB)
                    bj = b_s[pl.ds(s * _SUB + jp, 1), :]
                    kj = k_s[pl.ds(s * _SUB + jp, 1), :]
                    tm_ = dcols[u * CHUNK + s * _SUB:u * CHUNK + (s + 1) * _SUB, :] * jnp.exp(jnp.minimum(b_s[rs_, :] - bj, 0.0))
                    dq3[s] = dq3[s] + tm_ * kj
                    gq = tm_ * q_s[rs_, :]
                    dk3_s[pl.ds(s * _SUB + jp, 1), :] = jnp.sum(gq, axis=0, keepdims=True)
                    w = gq * kj
                    dbn_s[pl.ds(s * _SUB + jp, 1), :] = jnp.sum(w, axis=0, keepdims=True)
                    db3[s] = db3[s] + w
            return tuple(dq3) + tuple(db3)

        zero = jnp.zeros((_SUB, _HK), F32)
        acc = lax.fori_loop(0, _SUB // _COLS, cols, (zero,) * (2 * _NSUB))
        dq3 = jnp.concatenate(acc[:_NSUB], axis=0)
        db3 = jnp.concatenate(acc[_NSUB:], axis=0)
        head = lax.broadcasted_iota(jnp.int32, (1, _HK), 1) // GLA_DK
        dq_x, db_x = [], []
        dk_x = jnp.zeros((CHUNK, _HK), F32)
        db_k = jnp.zeros((CHUNK, _HK), F32)
        for s in range(_NSUB):
            if s not in _gla_cross_blocks(reverse):
                dq_x.append(zero)
                db_x.append(zero)
                continue
            r0 = s * _SUB
            qt, eq, kt, ek, nmat = _gla_cross_terms(s, reverse, b_s, q_s, k_s, oseg_ref)
            seen = (lane >= r0 + _SUB) if reverse else (lane < r0)
            dax = jnp.where(seen, da_s[r0:r0 + _SUB, :], 0.0).astype(BF16)
            dqt = jnp.dot(dax, nmat, preferred_element_type=F32)
            full = lax.dot_general(dax, qt.astype(BF16), _TN, preferred_element_type=F32)
            dkt = full[0:CHUNK, :]
            for h in range(1, GLA_HEADS):
                dkt = jnp.where(head == h, full[h * CHUNK:(h + 1) * CHUNK, :], dkt)
            dq_x.append(dqt * eq)
            db_x.append(dqt * qt)
            dk_x = dk_x + dkt * ek
            db_k = db_k + dkt * kt
        dq = (dqb * eb + dq3 + jnp.concatenate(dq_x, axis=0)) * (GLA_DK ** -0.5)
        dk = dkd * ekd + dk3_s[...] + dk_x
        db = dqb * qb - dkd * kd + db3 - dbn_s[...] + jnp.concatenate(db_x, axis=0) - db_k
        db_ref[...] = jnp.where(rowi == last, db + dbl_row, db)
        if has_prev:
            dq = dq + pq_ref[...]
            dk = dk + pk_ref[...]
        dq_ref[...] = dq.astype(odt)
        dk_ref[...] = dk.astype(odt)
        dst[...] = dsp_s[...]

    def rowblk(e, c):
        return e * n + (c if reverse else (n - 1 - c))

    hk = pl.BlockSpec((CHUNK, _HK), lambda e, c: (rowblk(e, c), 0))
    hv = pl.BlockSpec((CHUNK, _HV), lambda e, c: (rowblk(e, c), 0))
    stb = pl.BlockSpec((1, _HK, GLA_DV), lambda e, c: (rowblk(e, c), 0, 0))
    in_specs = [pl.BlockSpec((CHUNK, _HK), lambda e, c: (rowblk(e, c), C_Q // _HK)),
                pl.BlockSpec((CHUNK, _HK), lambda e, c: (rowblk(e, c), C_K // _HK)),
                pl.BlockSpec((CHUNK, _HV), lambda e, c: (rowblk(e, c), C_V // _HV)),
                hk, stb, pl.BlockSpec((_HK, _HK), lambda e, c: (0, 0)), hk, stb, hv]
    args = [z, z, z, b_all, ecol, oseg, att, states, do]
    if has_prev:
        in_specs += [hk, hk, hv]
        args += list(prev)
    return pl.pallas_call(
        body, out_shape=(jax.ShapeDtypeStruct((T, _HK), odt), jax.ShapeDtypeStruct((T, _HK), odt),
                         jax.ShapeDtypeStruct((T, _HV), odt), jax.ShapeDtypeStruct((T, _HK), F32)),
        grid=(E, n), in_specs=in_specs, out_specs=(hk, hk, hv, hk),
        scratch_shapes=[pltpu.VMEM((_HK, GLA_DV), F32)] + [pltpu.VMEM((CHUNK, _HK), F32)] * 8 + [pltpu.VMEM((_HK, GLA_DV), F32)],
        name=name, compiler_params=_cp(("parallel", "arbitrary")))(*args)


def _gla_bwd_both(z, fwd_saved, rev_saved, do, S, name):
    T = z.shape[0]
    E = T // S
    n = S // CHUNK
    _, _, oseg = _gla_consts(False)
    NI, NO, NS = 8, 4, 10

    def body(*refs):
        oseg_ref = refs[2 * NI]
        dirs = []
        for di, reverse in enumerate((False, True)):
            ins = refs[NI * di:NI * (di + 1)]
            outs = refs[2 * NI + 1 + NO * di:2 * NI + 1 + NO * (di + 1)]
            scr = refs[2 * NI + 1 + 2 * NO + NS * di:2 * NI + 1 + 2 * NO + NS * (di + 1)]
            dirs.append((reverse, ins, outs, scr))

        @pl.when(pl.program_id(1) == 0)
        def _():
            for d in dirs:
                d[3][0][...] = jnp.zeros_like(d[3][0])

        lane = lax.broadcasted_iota(jnp.int32, (1, _HK), 1) % GLA_DK
        head = lax.broadcasted_iota(jnp.int32, (1, _HK), 1) // GLA_DK
        rowi = lax.broadcasted_iota(jnp.int32, (CHUNK, 1), 0)
        blk0 = (rowi // _SUB) * _SUB

        def factors(reverse, b_ref, b_s, q_s, k_s):
            last = 0 if reverse else CHUNK - 1
            b = b_s[...]
            eb = jnp.exp(b)
            ekd = jnp.exp(b_ref[last:last + 1, :] - b)
            return eb, ekd, q_s[...] * eb, k_s[...] * ekd

        dbl_rows = []
        for reverse, (q_ref, k_ref, v_ref, b_ref, ec_ref, att_ref, st_ref, do_ref), (dq_ref, dk_ref, dv_ref, db_ref), \
                (dst, b_s, q_s, k_s, da_s, dqb_s, dkd_s, dk3_s, dbn_s, dsp_s) in dirs:
            b_s[...] = b_ref[...]
            q_s[...] = q_ref[...] * (GLA_DK ** -0.5)
            k_s[...] = k_ref[...]
            _, _, qb, kd = factors(reverse, b_ref, b_s, q_s, k_s)
            v = v_ref[...]
            att = att_ref[...]
            s_all = st_ref[0]
            dsn = dst[...]
            e_col = ec_ref[0]
            do = do_ref[...]
            keep = (rowi <= lane) if reverse else (rowi >= lane)
            for h in range(GLA_HEADS):
                ks_ = slice(h * GLA_DK, (h + 1) * GLA_DK)
                vs_ = slice(h * GLA_DV, (h + 1) * GLA_DV)
                do_h = do[:, vs_]
                s_h = s_all[ks_, :]
                dsn_h = dsn[ks_, :]
                dqb_s[:, ks_] = _dotb(do_h, s_h, _NT)
                dsp_s[ks_, :] = _dotb(qb[:, ks_], do_h, _TN) + dsn_h * e_col[ks_, :]
                da_s[:, ks_] = _dotb(do_h, v[:, vs_], _NT)
                dv_ref[:, vs_] = _dotb(att[:, ks_], do_h, _TN) + _dotb(kd[:, ks_], dsn_h, _NN)
                dkd_s[:, ks_] = _dotb(v[:, vs_], dsn_h, _NT)
            da_s[...] = jnp.where(keep, da_s[...], 0.0)
            x = dsn * s_all * e_col
            dbl_rows.append(_dotf(jnp.ones((8, GLA_DV), F32), x, _NT)[0:1, :] + jnp.sum(dkd_s[...] * kd, axis=0, keepdims=True))

        def cols(d, jj, carry):
            _, b_s, q_s, k_s, da_s, _, _, dk3_s, dbn_s, _ = d[3]
            dq3, db3 = list(carry[:_NSUB]), list(carry[_NSUB:])
            sel = [jnp.where(lane == blk0 + (jj * _COLS + u), da_s[...], 0.0).astype(BF16) for u in range(_COLS)]
            dcols = jnp.dot(jnp.concatenate(sel, axis=0), oseg_ref[...], preferred_element_type=F32)
            for u in range(_COLS):
                jp = jj * _COLS + u
                for s in range(_NSUB):
                    rs_ = slice(s * _SUB, (s + 1) * _SUB)
                    bj = b_s[pl.ds(s * _SUB + jp, 1), :]
                    kj = k_s[pl.ds(s * _SUB + jp, 1), :]
                    tm_ = dcols[u * CHUNK + s * _SUB:u * CHUNK + (s + 1) * _SUB, :] * jnp.exp(jnp.minimum(b_s[rs_, :] - bj, 0.0))
                    dq3[s] = dq3[s] + tm_ * kj
                    gq = tm_ * q_s[rs_, :]
                    dk3_s[pl.ds(s * _SUB + jp, 1), :] = jnp.sum(gq, axis=0, keepdims=True)
                    w = gq * kj
                    dbn_s[pl.ds(s * _SUB + jp, 1), :] = jnp.sum(w, axis=0, keepdims=True)
                    db3[s] = db3[s] + w
            return tuple(dq3) + tuple(db3)

        zero = jnp.zeros((_SUB, _HK), F32)
        init = (zero,) * (2 * _NSUB)
        accs = lax.fori_loop(0, _SUB // _COLS, lambda jj, c: tuple(cols(d, jj, a) for d, a in zip(dirs, c)), (init, init),
                             unroll=True)

        for (reverse, ins, (dq_ref, dk_ref, dv_ref, db_ref), (dst, b_s, q_s, k_s, da_s, dqb_s, dkd_s, dk3_s, dbn_s, dsp_s)), \
                acc, dbl_row in zip(dirs, accs, dbl_rows):
            last = 0 if reverse else CHUNK - 1
            eb, ekd, qb, kd = factors(reverse, ins[3], b_s, q_s, k_s)
            dqb = dqb_s[...]
            dkd = dkd_s[...]
            dq3 = jnp.concatenate(acc[:_NSUB], axis=0)
            db3 = jnp.concatenate(acc[_NSUB:], axis=0)
            dq_x, db_x = [], []
            dk_x = jnp.zeros((CHUNK, _HK), F32)
            db_k = jnp.zeros((CHUNK, _HK), F32)
            for s in range(_NSUB):
                if s not in _gla_cross_blocks(reverse):
                    dq_x.append(zero)
                    db_x.append(zero)
                    continue
                r0 = s * _SUB
                qt, eq, kt, ek, nmat = _gla_cross_terms(s, reverse, b_s, q_s, k_s, oseg_ref)
                seen = (lane >= r0 + _SUB) if reverse else (lane < r0)
                dax = jnp.where(seen, da_s[r0:r0 + _SUB, :], 0.0).astype(BF16)
                dqt = jnp.dot(dax, nmat, preferred_element_type=F32)
                full = lax.dot_general(dax, qt.astype(BF16), _TN, preferred_element_type=F32)
                dkt = full[0:CHUNK, :]
                for h in range(1, GLA_HEADS):
                    dkt = jnp.where(head == h, full[h * CHUNK:(h + 1) * CHUNK, :], dkt)
                dq_x.append(dqt * eq)
                db_x.append(dqt * qt)
                dk_x = dk_x + dkt * ek
                db_k = db_k + dkt * kt
            dq_ref[...] = (dqb * eb + dq3 + jnp.concatenate(dq_x, axis=0)) * (GLA_DK ** -0.5)
            dk_ref[...] = dkd * ekd + dk3_s[...] + dk_x
            db = dqb * qb - dkd * kd + db3 - dbn_s[...] + jnp.concatenate(db_x, axis=0) - db_k
            db_ref[...] = jnp.where(rowi == last, db + dbl_row, db)
            dst[...] = dsp_s[...]

    def specs(reverse):
        rb = (lambda e, c: e * n + c) if reverse else (lambda e, c: e * n + (n - 1 - c))
        hk = pl.BlockSpec((CHUNK, _HK), lambda e, c: (rb(e, c), 0))
        hv = pl.BlockSpec((CHUNK, _HV), lambda e, c: (rb(e, c), 0))
        stb = pl.BlockSpec((1, _HK, GLA_DV), lambda e, c: (rb(e, c), 0, 0))
        ins = [pl.BlockSpec((CHUNK, _HK), lambda e, c: (rb(e, c), C_Q // _HK)),
               pl.BlockSpec((CHUNK, _HK), lambda e, c: (rb(e, c), C_K // _HK)),
               pl.BlockSpec((CHUNK, _HV), lambda e, c: (rb(e, c), C_V // _HV)),
               hk, stb, hk, stb, hv]
        return ins, [hk, hk, hv, hk]

    in_f, out_f = specs(False)
    in_b, out_b = specs(True)
    shapes = (jax.ShapeDtypeStruct((T, _HK), F32), jax.ShapeDtypeStruct((T, _HK), F32),
              jax.ShapeDtypeStruct((T, _HV), F32), jax.ShapeDtypeStruct((T, _HK), F32))
    scratch = [pltpu.VMEM((_HK, GLA_DV), F32)] + [pltpu.VMEM((CHUNK, _HK), F32)] * 8 + [pltpu.VMEM((_HK, GLA_DV), F32)]
    res = pl.pallas_call(
        body, out_shape=shapes + shapes, grid=(E, n),
        in_specs=in_f + in_b + [pl.BlockSpec((_HK, _HK), lambda e, c: (0, 0))], out_specs=tuple(out_f + out_b),
        scratch_shapes=scratch * 2, name=name, compiler_params=_cp(("parallel", "arbitrary")))(
            z, z, z, *fwd_saved, do, z, z, z, *rev_saved, do, oseg)
    return res[:4], res[4:]


def _gla_norm_fwd(of, ob, og, name):
    T = of.shape[0]
    tm = 256

    def body(f_ref, b_ref, g_ref, o_ref):
        for h in range(GLA_HEADS):
            vs_ = slice(h * GLA_DV, (h + 1) * GLA_DV)
            o = f_ref[:, vs_] + b_ref[:, vs_]
            o_ref[:, vs_] = o * lax.rsqrt(jnp.mean(o * o, axis=-1, keepdims=True) + EPS) * g_ref[:, vs_]

    row = pl.BlockSpec((tm, _HV), lambda i: (i, 0))
    vec = pl.BlockSpec((1, _HV), lambda i: (0, 0))
    return pl.pallas_call(body, out_shape=jax.ShapeDtypeStruct((T, _HV), F32), grid=(T // tm,),
                          in_specs=[row, row, vec], out_specs=row, name=name, compiler_params=_cp(("parallel",)))(of, ob, og)


def _gla_norm_bwd(of, ob, og, dpre, name):
    T = of.shape[0]
    tm = 256

    def body(f_ref, b_ref, g_ref, dp_ref, do_ref, dg_ref):
        @pl.when(pl.program_id(0) == 0)
        def _():
            dg_ref[...] = jnp.zeros_like(dg_ref)

        for h in range(GLA_HEADS):
            vs_ = slice(h * GLA_DV, (h + 1) * GLA_DV)
            o = f_ref[:, vs_] + b_ref[:, vs_]
            r = lax.rsqrt(jnp.mean(o * o, axis=-1, keepdims=True) + EPS)
            xh = o * r
            dp = dp_ref[:, vs_]
            dxh = dp * g_ref[:, vs_]
            do_ref[:, vs_] = r * (dxh - xh * jnp.mean(dxh * xh, axis=-1, keepdims=True))
            dg_ref[:, vs_] += jnp.sum(dp * xh, axis=0, keepdims=True)

    row = pl.BlockSpec((tm, _HV), lambda i: (i, 0))
    vec = pl.BlockSpec((1, _HV), lambda i: (0, 0))
    return pl.pallas_call(
        body, out_shape=(jax.ShapeDtypeStruct((T, _HV), F32), jax.ShapeDtypeStruct((1, _HV), F32)), grid=(T // tm,),
        in_specs=[row, row, vec, row], out_specs=(row, vec), name=name, compiler_params=_cp(("arbitrary",)))(of, ob, og, dpre)


_ANY = pl.BlockSpec(memory_space=pl.ANY)


def _coords():
    return lax.axis_index("x"), lax.axis_index("y"), lax.axis_index("c")


def _other_chips(x, y):
    return ((1 - x, y), (x, 1 - y), (1 - x, 1 - y))


def _gather_weights(arrays, chunks, name):
    n = len(arrays)
    pieces = []
    for k in range(max(chunks)):
        for i, a in enumerate(arrays):
            if k < chunks[i]:
                rc = a.shape[1] // chunks[i]
                pieces.append((i, k * rc, rc))
    m = len(pieces)

    def body(*refs):
        srcs, dsts = refs[:n], refs[n:2 * n]
        send_sems, recv_sems, local_sems = refs[2 * n:]
        x, y, c = _coords()
        me = 2 * x + y
        loc = [pltpu.make_async_copy(s, d.at[me], local_sems.at[i]) for i, (s, d) in enumerate(zip(srcs, dsts))]
        for cp in loc:
            cp.start()
        ici = []
        for p, (i, r0, rc) in enumerate(pieces):
            for j, (px, py) in enumerate(_other_chips(x, y)):
                ici.append(pltpu.make_async_remote_copy(
                    src_ref=srcs[i].at[c, pl.ds(r0, rc)], dst_ref=dsts[i].at[me, c, pl.ds(r0, rc)],
                    send_sem=send_sems.at[3 * p + j], recv_sem=recv_sems.at[3 * p + j],
                    device_id=(px, py, c), device_id_type=MESH))
        for cp in ici:
            cp.start()
        fwd = []
        for p, (i, r0, rc) in enumerate(pieces):
            for j, (px, py) in enumerate(_other_chips(x, y)):
                ici[3 * p + j].wait_recv()
                part = dsts[i].at[2 * px + py, c, pl.ds(r0, rc)]
                cp = pltpu.make_async_remote_copy(
                    src_ref=part, dst_ref=part, send_sem=send_sems.at[3 * m + 3 * p + j], recv_sem=recv_sems.at[3 * m + 3 * p + j],
                    device_id=(x, y, 1 - c), device_id_type=MESH)
                cp.start()
                fwd.append(cp)
        for cp in fwd:
            cp.wait_recv()
        for cp in ici + fwd:
            cp.wait_send()
        for cp in loc:
            cp.wait()

    return pl.pallas_call(
        body, out_shape=tuple(jax.ShapeDtypeStruct((4,) + a.shape, a.dtype) for a in arrays),
        in_specs=[_ANY] * n, out_specs=(_ANY,) * n,
        scratch_shapes=[pltpu.SemaphoreType.DMA((6 * m,)), pltpu.SemaphoreType.DMA((6 * m,)), pltpu.SemaphoreType.DMA((n,))],
        name=name)(*arrays)


def _sibling_exchange(layered, whole, name):
    nl, n = len(layered), len(layered) + len(whole)

    def body(*refs):
        srcs, dsts = refs[:n], refs[n:2 * n]
        send_sems, recv_sems = refs[2 * n:]
        x, y, c = _coords()
        rem = [pltpu.make_async_remote_copy(src_ref=(s.at[1 - c] if i < nl else s), dst_ref=d, send_sem=send_sems.at[i],
                                            recv_sem=recv_sems.at[i], device_id=(x, y, 1 - c), device_id_type=MESH)
               for i, (s, d) in enumerate(zip(srcs, dsts))]
        for cp in rem:
            cp.start()
        for cp in rem:
            cp.wait()

    outs = [jax.ShapeDtypeStruct(a.shape[1:], a.dtype) for a in layered] + [jax.ShapeDtypeStruct(a.shape, a.dtype) for a in whole]
    return pl.pallas_call(
        body, out_shape=tuple(outs), in_specs=[_ANY] * n, out_specs=(_ANY,) * n,
        scratch_shapes=[pltpu.SemaphoreType.DMA((n,)), pltpu.SemaphoreType.DMA((n,))], name=name)(*layered, *whole)


def _chip_exchange(scatter, bcast, name):
    ns, n = len(scatter), len(scatter) + len(bcast)

    def body(*refs):
        srcs, dsts = refs[:n], refs[n:2 * n]
        send_sems, recv_sems, local_sems = refs[2 * n:]
        x, y, c = _coords()
        me = 2 * x + y
        loc = [pltpu.make_async_copy((s.at[me] if i < ns else s), d.at[me], local_sems.at[i])
               for i, (s, d) in enumerate(zip(srcs, dsts))]
        for cp in loc:
            cp.start()
        rem = []
        for j, (px, py) in enumerate(_other_chips(x, y)):
            for i, (s, d) in enumerate(zip(srcs, dsts)):
                rem.append(pltpu.make_async_remote_copy(
                    src_ref=(s.at[2 * px + py] if i < ns else s), dst_ref=d.at[me], send_sem=send_sems.at[n * j + i],
                    recv_sem=recv_sems.at[n * j + i], device_id=(px, py, c), device_id_type=MESH))
        for cp in rem:
            cp.start()
        for cp in rem:
            cp.wait()
        for cp in loc:
            cp.wait()

    outs = [jax.ShapeDtypeStruct(a.shape, a.dtype) for a in scatter] + [jax.ShapeDtypeStruct((4,) + a.shape, a.dtype) for a in bcast]
    return pl.pallas_call(
        body, out_shape=tuple(outs), in_specs=[_ANY] * n, out_specs=(_ANY,) * n,
        scratch_shapes=[pltpu.SemaphoreType.DMA((3 * n,)), pltpu.SemaphoreType.DMA((3 * n,)), pltpu.SemaphoreType.DMA((n,))],
        name=name)(*scatter, *bcast)


_EW_BLOCK_BYTES = 2 * 1024 * 1024


def _tile2d(R, C):
    if R % 256 == 0 and 256 * C * 4 <= _EW_BLOCK_BYTES:
        return 256, C
    bc = 256 if C % 256 == 0 else C
    for br in range(R, 0, -1):
        if R % br == 0 and (br % 8 == 0 or br == R) and br * bc * 4 <= _EW_BLOCK_BYTES:
            return br, bc
    return R, bc


def _sum_slots(r, name):
    n, R, C = r.shape
    br, bc = _tile2d(R, C)

    def body(r_ref, o_ref):
        acc = r_ref[0].astype(F32)
        for i in range(1, n):
            acc = acc + r_ref[i].astype(F32)
        o_ref[...] = acc

    return pl.pallas_call(body, out_shape=jax.ShapeDtypeStruct((R, C), F32), grid=(R // br, C // bc),
                          in_specs=[pl.BlockSpec((n, br, bc), lambda i, j: (0, i, j))],
                          out_specs=pl.BlockSpec((br, bc), lambda i, j: (i, j)),
                          name=name, compiler_params=_cp(("parallel", "parallel")))(r)


_SMEM = pl.BlockSpec(memory_space=pltpu.SMEM)


def _add2(a, b, out_dtype, name, pick=None):
    R, C = b.shape
    br, bc = _tile2d(R, C)
    blk = pl.BlockSpec((br, bc), lambda i, j: (i, j))
    if pick is None:
        def body(a_ref, b_ref, o_ref):
            o_ref[...] = (a_ref[...].astype(F32) + b_ref[...].astype(F32)).astype(out_dtype)
        in_specs, args = [blk, blk], (a, b)
    else:
        def body(c_ref, a_ref, b_ref, o_ref):
            av = jnp.where(c_ref[0] == 0, a_ref[0], a_ref[1])
            o_ref[...] = (av.astype(F32) + b_ref[...].astype(F32)).astype(out_dtype)
        in_specs = [_SMEM, pl.BlockSpec((2, br, bc), lambda i, j: (0, i, j)), blk]
        args = (pick.reshape(1).astype(jnp.int32), a, b)
    return pl.pallas_call(body, out_shape=jax.ShapeDtypeStruct((R, C), out_dtype), grid=(R // br, C // bc), in_specs=in_specs,
                          out_specs=blk, name=name, compiler_params=_cp(("parallel", "parallel")))(*args)


def _adamw_math(w, g, m, v):
    m = ADAM_B1 * m + (1.0 - ADAM_B1) * g
    v = ADAM_B2 * v + (1.0 - ADAM_B2) * (g * g)
    m_hat = m / (1.0 - ADAM_B1 ** ADAM_STEP)
    v_hat = v / (1.0 - ADAM_B2 ** ADAM_STEP)
    delta = -ADAM_LR * (m_hat / (jnp.sqrt(v_hat) + ADAM_EPS) + ADAM_WD * w)
    return delta, m, v


def _adamw(w, gs, m, v, name, pick=None):
    R, C = w.shape
    br, bc = _tile2d(R, C)
    blk = pl.BlockSpec((br, bc), lambda i, j: (i, j))
    if pick is None:
        g_specs = [pl.BlockSpec((g.shape[0], br, bc), lambda i, j: (0, i, j)) if g.ndim == 3 else blk for g in gs]
        lead = ()
    else:
        nb = (R // 2) // br
        assert nb * br * 2 == R
        g_specs = [pl.BlockSpec((br, bc), lambda i, j: (i % nb, j))] * 2
        lead = (pick.reshape(1).astype(jnp.int32),)

    def body(*refs):
        if pick is not None:
            c_ref, refs = refs[0], refs[1:]
        w_ref = refs[0]
        g_refs = refs[1:1 + len(gs)]
        m_ref, v_ref, g_out, d_out, m_out, v_out = refs[1 + len(gs):]
        if pick is None:
            g = None
            for gr in g_refs:
                parts = [gr[i] for i in range(gr.shape[0])] if len(gr.shape) == 3 else [gr[...]]
                for p in parts:
                    g = p if g is None else g + p
        else:
            g = jnp.where(pl.program_id(0) // nb == c_ref[0], g_refs[0][...], g_refs[1][...])
        d, mn, vn = _adamw_math(w_ref[...], g, m_ref[...], v_ref[...])
        g_out[...] = g
        d_out[...] = d
        m_out[...] = mn
        v_out[...] = vn

    return pl.pallas_call(
        body, out_shape=tuple(jax.ShapeDtypeStruct((R, C), F32) for _ in range(4)), grid=(R // br, C // bc),
        in_specs=[_SMEM] * len(lead) + [blk] + g_specs + [blk, blk], out_specs=(blk,) * 4, name=name,
        compiler_params=_cp(("parallel", "parallel")))(*lead, w, *gs, m, v)


WEIGHTS = ("norm_g", "w_in", "conv_w", "conv_b", "conv_ln_g", "conv_ln_b", "na_q_g", "na_k_g", "na_rpb", "gla_a2_f",
           "gla_ab_f", "gla_a2_b", "gla_ab_b", "gla_o_g", "pool_w", "pool_scale", "w_out")
_REPL = ("norm_g", "conv_b", "conv_ln_g", "conv_ln_b", "na_q_g", "na_k_g", "na_rpb", "gla_ab_f", "gla_ab_b", "gla_o_g",
         "pool_w", "pool_scale")
_SHARD_SMALL = ("conv_w", "gla_a2_f", "gla_a2_b")
_PACK_ROWS = 8 * 128


def _pack(arrs):
    flat = jnp.concatenate([a.reshape(-1) for a in arrs])
    n = -(-flat.shape[0] // _PACK_ROWS) * _PACK_ROWS
    return jnp.pad(flat, (0, n - flat.shape[0])).reshape(-1, 128)


def _unpack(p, shapes):
    flat = p.reshape(-1)
    out, o = [], 0
    for s in shapes:
        n = int(np.prod(s))
        out.append(flat[o:o + n].reshape(s))
        o += n
    return out


def _to_layout_rows(w):
    pad = jnp.zeros((w.shape[0], NZ - N_IN, w.shape[2]), w.dtype)
    return jnp.concatenate([w[:, :5120], w[:, 5152:6176], w[:, 5120:5152], pad], axis=1)


def _from_layout_rows(w):
    return jnp.concatenate([w[:, :5120], w[:, LR_OFF:LR_OFF + 32], w[:, 5120:LR_OFF]], axis=1)


def _reduce_gradients(p_a, p_b, small_g, ci):
    s_a, s_b, s_small = _sibling_exchange((p_a, p_b), (small_g,), "grad_to_sibling")
    flat = lambda a: a.reshape(a.shape[0], -1, a.shape[-1])
    c_a = _add2(flat(p_a), s_a.reshape(-1, s_a.shape[-1]), BF16, "chip_sum_a", pick=ci).reshape(s_a.shape)
    c_b = _add2(flat(p_b), s_b.reshape(-1, s_b.shape[-1]), BF16, "chip_sum_b", pick=ci).reshape(s_b.shape)
    c_small = _add2(small_g, s_small, F32, "chip_sum_small")
    r_a, r_b, r_small = _chip_exchange((c_a, c_b), (c_small,), "grad_to_owner")
    own_a = _sum_slots(r_a, "sum_a")
    own_b = _sum_slots(r_b, "sum_b")
    sib_a, sib_b = _sibling_exchange((), (own_a, own_b), "reduced_to_sibling")
    return (own_a, sib_a), (own_b, sib_b), r_small


def _layer_fwd(l, x, P, S, target=None):
    n = f"l{l}_"
    h = _rmsnorm_fwd(x, P["norm_g"], n + "rms_fwd")
    z = _matmul(h, P["w_in"], dims="nn", out_dtype=F32, tm=1024, tn=1280, tk=D_MODEL, name=n + "mm_z")
    yc = _conv_fwd(z, P["conv_w32"], P["conv_b"], S, n + "conv_fwd")
    pre_a = _ln_silu_fwd(yc, P["conv_ln_g"], P["conv_ln_b"], n + "ln_fwd")
    pre_b = _na_fwd(z, P["na_q_g"], P["na_k_g"], P["na_bias"], S, n + "na_fwd")
    bf, ecf, bb, ecb = _gla_decay_fwd_both(z, P["a2_f"], P["gla_ab_f"], P["a2_b"], P["gla_ab_b"], n + "gla_decay")
    (of, af, sf), (ob, ab, sb) = _gla_fwd_both(z, bf, ecf, bb, ecb, S, n + "gla_fwd")
    pre_c = _gla_norm_fwd(of, ob, P["gla_o_g"], n + "gla_norm_fwd")
    pre_d = _pool_fwd(z, P["pool_w_bf"], P["pool_scale"], S, n + "pool_fwd")
    pres = (pre_a, pre_b, pre_c, pre_d)
    res = _out_proj_fwd(pres, z, P["w_out"], x, target, n + "out_proj")
    y, out = res[0], (res[1] if target is None else res[1:])
    return out, dict(x=x, h=h, z=z, yc=yc, pres=pres, of=of, af=af, sf=sf, ob=ob, ab=ab, sb=sb, y=y, bf=bf, ecf=ecf, bb=bb, ecb=ecb)


def _layer_bwd(l, dout, dout_bf, sv, P, S):
    n = f"l{l}_"
    z = sv["z"]
    T = z.shape[0]
    d_w_out = _matmul(sv["y"], dout_bf, dims="tn", out_dtype=BF16, tm=1024, tn=2048, tk=512, name=n + "mm_dwout")
    dpa, dpb, dpc, dpd, dga, dgb, dgc, dgd = _out_proj_bwd(dout_bf, P["w_out_t"], sv["pres"], z, n + "out_proj_bwd")
    dyc, d_ln_g, d_ln_b = _ln_silu_bwd(sv["yc"], P["conv_ln_g"], P["conv_ln_b"], dpa, n + "ln_bwd")
    dval, dglu, d_cw, d_cb = _conv_bwd(z, P["conv_w32"], dyc, S, n + "conv_bwd")
    dq, dk, dv, dbias, d_qg, d_kg = _na_bwd(z, P["na_q_g"], P["na_k_g"], P["na_bias"], dpb, S, n + "na_bwd")
    d_rpb = _na_rpb_grad(dbias, n + "na_rpb")
    do, d_og = _gla_norm_bwd(sv["of"], sv["ob"], P["gla_o_g"], dpc, n + "gla_norm_bwd")
    gf, gb_ = _gla_bwd_both(z, (sv["bf"], sv["ecf"], sv["af"], sv["sf"]), (sv["bb"], sv["ecb"], sv["ab"], sv["sb"]), do, S,
                            n + "gla_bwd")
    dcq, dck, dcv = zip(gf[:3], gb_[:3])
    dlr, d_a2f, d_abf, d_a2b, d_abb = _gla_decay_bwd(z, P["a2_f"], P["gla_ab_f"], P["a2_b"], P["gla_ab_b"], gf[3], gb_[3],
                                                     n + "gla_decay_bwd")
    dd, d_pw, d_ps = _pool_bwd(z, P["pool_w_bf"], P["pool_scale"], dpd, S, n + "pool_bwd")
    dz = _concat_cols([dval, dglu, dga, dq, dk, dv, dgb, dcq, dck, dcv, dgc, dd, dgd, dlr], NZ, n + "dz_concat")
    dh = _matmul(dz, P["w_in_t"], dims="nn", out_dtype=F32, tm=1024, tn=1024, tk=3200, name=n + "mm_dh")
    d_w_in = _matmul(dz, sv["h"], dims="tn", out_dtype=BF16, tm=1280, tn=1024, tk=1024, name=n + "mm_dwin")
    dx, dx_bf, d_ng = _rmsnorm_bwd(sv["x"], P["norm_g"], dh, dout, n + "rms_bwd")
    grads = dict(norm_g=d_ng[0], w_in=d_w_in, conv_w=d_cw[:CONV_K], conv_b=d_cb[0], conv_ln_g=d_ln_g[0], conv_ln_b=d_ln_b[0],
                 na_q_g=d_qg.reshape(NA_HEADS, NA_DH), na_k_g=d_kg.reshape(NA_HEADS, NA_DH), na_rpb=d_rpb,
                 gla_a2_f=d_a2f[0:GLA_RANK], gla_ab_f=d_abf[0], gla_a2_b=d_a2b[GLA_RANK:2 * GLA_RANK], gla_ab_b=d_abb[0],
                 gla_o_g=d_og.reshape(GLA_HEADS, GLA_DV), pool_w=d_pw, pool_scale=d_ps[0], w_out=d_w_out)
    return dx, dx_bf, grads


def kernel(x, norm_g, w_in, conv_w, conv_b, conv_ln_g, conv_ln_b, na_q_g, na_k_g, na_rpb, gla_a2_f, gla_ab_f, gla_a2_b, gla_ab_b, gla_o_g, pool_w, pool_scale, w_out, loss_target, m_norm_g, m_w_in, m_conv_w, m_conv_b, m_conv_ln_g, m_conv_ln_b, m_na_q_g, m_na_k_g, m_na_rpb, m_gla_a2_f, m_gla_ab_f, m_gla_a2_b, m_gla_ab_b, m_gla_o_g, m_pool_w, m_pool_scale, m_w_out, v_norm_g, v_w_in, v_conv_w, v_conv_b, v_conv_ln_g, v_conv_ln_b, v_na_q_g, v_na_k_g, v_na_rpb, v_gla_a2_f, v_gla_ab_f, v_gla_a2_b, v_gla_ab_b, v_gla_o_g, v_pool_w, v_pool_scale, v_w_out):
    W = dict(norm_g=norm_g, w_in=w_in, conv_w=conv_w, conv_b=conv_b, conv_ln_g=conv_ln_g, conv_ln_b=conv_ln_b, na_q_g=na_q_g,
             na_k_g=na_k_g, na_rpb=na_rpb, gla_a2_f=gla_a2_f, gla_ab_f=gla_ab_f, gla_a2_b=gla_a2_b, gla_ab_b=gla_ab_b,
             gla_o_g=gla_o_g, pool_w=pool_w, pool_scale=pool_scale, w_out=w_out)
    M = dict(norm_g=m_norm_g, w_in=m_w_in, conv_w=m_conv_w, conv_b=m_conv_b, conv_ln_g=m_conv_ln_g, conv_ln_b=m_conv_ln_b,
             na_q_g=m_na_q_g, na_k_g=m_na_k_g, na_rpb=m_na_rpb, gla_a2_f=m_gla_a2_f, gla_ab_f=m_gla_ab_f, gla_a2_b=m_gla_a2_b,
             gla_ab_b=m_gla_ab_b, gla_o_g=m_gla_o_g, pool_w=m_pool_w, pool_scale=m_pool_scale, w_out=m_w_out)
    V = dict(norm_g=v_norm_g, w_in=v_w_in, conv_w=v_conv_w, conv_b=v_conv_b, conv_ln_g=v_conv_ln_g, conv_ln_b=v_conv_ln_b,
             na_q_g=v_na_q_g, na_k_g=v_na_k_g, na_rpb=v_na_rpb, gla_a2_f=v_gla_a2_f, gla_ab_f=v_gla_ab_f, gla_a2_b=v_gla_a2_b,
             gla_ab_b=v_gla_ab_b, gla_o_g=v_gla_o_g, pool_w=v_pool_w, pool_scale=v_pool_scale, w_out=v_w_out)
    E, S, D = x.shape
    T = E * S
    L = DEPTH
    xi, yi, ci = _coords()
    chip = 2 * xi + yi
    cw_sh, a2_sh = conv_w.shape[-1], gla_a2_f.shape[-1]

    small_sh = jnp.concatenate([
        jnp.pad(conv_w, ((0, 0), (0, 1), (0, 0))),
        jnp.pad(gla_a2_f, ((0, 0), (0, 0), (0, 128 - a2_sh))),
        jnp.pad(gla_a2_b, ((0, 0), (0, 0), (0, 128 - a2_sh)))], axis=1)
    w_in_tr, m_w_in_tr, v_w_in_tr = (jnp.transpose(a, (0, 2, 1)) for a in (w_in, m_w_in, v_w_in))
    g_win, g_wout, g_small = _gather_weights((w_in_tr.astype(BF16), w_out.astype(BF16), small_sh), (1, 2, 1), "gather_weights")
    w_in_t_full = _to_layout_rows(jnp.transpose(g_win, (1, 0, 2, 3)).reshape(L, N_IN, D))
    w_in_full = jnp.transpose(w_in_t_full, (0, 2, 1))
    w_out_full = jnp.transpose(g_wout, (1, 0, 2, 3)).reshape(L, D, D)
    conv_w_full = jnp.transpose(g_small[:, :, 0:32, :], (1, 2, 0, 3)).reshape(L, 32, 4 * cw_sh)
    a2f_full = jnp.transpose(g_small[:, :, 32:48, :a2_sh], (1, 2, 0, 3)).reshape(L, GLA_RANK, 4 * a2_sh)
    a2b_full = jnp.transpose(g_small[:, :, 48:64, :a2_sh], (1, 2, 0, 3)).reshape(L, GLA_RANK, 4 * a2_sh)

    params = []
    for l in range(L):
        params.append(dict(
            norm_g=norm_g[l][None], w_in=w_in_full[l], w_out=w_out_full[l], w_in_t=w_in_t_full[l], w_out_t=w_out_full[l].T,
            conv_w32=conv_w_full[l], conv_b=conv_b[l][None],
            conv_ln_g=conv_ln_g[l][None], conv_ln_b=conv_ln_b[l][None], na_q_g=na_q_g[l].reshape(1, GROUP_W),
            na_k_g=na_k_g[l].reshape(1, GROUP_W), na_bias=_na_bias(na_rpb[l], f"l{l}_na_bias"),
            a2_f=jnp.zeros((128, _HK), F32).at[0:GLA_RANK].set(a2f_full[l]),
            a2_b=jnp.zeros((128, _HK), F32).at[GLA_RANK:2 * GLA_RANK].set(a2b_full[l]),
            gla_ab_f=gla_ab_f[l][None], gla_ab_b=gla_ab_b[l][None], gla_o_g=gla_o_g[l].reshape(1, GROUP_W),
            pool_w_bf=pool_w[l].astype(BF16), pool_scale=pool_scale[l][None]))

    act = x.reshape(T, D)
    saved = []
    for l in range(L):
        act, sv = _layer_fwd(l, act, params[l], S, loss_target.reshape(T, D) if l == L - 1 else None)
        saved.append(sv)
    dact, dact_bf, loss_loc = act
    loss = lax.psum(loss_loc[0, 0], ("x", "y", "c"))
    grads = [None] * L
    for l in reversed(range(L)):
        dact, dact_bf, grads[l] = _layer_bwd(l, dact, dact_bf, saved[l], params[l], S)
    grad_x = dact.reshape(E, S, D)
    G = {k: jnp.stack([grads[l][k] for l in range(L)]) for k in WEIGHTS}

    cols_in, cols_out = N_IN // 4, D
    p_win = _from_layout_rows(G["w_in"]).reshape(L, 4, cols_in, D)
    p_wout = G["w_out"].reshape(L, 4, D // 4, D)
    small_names = _REPL + _SHARD_SMALL
    small_g = _pack([G[k] for k in small_names])
    g_in, g_out, r_small = _reduce_gradients(p_win, p_wout, small_g, ci)

    rows_in, rows_out = L * cols_in, L * (D // 4)
    res = {}
    res["w_in"] = [jnp.transpose(a.reshape(L, cols_in, D), (0, 2, 1)) for a in _adamw(
        w_in_tr.reshape(rows_in, D), g_in, m_w_in_tr.reshape(rows_in, D), v_w_in_tr.reshape(rows_in, D), "adamw_w_in", pick=ci)]
    res["w_out"] = [a.reshape(L, D // 4, D) for a in _adamw(
        w_out.reshape(rows_out, cols_out), g_out, m_w_out.reshape(rows_out, cols_out),
        v_w_out.reshape(rows_out, cols_out), "adamw_w_out", pick=ci)]
    zeros_sh = [jnp.zeros(G[k].shape, F32) for k in _SHARD_SMALL]
    pk = lambda dct: _pack([dct[k] for k in _REPL] + zeros_sh)
    small_res = _adamw(pk(W), (r_small,), pk(M), pk(V), "adamw_small")
    shapes = [G[k].shape for k in small_names]
    unp = [_unpack(a, shapes) for a in small_res]
    for i, k in enumerate(_REPL):
        res[k] = [u[i] for u in unp]
    g_sh = []
    for i, k in enumerate(_SHARD_SMALL):
        gfull = unp[0][len(_REPL) + i]
        wdt = W[k].shape[-1]
        g_sh.append(lax.dynamic_slice_in_dim(gfull, chip * wdt, wdt, axis=2))
    g_sh_p = _pack(g_sh)
    sh_res = _adamw(_pack([W[k] for k in _SHARD_SMALL]), (g_sh_p,), _pack([M[k] for k in _SHARD_SMALL]),
                    _pack([V[k] for k in _SHARD_SMALL]), "adamw_shard_small")
    shapes2 = [W[k].shape for k in _SHARD_SMALL]
    unp2 = [_unpack(a, shapes2) for a in sh_res]
    for i, k in enumerate(_SHARD_SMALL):
        res[k] = [u[i] for u in unp2]

    outs = [loss, grad_x]
    for j in range(4):
        outs += [res[k][j] for k in WEIGHTS]
    return tuple(outs)
```
